```python
import jax, jax.numpy as jnp
from jax import lax
import numpy as np

D_MODEL = 1024
BATCH = 8
SEQ = 2048
DEPTH = 1
DEC_BATCH = 128
DEC_SEQ = 8
PAST_LEN = 8192
PAGE_SIZE = 128

MIX_WIDTH = D_MODEL
LRU_WIDTH = MIX_WIDTH // 2
LRU_BLOCKS = 8
LRU_BLOCK_W = LRU_WIDTH // LRU_BLOCKS
CONV_W = 4
LRU_C = 8.0
HEAD_DIM = 64
N_HEADS = (MIX_WIDTH - LRU_WIDTH) // HEAD_DIM
N_KV_HEADS = 2
GROUP = N_HEADS // N_KV_HEADS
WINDOW = 128
ROPE_THETA = 10000.0
N_EXPERTS = 32
TOP_K = 4
D_FF = D_MODEL
SWIGLU_LIMIT = 7.0
SWIGLU_ALPHA = 1.702
MOE_BLOCK = 128
NORM_EPS = 1e-5
IN_WIDTH = 2 * LRU_WIDTH + (N_HEADS + 2 * N_KV_HEADS) * HEAD_DIM

kernel_name = 'hymba_rglru_swa_sink_moe_step'


def rms_norm(x, g):
    xf = x.astype(jnp.float32)
    y = xf * lax.rsqrt(jnp.mean(xf * xf, axis=-1, keepdims=True) + NORM_EPS) * g.astype(jnp.float32)
    return y.astype(x.dtype)


def rope(x, pos):
    half = HEAD_DIM // 2
    inv = ROPE_THETA ** (-jnp.arange(half, dtype=jnp.float32) / half)
    ang = pos.astype(jnp.float32)[:, None] * inv[None, :]
    cos = jnp.cos(ang)[:, None, :]
    sin = jnp.sin(ang)[:, None, :]
    xf = x.astype(jnp.float32)
    x1, x2 = xf[..., :half], xf[..., half:]
    return jnp.concatenate([x1 * cos - x2 * sin, x2 * cos + x1 * sin], axis=-1).astype(x.dtype)


def attend(q, k, v, mask, sinks):
    s = jnp.einsum('...qkgd,...skd->...kgqs', q.astype(jnp.float32), k.astype(jnp.float32)) * (HEAD_DIM ** -0.5)
    s = jnp.where(mask, s, -jnp.inf)
    sink = jnp.broadcast_to(sinks.astype(jnp.float32)[:, :, None, None], s.shape[:-1] + (1,))
    p = jax.nn.softmax(jnp.concatenate([s, sink], axis=-1), axis=-1)[..., :-1]
    return jnp.einsum('...kgqs,...skd->...qkgd', p, v.astype(jnp.float32))


def prompt_attention(q, k, v, sinks):
    B, S = q.shape[0], q.shape[1]
    nb = S // WINDOW
    qb = q.reshape(B, nb, WINDOW, N_KV_HEADS, GROUP, HEAD_DIM)
    kb = k.reshape(B, nb, WINDOW, N_KV_HEADS, HEAD_DIM)
    vb = v.reshape(B, nb, WINDOW, N_KV_HEADS, HEAD_DIM)
    prev = lambda t: jnp.concatenate([jnp.zeros_like(t[:, :1]), t[:, :-1]], axis=1)
    kk = jnp.concatenate([prev(kb), kb], axis=2)
    vv = jnp.concatenate([prev(vb), vb], axis=2)
    qi = jnp.arange(WINDOW)
    kj = jnp.arange(2 * WINDOW)
    dist = qi[:, None] + WINDOW - kj[None, :]
    kpos = jnp.arange(nb)[:, None] * WINDOW - WINDOW + kj[None, :]
    mask = ((dist >= 0) & (dist < WINDOW))[None] & (kpos >= 0)[:, None, :]
    o = attend(qb, kk, vv, mask[:, None, None], sinks)
    return o.reshape(B, S, N_HEADS * HEAD_DIM), k[:, -WINDOW:], v[:, -WINDOW:]


def sample_attention(q, k, v, buf_k, buf_v, sinks):
    Bd, T = q.shape[0], q.shape[1]
    wb = buf_k.shape[1]
    kk = jnp.concatenate([buf_k.astype(k.dtype), k], axis=1)
    vv = jnp.concatenate([buf_v.astype(v.dtype), v], axis=1)
    qpos = PAST_LEN + jnp.arange(T)
    kpos = PAST_LEN - wb + jnp.arange(wb + T)
    dist = qpos[:, None] - kpos[None, :]
    mask = (dist >= 0) & (dist < WINDOW)
    o = attend(q.reshape(Bd, T, N_KV_HEADS, GROUP, HEAD_DIM), kk, vv, mask, sinks)
    return o.reshape(Bd, T, N_HEADS * HEAD_DIM), kk[:, -wb:], vv[:, -wb:]


def causal_conv(u, prev, w, b):
    T = u.shape[1]
    upad = jnp.concatenate([prev.astype(u.dtype), u], axis=1)
    out = b + sum(upad[:, j:j + T] * w[j] for j in range(CONV_W))
    return out, upad[:, -(CONV_W - 1):]


def _lin_combine(e1, e2):
    a1, b1 = e1
    a2, b2 = e2
    return a1 * a2, a2 * b1 + b2


def rg_lru(u, pos, h0, w_a, b_a, w_x, b_x, lam):
    B, T, _ = u.shape
    uf = u.astype(jnp.float32)
    ub = uf.reshape(B, T, LRU_BLOCKS, LRU_BLOCK_W)
    r = jax.nn.sigmoid(jnp.einsum('btnd,nde->btne', ub, w_a.astype(jnp.float32)).reshape(B, T, LRU_WIDTH) + b_a)
    i = jax.nn.sigmoid(jnp.einsum('btnd,nde->btne', ub, w_x.astype(jnp.float32)).reshape(B, T, LRU_WIDTH) + b_x)
    log_a = -LRU_C * r * jax.nn.softplus(-lam.astype(jnp.float32))
    a = jnp.exp(log_a)
    mult = jnp.sqrt(-jnp.expm1(2.0 * log_a))
    mult = jnp.where((pos == 0)[None, :, None], 1.0, mult)
    bt = mult * i * uf
    bt = bt.at[:, 0].add(a[:, 0] * h0.astype(jnp.float32))
    _, h = lax.associative_scan(_lin_combine, (a, bt), axis=1)
    return h, h[:, -1]


def moe(h, w_router, b_router, w1, b1, w2, b2):
    N, D = h.shape
    A = N * TOP_K
    logits = h.astype(jnp.float32) @ w_router.astype(jnp.float32) + b_router.astype(jnp.float32)
    top_v, top_i = lax.top_k(logits, TOP_K)
    gates = jax.nn.softmax(top_v, axis=-1)
    flat_e = top_i.reshape(-1)
    flat_g = gates.reshape(-1)
    order = jnp.argsort(flat_e)
    se = flat_e[order]
    stok = (order // TOP_K).astype(jnp.int32)
    sg = flat_g[order]
    counts = jnp.bincount(flat_e, length=N_EXPERTS)
    pcounts = (counts + MOE_BLOCK - 1) // MOE_BLOCK * MOE_BLOCK
    pend = jnp.cumsum(pcounts)
    pstart = pend - pcounts
    ustart = jnp.cumsum(counts) - counts
    dest = pstart[se] + jnp.arange(A) - ustart[se]
    n_blocks = -(-(A + N_EXPERTS * (MOE_BLOCK - 1)) // MOE_BLOCK)
    P = n_blocks * MOE_BLOCK
    tok_buf = jnp.full((P,), N, jnp.int32).at[dest].set(stok)
    g_buf = jnp.zeros((P,), jnp.float32).at[dest].set(sg)
    blk_e = jnp.minimum(jnp.searchsorted(pend, jnp.arange(n_blocks) * MOE_BLOCK, side='right'), N_EXPERTS - 1)
    xs = h[jnp.minimum(tok_buf, N - 1)].reshape(n_blocks, MOE_BLOCK, D)

    def expert_block(args):
        xb, e = args
        z = xb @ w1[e] + b1[e]
        glu = jnp.minimum(z[:, :D_FF], SWIGLU_LIMIT)
        lin = jnp.clip(z[:, D_FF:], -SWIGLU_LIMIT, SWIGLU_LIMIT)
        act = glu * jax.nn.sigmoid(SWIGLU_ALPHA * glu) * (lin + 1)
        return act @ w2[e] + b2[e]

    ys = lax.map(expert_block, (xs, blk_e))
    out = jnp.zeros((N, D), jnp.float32).at[tok_buf].add(
        ys.reshape(P, D).astype(jnp.float32) * g_buf[:, None], mode='drop')
    return out.astype(h.dtype)


def setup_inputs(seed: int = 0) -> dict:
    key = jax.random.key(seed)
    ks = jax.random.split(key, 32)
    f32 = jnp.float32
    nrm = lambda i, shape, scale: jax.random.normal(ks[i], shape, f32) * scale
    wb = min(WINDOW, PAST_LEN)
    a_c = jax.random.uniform(ks[19], (DEPTH, LRU_WIDTH), f32, 0.9, 0.999)
    a = a_c ** (1.0 / LRU_C)
    lam = jnp.log(a) - jnp.log1p(-a)
    return {
        'x_prompt': nrm(0, (BATCH, SEQ, D_MODEL), 1.0),
        'x_sample': nrm(1, (DEC_BATCH, DEC_SEQ, D_MODEL), 1.0),
        'state_lru_h': nrm(2, (DEPTH, DEC_BATCH, LRU_WIDTH), 0.5),
        'state_conv': nrm(3, (DEPTH, DEC_BATCH, CONV_W - 1, LRU_WIDTH), 1.0),
        'cache_win_k': nrm(4, (DEPTH, DEC_BATCH, wb, N_KV_HEADS, HEAD_DIM), 1.0),
        'cache_win_v': nrm(5, (DEPTH, DEC_BATCH, wb, N_KV_HEADS, HEAD_DIM), 1.0),
        'c_prompt': nrm(6, (BATCH, D_MODEL), 1.0),
        'c_sample': nrm(7, (DEC_BATCH, D_MODEL), 1.0),
        'w_ada': nrm(8, (DEPTH, D_MODEL, 6 * D_MODEL), 0.5 * D_MODEL ** -0.5),
        'b_ada': nrm(9, (DEPTH, 6 * D_MODEL), 0.01),
        'norm1_g': 1.0 + nrm(10, (DEPTH, D_MODEL), 0.02),
        'w_in': nrm(11, (DEPTH, D_MODEL, IN_WIDTH), D_MODEL ** -0.5),
        'b_in': nrm(12, (DEPTH, IN_WIDTH), 0.01),
        'conv_w': nrm(13, (DEPTH, CONV_W, LRU_WIDTH), CONV_W ** -0.5),
        'conv_b': nrm(14, (DEPTH, LRU_WIDTH), 0.01),
        'lru_wa': nrm(15, (DEPTH, LRU_BLOCKS, LRU_BLOCK_W, LRU_BLOCK_W), LRU_BLOCK_W ** -0.5),
        'lru_ba': nrm(16, (DEPTH, LRU_WIDTH), 0.01),
        'lru_wx': nrm(17, (DEPTH, LRU_BLOCKS, LRU_BLOCK_W, LRU_BLOCK_W), LRU_BLOCK_W ** -0.5),
        'lru_bx': nrm(18, (DEPTH, LRU_WIDTH), 0.01),
        'lru_lambda': lam,
        'attn_sinks': nrm(20, (DEPTH, N_HEADS), 0.5),
        'w_out': nrm(21, (DEPTH, MIX_WIDTH, D_MODEL), MIX_WIDTH ** -0.5),
        'b_out': nrm(22, (DEPTH, D_MODEL), 0.01),
        'norm2_g': 1.0 + nrm(23, (DEPTH, D_MODEL), 0.02),
        'w_router': nrm(24, (DEPTH, D_MODEL, N_EXPERTS), D_MODEL ** -0.5),
        'b_router': nrm(25, (DEPTH, N_EXPERTS), 0.01),
        'w1': nrm(26, (DEPTH, N_EXPERTS, D_MODEL, 2 * D_FF), D_MODEL ** -0.5),
        'b1': nrm(27, (DEPTH, N_EXPERTS, 2 * D_FF), 0.01),
        'w2': nrm(28, (DEPTH, N_EXPERTS, D_FF, D_MODEL), D_FF ** -0.5),
        'b2': nrm(29, (DEPTH, N_EXPERTS, D_MODEL), 0.01),
        'final_g': 1.0 + nrm(30, (D_MODEL,), 0.02),
    }


def reference(x_prompt, x_sample, state_lru_h, state_conv, cache_win_k, cache_win_v, c_prompt, c_sample,
              w_ada, b_ada, norm1_g, w_in, b_in, conv_w, conv_b, lru_wa, lru_ba, lru_wx, lru_bx, lru_lambda,
              attn_sinks, w_out, b_out, norm2_g, w_router, b_router, w1, b1, w2, b2, final_g):
    o1 = LRU_WIDTH
    o2 = 2 * LRU_WIDTH
    o3 = o2 + N_HEADS * HEAD_DIM
    o4 = o3 + N_KV_HEADS * HEAD_DIM

    def layer(l, x, c, pos, lru_h0, conv_prev, attn_fn):
        B, T, _ = x.shape
        mod = (jax.nn.silu(c.astype(jnp.float32)) @ w_ada[l].astype(jnp.float32)
               + b_ada[l].astype(jnp.float32)).astype(x.dtype)[:, None, :]
        sh1, sc1, g1, sh2, sc2, g2 = jnp.split(mod, 6, axis=-1)
        h = rms_norm(x, norm1_g[l]) * (1 + sc1) + sh1
        proj = h @ w_in[l] + b_in[l]
        u, gate = proj[..., :o1], proj[..., o1:o2]
        q, k, v = proj[..., o2:o3], proj[..., o3:o4], proj[..., o4:]
        u, conv_new = causal_conv(u, conv_prev, conv_w[l], conv_b[l])
        hs, h_last = rg_lru(u, pos, lru_h0, lru_wa[l], lru_ba[l], lru_wx[l], lru_bx[l], lru_lambda[l])
        lru_out = (hs * jax.nn.gelu(gate.astype(jnp.float32))).astype(x.dtype)
        q = rope(q.reshape(B, T, N_HEADS, HEAD_DIM), pos)
        k = rope(k.reshape(B, T, N_KV_HEADS, HEAD_DIM), pos)
        v = v.reshape(B, T, N_KV_HEADS, HEAD_DIM)
        attn_out, k_buf, v_buf = attn_fn(q, k, v, attn_sinks[l].reshape(N_KV_HEADS, GROUP))
        mix = jnp.concatenate([lru_out, attn_out.astype(x.dtype)], axis=-1)
        x = x + g1 * (mix @ w_out[l] + b_out[l])
        h2 = rms_norm(x, norm2_g[l]) * (1 + sc2) + sh2
        ff = moe(h2.reshape(B * T, D_MODEL), w_router[l], b_router[l], w1[l], b1[l], w2[l], b2[l])
        x = x + g2 * ff.reshape(B, T, D_MODEL)
        return x, h_last.astype(lru_h0.dtype), conv_new, k_buf, v_buf

    Bp, S = x_prompt.shape[0], x_prompt.shape[1]
    Bd, T = x_sample.shape[0], x_sample.shape[1]
    pos_p = jnp.arange(S, dtype=jnp.int32)
    pos_s = PAST_LEN + jnp.arange(T, dtype=jnp.int32)
    xp, xs = x_prompt, x_sample
    p_h, p_c, p_k, p_v, s_h, s_c, s_k, s_v = [], [], [], [], [], [], [], []
    for l in range(DEPTH):
        h0p = jnp.zeros((Bp, LRU_WIDTH), x_prompt.dtype)
        c0p = jnp.zeros((Bp, CONV_W - 1, LRU_WIDTH), x_prompt.dtype)
        xp, hh, cc, kk, vv = layer(l, xp, c_prompt, pos_p, h0p, c0p, prompt_attention)
        p_h.append(hh); p_c.append(cc); p_k.append(kk); p_v.append(vv)
        bk, bv = cache_win_k[l], cache_win_v[l]
        samp_attn = lambda q, k, v, s, bk=bk, bv=bv: sample_attention(q, k, v, bk, bv, s)
        xs, hh, cc, kk, vv = layer(l, xs, c_sample, pos_s, state_lru_h[l], state_conv[l], samp_attn)
        s_h.append(hh); s_c.append(cc); s_k.append(kk); s_v.append(vv)
    y_prompt = rms_norm(xp, final_g)
    y_sample = rms_norm(xs, final_g)
    return (y_prompt, y_sample,
            jnp.stack(p_h), jnp.stack(p_c), jnp.stack(p_k), jnp.stack(p_v),
            jnp.stack(s_h), jnp.stack(s_c), jnp.stack(s_k), jnp.stack(s_v))
```

```python
import functools

import jax
import jax.numpy as jnp
from jax import lax
from jax.experimental import pallas as pl
from jax.experimental.pallas import tpu as pltpu

F32 = jnp.float32
BF16 = jnp.bfloat16

D_MODEL = 1024
LRU_WIDTH = 512
LRU_BLOCKS = 8
LRU_BLOCK_W = LRU_WIDTH // LRU_BLOCKS
CONV_W = 4
LRU_C = 8.0
HEAD_DIM = 64
N_HEADS = 8
N_KV_HEADS = 2
GROUP = N_HEADS // N_KV_HEADS
WINDOW = 128
ROPE_THETA = 10000.0
N_EXPERTS = 32
TOP_K = 4
D_FF = D_MODEL
SWIGLU_LIMIT = 7.0
SWIGLU_ALPHA = 1.702
NORM_EPS = 1e-5
PAST_LEN = 8192
Q_WIDTH = N_HEADS * HEAD_DIM
KV_WIDTH = N_KV_HEADS * HEAD_DIM
IN_WIDTH = 2 * LRU_WIDTH + Q_WIDTH + 2 * KV_WIDTH

LANES = 128
SUBLANES = 8
SEQ_TILE = 256
SAMPLE_BT = 32
MOE_TM = 256
COMB_TT = 256
NEG_BIG = -1e30
VMEM_LIMIT = 56 * 1024 * 1024


def _rms(x, g):
    return x * lax.rsqrt(jnp.mean(x * x, axis=-1, keepdims=True) + NORM_EPS) * g


def _shift_rows(x, d, fill, rowid):
    return jnp.where(rowid >= d, pltpu.roll(x, d, axis=0), fill)


def _lin_scan(a, b, seg, rowid):
    d = 1
    while d < seg:
        a_s = _shift_rows(a, d, 1.0, rowid)
        b_s = _shift_rows(b, d, 0.0, rowid)
        b = a * b_s + b
        a = a * a_s
        d *= 2
    return a, b


def _rope128(x, cos, sin_signed, first_half):
    sw = jnp.where(first_half, pltpu.roll(x, LANES - HEAD_DIM // 2, axis=1), pltpu.roll(x, HEAD_DIM // 2, axis=1))
    return x * cos + sw * sin_signed


def _neg_expm1(y):
    return -(jnp.tanh(0.5 * y) * (jnp.exp(y) + 1.0))


def _softplus(x):
    return jnp.maximum(x, 0.0) + jnp.log1p(jnp.exp(-jnp.abs(x)))


def _lru_coeffs(uc, w, first_pos_mask):
    ub = uc.astype(BF16)
    half = LRU_WIDTH // 2
    ra = jnp.concatenate([jnp.dot(ub[:, :half], w['ga'][0], preferred_element_type=F32),
                          jnp.dot(ub[:, half:], w['ga'][1], preferred_element_type=F32)], axis=1)
    rx = jnp.concatenate([jnp.dot(ub[:, :half], w['gx'][0], preferred_element_type=F32),
                          jnp.dot(ub[:, half:], w['gx'][1], preferred_element_type=F32)], axis=1)
    r = jax.nn.sigmoid(ra + w['lru_ba'][...])
    i = jax.nn.sigmoid(rx + w['lru_bx'][...])
    log_a = -LRU_C * r * _softplus(-w['lam'][...])
    a = jnp.exp(log_a)
    mult = jnp.sqrt(_neg_expm1(2.0 * log_a))
    if first_pos_mask is not None:
        mult = jnp.where(first_pos_mask, 1.0, mult)
    return a, mult * i * uc


def _conv_taps(u, s1, s2, s3, w):
    cw = w['conv_w']
    return w['conv_b'][...] + s3 * cw[0:1, :] + s2 * cw[1:2, :] + s1 * cw[2:3, :] + u * cw[3:4, :]


def _in_proj(x, mod, w):
    sh1, sc1 = mod
    h = _rms(x, w['norm1_g'][...]) * (1.0 + sc1) + sh1
    return jnp.dot(h.astype(BF16), w['w_in'][...], preferred_element_type=F32) + w['b_in'][...]


def _post_mix(x, mix, mod, w):
    g1, sh2, sc2 = mod
    x1 = x + g1 * (jnp.dot(mix.astype(BF16), w['w_out'][...], preferred_element_type=F32) + w['b_out'][...])
    h2 = _rms(x1, w['norm2_g'][...]) * (1.0 + sc2) + sh2
    h2_hi = h2.astype(BF16)
    h2_lo = (h2 - h2_hi.astype(F32)).astype(BF16)
    logits = (jnp.dot(h2_hi, w['wr_hi'][...], preferred_element_type=F32)
              + jnp.dot(h2_lo, w['wr_hi'][...], preferred_element_type=F32)
              + jnp.dot(h2_hi, w['wr_lo'][...], preferred_element_type=F32)) + w['b_router'][...]
    return x1, h2, logits


def _softmax_sink_pv(s, sink_col, v_bf16):
    m = jnp.maximum(jnp.max(s, axis=-1, keepdims=True), sink_col)
    p = jnp.exp(s - m)
    denom = jnp.sum(p, axis=-1, keepdims=True) + jnp.exp(sink_col - m)
    return p, denom


WEIGHT_NAMES = ('norm1_g', 'w_in', 'b_in', 'conv_w', 'conv_b', 'ga', 'gx', 'lru_ba', 'lru_bx', 'lam',
                'w_out', 'b_out', 'norm2_g', 'wr_hi', 'wr_lo', 'b_router')


def _ada_kernel(c_ref, w_ref, b_ref, o_ref):
    c = c_ref[...]
    s = c * jax.nn.sigmoid(c)
    o_ref[...] = jnp.dot(s, w_ref[...], preferred_element_type=F32, precision=lax.Precision.HIGHEST) + b_ref[...]


def _ada(c_all, w_ada, b_ada):
    rows = c_all.shape[0]
    return pl.pallas_call(
        _ada_kernel,
        grid=(6,),
        in_specs=[pl.BlockSpec((rows, D_MODEL), lambda i: (0, 0)),
                  pl.BlockSpec((D_MODEL, D_MODEL), lambda i: (0, i)),
                  pl.BlockSpec((1, D_MODEL), lambda i: (0, i))],
        out_specs=pl.BlockSpec((rows, D_MODEL), lambda i: (0, i)),
        out_shape=jax.ShapeDtypeStruct((rows, 6 * D_MODEL), F32),
        compiler_params=pltpu.CompilerParams(dimension_semantics=("arbitrary",), vmem_limit_bytes=VMEM_LIMIT),
        name="ada",
    )(c_all, w_ada, b_ada)


def _prompt_body(j, x_ref, mod_ref, cos_ref, sin_ref, sinks_ref, w, x1_ref, h2_ref, lg_ref,
                 hlast_ref, ulast_ref, klast_ref, vlast_ref, conv_c, h_c, k_c, v_c):
    ts = SEQ_TILE

    @pl.when(j == 0)
    def _():
        conv_c[...] = jnp.zeros_like(conv_c)
        h_c[...] = jnp.zeros_like(h_c)
        k_c[...] = jnp.zeros_like(k_c)
        v_c[...] = jnp.zeros_like(v_c)

    x = x_ref[...]
    mod = mod_ref[...]
    proj = _in_proj(x, (mod[0:1], mod[1:2]), w)
    u = proj[:, :LRU_WIDTH]
    gate = proj[:, LRU_WIDTH:2 * LRU_WIDTH]
    o2 = 2 * LRU_WIDTH

    rowid = lax.broadcasted_iota(jnp.int32, (ts, 1), 0)
    u_ext = jnp.concatenate([conv_c[...], u], axis=0)
    s1, s2, s3 = (pltpu.roll(u_ext, d, axis=0)[SUBLANES:] for d in (1, 2, 3))
    uc = _conv_taps(u, s1, s2, s3, w)
    conv_c[...] = u[ts - SUBLANES:]
    ulast_ref[...] = u[ts - SUBLANES:]

    first_pos = jnp.logical_and(rowid == 0, j == 0)
    a, bt = _lru_coeffs(uc, w, first_pos)
    a_cum, b_cum = _lin_scan(a, bt, ts, rowid)
    hs = b_cum + a_cum * h_c[0:1, :]
    h_tail = hs[ts - SUBLANES:]
    h_c[...] = jnp.broadcast_to(h_tail[SUBLANES - 1:SUBLANES, :], h_c.shape)
    hlast_ref[...] = h_tail
    lru_out = hs * jax.nn.gelu(gate)

    cos = cos_ref[...]
    sin = sin_ref[...]
    lane = lax.broadcasted_iota(jnp.int32, (1, LANES), 1)
    first_half = (lane % HEAD_DIM) < (HEAD_DIM // 2)
    qcols = [_rope128(proj[:, o2 + c * LANES:o2 + (c + 1) * LANES], cos, sin, first_half) for c in range(4)]
    k = _rope128(proj[:, o2 + Q_WIDTH:o2 + Q_WIDTH + KV_WIDTH], cos, sin, first_half)
    v = proj[:, o2 + Q_WIDTH + KV_WIDTH:]
    k_ext = jnp.concatenate([k_c[...], k], axis=0).astype(BF16)
    v_ext = jnp.concatenate([v_c[...], v], axis=0).astype(BF16)
    k_c[...] = k[ts - WINDOW:]
    v_c[...] = v[ts - WINDOW:]
    klast_ref[...] = k[ts - WINDOW:]
    vlast_ref[...] = v[ts - WINDOW:]

    qi = lax.broadcasted_iota(jnp.int32, (WINDOW, 2 * WINDOW), 0)
    kj = lax.broadcasted_iota(jnp.int32, (WINDOW, 2 * WINDOW), 1)
    band = jnp.logical_and(kj > qi, kj <= qi + WINDOW)
    lane_lo = lane < HEAD_DIM
    grow = lax.broadcasted_iota(jnp.int32, (GROUP * WINDOW, 1), 0) // WINDOW
    attn_cols = [[] for _ in range(4)]
    for blk in range(ts // WINDOW):
        prev_ok = jnp.logical_or(j > 0, blk > 0)
        mask = jnp.logical_and(band, jnp.logical_or(kj >= WINDOW, prev_ok))
        mask4 = jnp.concatenate([mask] * GROUP, axis=0)
        kb = k_ext[blk * WINDOW:(blk + 2) * WINDOW]
        vb = v_ext[blk * WINDOW:(blk + 2) * WINDOW]
        outs = []
        for kv in range(N_KV_HEADS):
            sel = lane_lo if kv == 0 else jnp.logical_not(lane_lo)
            qs = jnp.concatenate(
                [jnp.where(sel, qc[blk * WINDOW:(blk + 1) * WINDOW], 0.0) for qc in qcols], axis=0).astype(BF16)
            s = lax.dot_general(qs, kb, (((1,), (1,)), ((), ())), preferred_element_type=F32) * (HEAD_DIM ** -0.5)
            s = jnp.where(mask4, s, NEG_BIG)
            sink_col = jnp.zeros((GROUP * WINDOW, 1), F32)
            for g in range(GROUP):
                sink_col = jnp.where(grow == g, sinks_ref[kv * GROUP + g], sink_col)
            p, denom = _softmax_sink_pv(s, sink_col, vb)
            outs.append(jnp.dot(p.astype(BF16), vb, preferred_element_type=F32) / denom)
        for c in range(4):
            attn_cols[c].append(jnp.where(lane_lo, outs[0][c * WINDOW:(c + 1) * WINDOW],
                                          outs[1][c * WINDOW:(c + 1) * WINDOW]))
    attn = jnp.concatenate([jnp.concatenate(cols, axis=0) for cols in attn_cols], axis=1)

    mix = jnp.concatenate([lru_out, attn], axis=1)
    x1, h2, logits = _post_mix(x, mix, (mod[2:3], mod[3:4], mod[4:5]), w)
    x1_ref[...] = x1
    h2_ref[...] = h2
    lg_ref[...] = logits


def _expand_rows(m, t):
    b, wd = m.shape
    return jnp.broadcast_to(m[:, None, :], (b, t, wd)).reshape(b * t, wd)


def _sample_body(x_ref, mod_ref, cos_ref, sin_ref, sinks_ref, h0_ref, cprev_ref, ck_ref, cv_ref, w,
                 x1_ref, h2_ref, lg_ref, g2_ref, hs_ref, u_ref, ko_ref, vo_ref):
    bt_, t = SAMPLE_BT, SUBLANES
    rows = bt_ * t

    x = x_ref[...]
    mods = [_expand_rows(mod_ref[i], t) for i in range(6)]
    proj = _in_proj(x, (mods[0], mods[1]), w)
    u = proj[:, :LRU_WIDTH]
    gate = proj[:, LRU_WIDTH:2 * LRU_WIDTH]
    o2 = 2 * LRU_WIDTH
    u_ref[...] = u

    rowid = lax.broadcasted_iota(jnp.int32, (rows, 1), 0) % t
    cprev = cprev_ref[...]
    taps = []
    for d in (1, 2, 3):
        taps.append(jnp.where(rowid >= d, pltpu.roll(u, d, axis=0),
                              pltpu.roll(cprev, (d - (CONV_W - 1)) % rows, axis=0)))
    uc = _conv_taps(u, taps[0], taps[1], taps[2], w)

    a, bt = _lru_coeffs(uc, w, None)
    bt = bt + a * h0_ref[...]
    _, hs = _lin_scan(a, bt, t, rowid)
    hs_ref[...] = hs
    lru_out = hs * jax.nn.gelu(gate)

    cos = cos_ref[...]
    sin = sin_ref[...]
    lane = lax.broadcasted_iota(jnp.int32, (1, LANES), 1)
    first_half = (lane % HEAD_DIM) < (HEAD_DIM // 2)
    qcols = [_rope128(proj[:, o2 + c * LANES:o2 + (c + 1) * LANES], cos, sin, first_half) for c in range(4)]
    k = _rope128(proj[:, o2 + Q_WIDTH:o2 + Q_WIDTH + KV_WIDTH], cos, sin, first_half)
    v = proj[:, o2 + Q_WIDTH + KV_WIDTH:]
    k3 = k.reshape(bt_, t, KV_WIDTH)
    v3 = v.reshape(bt_, t, KV_WIDTH)
    ck = ck_ref[...]
    cv = cv_ref[...]
    ko_ref[:, :WINDOW - t, :] = ck[:, t:, :]
    ko_ref[:, WINDOW - t:, :] = k3
    vo_ref[:, :WINDOW - t, :] = cv[:, t:, :]
    vo_ref[:, WINDOW - t:, :] = v3

    ckb, cvb, k3b, v3b = ck.astype(BF16), cv.astype(BF16), k3.astype(BF16), v3.astype(BF16)
    lane_lo = lane < HEAD_DIM
    gq = GROUP * t
    tq = lax.broadcasted_iota(jnp.int32, (1, gq, 1), 1) % t
    mask_c = lax.broadcasted_iota(jnp.int32, (1, gq, WINDOW), 2) > tq
    mask_n = lax.broadcasted_iota(jnp.int32, (1, gq, t), 2) <= tq
    grow = lax.broadcasted_iota(jnp.int32, (1, gq, 1), 1) // t
    bdims = (((2,), (2,)), ((0,), (0,)))
    pdims = (((2,), (1,)), ((0,), (0,)))
    outs = []
    for kv in range(N_KV_HEADS):
        sel = lane_lo if kv == 0 else jnp.logical_not(lane_lo)
        q3 = jnp.concatenate([jnp.where(sel, qc, 0.0).reshape(bt_, t, LANES) for qc in qcols], axis=1).astype(BF16)
        sc = lax.dot_general(q3, ckb, bdims, preferred_element_type=F32) * (HEAD_DIM ** -0.5)
        sn = lax.dot_general(q3, k3b, bdims, preferred_element_type=F32) * (HEAD_DIM ** -0.5)
        sc = jnp.where(mask_c, sc, NEG_BIG)
        sn = jnp.where(mask_n, sn, NEG_BIG)
        sink_col = jnp.zeros((1, gq, 1), F32)
        for g in range(GROUP):
            sink_col = jnp.where(grow == g, sinks_ref[kv * GROUP + g], sink_col)
        m = jnp.maximum(jnp.maximum(jnp.max(sc, axis=-1, keepdims=True), jnp.max(sn, axis=-1, keepdims=True)),
                        sink_col)
        pc = jnp.exp(sc - m)
        pn = jnp.exp(sn - m)
        denom = jnp.sum(pc, axis=-1, keepdims=True) + jnp.sum(pn, axis=-1, keepdims=True) + jnp.exp(sink_col - m)
        o = (lax.dot_general(pc.astype(BF16), cvb, pdims, preferred_element_type=F32)
             + lax.dot_general(pn.astype(BF16), v3b, pdims, preferred_element_type=F32)) / denom
        outs.append(o)
    attn = jnp.concatenate(
        [jnp.where(lane_lo, outs[0][:, c * t:(c + 1) * t, :], outs[1][:, c * t:(c + 1) * t, :]).reshape(rows, LANES)
         for c in range(4)], axis=1)

    mix = jnp.concatenate([lru_out, attn], axis=1)
    x1, h2, logits = _post_mix(x, mix, (mods[2], mods[3], mods[4]), w)
    x1_ref[...] = x1
    h2_ref[...] = h2
    lg_ref[...] = logits
    g2_ref[...] = mods[5]


def _mixer_kernel(n_prompt_tiles, tiles_per_seq,
                  xp_ref, modp_ref, cosp_ref, sinp_ref, xs_ref, mods_ref, coss_ref, sins_ref, sinks_ref,
                  h0_ref, cprev_ref, ck_ref, cv_ref, *rest):
    nw = len(WEIGHT_NAMES)
    w = dict(zip(WEIGHT_NAMES, rest[:nw]))
    (x1_ref, h2_ref, lg_ref, hlast_ref, ulast_ref, klast_ref, vlast_ref,
     g2_ref, hs_ref, u_ref, ko_ref, vo_ref, conv_c, h_c, k_c, v_c) = rest[nw:]
    i = pl.program_id(0)

    @pl.when(i < n_prompt_tiles)
    def _():
        _prompt_body(i % tiles_per_seq, xp_ref, modp_ref, cosp_ref, sinp_ref, sinks_ref, w, x1_ref, h2_ref, lg_ref,
                     hlast_ref, ulast_ref, klast_ref, vlast_ref, conv_c, h_c, k_c, v_c)

    @pl.when(i >= n_prompt_tiles)
    def _():
        _sample_body(xs_ref, mods_ref, coss_ref, sins_ref, sinks_ref, h0_ref, cprev_ref, ck_ref, cv_ref, w,
                     x1_ref, h2_ref, lg_ref, g2_ref, hs_ref, u_ref, ko_ref, vo_ref)


def _moe_kernel(blk_e_ref, tokc_ref, tokn_ref, h2_hbm, w1_ref, b1_ref, w2_ref, b2_ref, g_ref, o_ref,
                xbuf, w1b, w2b, sem):
    i = pl.program_id(0)
    nb = pl.num_programs(0)
    slot = i % 2
    tm = MOE_TM

    def gather_copy(tok, r, s):
        return pltpu.make_async_copy(h2_hbm.at[pl.ds(tok, 1)], xbuf.at[s, pl.ds(r, 1)], sem.at[s])

    def issue(tok_ref, s):
        def body(r, carry):
            gather_copy(tok_ref[0, 0, r], r, s).start()
            return carry
        lax.fori_loop(0, tm, body, 0, unroll=8)

    @pl.when(i == 0)
    def _():
        issue(tokc_ref, 0)

    @pl.when(i + 1 < nb)
    def _():
        issue(tokn_ref, 1 - slot)

    pltpu.make_async_copy(h2_hbm.at[pl.ds(0, tm)], xbuf.at[slot], sem.at[slot]).wait()

    e_cur = blk_e_ref[i]
    e_prev = blk_e_ref[jnp.maximum(i - 1, 0)]

    @pl.when(jnp.logical_or(i == 0, e_cur != e_prev))
    def _():
        chunk = 128
        def cast_body(c, carry):
            r0 = pl.multiple_of(c * chunk, chunk)
            w1b[pl.ds(r0, chunk), :] = w1_ref[pl.ds(r0, chunk), :].astype(BF16)
            w2b[pl.ds(r0, chunk), :] = w2_ref[pl.ds(r0, chunk), :].astype(BF16)
            return carry
        lax.fori_loop(0, D_MODEL // chunk, cast_body, 0)

    x = xbuf[slot].astype(BF16)
    z = jnp.dot(x, w1b[...], preferred_element_type=F32) + b1_ref[...]
    glu = jnp.minimum(z[:, :D_FF], SWIGLU_LIMIT)
    lin = jnp.clip(z[:, D_FF:], -SWIGLU_LIMIT, SWIGLU_LIMIT)
    act = glu * jax.nn.sigmoid(SWIGLU_ALPHA * glu) * (lin + 1.0)
    y = jnp.dot(act.astype(BF16), w2b[...], preferred_element_type=F32) + b2_ref[...]
    o_ref[...] = y * g_ref[...]


def _moe(blk_e, tok_blocks, g_col, h2_all, w1, b1, w2, b2):
    nb = blk_e.shape[0]
    tm = MOE_TM
    grid_spec = pltpu.PrefetchScalarGridSpec(
        num_scalar_prefetch=1,
        grid=(nb,),
        in_specs=[
            pl.BlockSpec((1, 1, tm), lambda i, be: (i, 0, 0), memory_space=pltpu.SMEM),
            pl.BlockSpec((1, 1, tm), lambda i, be: (jnp.minimum(i + 1, nb - 1), 0, 0), memory_space=pltpu.SMEM),
            pl.BlockSpec(memory_space=pl.ANY),
            pl.BlockSpec((None, D_MODEL, 2 * D_FF), lambda i, be: (be[i], 0, 0)),
            pl.BlockSpec((None, 1, 2 * D_FF), lambda i, be: (be[i], 0, 0)),
            pl.BlockSpec((None, D_FF, D_MODEL), lambda i, be: (be[i], 0, 0)),
            pl.BlockSpec((None, 1, D_MODEL), lambda i, be: (be[i], 0, 0)),
            pl.BlockSpec((tm, 1), lambda i, be: (i, 0)),
        ],
        out_specs=pl.BlockSpec((tm, D_MODEL), lambda i, be: (i, 0)),
        scratch_shapes=[pltpu.VMEM((2, tm, D_MODEL), F32),
                        pltpu.VMEM((D_MODEL, 2 * D_FF), BF16),
                        pltpu.VMEM((D_FF, D_MODEL), BF16),
                        pltpu.SemaphoreType.DMA((2,))],
    )
    return pl.pallas_call(
        _moe_kernel,
        grid_spec=grid_spec,
        out_shape=jax.ShapeDtypeStruct((nb * tm, D_MODEL), F32),
        compiler_params=pltpu.CompilerParams(dimension_semantics=("arbitrary",), vmem_limit_bytes=VMEM_LIMIT),
        name="moe",
    )(blk_e, tok_blocks, tok_blocks, h2_all, w1, b1, w2, b2, g_col)


def _combine_kernel(n_prompt_tiles, slotc_ref, slotn_ref, ys_hbm, x1_ref, modp_ref, g2s_ref, fg_ref, o_ref, buf, sem):
    i = pl.program_id(0)
    nb = pl.num_programs(0)
    slot = i % 2
    tt = COMB_TT

    def issue(slot_ref, s):
        def body(t, carry):
            for kk in range(TOP_K):
                src = slot_ref[0, 0, t * TOP_K + kk]
                pltpu.make_async_copy(ys_hbm.at[pl.ds(src, 1)], buf.at[s, pl.ds(kk * tt + t, 1)], sem.at[s]).start()
            return carry
        lax.fori_loop(0, tt, body, 0, unroll=4)

    @pl.when(i == 0)
    def _():
        issue(slotc_ref, 0)

    @pl.when(i + 1 < nb)
    def _():
        issue(slotn_ref, 1 - slot)

    pltpu.make_async_copy(ys_hbm.at[pl.ds(0, TOP_K * tt)], buf.at[slot], sem.at[slot]).wait()

    ff = buf[slot, 0:tt, :]
    for kk in range(1, TOP_K):
        ff = ff + buf[slot, kk * tt:(kk + 1) * tt, :]
    g2 = jnp.where(i < n_prompt_tiles, modp_ref[5:6, :], g2s_ref[...])
    x = x1_ref[...] + g2 * ff
    o_ref[...] = _rms(x, fg_ref[...])


def _combine(slot_blocks, ys, x1_all, mod_p, g2_rows, final_g, n_prompt_tiles, tiles_per_batch):
    nt = slot_blocks.shape[0]
    tt = COMB_TT
    npt = n_prompt_tiles
    return pl.pallas_call(
        functools.partial(_combine_kernel, npt),
        grid=(nt,),
        in_specs=[
            pl.BlockSpec((1, 1, TOP_K * tt), lambda i: (i, 0, 0), memory_space=pltpu.SMEM),
            pl.BlockSpec((1, 1, TOP_K * tt), lambda i: (jnp.minimum(i + 1, nt - 1), 0, 0), memory_space=pltpu.SMEM),
            pl.BlockSpec(memory_space=pl.ANY),
            pl.BlockSpec((tt, D_MODEL), lambda i: (i, 0)),
            pl.BlockSpec((None, 6, D_MODEL), lambda i: (jnp.minimum(i, npt - 1) // tiles_per_batch, 0, 0)),
            pl.BlockSpec((tt, D_MODEL), lambda i: (jnp.maximum(i - npt, 0), 0)),
            pl.BlockSpec((1, D_MODEL), lambda i: (0, 0)),
        ],
        out_specs=pl.BlockSpec((tt, D_MODEL), lambda i: (i, 0)),
        out_shape=jax.ShapeDtypeStruct((nt * tt, D_MODEL), F32),
        scratch_shapes=[pltpu.VMEM((2, TOP_K * tt, D_MODEL), F32), pltpu.SemaphoreType.DMA((2,))],
        compiler_params=pltpu.CompilerParams(dimension_semantics=("arbitrary",), vmem_limit_bytes=VMEM_LIMIT),
        name="combine",
    )(slot_blocks, slot_blocks, ys, x1_all, mod_p, g2_rows, final_g)


def _block_diag_halves(wg):
    halves = []
    for hh in range(2):
        rows = []
        for bi in range(4):
            row = [wg[hh * 4 + bi] if bj == bi else jnp.zeros((LRU_BLOCK_W, LRU_BLOCK_W), wg.dtype) for bj in range(4)]
            rows.append(jnp.concatenate(row, axis=1))
        halves.append(jnp.concatenate(rows, axis=0))
    return jnp.stack(halves).astype(BF16)


def _rope_tables(pos):
    half = HEAD_DIM // 2
    inv = ROPE_THETA ** (-jnp.arange(half, dtype=F32) / half)
    ang = pos.astype(F32)[:, None] * inv[None, :]
    cos = jnp.cos(ang)
    sin = jnp.sin(ang)
    cos128 = jnp.concatenate([cos, cos, cos, cos], axis=1)
    sin128 = jnp.concatenate([-sin, sin, -sin, sin], axis=1)
    return cos128, sin128


def _full_spec(arr, grid_rank):
    zeros = (0,) * arr.ndim
    if grid_rank == 1:
        return pl.BlockSpec(arr.shape, lambda i: zeros)
    return pl.BlockSpec(arr.shape, lambda b, j: zeros)


def kernel(x_prompt, x_sample, state_lru_h, state_conv, cache_win_k, cache_win_v, c_prompt, c_sample, w_ada, b_ada, norm1_g, w_in, b_in, conv_w, conv_b, lru_wa, lru_ba, lru_wx, lru_bx, lru_lambda, attn_sinks, w_out, b_out, norm2_g, w_router, b_router, w1, b1, w2, b2, final_g):
    bp, seq, _ = x_prompt.shape
    bd, tdec, _ = x_sample.shape
    assert tdec == SUBLANES and seq % SEQ_TILE == 0 and bd % SAMPLE_BT == 0
    n_prompt = bp * seq
    n_sample = bd * tdec
    n_tok = n_prompt + n_sample
    l = 0

    head_perm = jnp.array([h for c in range(4) for h in (c, GROUP + c)], dtype=jnp.int32)
    qcol_perm = (head_perm[:, None] * HEAD_DIM + jnp.arange(HEAD_DIM, dtype=jnp.int32)[None, :]).reshape(-1)
    o2 = 2 * LRU_WIDTH
    in_perm = jnp.concatenate([jnp.arange(o2, dtype=jnp.int32), o2 + qcol_perm,
                               jnp.arange(o2 + Q_WIDTH, IN_WIDTH, dtype=jnp.int32)])
    out_perm = jnp.concatenate([jnp.arange(LRU_WIDTH, dtype=jnp.int32), LRU_WIDTH + qcol_perm])
    sinks_perm = attn_sinks[l]

    wr = w_router[l]
    wr_hi = wr.astype(BF16)
    weights = dict(
        norm1_g=norm1_g[l][None, :], w_in=w_in[l][:, in_perm].astype(BF16), b_in=b_in[l][in_perm][None, :],
        conv_w=conv_w[l], conv_b=conv_b[l][None, :],
        ga=_block_diag_halves(lru_wa[l]), gx=_block_diag_halves(lru_wx[l]),
        lru_ba=lru_ba[l][None, :], lru_bx=lru_bx[l][None, :], lam=lru_lambda[l][None, :],
        w_out=w_out[l][out_perm, :].astype(BF16), b_out=b_out[l][None, :], norm2_g=norm2_g[l][None, :],
        wr_hi=wr_hi, wr_lo=(wr - wr_hi.astype(F32)).astype(BF16), b_router=b_router[l][None, :],
    )
    wlist = [weights[n] for n in WEIGHT_NAMES]

    mod_all = _ada(jnp.concatenate([c_prompt, c_sample], axis=0), w_ada[l], b_ada[l][None, :])
    mod_p = mod_all[:bp].reshape(bp, 6, D_MODEL)
    mod_s = mod_all[bp:].reshape(bd, 6, D_MODEL).transpose(1, 0, 2)

    cos_p, sin_p = _rope_tables(jnp.arange(seq, dtype=jnp.int32))
    cos_s, sin_s = _rope_tables(PAST_LEN + jnp.arange(tdec, dtype=jnp.int32))
    cos_s = jnp.tile(cos_s, (SAMPLE_BT, 1))
    sin_s = jnp.tile(sin_s, (SAMPLE_BT, 1))
    h0_rows = jnp.pad(state_lru_h[l][:, None, :], ((0, 0), (0, tdec - 1), (0, 0))).reshape(n_sample, LRU_WIDTH)
    cprev_rows = jnp.pad(state_conv[l], ((0, 0), (0, tdec - (CONV_W - 1)), (0, 0))).reshape(n_sample, LRU_WIDTH)
    ck = cache_win_k[l].reshape(bd, WINDOW, KV_WIDTH)
    cv = cache_win_v[l].reshape(bd, WINDOW, KV_WIDTH)
    nj = seq // SEQ_TILE
    npt = n_prompt // SEQ_TILE
    nst = n_sample // SEQ_TILE
    p_tile = lambda i: jnp.minimum(i, npt - 1)
    s_tile = lambda i: jnp.maximum(i - npt, 0)
    all_rows = lambda width: pl.BlockSpec((SEQ_TILE, width), lambda i: (i, 0))
    p_rows = lambda width: pl.BlockSpec((SEQ_TILE, width), lambda i: (p_tile(i), 0))
    s_rows = lambda width: pl.BlockSpec((SEQ_TILE, width), lambda i: (s_tile(i), 0))
    p_tail = lambda rows, width: pl.BlockSpec((None, rows, width), lambda i: (p_tile(i) // nj, 0, 0))
    cache_spec = pl.BlockSpec((SAMPLE_BT, WINDOW, KV_WIDTH), lambda i: (s_tile(i), 0, 0))
    mixer_out_shapes = (
        jax.ShapeDtypeStruct((n_tok, D_MODEL), F32),
        jax.ShapeDtypeStruct((n_tok, D_MODEL), F32),
        jax.ShapeDtypeStruct((n_tok, N_EXPERTS), F32),
        jax.ShapeDtypeStruct((bp, SUBLANES, LRU_WIDTH), F32),
        jax.ShapeDtypeStruct((bp, SUBLANES, LRU_WIDTH), F32),
        jax.ShapeDtypeStruct((bp, WINDOW, KV_WIDTH), F32),
        jax.ShapeDtypeStruct((bp, WINDOW, KV_WIDTH), F32),
        jax.ShapeDtypeStruct((n_sample, D_MODEL), F32),
        jax.ShapeDtypeStruct((n_sample, LRU_WIDTH), F32),
        jax.ShapeDtypeStruct((n_sample, LRU_WIDTH), F32),
        jax.ShapeDtypeStruct((bd, WINDOW, KV_WIDTH), F32),
        jax.ShapeDtypeStruct((bd, WINDOW, KV_WIDTH), F32),
    )
    (x1_all, h2_all, logits, hlast_p, ulast_p, klast_p, vlast_p, g2_rows, hs_s, u_s, s_k, s_v) = pl.pallas_call(
        functools.partial(_mixer_kernel, npt, nj),
        grid=(npt + nst,),
        in_specs=[p_rows(D_MODEL),
                  pl.BlockSpec((None, 6, D_MODEL), lambda i: (p_tile(i) // nj, 0, 0)),
                  pl.BlockSpec((SEQ_TILE, LANES), lambda i: (p_tile(i) % nj, 0)),
                  pl.BlockSpec((SEQ_TILE, LANES), lambda i: (p_tile(i) % nj, 0)),
                  s_rows(D_MODEL),
                  pl.BlockSpec((6, SAMPLE_BT, D_MODEL), lambda i: (0, s_tile(i), 0)),
                  pl.BlockSpec((SEQ_TILE, LANES), lambda i: (0, 0)),
                  pl.BlockSpec((SEQ_TILE, LANES), lambda i: (0, 0)),
                  pl.BlockSpec(memory_space=pltpu.SMEM),
                  s_rows(LRU_WIDTH), s_rows(LRU_WIDTH), cache_spec, cache_spec]
                 + [_full_spec(a, 1) for a in wlist],
        out_specs=(all_rows(D_MODEL), all_rows(D_MODEL), all_rows(N_EXPERTS),
                   p_tail(SUBLANES, LRU_WIDTH), p_tail(SUBLANES, LRU_WIDTH),
                   p_tail(WINDOW, KV_WIDTH), p_tail(WINDOW, KV_WIDTH),
                   s_rows(D_MODEL), s_rows(LRU_WIDTH), s_rows(LRU_WIDTH), cache_spec, cache_spec),
        out_shape=mixer_out_shapes,
        scratch_shapes=[pltpu.VMEM((SUBLANES, LRU_WIDTH), F32), pltpu.VMEM((SUBLANES, LRU_WIDTH), F32),
                        pltpu.VMEM((WINDOW, KV_WIDTH), F32), pltpu.VMEM((WINDOW, KV_WIDTH), F32)],
        compiler_params=pltpu.CompilerParams(dimension_semantics=("arbitrary",), vmem_limit_bytes=VMEM_LIMIT),
        name="mixer",
    )(x_prompt.reshape(n_prompt, D_MODEL), mod_p, cos_p, sin_p,
      x_sample.reshape(n_sample, D_MODEL), mod_s, cos_s, sin_s, sinks_perm, h0_rows, cprev_rows, ck, cv, *wlist)

    top_v, top_i = lax.top_k(logits, TOP_K)
    gates = jax.nn.softmax(top_v, axis=-1)
    n_assign = n_tok * TOP_K
    flat_e = top_i.reshape(-1)
    flat_g = gates.reshape(-1)
    order = jnp.argsort(flat_e)
    se = flat_e[order]
    counts = jnp.bincount(flat_e, length=N_EXPERTS)
    pcounts = (counts + MOE_TM - 1) // MOE_TM * MOE_TM
    pend = jnp.cumsum(pcounts)
    pstart = pend - pcounts
    ustart = jnp.cumsum(counts) - counts
    dest = (pstart[se] + jnp.arange(n_assign) - ustart[se]).astype(jnp.int32)
    n_blocks = -(-(n_assign + N_EXPERTS * (MOE_TM - 1)) // MOE_TM)
    n_rows = n_blocks * MOE_TM
    tok_buf = jnp.zeros((n_rows,), jnp.int32).at[dest].set((order // TOP_K).astype(jnp.int32))
    g_buf = jnp.zeros((n_rows,), F32).at[dest].set(flat_g[order])
    blk_e = jnp.minimum(jnp.searchsorted(pend, jnp.arange(n_blocks) * MOE_TM, side='right'),
                        N_EXPERTS - 1).astype(jnp.int32)
    slot_of = jnp.zeros((n_assign,), jnp.int32).at[order].set(dest)

    ys = _moe(blk_e, tok_buf.reshape(n_blocks, 1, MOE_TM), g_buf[:, None], h2_all,
              w1[l], b1[l][:, None, :], w2[l], b2[l][:, None, :])

    n_tiles = n_tok // COMB_TT
    y_all = _combine(slot_of.reshape(n_tiles, 1, COMB_TT * TOP_K), ys, x1_all, mod_p, g2_rows,
                     final_g[None, :], n_prompt // COMB_TT, seq // COMB_TT)

    y_prompt = y_all[:n_prompt].reshape(bp, seq, D_MODEL)
    y_sample = y_all[n_prompt:].reshape(bd, tdec, D_MODEL)
    p_h = hlast_p[:, SUBLANES - 1, :][None]
    p_c = ulast_p[:, SUBLANES - (CONV_W - 1):, :][None]
    p_k = klast_p.reshape(1, bp, WINDOW, N_KV_HEADS, HEAD_DIM)
    p_v = vlast_p.reshape(1, bp, WINDOW, N_KV_HEADS, HEAD_DIM)
    s_h = hs_s.reshape(bd, tdec, LRU_WIDTH)[:, tdec - 1, :][None]
    s_c = u_s.reshape(bd, tdec, LRU_WIDTH)[:, tdec - (CONV_W - 1):, :][None]
    s_kk = s_k.reshape(1, bd, WINDOW, N_KV_HEADS, HEAD_DIM)
    s_vv = s_v.reshape(1, bd, WINDOW, N_KV_HEADS, HEAD_DIM)
    return (y_prompt, y_sample, p_h, p_c, p_k, p_v, s_h, s_c, s_kk, s_vv)
```

```python
import functools

import jax
import jax.numpy as jnp
from jax import lax
from jax.experimental import pallas as pl
from jax.experimental.pallas import tpu as pltpu

F32 = jnp.float32
BF16 = jnp.bfloat16

D_MODEL = 1024
LRU_WIDTH = 512
LRU_BLOCKS = 8
LRU_BLOCK_W = LRU_WIDTH // LRU_BLOCKS
CONV_W = 4
LRU_C = 8.0
HEAD_DIM = 64
N_HEADS = 8
N_KV_HEADS = 2
GROUP = N_HEADS // N_KV_HEADS
WINDOW = 128
ROPE_THETA = 10000.0
N_EXPERTS = 32
TOP_K = 4
D_FF = D_MODEL
SWIGLU_LIMIT = 7.0
SWIGLU_ALPHA = 1.702
NORM_EPS = 1e-5
PAST_LEN = 8192
Q_WIDTH = N_HEADS * HEAD_DIM
KV_WIDTH = N_KV_HEADS * HEAD_DIM
IN_WIDTH = 2 * LRU_WIDTH + Q_WIDTH + 2 * KV_WIDTH

LANES = 128
SUBLANES = 8
SEQ_TILE = 256
SAMPLE_BT = 32
MOE_TM = 256
COMB_TT = 256
NEG_BIG = -1e30
VMEM_LIMIT = 56 * 1024 * 1024


def _rms(x, g):
    return x * lax.rsqrt(jnp.mean(x * x, axis=-1, keepdims=True) + NORM_EPS) * g


def _shift_rows(x, d, fill, rowid):
    return jnp.where(rowid >= d, pltpu.roll(x, d, axis=0), fill)


def _lin_scan(a, b, seg, rowid):
    d = 1
    while d < seg:
        a_s = _shift_rows(a, d, 1.0, rowid)
        b_s = _shift_rows(b, d, 0.0, rowid)
        b = a * b_s + b
        a = a * a_s
        d *= 2
    return a, b


def _rope128(x, cos, sin_signed, first_half):
    sw = jnp.where(first_half, pltpu.roll(x, LANES - HEAD_DIM // 2, axis=1), pltpu.roll(x, HEAD_DIM // 2, axis=1))
    return x * cos + sw * sin_signed


def _neg_expm1(y):
    return -(jnp.tanh(0.5 * y) * (jnp.exp(y) + 1.0))


def _softplus(x):
    return jnp.maximum(x, 0.0) + jnp.log1p(jnp.exp(-jnp.abs(x)))


def _lru_coeffs(uc, w, first_pos_mask):
    ub = uc.astype(BF16)
    half = LRU_WIDTH // 2
    ra = jnp.concatenate([jnp.dot(ub[:, :half], w['ga'][0], preferred_element_type=F32),
                          jnp.dot(ub[:, half:], w['ga'][1], preferred_element_type=F32)], axis=1)
    rx = jnp.concatenate([jnp.dot(ub[:, :half], w['gx'][0], preferred_element_type=F32),
                          jnp.dot(ub[:, half:], w['gx'][1], preferred_element_type=F32)], axis=1)
    r = jax.nn.sigmoid(ra + w['lru_ba'][...])
    i = jax.nn.sigmoid(rx + w['lru_bx'][...])
    log_a = -LRU_C * r * _softplus(-w['lam'][...])
    a = jnp.exp(log_a)
    mult = jnp.sqrt(_neg_expm1(2.0 * log_a))
    if first_pos_mask is not None:
        mult = jnp.where(first_pos_mask, 1.0, mult)
    return a, mult * i * uc


def _conv_taps(u, s1, s2, s3, w):
    cw = w['conv_w']
    return w['conv_b'][...] + s3 * cw[0:1, :] + s2 * cw[1:2, :] + s1 * cw[2:3, :] + u * cw[3:4, :]


def _in_proj(x, mod, w):
    sh1, sc1 = mod
    h = _rms(x, w['norm1_g'][...]) * (1.0 + sc1) + sh1
    return jnp.dot(h.astype(BF16), w['w_in'][...], preferred_element_type=F32) + w['b_in'][...]


def _post_mix(x, mix, mod, w):
    g1, sh2, sc2 = mod
    x1 = x + g1 * (jnp.dot(mix.astype(BF16), w['w_out'][...], preferred_element_type=F32) + w['b_out'][...])
    h2 = _rms(x1, w['norm2_g'][...]) * (1.0 + sc2) + sh2
    h2_hi = h2.astype(BF16)
    h2_lo = (h2 - h2_hi.astype(F32)).astype(BF16)
    logits = (jnp.dot(h2_hi, w['wr_hi'][...], preferred_element_type=F32)
              + jnp.dot(h2_lo, w['wr_hi'][...], preferred_element_type=F32)
              + jnp.dot(h2_hi, w['wr_lo'][...], preferred_element_type=F32)) + w['b_router'][...]
    return x1, h2, logits


ROUTE_E, ROUTE_G, ROUTE_R = 0, TOP_K, 2 * TOP_K


def _lane_roll1(v, shift):
    return pltpu.roll(jnp.broadcast_to(v, (SUBLANES, LANES)), shift, axis=1)[0:1]


def _route_tile(lg):
    rows = lg.shape[0]
    lane = lax.broadcasted_iota(jnp.int32, (1, LANES), 1)
    e_of = lane % N_EXPERTS
    grp = lane // N_EXPERTS
    onehot = jnp.zeros((rows, LANES), F32)
    vals, ids = [], []
    for k in range(TOP_K):
        m = jnp.max(lg, axis=1, keepdims=True)
        idx = jnp.min(jnp.where(lg == m, e_of, N_EXPERTS), axis=1, keepdims=True)
        sel = e_of == idx
        lg = jnp.where(sel, -jnp.inf, lg)
        onehot = jnp.where(jnp.logical_and(sel, grp == k), 1.0, onehot)
        vals.append(m)
        ids.append(idx)
    ex = [jnp.exp(v - vals[0]) for v in vals]
    denom = ex[0] + ex[1] + ex[2] + ex[3]

    r_i = lax.broadcasted_iota(jnp.int32, (rows, rows), 0)
    c_i = lax.broadcasted_iota(jnp.int32, (rows, rows), 1)
    strict_lower = jnp.where(r_i > c_i, 1.0, 0.0).astype(BF16)
    prefix = jnp.dot(strict_lower, onehot.astype(BF16), preferred_element_type=F32)
    cnt = jnp.sum(onehot, axis=0, keepdims=True)
    base = jnp.zeros((1, LANES), F32)
    tot = cnt
    for s in range(1, TOP_K):
        rolled = _lane_roll1(cnt, s * N_EXPERTS)
        base = base + jnp.where(lane >= s * N_EXPERTS, rolled, 0.0)
        tot = tot + rolled
    ranked = onehot * (prefix + base)

    route = jnp.zeros((rows, LANES), F32)
    for k in range(TOP_K):
        rank_k = jnp.sum(jnp.where(grp == k, ranked, 0.0), axis=1, keepdims=True)
        route = jnp.where(lane == ROUTE_E + k, ids[k].astype(F32), route)
        route = jnp.where(lane == ROUTE_G + k, ex[k] / denom, route)
        route = jnp.where(lane == ROUTE_R + k, rank_k, route)
    return route, tot


def _softmax_sink_pv(s, sink_col, v_bf16):
    m = jnp.maximum(jnp.max(s, axis=-1, keepdims=True), sink_col)
    p = jnp.exp(s - m)
    denom = jnp.sum(p, axis=-1, keepdims=True) + jnp.exp(sink_col - m)
    return p, denom


WEIGHT_NAMES = ('norm1_g', 'w_in', 'b_in', 'conv_w', 'conv_b', 'ga', 'gx', 'lru_ba', 'lru_bx', 'lam',
                'w_out', 'b_out', 'norm2_g', 'wr_hi', 'wr_lo', 'b_router')


def _ada_kernel(c_ref, w_ref, b_ref, o_ref):
    c = c_ref[...]
    s = c * jax.nn.sigmoid(c)
    o_ref[...] = jnp.dot(s, w_ref[...], preferred_element_type=F32, precision=lax.Precision.HIGHEST) + b_ref[...]


def _ada(c_all, w_ada, b_ada):
    rows = c_all.shape[0]
    return pl.pallas_call(
        _ada_kernel,
        grid=(6,),
        in_specs=[pl.BlockSpec((rows, D_MODEL), lambda i: (0, 0)),
                  pl.BlockSpec((D_MODEL, D_MODEL), lambda i: (0, i)),
                  pl.BlockSpec((1, D_MODEL), lambda i: (0, i))],
        out_specs=pl.BlockSpec((rows, D_MODEL), lambda i: (0, i)),
        out_shape=jax.ShapeDtypeStruct((rows, 6 * D_MODEL), F32),
        compiler_params=pltpu.CompilerParams(dimension_semantics=("arbitrary",), vmem_limit_bytes=VMEM_LIMIT),
        name="ada",
    )(c_all, w_ada, b_ada)


def _prompt_body(j, x_ref, mod_ref, cos_ref, sin_ref, sinks_ref, w, x1_ref, h2_ref, lg_ref,
                 hlast_ref, ulast_ref, klast_ref, vlast_ref, conv_c, h_c, k_c, v_c):
    ts = SEQ_TILE

    @pl.when(j == 0)
    def _():
        conv_c[...] = jnp.zeros_like(conv_c)
        h_c[...] = jnp.zeros_like(h_c)
        k_c[...] = jnp.zeros_like(k_c)
        v_c[...] = jnp.zeros_like(v_c)

    x = x_ref[...]
    mod = mod_ref[...]
    proj = _in_proj(x, (mod[0:1], mod[1:2]), w)
    u = proj[:, :LRU_WIDTH]
    gate = proj[:, LRU_WIDTH:2 * LRU_WIDTH]
    o2 = 2 * LRU_WIDTH

    rowid = lax.broadcasted_iota(jnp.int32, (ts, 1), 0)
    u_ext = jnp.concatenate([conv_c[...], u], axis=0)
    s1, s2, s3 = (pltpu.roll(u_ext, d, axis=0)[SUBLANES:] for d in (1, 2, 3))
    uc = _conv_taps(u, s1, s2, s3, w)
    conv_c[...] = u[ts - SUBLANES:]
    ulast_ref[...] = u[ts - SUBLANES:]

    first_pos = jnp.logical_and(rowid == 0, j == 0)
    a, bt = _lru_coeffs(uc, w, first_pos)
    a_cum, b_cum = _lin_scan(a, bt, ts, rowid)
    hs = b_cum + a_cum * h_c[0:1, :]
    h_tail = hs[ts - SUBLANES:]
    h_c[...] = jnp.broadcast_to(h_tail[SUBLANES - 1:SUBLANES, :], h_c.shape)
    hlast_ref[...] = h_tail
    lru_out = hs * jax.nn.gelu(gate)

    cos = cos_ref[...]
    sin = sin_ref[...]
    lane = lax.broadcasted_iota(jnp.int32, (1, LANES), 1)
    first_half = (lane % HEAD_DIM) < (HEAD_DIM // 2)
    qcols = [_rope128(proj[:, o2 + c * LANES:o2 + (c + 1) * LANES], cos, sin, first_half) for c in range(4)]
    k = _rope128(proj[:, o2 + Q_WIDTH:o2 + Q_WIDTH + KV_WIDTH], cos, sin, first_half)
    v = proj[:, o2 + Q_WIDTH + KV_WIDTH:]
    k_ext = jnp.concatenate([k_c[...], k], axis=0).astype(BF16)
    v_ext = jnp.concatenate([v_c[...], v], axis=0).astype(BF16)
    k_c[...] = k[ts - WINDOW:]
    v_c[...] = v[ts - WINDOW:]
    klast_ref[...] = k[ts - WINDOW:]
    vlast_ref[...] = v[ts - WINDOW:]

    qi = lax.broadcasted_iota(jnp.int32, (WINDOW, 2 * WINDOW), 0)
    kj = lax.broadcasted_iota(jnp.int32, (WINDOW, 2 * WINDOW), 1)
    band = jnp.logical_and(kj > qi, kj <= qi + WINDOW)
    lane_lo = lane < HEAD_DIM
    grow = lax.broadcasted_iota(jnp.int32, (GROUP * WINDOW, 1), 0) // WINDOW
    attn_cols = [[] for _ in range(4)]
    for blk in range(ts // WINDOW):
        prev_ok = jnp.logical_or(j > 0, blk > 0)
        mask = jnp.logical_and(band, jnp.logical_or(kj >= WINDOW, prev_ok))
        mask4 = jnp.concatenate([mask] * GROUP, axis=0)
        kb = k_ext[blk * WINDOW:(blk + 2) * WINDOW]
        vb = v_ext[blk * WINDOW:(blk + 2) * WINDOW]
        outs = []
        for kv in range(N_KV_HEADS):
            sel = lane_lo if kv == 0 else jnp.logical_not(lane_lo)
            qs = jnp.concatenate(
                [jnp.where(sel, qc[blk * WINDOW:(blk + 1) * WINDOW], 0.0) for qc in qcols], axis=0).astype(BF16)
            s = lax.dot_general(qs, kb, (((1,), (1,)), ((), ())), preferred_element_type=F32) * (HEAD_DIM ** -0.5)
            s = jnp.where(mask4, s, NEG_BIG)
            sink_col = jnp.zeros((GROUP * WINDOW, 1), F32)
            for g in range(GROUP):
                sink_col = jnp.where(grow == g, sinks_ref[kv * GROUP + g], sink_col)
            p, denom = _softmax_sink_pv(s, sink_col, vb)
            outs.append(jnp.dot(p.astype(BF16), vb, preferred_element_type=F32) / denom)
        for c in range(4):
            attn_cols[c].append(jnp.where(lane_lo, outs[0][c * WINDOW:(c + 1) * WINDOW],
                                          outs[1][c * WINDOW:(c + 1) * WINDOW]))
    attn = jnp.concatenate([jnp.concatenate(cols, axis=0) for cols in attn_cols], axis=1)

    mix = jnp.concatenate([lru_out, attn], axis=1)
    x1, h2, logits = _post_mix(x, mix, (mod[2:3], mod[3:4], mod[4:5]), w)
    x1_ref[...] = x1
    h2_ref[...] = h2
    route, tot = _route_tile(logits)
    lg_ref[0][...] = route
    lg_ref[1][...] = jnp.broadcast_to(tot, lg_ref[1].shape)


def _expand_rows(m, t):
    b, wd = m.shape
    return jnp.broadcast_to(m[:, None, :], (b, t, wd)).reshape(b * t, wd)


def _sample_body(x_ref, mod_ref, cos_ref, sin_ref, sinks_ref, h0_ref, cprev_ref, ck_ref, cv_ref, w,
                 x1_ref, h2_ref, lg_ref, g2_ref, hs_ref, u_ref, ko_ref, vo_ref):
    bt_, t = SAMPLE_BT, SUBLANES
    rows = bt_ * t

    x = x_ref[...]
    mods = [_expand_rows(mod_ref[i], t) for i in range(6)]
    proj = _in_proj(x, (mods[0], mods[1]), w)
    u = proj[:, :LRU_WIDTH]
    gate = proj[:, LRU_WIDTH:2 * LRU_WIDTH]
    o2 = 2 * LRU_WIDTH
    u_ref[...] = u

    rowid = lax.broadcasted_iota(jnp.int32, (rows, 1), 0) % t
    cprev = cprev_ref[...]
    taps = []
    for d in (1, 2, 3):
        taps.append(jnp.where(rowid >= d, pltpu.roll(u, d, axis=0),
                              pltpu.roll(cprev, (d - (CONV_W - 1)) % rows, axis=0)))
    uc = _conv_taps(u, taps[0], taps[1], taps[2], w)

    a, bt = _lru_coeffs(uc, w, None)
    bt = bt + a * h0_ref[...]
    _, hs = _lin_scan(a, bt, t, rowid)
    hs_ref[...] = hs
    lru_out = hs * jax.nn.gelu(gate)

    cos = cos_ref[...]
    sin = sin_ref[...]
    lane = lax.broadcasted_iota(jnp.int32, (1, LANES), 1)
    first_half = (lane % HEAD_DIM) < (HEAD_DIM // 2)
    qcols = [_rope128(proj[:, o2 + c * LANES:o2 + (c + 1) * LANES], cos, sin, first_half) for c in range(4)]
    k = _rope128(proj[:, o2 + Q_WIDTH:o2 + Q_WIDTH + KV_WIDTH], cos, sin, first_half)
    v = proj[:, o2 + Q_WIDTH + KV_WIDTH:]
    k3 = k.reshape(bt_, t, KV_WIDTH)
    v3 = v.reshape(bt_, t, KV_WIDTH)
    ck = ck_ref[...]
    cv = cv_ref[...]
    ko_ref[:, :WINDOW - t, :] = ck[:, t:, :]
    ko_ref[:, WINDOW - t:, :] = k3
    vo_ref[:, :WINDOW - t, :] = cv[:, t:, :]
    vo_ref[:, WINDOW - t:, :] = v3

    ckb, cvb, k3b, v3b = ck.astype(BF16), cv.astype(BF16), k3.astype(BF16), v3.astype(BF16)
    lane_lo = lane < HEAD_DIM
    gq = GROUP * t
    tq = lax.broadcasted_iota(jnp.int32, (1, gq, 1), 1) % t
    mask_c = lax.broadcasted_iota(jnp.int32, (1, gq, WINDOW), 2) > tq
    mask_n = lax.broadcasted_iota(jnp.int32, (1, gq, t), 2) <= tq
    grow = lax.broadcasted_iota(jnp.int32, (1, gq, 1), 1) // t
    bdims = (((2,), (2,)), ((0,), (0,)))
    pdims = (((2,), (1,)), ((0,), (0,)))
    outs = []
    for kv in range(N_KV_HEADS):
        sel = lane_lo if kv == 0 else jnp.logical_not(lane_lo)
        q3 = jnp.concatenate([jnp.where(sel, qc, 0.0).reshape(bt_, t, LANES) for qc in qcols], axis=1).astype(BF16)
        sc = lax.dot_general(q3, ckb, bdims, preferred_element_type=F32) * (HEAD_DIM ** -0.5)
        sn = lax.dot_general(q3, k3b, bdims, preferred_element_type=F32) * (HEAD_DIM ** -0.5)
        sc = jnp.where(mask_c, sc, NEG_BIG)
        sn = jnp.where(mask_n, sn, NEG_BIG)
        sink_col = jnp.zeros((1, gq, 1), F32)
        for g in range(GROUP):
            sink_col = jnp.where(grow == g, sinks_ref[kv * GROUP + g], sink_col)
        m = jnp.maximum(jnp.maximum(jnp.max(sc, axis=-1, keepdims=True), jnp.max(sn, axis=-1, keepdims=True)),
                        sink_col)
        pc = jnp.exp(sc - m)
        pn = jnp.exp(sn - m)
        denom = jnp.sum(pc, axis=-1, keepdims=True) + jnp.sum(pn, axis=-1, keepdims=True) + jnp.exp(sink_col - m)
        o = (lax.dot_general(pc.astype(BF16), cvb, pdims, preferred_element_type=F32)
             + lax.dot_general(pn.astype(BF16), v3b, pdims, preferred_element_type=F32)) / denom
        outs.append(o)
    attn = jnp.concatenate(
        [jnp.where(lane_lo, outs[0][:, c * t:(c + 1) * t, :], outs[1][:, c * t:(c + 1) * t, :]).reshape(rows, LANES)
         for c in range(4)], axis=1)

    mix = jnp.concatenate([lru_out, attn], axis=1)
    x1, h2, logits = _post_mix(x, mix, (mods[2], mods[3], mods[4]), w)
    x1_ref[...] = x1
    h2_ref[...] = h2
    route, tot = _route_tile(logits)
    lg_ref[0][...] = route
    lg_ref[1][...] = jnp.broadcast_to(tot, lg_ref[1].shape)
    g2_ref[...] = mods[5]


def _mixer_kernel(n_prompt_tiles, tiles_per_seq,
                  xp_ref, modp_ref, cosp_ref, sinp_ref, xs_ref, mods_ref, coss_ref, sins_ref, sinks_ref,
                  h0_ref, cprev_ref, ck_ref, cv_ref, *rest):
    nw = len(WEIGHT_NAMES)
    w = dict(zip(WEIGHT_NAMES, rest[:nw]))
    (x1_ref, h2_ref, route_ref, cnt_ref, hlast_ref, ulast_ref, klast_ref, vlast_ref,
     g2_ref, hs_ref, u_ref, ko_ref, vo_ref, conv_c, h_c, k_c, v_c) = rest[nw:]
    lg_ref = (route_ref, cnt_ref)
    i = pl.program_id(0)

    @pl.when(i < n_prompt_tiles)
    def _():
        _prompt_body(i % tiles_per_seq, xp_ref, modp_ref, cosp_ref, sinp_ref, sinks_ref, w, x1_ref, h2_ref, lg_ref,
                     hlast_ref, ulast_ref, klast_ref, vlast_ref, conv_c, h_c, k_c, v_c)

    @pl.when(i >= n_prompt_tiles)
    def _():
        _sample_body(xs_ref, mods_ref, coss_ref, sins_ref, sinks_ref, h0_ref, cprev_ref, ck_ref, cv_ref, w,
                     x1_ref, h2_ref, lg_ref, g2_ref, hs_ref, u_ref, ko_ref, vo_ref)


def _dispatch_kernel(meta_ref, dest_ref, h2_ref, xs_hbm, xbuf, zblk, sem, zsem):
    i = pl.program_id(0)
    nb = pl.num_programs(0)
    slot = i % 2
    tt = SEQ_TILE
    n_blocks = xs_hbm.shape[0] // MOE_TM

    xbuf[slot] = h2_ref[...]

    def body(t, carry):
        for kk in range(TOP_K):
            dst = dest_ref[0, 0, t * TOP_K + kk]
            pltpu.make_async_copy(xbuf.at[slot, pl.ds(t, 1)], xs_hbm.at[pl.ds(dst, 1)], sem.at[slot]).start()
        return carry
    lax.fori_loop(0, tt, body, 0, unroll=4)

    def wait_tile(s):
        for _ in range(TOP_K):
            pltpu.make_async_copy(xbuf.at[s], xs_hbm.at[pl.ds(0, tt)], sem.at[s]).wait()

    @pl.when(i > 0)
    def _():
        wait_tile(1 - slot)

    @pl.when(i == nb - 1)
    def _():
        wait_tile(slot)
        zblk[...] = jnp.zeros_like(zblk)

        def zero_row(r):
            return pltpu.make_async_copy(zblk.at[pl.ds(0, 1)], xs_hbm.at[pl.ds(r, 1)], zsem.at[0])

        def zero_block(j):
            return pltpu.make_async_copy(zblk, xs_hbm.at[pl.ds(pl.multiple_of(j * MOE_TM, MOE_TM), MOE_TM)], zsem.at[0])

        def for_tail_rows(fn):
            def e_body(e, carry):
                def r_body(r, c2):
                    fn(zero_row(r))
                    return c2
                return lax.fori_loop(meta_ref[e], meta_ref[N_EXPERTS + e], r_body, carry)
            lax.fori_loop(0, N_EXPERTS, e_body, 0)

        def for_tail_blocks(fn):
            def b_body(j, carry):
                fn(zero_block(j))
                return carry
            lax.fori_loop(meta_ref[2 * N_EXPERTS], n_blocks, b_body, 0)

        for_tail_rows(lambda cp: cp.start())
        for_tail_blocks(lambda cp: cp.start())
        for_tail_rows(lambda cp: cp.wait())
        for_tail_blocks(lambda cp: cp.wait())


def _dispatch(meta, dest_blocks, h2_all, n_rows):
    nt = dest_blocks.shape[0]
    tt = SEQ_TILE
    grid_spec = pltpu.PrefetchScalarGridSpec(
        num_scalar_prefetch=1,
        grid=(nt,),
        in_specs=[pl.BlockSpec((1, 1, TOP_K * tt), lambda i, m: (i, 0, 0), memory_space=pltpu.SMEM),
                  pl.BlockSpec((tt, D_MODEL), lambda i, m: (i, 0))],
        out_specs=pl.BlockSpec(memory_space=pl.ANY),
        scratch_shapes=[pltpu.VMEM((2, tt, D_MODEL), F32), pltpu.VMEM((MOE_TM, D_MODEL), F32),
                        pltpu.SemaphoreType.DMA((2,)), pltpu.SemaphoreType.DMA((1,))],
    )
    return pl.pallas_call(
        _dispatch_kernel,
        grid_spec=grid_spec,
        out_shape=jax.ShapeDtypeStruct((n_rows, D_MODEL), F32),
        compiler_params=pltpu.CompilerParams(dimension_semantics=("arbitrary",), vmem_limit_bytes=VMEM_LIMIT),
        name="dispatch",
    )(meta, dest_blocks, h2_all)


def _moe_kernel(blk_e_ref, meta_ref, x_ref, w1_ref, b1_ref, w2_ref, b2_ref, o_ref, w1b, w2b):
    i = pl.program_id(0)
    n_used = meta_ref[2 * N_EXPERTS]

    @pl.when(i < n_used)
    def _():
        e_cur = blk_e_ref[i]
        e_prev = blk_e_ref[jnp.maximum(i - 1, 0)]

        @pl.when(jnp.logical_or(i == 0, e_cur != e_prev))
        def _():
            chunk = 128
            def cast_body(c, carry):
                r0 = pl.multiple_of(c * chunk, chunk)
                w1b[pl.ds(r0, chunk), :] = w1_ref[pl.ds(r0, chunk), :].astype(BF16)
                w2b[pl.ds(r0, chunk), :] = w2_ref[pl.ds(r0, chunk), :].astype(BF16)
                return carry
            lax.fori_loop(0, D_MODEL // chunk, cast_body, 0)

        x = x_ref[...].astype(BF16)
        z = jnp.dot(x, w1b[...], preferred_element_type=F32) + b1_ref[...]
        glu = jnp.minimum(z[:, :D_FF], SWIGLU_LIMIT)
        lin = jnp.clip(z[:, D_FF:], -SWIGLU_LIMIT, SWIGLU_LIMIT)
        act = glu * jax.nn.sigmoid(SWIGLU_ALPHA * glu) * (lin + 1.0)
        o_ref[...] = jnp.dot(act.astype(BF16), w2b[...], preferred_element_type=F32) + b2_ref[...]

    @pl.when(i >= n_used)
    def _():
        o_ref[...] = jnp.zeros_like(o_ref)


def _moe(blk_e, meta, xs, w1, b1, w2, b2):
    nb = blk_e.shape[0]
    tm = MOE_TM
    used = lambda i, m: jnp.minimum(i, m[2 * N_EXPERTS] - 1)
    grid_spec = pltpu.PrefetchScalarGridSpec(
        num_scalar_prefetch=2,
        grid=(nb,),
        in_specs=[
            pl.BlockSpec((tm, D_MODEL), lambda i, be, m: (used(i, m), 0)),
            pl.BlockSpec((None, D_MODEL, 2 * D_FF), lambda i, be, m: (be[used(i, m)], 0, 0)),
            pl.BlockSpec((None, 1, 2 * D_FF), lambda i, be, m: (be[used(i, m)], 0, 0)),
            pl.BlockSpec((None, D_FF, D_MODEL), lambda i, be, m: (be[used(i, m)], 0, 0)),
            pl.BlockSpec((None, 1, D_MODEL), lambda i, be, m: (be[used(i, m)], 0, 0)),
        ],
        out_specs=pl.BlockSpec((tm, D_MODEL), lambda i, be, m: (i, 0)),
        scratch_shapes=[pltpu.VMEM((D_MODEL, 2 * D_FF), BF16), pltpu.VMEM((D_FF, D_MODEL), BF16)],
    )
    return pl.pallas_call(
        _moe_kernel,
        grid_spec=grid_spec,
        out_shape=jax.ShapeDtypeStruct((nb * tm, D_MODEL), F32),
        compiler_params=pltpu.CompilerParams(dimension_semantics=("arbitrary",), vmem_limit_bytes=VMEM_LIMIT),
        name="moe",
    )(blk_e, meta, xs, w1, b1, w2, b2)


def _combine_kernel(n_prompt_tiles, slotc_ref, slotn_ref, ys_hbm, route_ref, x1_ref, modp_ref, g2s_ref, fg_ref, o_ref,
                    buf, sem):
    i = pl.program_id(0)
    nb = pl.num_programs(0)
    slot = i % 2
    tt = COMB_TT

    def issue(slot_ref, s):
        def body(t, carry):
            for kk in range(TOP_K):
                src = slot_ref[0, 0, t * TOP_K + kk]
                pltpu.make_async_copy(ys_hbm.at[pl.ds(src, 1)], buf.at[s, pl.ds(kk * tt + t, 1)], sem.at[s]).start()
            return carry
        lax.fori_loop(0, tt, body, 0, unroll=4)

    @pl.when(i == 0)
    def _():
        issue(slotc_ref, 0)

    @pl.when(i + 1 < nb)
    def _():
        issue(slotn_ref, 1 - slot)

    pltpu.make_async_copy(ys_hbm.at[pl.ds(0, TOP_K * tt)], buf.at[slot], sem.at[slot]).wait()

    route = route_ref[...]
    ff = route[:, ROUTE_G:ROUTE_G + 1] * buf[slot, 0:tt, :]
    for kk in range(1, TOP_K):
        ff = ff + route[:, ROUTE_G + kk:ROUTE_G + kk + 1] * buf[slot, kk * tt:(kk + 1) * tt, :]
    g2 = jnp.where(i < n_prompt_tiles, modp_ref[5:6, :], g2s_ref[...])
    x = x1_ref[...] + g2 * ff
    o_ref[...] = _rms(x, fg_ref[...])


def _combine(slot_blocks, ys, route, x1_all, mod_p, g2_rows, final_g, n_prompt_tiles, tiles_per_batch):
    nt = slot_blocks.shape[0]
    tt = COMB_TT
    npt = n_prompt_tiles
    return pl.pallas_call(
        functools.partial(_combine_kernel, npt),
        grid=(nt,),
        in_specs=[
            pl.BlockSpec((1, 1, TOP_K * tt), lambda i: (i, 0, 0), memory_space=pltpu.SMEM),
            pl.BlockSpec((1, 1, TOP_K * tt), lambda i: (jnp.minimum(i + 1, nt - 1), 0, 0), memory_space=pltpu.SMEM),
            pl.BlockSpec(memory_space=pl.ANY),
            pl.BlockSpec((tt, LANES), lambda i: (i, 0)),
            pl.BlockSpec((tt, D_MODEL), lambda i: (i, 0)),
            pl.BlockSpec((None, 6, D_MODEL), lambda i: (jnp.minimum(i, npt - 1) // tiles_per_batch, 0, 0)),
            pl.BlockSpec((tt, D_MODEL), lambda i: (jnp.maximum(i - npt, 0), 0)),
            pl.BlockSpec((1, D_MODEL), lambda i: (0, 0)),
        ],
        out_specs=pl.BlockSpec((tt, D_MODEL), lambda i: (i, 0)),
        out_shape=jax.ShapeDtypeStruct((nt * tt, D_MODEL), F32),
        scratch_shapes=[pltpu.VMEM((2, TOP_K * tt, D_MODEL), F32), pltpu.SemaphoreType.DMA((2,))],
        compiler_params=pltpu.CompilerParams(dimension_semantics=("arbitrary",), vmem_limit_bytes=VMEM_LIMIT),
        name="combine",
    )(slot_blocks, slot_blocks, ys, route, x1_all, mod_p, g2_rows, final_g)


def _block_diag_halves(wg):
    halves = []
    for hh in range(2):
        rows = []
        for bi in range(4):
            row = [wg[hh * 4 + bi] if bj == bi else jnp.zeros((LRU_BLOCK_W, LRU_BLOCK_W), wg.dtype) for bj in range(4)]
            rows.append(jnp.concatenate(row, axis=1))
        halves.append(jnp.concatenate(rows, axis=0))
    return jnp.stack(halves).astype(BF16)


def _rope_tables(pos):
    half = HEAD_DIM // 2
    inv = ROPE_THETA ** (-jnp.arange(half, dtype=F32) / half)
    ang = pos.astype(F32)[:, None] * inv[None, :]
    cos = jnp.cos(ang)
    sin = jnp.sin(ang)
    cos128 = jnp.concatenate([cos, cos, cos, cos], axis=1)
    sin128 = jnp.concatenate([-sin, sin, -sin, sin], axis=1)
    return cos128, sin128


def _full_spec(arr, grid_rank):
    zeros = (0,) * arr.ndim
    if grid_rank == 1:
        return pl.BlockSpec(arr.shape, lambda i: zeros)
    return pl.BlockSpec(arr.shape, lambda b, j: zeros)


def kernel(x_prompt, x_sample, state_lru_h, state_conv, cache_win_k, cache_win_v, c_prompt, c_sample, w_ada, b_ada, norm1_g, w_in, b_in, conv_w, conv_b, lru_wa, lru_ba, lru_wx, lru_bx, lru_lambda, attn_sinks, w_out, b_out, norm2_g, w_router, b_router, w1, b1, w2, b2, final_g):
    bp, seq, _ = x_prompt.shape
    bd, tdec, _ = x_sample.shape
    assert tdec == SUBLANES and seq % SEQ_TILE == 0 and bd % SAMPLE_BT == 0
    n_prompt = bp * seq
    n_sample = bd * tdec
    n_tok = n_prompt + n_sample
    l = 0

    head_perm = jnp.array([h for c in range(4) for h in (c, GROUP + c)], dtype=jnp.int32)
    qcol_perm = (head_perm[:, None] * HEAD_DIM + jnp.arange(HEAD_DIM, dtype=jnp.int32)[None, :]).reshape(-1)
    o2 = 2 * LRU_WIDTH
    in_perm = jnp.concatenate([jnp.arange(o2, dtype=jnp.int32), o2 + qcol_perm,
                               jnp.arange(o2 + Q_WIDTH, IN_WIDTH, dtype=jnp.int32)])
    out_perm = jnp.concatenate([jnp.arange(LRU_WIDTH, dtype=jnp.int32), LRU_WIDTH + qcol_perm])
    sinks_perm = attn_sinks[l]

    wr = jnp.tile(w_router[l], (1, TOP_K))
    wr_hi = wr.astype(BF16)
    weights = dict(
        norm1_g=norm1_g[l][None, :], w_in=w_in[l][:, in_perm].astype(BF16), b_in=b_in[l][in_perm][None, :],
        conv_w=conv_w[l], conv_b=conv_b[l][None, :],
        ga=_block_diag_halves(lru_wa[l]), gx=_block_diag_halves(lru_wx[l]),
        lru_ba=lru_ba[l][None, :], lru_bx=lru_bx[l][None, :], lam=lru_lambda[l][None, :],
        w_out=w_out[l][out_perm, :].astype(BF16), b_out=b_out[l][None, :], norm2_g=norm2_g[l][None, :],
        wr_hi=wr_hi, wr_lo=(wr - wr_hi.astype(F32)).astype(BF16),
        b_router=jnp.tile(b_router[l], TOP_K)[None, :],
    )
    wlist = [weights[n] for n in WEIGHT_NAMES]

    mod_all = _ada(jnp.concatenate([c_prompt, c_sample], axis=0), w_ada[l], b_ada[l][None, :])
    mod_p = mod_all[:bp].reshape(bp, 6, D_MODEL)
    mod_s = mod_all[bp:].reshape(bd, 6, D_MODEL).transpose(1, 0, 2)

    cos_p, sin_p = _rope_tables(jnp.arange(seq, dtype=jnp.int32))
    cos_s, sin_s = _rope_tables(PAST_LEN + jnp.arange(tdec, dtype=jnp.int32))
    cos_s = jnp.tile(cos_s, (SAMPLE_BT, 1))
    sin_s = jnp.tile(sin_s, (SAMPLE_BT, 1))
    h0_rows = jnp.pad(state_lru_h[l][:, None, :], ((0, 0), (0, tdec - 1), (0, 0))).reshape(n_sample, LRU_WIDTH)
    cprev_rows = jnp.pad(state_conv[l], ((0, 0), (0, tdec - (CONV_W - 1)), (0, 0))).reshape(n_sample, LRU_WIDTH)
    ck = cache_win_k[l].reshape(bd, WINDOW, KV_WIDTH)
    cv = cache_win_v[l].reshape(bd, WINDOW, KV_WIDTH)
    nj = seq // SEQ_TILE
    npt = n_prompt // SEQ_TILE
    nst = n_sample // SEQ_TILE
    p_tile = lambda i: jnp.minimum(i, npt - 1)
    s_tile = lambda i: jnp.maximum(i - npt, 0)
    all_rows = lambda width: pl.BlockSpec((SEQ_TILE, width), lambda i: (i, 0))
    p_rows = lambda width: pl.BlockSpec((SEQ_TILE, width), lambda i: (p_tile(i), 0))
    s_rows = lambda width: pl.BlockSpec((SEQ_TILE, width), lambda i: (s_tile(i), 0))
    p_tail = lambda rows, width: pl.BlockSpec((None, rows, width), lambda i: (p_tile(i) // nj, 0, 0))
    cache_spec = pl.BlockSpec((SAMPLE_BT, WINDOW, KV_WIDTH), lambda i: (s_tile(i), 0, 0))
    mixer_out_shapes = (
        jax.ShapeDtypeStruct((n_tok, D_MODEL), F32),
        jax.ShapeDtypeStruct((n_tok, D_MODEL), F32),
        jax.ShapeDtypeStruct((n_tok, LANES), F32),
        jax.ShapeDtypeStruct((npt + nst, SUBLANES, LANES), F32),
        jax.ShapeDtypeStruct((bp, SUBLANES, LRU_WIDTH), F32),
        jax.ShapeDtypeStruct((bp, SUBLANES, LRU_WIDTH), F32),
        jax.ShapeDtypeStruct((bp, WINDOW, KV_WIDTH), F32),
        jax.ShapeDtypeStruct((bp, WINDOW, KV_WIDTH), F32),
        jax.ShapeDtypeStruct((n_sample, D_MODEL), F32),
        jax.ShapeDtypeStruct((n_sample, LRU_WIDTH), F32),
        jax.ShapeDtypeStruct((n_sample, LRU_WIDTH), F32),
        jax.ShapeDtypeStruct((bd, WINDOW, KV_WIDTH), F32),
        jax.ShapeDtypeStruct((bd, WINDOW, KV_WIDTH), F32),
    )
    (x1_all, h2_all, route, tile_cnt, hlast_p, ulast_p, klast_p, vlast_p, g2_rows, hs_s, u_s, s_k, s_v) = pl.pallas_call(
        functools.partial(_mixer_kernel, npt, nj),
        grid=(npt + nst,),
        in_specs=[p_rows(D_MODEL),
                  pl.BlockSpec((None, 6, D_MODEL), lambda i: (p_tile(i) // nj, 0, 0)),
                  pl.BlockSpec((SEQ_TILE, LANES), lambda i: (p_tile(i) % nj, 0)),
                  pl.BlockSpec((SEQ_TILE, LANES), lambda i: (p_tile(i) % nj, 0)),
                  s_rows(D_MODEL),
                  pl.BlockSpec((6, SAMPLE_BT, D_MODEL), lambda i: (0, s_tile(i), 0)),
                  pl.BlockSpec((SEQ_TILE, LANES), lambda i: (0, 0)),
                  pl.BlockSpec((SEQ_TILE, LANES), lambda i: (0, 0)),
                  pl.BlockSpec(memory_space=pltpu.SMEM),
                  s_rows(LRU_WIDTH), s_rows(LRU_WIDTH), cache_spec, cache_spec]
                 + [_full_spec(a, 1) for a in wlist],
        out_specs=(all_rows(D_MODEL), all_rows(D_MODEL), all_rows(LANES),
                   pl.BlockSpec((None, SUBLANES, LANES), lambda i: (i, 0, 0)),
                   p_tail(SUBLANES, LRU_WIDTH), p_tail(SUBLANES, LRU_WIDTH),
                   p_tail(WINDOW, KV_WIDTH), p_tail(WINDOW, KV_WIDTH),
                   s_rows(D_MODEL), s_rows(LRU_WIDTH), s_rows(LRU_WIDTH), cache_spec, cache_spec),
        out_shape=mixer_out_shapes,
        scratch_shapes=[pltpu.VMEM((SUBLANES, LRU_WIDTH), F32), pltpu.VMEM((SUBLANES, LRU_WIDTH), F32),
                        pltpu.VMEM((WINDOW, KV_WIDTH), F32), pltpu.VMEM((WINDOW, KV_WIDTH), F32)],
        compiler_params=pltpu.CompilerParams(dimension_semantics=("arbitrary",), vmem_limit_bytes=VMEM_LIMIT),
        name="mixer",
    )(x_prompt.reshape(n_prompt, D_MODEL), mod_p, cos_p, sin_p,
      x_sample.reshape(n_sample, D_MODEL), mod_s, cos_s, sin_s, sinks_perm, h0_rows, cprev_rows, ck, cv, *wlist)

    n_tiles = npt + nst
    n_assign = n_tok * TOP_K
    n_blocks = -(-(n_assign + N_EXPERTS * (MOE_TM - 1)) // MOE_TM)
    cnt = tile_cnt[:, 0, :N_EXPERTS].astype(jnp.int32)
    counts = jnp.sum(cnt, axis=0)
    pcounts = (counts + MOE_TM - 1) // MOE_TM * MOE_TM
    pend = jnp.cumsum(pcounts)
    pstart = pend - pcounts
    tile_off = pstart[None, :] + jnp.cumsum(cnt, axis=0) - cnt
    e_ids = route[:, ROUTE_E:ROUTE_E + TOP_K].astype(jnp.int32).reshape(n_tiles, SEQ_TILE, TOP_K)
    ranks = route[:, ROUTE_R:ROUTE_R + TOP_K].astype(jnp.int32).reshape(n_tiles, SEQ_TILE, TOP_K)
    hit = e_ids[..., None] == jnp.arange(N_EXPERTS, dtype=jnp.int32)
    dest = jnp.sum(jnp.where(hit, tile_off[:, None, None, :], 0), axis=-1) + ranks
    dest_blocks = dest.reshape(n_tiles, 1, SEQ_TILE * TOP_K)
    blk_e = jnp.minimum(jnp.searchsorted(pend, jnp.arange(n_blocks) * MOE_TM, side='right'),
                        N_EXPERTS - 1).astype(jnp.int32)
    meta = jnp.concatenate([pstart + counts, pend, pend[-1:] // MOE_TM]).astype(jnp.int32)

    xs = _dispatch(meta, dest_blocks, h2_all, n_blocks * MOE_TM)
    ys = _moe(blk_e, meta, xs, w1[l], b1[l][:, None, :], w2[l], b2[l][:, None, :])
    y_all = _combine(dest_blocks, ys, route, x1_all, mod_p, g2_rows,
                     final_g[None, :], n_prompt // COMB_TT, seq // COMB_TT)

    y_prompt = y_all[:n_prompt].reshape(bp, seq, D_MODEL)
    y_sample = y_all[n_prompt:].reshape(bd, tdec, D_MODEL)
    p_h = hlast_p[:, SUBLANES - 1, :][None]
    p_c = ulast_p[:, SUBLANES - (CONV_W - 1):, :][None]
    p_k = klast_p.reshape(1, bp, WINDOW, N_KV_HEADS, HEAD_DIM)
    p_v = vlast_p.reshape(1, bp, WINDOW, N_KV_HEADS, HEAD_DIM)
    s_h = hs_s.reshape(bd, tdec, LRU_WIDTH)[:, tdec - 1, :][None]
    s_c = u_s.reshape(bd, tdec, LRU_WIDTH)[:, tdec - (CONV_W - 1):, :][None]
    s_kk = s_k.reshape(1, bd, WINDOW, N_KV_HEADS, HEAD_DIM)
    s_vv = s_v.reshape(1, bd, WINDOW, N_KV_HEADS, HEAD_DIM)
    return (y_prompt, y_sample, p_h, p_c, p_k, p_v, s_h, s_c, s_kk, s_vv)
```

```python
import functools

import jax
import jax.numpy as jnp
from jax import lax
from jax.experimental import pallas as pl
from jax.experimental.pallas import tpu as pltpu

F32 = jnp.float32
BF16 = jnp.bfloat16

D_MODEL = 1024
LRU_WIDTH = 512
LRU_BLOCKS = 8
LRU_BLOCK_W = LRU_WIDTH // LRU_BLOCKS
CONV_W = 4
LRU_C = 8.0
HEAD_DIM = 64
N_HEADS = 8
N_KV_HEADS = 2
GROUP = N_HEADS // N_KV_HEADS
WINDOW = 128
ROPE_THETA = 10000.0
N_EXPERTS = 32
TOP_K = 4
D_FF = D_MODEL
SWIGLU_LIMIT = 7.0
SWIGLU_ALPHA = 1.702
NORM_EPS = 1e-5
PAST_LEN = 8192
Q_WIDTH = N_HEADS * HEAD_DIM
KV_WIDTH = N_KV_HEADS * HEAD_DIM
IN_WIDTH = 2 * LRU_WIDTH + Q_WIDTH + 2 * KV_WIDTH

LANES = 128
SUBLANES = 8
SEQ_TILE = 256
SAMPLE_BT = 32
MOE_TM = 256
COMB_TT = 256
NEG_BIG = -1e30
VMEM_LIMIT = 56 * 1024 * 1024


def _rms(x, g):
    return x * lax.rsqrt(jnp.mean(x * x, axis=-1, keepdims=True) + NORM_EPS) * g


def _shift_rows(x, d, fill, rowid):
    return jnp.where(rowid >= d, pltpu.roll(x, d, axis=0), fill)


def _lin_scan(a, b, seg, rowid):
    d = 1
    while d < seg:
        a_s = _shift_rows(a, d, 1.0, rowid)
        b_s = _shift_rows(b, d, 0.0, rowid)
        b = a * b_s + b
        a = a * a_s
        d *= 2
    return a, b


def _rope128(x, cos, sin_signed, first_half):
    sw = jnp.where(first_half, pltpu.roll(x, LANES - HEAD_DIM // 2, axis=1), pltpu.roll(x, HEAD_DIM // 2, axis=1))
    return x * cos + sw * sin_signed


def _neg_expm1(y):
    return -(jnp.tanh(0.5 * y) * (jnp.exp(y) + 1.0))


def _softplus(x):
    return jnp.maximum(x, 0.0) + jnp.log1p(jnp.exp(-jnp.abs(x)))


def _lru_coeffs(uc, w, first_pos_mask):
    ub = uc.astype(BF16)
    half = LRU_WIDTH // 2
    ra = jnp.concatenate([jnp.dot(ub[:, :half], w['ga'][0], preferred_element_type=F32),
                          jnp.dot(ub[:, half:], w['ga'][1], preferred_element_type=F32)], axis=1)
    rx = jnp.concatenate([jnp.dot(ub[:, :half], w['gx'][0], preferred_element_type=F32),
                          jnp.dot(ub[:, half:], w['gx'][1], preferred_element_type=F32)], axis=1)
    r = jax.nn.sigmoid(ra + w['lru_ba'][...])
    i = jax.nn.sigmoid(rx + w['lru_bx'][...])
    log_a = -LRU_C * r * _softplus(-w['lam'][...])
    a = jnp.exp(log_a)
    mult = jnp.sqrt(_neg_expm1(2.0 * log_a))
    if first_pos_mask is not None:
        mult = jnp.where(first_pos_mask, 1.0, mult)
    return a, mult * i * uc


def _conv_taps(u, s1, s2, s3, w):
    cw = w['conv_w']
    return w['conv_b'][...] + s3 * cw[0:1, :] + s2 * cw[1:2, :] + s1 * cw[2:3, :] + u * cw[3:4, :]


def _in_proj(x, mod, w):
    sh1, sc1 = mod
    h = _rms(x, w['norm1_g'][...]) * (1.0 + sc1) + sh1
    return jnp.dot(h.astype(BF16), w['w_in'][...], preferred_element_type=F32) + w['b_in'][...]


def _post_mix(x, mix, mod, w):
    g1, sh2, sc2 = mod
    x1 = x + g1 * (jnp.dot(mix.astype(BF16), w['w_out'][...], preferred_element_type=F32) + w['b_out'][...])
    h2 = _rms(x1, w['norm2_g'][...]) * (1.0 + sc2) + sh2
    h2_hi = h2.astype(BF16)
    h2_lo = (h2 - h2_hi.astype(F32)).astype(BF16)
    logits = (jnp.dot(h2_hi, w['wr_hi'][...], preferred_element_type=F32)
              + jnp.dot(h2_lo, w['wr_hi'][...], preferred_element_type=F32)
              + jnp.dot(h2_hi, w['wr_lo'][...], preferred_element_type=F32)) + w['b_router'][...]
    return x1, h2, logits


ROUTE_E, ROUTE_G, ROUTE_R = 0, TOP_K, 2 * TOP_K


def _lane_roll1(v, shift):
    return pltpu.roll(jnp.broadcast_to(v, (SUBLANES, LANES)), shift, axis=1)[0:1]


def _route_tile(lg):
    rows = lg.shape[0]
    lane = lax.broadcasted_iota(jnp.int32, (1, LANES), 1)
    e_of = lane % N_EXPERTS
    grp = lane // N_EXPERTS
    onehot = jnp.zeros((rows, LANES), F32)
    vals, ids = [], []
    for k in range(TOP_K):
        m = jnp.max(lg, axis=1, keepdims=True)
        idx = jnp.min(jnp.where(lg == m, e_of, N_EXPERTS), axis=1, keepdims=True)
        sel = e_of == idx
        lg = jnp.where(sel, -jnp.inf, lg)
        onehot = jnp.where(jnp.logical_and(sel, grp == k), 1.0, onehot)
        vals.append(m)
        ids.append(idx)
    ex = [jnp.exp(v - vals[0]) for v in vals]
    denom = ex[0] + ex[1] + ex[2] + ex[3]

    r_i = lax.broadcasted_iota(jnp.int32, (rows, rows), 0)
    c_i = lax.broadcasted_iota(jnp.int32, (rows, rows), 1)
    strict_lower = jnp.where(r_i > c_i, 1.0, 0.0).astype(BF16)
    prefix = jnp.dot(strict_lower, onehot.astype(BF16), preferred_element_type=F32)
    cnt = jnp.sum(onehot, axis=0, keepdims=True)
    base = jnp.zeros((1, LANES), F32)
    tot = cnt
    for s in range(1, TOP_K):
        rolled = _lane_roll1(cnt, s * N_EXPERTS)
        base = base + jnp.where(lane >= s * N_EXPERTS, rolled, 0.0)
        tot = tot + rolled
    ranked = onehot * (prefix + base)

    route = jnp.zeros((rows, LANES), F32)
    for k in range(TOP_K):
        rank_k = jnp.sum(jnp.where(grp == k, ranked, 0.0), axis=1, keepdims=True)
        route = jnp.where(lane == ROUTE_E + k, ids[k].astype(F32), route)
        route = jnp.where(lane == ROUTE_G + k, ex[k] / denom, route)
        route = jnp.where(lane == ROUTE_R + k, rank_k, route)
    return route, tot


def _softmax_sink_pv(s, sink_col, v_bf16):
    m = jnp.maximum(jnp.max(s, axis=-1, keepdims=True), sink_col)
    p = jnp.exp(s - m)
    denom = jnp.sum(p, axis=-1, keepdims=True) + jnp.exp(sink_col - m)
    return p, denom


WEIGHT_NAMES = ('norm1_g', 'w_in', 'b_in', 'conv_w', 'conv_b', 'ga', 'gx', 'lru_ba', 'lru_bx', 'lam',
                'w_out', 'b_out', 'norm2_g', 'wr_hi', 'wr_lo', 'b_router')


def _ada_kernel(c_ref, w_ref, b_ref, o_ref):
    c = c_ref[...]
    s = c * jax.nn.sigmoid(c)
    o_ref[...] = jnp.dot(s, w_ref[...], preferred_element_type=F32, precision=lax.Precision.HIGHEST) + b_ref[...]


def _ada(c_all, w_ada, b_ada):
    rows = c_all.shape[0]
    return pl.pallas_call(
        _ada_kernel,
        grid=(6,),
        in_specs=[pl.BlockSpec((rows, D_MODEL), lambda i: (0, 0)),
                  pl.BlockSpec((D_MODEL, D_MODEL), lambda i: (0, i)),
                  pl.BlockSpec((1, D_MODEL), lambda i: (0, i))],
        out_specs=pl.BlockSpec((rows, D_MODEL), lambda i: (0, i)),
        out_shape=jax.ShapeDtypeStruct((rows, 6 * D_MODEL), F32),
        compiler_params=pltpu.CompilerParams(dimension_semantics=("arbitrary",), vmem_limit_bytes=VMEM_LIMIT),
        name="ada",
    )(c_all, w_ada, b_ada)


def _prompt_body(j, x_ref, mod_ref, cos_ref, sin_ref, sinks_ref, w, x1_ref, h2_ref, lg_ref,
                 hlast_ref, ulast_ref, klast_ref, vlast_ref, conv_c, h_c, k_c, v_c):
    ts = SEQ_TILE

    @pl.when(j == 0)
    def _():
        conv_c[...] = jnp.zeros_like(conv_c)
        h_c[...] = jnp.zeros_like(h_c)
        k_c[...] = jnp.zeros_like(k_c)
        v_c[...] = jnp.zeros_like(v_c)

    x = x_ref[...]
    mod = mod_ref[...]
    proj = _in_proj(x, (mod[0:1], mod[1:2]), w)
    u = proj[:, :LRU_WIDTH]
    gate = proj[:, LRU_WIDTH:2 * LRU_WIDTH]
    o2 = 2 * LRU_WIDTH

    rowid = lax.broadcasted_iota(jnp.int32, (ts, 1), 0)
    u_ext = jnp.concatenate([conv_c[...], u], axis=0)
    s1, s2, s3 = (pltpu.roll(u_ext, d, axis=0)[SUBLANES:] for d in (1, 2, 3))
    uc = _conv_taps(u, s1, s2, s3, w)
    conv_c[...] = u[ts - SUBLANES:]
    ulast_ref[...] = u[ts - SUBLANES:]

    first_pos = jnp.logical_and(rowid == 0, j == 0)
    a, bt = _lru_coeffs(uc, w, first_pos)
    a_cum, b_cum = _lin_scan(a, bt, ts, rowid)
    hs = b_cum + a_cum * h_c[0:1, :]
    h_tail = hs[ts - SUBLANES:]
    h_c[...] = jnp.broadcast_to(h_tail[SUBLANES - 1:SUBLANES, :], h_c.shape)
    hlast_ref[...] = h_tail
    lru_out = hs * jax.nn.gelu(gate)

    cos = cos_ref[...]
    sin = sin_ref[...]
    lane = lax.broadcasted_iota(jnp.int32, (1, LANES), 1)
    first_half = (lane % HEAD_DIM) < (HEAD_DIM // 2)
    qcols = [_rope128(proj[:, o2 + c * LANES:o2 + (c + 1) * LANES], cos, sin, first_half) for c in range(4)]
    k = _rope128(proj[:, o2 + Q_WIDTH:o2 + Q_WIDTH + KV_WIDTH], cos, sin, first_half)
    v = proj[:, o2 + Q_WIDTH + KV_WIDTH:]
    k_ext = jnp.concatenate([k_c[...], k], axis=0).astype(BF16)
    v_ext = jnp.concatenate([v_c[...], v], axis=0).astype(BF16)
    k_c[...] = k[ts - WINDOW:]
    v_c[...] = v[ts - WINDOW:]
    klast_ref[...] = k[ts - WINDOW:]
    vlast_ref[...] = v[ts - WINDOW:]

    qi = lax.broadcasted_iota(jnp.int32, (WINDOW, 2 * WINDOW), 0)
    kj = lax.broadcasted_iota(jnp.int32, (WINDOW, 2 * WINDOW), 1)
    band = jnp.logical_and(kj > qi, kj <= qi + WINDOW)
    lane_lo = lane < HEAD_DIM
    grow = lax.broadcasted_iota(jnp.int32, (GROUP * WINDOW, 1), 0) // WINDOW
    attn_cols = [[] for _ in range(4)]
    for blk in range(ts // WINDOW):
        prev_ok = jnp.logical_or(j > 0, blk > 0)
        mask = jnp.logical_and(band, jnp.logical_or(kj >= WINDOW, prev_ok))
        mask4 = jnp.concatenate([mask] * GROUP, axis=0)
        kb = k_ext[blk * WINDOW:(blk + 2) * WINDOW]
        vb = v_ext[blk * WINDOW:(blk + 2) * WINDOW]
        outs = []
        for kv in range(N_KV_HEADS):
            sel = lane_lo if kv == 0 else jnp.logical_not(lane_lo)
            qs = jnp.concatenate(
                [jnp.where(sel, qc[blk * WINDOW:(blk + 1) * WINDOW], 0.0) for qc in qcols], axis=0).astype(BF16)
            s = lax.dot_general(qs, kb, (((1,), (1,)), ((), ())), preferred_element_type=F32) * (HEAD_DIM ** -0.5)
            s = jnp.where(mask4, s, NEG_BIG)
            sink_col = jnp.zeros((GROUP * WINDOW, 1), F32)
            for g in range(GROUP):
                sink_col = jnp.where(grow == g, sinks_ref[kv * GROUP + g], sink_col)
            p, denom = _softmax_sink_pv(s, sink_col, vb)
            outs.append(jnp.dot(p.astype(BF16), vb, preferred_element_type=F32) / denom)
        for c in range(4):
            attn_cols[c].append(jnp.where(lane_lo, outs[0][c * WINDOW:(c + 1) * WINDOW],
                                          outs[1][c * WINDOW:(c + 1) * WINDOW]))
    attn = jnp.concatenate([jnp.concatenate(cols, axis=0) for cols in attn_cols], axis=1)

    mix = jnp.concatenate([lru_out, attn], axis=1)
    x1, h2, logits = _post_mix(x, mix, (mod[2:3], mod[3:4], mod[4:5]), w)
    x1_ref[...] = x1
    h2_ref[...] = h2
    route, tot = _route_tile(logits)
    lg_ref[0][...] = route
    lg_ref[1][...] = jnp.broadcast_to(tot, lg_ref[1].shape)


def _expand_rows(m, t):
    b, wd = m.shape
    return jnp.broadcast_to(m[:, None, :], (b, t, wd)).reshape(b * t, wd)


def _sample_body(x_ref, mod_ref, cos_ref, sin_ref, sinks_ref, h0_ref, cprev_ref, ck_ref, cv_ref, w,
                 x1_ref, h2_ref, lg_ref, g2_ref, hs_ref, u_ref, ko_ref, vo_ref):
    bt_, t = SAMPLE_BT, SUBLANES
    rows = bt_ * t

    x = x_ref[...]
    mods = [_expand_rows(mod_ref[i], t) for i in range(6)]
    proj = _in_proj(x, (mods[0], mods[1]), w)
    u = proj[:, :LRU_WIDTH]
    gate = proj[:, LRU_WIDTH:2 * LRU_WIDTH]
    o2 = 2 * LRU_WIDTH
    u_ref[...] = u

    rowid = lax.broadcasted_iota(jnp.int32, (rows, 1), 0) % t
    cprev = cprev_ref[...]
    taps = []
    for d in (1, 2, 3):
        taps.append(jnp.where(rowid >= d, pltpu.roll(u, d, axis=0),
                              pltpu.roll(cprev, (d - (CONV_W - 1)) % rows, axis=0)))
    uc = _conv_taps(u, taps[0], taps[1], taps[2], w)

    a, bt = _lru_coeffs(uc, w, None)
    bt = bt + a * h0_ref[...]
    _, hs = _lin_scan(a, bt, t, rowid)
    hs_ref[...] = hs
    lru_out = hs * jax.nn.gelu(gate)

    cos = cos_ref[...]
    sin = sin_ref[...]
    lane = lax.broadcasted_iota(jnp.int32, (1, LANES), 1)
    first_half = (lane % HEAD_DIM) < (HEAD_DIM // 2)
    qcols = [_rope128(proj[:, o2 + c * LANES:o2 + (c + 1) * LANES], cos, sin, first_half) for c in range(4)]
    k = _rope128(proj[:, o2 + Q_WIDTH:o2 + Q_WIDTH + KV_WIDTH], cos, sin, first_half)
    v = proj[:, o2 + Q_WIDTH + KV_WIDTH:]
    k3 = k.reshape(bt_, t, KV_WIDTH)
    v3 = v.reshape(bt_, t, KV_WIDTH)
    ck = ck_ref[...]
    cv = cv_ref[...]
    ko_ref[:, :WINDOW - t, :] = ck[:, t:, :]
    ko_ref[:, WINDOW - t:, :] = k3
    vo_ref[:, :WINDOW - t, :] = cv[:, t:, :]
    vo_ref[:, WINDOW - t:, :] = v3

    ckb, cvb, k3b, v3b = ck.astype(BF16), cv.astype(BF16), k3.astype(BF16), v3.astype(BF16)
    lane_lo = lane < HEAD_DIM
    gq = GROUP * t
    tq = lax.broadcasted_iota(jnp.int32, (1, gq, 1), 1) % t
    mask_c = lax.broadcasted_iota(jnp.int32, (1, gq, WINDOW), 2) > tq
    mask_n = lax.broadcasted_iota(jnp.int32, (1, gq, t), 2) <= tq
    grow = lax.broadcasted_iota(jnp.int32, (1, gq, 1), 1) // t
    bdims = (((2,), (2,)), ((0,), (0,)))
    pdims = (((2,), (1,)), ((0,), (0,)))
    outs = []
    for kv in range(N_KV_HEADS):
        sel = lane_lo if kv == 0 else jnp.logical_not(lane_lo)
        q3 = jnp.concatenate([jnp.where(sel, qc, 0.0).reshape(bt_, t, LANES) for qc in qcols], axis=1).astype(BF16)
        sc = lax.dot_general(q3, ckb, bdims, preferred_element_type=F32) * (HEAD_DIM ** -0.5)
        sn = lax.dot_general(q3, k3b, bdims, preferred_element_type=F32) * (HEAD_DIM ** -0.5)
        sc = jnp.where(mask_c, sc, NEG_BIG)
        sn = jnp.where(mask_n, sn, NEG_BIG)
        sink_col = jnp.zeros((1, gq, 1), F32)
        for g in range(GROUP):
            sink_col = jnp.where(grow == g, sinks_ref[kv * GROUP + g], sink_col)
        m = jnp.maximum(jnp.maximum(jnp.max(sc, axis=-1, keepdims=True), jnp.max(sn, axis=-1, keepdims=True)),
                        sink_col)
        pc = jnp.exp(sc - m)
        pn = jnp.exp(sn - m)
        denom = jnp.sum(pc, axis=-1, keepdims=True) + jnp.sum(pn, axis=-1, keepdims=True) + jnp.exp(sink_col - m)
        o = (lax.dot_general(pc.astype(BF16), cvb, pdims, preferred_element_type=F32)
             + lax.dot_general(pn.astype(BF16), v3b, pdims, preferred_element_type=F32)) / denom
        outs.append(o)
    attn = jnp.concatenate(
        [jnp.where(lane_lo, outs[0][:, c * t:(c + 1) * t, :], outs[1][:, c * t:(c + 1) * t, :]).reshape(rows, LANES)
         for c in range(4)], axis=1)

    mix = jnp.concatenate([lru_out, attn], axis=1)
    x1, h2, logits = _post_mix(x, mix, (mods[2], mods[3], mods[4]), w)
    x1_ref[...] = x1
    h2_ref[...] = h2
    route, tot = _route_tile(logits)
    lg_ref[0][...] = route
    lg_ref[1][...] = jnp.broadcast_to(tot, lg_ref[1].shape)
    g2_ref[...] = mods[5]


def _mixer_kernel(n_prompt_tiles, tiles_per_seq,
                  xp_ref, modp_ref, cosp_ref, sinp_ref, xs_ref, mods_ref, coss_ref, sins_ref, sinks_ref,
                  h0_ref, cprev_ref, ck_ref, cv_ref, *rest):
    nw = len(WEIGHT_NAMES)
    w = dict(zip(WEIGHT_NAMES, rest[:nw]))
    (x1_ref, h2_ref, route_ref, cnt_ref, hlast_ref, ulast_ref, klast_ref, vlast_ref,
     g2_ref, hs_ref, u_ref, ko_ref, vo_ref, conv_c, h_c, k_c, v_c) = rest[nw:]
    lg_ref = (route_ref, cnt_ref)
    i = pl.program_id(0)

    @pl.when(i < n_prompt_tiles)
    def _():
        _prompt_body(i % tiles_per_seq, xp_ref, modp_ref, cosp_ref, sinp_ref, sinks_ref, w, x1_ref, h2_ref, lg_ref,
                     hlast_ref, ulast_ref, klast_ref, vlast_ref, conv_c, h_c, k_c, v_c)

    @pl.when(i >= n_prompt_tiles)
    def _():
        _sample_body(xs_ref, mods_ref, coss_ref, sins_ref, sinks_ref, h0_ref, cprev_ref, ck_ref, cv_ref, w,
                     x1_ref, h2_ref, lg_ref, g2_ref, hs_ref, u_ref, ko_ref, vo_ref)


def _dispatch_kernel(meta_ref, dest_ref, h2_ref, xs_hbm, xbuf, zblk, sem, zsem):
    i = pl.program_id(0)
    nb = pl.num_programs(0)
    slot = i % 2
    tt = SEQ_TILE
    n_blocks = xs_hbm.shape[0] // MOE_TM

    xbuf[slot] = h2_ref[...]

    def body(t, carry):
        for kk in range(TOP_K):
            dst = dest_ref[0, 0, t * TOP_K + kk]
            pltpu.make_async_copy(xbuf.at[slot, pl.ds(t, 1)], xs_hbm.at[pl.ds(dst, 1)], sem.at[slot]).start()
        return carry
    lax.fori_loop(0, tt, body, 0, unroll=4)

    def wait_tile(s):
        for _ in range(TOP_K):
            pltpu.make_async_copy(xbuf.at[s], xs_hbm.at[pl.ds(0, tt)], sem.at[s]).wait()

    @pl.when(i > 0)
    def _():
        wait_tile(1 - slot)

    @pl.when(i == nb - 1)
    def _():
        wait_tile(slot)
        zblk[...] = jnp.zeros_like(zblk)

        def zero_row(r):
            return pltpu.make_async_copy(zblk.at[pl.ds(0, 1)], xs_hbm.at[pl.ds(r, 1)], zsem.at[0])

        def zero_block(j):
            return pltpu.make_async_copy(zblk, xs_hbm.at[pl.ds(pl.multiple_of(j * MOE_TM, MOE_TM), MOE_TM)], zsem.at[0])

        def for_tail_rows(fn):
            def e_body(e, carry):
                def r_body(r, c2):
                    fn(zero_row(r))
                    return c2
                return lax.fori_loop(meta_ref[e], meta_ref[N_EXPERTS + e], r_body, carry)
            lax.fori_loop(0, N_EXPERTS, e_body, 0)

        def for_tail_blocks(fn):
            def b_body(j, carry):
                fn(zero_block(j))
                return carry
            lax.fori_loop(meta_ref[2 * N_EXPERTS], n_blocks, b_body, 0)

        for_tail_rows(lambda cp: cp.start())
        for_tail_blocks(lambda cp: cp.start())
        for_tail_rows(lambda cp: cp.wait())
        for_tail_blocks(lambda cp: cp.wait())


def _dispatch(meta, dest_blocks, h2_all, n_rows):
    nt = dest_blocks.shape[0]
    tt = SEQ_TILE
    grid_spec = pltpu.PrefetchScalarGridSpec(
        num_scalar_prefetch=1,
        grid=(nt,),
        in_specs=[pl.BlockSpec((1, 1, TOP_K * tt), lambda i, m: (i, 0, 0), memory_space=pltpu.SMEM),
                  pl.BlockSpec((tt, D_MODEL), lambda i, m: (i, 0))],
        out_specs=pl.BlockSpec(memory_space=pl.ANY),
        scratch_shapes=[pltpu.VMEM((2, tt, D_MODEL), F32), pltpu.VMEM((MOE_TM, D_MODEL), F32),
                        pltpu.SemaphoreType.DMA((2,)), pltpu.SemaphoreType.DMA((1,))],
    )
    return pl.pallas_call(
        _dispatch_kernel,
        grid_spec=grid_spec,
        out_shape=jax.ShapeDtypeStruct((n_rows, D_MODEL), F32),
        compiler_params=pltpu.CompilerParams(dimension_semantics=("arbitrary",), vmem_limit_bytes=VMEM_LIMIT),
        name="dispatch",
    )(meta, dest_blocks, h2_all)


def _moe_kernel(blk_e_ref, meta_ref, x_ref, w1_ref, b1_ref, w2_ref, b2_ref, o_ref, w1b, w2b):
    i = pl.program_id(0)
    n_used = meta_ref[2 * N_EXPERTS]

    @pl.when(i < n_used)
    def _():
        e_cur = blk_e_ref[i]
        e_prev = blk_e_ref[jnp.maximum(i - 1, 0)]

        @pl.when(jnp.logical_or(i == 0, e_cur != e_prev))
        def _():
            chunk = 128
            def cast_body(c, carry):
                r0 = pl.multiple_of(c * chunk, chunk)
                w1b[pl.ds(r0, chunk), :] = w1_ref[pl.ds(r0, chunk), :].astype(BF16)
                w2b[pl.ds(r0, chunk), :] = w2_ref[pl.ds(r0, chunk), :].astype(BF16)
                return carry
            lax.fori_loop(0, D_MODEL // chunk, cast_body, 0)

        x = x_ref[...].astype(BF16)
        z = jnp.dot(x, w1b[...], preferred_element_type=F32) + b1_ref[...]
        glu = jnp.minimum(z[:, :D_FF], SWIGLU_LIMIT)
        lin = jnp.clip(z[:, D_FF:], -SWIGLU_LIMIT, SWIGLU_LIMIT)
        act = glu * jax.nn.sigmoid(SWIGLU_ALPHA * glu) * (lin + 1.0)
        o_ref[...] = jnp.dot(act.astype(BF16), w2b[...], preferred_element_type=F32) + b2_ref[...]

    @pl.when(i >= n_used)
    def _():
        o_ref[...] = jnp.zeros_like(o_ref)


def _moe(blk_e, meta, xs, w1, b1, w2, b2):
    nb = blk_e.shape[0]
    tm = MOE_TM
    used = lambda i, m: jnp.minimum(i, m[2 * N_EXPERTS] - 1)
    grid_spec = pltpu.PrefetchScalarGridSpec(
        num_scalar_prefetch=2,
        grid=(nb,),
        in_specs=[
            pl.BlockSpec((tm, D_MODEL), lambda i, be, m: (used(i, m), 0)),
            pl.BlockSpec((None, D_MODEL, 2 * D_FF), lambda i, be, m: (be[used(i, m)], 0, 0)),
            pl.BlockSpec((None, 1, 2 * D_FF), lambda i, be, m: (be[used(i, m)], 0, 0)),
            pl.BlockSpec((None, D_FF, D_MODEL), lambda i, be, m: (be[used(i, m)], 0, 0)),
            pl.BlockSpec((None, 1, D_MODEL), lambda i, be, m: (be[used(i, m)], 0, 0)),
        ],
        out_specs=pl.BlockSpec((tm, D_MODEL), lambda i, be, m: (i, 0)),
        scratch_shapes=[pltpu.VMEM((D_MODEL, 2 * D_FF), BF16), pltpu.VMEM((D_FF, D_MODEL), BF16)],
    )
    return pl.pallas_call(
        _moe_kernel,
        grid_spec=grid_spec,
        out_shape=jax.ShapeDtypeStruct((nb * tm, D_MODEL), F32),
        compiler_params=pltpu.CompilerParams(dimension_semantics=("arbitrary",), vmem_limit_bytes=VMEM_LIMIT),
        name="moe",
    )(blk_e, meta, xs, w1, b1, w2, b2)


def _combine_kernel(n_prompt_tiles, slotc_ref, slotn_ref, ys_hbm, route_ref, x1_ref, modp_ref, g2s_ref, fg_ref, op_ref,
                    os_ref, buf, sem):
    i = pl.program_id(0)
    nb = pl.num_programs(0)
    slot = i % 2
    tt = COMB_TT

    def issue(slot_ref, s):
        def body(t, carry):
            for kk in range(TOP_K):
                src = slot_ref[0, 0, t * TOP_K + kk]
                pltpu.make_async_copy(ys_hbm.at[pl.ds(src, 1)], buf.at[s, pl.ds(kk * tt + t, 1)], sem.at[s]).start()
            return carry
        lax.fori_loop(0, tt, body, 0, unroll=4)

    @pl.when(i == 0)
    def _():
        issue(slotc_ref, 0)

    @pl.when(i + 1 < nb)
    def _():
        issue(slotn_ref, 1 - slot)

    pltpu.make_async_copy(ys_hbm.at[pl.ds(0, TOP_K * tt)], buf.at[slot], sem.at[slot]).wait()

    route = route_ref[...]
    ff = route[:, ROUTE_G:ROUTE_G + 1] * buf[slot, 0:tt, :]
    for kk in range(1, TOP_K):
        ff = ff + route[:, ROUTE_G + kk:ROUTE_G + kk + 1] * buf[slot, kk * tt:(kk + 1) * tt, :]
    g2 = jnp.where(i < n_prompt_tiles, modp_ref[5:6, :], g2s_ref[...])
    x = x1_ref[...] + g2 * ff
    y = _rms(x, fg_ref[...])

    @pl.when(i < n_prompt_tiles)
    def _():
        op_ref[...] = y

    @pl.when(i >= n_prompt_tiles)
    def _():
        os_ref[...] = y


def _combine(slot_blocks, ys, route, x1_all, mod_p, g2_rows, final_g, n_prompt_tiles, tiles_per_batch):
    nt = slot_blocks.shape[0]
    tt = COMB_TT
    npt = n_prompt_tiles
    return pl.pallas_call(
        functools.partial(_combine_kernel, npt),
        grid=(nt,),
        in_specs=[
            pl.BlockSpec((1, 1, TOP_K * tt), lambda i: (i, 0, 0), memory_space=pltpu.SMEM),
            pl.BlockSpec((1, 1, TOP_K * tt), lambda i: (jnp.minimum(i + 1, nt - 1), 0, 0), memory_space=pltpu.SMEM),
            pl.BlockSpec(memory_space=pl.ANY),
            pl.BlockSpec((tt, LANES), lambda i: (i, 0)),
            pl.BlockSpec((tt, D_MODEL), lambda i: (i, 0)),
            pl.BlockSpec((None, 6, D_MODEL), lambda i: (jnp.minimum(i, npt - 1) // tiles_per_batch, 0, 0)),
            pl.BlockSpec((tt, D_MODEL), lambda i: (jnp.maximum(i - npt, 0), 0)),
            pl.BlockSpec((1, D_MODEL), lambda i: (0, 0)),
        ],
        out_specs=(pl.BlockSpec((tt, D_MODEL), lambda i: (jnp.minimum(i, npt - 1), 0)),
                   pl.BlockSpec((tt, D_MODEL), lambda i: (jnp.maximum(i - npt, 0), 0))),
        out_shape=(jax.ShapeDtypeStruct((npt * tt, D_MODEL), F32),
                   jax.ShapeDtypeStruct(((nt - npt) * tt, D_MODEL), F32)),
        scratch_shapes=[pltpu.VMEM((2, TOP_K * tt, D_MODEL), F32), pltpu.SemaphoreType.DMA((2,))],
        compiler_params=pltpu.CompilerParams(dimension_semantics=("arbitrary",), vmem_limit_bytes=VMEM_LIMIT),
        name="combine",
    )(slot_blocks, slot_blocks, ys, route, x1_all, mod_p, g2_rows, final_g)


def _block_diag_halves(wg):
    halves = []
    for hh in range(2):
        rows = []
        for bi in range(4):
            row = [wg[hh * 4 + bi] if bj == bi else jnp.zeros((LRU_BLOCK_W, LRU_BLOCK_W), wg.dtype) for bj in range(4)]
            rows.append(jnp.concatenate(row, axis=1))
        halves.append(jnp.concatenate(rows, axis=0))
    return jnp.stack(halves).astype(BF16)


def _rope_tables(pos):
    half = HEAD_DIM // 2
    inv = ROPE_THETA ** (-jnp.arange(half, dtype=F32) / half)
    ang = pos.astype(F32)[:, None] * inv[None, :]
    cos = jnp.cos(ang)
    sin = jnp.sin(ang)
    cos128 = jnp.concatenate([cos, cos, cos, cos], axis=1)
    sin128 = jnp.concatenate([-sin, sin, -sin, sin], axis=1)
    return cos128, sin128


def _full_spec(arr, grid_rank):
    zeros = (0,) * arr.ndim
    if grid_rank == 1:
        return pl.BlockSpec(arr.shape, lambda i: zeros)
    return pl.BlockSpec(arr.shape, lambda b, j: zeros)


def kernel(x_prompt, x_sample, state_lru_h, state_conv, cache_win_k, cache_win_v, c_prompt, c_sample, w_ada, b_ada, norm1_g, w_in, b_in, conv_w, conv_b, lru_wa, lru_ba, lru_wx, lru_bx, lru_lambda, attn_sinks, w_out, b_out, norm2_g, w_router, b_router, w1, b1, w2, b2, final_g):
    bp, seq, _ = x_prompt.shape
    bd, tdec, _ = x_sample.shape
    assert tdec == SUBLANES and seq % SEQ_TILE == 0 and bd % SAMPLE_BT == 0
    n_prompt = bp * seq
    n_sample = bd * tdec
    n_tok = n_prompt + n_sample
    l = 0

    head_perm = jnp.array([h for c in range(4) for h in (c, GROUP + c)], dtype=jnp.int32)
    qcol_perm = (head_perm[:, None] * HEAD_DIM + jnp.arange(HEAD_DIM, dtype=jnp.int32)[None, :]).reshape(-1)
    o2 = 2 * LRU_WIDTH
    in_perm = jnp.concatenate([jnp.arange(o2, dtype=jnp.int32), o2 + qcol_perm,
                               jnp.arange(o2 + Q_WIDTH, IN_WIDTH, dtype=jnp.int32)])
    out_perm = jnp.concatenate([jnp.arange(LRU_WIDTH, dtype=jnp.int32), LRU_WIDTH + qcol_perm])
    sinks_perm = attn_sinks[l]

    wr = jnp.tile(w_router[l], (1, TOP_K))
    wr_hi = wr.astype(BF16)
    weights = dict(
        norm1_g=norm1_g[l][None, :], w_in=w_in[l][:, in_perm].astype(BF16), b_in=b_in[l][in_perm][None, :],
        conv_w=conv_w[l], conv_b=conv_b[l][None, :],
        ga=_block_diag_halves(lru_wa[l]), gx=_block_diag_halves(lru_wx[l]),
        lru_ba=lru_ba[l][None, :], lru_bx=lru_bx[l][None, :], lam=lru_lambda[l][None, :],
        w_out=w_out[l][out_perm, :].astype(BF16), b_out=b_out[l][None, :], norm2_g=norm2_g[l][None, :],
        wr_hi=wr_hi, wr_lo=(wr - wr_hi.astype(F32)).astype(BF16),
        b_router=jnp.tile(b_router[l], TOP_K)[None, :],
    )
    wlist = [weights[n] for n in WEIGHT_NAMES]

    mod_all = _ada(jnp.concatenate([c_prompt, c_sample], axis=0), w_ada[l], b_ada[l][None, :])
    mod_p = mod_all[:bp].reshape(bp, 6, D_MODEL)
    mod_s = mod_all[bp:].reshape(bd, 6, D_MODEL).transpose(1, 0, 2)

    cos_p, sin_p = _rope_tables(jnp.arange(seq, dtype=jnp.int32))
    cos_s, sin_s = _rope_tables(PAST_LEN + jnp.arange(tdec, dtype=jnp.int32))
    cos_s = jnp.tile(cos_s, (SAMPLE_BT, 1))
    sin_s = jnp.tile(sin_s, (SAMPLE_BT, 1))
    h0_rows = jnp.pad(state_lru_h[l][:, None, :], ((0, 0), (0, tdec - 1), (0, 0))).reshape(n_sample, LRU_WIDTH)
    cprev_rows = jnp.pad(state_conv[l], ((0, 0), (0, tdec - (CONV_W - 1)), (0, 0))).reshape(n_sample, LRU_WIDTH)
    ck = cache_win_k[l].reshape(bd, WINDOW, KV_WIDTH)
    cv = cache_win_v[l].reshape(bd, WINDOW, KV_WIDTH)
    nj = seq // SEQ_TILE
    npt = n_prompt // SEQ_TILE
    nst = n_sample // SEQ_TILE
    p_tile = lambda i: jnp.minimum(i, npt - 1)
    s_tile = lambda i: jnp.maximum(i - npt, 0)
    all_rows = lambda width: pl.BlockSpec((SEQ_TILE, width), lambda i: (i, 0))
    p_rows = lambda width: pl.BlockSpec((SEQ_TILE, width), lambda i: (p_tile(i), 0))
    s_rows = lambda width: pl.BlockSpec((SEQ_TILE, width), lambda i: (s_tile(i), 0))
    p_tail = lambda rows, width: pl.BlockSpec((None, rows, width), lambda i: (p_tile(i) // nj, 0, 0))
    cache_spec = pl.BlockSpec((SAMPLE_BT, WINDOW, KV_WIDTH), lambda i: (s_tile(i), 0, 0))
    mixer_out_shapes = (
        jax.ShapeDtypeStruct((n_tok, D_MODEL), F32),
        jax.ShapeDtypeStruct((n_tok, D_MODEL), F32),
        jax.ShapeDtypeStruct((n_tok, LANES), F32),
        jax.ShapeDtypeStruct((npt + nst, SUBLANES, LANES), F32),
        jax.ShapeDtypeStruct((bp, SUBLANES, LRU_WIDTH), F32),
        jax.ShapeDtypeStruct((bp, SUBLANES, LRU_WIDTH), F32),
        jax.ShapeDtypeStruct((bp, WINDOW, KV_WIDTH), F32),
        jax.ShapeDtypeStruct((bp, WINDOW, KV_WIDTH), F32),
        jax.ShapeDtypeStruct((n_sample, D_MODEL), F32),
        jax.ShapeDtypeStruct((n_sample, LRU_WIDTH), F32),
        jax.ShapeDtypeStruct((n_sample, LRU_WIDTH), F32),
        jax.ShapeDtypeStruct((bd, WINDOW, KV_WIDTH), F32),
        jax.ShapeDtypeStruct((bd, WINDOW, KV_WIDTH), F32),
    )
    (x1_all, h2_all, route, tile_cnt, hlast_p, ulast_p, klast_p, vlast_p, g2_rows, hs_s, u_s, s_k, s_v) = pl.pallas_call(
        functools.partial(_mixer_kernel, npt, nj),
        grid=(npt + nst,),
        in_specs=[p_rows(D_MODEL),
                  pl.BlockSpec((None, 6, D_MODEL), lambda i: (p_tile(i) // nj, 0, 0)),
                  pl.BlockSpec((SEQ_TILE, LANES), lambda i: (p_tile(i) % nj, 0)),
                  pl.BlockSpec((SEQ_TILE, LANES), lambda i: (p_tile(i) % nj, 0)),
                  s_rows(D_MODEL),
                  pl.BlockSpec((6, SAMPLE_BT, D_MODEL), lambda i: (0, s_tile(i), 0)),
                  pl.BlockSpec((SEQ_TILE, LANES), lambda i: (0, 0)),
                  pl.BlockSpec((SEQ_TILE, LANES), lambda i: (0, 0)),
                  pl.BlockSpec(memory_space=pltpu.SMEM),
                  s_rows(LRU_WIDTH), s_rows(LRU_WIDTH), cache_spec, cache_spec]
                 + [_full_spec(a, 1) for a in wlist],
        out_specs=(all_rows(D_MODEL), all_rows(D_MODEL), all_rows(LANES),
                   pl.BlockSpec((None, SUBLANES, LANES), lambda i: (i, 0, 0)),
                   p_tail(SUBLANES, LRU_WIDTH), p_tail(SUBLANES, LRU_WIDTH),
                   p_tail(WINDOW, KV_WIDTH), p_tail(WINDOW, KV_WIDTH),
                   s_rows(D_MODEL), s_rows(LRU_WIDTH), s_rows(LRU_WIDTH), cache_spec, cache_spec),
        out_shape=mixer_out_shapes,
        scratch_shapes=[pltpu.VMEM((SUBLANES, LRU_WIDTH), F32), pltpu.VMEM((SUBLANES, LRU_WIDTH), F32),
                        pltpu.VMEM((WINDOW, KV_WIDTH), F32), pltpu.VMEM((WINDOW, KV_WIDTH), F32)],
        compiler_params=pltpu.CompilerParams(dimension_semantics=("arbitrary",), vmem_limit_bytes=VMEM_LIMIT),
        name="mixer",
    )(x_prompt.reshape(n_prompt, D_MODEL), mod_p, cos_p, sin_p,
      x_sample.reshape(n_sample, D_MODEL), mod_s, cos_s, sin_s, sinks_perm, h0_rows, cprev_rows, ck, cv, *wlist)

    n_tiles = npt + nst
    n_assign = n_tok * TOP_K
    n_blocks = -(-(n_assign + N_EXPERTS * (MOE_TM - 1)) // MOE_TM)
    cnt = tile_cnt[:, 0, :N_EXPERTS].astype(jnp.int32)
    counts = jnp.sum(cnt, axis=0)
    pcounts = (counts + MOE_TM - 1) // MOE_TM * MOE_TM
    pend = jnp.cumsum(pcounts)
    pstart = pend - pcounts
    tile_off = pstart[None, :] + jnp.cumsum(cnt, axis=0) - cnt
    e_ids = route[:, ROUTE_E:ROUTE_E + TOP_K].astype(jnp.int32).reshape(n_tiles, SEQ_TILE, TOP_K)
    ranks = route[:, ROUTE_R:ROUTE_R + TOP_K].astype(jnp.int32).reshape(n_tiles, SEQ_TILE, TOP_K)
    hit = e_ids[..., None] == jnp.arange(N_EXPERTS, dtype=jnp.int32)
    dest = jnp.sum(jnp.where(hit, tile_off[:, None, None, :], 0), axis=-1) + ranks
    dest_blocks = dest.reshape(n_tiles, 1, SEQ_TILE * TOP_K)
    blk_start = jnp.arange(n_blocks, dtype=jnp.int32) * MOE_TM
    blk_e = jnp.minimum(jnp.sum((pend[None, :] <= blk_start[:, None]).astype(jnp.int32), axis=1), N_EXPERTS - 1)
    meta = jnp.concatenate([pstart + counts, pend, pend[-1:] // MOE_TM]).astype(jnp.int32)

    xs = _dispatch(meta, dest_blocks, h2_all, n_blocks * MOE_TM)
    ys = _moe(blk_e, meta, xs, w1[l], b1[l][:, None, :], w2[l], b2[l][:, None, :])
    y_p, y_s = _combine(dest_blocks, ys, route, x1_all, mod_p, g2_rows,
                        final_g[None, :], n_prompt // COMB_TT, seq // COMB_TT)

    y_prompt = y_p.reshape(bp, seq, D_MODEL)
    y_sample = y_s.reshape(bd, tdec, D_MODEL)
    p_h = hlast_p[:, SUBLANES - 1, :][None]
    p_c = ulast_p[:, SUBLANES - (CONV_W - 1):, :][None]
    p_k = klast_p.reshape(1, bp, WINDOW, N_KV_HEADS, HEAD_DIM)
    p_v = vlast_p.reshape(1, bp, WINDOW, N_KV_HEADS, HEAD_DIM)
    s_h = hs_s.reshape(bd, tdec, LRU_WIDTH)[:, tdec - 1, :][None]
    s_c = u_s.reshape(bd, tdec, LRU_WIDTH)[:, tdec - (CONV_W - 1):, :][None]
    s_kk = s_k.reshape(1, bd, WINDOW, N_KV_HEADS, HEAD_DIM)
    s_vv = s_v.reshape(1, bd, WINDOW, N_KV_HEADS, HEAD_DIM)
    return (y_prompt, y_sample, p_h, p_c, p_k, p_v, s_h, s_c, s_kk, s_vv)
```

```python
import functools

import jax
import jax.numpy as jnp
from jax import lax
from jax.experimental import pallas as pl
from jax.experimental.pallas import tpu as pltpu

F32 = jnp.float32
BF16 = jnp.bfloat16

D_MODEL = 1024
LRU_WIDTH = 512
LRU_BLOCKS = 8
LRU_BLOCK_W = LRU_WIDTH // LRU_BLOCKS
CONV_W = 4
LRU_C = 8.0
HEAD_DIM = 64
N_HEADS = 8
N_KV_HEADS = 2
GROUP = N_HEADS // N_KV_HEADS
WINDOW = 128
ROPE_THETA = 10000.0
N_EXPERTS = 32
TOP_K = 4
D_FF = D_MODEL
SWIGLU_LIMIT = 7.0
SWIGLU_ALPHA = 1.702
NORM_EPS = 1e-5
PAST_LEN = 8192
Q_WIDTH = N_HEADS * HEAD_DIM
KV_WIDTH = N_KV_HEADS * HEAD_DIM
IN_WIDTH = 2 * LRU_WIDTH + Q_WIDTH + 2 * KV_WIDTH

LANES = 128
SUBLANES = 8
SEQ_TILE = 256
SAMPLE_BT = 32
MOE_TM = 256
MOE_CH = 2 * MOE_TM
COMB_TT = 256
NEG_BIG = -1e30
VMEM_LIMIT = 56 * 1024 * 1024


def _rms(x, g):
    return x * lax.rsqrt(jnp.mean(x * x, axis=-1, keepdims=True) + NORM_EPS) * g


def _shift_rows(x, d, fill, rowid):
    return jnp.where(rowid >= d, pltpu.roll(x, d, axis=0), fill)


def _lin_scan(a, b, seg, rowid):
    d = 1
    while d < seg:
        a_s = _shift_rows(a, d, 1.0, rowid)
        b_s = _shift_rows(b, d, 0.0, rowid)
        b = a * b_s + b
        a = a * a_s
        d *= 2
    return a, b


def _rope128(x, cos, sin_signed, first_half):
    sw = jnp.where(first_half, pltpu.roll(x, LANES - HEAD_DIM // 2, axis=1), pltpu.roll(x, HEAD_DIM // 2, axis=1))
    return x * cos + sw * sin_signed


def _neg_expm1(y):
    return -(jnp.tanh(0.5 * y) * (jnp.exp(y) + 1.0))


def _softplus(x):
    return jnp.maximum(x, 0.0) + jnp.log1p(jnp.exp(-jnp.abs(x)))


def _lru_coeffs(uc, w, first_pos_mask):
    ub = uc.astype(BF16)
    half = LRU_WIDTH // 2
    ra = jnp.concatenate([jnp.dot(ub[:, :half], w['ga'][0], preferred_element_type=F32),
                          jnp.dot(ub[:, half:], w['ga'][1], preferred_element_type=F32)], axis=1)
    rx = jnp.concatenate([jnp.dot(ub[:, :half], w['gx'][0], preferred_element_type=F32),
                          jnp.dot(ub[:, half:], w['gx'][1], preferred_element_type=F32)], axis=1)
    r = jax.nn.sigmoid(ra + w['lru_ba'][...])
    i = jax.nn.sigmoid(rx + w['lru_bx'][...])
    log_a = -LRU_C * r * _softplus(-w['lam'][...])
    a = jnp.exp(log_a)
    mult = jnp.sqrt(_neg_expm1(2.0 * log_a))
    if first_pos_mask is not None:
        mult = jnp.where(first_pos_mask, 1.0, mult)
    return a, mult * i * uc


def _conv_taps(u, s1, s2, s3, w):
    cw = w['conv_w']
    return w['conv_b'][...] + s3 * cw[0:1, :] + s2 * cw[1:2, :] + s1 * cw[2:3, :] + u * cw[3:4, :]


def _in_proj(x, mod, w):
    sh1, sc1 = mod
    h = _rms(x, w['norm1_g'][...]) * (1.0 + sc1) + sh1
    return jnp.dot(h.astype(BF16), w['w_in'][...], preferred_element_type=F32) + w['b_in'][...]


def _post_mix(x, mix, mod, w):
    g1, sh2, sc2 = mod
    x1 = x + g1 * (jnp.dot(mix.astype(BF16), w['w_out'][...], preferred_element_type=F32) + w['b_out'][...])
    h2 = _rms(x1, w['norm2_g'][...]) * (1.0 + sc2) + sh2
    h2_hi = h2.astype(BF16)
    h2_lo = (h2 - h2_hi.astype(F32)).astype(BF16)
    logits = (jnp.dot(h2_hi, w['wr_hi'][...], preferred_element_type=F32)
              + jnp.dot(h2_lo, w['wr_hi'][...], preferred_element_type=F32)
              + jnp.dot(h2_hi, w['wr_lo'][...], preferred_element_type=F32)) + w['b_router'][...]
    return x1, h2, logits


ROUTE_E, ROUTE_G, ROUTE_R = 0, TOP_K, 2 * TOP_K


def _lane_roll1(v, shift):
    return pltpu.roll(jnp.broadcast_to(v, (SUBLANES, LANES)), shift, axis=1)[0:1]


def _route_tile(lg):
    rows = lg.shape[0]
    lane = lax.broadcasted_iota(jnp.int32, (1, LANES), 1)
    e_of = lane % N_EXPERTS
    grp = lane // N_EXPERTS
    onehot = jnp.zeros((rows, LANES), F32)
    vals, ids = [], []
    for k in range(TOP_K):
        m = jnp.max(lg, axis=1, keepdims=True)
        idx = jnp.min(jnp.where(lg == m, e_of, N_EXPERTS), axis=1, keepdims=True)
        sel = e_of == idx
        lg = jnp.where(sel, -jnp.inf, lg)
        onehot = jnp.where(jnp.logical_and(sel, grp == k), 1.0, onehot)
        vals.append(m)
        ids.append(idx)
    ex = [jnp.exp(v - vals[0]) for v in vals]
    denom = ex[0] + ex[1] + ex[2] + ex[3]

    r_i = lax.broadcasted_iota(jnp.int32, (rows, rows), 0)
    c_i = lax.broadcasted_iota(jnp.int32, (rows, rows), 1)
    strict_lower = jnp.where(r_i > c_i, 1.0, 0.0).astype(BF16)
    prefix = jnp.dot(strict_lower, onehot.astype(BF16), preferred_element_type=F32)
    cnt = jnp.sum(onehot, axis=0, keepdims=True)
    base = jnp.zeros((1, LANES), F32)
    tot = cnt
    for s in range(1, TOP_K):
        rolled = _lane_roll1(cnt, s * N_EXPERTS)
        base = base + jnp.where(lane >= s * N_EXPERTS, rolled, 0.0)
        tot = tot + rolled
    ranked = onehot * (prefix + base)

    route = jnp.zeros((rows, LANES), F32)
    for k in range(TOP_K):
        rank_k = jnp.sum(jnp.where(grp == k, ranked, 0.0), axis=1, keepdims=True)
        route = jnp.where(lane == ROUTE_E + k, ids[k].astype(F32), route)
        route = jnp.where(lane == ROUTE_G + k, ex[k] / denom, route)
        route = jnp.where(lane == ROUTE_R + k, rank_k, route)
    return route, tot


def _softmax_sink_pv(s, sink_col, v_bf16):
    m = jnp.maximum(jnp.max(s, axis=-1, keepdims=True), sink_col)
    p = jnp.exp(s - m)
    denom = jnp.sum(p, axis=-1, keepdims=True) + jnp.exp(sink_col - m)
    return p, denom


WEIGHT_NAMES = ('norm1_g', 'w_in', 'b_in', 'conv_w', 'conv_b', 'ga', 'gx', 'lru_ba', 'lru_bx', 'lam',
                'w_out', 'b_out', 'norm2_g', 'wr_hi', 'wr_lo', 'b_router')


def _ada_kernel(c_ref, w_ref, b_ref, o_ref):
    c = c_ref[...]
    s = c * jax.nn.sigmoid(c)
    o_ref[...] = jnp.dot(s, w_ref[...], preferred_element_type=F32, precision=lax.Precision.HIGHEST) + b_ref[...]


def _ada(c_all, w_ada, b_ada):
    rows = c_all.shape[0]
    return pl.pallas_call(
        _ada_kernel,
        grid=(6,),
        in_specs=[pl.BlockSpec((rows, D_MODEL), lambda i: (0, 0)),
                  pl.BlockSpec((D_MODEL, D_MODEL), lambda i: (0, i)),
                  pl.BlockSpec((1, D_MODEL), lambda i: (0, i))],
        out_specs=pl.BlockSpec((rows, D_MODEL), lambda i: (0, i)),
        out_shape=jax.ShapeDtypeStruct((rows, 6 * D_MODEL), F32),
        compiler_params=pltpu.CompilerParams(dimension_semantics=("arbitrary",), vmem_limit_bytes=VMEM_LIMIT),
        name="ada",
    )(c_all, w_ada, b_ada)


def _prompt_body(j, x_ref, mod_ref, cos_ref, sin_ref, sinks_ref, w, x1_ref, h2_ref, lg_ref,
                 hlast_ref, ulast_ref, klast_ref, vlast_ref, conv_c, h_c, k_c, v_c):
    ts = SEQ_TILE

    @pl.when(j == 0)
    def _():
        conv_c[...] = jnp.zeros_like(conv_c)
        h_c[...] = jnp.zeros_like(h_c)
        k_c[...] = jnp.zeros_like(k_c)
        v_c[...] = jnp.zeros_like(v_c)

    x = x_ref[...]
    mod = mod_ref[...]
    proj = _in_proj(x, (mod[0:1], mod[1:2]), w)
    u = proj[:, :LRU_WIDTH]
    gate = proj[:, LRU_WIDTH:2 * LRU_WIDTH]
    o2 = 2 * LRU_WIDTH

    rowid = lax.broadcasted_iota(jnp.int32, (ts, 1), 0)
    u_ext = jnp.concatenate([conv_c[...], u], axis=0)
    s1, s2, s3 = (pltpu.roll(u_ext, d, axis=0)[SUBLANES:] for d in (1, 2, 3))
    uc = _conv_taps(u, s1, s2, s3, w)
    conv_c[...] = u[ts - SUBLANES:]
    ulast_ref[...] = u[ts - SUBLANES:]

    first_pos = jnp.logical_and(rowid == 0, j == 0)
    a, bt = _lru_coeffs(uc, w, first_pos)
    a_cum, b_cum = _lin_scan(a, bt, ts, rowid)
    hs = b_cum + a_cum * h_c[0:1, :]
    h_tail = hs[ts - SUBLANES:]
    h_c[...] = jnp.broadcast_to(h_tail[SUBLANES - 1:SUBLANES, :], h_c.shape)
    hlast_ref[...] = h_tail
    lru_out = hs * jax.nn.gelu(gate)

    cos = cos_ref[...]
    sin = sin_ref[...]
    lane = lax.broadcasted_iota(jnp.int32, (1, LANES), 1)
    first_half = (lane % HEAD_DIM) < (HEAD_DIM // 2)
    qcols = [_rope128(proj[:, o2 + c * LANES:o2 + (c + 1) * LANES], cos, sin, first_half) for c in range(4)]
    k = _rope128(proj[:, o2 + Q_WIDTH:o2 + Q_WIDTH + KV_WIDTH], cos, sin, first_half)
    v = proj[:, o2 + Q_WIDTH + KV_WIDTH:]
    k_ext = jnp.concatenate([k_c[...], k], axis=0).astype(BF16)
    v_ext = jnp.concatenate([v_c[...], v], axis=0).astype(BF16)
    k_c[...] = k[ts - WINDOW:]
    v_c[...] = v[ts - WINDOW:]
    klast_ref[...] = k[ts - WINDOW:]
    vlast_ref[...] = v[ts - WINDOW:]

    qi = lax.broadcasted_iota(jnp.int32, (WINDOW, 2 * WINDOW), 0)
    kj = lax.broadcasted_iota(jnp.int32, (WINDOW, 2 * WINDOW), 1)
    band = jnp.logical_and(kj > qi, kj <= qi + WINDOW)
    lane_lo = lane < HEAD_DIM
    grow = lax.broadcasted_iota(jnp.int32, (GROUP * WINDOW, 1), 0) // WINDOW
    attn_cols = [[] for _ in range(4)]
    for blk in range(ts // WINDOW):
        prev_ok = jnp.logical_or(j > 0, blk > 0)
        mask = jnp.logical_and(band, jnp.logical_or(kj >= WINDOW, prev_ok))
        mask4 = jnp.concatenate([mask] * GROUP, axis=0)
        kb = k_ext[blk * WINDOW:(blk + 2) * WINDOW]
        vb = v_ext[blk * WINDOW:(blk + 2) * WINDOW]
        outs = []
        for kv in range(N_KV_HEADS):
            sel = lane_lo if kv == 0 else jnp.logical_not(lane_lo)
            qs = jnp.concatenate(
                [jnp.where(sel, qc[blk * WINDOW:(blk + 1) * WINDOW], 0.0) for qc in qcols], axis=0).astype(BF16)
            s = lax.dot_general(qs, kb, (((1,), (1,)), ((), ())), preferred_element_type=F32) * (HEAD_DIM ** -0.5)
            s = jnp.where(mask4, s, NEG_BIG)
            sink_col = jnp.zeros((GROUP * WINDOW, 1), F32)
            for g in range(GROUP):
                sink_col = jnp.where(grow == g, sinks_ref[kv * GROUP + g], sink_col)
            p, denom = _softmax_sink_pv(s, sink_col, vb)
            outs.append(jnp.dot(p.astype(BF16), vb, preferred_element_type=F32) / denom)
        for c in range(4):
            attn_cols[c].append(jnp.where(lane_lo, outs[0][c * WINDOW:(c + 1) * WINDOW],
                                          outs[1][c * WINDOW:(c + 1) * WINDOW]))
    attn = jnp.concatenate([jnp.concatenate(cols, axis=0) for cols in attn_cols], axis=1)

    mix = jnp.concatenate([lru_out, attn], axis=1)
    x1, h2, logits = _post_mix(x, mix, (mod[2:3], mod[3:4], mod[4:5]), w)
    x1_ref[...] = x1
    h2_ref[...] = h2
    route, tot = _route_tile(logits)
    lg_ref[0][...] = route
    lg_ref[1][...] = jnp.broadcast_to(tot, lg_ref[1].shape)


def _expand_rows(m, t):
    b, wd = m.shape
    return jnp.broadcast_to(m[:, None, :], (b, t, wd)).reshape(b * t, wd)


def _sample_body(x_ref, mod_ref, cos_ref, sin_ref, sinks_ref, h0_ref, cprev_ref, ck_ref, cv_ref, w,
                 x1_ref, h2_ref, lg_ref, g2_ref, hs_ref, u_ref, ko_ref, vo_ref):
    bt_, t = SAMPLE_BT, SUBLANES
    rows = bt_ * t

    x = x_ref[...]
    mods = [_expand_rows(mod_ref[i], t) for i in range(6)]
    proj = _in_proj(x, (mods[0], mods[1]), w)
    u = proj[:, :LRU_WIDTH]
    gate = proj[:, LRU_WIDTH:2 * LRU_WIDTH]
    o2 = 2 * LRU_WIDTH
    u_ref[...] = u

    rowid = lax.broadcasted_iota(jnp.int32, (rows, 1), 0) % t
    cprev = cprev_ref[...]
    taps = []
    for d in (1, 2, 3):
        taps.append(jnp.where(rowid >= d, pltpu.roll(u, d, axis=0),
                              pltpu.roll(cprev, (d - (CONV_W - 1)) % rows, axis=0)))
    uc = _conv_taps(u, taps[0], taps[1], taps[2], w)

    a, bt = _lru_coeffs(uc, w, None)
    bt = bt + a * h0_ref[...]
    _, hs = _lin_scan(a, bt, t, rowid)
    hs_ref[...] = hs
    lru_out = hs * jax.nn.gelu(gate)

    cos = cos_ref[...]
    sin = sin_ref[...]
    lane = lax.broadcasted_iota(jnp.int32, (1, LANES), 1)
    first_half = (lane % HEAD_DIM) < (HEAD_DIM // 2)
    qcols = [_rope128(proj[:, o2 + c * LANES:o2 + (c + 1) * LANES], cos, sin, first_half) for c in range(4)]
    k = _rope128(proj[:, o2 + Q_WIDTH:o2 + Q_WIDTH + KV_WIDTH], cos, sin, first_half)
    v = proj[:, o2 + Q_WIDTH + KV_WIDTH:]
    k3 = k.reshape(bt_, t, KV_WIDTH)
    v3 = v.reshape(bt_, t, KV_WIDTH)
    ck = ck_ref[...]
    cv = cv_ref[...]
    ko_ref[:, :WINDOW - t, :] = ck[:, t:, :]
    ko_ref[:, WINDOW - t:, :] = k3
    vo_ref[:, :WINDOW - t, :] = cv[:, t:, :]
    vo_ref[:, WINDOW - t:, :] = v3

    ckb, cvb, k3b, v3b = ck.astype(BF16), cv.astype(BF16), k3.astype(BF16), v3.astype(BF16)
    lane_lo = lane < HEAD_DIM
    gq = GROUP * t
    tq = lax.broadcasted_iota(jnp.int32, (1, gq, 1), 1) % t
    mask_c = lax.broadcasted_iota(jnp.int32, (1, gq, WINDOW), 2) > tq
    mask_n = lax.broadcasted_iota(jnp.int32, (1, gq, t), 2) <= tq
    grow = lax.broadcasted_iota(jnp.int32, (1, gq, 1), 1) // t
    bdims = (((2,), (2,)), ((0,), (0,)))
    pdims = (((2,), (1,)), ((0,), (0,)))
    outs = []
    for kv in range(N_KV_HEADS):
        sel = lane_lo if kv == 0 else jnp.logical_not(lane_lo)
        q3 = jnp.concatenate([jnp.where(sel, qc, 0.0).reshape(bt_, t, LANES) for qc in qcols], axis=1).astype(BF16)
        sc = lax.dot_general(q3, ckb, bdims, preferred_element_type=F32) * (HEAD_DIM ** -0.5)
        sn = lax.dot_general(q3, k3b, bdims, preferred_element_type=F32) * (HEAD_DIM ** -0.5)
        sc = jnp.where(mask_c, sc, NEG_BIG)
        sn = jnp.where(mask_n, sn, NEG_BIG)
        sink_col = jnp.zeros((1, gq, 1), F32)
        for g in range(GROUP):
            sink_col = jnp.where(grow == g, sinks_ref[kv * GROUP + g], sink_col)
        m = jnp.maximum(jnp.maximum(jnp.max(sc, axis=-1, keepdims=True), jnp.max(sn, axis=-1, keepdims=True)),
                        sink_col)
        pc = jnp.exp(sc - m)
        pn = jnp.exp(sn - m)
        denom = jnp.sum(pc, axis=-1, keepdims=True) + jnp.sum(pn, axis=-1, keepdims=True) + jnp.exp(sink_col - m)
        o = (lax.dot_general(pc.astype(BF16), cvb, pdims, preferred_element_type=F32)
             + lax.dot_general(pn.astype(BF16), v3b, pdims, preferred_element_type=F32)) / denom
        outs.append(o)
    attn = jnp.concatenate(
        [jnp.where(lane_lo, outs[0][:, c * t:(c + 1) * t, :], outs[1][:, c * t:(c + 1) * t, :]).reshape(rows, LANES)
         for c in range(4)], axis=1)

    mix = jnp.concatenate([lru_out, attn], axis=1)
    x1, h2, logits = _post_mix(x, mix, (mods[2], mods[3], mods[4]), w)
    x1_ref[...] = x1
    h2_ref[...] = h2
    route, tot = _route_tile(logits)
    lg_ref[0][...] = route
    lg_ref[1][...] = jnp.broadcast_to(tot, lg_ref[1].shape)
    g2_ref[...] = mods[5]


def _mixer_kernel(n_prompt_tiles, tiles_per_seq,
                  xp_ref, modp_ref, cosp_ref, sinp_ref, xs_ref, mods_ref, coss_ref, sins_ref, sinks_ref,
                  h0_ref, cprev_ref, ck_ref, cv_ref, *rest):
    nw = len(WEIGHT_NAMES)
    w = dict(zip(WEIGHT_NAMES, rest[:nw]))
    (x1_ref, h2_ref, route_ref, cnt_ref, hlast_ref, ulast_ref, klast_ref, vlast_ref,
     g2_ref, hs_ref, u_ref, ko_ref, vo_ref, conv_c, h_c, k_c, v_c) = rest[nw:]
    lg_ref = (route_ref, cnt_ref)
    i = pl.program_id(0)

    @pl.when(i < n_prompt_tiles)
    def _():
        _prompt_body(i % tiles_per_seq, xp_ref, modp_ref, cosp_ref, sinp_ref, sinks_ref, w, x1_ref, h2_ref, lg_ref,
                     hlast_ref, ulast_ref, klast_ref, vlast_ref, conv_c, h_c, k_c, v_c)

    @pl.when(i >= n_prompt_tiles)
    def _():
        _sample_body(xs_ref, mods_ref, coss_ref, sins_ref, sinks_ref, h0_ref, cprev_ref, ck_ref, cv_ref, w,
                     x1_ref, h2_ref, lg_ref, g2_ref, hs_ref, u_ref, ko_ref, vo_ref)


def _dispatch_kernel(meta_ref, dest_ref, h2_ref, xs_hbm, xbuf, zblk, sem, zsem):
    i = pl.program_id(0)
    nb = pl.num_programs(0)
    slot = i % 2
    tt = SEQ_TILE
    n_blocks = xs_hbm.shape[0] // MOE_TM

    xbuf[slot] = h2_ref[...]

    def body(t, carry):
        for kk in range(TOP_K):
            dst = dest_ref[0, 0, t * TOP_K + kk]
            pltpu.make_async_copy(xbuf.at[slot, pl.ds(t, 1)], xs_hbm.at[pl.ds(dst, 1)],
                                  sem.at[slot]).start(priority=kk % 2)
        return carry
    lax.fori_loop(0, tt, body, 0, unroll=4)

    def wait_tile(s):
        for _ in range(TOP_K):
            pltpu.make_async_copy(xbuf.at[s], xs_hbm.at[pl.ds(0, tt)], sem.at[s]).wait()

    @pl.when(i > 0)
    def _():
        wait_tile(1 - slot)

    @pl.when(i == nb - 1)
    def _():
        wait_tile(slot)
        zblk[...] = jnp.zeros_like(zblk)

        def zero_row(r):
            return pltpu.make_async_copy(zblk.at[pl.ds(0, 1)], xs_hbm.at[pl.ds(r, 1)], zsem.at[0])

        def zero_block(j):
            return pltpu.make_async_copy(zblk, xs_hbm.at[pl.ds(pl.multiple_of(j * MOE_TM, MOE_TM), MOE_TM)], zsem.at[0])

        def for_tail_rows(fn):
            def e_body(e, carry):
                def r_body(r, c2):
                    fn(zero_row(r))
                    return c2
                return lax.fori_loop(meta_ref[e], meta_ref[N_EXPERTS + e], r_body, carry)
            lax.fori_loop(0, N_EXPERTS, e_body, 0)

        def for_tail_blocks(fn):
            def b_body(j, carry):
                fn(zero_block(j))
                return carry
            lax.fori_loop(meta_ref[2 * N_EXPERTS], n_blocks, b_body, 0)

        for_tail_rows(lambda cp: cp.start())
        for_tail_blocks(lambda cp: cp.start())
        for_tail_rows(lambda cp: cp.wait())
        for_tail_blocks(lambda cp: cp.wait())


def _dispatch(meta, dest_blocks, h2_all, n_rows):
    nt = dest_blocks.shape[0]
    tt = SEQ_TILE
    grid_spec = pltpu.PrefetchScalarGridSpec(
        num_scalar_prefetch=1,
        grid=(nt,),
        in_specs=[pl.BlockSpec((1, 1, TOP_K * tt), lambda i, m: (i, 0, 0), memory_space=pltpu.SMEM),
                  pl.BlockSpec((tt, D_MODEL), lambda i, m: (i, 0))],
        out_specs=pl.BlockSpec(memory_space=pl.ANY),
        scratch_shapes=[pltpu.VMEM((2, tt, D_MODEL), F32), pltpu.VMEM((MOE_TM, D_MODEL), F32),
                        pltpu.SemaphoreType.DMA((2,)), pltpu.SemaphoreType.DMA((1,))],
    )
    return pl.pallas_call(
        _dispatch_kernel,
        grid_spec=grid_spec,
        out_shape=jax.ShapeDtypeStruct((n_rows, D_MODEL), F32),
        compiler_params=pltpu.CompilerParams(dimension_semantics=("arbitrary",), vmem_limit_bytes=VMEM_LIMIT),
        name="dispatch",
    )(meta, dest_blocks, h2_all)


def _expert_mlp(x, w1b, b1_ref, w2b, b2_ref, act_ref):
    xb = x.astype(BF16)
    q = D_FF // 4
    for c in range(4):
        zg = jnp.dot(xb, w1b[:, c * q:(c + 1) * q], preferred_element_type=F32) + b1_ref[:, c * q:(c + 1) * q]
        zl = (jnp.dot(xb, w1b[:, D_FF + c * q:D_FF + (c + 1) * q], preferred_element_type=F32)
              + b1_ref[:, D_FF + c * q:D_FF + (c + 1) * q])
        glu = jnp.minimum(zg, SWIGLU_LIMIT)
        lin = jnp.clip(zl, -SWIGLU_LIMIT, SWIGLU_LIMIT)
        act_ref[:, c * q:(c + 1) * q] = (glu * jax.nn.sigmoid(SWIGLU_ALPHA * glu) * (lin + 1.0)).astype(BF16)
    return jnp.dot(act_ref[...], w2b[...], preferred_element_type=F32) + b2_ref[...]


def _moe_kernel(meta_ref, xs_hbm, w1_ref, b1_ref, w2_ref, b2_ref, ys_hbm, xin, yout, w1b, w2b, act, isem, osem):
    e = pl.program_id(0)
    ch = MOE_CH
    tm = MOE_TM
    n_blocks = ys_hbm.shape[0] // tm

    chunk = 128
    def cast_body(c, carry):
        k0 = pl.multiple_of(c * chunk, chunk)
        w1b[pl.ds(k0, chunk), :] = w1_ref[pl.ds(k0, chunk), :].astype(BF16)
        w2b[pl.ds(k0, chunk), :] = w2_ref[pl.ds(k0, chunk), :].astype(BF16)
        return carry
    lax.fori_loop(0, D_MODEL // chunk, cast_body, 0)

    r0 = jnp.where(e == 0, 0, meta_ref[N_EXPERTS + jnp.maximum(e - 1, 0)])
    n_tm = lax.shift_right_logical(meta_ref[N_EXPERTS + e] - r0, MOE_TM.bit_length() - 1)
    nc = lax.shift_right_logical(n_tm, 1)
    has_tail = (n_tm & 1) == 1

    def rows(c):
        return pl.ds(pl.multiple_of(r0 + c * ch, tm), ch)

    def in_copy(c, s):
        return pltpu.make_async_copy(xs_hbm.at[rows(c)], xin.at[s], isem.at[s])

    def out_copy(c, s):
        return pltpu.make_async_copy(yout.at[s], ys_hbm.at[rows(c)], osem.at[s])

    @pl.when(nc > 0)
    def _():
        in_copy(0, 0).start()

    def body(c, carry):
        s = c % 2
        in_copy(c, s).wait()

        @pl.when(c + 1 < nc)
        def _():
            in_copy(c + 1, 1 - s).start()

        @pl.when(c >= 2)
        def _():
            out_copy(c - 2, s).wait()

        yout[s] = _expert_mlp(xin[s], w1b, b1_ref, w2b, b2_ref, act)
        out_copy(c, s).start()
        return carry
    lax.fori_loop(0, nc, body, 0)

    @pl.when(nc >= 2)
    def _():
        out_copy(nc - 2, nc % 2).wait()

    @pl.when(nc >= 1)
    def _():
        out_copy(nc - 1, (nc - 1) % 2).wait()

    @pl.when(has_tail)
    def _():
        tail_rows = pl.ds(pl.multiple_of(r0 + nc * ch, tm), tm)
        cin = pltpu.make_async_copy(xs_hbm.at[tail_rows], xin.at[0, pl.ds(0, tm)], isem.at[0])
        cin.start()
        cin.wait()
        yout[0, 0:tm] = _expert_mlp(xin[0, 0:tm], w1b, b1_ref, w2b, b2_ref, act.at[pl.ds(0, tm)])
        cout = pltpu.make_async_copy(yout.at[0, pl.ds(0, tm)], ys_hbm.at[tail_rows], osem.at[0])
        cout.start()
        cout.wait()

    @pl.when(e == pl.num_programs(0) - 1)
    def _():
        yout[1, 0:tm] = jnp.zeros((tm, D_MODEL), F32)

        def zero_block(j):
            return pltpu.make_async_copy(yout.at[1, pl.ds(0, tm)], ys_hbm.at[pl.ds(pl.multiple_of(j * tm, tm), tm)],
                                         osem.at[1])

        def start_body(j, carry):
            zero_block(j).start()
            return carry

        def wait_body(j, carry):
            zero_block(j).wait()
            return carry
        lax.fori_loop(meta_ref[2 * N_EXPERTS], n_blocks, start_body, 0)
        lax.fori_loop(meta_ref[2 * N_EXPERTS], n_blocks, wait_body, 0)


def _moe(meta, xs, w1, b1, w2, b2):
    grid_spec = pltpu.PrefetchScalarGridSpec(
        num_scalar_prefetch=1,
        grid=(N_EXPERTS,),
        in_specs=[
            pl.BlockSpec(memory_space=pl.ANY),
            pl.BlockSpec((None, D_MODEL, 2 * D_FF), lambda e, m: (e, 0, 0)),
            pl.BlockSpec((None, 1, 2 * D_FF), lambda e, m: (e, 0, 0)),
            pl.BlockSpec((None, D_FF, D_MODEL), lambda e, m: (e, 0, 0)),
            pl.BlockSpec((None, 1, D_MODEL), lambda e, m: (e, 0, 0)),
        ],
        out_specs=pl.BlockSpec(memory_space=pl.ANY),
        scratch_shapes=[pltpu.VMEM((2, MOE_CH, D_MODEL), F32), pltpu.VMEM((2, MOE_CH, D_MODEL), F32),
                        pltpu.VMEM((D_MODEL, 2 * D_FF), BF16), pltpu.VMEM((D_FF, D_MODEL), BF16),
                        pltpu.VMEM((MOE_CH, D_FF), BF16),
                        pltpu.SemaphoreType.DMA((2,)), pltpu.SemaphoreType.DMA((2,))],
    )
    return pl.pallas_call(
        _moe_kernel,
        grid_spec=grid_spec,
        out_shape=jax.ShapeDtypeStruct(xs.shape, F32),
        compiler_params=pltpu.CompilerParams(dimension_semantics=("arbitrary",), vmem_limit_bytes=VMEM_LIMIT),
        name="moe",
    )(meta, xs, w1, b1, w2, b2)


def _combine_kernel(n_prompt_tiles, slotc_ref, slotn_ref, ys_hbm, route_ref, x1_ref, modp_ref, g2s_ref, fg_ref, op_ref,
                    os_ref, buf, sem):
    i = pl.program_id(0)
    nb = pl.num_programs(0)
    slot = i % 2
    tt = COMB_TT

    def issue(slot_ref, s):
        def body(t, carry):
            for kk in range(TOP_K):
                src = slot_ref[0, 0, t * TOP_K + kk]
                pltpu.make_async_copy(ys_hbm.at[pl.ds(src, 1)], buf.at[s, pl.ds(kk * tt + t, 1)],
                                      sem.at[s]).start(priority=kk % 2)
            return carry
        lax.fori_loop(0, tt, body, 0, unroll=4)

    @pl.when(i == 0)
    def _():
        issue(slotc_ref, 0)

    @pl.when(i + 1 < nb)
    def _():
        issue(slotn_ref, 1 - slot)

    pltpu.make_async_copy(ys_hbm.at[pl.ds(0, TOP_K * tt)], buf.at[slot], sem.at[slot]).wait()

    route = route_ref[...]
    ff = route[:, ROUTE_G:ROUTE_G + 1] * buf[slot, 0:tt, :]
    for kk in range(1, TOP_K):
        ff = ff + route[:, ROUTE_G + kk:ROUTE_G + kk + 1] * buf[slot, kk * tt:(kk + 1) * tt, :]
    g2 = jnp.where(i < n_prompt_tiles, modp_ref[5:6, :], g2s_ref[...])
    x = x1_ref[...] + g2 * ff
    y = _rms(x, fg_ref[...])

    @pl.when(i < n_prompt_tiles)
    def _():
        op_ref[...] = y

    @pl.when(i >= n_prompt_tiles)
    def _():
        os_ref[...] = y


def _combine(slot_blocks, ys, route, x1_all, mod_p, g2_rows, final_g, n_prompt_tiles, tiles_per_batch):
    nt = slot_blocks.shape[0]
    tt = COMB_TT
    npt = n_prompt_tiles
    return pl.pallas_call(
        functools.partial(_combine_kernel, npt),
        grid=(nt,),
        in_specs=[
            pl.BlockSpec((1, 1, TOP_K * tt), lambda i: (i, 0, 0), memory_space=pltpu.SMEM),
            pl.BlockSpec((1, 1, TOP_K * tt), lambda i: (jnp.minimum(i + 1, nt - 1), 0, 0), memory_space=pltpu.SMEM),
            pl.BlockSpec(memory_space=pl.ANY),
            pl.BlockSpec((tt, LANES), lambda i: (i, 0)),
            pl.BlockSpec((tt, D_MODEL), lambda i: (i, 0)),
            pl.BlockSpec((None, 6, D_MODEL), lambda i: (jnp.minimum(i, npt - 1) // tiles_per_batch, 0, 0)),
            pl.BlockSpec((tt, D_MODEL), lambda i: (jnp.maximum(i - npt, 0), 0)),
            pl.BlockSpec((1, D_MODEL), lambda i: (0, 0)),
        ],
        out_specs=(pl.BlockSpec((tt, D_MODEL), lambda i: (jnp.minimum(i, npt - 1), 0)),
                   pl.BlockSpec((tt, D_MODEL), lambda i: (jnp.maximum(i - npt, 0), 0))),
        out_shape=(jax.ShapeDtypeStruct((npt * tt, D_MODEL), F32),
                   jax.ShapeDtypeStruct(((nt - npt) * tt, D_MODEL), F32)),
        scratch_shapes=[pltpu.VMEM((2, TOP_K * tt, D_MODEL), F32), pltpu.SemaphoreType.DMA((2,))],
        compiler_params=pltpu.CompilerParams(dimension_semantics=("arbitrary",), vmem_limit_bytes=VMEM_LIMIT),
        name="combine",
    )(slot_blocks, slot_blocks, ys, route, x1_all, mod_p, g2_rows, final_g)


def _block_diag_halves(wg):
    halves = []
    for hh in range(2):
        rows = []
        for bi in range(4):
            row = [wg[hh * 4 + bi] if bj == bi else jnp.zeros((LRU_BLOCK_W, LRU_BLOCK_W), wg.dtype) for bj in range(4)]
            rows.append(jnp.concatenate(row, axis=1))
        halves.append(jnp.concatenate(rows, axis=0))
    return jnp.stack(halves).astype(BF16)


def _rope_tables(pos):
    half = HEAD_DIM // 2
    inv = ROPE_THETA ** (-jnp.arange(half, dtype=F32) / half)
    ang = pos.astype(F32)[:, None] * inv[None, :]
    cos = jnp.cos(ang)
    sin = jnp.sin(ang)
    cos128 = jnp.concatenate([cos, cos, cos, cos], axis=1)
    sin128 = jnp.concatenate([-sin, sin, -sin, sin], axis=1)
    return cos128, sin128


def _full_spec(arr, grid_rank):
    zeros = (0,) * arr.ndim
    if grid_rank == 1:
        return pl.BlockSpec(arr.shape, lambda i: zeros)
    return pl.BlockSpec(arr.shape, lambda b, j: zeros)


def kernel(x_prompt, x_sample, state_lru_h, state_conv, cache_win_k, cache_win_v, c_prompt, c_sample, w_ada, b_ada, norm1_g, w_in, b_in, conv_w, conv_b, lru_wa, lru_ba, lru_wx, lru_bx, lru_lambda, attn_sinks, w_out, b_out, norm2_g, w_router, b_router, w1, b1, w2, b2, final_g):
    bp, seq, _ = x_prompt.shape
    bd, tdec, _ = x_sample.shape
    assert tdec == SUBLANES and seq % SEQ_TILE == 0 and bd % SAMPLE_BT == 0
    n_prompt = bp * seq
    n_sample = bd * tdec
    n_tok = n_prompt + n_sample
    l = 0

    head_perm = jnp.array([h for c in range(4) for h in (c, GROUP + c)], dtype=jnp.int32)
    qcol_perm = (head_perm[:, None] * HEAD_DIM + jnp.arange(HEAD_DIM, dtype=jnp.int32)[None, :]).reshape(-1)
    o2 = 2 * LRU_WIDTH
    in_perm = jnp.concatenate([jnp.arange(o2, dtype=jnp.int32), o2 + qcol_perm,
                               jnp.arange(o2 + Q_WIDTH, IN_WIDTH, dtype=jnp.int32)])
    out_perm = jnp.concatenate([jnp.arange(LRU_WIDTH, dtype=jnp.int32), LRU_WIDTH + qcol_perm])
    sinks_perm = attn_sinks[l]

    wr = jnp.tile(w_router[l], (1, TOP_K))
    wr_hi = wr.astype(BF16)
    weights = dict(
        norm1_g=norm1_g[l][None, :], w_in=w_in[l][:, in_perm].astype(BF16), b_in=b_in[l][in_perm][None, :],
        conv_w=conv_w[l], conv_b=conv_b[l][None, :],
        ga=_block_diag_halves(lru_wa[l]), gx=_block_diag_halves(lru_wx[l]),
        lru_ba=lru_ba[l][None, :], lru_bx=lru_bx[l][None, :], lam=lru_lambda[l][None, :],
        w_out=w_out[l][out_perm, :].astype(BF16), b_out=b_out[l][None, :], norm2_g=norm2_g[l][None, :],
        wr_hi=wr_hi, wr_lo=(wr - wr_hi.astype(F32)).astype(BF16),
        b_router=jnp.tile(b_router[l], TOP_K)[None, :],
    )
    wlist = [weights[n] for n in WEIGHT_NAMES]

    mod_all = _ada(jnp.concatenate([c_prompt, c_sample], axis=0), w_ada[l], b_ada[l][None, :])
    mod_p = mod_all[:bp].reshape(bp, 6, D_MODEL)
    mod_s = mod_all[bp:].reshape(bd, 6, D_MODEL).transpose(1, 0, 2)

    cos_p, sin_p = _rope_tables(jnp.arange(seq, dtype=jnp.int32))
    cos_s, sin_s = _rope_tables(PAST_LEN + jnp.arange(tdec, dtype=jnp.int32))
    cos_s = jnp.tile(cos_s, (SAMPLE_BT, 1))
    sin_s = jnp.tile(sin_s, (SAMPLE_BT, 1))
    h0_rows = jnp.pad(state_lru_h[l][:, None, :], ((0, 0), (0, tdec - 1), (0, 0))).reshape(n_sample, LRU_WIDTH)
    cprev_rows = jnp.pad(state_conv[l], ((0, 0), (0, tdec - (CONV_W - 1)), (0, 0))).reshape(n_sample, LRU_WIDTH)
    ck = cache_win_k[l].reshape(bd, WINDOW, KV_WIDTH)
    cv = cache_win_v[l].reshape(bd, WINDOW, KV_WIDTH)
    nj = seq // SEQ_TILE
    npt = n_prompt // SEQ_TILE
    nst = n_sample // SEQ_TILE
    p_tile = lambda i: jnp.minimum(i, npt - 1)
    s_tile = lambda i: jnp.maximum(i - npt, 0)
    all_rows = lambda width: pl.BlockSpec((SEQ_TILE, width), lambda i: (i, 0))
    p_rows = lambda width: pl.BlockSpec((SEQ_TILE, width), lambda i: (p_tile(i), 0))
    s_rows = lambda width: pl.BlockSpec((SEQ_TILE, width), lambda i: (s_tile(i), 0))
    p_tail = lambda rows, width: pl.BlockSpec((None, rows, width), lambda i: (p_tile(i) // nj, 0, 0))
    cache_spec = pl.BlockSpec((SAMPLE_BT, WINDOW, KV_WIDTH), lambda i: (s_tile(i), 0, 0))
    mixer_out_shapes = (
        jax.ShapeDtypeStruct((n_tok, D_MODEL), F32),
        jax.ShapeDtypeStruct((n_tok, D_MODEL), F32),
        jax.ShapeDtypeStruct((n_tok, LANES), F32),
        jax.ShapeDtypeStruct((npt + nst, SUBLANES, LANES), F32),
        jax.ShapeDtypeStruct((bp, SUBLANES, LRU_WIDTH), F32),
        jax.ShapeDtypeStruct((bp, SUBLANES, LRU_WIDTH), F32),
        jax.ShapeDtypeStruct((bp, WINDOW, KV_WIDTH), F32),
        jax.ShapeDtypeStruct((bp, WINDOW, KV_WIDTH), F32),
        jax.ShapeDtypeStruct((n_sample, D_MODEL), F32),
        jax.ShapeDtypeStruct((n_sample, LRU_WIDTH), F32),
        jax.ShapeDtypeStruct((n_sample, LRU_WIDTH), F32),
        jax.ShapeDtypeStruct((bd, WINDOW, KV_WIDTH), F32),
        jax.ShapeDtypeStruct((bd, WINDOW, KV_WIDTH), F32),
    )
    (x1_all, h2_all, route, tile_cnt, hlast_p, ulast_p, klast_p, vlast_p, g2_rows, hs_s, u_s, s_k, s_v) = pl.pallas_call(
        functools.partial(_mixer_kernel, npt, nj),
        grid=(npt + nst,),
        in_specs=[p_rows(D_MODEL),
                  pl.BlockSpec((None, 6, D_MODEL), lambda i: (p_tile(i) // nj, 0, 0)),
                  pl.BlockSpec((SEQ_TILE, LANES), lambda i: (p_tile(i) % nj, 0)),
                  pl.BlockSpec((SEQ_TILE, LANES), lambda i: (p_tile(i) % nj, 0)),
                  s_rows(D_MODEL),
                  pl.BlockSpec((6, SAMPLE_BT, D_MODEL), lambda i: (0, s_tile(i), 0)),
                  pl.BlockSpec((SEQ_TILE, LANES), lambda i: (0, 0)),
                  pl.BlockSpec((SEQ_TILE, LANES), lambda i: (0, 0)),
                  pl.BlockSpec(memory_space=pltpu.SMEM),
                  s_rows(LRU_WIDTH), s_rows(LRU_WIDTH), cache_spec, cache_spec]
                 + [_full_spec(a, 1) for a in wlist],
        out_specs=(all_rows(D_MODEL), all_rows(D_MODEL), all_rows(LANES),
                   pl.BlockSpec((None, SUBLANES, LANES), lambda i: (i, 0, 0)),
                   p_tail(SUBLANES, LRU_WIDTH), p_tail(SUBLANES, LRU_WIDTH),
                   p_tail(WINDOW, KV_WIDTH), p_tail(WINDOW, KV_WIDTH),
                   s_rows(D_MODEL), s_rows(LRU_WIDTH), s_rows(LRU_WIDTH), cache_spec, cache_spec),
        out_shape=mixer_out_shapes,
        scratch_shapes=[pltpu.VMEM((SUBLANES, LRU_WIDTH), F32), pltpu.VMEM((SUBLANES, LRU_WIDTH), F32),
                        pltpu.VMEM((WINDOW, KV_WIDTH), F32), pltpu.VMEM((WINDOW, KV_WIDTH), F32)],
        compiler_params=pltpu.CompilerParams(dimension_semantics=("arbitrary",), vmem_limit_bytes=VMEM_LIMIT),
        name="mixer",
    )(x_prompt.reshape(n_prompt, D_MODEL), mod_p, cos_p, sin_p,
      x_sample.reshape(n_sample, D_MODEL), mod_s, cos_s, sin_s, sinks_perm, h0_rows, cprev_rows, ck, cv, *wlist)

    n_tiles = npt + nst
    n_assign = n_tok * TOP_K
    n_blocks = -(-(n_assign + N_EXPERTS * (MOE_TM - 1)) // MOE_TM)
    cnt = tile_cnt[:, 0, :N_EXPERTS].astype(jnp.int32)
    counts = jnp.sum(cnt, axis=0)
    pcounts = (counts + MOE_TM - 1) // MOE_TM * MOE_TM
    pend = jnp.cumsum(pcounts)
    pstart = pend - pcounts
    tile_off = pstart[None, :] + jnp.cumsum(cnt, axis=0) - cnt
    e_ids = route[:, ROUTE_E:ROUTE_E + TOP_K].astype(jnp.int32).reshape(n_tiles, SEQ_TILE, TOP_K)
    ranks = route[:, ROUTE_R:ROUTE_R + TOP_K].astype(jnp.int32).reshape(n_tiles, SEQ_TILE, TOP_K)
    hit = e_ids[..., None] == jnp.arange(N_EXPERTS, dtype=jnp.int32)
    dest = jnp.sum(jnp.where(hit, tile_off[:, None, None, :], 0), axis=-1) + ranks
    dest_blocks = dest.reshape(n_tiles, 1, SEQ_TILE * TOP_K)
    meta =jnp.concatenate([pstart + counts, pend, pend[-1:] // MOE_TM]).astype(jnp.int32)

    xs = _dispatch(meta, dest_blocks, h2_all, n_blocks * MOE_TM)
    ys = _moe(meta, xs, w1[l], b1[l][:, None, :], w2[l], b2[l][:, None, :])
    y_p, y_s = _combine(dest_blocks, ys, route, x1_all, mod_p, g2_rows,
                        final_g[None, :], n_prompt // COMB_TT, seq // COMB_TT)

    y_prompt = y_p.reshape(bp, seq, D_MODEL)
    y_sample = y_s.reshape(bd, tdec, D_MODEL)
    p_h = hlast_p[:, SUBLANES - 1, :][None]
    p_c = ulast_p[:, SUBLANES - (CONV_W - 1):, :][None]
    p_k = klast_p.reshape(1, bp, WINDOW, N_KV_HEADS, HEAD_DIM)
    p_v = vlast_p.reshape(1, bp, WINDOW, N_KV_HEADS, HEAD_DIM)
    s_h = hs_s.reshape(bd, tdec, LRU_WIDTH)[:, tdec - 1, :][None]
    s_c = u_s.reshape(bd, tdec, LRU_WIDTH)[:, tdec - (CONV_W - 1):, :][None]
    s_kk = s_k.reshape(1, bd, WINDOW, N_KV_HEADS, HEAD_DIM)
    s_vv = s_v.reshape(1, bd, WINDOW, N_KV_HEADS, HEAD_DIM)
    return (y_prompt, y_sample, p_h, p_c, p_k, p_v, s_h, s_c, s_kk, s_vv)
```

```python
import functools

import jax
import jax.numpy as jnp
from jax import lax
from jax.experimental import pallas as pl
from jax.experimental.pallas import tpu as pltpu

F32 = jnp.float32
BF16 = jnp.bfloat16

D_MODEL = 1024
LRU_WIDTH = 512
LRU_BLOCKS = 8
LRU_BLOCK_W = LRU_WIDTH // LRU_BLOCKS
CONV_W = 4
LRU_C = 8.0
HEAD_DIM = 64
N_HEADS = 8
N_KV_HEADS = 2
GROUP = N_HEADS // N_KV_HEADS
WINDOW = 128
ROPE_THETA = 10000.0
N_EXPERTS = 32
TOP_K = 4
D_FF = D_MODEL
SWIGLU_LIMIT = 7.0
SWIGLU_ALPHA = 1.702
NORM_EPS = 1e-5
PAST_LEN = 8192
Q_WIDTH = N_HEADS * HEAD_DIM
KV_WIDTH = N_KV_HEADS * HEAD_DIM
IN_WIDTH = 2 * LRU_WIDTH + Q_WIDTH + 2 * KV_WIDTH

LANES = 128
SUBLANES = 8
SEQ_TILE = 256
SAMPLE_BT = 32
MOE_TM = 256
MOE_CH = 2 * MOE_TM
COMB_TT = 256
NEG_BIG = -1e30
VMEM_LIMIT = 56 * 1024 * 1024


def _rms(x, g):
    return x * lax.rsqrt(jnp.mean(x * x, axis=-1, keepdims=True) + NORM_EPS) * g


def _shift_rows(x, d, fill, rowid):
    return jnp.where(rowid >= d, pltpu.roll(x, d, axis=0), fill)


def _lin_scan(a, b, seg, rowid):
    d = 1
    while d < seg:
        a_s = _shift_rows(a, d, 1.0, rowid)
        b_s = _shift_rows(b, d, 0.0, rowid)
        b = a * b_s + b
        a = a * a_s
        d *= 2
    return a, b


def _rope128(x, cos, sin_signed, first_half):
    sw = jnp.where(first_half, pltpu.roll(x, LANES - HEAD_DIM // 2, axis=1), pltpu.roll(x, HEAD_DIM // 2, axis=1))
    return x * cos + sw * sin_signed


def _neg_expm1(y):
    return -(jnp.tanh(0.5 * y) * (jnp.exp(y) + 1.0))


def _softplus(x):
    return jnp.maximum(x, 0.0) + jnp.log1p(jnp.exp(-jnp.abs(x)))


def _lru_coeffs(uc, w, first_pos_mask):
    ub = uc.astype(BF16)
    half = LRU_WIDTH // 2
    ra = jnp.concatenate([jnp.dot(ub[:, :half], w['ga'][0], preferred_element_type=F32),
                          jnp.dot(ub[:, half:], w['ga'][1], preferred_element_type=F32)], axis=1)
    rx = jnp.concatenate([jnp.dot(ub[:, :half], w['gx'][0], preferred_element_type=F32),
                          jnp.dot(ub[:, half:], w['gx'][1], preferred_element_type=F32)], axis=1)
    r = jax.nn.sigmoid(ra + w['lru_ba'][...])
    i = jax.nn.sigmoid(rx + w['lru_bx'][...])
    log_a = -LRU_C * r * _softplus(-w['lam'][...])
    a = jnp.exp(log_a)
    mult = jnp.sqrt(_neg_expm1(2.0 * log_a))
    if first_pos_mask is not None:
        mult = jnp.where(first_pos_mask, 1.0, mult)
    return a, mult * i * uc


def _conv_taps(u, s1, s2, s3, w):
    cw = w['conv_w']
    return w['conv_b'][...] + s3 * cw[0:1, :] + s2 * cw[1:2, :] + s1 * cw[2:3, :] + u * cw[3:4, :]


def _in_proj(x, mod, w):
    sh1, sc1 = mod
    h = _rms(x, w['norm1_g'][...]) * (1.0 + sc1) + sh1
    return jnp.dot(h.astype(BF16), w['w_in'][...], preferred_element_type=F32) + w['b_in'][...]


def _post_mix(x, mix, mod, w):
    g1, sh2, sc2 = mod
    x1 = x + g1 * (jnp.dot(mix.astype(BF16), w['w_out'][...], preferred_element_type=F32) + w['b_out'][...])
    h2 = _rms(x1, w['norm2_g'][...]) * (1.0 + sc2) + sh2
    h2_hi = h2.astype(BF16)
    h2_lo = (h2 - h2_hi.astype(F32)).astype(BF16)
    logits = (jnp.dot(h2_hi, w['wr_hi'][...], preferred_element_type=F32)
              + jnp.dot(h2_lo, w['wr_hi'][...], preferred_element_type=F32)
              + jnp.dot(h2_hi, w['wr_lo'][...], preferred_element_type=F32)) + w['b_router'][...]
    return x1, h2, logits


ROUTE_E, ROUTE_G, ROUTE_R = 0, TOP_K, 2 * TOP_K


def _lane_roll1(v, shift):
    return pltpu.roll(jnp.broadcast_to(v, (SUBLANES, LANES)), shift, axis=1)[0:1]


def _route_tile(lg):
    rows = lg.shape[0]
    lane = lax.broadcasted_iota(jnp.int32, (1, LANES), 1)
    e_of = lane % N_EXPERTS
    grp = lane // N_EXPERTS
    onehot = jnp.zeros((rows, LANES), F32)
    vals, ids = [], []
    for k in range(TOP_K):
        m = jnp.max(lg, axis=1, keepdims=True)
        idx = jnp.min(jnp.where(lg == m, e_of, N_EXPERTS), axis=1, keepdims=True)
        sel = e_of == idx
        lg = jnp.where(sel, -jnp.inf, lg)
        onehot = jnp.where(jnp.logical_and(sel, grp == k), 1.0, onehot)
        vals.append(m)
        ids.append(idx)
    ex = [jnp.exp(v - vals[0]) for v in vals]
    denom = ex[0] + ex[1] + ex[2] + ex[3]

    r_i = lax.broadcasted_iota(jnp.int32, (rows, rows), 0)
    c_i = lax.broadcasted_iota(jnp.int32, (rows, rows), 1)
    strict_lower = jnp.where(r_i > c_i, 1.0, 0.0).astype(BF16)
    prefix = jnp.dot(strict_lower, onehot.astype(BF16), preferred_element_type=F32)
    cnt = jnp.sum(onehot, axis=0, keepdims=True)
    base = jnp.zeros((1, LANES), F32)
    tot = cnt
    for s in range(1, TOP_K):
        rolled = _lane_roll1(cnt, s * N_EXPERTS)
        base = base + jnp.where(lane >= s * N_EXPERTS, rolled, 0.0)
        tot = tot + rolled
    ranked = onehot * (prefix + base)

    route = jnp.zeros((rows, LANES), F32)
    for k in range(TOP_K):
        rank_k = jnp.sum(jnp.where(grp == k, ranked, 0.0), axis=1, keepdims=True)
        route = jnp.where(lane == ROUTE_E + k, ids[k].astype(F32), route)
        route = jnp.where(lane == ROUTE_G + k, ex[k] / denom, route)
        route = jnp.where(lane == ROUTE_R + k, rank_k, route)
    return route, tot


def _softmax_sink_pv(s, sink_col, v_bf16):
    m = jnp.maximum(jnp.max(s, axis=-1, keepdims=True), sink_col)
    p = jnp.exp(s - m)
    denom = jnp.sum(p, axis=-1, keepdims=True) + jnp.exp(sink_col - m)
    return p, denom


WEIGHT_NAMES = ('norm1_g', 'w_in', 'b_in', 'conv_w', 'conv_b', 'ga', 'gx', 'lru_ba', 'lru_bx', 'lam',
                'w_out', 'b_out', 'norm2_g', 'wr_hi', 'wr_lo', 'b_router')


def _ada_kernel(c_ref, w_ref, b_ref, o_ref):
    c = c_ref[...]
    s = c * jax.nn.sigmoid(c)
    o_ref[...] = jnp.dot(s, w_ref[...], preferred_element_type=F32, precision=lax.Precision.HIGHEST) + b_ref[...]


def _ada(c_all, w_ada, b_ada):
    rows = c_all.shape[0]
    return pl.pallas_call(
        _ada_kernel,
        grid=(6,),
        in_specs=[pl.BlockSpec((rows, D_MODEL), lambda i: (0, 0)),
                  pl.BlockSpec((D_MODEL, D_MODEL), lambda i: (0, i)),
                  pl.BlockSpec((1, D_MODEL), lambda i: (0, i))],
        out_specs=pl.BlockSpec((rows, D_MODEL), lambda i: (0, i)),
        out_shape=jax.ShapeDtypeStruct((rows, 6 * D_MODEL), F32),
        compiler_params=pltpu.CompilerParams(dimension_semantics=("arbitrary",), vmem_limit_bytes=VMEM_LIMIT),
        name="ada",
    )(c_all, w_ada, b_ada)


def _prompt_body(j, x_ref, mod_ref, cos_ref, sin_ref, sinks_ref, w, x1_ref, h2_ref, lg_ref,
                 hlast_ref, ulast_ref, klast_ref, vlast_ref, conv_c, h_c, k_c, v_c):
    ts = SEQ_TILE

    @pl.when(j == 0)
    def _():
        conv_c[...] = jnp.zeros_like(conv_c)
        h_c[...] = jnp.zeros_like(h_c)
        k_c[...] = jnp.zeros_like(k_c)
        v_c[...] = jnp.zeros_like(v_c)

    x = x_ref[...]
    mod = mod_ref[...]
    proj = _in_proj(x, (mod[0:1], mod[1:2]), w)
    u = proj[:, :LRU_WIDTH]
    gate = proj[:, LRU_WIDTH:2 * LRU_WIDTH]
    o2 = 2 * LRU_WIDTH

    rowid = lax.broadcasted_iota(jnp.int32, (ts, 1), 0)
    u_ext = jnp.concatenate([conv_c[...], u], axis=0)
    s1, s2, s3 = (pltpu.roll(u_ext, d, axis=0)[SUBLANES:] for d in (1, 2, 3))
    uc = _conv_taps(u, s1, s2, s3, w)
    conv_c[...] = u[ts - SUBLANES:]
    ulast_ref[...] = u[ts - SUBLANES:]

    first_pos = jnp.logical_and(rowid == 0, j == 0)
    a, bt = _lru_coeffs(uc, w, first_pos)
    a_cum, b_cum = _lin_scan(a, bt, ts, rowid)
    hs = b_cum + a_cum * h_c[0:1, :]
    h_tail = hs[ts - SUBLANES:]
    h_c[...] = jnp.broadcast_to(h_tail[SUBLANES - 1:SUBLANES, :], h_c.shape)
    hlast_ref[...] = h_tail
    lru_out = hs * jax.nn.gelu(gate)

    cos = cos_ref[...]
    sin = sin_ref[...]
    lane = lax.broadcasted_iota(jnp.int32, (1, LANES), 1)
    first_half = (lane % HEAD_DIM) < (HEAD_DIM // 2)
    qcols = [_rope128(proj[:, o2 + c * LANES:o2 + (c + 1) * LANES], cos, sin, first_half) for c in range(4)]
    k = _rope128(proj[:, o2 + Q_WIDTH:o2 + Q_WIDTH + KV_WIDTH], cos, sin, first_half)
    v = proj[:, o2 + Q_WIDTH + KV_WIDTH:]
    k_ext = jnp.concatenate([k_c[...], k], axis=0).astype(BF16)
    v_ext = jnp.concatenate([v_c[...], v], axis=0).astype(BF16)
    k_c[...] = k[ts - WINDOW:]
    v_c[...] = v[ts - WINDOW:]
    klast_ref[...] = k[ts - WINDOW:]
    vlast_ref[...] = v[ts - WINDOW:]

    qi = lax.broadcasted_iota(jnp.int32, (WINDOW, 2 * WINDOW), 0)
    kj = lax.broadcasted_iota(jnp.int32, (WINDOW, 2 * WINDOW), 1)
    band = jnp.logical_and(kj > qi, kj <= qi + WINDOW)
    lane_lo = lane < HEAD_DIM
    grow = lax.broadcasted_iota(jnp.int32, (GROUP * WINDOW, 1), 0) // WINDOW
    attn_cols = [[] for _ in range(4)]
    for blk in range(ts // WINDOW):
        prev_ok = jnp.logical_or(j > 0, blk > 0)
        mask = jnp.logical_and(band, jnp.logical_or(kj >= WINDOW, prev_ok))
        mask4 = jnp.concatenate([mask] * GROUP, axis=0)
        kb = k_ext[blk * WINDOW:(blk + 2) * WINDOW]
        vb = v_ext[blk * WINDOW:(blk + 2) * WINDOW]
        outs = []
        for kv in range(N_KV_HEADS):
            sel = lane_lo if kv == 0 else jnp.logical_not(lane_lo)
            qs = jnp.concatenate(
                [jnp.where(sel, qc[blk * WINDOW:(blk + 1) * WINDOW], 0.0) for qc in qcols], axis=0).astype(BF16)
            s = lax.dot_general(qs, kb, (((1,), (1,)), ((), ())), preferred_element_type=F32) * (HEAD_DIM ** -0.5)
            s = jnp.where(mask4, s, NEG_BIG)
            sink_col = jnp.zeros((GROUP * WINDOW, 1), F32)
            for g in range(GROUP):
                sink_col = jnp.where(grow == g, sinks_ref[kv * GROUP + g], sink_col)
            p, denom = _softmax_sink_pv(s, sink_col, vb)
            outs.append(jnp.dot(p.astype(BF16), vb, preferred_element_type=F32) / denom)
        for c in range(4):
            attn_cols[c].append(jnp.where(lane_lo, outs[0][c * WINDOW:(c + 1) * WINDOW],
                                          outs[1][c * WINDOW:(c + 1) * WINDOW]))
    attn = jnp.concatenate([jnp.concatenate(cols, axis=0) for cols in attn_cols], axis=1)

    mix = jnp.concatenate([lru_out, attn], axis=1)
    x1, h2, logits = _post_mix(x, mix, (mod[2:3], mod[3:4], mod[4:5]), w)
    x1_ref[...] = x1
    h2_ref[...] = h2
    route, tot = _route_tile(logits)
    lg_ref[0][...] = route
    lg_ref[1][...] = jnp.broadcast_to(tot, lg_ref[1].shape)


def _expand_rows(m, t):
    b, wd = m.shape
    return jnp.broadcast_to(m[:, None, :], (b, t, wd)).reshape(b * t, wd)


def _sample_body(x_ref, mod_ref, cos_ref, sin_ref, sinks_ref, h0_ref, cprev_ref, ck_ref, cv_ref, w,
                 x1_ref, h2_ref, lg_ref, g2_ref, hs_ref, u_ref, ko_ref, vo_ref):
    bt_, t = SAMPLE_BT, SUBLANES
    rows = bt_ * t

    x = x_ref[...]
    mods = [_expand_rows(mod_ref[i], t) for i in range(6)]
    proj = _in_proj(x, (mods[0], mods[1]), w)
    u = proj[:, :LRU_WIDTH]
    gate = proj[:, LRU_WIDTH:2 * LRU_WIDTH]
    o2 = 2 * LRU_WIDTH
    u_ref[...] = u

    rowid = lax.broadcasted_iota(jnp.int32, (rows, 1), 0) % t
    cprev = cprev_ref[...]
    taps = []
    for d in (1, 2, 3):
        taps.append(jnp.where(rowid >= d, pltpu.roll(u, d, axis=0),
                              pltpu.roll(cprev, (d - (CONV_W - 1)) % rows, axis=0)))
    uc = _conv_taps(u, taps[0], taps[1], taps[2], w)

    a, bt = _lru_coeffs(uc, w, None)
    bt = bt + a * h0_ref[...]
    _, hs = _lin_scan(a, bt, t, rowid)
    hs_ref[...] = hs
    lru_out = hs * jax.nn.gelu(gate)

    cos = cos_ref[...]
    sin = sin_ref[...]
    lane = lax.broadcasted_iota(jnp.int32, (1, LANES), 1)
    first_half = (lane % HEAD_DIM) < (HEAD_DIM // 2)
    qcols = [_rope128(proj[:, o2 + c * LANES:o2 + (c + 1) * LANES], cos, sin, first_half) for c in range(4)]
    k = _rope128(proj[:, o2 + Q_WIDTH:o2 + Q_WIDTH + KV_WIDTH], cos, sin, first_half)
    v = proj[:, o2 + Q_WIDTH + KV_WIDTH:]
    k3 = k.reshape(bt_, t, KV_WIDTH)
    v3 = v.reshape(bt_, t, KV_WIDTH)
    ck = ck_ref[...]
    cv = cv_ref[...]
    ko_ref[:, :WINDOW - t, :] = ck[:, t:, :]
    ko_ref[:, WINDOW - t:, :] = k3
    vo_ref[:, :WINDOW - t, :] = cv[:, t:, :]
    vo_ref[:, WINDOW - t:, :] = v3

    ckb, cvb, k3b, v3b = ck.astype(BF16), cv.astype(BF16), k3.astype(BF16), v3.astype(BF16)
    lane_lo = lane < HEAD_DIM
    gq = GROUP * t
    tq = lax.broadcasted_iota(jnp.int32, (1, gq, 1), 1) % t
    mask_c = lax.broadcasted_iota(jnp.int32, (1, gq, WINDOW), 2) > tq
    mask_n = lax.broadcasted_iota(jnp.int32, (1, gq, t), 2) <= tq
    grow = lax.broadcasted_iota(jnp.int32, (1, gq, 1), 1) // t
    bdims = (((2,), (2,)), ((0,), (0,)))
    pdims = (((2,), (1,)), ((0,), (0,)))
    outs = []
    for kv in range(N_KV_HEADS):
        sel = lane_lo if kv == 0 else jnp.logical_not(lane_lo)
        q3 = jnp.concatenate([jnp.where(sel, qc, 0.0).reshape(bt_, t, LANES) for qc in qcols], axis=1).astype(BF16)
        sc = lax.dot_general(q3, ckb, bdims, preferred_element_type=F32) * (HEAD_DIM ** -0.5)
        sn = lax.dot_general(q3, k3b, bdims, preferred_element_type=F32) * (HEAD_DIM ** -0.5)
        sc = jnp.where(mask_c, sc, NEG_BIG)
        sn = jnp.where(mask_n, sn, NEG_BIG)
        sink_col = jnp.zeros((1, gq, 1), F32)
        for g in range(GROUP):
            sink_col = jnp.where(grow == g, sinks_ref[kv * GROUP + g], sink_col)
        m = jnp.maximum(jnp.maximum(jnp.max(sc, axis=-1, keepdims=True), jnp.max(sn, axis=-1, keepdims=True)),
                        sink_col)
        pc = jnp.exp(sc - m)
        pn = jnp.exp(sn - m)
        denom = jnp.sum(pc, axis=-1, keepdims=True) + jnp.sum(pn, axis=-1, keepdims=True) + jnp.exp(sink_col - m)
        o = (lax.dot_general(pc.astype(BF16), cvb, pdims, preferred_element_type=F32)
             + lax.dot_general(pn.astype(BF16), v3b, pdims, preferred_element_type=F32)) / denom
        outs.append(o)
    attn = jnp.concatenate(
        [jnp.where(lane_lo, outs[0][:, c * t:(c + 1) * t, :], outs[1][:, c * t:(c + 1) * t, :]).reshape(rows, LANES)
         for c in range(4)], axis=1)

    mix = jnp.concatenate([lru_out, attn], axis=1)
    x1, h2, logits = _post_mix(x, mix, (mods[2], mods[3], mods[4]), w)
    x1_ref[...] = x1
    h2_ref[...] = h2
    route, tot = _route_tile(logits)
    lg_ref[0][...] = route
    lg_ref[1][...] = jnp.broadcast_to(tot, lg_ref[1].shape)
    g2_ref[...] = mods[5]


def _mixer_kernel(n_prompt_tiles, tiles_per_seq,
                  xp_ref, modp_ref, cosp_ref, sinp_ref, xs_ref, mods_ref, coss_ref, sins_ref, sinks_ref,
                  h0_ref, cprev_ref, ck_ref, cv_ref, *rest):
    nw = len(WEIGHT_NAMES)
    w = dict(zip(WEIGHT_NAMES, rest[:nw]))
    (x1_ref, h2_ref, route_ref, cnt_ref, hlast_ref, ulast_ref, klast_ref, vlast_ref,
     g2_ref, hs_ref, u_ref, ko_ref, vo_ref, conv_c, h_c, k_c, v_c) = rest[nw:]
    lg_ref = (route_ref, cnt_ref)
    i = pl.program_id(0)

    @pl.when(i < n_prompt_tiles)
    def _():
        _prompt_body(i % tiles_per_seq, xp_ref, modp_ref, cosp_ref, sinp_ref, sinks_ref, w, x1_ref, h2_ref, lg_ref,
                     hlast_ref, ulast_ref, klast_ref, vlast_ref, conv_c, h_c, k_c, v_c)

    @pl.when(i >= n_prompt_tiles)
    def _():
        _sample_body(xs_ref, mods_ref, coss_ref, sins_ref, sinks_ref, h0_ref, cprev_ref, ck_ref, cv_ref, w,
                     x1_ref, h2_ref, lg_ref, g2_ref, hs_ref, u_ref, ko_ref, vo_ref)


def _dispatch_kernel(meta_ref, dest_ref, h2_ref, xs_hbm, xbuf, zblk, sem, zsem):
    i = pl.program_id(0)
    nb = pl.num_programs(0)
    slot = i % 2
    tt = SEQ_TILE
    n_blocks = xs_hbm.shape[0] // MOE_TM

    xbuf[slot] = h2_ref[...]

    def body(t, carry):
        for kk in range(TOP_K):
            dst = dest_ref[0, 0, t * TOP_K + kk]
            pltpu.make_async_copy(xbuf.at[slot, pl.ds(t, 1)], xs_hbm.at[pl.ds(dst, 1)],
                                  sem.at[slot]).start(priority=kk % 2)
        return carry
    lax.fori_loop(0, tt, body, 0, unroll=4)

    def wait_tile(s):
        for _ in range(TOP_K):
            pltpu.make_async_copy(xbuf.at[s], xs_hbm.at[pl.ds(0, tt)], sem.at[s]).wait()

    @pl.when(i > 0)
    def _():
        wait_tile(1 - slot)

    @pl.when(i == nb - 1)
    def _():
        wait_tile(slot)
        zblk[...] = jnp.zeros_like(zblk)

        def zero_row(r):
            return pltpu.make_async_copy(zblk.at[pl.ds(0, 1)], xs_hbm.at[pl.ds(r, 1)], zsem.at[0])

        def zero_block(j):
            return pltpu.make_async_copy(zblk, xs_hbm.at[pl.ds(pl.multiple_of(j * MOE_TM, MOE_TM), MOE_TM)], zsem.at[0])

        def for_tail_rows(fn):
            def e_body(e, carry):
                def r_body(r, c2):
                    fn(zero_row(r))
                    return c2
                return lax.fori_loop(meta_ref[e], meta_ref[N_EXPERTS + e], r_body, carry)
            lax.fori_loop(0, N_EXPERTS, e_body, 0)

        def for_tail_blocks(fn):
            def b_body(j, carry):
                fn(zero_block(j))
                return carry
            lax.fori_loop(meta_ref[2 * N_EXPERTS], n_blocks, b_body, 0)

        for_tail_rows(lambda cp: cp.start())
        for_tail_blocks(lambda cp: cp.start())
        for_tail_rows(lambda cp: cp.wait())
        for_tail_blocks(lambda cp: cp.wait())


def _dispatch(meta, dest_blocks, h2_all, n_rows):
    nt = dest_blocks.shape[0]
    tt = SEQ_TILE
    grid_spec = pltpu.PrefetchScalarGridSpec(
        num_scalar_prefetch=1,
        grid=(nt,),
        in_specs=[pl.BlockSpec((1, 1, TOP_K * tt), lambda i, m: (i, 0, 0), memory_space=pltpu.SMEM),
                  pl.BlockSpec((tt, D_MODEL), lambda i, m: (i, 0))],
        out_specs=pl.BlockSpec(memory_space=pl.ANY),
        scratch_shapes=[pltpu.VMEM((2, tt, D_MODEL), F32), pltpu.VMEM((MOE_TM, D_MODEL), F32),
                        pltpu.SemaphoreType.DMA((2,)), pltpu.SemaphoreType.DMA((1,))],
    )
    return pl.pallas_call(
        _dispatch_kernel,
        grid_spec=grid_spec,
        out_shape=jax.ShapeDtypeStruct((n_rows, D_MODEL), F32),
        compiler_params=pltpu.CompilerParams(dimension_semantics=("arbitrary",), vmem_limit_bytes=VMEM_LIMIT),
        name="dispatch",
    )(meta, dest_blocks, h2_all)


def _expert_mlp(x, w1b, b1_ref, w2b, b2_ref, act_ref):
    xb = x.astype(BF16)
    q = D_FF // 4
    for c in range(4):
        zg = jnp.dot(xb, w1b[:, c * q:(c + 1) * q], preferred_element_type=F32) + b1_ref[:, c * q:(c + 1) * q]
        zl = (jnp.dot(xb, w1b[:, D_FF + c * q:D_FF + (c + 1) * q], preferred_element_type=F32)
              + b1_ref[:, D_FF + c * q:D_FF + (c + 1) * q])
        glu = jnp.minimum(zg, SWIGLU_LIMIT)
        lin = jnp.clip(zl, -SWIGLU_LIMIT, SWIGLU_LIMIT)
        act_ref[:, c * q:(c + 1) * q] = (glu * jax.nn.sigmoid(SWIGLU_ALPHA * glu) * (lin + 1.0)).astype(BF16)
    return jnp.dot(act_ref[...], w2b[...], preferred_element_type=F32) + b2_ref[...]


def _moe_kernel(row_ref, size_ref, exp_ref, meta_ref, xs_hbm, w1_ref, b1_ref, w2_ref, b2_ref, ys_hbm,
                xin, yout, w1b, w2b, act, isem, osem):
    i = pl.program_id(0)
    n = pl.num_programs(0)
    slot = i % 2
    tm = MOE_TM
    n_blocks = ys_hbm.shape[0] // tm

    def for_chunk(j, full_fn, half_fn):
        r = pl.multiple_of(row_ref[j], tm)

        @pl.when(size_ref[j] == 2)
        def _():
            full_fn(r)

        @pl.when(size_ref[j] == 1)
        def _():
            half_fn(r)

    def in_full(r, s):
        return pltpu.make_async_copy(xs_hbm.at[pl.ds(r, MOE_CH)], xin.at[s], isem.at[s])

    def in_half(r, s):
        return pltpu.make_async_copy(xs_hbm.at[pl.ds(r, tm)], xin.at[s, pl.ds(0, tm)], isem.at[s])

    def out_full(r, s):
        return pltpu.make_async_copy(yout.at[s], ys_hbm.at[pl.ds(r, MOE_CH)], osem.at[s])

    def out_half(r, s):
        return pltpu.make_async_copy(yout.at[s, pl.ds(0, tm)], ys_hbm.at[pl.ds(r, tm)], osem.at[s])

    def start_in(j, s):
        for_chunk(j, lambda r: in_full(r, s).start(), lambda r: in_half(r, s).start())

    def wait_in(j, s):
        for_chunk(j, lambda r: in_full(r, s).wait(), lambda r: in_half(r, s).wait())

    def start_out(j, s):
        for_chunk(j, lambda r: out_full(r, s).start(), lambda r: out_half(r, s).start())

    def wait_out(j, s):
        for_chunk(j, lambda r: out_full(r, s).wait(), lambda r: out_half(r, s).wait())

    @pl.when(i == 0)
    def _():
        start_in(0, 0)

    @pl.when(i + 1 < n)
    def _():
        start_in(i + 1, 1 - slot)

    @pl.when(i >= 2)
    def _():
        wait_out(i - 2, slot)

    new_expert = jnp.logical_or(i == 0, exp_ref[i] != exp_ref[jnp.maximum(i - 1, 0)])

    @pl.when(jnp.logical_and(new_expert, size_ref[i] > 0))
    def _():
        chunk = 128
        def cast_body(c, carry):
            k0 = pl.multiple_of(c * chunk, chunk)
            w1b[pl.ds(k0, chunk), :] = w1_ref[pl.ds(k0, chunk), :].astype(BF16)
            w2b[pl.ds(k0, chunk), :] = w2_ref[pl.ds(k0, chunk), :].astype(BF16)
            return carry
        lax.fori_loop(0, D_MODEL // chunk, cast_body, 0)

    wait_in(i, slot)

    @pl.when(size_ref[i] == 2)
    def _():
        yout[slot] = _expert_mlp(xin[slot], w1b, b1_ref, w2b, b2_ref, act)

    @pl.when(size_ref[i] == 1)
    def _():
        yout[slot, 0:tm] = _expert_mlp(xin[slot, 0:tm], w1b, b1_ref, w2b, b2_ref, act.at[pl.ds(0, tm)])

    start_out(i, slot)

    @pl.when(i == n - 1)
    def _():
        @pl.when(i >= 1)
        def _():
            wait_out(i - 1, 1 - slot)
        wait_out(i, slot)
        yout[1, 0:tm] = jnp.zeros((tm, D_MODEL), F32)

        def zero_block(j):
            return pltpu.make_async_copy(yout.at[1, pl.ds(0, tm)], ys_hbm.at[pl.ds(pl.multiple_of(j * tm, tm), tm)],
                                         osem.at[1])

        def start_body(j, carry):
            zero_block(j).start()
            return carry

        def wait_body(j, carry):
            zero_block(j).wait()
            return carry
        lax.fori_loop(meta_ref[2 * N_EXPERTS], n_blocks, start_body, 0)
        lax.fori_loop(meta_ref[2 * N_EXPERTS], n_blocks, wait_body, 0)


def _moe(chunk_row, chunk_size, chunk_exp, meta, xs, w1, b1, w2, b2):
    by_expert = lambda i, row, size, exp, m: (exp[i], 0, 0)
    grid_spec = pltpu.PrefetchScalarGridSpec(
        num_scalar_prefetch=4,
        grid=(chunk_row.shape[0],),
        in_specs=[
            pl.BlockSpec(memory_space=pl.ANY),
            pl.BlockSpec((None, D_MODEL, 2 * D_FF), by_expert),
            pl.BlockSpec((None, 1, 2 * D_FF), by_expert),
            pl.BlockSpec((None, D_FF, D_MODEL), by_expert),
            pl.BlockSpec((None, 1, D_MODEL), by_expert),
        ],
        out_specs=pl.BlockSpec(memory_space=pl.ANY),
        scratch_shapes=[pltpu.VMEM((2, MOE_CH, D_MODEL), F32), pltpu.VMEM((2, MOE_CH, D_MODEL), F32),
                        pltpu.VMEM((D_MODEL, 2 * D_FF), BF16), pltpu.VMEM((D_FF, D_MODEL), BF16),
                        pltpu.VMEM((MOE_CH, D_FF), BF16),
                        pltpu.SemaphoreType.DMA((2,)), pltpu.SemaphoreType.DMA((2,))],
    )
    return pl.pallas_call(
        _moe_kernel,
        grid_spec=grid_spec,
        out_shape=jax.ShapeDtypeStruct(xs.shape, F32),
        compiler_params=pltpu.CompilerParams(dimension_semantics=("arbitrary",), vmem_limit_bytes=VMEM_LIMIT),
        name="moe",
    )(chunk_row, chunk_size, chunk_exp, meta, xs, w1, b1, w2, b2)


def _combine_kernel(n_prompt_tiles, slotc_ref, slotn_ref, ys_hbm, route_ref, x1_ref, modp_ref, g2s_ref, fg_ref, op_ref,
                    os_ref, buf, sem):
    i = pl.program_id(0)
    nb = pl.num_programs(0)
    slot = i % 2
    tt = COMB_TT

    def issue(slot_ref, s):
        def body(t, carry):
            for kk in range(TOP_K):
                src = slot_ref[0, 0, t * TOP_K + kk]
                pltpu.make_async_copy(ys_hbm.at[pl.ds(src, 1)], buf.at[s, pl.ds(kk * tt + t, 1)],
                                      sem.at[s]).start(priority=kk % 2)
            return carry
        lax.fori_loop(0, tt, body, 0, unroll=4)

    @pl.when(i == 0)
    def _():
        issue(slotc_ref, 0)

    @pl.when(i + 1 < nb)
    def _():
        issue(slotn_ref, 1 - slot)

    pltpu.make_async_copy(ys_hbm.at[pl.ds(0, TOP_K * tt)], buf.at[slot], sem.at[slot]).wait()

    route = route_ref[...]
    ff = route[:, ROUTE_G:ROUTE_G + 1] * buf[slot, 0:tt, :]
    for kk in range(1, TOP_K):
        ff = ff + route[:, ROUTE_G + kk:ROUTE_G + kk + 1] * buf[slot, kk * tt:(kk + 1) * tt, :]
    g2 = jnp.where(i < n_prompt_tiles, modp_ref[5:6, :], g2s_ref[...])
    x = x1_ref[...] + g2 * ff
    y = _rms(x, fg_ref[...])

    @pl.when(i < n_prompt_tiles)
    def _():
        op_ref[...] = y

    @pl.when(i >= n_prompt_tiles)
    def _():
        os_ref[...] = y


def _combine(slot_blocks, ys, route, x1_all, mod_p, g2_rows, final_g, n_prompt_tiles, tiles_per_batch):
    nt = slot_blocks.shape[0]
    tt = COMB_TT
    npt = n_prompt_tiles
    return pl.pallas_call(
        functools.partial(_combine_kernel, npt),
        grid=(nt,),
        in_specs=[
            pl.BlockSpec((1, 1, TOP_K * tt), lambda i: (i, 0, 0), memory_space=pltpu.SMEM),
            pl.BlockSpec((1, 1, TOP_K * tt), lambda i: (jnp.minimum(i + 1, nt - 1), 0, 0), memory_space=pltpu.SMEM),
            pl.BlockSpec(memory_space=pl.ANY),
            pl.BlockSpec((tt, LANES), lambda i: (i, 0)),
            pl.BlockSpec((tt, D_MODEL), lambda i: (i, 0)),
            pl.BlockSpec((None, 6, D_MODEL), lambda i: (jnp.minimum(i, npt - 1) // tiles_per_batch, 0, 0)),
            pl.BlockSpec((tt, D_MODEL), lambda i: (jnp.maximum(i - npt, 0), 0)),
            pl.BlockSpec((1, D_MODEL), lambda i: (0, 0)),
        ],
        out_specs=(pl.BlockSpec((tt, D_MODEL), lambda i: (jnp.minimum(i, npt - 1), 0)),
                   pl.BlockSpec((tt, D_MODEL), lambda i: (jnp.maximum(i - npt, 0), 0))),
        out_shape=(jax.ShapeDtypeStruct((npt * tt, D_MODEL), F32),
                   jax.ShapeDtypeStruct(((nt - npt) * tt, D_MODEL), F32)),
        scratch_shapes=[pltpu.VMEM((2, TOP_K * tt, D_MODEL), F32), pltpu.SemaphoreType.DMA((2,))],
        compiler_params=pltpu.CompilerParams(dimension_semantics=("arbitrary",), vmem_limit_bytes=VMEM_LIMIT),
        name="combine",
    )(slot_blocks, slot_blocks, ys, route, x1_all, mod_p, g2_rows, final_g)


def _block_diag_halves(wg):
    halves = []
    for hh in range(2):
        rows = []
        for bi in range(4):
            row = [wg[hh * 4 + bi] if bj == bi else jnp.zeros((LRU_BLOCK_W, LRU_BLOCK_W), wg.dtype) for bj in range(4)]
            rows.append(jnp.concatenate(row, axis=1))
        halves.append(jnp.concatenate(rows, axis=0))
    return jnp.stack(halves).astype(BF16)


def _rope_tables(pos):
    half = HEAD_DIM // 2
    inv = ROPE_THETA ** (-jnp.arange(half, dtype=F32) / half)
    ang = pos.astype(F32)[:, None] * inv[None, :]
    cos = jnp.cos(ang)
    sin = jnp.sin(ang)
    cos128 = jnp.concatenate([cos, cos, cos, cos], axis=1)
    sin128 = jnp.concatenate([-sin, sin, -sin, sin], axis=1)
    return cos128, sin128


def _full_spec(arr, grid_rank):
    zeros = (0,) * arr.ndim
    if grid_rank == 1:
        return pl.BlockSpec(arr.shape, lambda i: zeros)
    return pl.BlockSpec(arr.shape, lambda b, j: zeros)


def kernel(x_prompt, x_sample, state_lru_h, state_conv, cache_win_k, cache_win_v, c_prompt, c_sample, w_ada, b_ada, norm1_g, w_in, b_in, conv_w, conv_b, lru_wa, lru_ba, lru_wx, lru_bx, lru_lambda, attn_sinks, w_out, b_out, norm2_g, w_router, b_router, w1, b1, w2, b2, final_g):
    bp, seq, _ = x_prompt.shape
    bd, tdec, _ = x_sample.shape
    assert tdec == SUBLANES and seq % SEQ_TILE == 0 and bd % SAMPLE_BT == 0
    n_prompt = bp * seq
    n_sample = bd * tdec
    n_tok = n_prompt + n_sample
    l = 0

    head_perm = jnp.array([h for c in range(4) for h in (c, GROUP + c)], dtype=jnp.int32)
    qcol_perm = (head_perm[:, None] * HEAD_DIM + jnp.arange(HEAD_DIM, dtype=jnp.int32)[None, :]).reshape(-1)
    o2 = 2 * LRU_WIDTH
    in_perm = jnp.concatenate([jnp.arange(o2, dtype=jnp.int32), o2 + qcol_perm,
                               jnp.arange(o2 + Q_WIDTH, IN_WIDTH, dtype=jnp.int32)])
    out_perm = jnp.concatenate([jnp.arange(LRU_WIDTH, dtype=jnp.int32), LRU_WIDTH + qcol_perm])
    sinks_perm = attn_sinks[l]

    wr = jnp.tile(w_router[l], (1, TOP_K))
    wr_hi = wr.astype(BF16)
    weights = dict(
        norm1_g=norm1_g[l][None, :], w_in=w_in[l][:, in_perm].astype(BF16), b_in=b_in[l][in_perm][None, :],
        conv_w=conv_w[l], conv_b=conv_b[l][None, :],
        ga=_block_diag_halves(lru_wa[l]), gx=_block_diag_halves(lru_wx[l]),
        lru_ba=lru_ba[l][None, :], lru_bx=lru_bx[l][None, :], lam=lru_lambda[l][None, :],
        w_out=w_out[l][out_perm, :].astype(BF16), b_out=b_out[l][None, :], norm2_g=norm2_g[l][None, :],
        wr_hi=wr_hi, wr_lo=(wr - wr_hi.astype(F32)).astype(BF16),
        b_router=jnp.tile(b_router[l], TOP_K)[None, :],
    )
    wlist = [weights[n] for n in WEIGHT_NAMES]

    mod_all = _ada(jnp.concatenate([c_prompt, c_sample], axis=0), w_ada[l], b_ada[l][None, :])
    mod_p = mod_all[:bp].reshape(bp, 6, D_MODEL)
    mod_s = mod_all[bp:].reshape(bd, 6, D_MODEL).transpose(1, 0, 2)

    cos_p, sin_p = _rope_tables(jnp.arange(seq, dtype=jnp.int32))
    cos_s, sin_s = _rope_tables(PAST_LEN + jnp.arange(tdec, dtype=jnp.int32))
    cos_s = jnp.tile(cos_s, (SAMPLE_BT, 1))
    sin_s = jnp.tile(sin_s, (SAMPLE_BT, 1))
    h0_rows = jnp.pad(state_lru_h[l][:, None, :], ((0, 0), (0, tdec - 1), (0, 0))).reshape(n_sample, LRU_WIDTH)
    cprev_rows = jnp.pad(state_conv[l], ((0, 0), (0, tdec - (CONV_W - 1)), (0, 0))).reshape(n_sample, LRU_WIDTH)
    ck = cache_win_k[l].reshape(bd, WINDOW, KV_WIDTH)
    cv = cache_win_v[l].reshape(bd, WINDOW, KV_WIDTH)
    nj = seq // SEQ_TILE
    npt = n_prompt // SEQ_TILE
    nst = n_sample // SEQ_TILE
    p_tile = lambda i: jnp.minimum(i, npt - 1)
    s_tile = lambda i: jnp.maximum(i - npt, 0)
    all_rows = lambda width: pl.BlockSpec((SEQ_TILE, width), lambda i: (i, 0))
    p_rows = lambda width: pl.BlockSpec((SEQ_TILE, width), lambda i: (p_tile(i), 0))
    s_rows = lambda width: pl.BlockSpec((SEQ_TILE, width), lambda i: (s_tile(i), 0))
    p_tail = lambda rows, width: pl.BlockSpec((None, rows, width), lambda i: (p_tile(i) // nj, 0, 0))
    cache_spec = pl.BlockSpec((SAMPLE_BT, WINDOW, KV_WIDTH), lambda i: (s_tile(i), 0, 0))
    mixer_out_shapes = (
        jax.ShapeDtypeStruct((n_tok, D_MODEL), F32),
        jax.ShapeDtypeStruct((n_tok, D_MODEL), F32),
        jax.ShapeDtypeStruct((n_tok, LANES), F32),
        jax.ShapeDtypeStruct((npt + nst, SUBLANES, LANES), F32),
        jax.ShapeDtypeStruct((bp, SUBLANES, LRU_WIDTH), F32),
        jax.ShapeDtypeStruct((bp, SUBLANES, LRU_WIDTH), F32),
        jax.ShapeDtypeStruct((bp, WINDOW, KV_WIDTH), F32),
        jax.ShapeDtypeStruct((bp, WINDOW, KV_WIDTH), F32),
        jax.ShapeDtypeStruct((n_sample, D_MODEL), F32),
        jax.ShapeDtypeStruct((n_sample, LRU_WIDTH), F32),
        jax.ShapeDtypeStruct((n_sample, LRU_WIDTH), F32),
        jax.ShapeDtypeStruct((bd, WINDOW, KV_WIDTH), F32),
        jax.ShapeDtypeStruct((bd, WINDOW, KV_WIDTH), F32),
    )
    (x1_all, h2_all, route, tile_cnt, hlast_p, ulast_p, klast_p, vlast_p, g2_rows, hs_s, u_s, s_k, s_v) = pl.pallas_call(
        functools.partial(_mixer_kernel, npt, nj),
        grid=(npt + nst,),
        in_specs=[p_rows(D_MODEL),
                  pl.BlockSpec((None, 6, D_MODEL), lambda i: (p_tile(i) // nj, 0, 0)),
                  pl.BlockSpec((SEQ_TILE, LANES), lambda i: (p_tile(i) % nj, 0)),
                  pl.BlockSpec((SEQ_TILE, LANES), lambda i: (p_tile(i) % nj, 0)),
                  s_rows(D_MODEL),
                  pl.BlockSpec((6, SAMPLE_BT, D_MODEL), lambda i: (0, s_tile(i), 0)),
                  pl.BlockSpec((SEQ_TILE, LANES), lambda i: (0, 0)),
                  pl.BlockSpec((SEQ_TILE, LANES), lambda i: (0, 0)),
                  pl.BlockSpec(memory_space=pltpu.SMEM),
                  s_rows(LRU_WIDTH), s_rows(LRU_WIDTH), cache_spec, cache_spec]
                 + [_full_spec(a, 1) for a in wlist],
        out_specs=(all_rows(D_MODEL), all_rows(D_MODEL), all_rows(LANES),
                   pl.BlockSpec((None, SUBLANES, LANES), lambda i: (i, 0, 0)),
                   p_tail(SUBLANES, LRU_WIDTH), p_tail(SUBLANES, LRU_WIDTH),
                   p_tail(WINDOW, KV_WIDTH), p_tail(WINDOW, KV_WIDTH),
                   s_rows(D_MODEL), s_rows(LRU_WIDTH), s_rows(LRU_WIDTH), cache_spec, cache_spec),
        out_shape=mixer_out_shapes,
        scratch_shapes=[pltpu.VMEM((SUBLANES, LRU_WIDTH), F32), pltpu.VMEM((SUBLANES, LRU_WIDTH), F32),
                        pltpu.VMEM((WINDOW, KV_WIDTH), F32), pltpu.VMEM((WINDOW, KV_WIDTH), F32)],
        compiler_params=pltpu.CompilerParams(dimension_semantics=("arbitrary",), vmem_limit_bytes=VMEM_LIMIT),
        name="mixer",
    )(x_prompt.reshape(n_prompt, D_MODEL), mod_p, cos_p, sin_p,
      x_sample.reshape(n_sample, D_MODEL), mod_s, cos_s, sin_s, sinks_perm, h0_rows, cprev_rows, ck, cv, *wlist)

    n_tiles = npt + nst
    n_assign = n_tok * TOP_K
    n_blocks = -(-(n_assign + N_EXPERTS * (MOE_TM - 1)) // MOE_TM)
    cnt = tile_cnt[:, 0, :N_EXPERTS].astype(jnp.int32)
    counts = jnp.sum(cnt, axis=0)
    pcounts = (counts + MOE_TM - 1) // MOE_TM * MOE_TM
    pend = jnp.cumsum(pcounts)
    pstart = pend - pcounts
    tile_off = pstart[None, :] + jnp.cumsum(cnt, axis=0) - cnt
    e_ids = route[:, ROUTE_E:ROUTE_E + TOP_K].astype(jnp.int32).reshape(n_tiles, SEQ_TILE, TOP_K)
    ranks = route[:, ROUTE_R:ROUTE_R + TOP_K].astype(jnp.int32).reshape(n_tiles, SEQ_TILE, TOP_K)
    hit = e_ids[..., None] == jnp.arange(N_EXPERTS, dtype=jnp.int32)
    dest = jnp.sum(jnp.where(hit, tile_off[:, None, None, :], 0), axis=-1) + ranks
    dest_blocks = dest.reshape(n_tiles, 1, SEQ_TILE * TOP_K)
    meta =jnp.concatenate([pstart + counts, pend, pend[-1:] // MOE_TM]).astype(jnp.int32)

    xs = _dispatch(meta, dest_blocks, h2_all, n_blocks * MOE_TM)
    n_tm = pcounts // MOE_TM
    n_ch = (n_tm + 1) // 2
    ch_end = jnp.cumsum(n_ch)
    n_chunks = (n_blocks + N_EXPERTS + 1) // 2
    ci = jnp.arange(n_chunks, dtype=jnp.int32)
    chunk_exp = jnp.minimum(jnp.sum((ch_end[None, :] <= ci[:, None]).astype(jnp.int32), axis=1), N_EXPERTS - 1)
    local = ci - (ch_end - n_ch)[chunk_exp]
    live = ci < ch_end[-1]
    chunk_row = jnp.where(live, pstart[chunk_exp] + local * MOE_CH, 0).astype(jnp.int32)
    chunk_size = jnp.where(live, jnp.where(2 * local + 2 <= n_tm[chunk_exp], 2, 1), 0).astype(jnp.int32)
    ys = _moe(chunk_row, chunk_size, chunk_exp.astype(jnp.int32), meta, xs, w1[l], b1[l][:, None, :], w2[l], b2[l][:, None, :])
    y_p, y_s = _combine(dest_blocks, ys, route, x1_all, mod_p, g2_rows,
                        final_g[None, :], n_prompt // COMB_TT, seq // COMB_TT)

    y_prompt = y_p.reshape(bp, seq, D_MODEL)
    y_sample = y_s.reshape(bd, tdec, D_MODEL)
    p_h = hlast_p[:, SUBLANES - 1, :][None]
    p_c = ulast_p[:, SUBLANES - (CONV_W - 1):, :][None]
    p_k = klast_p.reshape(1, bp, WINDOW, N_KV_HEADS, HEAD_DIM)
    p_v = vlast_p.reshape(1, bp, WINDOW, N_KV_HEADS, HEAD_DIM)
    s_h = hs_s.reshape(bd, tdec, LRU_WIDTH)[:, tdec - 1, :][None]
    s_c = u_s.reshape(bd, tdec, LRU_WIDTH)[:, tdec - (CONV_W - 1):, :][None]
    s_kk = s_k.reshape(1, bd, WINDOW, N_KV_HEADS, HEAD_DIM)
    s_vv = s_v.reshape(1, bd, WINDOW, N_KV_HEADS, HEAD_DIM)
    return (y_prompt, y_sample, p_h, p_c, p_k, p_v, s_h, s_c, s_kk, s_vv)
```

```python
import functools

import jax
import jax.numpy as jnp
from jax import lax
from jax.experimental import pallas as pl
from jax.experimental.pallas import tpu as pltpu

F32 = jnp.float32
BF16 = jnp.bfloat16

D_MODEL = 1024
LRU_WIDTH = 512
LRU_BLOCKS = 8
LRU_BLOCK_W = LRU_WIDTH // LRU_BLOCKS
CONV_W = 4
LRU_C = 8.0
HEAD_DIM = 64
N_HEADS = 8
N_KV_HEADS = 2
GROUP = N_HEADS // N_KV_HEADS
WINDOW = 128
ROPE_THETA = 10000.0
N_EXPERTS = 32
TOP_K = 4
D_FF = D_MODEL
SWIGLU_LIMIT = 7.0
SWIGLU_ALPHA = 1.702
NORM_EPS = 1e-5
PAST_LEN = 8192
Q_WIDTH = N_HEADS * HEAD_DIM
KV_WIDTH = N_KV_HEADS * HEAD_DIM
IN_WIDTH = 2 * LRU_WIDTH + Q_WIDTH + 2 * KV_WIDTH

LANES = 128
SUBLANES = 8
SEQ_TILE = 256
SAMPLE_BT = 32
MOE_TM = 256
MOE_CH = 2 * MOE_TM
COMB_TT = 256
NEG_BIG = -1e30
VMEM_LIMIT = 56 * 1024 * 1024


def _rms(x, g):
    return x * lax.rsqrt(jnp.mean(x * x, axis=-1, keepdims=True) + NORM_EPS) * g


def _shift_rows(x, d, fill, rowid):
    return jnp.where(rowid >= d, pltpu.roll(x, d, axis=0), fill)


def _lin_scan(a, b, seg, rowid):
    d = 1
    while d < seg:
        a_s = _shift_rows(a, d, 1.0, rowid)
        b_s = _shift_rows(b, d, 0.0, rowid)
        b = a * b_s + b
        a = a * a_s
        d *= 2
    return a, b


def _rope128(x, cos, sin_signed, first_half):
    sw = jnp.where(first_half, pltpu.roll(x, LANES - HEAD_DIM // 2, axis=1), pltpu.roll(x, HEAD_DIM // 2, axis=1))
    return x * cos + sw * sin_signed


def _neg_expm1(y):
    return -(jnp.tanh(0.5 * y) * (jnp.exp(y) + 1.0))


def _softplus(x):
    return jnp.maximum(x, 0.0) + jnp.log1p(jnp.exp(-jnp.abs(x)))


def _lru_coeffs(uc, w, first_pos_mask):
    ub = uc.astype(BF16)
    half = LRU_WIDTH // 2
    ra = jnp.concatenate([jnp.dot(ub[:, :half], w['ga'][0], preferred_element_type=F32),
                          jnp.dot(ub[:, half:], w['ga'][1], preferred_element_type=F32)], axis=1)
    rx = jnp.concatenate([jnp.dot(ub[:, :half], w['gx'][0], preferred_element_type=F32),
                          jnp.dot(ub[:, half:], w['gx'][1], preferred_element_type=F32)], axis=1)
    r = jax.nn.sigmoid(ra + w['lru_ba'][...])
    i = jax.nn.sigmoid(rx + w['lru_bx'][...])
    log_a = -LRU_C * r * _softplus(-w['lam'][...])
    a = jnp.exp(log_a)
    mult = jnp.sqrt(_neg_expm1(2.0 * log_a))
    if first_pos_mask is not None:
        mult = jnp.where(first_pos_mask, 1.0, mult)
    return a, mult * i * uc


def _conv_taps(u, s1, s2, s3, w):
    cw = w['conv_w']
    return w['conv_b'][...] + s3 * cw[0:1, :] + s2 * cw[1:2, :] + s1 * cw[2:3, :] + u * cw[3:4, :]


def _in_proj(x, mod, w):
    sh1, sc1 = mod
    h = _rms(x, w['norm1_g'][...]) * (1.0 + sc1) + sh1
    return jnp.dot(h.astype(BF16), w['w_in'][...], preferred_element_type=F32) + w['b_in'][...]


def _post_mix(x, mix, mod, w):
    g1, sh2, sc2 = mod
    x1 = x + g1 * (jnp.dot(mix.astype(BF16), w['w_out'][...], preferred_element_type=F32) + w['b_out'][...])
    h2 = _rms(x1, w['norm2_g'][...]) * (1.0 + sc2) + sh2
    h2_hi = h2.astype(BF16)
    h2_lo = (h2 - h2_hi.astype(F32)).astype(BF16)
    logits = (jnp.dot(h2_hi, w['wr_hi'][...], preferred_element_type=F32)
              + jnp.dot(h2_lo, w['wr_hi'][...], preferred_element_type=F32)
              + jnp.dot(h2_hi, w['wr_lo'][...], preferred_element_type=F32)) + w['b_router'][...]
    return x1, h2, logits


ROUTE_E, ROUTE_G, ROUTE_R = 0, TOP_K, 2 * TOP_K


def _lane_roll1(v, shift):
    return pltpu.roll(jnp.broadcast_to(v, (SUBLANES, LANES)), shift, axis=1)[0:1]


def _route_tile(lg):
    rows = lg.shape[0]
    lane = lax.broadcasted_iota(jnp.int32, (1, LANES), 1)
    e_of = lane % N_EXPERTS
    grp = lane // N_EXPERTS
    onehot = jnp.zeros((rows, LANES), F32)
    vals, ids = [], []
    for k in range(TOP_K):
        m = jnp.max(lg, axis=1, keepdims=True)
        idx = jnp.min(jnp.where(lg == m, e_of, N_EXPERTS), axis=1, keepdims=True)
        sel = e_of == idx
        lg = jnp.where(sel, -jnp.inf, lg)
        onehot = jnp.where(jnp.logical_and(sel, grp == k), 1.0, onehot)
        vals.append(m)
        ids.append(idx)
    ex = [jnp.exp(v - vals[0]) for v in vals]
    denom = ex[0] + ex[1] + ex[2] + ex[3]

    r_i = lax.broadcasted_iota(jnp.int32, (rows, rows), 0)
    c_i = lax.broadcasted_iota(jnp.int32, (rows, rows), 1)
    strict_lower = jnp.where(r_i > c_i, 1.0, 0.0).astype(BF16)
    prefix = jnp.dot(strict_lower, onehot.astype(BF16), preferred_element_type=F32)
    cnt = jnp.sum(onehot, axis=0, keepdims=True)
    base = jnp.zeros((1, LANES), F32)
    tot = cnt
    for s in range(1, TOP_K):
        rolled = _lane_roll1(cnt, s * N_EXPERTS)
        base = base + jnp.where(lane >= s * N_EXPERTS, rolled, 0.0)
        tot = tot + rolled
    pad_cnt = jnp.floor((tot + (SUBLANES - 1.0)) * (1.0 / SUBLANES)) * SUBLANES
    inc = pad_cnt
    d = 1
    while d < N_EXPERTS:
        inc = inc + jnp.where(e_of >= d, _lane_roll1(inc, d), 0.0)
        d *= 2
    strip_start = inc - pad_cnt
    ranked = onehot * (prefix + base + strip_start)

    route = jnp.zeros((rows, LANES), F32)
    for k in range(TOP_K):
        rank_k = jnp.sum(jnp.where(grp == k, ranked, 0.0), axis=1, keepdims=True)
        route = jnp.where(lane == ROUTE_E + k, ids[k].astype(F32), route)
        route = jnp.where(lane == ROUTE_G + k, ex[k] / denom, route)
        route = jnp.where(lane == ROUTE_R + k, rank_k, route)
    return route, tot


def _softmax_sink_pv(s, sink_col, v_bf16):
    m = jnp.maximum(jnp.max(s, axis=-1, keepdims=True), sink_col)
    p = jnp.exp(s - m)
    denom = jnp.sum(p, axis=-1, keepdims=True) + jnp.exp(sink_col - m)
    return p, denom


WEIGHT_NAMES = ('norm1_g', 'w_in', 'b_in', 'conv_w', 'conv_b', 'ga', 'gx', 'lru_ba', 'lru_bx', 'lam',
                'w_out', 'b_out', 'norm2_g', 'wr_hi', 'wr_lo', 'b_router')


def _ada_kernel(c_ref, w_ref, b_ref, o_ref):
    c = c_ref[...]
    s = c * jax.nn.sigmoid(c)
    o_ref[...] = jnp.dot(s, w_ref[...], preferred_element_type=F32, precision=lax.Precision.HIGHEST) + b_ref[...]


def _ada(c_all, w_ada, b_ada):
    rows = c_all.shape[0]
    return pl.pallas_call(
        _ada_kernel,
        grid=(6,),
        in_specs=[pl.BlockSpec((rows, D_MODEL), lambda i: (0, 0)),
                  pl.BlockSpec((D_MODEL, D_MODEL), lambda i: (0, i)),
                  pl.BlockSpec((1, D_MODEL), lambda i: (0, i))],
        out_specs=pl.BlockSpec((rows, D_MODEL), lambda i: (0, i)),
        out_shape=jax.ShapeDtypeStruct((rows, 6 * D_MODEL), F32),
        compiler_params=pltpu.CompilerParams(dimension_semantics=("arbitrary",), vmem_limit_bytes=VMEM_LIMIT),
        name="ada",
    )(c_all, w_ada, b_ada)


def _prompt_body(j, x_ref, mod_ref, cos_ref, sin_ref, sinks_ref, w, x1_ref, h2_ref, lg_ref,
                 hlast_ref, ulast_ref, klast_ref, vlast_ref, conv_c, h_c, k_c, v_c):
    ts = SEQ_TILE

    @pl.when(j == 0)
    def _():
        conv_c[...] = jnp.zeros_like(conv_c)
        h_c[...] = jnp.zeros_like(h_c)
        k_c[...] = jnp.zeros_like(k_c)
        v_c[...] = jnp.zeros_like(v_c)

    x = x_ref[...]
    mod = mod_ref[...]
    proj = _in_proj(x, (mod[0:1], mod[1:2]), w)
    u = proj[:, :LRU_WIDTH]
    gate = proj[:, LRU_WIDTH:2 * LRU_WIDTH]
    o2 = 2 * LRU_WIDTH

    rowid = lax.broadcasted_iota(jnp.int32, (ts, 1), 0)
    u_ext = jnp.concatenate([conv_c[...], u], axis=0)
    s1, s2, s3 = (pltpu.roll(u_ext, d, axis=0)[SUBLANES:] for d in (1, 2, 3))
    uc = _conv_taps(u, s1, s2, s3, w)
    conv_c[...] = u[ts - SUBLANES:]
    ulast_ref[...] = u[ts - SUBLANES:]

    first_pos = jnp.logical_and(rowid == 0, j == 0)
    a, bt = _lru_coeffs(uc, w, first_pos)
    a_cum, b_cum = _lin_scan(a, bt, ts, rowid)
    hs = b_cum + a_cum * h_c[0:1, :]
    h_tail = hs[ts - SUBLANES:]
    h_c[...] = jnp.broadcast_to(h_tail[SUBLANES - 1:SUBLANES, :], h_c.shape)
    hlast_ref[...] = h_tail
    lru_out = hs * jax.nn.gelu(gate)

    cos = cos_ref[...]
    sin = sin_ref[...]
    lane = lax.broadcasted_iota(jnp.int32, (1, LANES), 1)
    first_half = (lane % HEAD_DIM) < (HEAD_DIM // 2)
    qcols = [_rope128(proj[:, o2 + c * LANES:o2 + (c + 1) * LANES], cos, sin, first_half) for c in range(4)]
    k = _rope128(proj[:, o2 + Q_WIDTH:o2 + Q_WIDTH + KV_WIDTH], cos, sin, first_half)
    v = proj[:, o2 + Q_WIDTH + KV_WIDTH:]
    k_ext = jnp.concatenate([k_c[...], k], axis=0).astype(BF16)
    v_ext = jnp.concatenate([v_c[...], v], axis=0).astype(BF16)
    k_c[...] = k[ts - WINDOW:]
    v_c[...] = v[ts - WINDOW:]
    klast_ref[...] = k[ts - WINDOW:]
    vlast_ref[...] = v[ts - WINDOW:]

    qi = lax.broadcasted_iota(jnp.int32, (WINDOW, 2 * WINDOW), 0)
    kj = lax.broadcasted_iota(jnp.int32, (WINDOW, 2 * WINDOW), 1)
    band = jnp.logical_and(kj > qi, kj <= qi + WINDOW)
    lane_lo = lane < HEAD_DIM
    grow = lax.broadcasted_iota(jnp.int32, (GROUP * WINDOW, 1), 0) // WINDOW
    attn_cols = [[] for _ in range(4)]
    for blk in range(ts // WINDOW):
        prev_ok = jnp.logical_or(j > 0, blk > 0)
        mask = jnp.logical_and(band, jnp.logical_or(kj >= WINDOW, prev_ok))
        mask4 = jnp.concatenate([mask] * GROUP, axis=0)
        kb = k_ext[blk * WINDOW:(blk + 2) * WINDOW]
        vb = v_ext[blk * WINDOW:(blk + 2) * WINDOW]
        outs = []
        for kv in range(N_KV_HEADS):
            sel = lane_lo if kv == 0 else jnp.logical_not(lane_lo)
            qs = jnp.concatenate(
                [jnp.where(sel, qc[blk * WINDOW:(blk + 1) * WINDOW], 0.0) for qc in qcols], axis=0).astype(BF16)
            s = lax.dot_general(qs, kb, (((1,), (1,)), ((), ())), preferred_element_type=F32) * (HEAD_DIM ** -0.5)
            s = jnp.where(mask4, s, NEG_BIG)
            sink_col = jnp.zeros((GROUP * WINDOW, 1), F32)
            for g in range(GROUP):
                sink_col = jnp.where(grow == g, sinks_ref[kv * GROUP + g], sink_col)
            p, denom = _softmax_sink_pv(s, sink_col, vb)
            outs.append(jnp.dot(p.astype(BF16), vb, preferred_element_type=F32) / denom)
        for c in range(4):
            attn_cols[c].append(jnp.where(lane_lo, outs[0][c * WINDOW:(c + 1) * WINDOW],
                                          outs[1][c * WINDOW:(c + 1) * WINDOW]))
    attn = jnp.concatenate([jnp.concatenate(cols, axis=0) for cols in attn_cols], axis=1)

    mix = jnp.concatenate([lru_out, attn], axis=1)
    x1, h2, logits = _post_mix(x, mix, (mod[2:3], mod[3:4], mod[4:5]), w)
    x1_ref[...] = x1
    h2_ref[...] = h2
    route, tot = _route_tile(logits)
    lg_ref[0][...] = route
    lg_ref[1][...] = jnp.broadcast_to(tot, lg_ref[1].shape)


def _expand_rows(m, t):
    b, wd = m.shape
    return jnp.broadcast_to(m[:, None, :], (b, t, wd)).reshape(b * t, wd)


def _sample_body(x_ref, mod_ref, cos_ref, sin_ref, sinks_ref, h0_ref, cprev_ref, ck_ref, cv_ref, w,
                 x1_ref, h2_ref, lg_ref, g2_ref, hs_ref, u_ref, ko_ref, vo_ref):
    bt_, t = SAMPLE_BT, SUBLANES
    rows = bt_ * t

    x = x_ref[...]
    mods = [_expand_rows(mod_ref[i], t) for i in range(6)]
    proj = _in_proj(x, (mods[0], mods[1]), w)
    u = proj[:, :LRU_WIDTH]
    gate = proj[:, LRU_WIDTH:2 * LRU_WIDTH]
    o2 = 2 * LRU_WIDTH
    u_ref[...] = u

    rowid = lax.broadcasted_iota(jnp.int32, (rows, 1), 0) % t
    cprev = cprev_ref[...]
    taps = []
    for d in (1, 2, 3):
        taps.append(jnp.where(rowid >= d, pltpu.roll(u, d, axis=0),
                              pltpu.roll(cprev, (d - (CONV_W - 1)) % rows, axis=0)))
    uc = _conv_taps(u, taps[0], taps[1], taps[2], w)

    a, bt = _lru_coeffs(uc, w, None)
    bt = bt + a * h0_ref[...]
    _, hs = _lin_scan(a, bt, t, rowid)
    hs_ref[...] = hs
    lru_out = hs * jax.nn.gelu(gate)

    cos = cos_ref[...]
    sin = sin_ref[...]
    lane = lax.broadcasted_iota(jnp.int32, (1, LANES), 1)
    first_half = (lane % HEAD_DIM) < (HEAD_DIM // 2)
    qcols = [_rope128(proj[:, o2 + c * LANES:o2 + (c + 1) * LANES], cos, sin, first_half) for c in range(4)]
    k = _rope128(proj[:, o2 + Q_WIDTH:o2 + Q_WIDTH + KV_WIDTH], cos, sin, first_half)
    v = proj[:, o2 + Q_WIDTH + KV_WIDTH:]
    k3 = k.reshape(bt_, t, KV_WIDTH)
    v3 = v.reshape(bt_, t, KV_WIDTH)
    ck = ck_ref[...]
    cv = cv_ref[...]
    ko_ref[:, :WINDOW - t, :] = ck[:, t:, :]
    ko_ref[:, WINDOW - t:, :] = k3
    vo_ref[:, :WINDOW - t, :] = cv[:, t:, :]
    vo_ref[:, WINDOW - t:, :] = v3

    ckb, cvb, k3b, v3b = ck.astype(BF16), cv.astype(BF16), k3.astype(BF16), v3.astype(BF16)
    lane_lo = lane < HEAD_DIM
    gq = GROUP * t
    tq = lax.broadcasted_iota(jnp.int32, (1, gq, 1), 1) % t
    mask_c = lax.broadcasted_iota(jnp.int32, (1, gq, WINDOW), 2) > tq
    mask_n = lax.broadcasted_iota(jnp.int32, (1, gq, t), 2) <= tq
    grow = lax.broadcasted_iota(jnp.int32, (1, gq, 1), 1) // t
    bdims = (((2,), (2,)), ((0,), (0,)))
    pdims = (((2,), (1,)), ((0,), (0,)))
    outs = []
    for kv in range(N_KV_HEADS):
        sel = lane_lo if kv == 0 else jnp.logical_not(lane_lo)
        q3 = jnp.concatenate([jnp.where(sel, qc, 0.0).reshape(bt_, t, LANES) for qc in qcols], axis=1).astype(BF16)
        sc = lax.dot_general(q3, ckb, bdims, preferred_element_type=F32) * (HEAD_DIM ** -0.5)
        sn = lax.dot_general(q3, k3b, bdims, preferred_element_type=F32) * (HEAD_DIM ** -0.5)
        sc = jnp.where(mask_c, sc, NEG_BIG)
        sn = jnp.where(mask_n, sn, NEG_BIG)
        sink_col = jnp.zeros((1, gq, 1), F32)
        for g in range(GROUP):
            sink_col = jnp.where(grow == g, sinks_ref[kv * GROUP + g], sink_col)
        m = jnp.maximum(jnp.maximum(jnp.max(sc, axis=-1, keepdims=True), jnp.max(sn, axis=-1, keepdims=True)),
                        sink_col)
        pc = jnp.exp(sc - m)
        pn = jnp.exp(sn - m)
        denom = jnp.sum(pc, axis=-1, keepdims=True) + jnp.sum(pn, axis=-1, keepdims=True) + jnp.exp(sink_col - m)
        o = (lax.dot_general(pc.astype(BF16), cvb, pdims, preferred_element_type=F32)
             + lax.dot_general(pn.astype(BF16), v3b, pdims, preferred_element_type=F32)) / denom
        outs.append(o)
    attn = jnp.concatenate(
        [jnp.where(lane_lo, outs[0][:, c * t:(c + 1) * t, :], outs[1][:, c * t:(c + 1) * t, :]).reshape(rows, LANES)
         for c in range(4)], axis=1)

    mix = jnp.concatenate([lru_out, attn], axis=1)
    x1, h2, logits = _post_mix(x, mix, (mods[2], mods[3], mods[4]), w)
    x1_ref[...] = x1
    h2_ref[...] = h2
    route, tot = _route_tile(logits)
    lg_ref[0][...] = route
    lg_ref[1][...] = jnp.broadcast_to(tot, lg_ref[1].shape)
    g2_ref[...] = mods[5]


def _mixer_kernel(n_prompt_tiles, tiles_per_seq,
                  xp_ref, modp_ref, cosp_ref, sinp_ref, xs_ref, mods_ref, coss_ref, sins_ref, sinks_ref,
                  h0_ref, cprev_ref, ck_ref, cv_ref, *rest):
    nw = len(WEIGHT_NAMES)
    w = dict(zip(WEIGHT_NAMES, rest[:nw]))
    (x1_ref, h2_ref, route_ref, cnt_ref, hlast_ref, ulast_ref, klast_ref, vlast_ref,
     g2_ref, hs_ref, u_ref, ko_ref, vo_ref, conv_c, h_c, k_c, v_c) = rest[nw:]
    lg_ref = (route_ref, cnt_ref)
    i = pl.program_id(0)

    @pl.when(i < n_prompt_tiles)
    def _():
        _prompt_body(i % tiles_per_seq, xp_ref, modp_ref, cosp_ref, sinp_ref, sinks_ref, w, x1_ref, h2_ref, lg_ref,
                     hlast_ref, ulast_ref, klast_ref, vlast_ref, conv_c, h_c, k_c, v_c)

    @pl.when(i >= n_prompt_tiles)
    def _():
        _sample_body(xs_ref, mods_ref, coss_ref, sins_ref, sinks_ref, h0_ref, cprev_ref, ck_ref, cv_ref, w,
                     x1_ref, h2_ref, lg_ref, g2_ref, hs_ref, u_ref, ko_ref, vo_ref)


STRIP_SIZES = tuple(SUBLANES << b for b in range(6))
SORT_ROWS = SEQ_TILE * TOP_K + N_EXPERTS * SUBLANES
TILE_WAIT_SIZES = tuple(SUBLANES << b for b in range(8))


def _for_strips(cnt_ref, off_ref, tile, buf_slot, hbm, sem, to_hbm, act):
    def e_body(e, local):
        n = cnt_ref[tile * N_EXPERTS + e]
        glob = off_ref[tile * N_EXPERTS + e]
        done = 0
        for p in reversed(STRIP_SIZES):
            piece = n & p
            lo = pl.ds(pl.multiple_of(local + done, SUBLANES), p)
            gl = pl.ds(pl.multiple_of(glob + done, SUBLANES), p)

            @pl.when(piece != 0)
            def _():
                if to_hbm:
                    act(pltpu.make_async_copy(buf_slot.at[lo], hbm.at[gl], sem))
                else:
                    act(pltpu.make_async_copy(hbm.at[gl], buf_slot.at[lo], sem))
            done = done + piece
        return local + n
    lax.fori_loop(0, N_EXPERTS, e_body, 0)


def _wait_tile_rows(total, buf_slot, hbm, sem, to_hbm):
    for p in TILE_WAIT_SIZES:
        @pl.when((total & p) != 0)
        def _():
            if to_hbm:
                pltpu.make_async_copy(buf_slot.at[pl.ds(0, p)], hbm.at[pl.ds(0, p)], sem).wait()
            else:
                pltpu.make_async_copy(hbm.at[pl.ds(0, p)], buf_slot.at[pl.ds(0, p)], sem).wait()


def _dispatch_kernel(cnt_ref, off_ref, tot_ref, meta_ref, h2_ref, route_ref, xs_hbm, sbuf, zblk, sem, zsem):
    i = pl.program_id(0)
    nb = pl.num_programs(0)
    slot = i % 2
    n_blocks = xs_hbm.shape[0] // MOE_TM

    route_t = route_ref[...].T
    r_pos = lax.broadcasted_iota(jnp.int32, (SORT_ROWS, SEQ_TILE), 0).astype(F32)
    perm = jnp.zeros((SORT_ROWS, SEQ_TILE), F32)
    for k in range(TOP_K):
        perm = perm + jnp.where(r_pos == route_t[ROUTE_R + k:ROUTE_R + k + 1, :], 1.0, 0.0)
    sbuf[slot] = jnp.dot(perm.astype(BF16), h2_ref[...].astype(BF16), preferred_element_type=F32)

    _for_strips(cnt_ref, off_ref, i, sbuf.at[slot], xs_hbm, sem.at[slot], True, lambda cp: cp.start())

    @pl.when(i > 0)
    def _():
        _wait_tile_rows(tot_ref[jnp.maximum(i - 1, 0)], sbuf.at[1 - slot], xs_hbm, sem.at[1 - slot], True)

    @pl.when(i == nb - 1)
    def _():
        _wait_tile_rows(tot_ref[i], sbuf.at[slot], xs_hbm, sem.at[slot], True)
        zblk[...] = jnp.zeros_like(zblk)

        def for_region_tails(act):
            def e_body(e, carry):
                start = meta_ref[e]
                n = meta_ref[N_EXPERTS + e] - start
                done = 0
                for p in reversed(STRIP_SIZES[:-1]):
                    piece = n & p
                    rows = pl.ds(pl.multiple_of(start + done, SUBLANES), p)

                    @pl.when(piece != 0)
                    def _():
                        act(pltpu.make_async_copy(zblk.at[pl.ds(0, p)], xs_hbm.at[rows], zsem.at[0]))
                    done = done + piece
                return carry
            lax.fori_loop(0, N_EXPERTS, e_body, 0)

        def for_tail_blocks(act):
            def b_body(j, carry):
                act(pltpu.make_async_copy(zblk, xs_hbm.at[pl.ds(pl.multiple_of(j * MOE_TM, MOE_TM), MOE_TM)],
                                          zsem.at[0]))
                return carry
            lax.fori_loop(meta_ref[2 * N_EXPERTS], n_blocks, b_body, 0)

        for_region_tails(lambda cp: cp.start())
        for_tail_blocks(lambda cp: cp.start())
        for_region_tails(lambda cp: cp.wait())
        for_tail_blocks(lambda cp: cp.wait())


def _dispatch(cnt8, tile_off, tot8, meta, h2_all, route, n_rows):
    nt = tot8.shape[0]
    tt = SEQ_TILE
    grid_spec = pltpu.PrefetchScalarGridSpec(
        num_scalar_prefetch=4,
        grid=(nt,),
        in_specs=[pl.BlockSpec((tt, D_MODEL), lambda i, *_: (i, 0)),
                  pl.BlockSpec((tt, LANES), lambda i, *_: (i, 0))],
        out_specs=pl.BlockSpec(memory_space=pl.ANY),
        scratch_shapes=[pltpu.VMEM((2, SORT_ROWS, D_MODEL), F32), pltpu.VMEM((MOE_TM, D_MODEL), F32),
                        pltpu.SemaphoreType.DMA((2,)), pltpu.SemaphoreType.DMA((1,))],
    )
    return pl.pallas_call(
        _dispatch_kernel,
        grid_spec=grid_spec,
        out_shape=jax.ShapeDtypeStruct((n_rows, D_MODEL), F32),
        compiler_params=pltpu.CompilerParams(dimension_semantics=("arbitrary",), vmem_limit_bytes=VMEM_LIMIT),
        name="dispatch",
    )(cnt8, tile_off, tot8, meta, h2_all, route)


def _expert_mlp(x, w1b, b1_ref, w2b, b2_ref, act_ref):
    xb = x.astype(BF16)
    q = D_FF // 4
    for c in range(4):
        zg = jnp.dot(xb, w1b[:, c * q:(c + 1) * q], preferred_element_type=F32) + b1_ref[:, c * q:(c + 1) * q]
        zl = (jnp.dot(xb, w1b[:, D_FF + c * q:D_FF + (c + 1) * q], preferred_element_type=F32)
              + b1_ref[:, D_FF + c * q:D_FF + (c + 1) * q])
        glu = jnp.minimum(zg, SWIGLU_LIMIT)
        lin = jnp.clip(zl, -SWIGLU_LIMIT, SWIGLU_LIMIT)
        act_ref[:, c * q:(c + 1) * q] = (glu * jax.nn.sigmoid(SWIGLU_ALPHA * glu) * (lin + 1.0)).astype(BF16)
    return jnp.dot(act_ref[...], w2b[...], preferred_element_type=F32) + b2_ref[...]


def _moe_kernel(row_ref, size_ref, exp_ref, meta_ref, xs_hbm, w1_ref, b1_ref, w2_ref, b2_ref, ys_hbm,
                xin, yout, w1b, w2b, act, isem, osem):
    i = pl.program_id(0)
    n = pl.num_programs(0)
    slot = i % 2
    tm = MOE_TM
    n_blocks = ys_hbm.shape[0] // tm

    def for_chunk(j, full_fn, half_fn):
        r = pl.multiple_of(row_ref[j], tm)

        @pl.when(size_ref[j] == 2)
        def _():
            full_fn(r)

        @pl.when(size_ref[j] == 1)
        def _():
            half_fn(r)

    def in_full(r, s):
        return pltpu.make_async_copy(xs_hbm.at[pl.ds(r, MOE_CH)], xin.at[s], isem.at[s])

    def in_half(r, s):
        return pltpu.make_async_copy(xs_hbm.at[pl.ds(r, tm)], xin.at[s, pl.ds(0, tm)], isem.at[s])

    def out_full(r, s):
        return pltpu.make_async_copy(yout.at[s], ys_hbm.at[pl.ds(r, MOE_CH)], osem.at[s])

    def out_half(r, s):
        return pltpu.make_async_copy(yout.at[s, pl.ds(0, tm)], ys_hbm.at[pl.ds(r, tm)], osem.at[s])

    def start_in(j, s):
        for_chunk(j, lambda r: in_full(r, s).start(), lambda r: in_half(r, s).start())

    def wait_in(j, s):
        for_chunk(j, lambda r: in_full(r, s).wait(), lambda r: in_half(r, s).wait())

    def start_out(j, s):
        for_chunk(j, lambda r: out_full(r, s).start(), lambda r: out_half(r, s).start())

    def wait_out(j, s):
        for_chunk(j, lambda r: out_full(r, s).wait(), lambda r: out_half(r, s).wait())

    @pl.when(i == 0)
    def _():
        start_in(0, 0)

    @pl.when(i + 1 < n)
    def _():
        start_in(i + 1, 1 - slot)

    @pl.when(i >= 2)
    def _():
        wait_out(i - 2, slot)

    new_expert = jnp.logical_or(i == 0, exp_ref[i] != exp_ref[jnp.maximum(i - 1, 0)])

    @pl.when(jnp.logical_and(new_expert, size_ref[i] > 0))
    def _():
        chunk = 128
        def cast_body(c, carry):
            k0 = pl.multiple_of(c * chunk, chunk)
            w1b[pl.ds(k0, chunk), :] = w1_ref[pl.ds(k0, chunk), :].astype(BF16)
            w2b[pl.ds(k0, chunk), :] = w2_ref[pl.ds(k0, chunk), :].astype(BF16)
            return carry
        lax.fori_loop(0, D_MODEL // chunk, cast_body, 0)

    wait_in(i, slot)

    @pl.when(size_ref[i] == 2)
    def _():
        yout[slot] = _expert_mlp(xin[slot], w1b, b1_ref, w2b, b2_ref, act)

    @pl.when(size_ref[i] == 1)
    def _():
        yout[slot, 0:tm] = _expert_mlp(xin[slot, 0:tm], w1b, b1_ref, w2b, b2_ref, act.at[pl.ds(0, tm)])

    start_out(i, slot)

    @pl.when(i == n - 1)
    def _():
        @pl.when(i >= 1)
        def _():
            wait_out(i - 1, 1 - slot)
        wait_out(i, slot)
        yout[1, 0:tm] = jnp.zeros((tm, D_MODEL), F32)

        def zero_block(j):
            return pltpu.make_async_copy(yout.at[1, pl.ds(0, tm)], ys_hbm.at[pl.ds(pl.multiple_of(j * tm, tm), tm)],
                                         osem.at[1])

        def start_body(j, carry):
            zero_block(j).start()
            return carry

        def wait_body(j, carry):
            zero_block(j).wait()
            return carry
        lax.fori_loop(meta_ref[2 * N_EXPERTS], n_blocks, start_body, 0)
        lax.fori_loop(meta_ref[2 * N_EXPERTS], n_blocks, wait_body, 0)


def _moe(chunk_row, chunk_size, chunk_exp, meta, xs, w1, b1, w2, b2):
    by_expert = lambda i, row, size, exp, m: (exp[i], 0, 0)
    grid_spec = pltpu.PrefetchScalarGridSpec(
        num_scalar_prefetch=4,
        grid=(chunk_row.shape[0],),
        in_specs=[
            pl.BlockSpec(memory_space=pl.ANY),
            pl.BlockSpec((None, D_MODEL, 2 * D_FF), by_expert),
            pl.BlockSpec((None, 1, 2 * D_FF), by_expert),
            pl.BlockSpec((None, D_FF, D_MODEL), by_expert),
            pl.BlockSpec((None, 1, D_MODEL), by_expert),
        ],
        out_specs=pl.BlockSpec(memory_space=pl.ANY),
        scratch_shapes=[pltpu.VMEM((2, MOE_CH, D_MODEL), F32), pltpu.VMEM((2, MOE_CH, D_MODEL), F32),
                        pltpu.VMEM((D_MODEL, 2 * D_FF), BF16), pltpu.VMEM((D_FF, D_MODEL), BF16),
                        pltpu.VMEM((MOE_CH, D_FF), BF16),
                        pltpu.SemaphoreType.DMA((2,)), pltpu.SemaphoreType.DMA((2,))],
    )
    return pl.pallas_call(
        _moe_kernel,
        grid_spec=grid_spec,
        out_shape=jax.ShapeDtypeStruct(xs.shape, F32),
        compiler_params=pltpu.CompilerParams(dimension_semantics=("arbitrary",), vmem_limit_bytes=VMEM_LIMIT),
        name="moe",
    )(chunk_row, chunk_size, chunk_exp, meta, xs, w1, b1, w2, b2)


def _combine_kernel(n_prompt_tiles, cnt_ref, off_ref, tot_ref, ys_hbm, route_ref, x1_ref, modp_ref, g2s_ref, fg_ref,
                    op_ref, os_ref, buf, sem):
    i = pl.program_id(0)
    nb = pl.num_programs(0)
    slot = i % 2

    def fetch(tile, s):
        _for_strips(cnt_ref, off_ref, tile, buf.at[s], ys_hbm, sem.at[s], False, lambda cp: cp.start())

    @pl.when(i == 0)
    def _():
        buf[...] = jnp.zeros_like(buf)
        fetch(0, 0)

    @pl.when(i + 1 < nb)
    def _():
        fetch(i + 1, 1 - slot)

    _wait_tile_rows(tot_ref[i], buf.at[slot], ys_hbm, sem.at[slot], False)

    route = route_ref[...]
    c_pos = lax.broadcasted_iota(jnp.int32, (SEQ_TILE, SORT_ROWS), 1).astype(F32)
    gmat = jnp.zeros((SEQ_TILE, SORT_ROWS), F32)
    for k in range(TOP_K):
        gmat = gmat + jnp.where(c_pos == route[:, ROUTE_R + k:ROUTE_R + k + 1],
                                route[:, ROUTE_G + k:ROUTE_G + k + 1], 0.0)
    g_hi = gmat.astype(BF16)
    g_lo = (gmat - g_hi.astype(F32)).astype(BF16)
    yb = buf[slot].astype(BF16)
    ff = jnp.dot(g_hi, yb, preferred_element_type=F32) + jnp.dot(g_lo, yb, preferred_element_type=F32)
    g2 = jnp.where(i < n_prompt_tiles, modp_ref[5:6, :], g2s_ref[...])
    x = x1_ref[...] + g2 * ff
    y = _rms(x, fg_ref[...])

    @pl.when(i < n_prompt_tiles)
    def _():
        op_ref[...] = y

    @pl.when(i >= n_prompt_tiles)
    def _():
        os_ref[...] = y


def _combine(cnt8, tile_off, tot8, ys, route, x1_all, mod_p, g2_rows, final_g, n_prompt_tiles, tiles_per_batch):
    nt = tot8.shape[0]
    tt = SEQ_TILE
    npt = n_prompt_tiles
    grid_spec = pltpu.PrefetchScalarGridSpec(
        num_scalar_prefetch=3,
        grid=(nt,),
        in_specs=[
            pl.BlockSpec(memory_space=pl.ANY),
            pl.BlockSpec((tt, LANES), lambda i, *_: (i, 0)),
            pl.BlockSpec((tt, D_MODEL), lambda i, *_: (i, 0)),
            pl.BlockSpec((None, 6, D_MODEL), lambda i, *_: (jnp.minimum(i, npt - 1) // tiles_per_batch, 0, 0)),
            pl.BlockSpec((tt, D_MODEL), lambda i, *_: (jnp.maximum(i - npt, 0), 0)),
            pl.BlockSpec((1, D_MODEL), lambda i, *_: (0, 0)),
        ],
        out_specs=(pl.BlockSpec((tt, D_MODEL), lambda i, *_: (jnp.minimum(i, npt - 1), 0)),
                   pl.BlockSpec((tt, D_MODEL), lambda i, *_: (jnp.maximum(i - npt, 0), 0))),
        scratch_shapes=[pltpu.VMEM((2, SORT_ROWS, D_MODEL), F32), pltpu.SemaphoreType.DMA((2,))],
    )
    return pl.pallas_call(
        functools.partial(_combine_kernel, npt),
        grid_spec=grid_spec,
        out_shape=(jax.ShapeDtypeStruct((npt * tt, D_MODEL), F32),
                   jax.ShapeDtypeStruct(((nt - npt) * tt, D_MODEL), F32)),
        compiler_params=pltpu.CompilerParams(dimension_semantics=("arbitrary",), vmem_limit_bytes=VMEM_LIMIT),
        name="combine",
    )(cnt8, tile_off, tot8, ys, route, x1_all, mod_p, g2_rows, final_g)


def _block_diag_halves(wg):
    halves = []
    for hh in range(2):
        rows = []
        for bi in range(4):
            row = [wg[hh * 4 + bi] if bj == bi else jnp.zeros((LRU_BLOCK_W, LRU_BLOCK_W), wg.dtype) for bj in range(4)]
            rows.append(jnp.concatenate(row, axis=1))
        halves.append(jnp.concatenate(rows, axis=0))
    return jnp.stack(halves).astype(BF16)


def _rope_tables(pos):
    half = HEAD_DIM // 2
    inv = ROPE_THETA ** (-jnp.arange(half, dtype=F32) / half)
    ang = pos.astype(F32)[:, None] * inv[None, :]
    cos = jnp.cos(ang)
    sin = jnp.sin(ang)
    cos128 = jnp.concatenate([cos, cos, cos, cos], axis=1)
    sin128 = jnp.concatenate([-sin, sin, -sin, sin], axis=1)
    return cos128, sin128


def _full_spec(arr, grid_rank):
    zeros = (0,) * arr.ndim
    if grid_rank == 1:
        return pl.BlockSpec(arr.shape, lambda i: zeros)
    return pl.BlockSpec(arr.shape, lambda b, j: zeros)


def kernel(x_prompt, x_sample, state_lru_h, state_conv, cache_win_k, cache_win_v, c_prompt, c_sample, w_ada, b_ada, norm1_g, w_in, b_in, conv_w, conv_b, lru_wa, lru_ba, lru_wx, lru_bx, lru_lambda, attn_sinks, w_out, b_out, norm2_g, w_router, b_router, w1, b1, w2, b2, final_g):
    bp, seq, _ = x_prompt.shape
    bd, tdec, _ = x_sample.shape
    assert tdec == SUBLANES and seq % SEQ_TILE == 0 and bd % SAMPLE_BT == 0
    n_prompt = bp * seq
    n_sample = bd * tdec
    n_tok = n_prompt + n_sample
    l = 0

    head_perm = jnp.array([h for c in range(4) for h in (c, GROUP + c)], dtype=jnp.int32)
    qcol_perm = (head_perm[:, None] * HEAD_DIM + jnp.arange(HEAD_DIM, dtype=jnp.int32)[None, :]).reshape(-1)
    o2 = 2 * LRU_WIDTH
    in_perm = jnp.concatenate([jnp.arange(o2, dtype=jnp.int32), o2 + qcol_perm,
                               jnp.arange(o2 + Q_WIDTH, IN_WIDTH, dtype=jnp.int32)])
    out_perm = jnp.concatenate([jnp.arange(LRU_WIDTH, dtype=jnp.int32), LRU_WIDTH + qcol_perm])
    sinks_perm = attn_sinks[l]

    wr = jnp.tile(w_router[l], (1, TOP_K))
    wr_hi = wr.astype(BF16)
    weights = dict(
        norm1_g=norm1_g[l][None, :], w_in=w_in[l][:, in_perm].astype(BF16), b_in=b_in[l][in_perm][None, :],
        conv_w=conv_w[l], conv_b=conv_b[l][None, :],
        ga=_block_diag_halves(lru_wa[l]), gx=_block_diag_halves(lru_wx[l]),
        lru_ba=lru_ba[l][None, :], lru_bx=lru_bx[l][None, :], lam=lru_lambda[l][None, :],
        w_out=w_out[l][out_perm, :].astype(BF16), b_out=b_out[l][None, :], norm2_g=norm2_g[l][None, :],
        wr_hi=wr_hi, wr_lo=(wr - wr_hi.astype(F32)).astype(BF16),
        b_router=jnp.tile(b_router[l], TOP_K)[None, :],
    )
    wlist = [weights[n] for n in WEIGHT_NAMES]

    mod_all = _ada(jnp.concatenate([c_prompt, c_sample], axis=0), w_ada[l], b_ada[l][None, :])
    mod_p = mod_all[:bp].reshape(bp, 6, D_MODEL)
    mod_s = mod_all[bp:].reshape(bd, 6, D_MODEL).transpose(1, 0, 2)

    cos_p, sin_p = _rope_tables(jnp.arange(seq, dtype=jnp.int32))
    cos_s, sin_s = _rope_tables(PAST_LEN + jnp.arange(tdec, dtype=jnp.int32))
    cos_s = jnp.tile(cos_s, (SAMPLE_BT, 1))
    sin_s = jnp.tile(sin_s, (SAMPLE_BT, 1))
    h0_rows = jnp.pad(state_lru_h[l][:, None, :], ((0, 0), (0, tdec - 1), (0, 0))).reshape(n_sample, LRU_WIDTH)
    cprev_rows = jnp.pad(state_conv[l], ((0, 0), (0, tdec - (CONV_W - 1)), (0, 0))).reshape(n_sample, LRU_WIDTH)
    ck = cache_win_k[l].reshape(bd, WINDOW, KV_WIDTH)
    cv = cache_win_v[l].reshape(bd, WINDOW, KV_WIDTH)
    nj = seq // SEQ_TILE
    npt = n_prompt // SEQ_TILE
    nst = n_sample // SEQ_TILE
    p_tile = lambda i: jnp.minimum(i, npt - 1)
    s_tile = lambda i: jnp.maximum(i - npt, 0)
    all_rows = lambda width: pl.BlockSpec((SEQ_TILE, width), lambda i: (i, 0))
    p_rows = lambda width: pl.BlockSpec((SEQ_TILE, width), lambda i: (p_tile(i), 0))
    s_rows = lambda width: pl.BlockSpec((SEQ_TILE, width), lambda i: (s_tile(i), 0))
    p_tail = lambda rows, width: pl.BlockSpec((None, rows, width), lambda i: (p_tile(i) // nj, 0, 0))
    cache_spec = pl.BlockSpec((SAMPLE_BT, WINDOW, KV_WIDTH), lambda i: (s_tile(i), 0, 0))
    mixer_out_shapes = (
        jax.ShapeDtypeStruct((n_tok, D_MODEL), F32),
        jax.ShapeDtypeStruct((n_tok, D_MODEL), F32),
        jax.ShapeDtypeStruct((n_tok, LANES), F32),
        jax.ShapeDtypeStruct((npt + nst, SUBLANES, LANES), F32),
        jax.ShapeDtypeStruct((bp, SUBLANES, LRU_WIDTH), F32),
        jax.ShapeDtypeStruct((bp, SUBLANES, LRU_WIDTH), F32),
        jax.ShapeDtypeStruct((bp, WINDOW, KV_WIDTH), F32),
        jax.ShapeDtypeStruct((bp, WINDOW, KV_WIDTH), F32),
        jax.ShapeDtypeStruct((n_sample, D_MODEL), F32),
        jax.ShapeDtypeStruct((n_sample, LRU_WIDTH), F32),
        jax.ShapeDtypeStruct((n_sample, LRU_WIDTH), F32),
        jax.ShapeDtypeStruct((bd, WINDOW, KV_WIDTH), F32),
        jax.ShapeDtypeStruct((bd, WINDOW, KV_WIDTH), F32),
    )
    (x1_all, h2_all, route, tile_cnt, hlast_p, ulast_p, klast_p, vlast_p, g2_rows, hs_s, u_s, s_k, s_v) = pl.pallas_call(
        functools.partial(_mixer_kernel, npt, nj),
        grid=(npt + nst,),
        in_specs=[p_rows(D_MODEL),
                  pl.BlockSpec((None, 6, D_MODEL), lambda i: (p_tile(i) // nj, 0, 0)),
                  pl.BlockSpec((SEQ_TILE, LANES), lambda i: (p_tile(i) % nj, 0)),
                  pl.BlockSpec((SEQ_TILE, LANES), lambda i: (p_tile(i) % nj, 0)),
                  s_rows(D_MODEL),
                  pl.BlockSpec((6, SAMPLE_BT, D_MODEL), lambda i: (0, s_tile(i), 0)),
                  pl.BlockSpec((SEQ_TILE, LANES), lambda i: (0, 0)),
                  pl.BlockSpec((SEQ_TILE, LANES), lambda i: (0, 0)),
                  pl.BlockSpec(memory_space=pltpu.SMEM),
                  s_rows(LRU_WIDTH), s_rows(LRU_WIDTH), cache_spec, cache_spec]
                 + [_full_spec(a, 1) for a in wlist],
        out_specs=(all_rows(D_MODEL), all_rows(D_MODEL), all_rows(LANES),
                   pl.BlockSpec((None, SUBLANES, LANES), lambda i: (i, 0, 0)),
                   p_tail(SUBLANES, LRU_WIDTH), p_tail(SUBLANES, LRU_WIDTH),
                   p_tail(WINDOW, KV_WIDTH), p_tail(WINDOW, KV_WIDTH),
                   s_rows(D_MODEL), s_rows(LRU_WIDTH), s_rows(LRU_WIDTH), cache_spec, cache_spec),
        out_shape=mixer_out_shapes,
        scratch_shapes=[pltpu.VMEM((SUBLANES, LRU_WIDTH), F32), pltpu.VMEM((SUBLANES, LRU_WIDTH), F32),
                        pltpu.VMEM((WINDOW, KV_WIDTH), F32), pltpu.VMEM((WINDOW, KV_WIDTH), F32)],
        compiler_params=pltpu.CompilerParams(dimension_semantics=("arbitrary",), vmem_limit_bytes=VMEM_LIMIT),
        name="mixer",
    )(x_prompt.reshape(n_prompt, D_MODEL), mod_p, cos_p, sin_p,
      x_sample.reshape(n_sample, D_MODEL), mod_s, cos_s, sin_s, sinks_perm, h0_rows, cprev_rows, ck, cv, *wlist)

    n_tiles = npt + nst
    n_assign = n_tok * TOP_K
    max_rows = n_assign + n_tiles * N_EXPERTS * (SUBLANES - 1) + N_EXPERTS * (MOE_TM - 1)
    n_blocks = -(-max_rows // MOE_TM)
    cnt = tile_cnt[:, 0, :N_EXPERTS].astype(jnp.int32)
    cnt8 = (cnt + SUBLANES - 1) // SUBLANES * SUBLANES
    counts = jnp.sum(cnt8, axis=0)
    pcounts = (counts + MOE_TM - 1) // MOE_TM * MOE_TM
    pend = jnp.cumsum(pcounts)
    pstart = pend - pcounts
    tile_off = pstart[None, :] + jnp.cumsum(cnt8, axis=0) - cnt8
    tot8 = jnp.sum(cnt8, axis=1)
    meta = jnp.concatenate([pstart + counts, pend, pend[-1:] // MOE_TM]).astype(jnp.int32)
    cnt8_flat = cnt8.reshape(-1)
    off_flat = tile_off.reshape(-1).astype(jnp.int32)

    xs = _dispatch(cnt8_flat, off_flat, tot8, meta, h2_all, route, n_blocks * MOE_TM)

    n_tm = pcounts // MOE_TM
    n_ch = (n_tm + 1) // 2
    ch_end = jnp.cumsum(n_ch)
    ch_start = ch_end - n_ch
    n_chunks = (n_blocks + N_EXPERTS + 1) // 2
    ci = jnp.arange(n_chunks, dtype=jnp.int32)
    owner = jnp.logical_and(ch_start[None, :] <= ci[:, None], ci[:, None] < ch_end[None, :])
    pick = lambda v: jnp.sum(jnp.where(owner, v[None, :], 0), axis=1)
    live = ci < ch_end[-1]
    local = ci - pick(ch_start)
    chunk_exp = jnp.where(live, pick(jnp.arange(N_EXPERTS, dtype=jnp.int32)), N_EXPERTS - 1).astype(jnp.int32)
    chunk_row = jnp.where(live, pick(pstart) + local * MOE_CH, 0).astype(jnp.int32)
    chunk_size = jnp.where(live, jnp.where(2 * local + 2 <= pick(n_tm), 2, 1), 0).astype(jnp.int32)
    ys = _moe(chunk_row, chunk_size, chunk_exp, meta, xs, w1[l], b1[l][:, None, :], w2[l], b2[l][:, None, :])

    y_p, y_s = _combine(cnt8_flat, off_flat, tot8, ys, route, x1_all, mod_p, g2_rows,
                        final_g[None, :], n_prompt // SEQ_TILE, seq // SEQ_TILE)

    y_prompt = y_p.reshape(bp, seq, D_MODEL)
    y_sample = y_s.reshape(bd, tdec, D_MODEL)
    p_h = hlast_p[:, SUBLANES - 1, :][None]
    p_c = ulast_p[:, SUBLANES - (CONV_W - 1):, :][None]
    p_k = klast_p.reshape(1, bp, WINDOW, N_KV_HEADS, HEAD_DIM)
    p_v = vlast_p.reshape(1, bp, WINDOW, N_KV_HEADS, HEAD_DIM)
    s_h = hs_s.reshape(bd, tdec, LRU_WIDTH)[:, tdec - 1, :][None]
    s_c = u_s.reshape(bd, tdec, LRU_WIDTH)[:, tdec - (CONV_W - 1):, :][None]
    s_kk = s_k.reshape(1, bd, WINDOW, N_KV_HEADS, HEAD_DIM)
    s_vv = s_v.reshape(1, bd, WINDOW, N_KV_HEADS, HEAD_DIM)
    return (y_prompt, y_sample, p_h, p_c, p_k, p_v, s_h, s_c, s_kk, s_vv)
```

```python
import functools

import jax
import jax.numpy as jnp
from jax import lax
from jax.experimental import pallas as pl
from jax.experimental.pallas import tpu as pltpu

F32 = jnp.float32
BF16 = jnp.bfloat16

D_MODEL = 1024
LRU_WIDTH = 512
LRU_BLOCKS = 8
LRU_BLOCK_W = LRU_WIDTH // LRU_BLOCKS
CONV_W = 4
LRU_C = 8.0
HEAD_DIM = 64
N_HEADS = 8
N_KV_HEADS = 2
GROUP = N_HEADS // N_KV_HEADS
WINDOW = 128
ROPE_THETA = 10000.0
N_EXPERTS = 32
TOP_K = 4
D_FF = D_MODEL
SWIGLU_LIMIT = 7.0
SWIGLU_ALPHA = 1.702
NORM_EPS = 1e-5
PAST_LEN = 8192
Q_WIDTH = N_HEADS * HEAD_DIM
KV_WIDTH = N_KV_HEADS * HEAD_DIM
IN_WIDTH = 2 * LRU_WIDTH + Q_WIDTH + 2 * KV_WIDTH

LANES = 128
SUBLANES = 8
SEQ_TILE = 256
SAMPLE_BT = 32
MOE_TM = 256
MOE_CH = 2 * MOE_TM
COMB_TT = 256
NEG_BIG = -1e30
VMEM_LIMIT = 56 * 1024 * 1024


def _rms(x, g):
    return x * lax.rsqrt(jnp.mean(x * x, axis=-1, keepdims=True) + NORM_EPS) * g


def _shift_rows(x, d, fill, rowid):
    return jnp.where(rowid >= d, pltpu.roll(x, d, axis=0), fill)


def _lin_scan(a, b, seg, rowid):
    d = 1
    while d < seg:
        a_s = _shift_rows(a, d, 1.0, rowid)
        b_s = _shift_rows(b, d, 0.0, rowid)
        b = a * b_s + b
        a = a * a_s
        d *= 2
    return a, b


def _rope128(x, cos, sin_signed, first_half):
    sw = jnp.where(first_half, pltpu.roll(x, LANES - HEAD_DIM // 2, axis=1), pltpu.roll(x, HEAD_DIM // 2, axis=1))
    return x * cos + sw * sin_signed


def _neg_expm1(y):
    return -(jnp.tanh(0.5 * y) * (jnp.exp(y) + 1.0))


def _softplus(x):
    return jnp.maximum(x, 0.0) + jnp.log1p(jnp.exp(-jnp.abs(x)))


def _lru_coeffs(uc, w, first_pos_mask):
    ub = uc.astype(BF16)
    half = LRU_WIDTH // 2
    ra = jnp.concatenate([jnp.dot(ub[:, :half], w['ga'][0], preferred_element_type=F32),
                          jnp.dot(ub[:, half:], w['ga'][1], preferred_element_type=F32)], axis=1)
    rx = jnp.concatenate([jnp.dot(ub[:, :half], w['gx'][0], preferred_element_type=F32),
                          jnp.dot(ub[:, half:], w['gx'][1], preferred_element_type=F32)], axis=1)
    r = jax.nn.sigmoid(ra + w['lru_ba'][...])
    i = jax.nn.sigmoid(rx + w['lru_bx'][...])
    log_a = -LRU_C * r * _softplus(-w['lam'][...])
    a = jnp.exp(log_a)
    mult = jnp.sqrt(_neg_expm1(2.0 * log_a))
    if first_pos_mask is not None:
        mult = jnp.where(first_pos_mask, 1.0, mult)
    return a, mult * i * uc


def _conv_taps(u, s1, s2, s3, w):
    cw = w['conv_w']
    return w['conv_b'][...] + s3 * cw[0:1, :] + s2 * cw[1:2, :] + s1 * cw[2:3, :] + u * cw[3:4, :]


def _in_proj(x, mod, w):
    sh1, sc1 = mod
    h = _rms(x, w['norm1_g'][...]) * (1.0 + sc1) + sh1
    return jnp.dot(h.astype(BF16), w['w_in'][...], preferred_element_type=F32) + w['b_in'][...]


def _post_mix(x, mix, mod, w):
    g1, sh2, sc2 = mod
    x1 = x + g1 * (jnp.dot(mix.astype(BF16), w['w_out'][...], preferred_element_type=F32) + w['b_out'][...])
    h2 = _rms(x1, w['norm2_g'][...]) * (1.0 + sc2) + sh2
    h2_hi = h2.astype(BF16)
    h2_lo = (h2 - h2_hi.astype(F32)).astype(BF16)
    logits = (jnp.dot(h2_hi, w['wr_hi'][...], preferred_element_type=F32)
              + jnp.dot(h2_lo, w['wr_hi'][...], preferred_element_type=F32)
              + jnp.dot(h2_hi, w['wr_lo'][...], preferred_element_type=F32)) + w['b_router'][...]
    return x1, h2, logits


ROUTE_E, ROUTE_G, ROUTE_R = 0, TOP_K, 2 * TOP_K


def _lane_roll1(v, shift):
    return pltpu.roll(jnp.broadcast_to(v, (SUBLANES, LANES)), shift, axis=1)[0:1]


def _route_tile(lg):
    rows = lg.shape[0]
    lane = lax.broadcasted_iota(jnp.int32, (1, LANES), 1)
    e_of = lane % N_EXPERTS
    grp = lane // N_EXPERTS
    onehot = jnp.zeros((rows, LANES), F32)
    vals, ids = [], []
    for k in range(TOP_K):
        m = jnp.max(lg, axis=1, keepdims=True)
        idx = jnp.min(jnp.where(lg == m, e_of, N_EXPERTS), axis=1, keepdims=True)
        sel = e_of == idx
        lg = jnp.where(sel, -jnp.inf, lg)
        onehot = jnp.where(jnp.logical_and(sel, grp == k), 1.0, onehot)
        vals.append(m)
        ids.append(idx)
    ex = [jnp.exp(v - vals[0]) for v in vals]
    denom = ex[0] + ex[1] + ex[2] + ex[3]

    r_i = lax.broadcasted_iota(jnp.int32, (rows, rows), 0)
    c_i = lax.broadcasted_iota(jnp.int32, (rows, rows), 1)
    strict_lower = jnp.where(r_i > c_i, 1.0, 0.0).astype(BF16)
    prefix = jnp.dot(strict_lower, onehot.astype(BF16), preferred_element_type=F32)
    cnt = jnp.sum(onehot, axis=0, keepdims=True)
    base = jnp.zeros((1, LANES), F32)
    tot = cnt
    for s in range(1, TOP_K):
        rolled = _lane_roll1(cnt, s * N_EXPERTS)
        base = base + jnp.where(lane >= s * N_EXPERTS, rolled, 0.0)
        tot = tot + rolled
    pad_cnt = jnp.floor((tot + (SUBLANES - 1.0)) * (1.0 / SUBLANES)) * SUBLANES
    inc = pad_cnt
    d = 1
    while d < N_EXPERTS:
        inc = inc + jnp.where(e_of >= d, _lane_roll1(inc, d), 0.0)
        d *= 2
    strip_start = inc - pad_cnt
    ranked = onehot * (prefix + base + strip_start)

    route = jnp.zeros((rows, LANES), F32)
    for k in range(TOP_K):
        rank_k = jnp.sum(jnp.where(grp == k, ranked, 0.0), axis=1, keepdims=True)
        route = jnp.where(lane == ROUTE_E + k, ids[k].astype(F32), route)
        route = jnp.where(lane == ROUTE_G + k, ex[k] / denom, route)
        route = jnp.where(lane == ROUTE_R + k, rank_k, route)
    return route, tot


def _softmax_sink_pv(s, sink_col, v_bf16):
    m = jnp.maximum(jnp.max(s, axis=-1, keepdims=True), sink_col)
    p = jnp.exp(s - m)
    denom = jnp.sum(p, axis=-1, keepdims=True) + jnp.exp(sink_col - m)
    return p, denom


WEIGHT_NAMES = ('norm1_g', 'w_in', 'b_in', 'conv_w', 'conv_b', 'ga', 'gx', 'lru_ba', 'lru_bx', 'lam',
                'w_out', 'b_out', 'norm2_g', 'wr_hi', 'wr_lo', 'b_router')


def _ada_kernel(c_ref, w_ref, b_ref, o_ref):
    c = c_ref[...]
    s = c * jax.nn.sigmoid(c)
    o_ref[...] = jnp.dot(s, w_ref[...], preferred_element_type=F32, precision=lax.Precision.HIGHEST) + b_ref[...]


def _ada(c_all, w_ada, b_ada):
    rows = c_all.shape[0]
    return pl.pallas_call(
        _ada_kernel,
        grid=(6,),
        in_specs=[pl.BlockSpec((rows, D_MODEL), lambda i: (0, 0)),
                  pl.BlockSpec((D_MODEL, D_MODEL), lambda i: (0, i)),
                  pl.BlockSpec((1, D_MODEL), lambda i: (0, i))],
        out_specs=pl.BlockSpec((rows, D_MODEL), lambda i: (0, i)),
        out_shape=jax.ShapeDtypeStruct((rows, 6 * D_MODEL), F32),
        compiler_params=pltpu.CompilerParams(dimension_semantics=("arbitrary",), vmem_limit_bytes=VMEM_LIMIT),
        name="ada",
    )(c_all, w_ada, b_ada)


def _prompt_body(j, x_ref, mod_ref, cos_ref, sin_ref, sinks_ref, w, x1_ref, h2_ref, lg_ref,
                 hlast_ref, ulast_ref, klast_ref, vlast_ref, conv_c, h_c, k_c, v_c):
    ts = SEQ_TILE

    @pl.when(j == 0)
    def _():
        conv_c[...] = jnp.zeros_like(conv_c)
        h_c[...] = jnp.zeros_like(h_c)
        k_c[...] = jnp.zeros_like(k_c)
        v_c[...] = jnp.zeros_like(v_c)

    x = x_ref[...]
    mod = mod_ref[...]
    proj = _in_proj(x, (mod[0:1], mod[1:2]), w)
    u = proj[:, :LRU_WIDTH]
    gate = proj[:, LRU_WIDTH:2 * LRU_WIDTH]
    o2 = 2 * LRU_WIDTH

    rowid = lax.broadcasted_iota(jnp.int32, (ts, 1), 0)
    u_ext = jnp.concatenate([conv_c[...], u], axis=0)
    s1, s2, s3 = (pltpu.roll(u_ext, d, axis=0)[SUBLANES:] for d in (1, 2, 3))
    uc = _conv_taps(u, s1, s2, s3, w)
    conv_c[...] = u[ts - SUBLANES:]
    ulast_ref[...] = u[ts - SUBLANES:]

    first_pos = jnp.logical_and(rowid == 0, j == 0)
    a, bt = _lru_coeffs(uc, w, first_pos)
    a_cum, b_cum = _lin_scan(a, bt, ts, rowid)
    hs = b_cum + a_cum * h_c[0:1, :]
    h_tail = hs[ts - SUBLANES:]
    h_c[...] = jnp.broadcast_to(h_tail[SUBLANES - 1:SUBLANES, :], h_c.shape)
    hlast_ref[...] = h_tail
    lru_out = hs * jax.nn.gelu(gate)

    cos = cos_ref[...]
    sin = sin_ref[...]
    lane = lax.broadcasted_iota(jnp.int32, (1, LANES), 1)
    first_half = (lane % HEAD_DIM) < (HEAD_DIM // 2)
    qcols = [_rope128(proj[:, o2 + c * LANES:o2 + (c + 1) * LANES], cos, sin, first_half) for c in range(4)]
    k = _rope128(proj[:, o2 + Q_WIDTH:o2 + Q_WIDTH + KV_WIDTH], cos, sin, first_half)
    v = proj[:, o2 + Q_WIDTH + KV_WIDTH:]
    k_ext = jnp.concatenate([k_c[...], k], axis=0).astype(BF16)
    v_ext = jnp.concatenate([v_c[...], v], axis=0).astype(BF16)
    k_c[...] = k[ts - WINDOW:]
    v_c[...] = v[ts - WINDOW:]
    klast_ref[...] = k[ts - WINDOW:]
    vlast_ref[...] = v[ts - WINDOW:]

    qi = lax.broadcasted_iota(jnp.int32, (WINDOW, 2 * WINDOW), 0)
    kj = lax.broadcasted_iota(jnp.int32, (WINDOW, 2 * WINDOW), 1)
    band = jnp.logical_and(kj > qi, kj <= qi + WINDOW)
    lane_lo = lane < HEAD_DIM
    grow = lax.broadcasted_iota(jnp.int32, (GROUP * WINDOW, 1), 0) // WINDOW
    attn_cols = [[] for _ in range(4)]
    for blk in range(ts // WINDOW):
        prev_ok = jnp.logical_or(j > 0, blk > 0)
        mask = jnp.logical_and(band, jnp.logical_or(kj >= WINDOW, prev_ok))
        mask4 = jnp.concatenate([mask] * GROUP, axis=0)
        kb = k_ext[blk * WINDOW:(blk + 2) * WINDOW]
        vb = v_ext[blk * WINDOW:(blk + 2) * WINDOW]
        outs = []
        for kv in range(N_KV_HEADS):
            sel = lane_lo if kv == 0 else jnp.logical_not(lane_lo)
            qs = jnp.concatenate(
                [jnp.where(sel, qc[blk * WINDOW:(blk + 1) * WINDOW], 0.0) for qc in qcols], axis=0).astype(BF16)
            s = lax.dot_general(qs, kb, (((1,), (1,)), ((), ())), preferred_element_type=F32) * (HEAD_DIM ** -0.5)
            s = jnp.where(mask4, s, NEG_BIG)
            sink_col = jnp.zeros((GROUP * WINDOW, 1), F32)
            for g in range(GROUP):
                sink_col = jnp.where(grow == g, sinks_ref[kv * GROUP + g], sink_col)
            p, denom = _softmax_sink_pv(s, sink_col, vb)
            outs.append(jnp.dot(p.astype(BF16), vb, preferred_element_type=F32) / denom)
        for c in range(4):
            attn_cols[c].append(jnp.where(lane_lo, outs[0][c * WINDOW:(c + 1) * WINDOW],
                                          outs[1][c * WINDOW:(c + 1) * WINDOW]))
    attn = jnp.concatenate([jnp.concatenate(cols, axis=0) for cols in attn_cols], axis=1)

    mix = jnp.concatenate([lru_out, attn], axis=1)
    x1, h2, logits = _post_mix(x, mix, (mod[2:3], mod[3:4], mod[4:5]), w)
    x1_ref[...] = x1
    h2_ref[...] = h2
    route, tot = _route_tile(logits)
    lg_ref[0][...] = route
    lg_ref[1][...] = jnp.broadcast_to(tot, lg_ref[1].shape)


def _expand_rows(m, t):
    b, wd = m.shape
    return jnp.broadcast_to(m[:, None, :], (b, t, wd)).reshape(b * t, wd)


def _sample_body(x_ref, mod_ref, cos_ref, sin_ref, sinks_ref, h0_ref, cprev_ref, ck_ref, cv_ref, w,
                 x1_ref, h2_ref, lg_ref, g2_ref, hs_ref, u_ref, ko_ref, vo_ref):
    bt_, t = SAMPLE_BT, SUBLANES
    rows = bt_ * t

    x = x_ref[...]
    mods = [_expand_rows(mod_ref[i], t) for i in range(6)]
    proj = _in_proj(x, (mods[0], mods[1]), w)
    u = proj[:, :LRU_WIDTH]
    gate = proj[:, LRU_WIDTH:2 * LRU_WIDTH]
    o2 = 2 * LRU_WIDTH
    u_ref[...] = u

    rowid = lax.broadcasted_iota(jnp.int32, (rows, 1), 0) % t
    cprev = cprev_ref[...]
    taps = []
    for d in (1, 2, 3):
        taps.append(jnp.where(rowid >= d, pltpu.roll(u, d, axis=0),
                              pltpu.roll(cprev, (d - (CONV_W - 1)) % rows, axis=0)))
    uc = _conv_taps(u, taps[0], taps[1], taps[2], w)

    a, bt = _lru_coeffs(uc, w, None)
    bt = bt + a * h0_ref[...]
    _, hs = _lin_scan(a, bt, t, rowid)
    hs_ref[...] = hs
    lru_out = hs * jax.nn.gelu(gate)

    cos = cos_ref[...]
    sin = sin_ref[...]
    lane = lax.broadcasted_iota(jnp.int32, (1, LANES), 1)
    first_half = (lane % HEAD_DIM) < (HEAD_DIM // 2)
    qcols = [_rope128(proj[:, o2 + c * LANES:o2 + (c + 1) * LANES], cos, sin, first_half) for c in range(4)]
    k = _rope128(proj[:, o2 + Q_WIDTH:o2 + Q_WIDTH + KV_WIDTH], cos, sin, first_half)
    v = proj[:, o2 + Q_WIDTH + KV_WIDTH:]
    k3 = k.reshape(bt_, t, KV_WIDTH)
    v3 = v.reshape(bt_, t, KV_WIDTH)
    ck = ck_ref[...]
    cv = cv_ref[...]
    ko_ref[:, :WINDOW - t, :] = ck[:, t:, :]
    ko_ref[:, WINDOW - t:, :] = k3
    vo_ref[:, :WINDOW - t, :] = cv[:, t:, :]
    vo_ref[:, WINDOW - t:, :] = v3

    ckb, cvb, k3b, v3b = ck.astype(BF16), cv.astype(BF16), k3.astype(BF16), v3.astype(BF16)
    lane_lo = lane < HEAD_DIM
    gq = GROUP * t
    tq = lax.broadcasted_iota(jnp.int32, (1, gq, 1), 1) % t
    mask_c = lax.broadcasted_iota(jnp.int32, (1, gq, WINDOW), 2) > tq
    mask_n = lax.broadcasted_iota(jnp.int32, (1, gq, t), 2) <= tq
    grow = lax.broadcasted_iota(jnp.int32, (1, gq, 1), 1) // t
    bdims = (((2,), (2,)), ((0,), (0,)))
    pdims = (((2,), (1,)), ((0,), (0,)))
    outs = []
    for kv in range(N_KV_HEADS):
        sel = lane_lo if kv == 0 else jnp.logical_not(lane_lo)
        q3 = jnp.concatenate([jnp.where(sel, qc, 0.0).reshape(bt_, t, LANES) for qc in qcols], axis=1).astype(BF16)
        sc = lax.dot_general(q3, ckb, bdims, preferred_element_type=F32) * (HEAD_DIM ** -0.5)
        sn = lax.dot_general(q3, k3b, bdims, preferred_element_type=F32) * (HEAD_DIM ** -0.5)
        sc = jnp.where(mask_c, sc, NEG_BIG)
        sn = jnp.where(mask_n, sn, NEG_BIG)
        sink_col = jnp.zeros((1, gq, 1), F32)
        for g in range(GROUP):
            sink_col = jnp.where(grow == g, sinks_ref[kv * GROUP + g], sink_col)
        m = jnp.maximum(jnp.maximum(jnp.max(sc, axis=-1, keepdims=True), jnp.max(sn, axis=-1, keepdims=True)),
                        sink_col)
        pc = jnp.exp(sc - m)
        pn = jnp.exp(sn - m)
        denom = jnp.sum(pc, axis=-1, keepdims=True) + jnp.sum(pn, axis=-1, keepdims=True) + jnp.exp(sink_col - m)
        o = (lax.dot_general(pc.astype(BF16), cvb, pdims, preferred_element_type=F32)
             + lax.dot_general(pn.astype(BF16), v3b, pdims, preferred_element_type=F32)) / denom
        outs.append(o)
    attn = jnp.concatenate(
        [jnp.where(lane_lo, outs[0][:, c * t:(c + 1) * t, :], outs[1][:, c * t:(c + 1) * t, :]).reshape(rows, LANES)
         for c in range(4)], axis=1)

    mix = jnp.concatenate([lru_out, attn], axis=1)
    x1, h2, logits = _post_mix(x, mix, (mods[2], mods[3], mods[4]), w)
    x1_ref[...] = x1
    h2_ref[...] = h2
    route, tot = _route_tile(logits)
    lg_ref[0][...] = route
    lg_ref[1][...] = jnp.broadcast_to(tot, lg_ref[1].shape)
    g2_ref[...] = mods[5]


def _mixer_kernel(n_prompt_tiles, tiles_per_seq,
                  xp_ref, modp_ref, cosp_ref, sinp_ref, xs_ref, mods_ref, coss_ref, sins_ref, sinks_ref,
                  h0_ref, cprev_ref, ck_ref, cv_ref, *rest):
    nw = len(WEIGHT_NAMES)
    w = dict(zip(WEIGHT_NAMES, rest[:nw]))
    (x1_ref, h2_ref, route_ref, cnt_ref, hlast_ref, ulast_ref, klast_ref, vlast_ref,
     g2_ref, hs_ref, u_ref, ko_ref, vo_ref, conv_c, h_c, k_c, v_c) = rest[nw:]
    lg_ref = (route_ref, cnt_ref)
    i = pl.program_id(0)

    @pl.when(i < n_prompt_tiles)
    def _():
        _prompt_body(i % tiles_per_seq, xp_ref, modp_ref, cosp_ref, sinp_ref, sinks_ref, w, x1_ref, h2_ref, lg_ref,
                     hlast_ref, ulast_ref, klast_ref, vlast_ref, conv_c, h_c, k_c, v_c)

    @pl.when(i >= n_prompt_tiles)
    def _():
        _sample_body(xs_ref, mods_ref, coss_ref, sins_ref, sinks_ref, h0_ref, cprev_ref, ck_ref, cv_ref, w,
                     x1_ref, h2_ref, lg_ref, g2_ref, hs_ref, u_ref, ko_ref, vo_ref)


PACK_W = D_MODEL // 2
U32 = jnp.uint32


def _pack_bf16_pairs(x):
    xb = x.astype(BF16).astype(F32)
    lo = lax.bitcast_convert_type(xb[:, :PACK_W], U32)
    hi = lax.bitcast_convert_type(xb[:, PACK_W:], U32)
    return jnp.bitwise_or(jnp.bitwise_and(hi, jnp.uint32(0xFFFF0000)), lax.shift_right_logical(lo, jnp.uint32(16)))


def _unpack_bf16_pairs(w):
    lo = lax.bitcast_convert_type(lax.shift_left(w, jnp.uint32(16)), F32).astype(BF16)
    hi = lax.bitcast_convert_type(jnp.bitwise_and(w, jnp.uint32(0xFFFF0000)), F32).astype(BF16)
    return lo, hi


STRIP_SIZES = tuple(SUBLANES << b for b in range(6))
SORT_ROWS = SEQ_TILE * TOP_K + N_EXPERTS * SUBLANES
TILE_WAIT_SIZES = tuple(SUBLANES << b for b in range(8))


def _for_strips(cnt_ref, off_ref, tile, buf_slot, hbm, sem, to_hbm, act):
    def e_body(e, local):
        n = cnt_ref[tile * N_EXPERTS + e]
        glob = off_ref[tile * N_EXPERTS + e]
        done = 0
        for p in reversed(STRIP_SIZES):
            piece = n & p
            lo = pl.ds(pl.multiple_of(local + done, SUBLANES), p)
            gl = pl.ds(pl.multiple_of(glob + done, SUBLANES), p)

            @pl.when(piece != 0)
            def _():
                if to_hbm:
                    act(pltpu.make_async_copy(buf_slot.at[lo], hbm.at[gl], sem))
                else:
                    act(pltpu.make_async_copy(hbm.at[gl], buf_slot.at[lo], sem))
            done = done + piece
        return local + n
    lax.fori_loop(0, N_EXPERTS, e_body, 0)


def _wait_tile_rows(total, buf_slot, hbm, sem, to_hbm):
    for p in TILE_WAIT_SIZES:
        @pl.when((total & p) != 0)
        def _():
            if to_hbm:
                pltpu.make_async_copy(buf_slot.at[pl.ds(0, p)], hbm.at[pl.ds(0, p)], sem).wait()
            else:
                pltpu.make_async_copy(hbm.at[pl.ds(0, p)], buf_slot.at[pl.ds(0, p)], sem).wait()


def _dispatch_kernel(cnt_ref, off_ref, tot_ref, meta_ref, h2_ref, route_ref, xs_hbm, sbuf, zblk, sem, zsem):
    i = pl.program_id(0)
    nb = pl.num_programs(0)
    slot = i % 2
    n_blocks = xs_hbm.shape[0] // MOE_TM

    route_t = route_ref[...].T
    r_pos = lax.broadcasted_iota(jnp.int32, (SORT_ROWS, SEQ_TILE), 0).astype(F32)
    perm = jnp.zeros((SORT_ROWS, SEQ_TILE), F32)
    for k in range(TOP_K):
        perm = perm + jnp.where(r_pos == route_t[ROUTE_R + k:ROUTE_R + k + 1, :], 1.0, 0.0)
    sbuf[slot] = _pack_bf16_pairs(jnp.dot(perm.astype(BF16), h2_ref[...].astype(BF16), preferred_element_type=F32))

    _for_strips(cnt_ref, off_ref, i, sbuf.at[slot], xs_hbm, sem.at[slot], True, lambda cp: cp.start())

    @pl.when(i > 0)
    def _():
        _wait_tile_rows(tot_ref[jnp.maximum(i - 1, 0)], sbuf.at[1 - slot], xs_hbm, sem.at[1 - slot], True)

    @pl.when(i == nb - 1)
    def _():
        _wait_tile_rows(tot_ref[i], sbuf.at[slot], xs_hbm, sem.at[slot], True)
        zblk[...] = jnp.zeros_like(zblk)

        def for_region_tails(act):
            def e_body(e, carry):
                start = meta_ref[e]
                n = meta_ref[N_EXPERTS + e] - start
                done = 0
                for p in reversed(STRIP_SIZES[:-1]):
                    piece = n & p
                    rows = pl.ds(pl.multiple_of(start + done, SUBLANES), p)

                    @pl.when(piece != 0)
                    def _():
                        act(pltpu.make_async_copy(zblk.at[pl.ds(0, p)], xs_hbm.at[rows], zsem.at[0]))
                    done = done + piece
                return carry
            lax.fori_loop(0, N_EXPERTS, e_body, 0)

        def for_tail_blocks(act):
            def b_body(j, carry):
                act(pltpu.make_async_copy(zblk, xs_hbm.at[pl.ds(pl.multiple_of(j * MOE_TM, MOE_TM), MOE_TM)],
                                          zsem.at[0]))
                return carry
            lax.fori_loop(meta_ref[2 * N_EXPERTS], n_blocks, b_body, 0)

        for_region_tails(lambda cp: cp.start())
        for_tail_blocks(lambda cp: cp.start())
        for_region_tails(lambda cp: cp.wait())
        for_tail_blocks(lambda cp: cp.wait())


def _dispatch(cnt8, tile_off, tot8, meta, h2_all, route, n_rows):
    nt = tot8.shape[0]
    tt = SEQ_TILE
    grid_spec = pltpu.PrefetchScalarGridSpec(
        num_scalar_prefetch=4,
        grid=(nt,),
        in_specs=[pl.BlockSpec((tt, D_MODEL), lambda i, *_: (i, 0)),
                  pl.BlockSpec((tt, LANES), lambda i, *_: (i, 0))],
        out_specs=pl.BlockSpec(memory_space=pl.ANY),
        scratch_shapes=[pltpu.VMEM((2, SORT_ROWS, PACK_W), U32), pltpu.VMEM((MOE_TM, PACK_W), U32),
                        pltpu.SemaphoreType.DMA((2,)), pltpu.SemaphoreType.DMA((1,))],
    )
    return pl.pallas_call(
        _dispatch_kernel,
        grid_spec=grid_spec,
        out_shape=jax.ShapeDtypeStruct((n_rows, PACK_W), U32),
        compiler_params=pltpu.CompilerParams(dimension_semantics=("arbitrary",), vmem_limit_bytes=VMEM_LIMIT),
        name="dispatch",
    )(cnt8, tile_off, tot8, meta, h2_all, route)


def _expert_mlp(words, w1b, b1_ref, w2b, b2_ref, act_ref):
    xb = jnp.concatenate(_unpack_bf16_pairs(words), axis=1)
    q = D_FF // 4
    for c in range(4):
        zg = jnp.dot(xb, w1b[:, c * q:(c + 1) * q], preferred_element_type=F32) + b1_ref[:, c * q:(c + 1) * q]
        zl = (jnp.dot(xb, w1b[:, D_FF + c * q:D_FF + (c + 1) * q], preferred_element_type=F32)
              + b1_ref[:, D_FF + c * q:D_FF + (c + 1) * q])
        glu = jnp.minimum(zg, SWIGLU_LIMIT)
        lin = jnp.clip(zl, -SWIGLU_LIMIT, SWIGLU_LIMIT)
        act_ref[:, c * q:(c + 1) * q] = (glu * jax.nn.sigmoid(SWIGLU_ALPHA * glu) * (lin + 1.0)).astype(BF16)
    return _pack_bf16_pairs(jnp.dot(act_ref[...], w2b[...], preferred_element_type=F32) + b2_ref[...])


def _moe_kernel(row_ref, size_ref, exp_ref, meta_ref, xs_hbm, w1_ref, b1_ref, w2_ref, b2_ref, ys_hbm,
                xin, yout, w1b, w2b, act, isem, osem):
    i = pl.program_id(0)
    n = pl.num_programs(0)
    slot = i % 2
    tm = MOE_TM
    n_blocks = ys_hbm.shape[0] // tm

    def for_chunk(j, full_fn, half_fn):
        r = pl.multiple_of(row_ref[j], tm)

        @pl.when(size_ref[j] == 2)
        def _():
            full_fn(r)

        @pl.when(size_ref[j] == 1)
        def _():
            half_fn(r)

    def in_full(r, s):
        return pltpu.make_async_copy(xs_hbm.at[pl.ds(r, MOE_CH)], xin.at[s], isem.at[s])

    def in_half(r, s):
        return pltpu.make_async_copy(xs_hbm.at[pl.ds(r, tm)], xin.at[s, pl.ds(0, tm)], isem.at[s])

    def out_full(r, s):
        return pltpu.make_async_copy(yout.at[s], ys_hbm.at[pl.ds(r, MOE_CH)], osem.at[s])

    def out_half(r, s):
        return pltpu.make_async_copy(yout.at[s, pl.ds(0, tm)], ys_hbm.at[pl.ds(r, tm)], osem.at[s])

    def start_in(j, s):
        for_chunk(j, lambda r: in_full(r, s).start(), lambda r: in_half(r, s).start())

    def wait_in(j, s):
        for_chunk(j, lambda r: in_full(r, s).wait(), lambda r: in_half(r, s).wait())

    def start_out(j, s):
        for_chunk(j, lambda r: out_full(r, s).start(), lambda r: out_half(r, s).start())

    def wait_out(j, s):
        for_chunk(j, lambda r: out_full(r, s).wait(), lambda r: out_half(r, s).wait())

    @pl.when(i == 0)
    def _():
        start_in(0, 0)

    @pl.when(i + 1 < n)
    def _():
        start_in(i + 1, 1 - slot)

    @pl.when(i >= 2)
    def _():
        wait_out(i - 2, slot)

    new_expert = jnp.logical_or(i == 0, exp_ref[i] != exp_ref[jnp.maximum(i - 1, 0)])

    @pl.when(jnp.logical_and(new_expert, size_ref[i] > 0))
    def _():
        chunk = 128
        def cast_body(c, carry):
            k0 = pl.multiple_of(c * chunk, chunk)
            w1b[pl.ds(k0, chunk), :] = w1_ref[pl.ds(k0, chunk), :].astype(BF16)
            w2b[pl.ds(k0, chunk), :] = w2_ref[pl.ds(k0, chunk), :].astype(BF16)
            return carry
        lax.fori_loop(0, D_MODEL // chunk, cast_body, 0)

    wait_in(i, slot)

    @pl.when(size_ref[i] == 2)
    def _():
        yout[slot] = _expert_mlp(xin[slot], w1b, b1_ref, w2b, b2_ref, act)

    @pl.when(size_ref[i] == 1)
    def _():
        yout[slot, 0:tm] = _expert_mlp(xin[slot, 0:tm], w1b, b1_ref, w2b, b2_ref, act.at[pl.ds(0, tm)])

    start_out(i, slot)

    @pl.when(i == n - 1)
    def _():
        @pl.when(i >= 1)
        def _():
            wait_out(i - 1, 1 - slot)
        wait_out(i, slot)
        yout[1, 0:tm] = jnp.zeros((tm, PACK_W), U32)

        def zero_block(j):
            return pltpu.make_async_copy(yout.at[1, pl.ds(0, tm)], ys_hbm.at[pl.ds(pl.multiple_of(j * tm, tm), tm)],
                                         osem.at[1])

        def start_body(j, carry):
            zero_block(j).start()
            return carry

        def wait_body(j, carry):
            zero_block(j).wait()
            return carry
        lax.fori_loop(meta_ref[2 * N_EXPERTS], n_blocks, start_body, 0)
        lax.fori_loop(meta_ref[2 * N_EXPERTS], n_blocks, wait_body, 0)


def _moe(chunk_row, chunk_size, chunk_exp, meta, xs, w1, b1, w2, b2):
    by_expert = lambda i, row, size, exp, m: (exp[i], 0, 0)
    grid_spec = pltpu.PrefetchScalarGridSpec(
        num_scalar_prefetch=4,
        grid=(chunk_row.shape[0],),
        in_specs=[
            pl.BlockSpec(memory_space=pl.ANY),
            pl.BlockSpec((None, D_MODEL, 2 * D_FF), by_expert),
            pl.BlockSpec((None, 1, 2 * D_FF), by_expert),
            pl.BlockSpec((None, D_FF, D_MODEL), by_expert),
            pl.BlockSpec((None, 1, D_MODEL), by_expert),
        ],
        out_specs=pl.BlockSpec(memory_space=pl.ANY),
        scratch_shapes=[pltpu.VMEM((2, MOE_CH, PACK_W), U32), pltpu.VMEM((2, MOE_CH, PACK_W), U32),
                        pltpu.VMEM((D_MODEL, 2 * D_FF), BF16), pltpu.VMEM((D_FF, D_MODEL), BF16),
                        pltpu.VMEM((MOE_CH, D_FF), BF16),
                        pltpu.SemaphoreType.DMA((2,)), pltpu.SemaphoreType.DMA((2,))],
    )
    return pl.pallas_call(
        _moe_kernel,
        grid_spec=grid_spec,
        out_shape=jax.ShapeDtypeStruct(xs.shape, U32),
        compiler_params=pltpu.CompilerParams(dimension_semantics=("arbitrary",), vmem_limit_bytes=VMEM_LIMIT),
        name="moe",
    )(chunk_row, chunk_size, chunk_exp, meta, xs, w1, b1, w2, b2)


def _combine_kernel(n_prompt_tiles, cnt_ref, off_ref, tot_ref, ys_hbm, route_ref, x1_ref, modp_ref, g2s_ref, fg_ref,
                    op_ref, os_ref, buf, sem):
    i = pl.program_id(0)
    nb = pl.num_programs(0)
    slot = i % 2

    def fetch(tile, s):
        _for_strips(cnt_ref, off_ref, tile, buf.at[s], ys_hbm, sem.at[s], False, lambda cp: cp.start())

    @pl.when(i == 0)
    def _():
        buf[...] = jnp.zeros_like(buf)
        fetch(0, 0)

    @pl.when(i + 1 < nb)
    def _():
        fetch(i + 1, 1 - slot)

    _wait_tile_rows(tot_ref[i], buf.at[slot], ys_hbm, sem.at[slot], False)

    route = route_ref[...]
    c_pos = lax.broadcasted_iota(jnp.int32, (SEQ_TILE, SORT_ROWS), 1).astype(F32)
    gmat = jnp.zeros((SEQ_TILE, SORT_ROWS), F32)
    for k in range(TOP_K):
        gmat = gmat + jnp.where(c_pos == route[:, ROUTE_R + k:ROUTE_R + k + 1],
                                route[:, ROUTE_G + k:ROUTE_G + k + 1], 0.0)
    g_hi = gmat.astype(BF16)
    g_lo = (gmat - g_hi.astype(F32)).astype(BF16)
    ff = jnp.concatenate(
        [jnp.dot(g_hi, yb, preferred_element_type=F32) + jnp.dot(g_lo, yb, preferred_element_type=F32)
         for yb in _unpack_bf16_pairs(buf[slot])], axis=1)
    g2 = jnp.where(i < n_prompt_tiles, modp_ref[5:6, :], g2s_ref[...])
    x = x1_ref[...] + g2 * ff
    y = _rms(x, fg_ref[...])

    @pl.when(i < n_prompt_tiles)
    def _():
        op_ref[...] = y

    @pl.when(i >= n_prompt_tiles)
    def _():
        os_ref[...] = y


def _combine(cnt8, tile_off, tot8, ys, route, x1_all, mod_p, g2_rows, final_g, n_prompt_tiles, tiles_per_batch):
    nt = tot8.shape[0]
    tt = SEQ_TILE
    npt = n_prompt_tiles
    grid_spec = pltpu.PrefetchScalarGridSpec(
        num_scalar_prefetch=3,
        grid=(nt,),
        in_specs=[
            pl.BlockSpec(memory_space=pl.ANY),
            pl.BlockSpec((tt, LANES), lambda i, *_: (i, 0)),
            pl.BlockSpec((tt, D_MODEL), lambda i, *_: (i, 0)),
            pl.BlockSpec((None, 6, D_MODEL), lambda i, *_: (jnp.minimum(i, npt - 1) // tiles_per_batch, 0, 0)),
            pl.BlockSpec((tt, D_MODEL), lambda i, *_: (jnp.maximum(i - npt, 0), 0)),
            pl.BlockSpec((1, D_MODEL), lambda i, *_: (0, 0)),
        ],
        out_specs=(pl.BlockSpec((tt, D_MODEL), lambda i, *_: (jnp.minimum(i, npt - 1), 0)),
                   pl.BlockSpec((tt, D_MODEL), lambda i, *_: (jnp.maximum(i - npt, 0), 0))),
        scratch_shapes=[pltpu.VMEM((2, SORT_ROWS, PACK_W), U32), pltpu.SemaphoreType.DMA((2,))],
    )
    return pl.pallas_call(
        functools.partial(_combine_kernel, npt),
        grid_spec=grid_spec,
        out_shape=(jax.ShapeDtypeStruct((npt * tt, D_MODEL), F32),
                   jax.ShapeDtypeStruct(((nt - npt) * tt, D_MODEL), F32)),
        compiler_params=pltpu.CompilerParams(dimension_semantics=("arbitrary",), vmem_limit_bytes=VMEM_LIMIT),
        name="combine",
    )(cnt8, tile_off, tot8, ys, route, x1_all, mod_p, g2_rows, final_g)


def _block_diag_halves(wg):
    halves = []
    for hh in range(2):
        rows = []
        for bi in range(4):
            row = [wg[hh * 4 + bi] if bj == bi else jnp.zeros((LRU_BLOCK_W, LRU_BLOCK_W), wg.dtype) for bj in range(4)]
            rows.append(jnp.concatenate(row, axis=1))
        halves.append(jnp.concatenate(rows, axis=0))
    return jnp.stack(halves).astype(BF16)


def _rope_tables(pos):
    half = HEAD_DIM // 2
    inv = ROPE_THETA ** (-jnp.arange(half, dtype=F32) / half)
    ang = pos.astype(F32)[:, None] * inv[None, :]
    cos = jnp.cos(ang)
    sin = jnp.sin(ang)
    cos128 = jnp.concatenate([cos, cos, cos, cos], axis=1)
    sin128 = jnp.concatenate([-sin, sin, -sin, sin], axis=1)
    return cos128, sin128


def _full_spec(arr, grid_rank):
    zeros = (0,) * arr.ndim
    if grid_rank == 1:
        return pl.BlockSpec(arr.shape, lambda i: zeros)
    return pl.BlockSpec(arr.shape, lambda b, j: zeros)


def kernel(x_prompt, x_sample, state_lru_h, state_conv, cache_win_k, cache_win_v, c_prompt, c_sample, w_ada, b_ada, norm1_g, w_in, b_in, conv_w, conv_b, lru_wa, lru_ba, lru_wx, lru_bx, lru_lambda, attn_sinks, w_out, b_out, norm2_g, w_router, b_router, w1, b1, w2, b2, final_g):
    bp, seq, _ = x_prompt.shape
    bd, tdec, _ = x_sample.shape
    assert tdec == SUBLANES and seq % SEQ_TILE == 0 and bd % SAMPLE_BT == 0
    n_prompt = bp * seq
    n_sample = bd * tdec
    n_tok = n_prompt + n_sample
    l = 0

    head_perm = jnp.array([h for c in range(4) for h in (c, GROUP + c)], dtype=jnp.int32)
    qcol_perm = (head_perm[:, None] * HEAD_DIM + jnp.arange(HEAD_DIM, dtype=jnp.int32)[None, :]).reshape(-1)
    o2 = 2 * LRU_WIDTH
    in_perm = jnp.concatenate([jnp.arange(o2, dtype=jnp.int32), o2 + qcol_perm,
                               jnp.arange(o2 + Q_WIDTH, IN_WIDTH, dtype=jnp.int32)])
    out_perm = jnp.concatenate([jnp.arange(LRU_WIDTH, dtype=jnp.int32), LRU_WIDTH + qcol_perm])
    sinks_perm = attn_sinks[l]

    wr = jnp.tile(w_router[l], (1, TOP_K))
    wr_hi = wr.astype(BF16)
    weights = dict(
        norm1_g=norm1_g[l][None, :], w_in=w_in[l][:, in_perm].astype(BF16), b_in=b_in[l][in_perm][None, :],
        conv_w=conv_w[l], conv_b=conv_b[l][None, :],
        ga=_block_diag_halves(lru_wa[l]), gx=_block_diag_halves(lru_wx[l]),
        lru_ba=lru_ba[l][None, :], lru_bx=lru_bx[l][None, :], lam=lru_lambda[l][None, :],
        w_out=w_out[l][out_perm, :].astype(BF16), b_out=b_out[l][None, :], norm2_g=norm2_g[l][None, :],
        wr_hi=wr_hi, wr_lo=(wr - wr_hi.astype(F32)).astype(BF16),
        b_router=jnp.tile(b_router[l], TOP_K)[None, :],
    )
    wlist = [weights[n] for n in WEIGHT_NAMES]

    mod_all = _ada(jnp.concatenate([c_prompt, c_sample], axis=0), w_ada[l], b_ada[l][None, :])
    mod_p = mod_all[:bp].reshape(bp, 6, D_MODEL)
    mod_s = mod_all[bp:].reshape(bd, 6, D_MODEL).transpose(1, 0, 2)

    cos_p, sin_p = _rope_tables(jnp.arange(seq, dtype=jnp.int32))
    cos_s, sin_s = _rope_tables(PAST_LEN + jnp.arange(tdec, dtype=jnp.int32))
    cos_s = jnp.tile(cos_s, (SAMPLE_BT, 1))
    sin_s = jnp.tile(sin_s, (SAMPLE_BT, 1))
    h0_rows = jnp.pad(state_lru_h[l][:, None, :], ((0, 0), (0, tdec - 1), (0, 0))).reshape(n_sample, LRU_WIDTH)
    cprev_rows = jnp.pad(state_conv[l], ((0, 0), (0, tdec - (CONV_W - 1)), (0, 0))).reshape(n_sample, LRU_WIDTH)
    ck = cache_win_k[l].reshape(bd, WINDOW, KV_WIDTH)
    cv = cache_win_v[l].reshape(bd, WINDOW, KV_WIDTH)
    nj = seq // SEQ_TILE
    npt = n_prompt // SEQ_TILE
    nst = n_sample // SEQ_TILE
    p_tile = lambda i: jnp.minimum(i, npt - 1)
    s_tile = lambda i: jnp.maximum(i - npt, 0)
    all_rows = lambda width: pl.BlockSpec((SEQ_TILE, width), lambda i: (i, 0))
    p_rows = lambda width: pl.BlockSpec((SEQ_TILE, width), lambda i: (p_tile(i), 0))
    s_rows = lambda width: pl.BlockSpec((SEQ_TILE, width), lambda i: (s_tile(i), 0))
    p_tail = lambda rows, width: pl.BlockSpec((None, rows, width), lambda i: (p_tile(i) // nj, 0, 0))
    cache_spec = pl.BlockSpec((SAMPLE_BT, WINDOW, KV_WIDTH), lambda i: (s_tile(i), 0, 0))
    mixer_out_shapes = (
        jax.ShapeDtypeStruct((n_tok, D_MODEL), F32),
        jax.ShapeDtypeStruct((n_tok, D_MODEL), F32),
        jax.ShapeDtypeStruct((n_tok, LANES), F32),
        jax.ShapeDtypeStruct((npt + nst, SUBLANES, LANES), F32),
        jax.ShapeDtypeStruct((bp, SUBLANES, LRU_WIDTH), F32),
        jax.ShapeDtypeStruct((bp, SUBLANES, LRU_WIDTH), F32),
        jax.ShapeDtypeStruct((bp, WINDOW, KV_WIDTH), F32),
        jax.ShapeDtypeStruct((bp, WINDOW, KV_WIDTH), F32),
        jax.ShapeDtypeStruct((n_sample, D_MODEL), F32),
        jax.ShapeDtypeStruct((n_sample, LRU_WIDTH), F32),
        jax.ShapeDtypeStruct((n_sample, LRU_WIDTH), F32),
        jax.ShapeDtypeStruct((bd, WINDOW, KV_WIDTH), F32),
        jax.ShapeDtypeStruct((bd, WINDOW, KV_WIDTH), F32),
    )
    (x1_all, h2_all, route, tile_cnt, hlast_p, ulast_p, klast_p, vlast_p, g2_rows, hs_s, u_s, s_k, s_v) = pl.pallas_call(
        functools.partial(_mixer_kernel, npt, nj),
        grid=(npt + nst,),
        in_specs=[p_rows(D_MODEL),
                  pl.BlockSpec((None, 6, D_MODEL), lambda i: (p_tile(i) // nj, 0, 0)),
                  pl.BlockSpec((SEQ_TILE, LANES), lambda i: (p_tile(i) % nj, 0)),
                  pl.BlockSpec((SEQ_TILE, LANES), lambda i: (p_tile(i) % nj, 0)),
                  s_rows(D_MODEL),
                  pl.BlockSpec((6, SAMPLE_BT, D_MODEL), lambda i: (0, s_tile(i), 0)),
                  pl.BlockSpec((SEQ_TILE, LANES), lambda i: (0, 0)),
                  pl.BlockSpec((SEQ_TILE, LANES), lambda i: (0, 0)),
                  pl.BlockSpec(memory_space=pltpu.SMEM),
                  s_rows(LRU_WIDTH), s_rows(LRU_WIDTH), cache_spec, cache_spec]
                 + [_full_spec(a, 1) for a in wlist],
        out_specs=(all_rows(D_MODEL), all_rows(D_MODEL), all_rows(LANES),
                   pl.BlockSpec((None, SUBLANES, LANES), lambda i: (i, 0, 0)),
                   p_tail(SUBLANES, LRU_WIDTH), p_tail(SUBLANES, LRU_WIDTH),
                   p_tail(WINDOW, KV_WIDTH), p_tail(WINDOW, KV_WIDTH),
                   s_rows(D_MODEL), s_rows(LRU_WIDTH), s_rows(LRU_WIDTH), cache_spec, cache_spec),
        out_shape=mixer_out_shapes,
        scratch_shapes=[pltpu.VMEM((SUBLANES, LRU_WIDTH), F32), pltpu.VMEM((SUBLANES, LRU_WIDTH), F32),
                        pltpu.VMEM((WINDOW, KV_WIDTH), F32), pltpu.VMEM((WINDOW, KV_WIDTH), F32)],
        compiler_params=pltpu.CompilerParams(dimension_semantics=("arbitrary",), vmem_limit_bytes=VMEM_LIMIT),
        name="mixer",
    )(x_prompt.reshape(n_prompt, D_MODEL), mod_p, cos_p, sin_p,
      x_sample.reshape(n_sample, D_MODEL), mod_s, cos_s, sin_s, sinks_perm, h0_rows, cprev_rows, ck, cv, *wlist)

    n_tiles = npt + nst
    n_assign = n_tok * TOP_K
    max_rows = n_assign + n_tiles * N_EXPERTS * (SUBLANES - 1) + N_EXPERTS * (MOE_TM - 1)
    n_blocks = -(-max_rows // MOE_TM)
    cnt = tile_cnt[:, 0, :N_EXPERTS].astype(jnp.int32)
    cnt8 = (cnt + SUBLANES - 1) // SUBLANES * SUBLANES
    counts = jnp.sum(cnt8, axis=0)
    pcounts = (counts + MOE_TM - 1) // MOE_TM * MOE_TM
    pend = jnp.cumsum(pcounts)
    pstart = pend - pcounts
    tile_off = pstart[None, :] + jnp.cumsum(cnt8, axis=0) - cnt8
    tot8 = jnp.sum(cnt8, axis=1)
    meta = jnp.concatenate([pstart + counts, pend, pend[-1:] // MOE_TM]).astype(jnp.int32)
    cnt8_flat = cnt8.reshape(-1)
    off_flat = tile_off.reshape(-1).astype(jnp.int32)

    xs = _dispatch(cnt8_flat, off_flat, tot8, meta, h2_all, route, n_blocks * MOE_TM)

    n_tm = pcounts // MOE_TM
    n_ch = (n_tm + 1) // 2
    ch_end = jnp.cumsum(n_ch)
    ch_start = ch_end - n_ch
    n_chunks = (n_blocks + N_EXPERTS + 1) // 2
    ci = jnp.arange(n_chunks, dtype=jnp.int32)
    owner = jnp.logical_and(ch_start[None, :] <= ci[:, None], ci[:, None] < ch_end[None, :])
    pick = lambda v: jnp.sum(jnp.where(owner, v[None, :], 0), axis=1)
    live = ci < ch_end[-1]
    local = ci - pick(ch_start)
    chunk_exp = jnp.where(live, pick(jnp.arange(N_EXPERTS, dtype=jnp.int32)), N_EXPERTS - 1).astype(jnp.int32)
    chunk_row = jnp.where(live, pick(pstart) + local * MOE_CH, 0).astype(jnp.int32)
    chunk_size = jnp.where(live, jnp.where(2 * local + 2 <= pick(n_tm), 2, 1), 0).astype(jnp.int32)
    ys = _moe(chunk_row, chunk_size, chunk_exp, meta, xs, w1[l], b1[l][:, None, :], w2[l], b2[l][:, None, :])

    y_p, y_s = _combine(cnt8_flat, off_flat, tot8, ys, route, x1_all, mod_p, g2_rows,
                        final_g[None, :], n_prompt // SEQ_TILE, seq // SEQ_TILE)

    y_prompt = y_p.reshape(bp, seq, D_MODEL)
    y_sample = y_s.reshape(bd, tdec, D_MODEL)
    p_h = hlast_p[:, SUBLANES - 1, :][None]
    p_c = ulast_p[:, SUBLANES - (CONV_W - 1):, :][None]
    p_k = klast_p.reshape(1, bp, WINDOW, N_KV_HEADS, HEAD_DIM)
    p_v = vlast_p.reshape(1, bp, WINDOW, N_KV_HEADS, HEAD_DIM)
    s_h = hs_s.reshape(bd, tdec, LRU_WIDTH)[:, tdec - 1, :][None]
    s_c = u_s.reshape(bd, tdec, LRU_WIDTH)[:, tdec - (CONV_W - 1):, :][None]
    s_kk = s_k.reshape(1, bd, WINDOW, N_KV_HEADS, HEAD_DIM)
    s_vv = s_v.reshape(1, bd, WINDOW, N_KV_HEADS, HEAD_DIM)
    return (y_prompt, y_sample, p_h, p_c, p_k, p_v, s_h, s_c, s_kk, s_vv)
```

```python
import functools

import jax
import jax.numpy as jnp
from jax import lax
from jax.experimental import pallas as pl
from jax.experimental.pallas import tpu as pltpu

F32 = jnp.float32
BF16 = jnp.bfloat16

D_MODEL = 1024
LRU_WIDTH = 512
LRU_BLOCKS = 8
LRU_BLOCK_W = LRU_WIDTH // LRU_BLOCKS
CONV_W = 4
LRU_C = 8.0
HEAD_DIM = 64
N_HEADS = 8
N_KV_HEADS = 2
GROUP = N_HEADS // N_KV_HEADS
WINDOW = 128
ROPE_THETA = 10000.0
N_EXPERTS = 32
TOP_K = 4
D_FF = D_MODEL
SWIGLU_LIMIT = 7.0
SWIGLU_ALPHA = 1.702
NORM_EPS = 1e-5
PAST_LEN = 8192
Q_WIDTH = N_HEADS * HEAD_DIM
KV_WIDTH = N_KV_HEADS * HEAD_DIM
IN_WIDTH = 2 * LRU_WIDTH + Q_WIDTH + 2 * KV_WIDTH

LANES = 128
SUBLANES = 8
SEQ_TILE = 256
SAMPLE_BT = 32
MOE_TM = 256
MOE_CH = 2 * MOE_TM
COMB_TT = 256
NEG_BIG = -1e30
VMEM_LIMIT = 56 * 1024 * 1024


def _rms(x, g):
    return x * lax.rsqrt(jnp.mean(x * x, axis=-1, keepdims=True) + NORM_EPS) * g


def _rms_mod(x, gain, shift):
    return x * lax.rsqrt(jnp.mean(x * x, axis=-1, keepdims=True) + NORM_EPS) * gain + shift


def _group_scan(a, b):
    rows, width = a.shape
    groups = rows // SUBLANES
    a3 = a.reshape(groups, SUBLANES, width)
    b3 = b.reshape(groups, SUBLANES, width)
    t = lax.broadcasted_iota(jnp.int32, (1, SUBLANES, 1), 1)
    d = 1
    while d < SUBLANES:
        keep = t >= d
        a_s = jnp.where(keep, pltpu.roll(a3, d, axis=1), 1.0)
        b_s = jnp.where(keep, pltpu.roll(b3, d, axis=1), 0.0)
        b3 = a3 * b_s + b3
        a3 = a3 * a_s
        d *= 2
    return a3.reshape(rows, width), b3.reshape(rows, width)


def _chain_groups(a_grp, b_grp, h_in):
    rows = a_grp.shape[0]
    out = []
    carry = h_in
    for g in range(rows // SUBLANES):
        sl = slice(g * SUBLANES, (g + 1) * SUBLANES)
        hg = b_grp[sl] + a_grp[sl] * carry
        out.append(hg)
        carry = hg[SUBLANES - 1:SUBLANES]
    return jnp.concatenate(out, axis=0)


def _rope128(x, cos, sin_signed, first_half):
    sw = jnp.where(first_half, pltpu.roll(x, LANES - HEAD_DIM // 2, axis=1), pltpu.roll(x, HEAD_DIM // 2, axis=1))
    return x * cos + sw * sin_signed


def _softplus(x):
    return jnp.maximum(x, 0.0) + jnp.log1p(jnp.exp(-jnp.abs(x)))


def _lru_coeffs(uc, w, first_pos_mask):
    ub = uc.astype(BF16)
    half = LRU_WIDTH // 2
    ra = jnp.concatenate([jnp.dot(ub[:, :half], w['ga'][0], preferred_element_type=F32),
                          jnp.dot(ub[:, half:], w['ga'][1], preferred_element_type=F32)], axis=1)
    rx = jnp.concatenate([jnp.dot(ub[:, :half], w['gx'][0], preferred_element_type=F32),
                          jnp.dot(ub[:, half:], w['gx'][1], preferred_element_type=F32)], axis=1)
    r = jax.nn.sigmoid(ra + w['lru_ba'][...])
    i = jax.nn.sigmoid(rx + w['lru_bx'][...])
    log_a = -LRU_C * r * _softplus(-w['lam'][...])
    a = jnp.exp(log_a)
    om = 1.0 - a * a
    mult = jnp.where(om > 0.0, om * lax.rsqrt(om), 0.0)
    if first_pos_mask is not None:
        mult = jnp.where(first_pos_mask, 1.0, mult)
    return a, mult * i * uc


def _conv_taps(u, s1, s2, s3, w):
    cw = w['conv_w']
    return w['conv_b'][...] + s3 * cw[0:1, :] + s2 * cw[1:2, :] + s1 * cw[2:3, :] + u * cw[3:4, :]


def _in_proj(x, mod, w):
    sh1, sc1 = mod
    h = _rms_mod(x, w['norm1_g'][...] * (1.0 + sc1), sh1)
    return jnp.dot(h.astype(BF16), w['w_in'][...], preferred_element_type=F32) + w['b_in'][...]


def _post_mix(x, mix, mod, w):
    g1, sh2, sc2 = mod
    x1 = x + g1 * (jnp.dot(mix.astype(BF16), w['w_out'][...], preferred_element_type=F32) + w['b_out'][...])
    h2 = _rms_mod(x1, w['norm2_g'][...] * (1.0 + sc2), sh2)
    h2_hi = h2.astype(BF16)
    h2_lo = (h2 - h2_hi.astype(F32)).astype(BF16)
    logits = (jnp.dot(h2_hi, w['wr_hi'][...], preferred_element_type=F32)
              + jnp.dot(h2_lo, w['wr_hi'][...], preferred_element_type=F32)
              + jnp.dot(h2_hi, w['wr_lo'][...], preferred_element_type=F32)) + w['b_router'][...]
    return x1, h2, logits


ROUTE_E, ROUTE_G, ROUTE_R = 0, TOP_K, 2 * TOP_K


def _lane_roll1(v, shift):
    return pltpu.roll(jnp.broadcast_to(v, (SUBLANES, LANES)), shift, axis=1)[0:1]


def _route_tile(lg):
    rows = lg.shape[0]
    lane = lax.broadcasted_iota(jnp.int32, (1, LANES), 1)
    e_of = lane % N_EXPERTS
    grp = lane // N_EXPERTS
    onehot = jnp.zeros((rows, LANES), F32)
    vals, ids = [], []
    for k in range(TOP_K):
        m = jnp.max(lg, axis=1, keepdims=True)
        idx = jnp.min(jnp.where(lg == m, e_of, N_EXPERTS), axis=1, keepdims=True)
        sel = e_of == idx
        lg = jnp.where(sel, -jnp.inf, lg)
        onehot = jnp.where(jnp.logical_and(sel, grp == k), 1.0, onehot)
        vals.append(m)
        ids.append(idx)
    ex = [jnp.exp(v - vals[0]) for v in vals]
    denom = ex[0] + ex[1] + ex[2] + ex[3]

    r_i = lax.broadcasted_iota(jnp.int32, (rows, rows), 0)
    c_i = lax.broadcasted_iota(jnp.int32, (rows, rows), 1)
    strict_lower = jnp.where(r_i > c_i, 1.0, 0.0).astype(BF16)
    prefix = jnp.dot(strict_lower, onehot.astype(BF16), preferred_element_type=F32)
    cnt = jnp.sum(onehot, axis=0, keepdims=True)
    base = jnp.zeros((1, LANES), F32)
    tot = cnt
    for s in range(1, TOP_K):
        rolled = _lane_roll1(cnt, s * N_EXPERTS)
        base = base + jnp.where(lane >= s * N_EXPERTS, rolled, 0.0)
        tot = tot + rolled
    pad_cnt = jnp.floor((tot + (SUBLANES - 1.0)) * (1.0 / SUBLANES)) * SUBLANES
    inc = pad_cnt
    d = 1
    while d < N_EXPERTS:
        inc = inc + jnp.where(e_of >= d, _lane_roll1(inc, d), 0.0)
        d *= 2
    strip_start = inc - pad_cnt
    ranked = onehot * (prefix + base + strip_start)

    route = jnp.zeros((rows, LANES), F32)
    for k in range(TOP_K):
        rank_k = jnp.sum(jnp.where(grp == k, ranked, 0.0), axis=1, keepdims=True)
        route = jnp.where(lane == ROUTE_E + k, ids[k].astype(F32), route)
        route = jnp.where(lane == ROUTE_G + k, ex[k] / denom, route)
        route = jnp.where(lane == ROUTE_R + k, rank_k, route)
    return route, tot


def _softmax_sink_pv(s, sink_col, v_bf16):
    m = jnp.maximum(jnp.max(s, axis=-1, keepdims=True), sink_col)
    p = jnp.exp(s - m)
    denom = jnp.sum(p, axis=-1, keepdims=True) + jnp.exp(sink_col - m)
    return p, denom


WEIGHT_NAMES = ('norm1_g', 'w_in', 'b_in', 'conv_w', 'conv_b', 'ga', 'gx', 'lru_ba', 'lru_bx', 'lam',
                'w_out', 'b_out', 'norm2_g', 'wr_hi', 'wr_lo', 'b_router')


def _ada_kernel(c_ref, w_ref, b_ref, o_ref):
    c = c_ref[...]
    s = c * jax.nn.sigmoid(c)
    o_ref[...] = jnp.dot(s, w_ref[...], preferred_element_type=F32, precision=lax.Precision.HIGHEST) + b_ref[...]


def _ada(c_all, w_ada, b_ada):
    rows = c_all.shape[0]
    return pl.pallas_call(
        _ada_kernel,
        grid=(6,),
        in_specs=[pl.BlockSpec((rows, D_MODEL), lambda i: (0, 0)),
                  pl.BlockSpec((D_MODEL, D_MODEL), lambda i: (0, i)),
                  pl.BlockSpec((1, D_MODEL), lambda i: (0, i))],
        out_specs=pl.BlockSpec((rows, D_MODEL), lambda i: (0, i)),
        out_shape=jax.ShapeDtypeStruct((rows, 6 * D_MODEL), F32),
        compiler_params=pltpu.CompilerParams(dimension_semantics=("arbitrary",), vmem_limit_bytes=VMEM_LIMIT),
        name="ada",
    )(c_all, w_ada, b_ada)


def _prompt_body(j, x_ref, mod_ref, cos_ref, sin_ref, sinks_ref, w, x1_ref, h2_ref, lg_ref,
                 hlast_ref, ulast_ref, klast_ref, vlast_ref, conv_c, h_c, k_c, v_c):
    ts = SEQ_TILE

    @pl.when(j == 0)
    def _():
        conv_c[...] = jnp.zeros_like(conv_c)
        h_c[...] = jnp.zeros_like(h_c)
        k_c[...] = jnp.zeros_like(k_c)
        v_c[...] = jnp.zeros_like(v_c)

    x = x_ref[...]
    mod = mod_ref[...]
    proj = _in_proj(x, (mod[0:1], mod[1:2]), w)
    u = proj[:, :LRU_WIDTH]
    gate = proj[:, LRU_WIDTH:2 * LRU_WIDTH]
    o2 = 2 * LRU_WIDTH

    rowid = lax.broadcasted_iota(jnp.int32, (ts, 1), 0)
    u_ext = jnp.concatenate([conv_c[...], u], axis=0)
    s1, s2, s3 = (pltpu.roll(u_ext, d, axis=0)[SUBLANES:] for d in (1, 2, 3))
    uc = _conv_taps(u, s1, s2, s3, w)
    conv_c[...] = u[ts - SUBLANES:]
    ulast_ref[...] = u[ts - SUBLANES:]

    first_pos = jnp.logical_and(rowid == 0, j == 0)
    a, bt = _lru_coeffs(uc, w, first_pos)
    hs = _chain_groups(*_group_scan(a, bt), h_c[0:1, :])
    h_tail = hs[ts - SUBLANES:]
    h_c[...] = jnp.broadcast_to(h_tail[SUBLANES - 1:SUBLANES, :], h_c.shape)
    hlast_ref[...] = h_tail
    lru_out = hs * jax.nn.gelu(gate)

    cos = cos_ref[...]
    sin = sin_ref[...]
    lane = lax.broadcasted_iota(jnp.int32, (1, LANES), 1)
    first_half = (lane % HEAD_DIM) < (HEAD_DIM // 2)
    qcols = [_rope128(proj[:, o2 + c * LANES:o2 + (c + 1) * LANES], cos, sin, first_half) * (HEAD_DIM ** -0.5)
             for c in range(4)]
    k = _rope128(proj[:, o2 + Q_WIDTH:o2 + Q_WIDTH + KV_WIDTH], cos, sin, first_half)
    v = proj[:, o2 + Q_WIDTH + KV_WIDTH:]
    k_ext = jnp.concatenate([k_c[...], k], axis=0).astype(BF16)
    v_ext = jnp.concatenate([v_c[...], v], axis=0).astype(BF16)
    k_c[...] = k[ts - WINDOW:]
    v_c[...] = v[ts - WINDOW:]
    klast_ref[...] = k[ts - WINDOW:]
    vlast_ref[...] = v[ts - WINDOW:]

    qi = lax.broadcasted_iota(jnp.int32, (WINDOW, 2 * WINDOW), 0)
    kj = lax.broadcasted_iota(jnp.int32, (WINDOW, 2 * WINDOW), 1)
    band = jnp.logical_and(kj > qi, kj <= qi + WINDOW)
    lane_lo = lane < HEAD_DIM
    grow = lax.broadcasted_iota(jnp.int32, (GROUP * WINDOW, 1), 0) // WINDOW
    attn_cols = [[] for _ in range(4)]
    for blk in range(ts // WINDOW):
        prev_ok = jnp.logical_or(j > 0, blk > 0)
        mask = jnp.logical_and(band, jnp.logical_or(kj >= WINDOW, prev_ok))
        mask4 = jnp.concatenate([mask] * GROUP, axis=0)
        kb = k_ext[blk * WINDOW:(blk + 2) * WINDOW]
        vb = v_ext[blk * WINDOW:(blk + 2) * WINDOW]
        outs = []
        for kv in range(N_KV_HEADS):
            sel = lane_lo if kv == 0 else jnp.logical_not(lane_lo)
            qs = jnp.concatenate(
                [jnp.where(sel, qc[blk * WINDOW:(blk + 1) * WINDOW], 0.0) for qc in qcols], axis=0).astype(BF16)
            s = lax.dot_general(qs, kb, (((1,), (1,)), ((), ())), preferred_element_type=F32)
            s = jnp.where(mask4, s, NEG_BIG)
            sink_col = jnp.zeros((GROUP * WINDOW, 1), F32)
            for g in range(GROUP):
                sink_col = jnp.where(grow == g, sinks_ref[kv * GROUP + g], sink_col)
            p, denom = _softmax_sink_pv(s, sink_col, vb)
            outs.append(jnp.dot(p.astype(BF16), vb, preferred_element_type=F32) / denom)
        for c in range(4):
            attn_cols[c].append(jnp.where(lane_lo, outs[0][c * WINDOW:(c + 1) * WINDOW],
                                          outs[1][c * WINDOW:(c + 1) * WINDOW]))
    attn = jnp.concatenate([jnp.concatenate(cols, axis=0) for cols in attn_cols], axis=1)

    mix = jnp.concatenate([lru_out, attn], axis=1)
    x1, h2, logits = _post_mix(x, mix, (mod[2:3], mod[3:4], mod[4:5]), w)
    x1_ref[...] = x1
    h2_ref[...] = h2
    route, tot = _route_tile(logits)
    lg_ref[0][...] = route
    lg_ref[1][...] = jnp.broadcast_to(tot, lg_ref[1].shape)


def _expand_rows(m, t):
    b, wd = m.shape
    return jnp.broadcast_to(m[:, None, :], (b, t, wd)).reshape(b * t, wd)


def _sample_body(x_ref, mod_ref, cos_ref, sin_ref, sinks_ref, h0_ref, cprev_ref, ck_ref, cv_ref, w,
                 x1_ref, h2_ref, lg_ref, g2_ref, hs_ref, u_ref, ko_ref, vo_ref):
    bt_, t = SAMPLE_BT, SUBLANES
    rows = bt_ * t

    x = x_ref[...]
    mods = [_expand_rows(mod_ref[i], t) for i in range(6)]
    proj = _in_proj(x, (mods[0], mods[1]), w)
    u = proj[:, :LRU_WIDTH]
    gate = proj[:, LRU_WIDTH:2 * LRU_WIDTH]
    o2 = 2 * LRU_WIDTH
    u_ref[...] = u

    rowid = lax.broadcasted_iota(jnp.int32, (rows, 1), 0) % t
    cprev = cprev_ref[...]
    taps = []
    for d in (1, 2, 3):
        taps.append(jnp.where(rowid >= d, pltpu.roll(u, d, axis=0),
                              pltpu.roll(cprev, (d - (CONV_W - 1)) % rows, axis=0)))
    uc = _conv_taps(u, taps[0], taps[1], taps[2], w)

    a, bt = _lru_coeffs(uc, w, None)
    bt = bt + a * h0_ref[...]
    _, hs = _group_scan(a, bt)
    hs_ref[...] = hs
    lru_out = hs * jax.nn.gelu(gate)

    cos = cos_ref[...]
    sin = sin_ref[...]
    lane = lax.broadcasted_iota(jnp.int32, (1, LANES), 1)
    first_half = (lane % HEAD_DIM) < (HEAD_DIM // 2)
    qcols = [_rope128(proj[:, o2 + c * LANES:o2 + (c + 1) * LANES], cos, sin, first_half) * (HEAD_DIM ** -0.5)
             for c in range(4)]
    k = _rope128(proj[:, o2 + Q_WIDTH:o2 + Q_WIDTH + KV_WIDTH], cos, sin, first_half)
    v = proj[:, o2 + Q_WIDTH + KV_WIDTH:]
    k3 = k.reshape(bt_, t, KV_WIDTH)
    v3 = v.reshape(bt_, t, KV_WIDTH)
    ck = ck_ref[...]
    cv = cv_ref[...]
    ko_ref[:, :WINDOW - t, :] = ck[:, t:, :]
    ko_ref[:, WINDOW - t:, :] = k3
    vo_ref[:, :WINDOW - t, :] = cv[:, t:, :]
    vo_ref[:, WINDOW - t:, :] = v3

    ckb, cvb, k3b, v3b = ck.astype(BF16), cv.astype(BF16), k3.astype(BF16), v3.astype(BF16)
    lane_lo = lane < HEAD_DIM
    gq = GROUP * t
    tq = lax.broadcasted_iota(jnp.int32, (1, gq, 1), 1) % t
    mask_c = lax.broadcasted_iota(jnp.int32, (1, gq, WINDOW), 2) > tq
    mask_n = lax.broadcasted_iota(jnp.int32, (1, gq, t), 2) <= tq
    grow = lax.broadcasted_iota(jnp.int32, (1, gq, 1), 1) // t
    bdims = (((2,), (2,)), ((0,), (0,)))
    pdims = (((2,), (1,)), ((0,), (0,)))
    outs = []
    for kv in range(N_KV_HEADS):
        sel = lane_lo if kv == 0 else jnp.logical_not(lane_lo)
        q3 = jnp.concatenate([jnp.where(sel, qc, 0.0).reshape(bt_, t, LANES) for qc in qcols], axis=1).astype(BF16)
        sc = lax.dot_general(q3, ckb, bdims, preferred_element_type=F32)
        sn = lax.dot_general(q3, k3b, bdims, preferred_element_type=F32)
        sc = jnp.where(mask_c, sc, NEG_BIG)
        sn = jnp.where(mask_n, sn, NEG_BIG)
        sink_col = jnp.zeros((1, gq, 1), F32)
        for g in range(GROUP):
            sink_col = jnp.where(grow == g, sinks_ref[kv * GROUP + g], sink_col)
        m = jnp.maximum(jnp.maximum(jnp.max(sc, axis=-1, keepdims=True), jnp.max(sn, axis=-1, keepdims=True)),
                        sink_col)
        pc = jnp.exp(sc - m)
        pn = jnp.exp(sn - m)
        denom = jnp.sum(pc, axis=-1, keepdims=True) + jnp.sum(pn, axis=-1, keepdims=True) + jnp.exp(sink_col - m)
        o = (lax.dot_general(pc.astype(BF16), cvb, pdims, preferred_element_type=F32)
             + lax.dot_general(pn.astype(BF16), v3b, pdims, preferred_element_type=F32)) / denom
        outs.append(o)
    attn = jnp.concatenate(
        [jnp.where(lane_lo, outs[0][:, c * t:(c + 1) * t, :], outs[1][:, c * t:(c + 1) * t, :]).reshape(rows, LANES)
         for c in range(4)], axis=1)

    mix = jnp.concatenate([lru_out, attn], axis=1)
    x1, h2, logits = _post_mix(x, mix, (mods[2], mods[3], mods[4]), w)
    x1_ref[...] = x1
    h2_ref[...] = h2
    route, tot = _route_tile(logits)
    lg_ref[0][...] = route
    lg_ref[1][...] = jnp.broadcast_to(tot, lg_ref[1].shape)
    g2_ref[...] = mods[5]


def _mixer_kernel(n_prompt_tiles, tiles_per_seq,
                  xp_ref, modp_ref, cosp_ref, sinp_ref, xs_ref, mods_ref, coss_ref, sins_ref, sinks_ref,
                  h0_ref, cprev_ref, ck_ref, cv_ref, *rest):
    nw = len(WEIGHT_NAMES)
    w = dict(zip(WEIGHT_NAMES, rest[:nw]))
    (x1_ref, h2_ref, route_ref, cnt_ref, hlast_ref, ulast_ref, klast_ref, vlast_ref,
     g2_ref, hs_ref, u_ref, ko_ref, vo_ref, conv_c, h_c, k_c, v_c) = rest[nw:]
    lg_ref = (route_ref, cnt_ref)
    i = pl.program_id(0)

    @pl.when(i < n_prompt_tiles)
    def _():
        _prompt_body(i % tiles_per_seq, xp_ref, modp_ref, cosp_ref, sinp_ref, sinks_ref, w, x1_ref, h2_ref, lg_ref,
                     hlast_ref, ulast_ref, klast_ref, vlast_ref, conv_c, h_c, k_c, v_c)

    @pl.when(i >= n_prompt_tiles)
    def _():
        _sample_body(xs_ref, mods_ref, coss_ref, sins_ref, sinks_ref, h0_ref, cprev_ref, ck_ref, cv_ref, w,
                     x1_ref, h2_ref, lg_ref, g2_ref, hs_ref, u_ref, ko_ref, vo_ref)


PACK_W = D_MODEL // 2
U32 = jnp.uint32


def _pack_bf16_pairs(x):
    xb = x.astype(BF16).astype(F32)
    lo = lax.bitcast_convert_type(xb[:, :PACK_W], U32)
    hi = lax.bitcast_convert_type(xb[:, PACK_W:], U32)
    return jnp.bitwise_or(jnp.bitwise_and(hi, jnp.uint32(0xFFFF0000)), lax.shift_right_logical(lo, jnp.uint32(16)))


def _unpack_bf16_pairs(w):
    lo = lax.bitcast_convert_type(lax.shift_left(w, jnp.uint32(16)), F32).astype(BF16)
    hi = lax.bitcast_convert_type(jnp.bitwise_and(w, jnp.uint32(0xFFFF0000)), F32).astype(BF16)
    return lo, hi


STRIP_SIZES = tuple(SUBLANES << b for b in range(6))
SORT_ROWS = SEQ_TILE * TOP_K + N_EXPERTS * SUBLANES
TILE_WAIT_SIZES = tuple(SUBLANES << b for b in range(8))


def _for_strips(cnt_ref, off_ref, tile, buf_slot, hbm, sem, to_hbm, act):
    def e_body(e, local):
        n = cnt_ref[tile * N_EXPERTS + e]
        glob = off_ref[tile * N_EXPERTS + e]
        done = 0
        for p in reversed(STRIP_SIZES):
            piece = n & p
            lo = pl.ds(pl.multiple_of(local + done, SUBLANES), p)
            gl = pl.ds(pl.multiple_of(glob + done, SUBLANES), p)

            @pl.when(piece != 0)
            def _():
                if to_hbm:
                    act(pltpu.make_async_copy(buf_slot.at[lo], hbm.at[gl], sem))
                else:
                    act(pltpu.make_async_copy(hbm.at[gl], buf_slot.at[lo], sem))
            done = done + piece
        return local + n
    lax.fori_loop(0, N_EXPERTS, e_body, 0)


def _wait_tile_rows(total, buf_slot, hbm, sem, to_hbm):
    for p in TILE_WAIT_SIZES:
        @pl.when((total & p) != 0)
        def _():
            if to_hbm:
                pltpu.make_async_copy(buf_slot.at[pl.ds(0, p)], hbm.at[pl.ds(0, p)], sem).wait()
            else:
                pltpu.make_async_copy(hbm.at[pl.ds(0, p)], buf_slot.at[pl.ds(0, p)], sem).wait()


def _dispatch_kernel(cnt_ref, off_ref, tot_ref, meta_ref, h2_ref, route_ref, xs_hbm, sbuf, zblk, sem, zsem):
    i = pl.program_id(0)
    nb = pl.num_programs(0)
    slot = i % 2
    n_blocks = xs_hbm.shape[0] // MOE_TM

    route_t = route_ref[...].T
    r_pos = lax.broadcasted_iota(jnp.int32, (SORT_ROWS, SEQ_TILE), 0).astype(F32)
    perm = jnp.zeros((SORT_ROWS, SEQ_TILE), F32)
    for k in range(TOP_K):
        perm = perm + jnp.where(r_pos == route_t[ROUTE_R + k:ROUTE_R + k + 1, :], 1.0, 0.0)
    sbuf[slot] = _pack_bf16_pairs(jnp.dot(perm.astype(BF16), h2_ref[...].astype(BF16), preferred_element_type=F32))

    _for_strips(cnt_ref, off_ref, i, sbuf.at[slot], xs_hbm, sem.at[slot], True, lambda cp: cp.start())

    @pl.when(i > 0)
    def _():
        _wait_tile_rows(tot_ref[jnp.maximum(i - 1, 0)], sbuf.at[1 - slot], xs_hbm, sem.at[1 - slot], True)

    @pl.when(i == nb - 1)
    def _():
        _wait_tile_rows(tot_ref[i], sbuf.at[slot], xs_hbm, sem.at[slot], True)
        zblk[...] = jnp.zeros_like(zblk)

        def for_region_tails(act):
            def e_body(e, carry):
                start = meta_ref[e]
                n = meta_ref[N_EXPERTS + e] - start
                done = 0
                for p in reversed(STRIP_SIZES[:-1]):
                    piece = n & p
                    rows = pl.ds(pl.multiple_of(start + done, SUBLANES), p)

                    @pl.when(piece != 0)
                    def _():
                        act(pltpu.make_async_copy(zblk.at[pl.ds(0, p)], xs_hbm.at[rows], zsem.at[0]))
                    done = done + piece
                return carry
            lax.fori_loop(0, N_EXPERTS, e_body, 0)

        def for_tail_blocks(act):
            def b_body(j, carry):
                act(pltpu.make_async_copy(zblk, xs_hbm.at[pl.ds(pl.multiple_of(j * MOE_TM, MOE_TM), MOE_TM)],
                                          zsem.at[0]))
                return carry
            lax.fori_loop(meta_ref[2 * N_EXPERTS], n_blocks, b_body, 0)

        for_region_tails(lambda cp: cp.start())
        for_tail_blocks(lambda cp: cp.start())
        for_region_tails(lambda cp: cp.wait())
        for_tail_blocks(lambda cp: cp.wait())


def _dispatch(cnt8, tile_off, tot8, meta, h2_all, route, n_rows):
    nt = tot8.shape[0]
    tt = SEQ_TILE
    grid_spec = pltpu.PrefetchScalarGridSpec(
        num_scalar_prefetch=4,
        grid=(nt,),
        in_specs=[pl.BlockSpec((tt, D_MODEL), lambda i, *_: (i, 0)),
                  pl.BlockSpec((tt, LANES), lambda i, *_: (i, 0))],
        out_specs=pl.BlockSpec(memory_space=pl.ANY),
        scratch_shapes=[pltpu.VMEM((2, SORT_ROWS, PACK_W), U32), pltpu.VMEM((MOE_TM, PACK_W), U32),
                        pltpu.SemaphoreType.DMA((2,)), pltpu.SemaphoreType.DMA((1,))],
    )
    return pl.pallas_call(
        _dispatch_kernel,
        grid_spec=grid_spec,
        out_shape=jax.ShapeDtypeStruct((n_rows, PACK_W), U32),
        compiler_params=pltpu.CompilerParams(dimension_semantics=("arbitrary",), vmem_limit_bytes=VMEM_LIMIT),
        name="dispatch",
    )(cnt8, tile_off, tot8, meta, h2_all, route)


def _expert_mlp(words, w1b, b1_ref, w2b, b2_ref, act_ref):
    xb = jnp.concatenate(_unpack_bf16_pairs(words), axis=1)
    q = D_FF // 4
    for c in range(4):
        zg = jnp.dot(xb, w1b[:, c * q:(c + 1) * q], preferred_element_type=F32) + b1_ref[:, c * q:(c + 1) * q]
        zl = (jnp.dot(xb, w1b[:, D_FF + c * q:D_FF + (c + 1) * q], preferred_element_type=F32)
              + b1_ref[:, D_FF + c * q:D_FF + (c + 1) * q])
        glu = jnp.minimum(zg, SWIGLU_LIMIT)
        lin = jnp.clip(zl, -SWIGLU_LIMIT, SWIGLU_LIMIT)
        act_ref[:, c * q:(c + 1) * q] = (glu * jax.nn.sigmoid(SWIGLU_ALPHA * glu) * (lin + 1.0)).astype(BF16)
    return _pack_bf16_pairs(jnp.dot(act_ref[...], w2b[...], preferred_element_type=F32) + b2_ref[...])


def _moe_kernel(row_ref, size_ref, exp_ref, wt_ref, meta_ref, xs_hbm, w1_hbm, b1_ref, w2_hbm, b2_ref, ys_hbm,
                xin, yout, w1f, w2f, w1b, w2b, act, isem, osem, wsem):
    i = pl.program_id(0)
    n = pl.num_programs(0)
    slot = i % 2
    tm = MOE_TM
    n_blocks = ys_hbm.shape[0] // tm

    def for_chunk(j, full_fn, half_fn):
        r = pl.multiple_of(row_ref[j], tm)

        @pl.when(size_ref[j] == 2)
        def _():
            full_fn(r)

        @pl.when(size_ref[j] == 1)
        def _():
            half_fn(r)

    def in_full(r, s):
        return pltpu.make_async_copy(xs_hbm.at[pl.ds(r, MOE_CH)], xin.at[s], isem.at[s])

    def in_half(r, s):
        return pltpu.make_async_copy(xs_hbm.at[pl.ds(r, tm)], xin.at[s, pl.ds(0, tm)], isem.at[s])

    def out_full(r, s):
        return pltpu.make_async_copy(yout.at[s], ys_hbm.at[pl.ds(r, MOE_CH)], osem.at[s])

    def out_half(r, s):
        return pltpu.make_async_copy(yout.at[s, pl.ds(0, tm)], ys_hbm.at[pl.ds(r, tm)], osem.at[s])

    def start_in(j, s):
        for_chunk(j, lambda r: in_full(r, s).start(), lambda r: in_half(r, s).start())

    def wait_in(j, s):
        for_chunk(j, lambda r: in_full(r, s).wait(), lambda r: in_half(r, s).wait())

    def start_out(j, s):
        for_chunk(j, lambda r: out_full(r, s).start(), lambda r: out_half(r, s).start())

    def wait_out(j, s):
        for_chunk(j, lambda r: out_full(r, s).wait(), lambda r: out_half(r, s).wait())

    @pl.when(i == 0)
    def _():
        start_in(0, 0)

    @pl.when(i + 1 < n)
    def _():
        start_in(i + 1, 1 - slot)

    @pl.when(i >= 2)
    def _():
        wait_out(i - 2, slot)

    def weight_copies(e, ws):
        return (pltpu.make_async_copy(w1_hbm.at[e], w1f.at[ws], wsem.at[ws]),
                pltpu.make_async_copy(w2_hbm.at[e], w2f.at[ws], wsem.at[ws]))

    n_steps = n
    @pl.when(wt_ref[i] == 1)
    def _():
        ws = wt_ref[n_steps + i]
        nxt = wt_ref[2 * n_steps + i]

        @pl.when(i == 0)
        def _():
            for cp in weight_copies(exp_ref[0], 0):
                cp.start()

        for cp in weight_copies(exp_ref[i], ws):
            cp.wait()

        @pl.when(nxt >= 0)
        def _():
            for cp in weight_copies(nxt, 1 - ws):
                cp.start()

        chunk = 128
        def cast_body(c, carry):
            k0 = pl.multiple_of(c * chunk, chunk)
            w1b[pl.ds(k0, chunk), :] = w1f[ws, pl.ds(k0, chunk), :].astype(BF16)
            w2b[pl.ds(k0, chunk), :] = w2f[ws, pl.ds(k0, chunk), :].astype(BF16)
            return carry
        lax.fori_loop(0, D_MODEL // chunk, cast_body, 0)

    wait_in(i, slot)

    @pl.when(size_ref[i] == 2)
    def _():
        yout[slot] = _expert_mlp(xin[slot], w1b, b1_ref, w2b, b2_ref, act)

    @pl.when(size_ref[i] == 1)
    def _():
        yout[slot, 0:tm] = _expert_mlp(xin[slot, 0:tm], w1b, b1_ref, w2b, b2_ref, act.at[pl.ds(0, tm)])

    start_out(i, slot)

    @pl.when(i == n - 1)
    def _():
        @pl.when(i >= 1)
        def _():
            wait_out(i - 1, 1 - slot)
        wait_out(i, slot)
        yout[1, 0:tm] = jnp.zeros((tm, PACK_W), U32)

        def zero_block(j):
            return pltpu.make_async_copy(yout.at[1, pl.ds(0, tm)], ys_hbm.at[pl.ds(pl.multiple_of(j * tm, tm), tm)],
                                         osem.at[1])

        def start_body(j, carry):
            zero_block(j).start()
            return carry

        def wait_body(j, carry):
            zero_block(j).wait()
            return carry
        lax.fori_loop(meta_ref[2 * N_EXPERTS], n_blocks, start_body, 0)
        lax.fori_loop(meta_ref[2 * N_EXPERTS], n_blocks, wait_body, 0)


def _moe(chunk_row, chunk_size, chunk_exp, weight_tbl, meta, xs, w1, b1, w2, b2):
    by_expert = lambda i, row, size, exp, wt, m: (exp[i], 0, 0)
    grid_spec = pltpu.PrefetchScalarGridSpec(
        num_scalar_prefetch=5,
        grid=(chunk_row.shape[0],),
        in_specs=[
            pl.BlockSpec(memory_space=pl.ANY),
            pl.BlockSpec(memory_space=pl.ANY),
            pl.BlockSpec((None, 1, 2 * D_FF), by_expert),
            pl.BlockSpec(memory_space=pl.ANY),
            pl.BlockSpec((None, 1, D_MODEL), by_expert),
        ],
        out_specs=pl.BlockSpec(memory_space=pl.ANY),
        scratch_shapes=[pltpu.VMEM((2, MOE_CH, PACK_W), U32), pltpu.VMEM((2, MOE_CH, PACK_W), U32),
                        pltpu.VMEM((2, D_MODEL, 2 * D_FF), F32), pltpu.VMEM((2, D_FF, D_MODEL), F32),
                        pltpu.VMEM((D_MODEL, 2 * D_FF), BF16), pltpu.VMEM((D_FF, D_MODEL), BF16),
                        pltpu.VMEM((MOE_CH, D_FF), BF16),
                        pltpu.SemaphoreType.DMA((2,)), pltpu.SemaphoreType.DMA((2,)), pltpu.SemaphoreType.DMA((2,))],
    )
    return pl.pallas_call(
        _moe_kernel,
        grid_spec=grid_spec,
        out_shape=jax.ShapeDtypeStruct(xs.shape, U32),
        compiler_params=pltpu.CompilerParams(dimension_semantics=("arbitrary",), vmem_limit_bytes=VMEM_LIMIT),
        name="moe",
    )(chunk_row, chunk_size, chunk_exp, weight_tbl, meta, xs, w1, b1, w2, b2)


def _combine_kernel(n_prompt_tiles, cnt_ref, off_ref, tot_ref, ys_hbm, route_ref, x1_ref, modp_ref, g2s_ref, fg_ref,
                    op_ref, os_ref, buf, sem):
    i = pl.program_id(0)
    nb = pl.num_programs(0)
    slot = i % 2

    def fetch(tile, s):
        _for_strips(cnt_ref, off_ref, tile, buf.at[s], ys_hbm, sem.at[s], False, lambda cp: cp.start())

    @pl.when(i == 0)
    def _():
        buf[...] = jnp.zeros_like(buf)
        fetch(0, 0)

    @pl.when(i + 1 < nb)
    def _():
        fetch(i + 1, 1 - slot)

    _wait_tile_rows(tot_ref[i], buf.at[slot], ys_hbm, sem.at[slot], False)

    route = route_ref[...]
    c_pos = lax.broadcasted_iota(jnp.int32, (SEQ_TILE, SORT_ROWS), 1).astype(F32)
    gmat = jnp.zeros((SEQ_TILE, SORT_ROWS), F32)
    for k in range(TOP_K):
        gmat = gmat + jnp.where(c_pos == route[:, ROUTE_R + k:ROUTE_R + k + 1],
                                route[:, ROUTE_G + k:ROUTE_G + k + 1], 0.0)
    g_hi = gmat.astype(BF16)
    g_lo = (gmat - g_hi.astype(F32)).astype(BF16)
    ff = jnp.concatenate(
        [jnp.dot(g_hi, yb, preferred_element_type=F32) + jnp.dot(g_lo, yb, preferred_element_type=F32)
         for yb in _unpack_bf16_pairs(buf[slot])], axis=1)
    g2 = jnp.where(i < n_prompt_tiles, modp_ref[5:6, :], g2s_ref[...])
    x = x1_ref[...] + g2 * ff
    y = _rms(x, fg_ref[...])

    @pl.when(i < n_prompt_tiles)
    def _():
        op_ref[...] = y

    @pl.when(i >= n_prompt_tiles)
    def _():
        os_ref[...] = y


def _combine(cnt8, tile_off, tot8, ys, route, x1_all, mod_p, g2_rows, final_g, n_prompt_tiles, tiles_per_batch):
    nt = tot8.shape[0]
    tt = SEQ_TILE
    npt = n_prompt_tiles
    grid_spec = pltpu.PrefetchScalarGridSpec(
        num_scalar_prefetch=3,
        grid=(nt,),
        in_specs=[
            pl.BlockSpec(memory_space=pl.ANY),
            pl.BlockSpec((tt, LANES), lambda i, *_: (i, 0)),
            pl.BlockSpec((tt, D_MODEL), lambda i, *_: (i, 0)),
            pl.BlockSpec((None, 6, D_MODEL), lambda i, *_: (jnp.minimum(i, npt - 1) // tiles_per_batch, 0, 0)),
            pl.BlockSpec((tt, D_MODEL), lambda i, *_: (jnp.maximum(i - npt, 0), 0)),
            pl.BlockSpec((1, D_MODEL), lambda i, *_: (0, 0)),
        ],
        out_specs=(pl.BlockSpec((tt, D_MODEL), lambda i, *_: (jnp.minimum(i, npt - 1), 0)),
                   pl.BlockSpec((tt, D_MODEL), lambda i, *_: (jnp.maximum(i - npt, 0), 0))),
        scratch_shapes=[pltpu.VMEM((2, SORT_ROWS, PACK_W), U32), pltpu.SemaphoreType.DMA((2,))],
    )
    return pl.pallas_call(
        functools.partial(_combine_kernel, npt),
        grid_spec=grid_spec,
        out_shape=(jax.ShapeDtypeStruct((npt * tt, D_MODEL), F32),
                   jax.ShapeDtypeStruct(((nt - npt) * tt, D_MODEL), F32)),
        compiler_params=pltpu.CompilerParams(dimension_semantics=("arbitrary",), vmem_limit_bytes=VMEM_LIMIT),
        name="combine",
    )(cnt8, tile_off, tot8, ys, route, x1_all, mod_p, g2_rows, final_g)


def _block_diag_halves(wg):
    halves = []
    for hh in range(2):
        rows = []
        for bi in range(4):
            row = [wg[hh * 4 + bi] if bj == bi else jnp.zeros((LRU_BLOCK_W, LRU_BLOCK_W), wg.dtype) for bj in range(4)]
            rows.append(jnp.concatenate(row, axis=1))
        halves.append(jnp.concatenate(rows, axis=0))
    return jnp.stack(halves).astype(BF16)


def _rope_tables(pos):
    half = HEAD_DIM // 2
    inv = ROPE_THETA ** (-jnp.arange(half, dtype=F32) / half)
    ang = pos.astype(F32)[:, None] * inv[None, :]
    cos = jnp.cos(ang)
    sin = jnp.sin(ang)
    cos128 = jnp.concatenate([cos, cos, cos, cos], axis=1)
    sin128 = jnp.concatenate([-sin, sin, -sin, sin], axis=1)
    return cos128, sin128


def _full_spec(arr, grid_rank):
    zeros = (0,) * arr.ndim
    if grid_rank == 1:
        return pl.BlockSpec(arr.shape, lambda i: zeros)
    return pl.BlockSpec(arr.shape, lambda b, j: zeros)


def kernel(x_prompt, x_sample, state_lru_h, state_conv, cache_win_k, cache_win_v, c_prompt, c_sample, w_ada, b_ada, norm1_g, w_in, b_in, conv_w, conv_b, lru_wa, lru_ba, lru_wx, lru_bx, lru_lambda, attn_sinks, w_out, b_out, norm2_g, w_router, b_router, w1, b1, w2, b2, final_g):
    bp, seq, _ = x_prompt.shape
    bd, tdec, _ = x_sample.shape
    assert tdec == SUBLANES and seq % SEQ_TILE == 0 and bd % SAMPLE_BT == 0
    n_prompt = bp * seq
    n_sample = bd * tdec
    n_tok = n_prompt + n_sample
    l = 0

    head_perm = jnp.array([h for c in range(4) for h in (c, GROUP + c)], dtype=jnp.int32)
    qcol_perm = (head_perm[:, None] * HEAD_DIM + jnp.arange(HEAD_DIM, dtype=jnp.int32)[None, :]).reshape(-1)
    o2 = 2 * LRU_WIDTH
    in_perm = jnp.concatenate([jnp.arange(o2, dtype=jnp.int32), o2 + qcol_perm,
                               jnp.arange(o2 + Q_WIDTH, IN_WIDTH, dtype=jnp.int32)])
    out_perm = jnp.concatenate([jnp.arange(LRU_WIDTH, dtype=jnp.int32), LRU_WIDTH + qcol_perm])
    sinks_perm = attn_sinks[l]

    wr = jnp.tile(w_router[l], (1, TOP_K))
    wr_hi = wr.astype(BF16)
    weights = dict(
        norm1_g=norm1_g[l][None, :], w_in=w_in[l][:, in_perm].astype(BF16), b_in=b_in[l][in_perm][None, :],
        conv_w=conv_w[l], conv_b=conv_b[l][None, :],
        ga=_block_diag_halves(lru_wa[l]), gx=_block_diag_halves(lru_wx[l]),
        lru_ba=lru_ba[l][None, :], lru_bx=lru_bx[l][None, :], lam=lru_lambda[l][None, :],
        w_out=w_out[l][out_perm, :].astype(BF16), b_out=b_out[l][None, :], norm2_g=norm2_g[l][None, :],
        wr_hi=wr_hi, wr_lo=(wr - wr_hi.astype(F32)).astype(BF16),
        b_router=jnp.tile(b_router[l], TOP_K)[None, :],
    )
    wlist = [weights[n] for n in WEIGHT_NAMES]

    mod_all = _ada(jnp.concatenate([c_prompt, c_sample], axis=0), w_ada[l], b_ada[l][None, :])
    mod_p = mod_all[:bp].reshape(bp, 6, D_MODEL)
    mod_s = mod_all[bp:].reshape(bd, 6, D_MODEL).transpose(1, 0, 2)

    cos_p, sin_p = _rope_tables(jnp.arange(seq, dtype=jnp.int32))
    cos_s, sin_s = _rope_tables(PAST_LEN + jnp.arange(tdec, dtype=jnp.int32))
    cos_s = jnp.tile(cos_s, (SAMPLE_BT, 1))
    sin_s = jnp.tile(sin_s, (SAMPLE_BT, 1))
    h0_rows = jnp.pad(state_lru_h[l][:, None, :], ((0, 0), (0, tdec - 1), (0, 0))).reshape(n_sample, LRU_WIDTH)
    cprev_rows = jnp.pad(state_conv[l], ((0, 0), (0, tdec - (CONV_W - 1)), (0, 0))).reshape(n_sample, LRU_WIDTH)
    ck = cache_win_k[l].reshape(bd, WINDOW, KV_WIDTH)
    cv = cache_win_v[l].reshape(bd, WINDOW, KV_WIDTH)
    nj = seq // SEQ_TILE
    npt = n_prompt // SEQ_TILE
    nst = n_sample // SEQ_TILE
    p_tile = lambda i: jnp.minimum(i, npt - 1)
    s_tile = lambda i: jnp.maximum(i - npt, 0)
    all_rows = lambda width: pl.BlockSpec((SEQ_TILE, width), lambda i: (i, 0))
    p_rows = lambda width: pl.BlockSpec((SEQ_TILE, width), lambda i: (p_tile(i), 0))
    s_rows = lambda width: pl.BlockSpec((SEQ_TILE, width), lambda i: (s_tile(i), 0))
    p_tail = lambda rows, width: pl.BlockSpec((None, rows, width), lambda i: (p_tile(i) // nj, 0, 0))
    cache_spec = pl.BlockSpec((SAMPLE_BT, WINDOW, KV_WIDTH), lambda i: (s_tile(i), 0, 0))
    mixer_out_shapes = (
        jax.ShapeDtypeStruct((n_tok, D_MODEL), F32),
        jax.ShapeDtypeStruct((n_tok, D_MODEL), F32),
        jax.ShapeDtypeStruct((n_tok, LANES), F32),
        jax.ShapeDtypeStruct((npt + nst, SUBLANES, LANES), F32),
        jax.ShapeDtypeStruct((bp, SUBLANES, LRU_WIDTH), F32),
        jax.ShapeDtypeStruct((bp, SUBLANES, LRU_WIDTH), F32),
        jax.ShapeDtypeStruct((bp, WINDOW, KV_WIDTH), F32),
        jax.ShapeDtypeStruct((bp, WINDOW, KV_WIDTH), F32),
        jax.ShapeDtypeStruct((n_sample, D_MODEL), F32),
        jax.ShapeDtypeStruct((n_sample, LRU_WIDTH), F32),
        jax.ShapeDtypeStruct((n_sample, LRU_WIDTH), F32),
        jax.ShapeDtypeStruct((bd, WINDOW, KV_WIDTH), F32),
        jax.ShapeDtypeStruct((bd, WINDOW, KV_WIDTH), F32),
    )
    (x1_all, h2_all, route, tile_cnt, hlast_p, ulast_p, klast_p, vlast_p, g2_rows, hs_s, u_s, s_k, s_v) = pl.pallas_call(
        functools.partial(_mixer_kernel, npt, nj),
        grid=(npt + nst,),
        in_specs=[p_rows(D_MODEL),
                  pl.BlockSpec((None, 6, D_MODEL), lambda i: (p_tile(i) // nj, 0, 0)),
                  pl.BlockSpec((SEQ_TILE, LANES), lambda i: (p_tile(i) % nj, 0)),
                  pl.BlockSpec((SEQ_TILE, LANES), lambda i: (p_tile(i) % nj, 0)),
                  s_rows(D_MODEL),
                  pl.BlockSpec((6, SAMPLE_BT, D_MODEL), lambda i: (0, s_tile(i), 0)),
                  pl.BlockSpec((SEQ_TILE, LANES), lambda i: (0, 0)),
                  pl.BlockSpec((SEQ_TILE, LANES), lambda i: (0, 0)),
                  pl.BlockSpec(memory_space=pltpu.SMEM),
                  s_rows(LRU_WIDTH), s_rows(LRU_WIDTH), cache_spec, cache_spec]
                 + [_full_spec(a, 1) for a in wlist],
        out_specs=(all_rows(D_MODEL), all_rows(D_MODEL), all_rows(LANES),
                   pl.BlockSpec((None, SUBLANES, LANES), lambda i: (i, 0, 0)),
                   p_tail(SUBLANES, LRU_WIDTH), p_tail(SUBLANES, LRU_WIDTH),
                   p_tail(WINDOW, KV_WIDTH), p_tail(WINDOW, KV_WIDTH),
                   s_rows(D_MODEL), s_rows(LRU_WIDTH), s_rows(LRU_WIDTH), cache_spec, cache_spec),
        out_shape=mixer_out_shapes,
        scratch_shapes=[pltpu.VMEM((SUBLANES, LRU_WIDTH), F32), pltpu.VMEM((SUBLANES, LRU_WIDTH), F32),
                        pltpu.VMEM((WINDOW, KV_WIDTH), F32), pltpu.VMEM((WINDOW, KV_WIDTH), F32)],
        compiler_params=pltpu.CompilerParams(dimension_semantics=("arbitrary",), vmem_limit_bytes=VMEM_LIMIT),
        name="mixer",
    )(x_prompt.reshape(n_prompt, D_MODEL), mod_p, cos_p, sin_p,
      x_sample.reshape(n_sample, D_MODEL), mod_s, cos_s, sin_s, sinks_perm, h0_rows, cprev_rows, ck, cv, *wlist)

    n_tiles = npt + nst
    n_assign = n_tok * TOP_K
    max_rows = n_assign + n_tiles * N_EXPERTS * (SUBLANES - 1) + N_EXPERTS * (MOE_TM - 1)
    n_blocks = -(-max_rows // MOE_TM)
    cnt = tile_cnt[:, 0, :N_EXPERTS].astype(jnp.int32)
    cnt8 = (cnt + SUBLANES - 1) // SUBLANES * SUBLANES
    counts = jnp.sum(cnt8, axis=0)
    pcounts = (counts + MOE_TM - 1) // MOE_TM * MOE_TM
    pend = jnp.cumsum(pcounts)
    pstart = pend - pcounts
    tile_off = pstart[None, :] + jnp.cumsum(cnt8, axis=0) - cnt8
    tot8 = jnp.sum(cnt8, axis=1)
    meta = jnp.concatenate([pstart + counts, pend, pend[-1:] // MOE_TM]).astype(jnp.int32)
    cnt8_flat = cnt8.reshape(-1)
    off_flat = tile_off.reshape(-1).astype(jnp.int32)

    xs = _dispatch(cnt8_flat, off_flat, tot8, meta, h2_all, route, n_blocks * MOE_TM)

    n_tm = pcounts // MOE_TM
    n_ch = (n_tm + 1) // 2
    ch_end = jnp.cumsum(n_ch)
    ch_start = ch_end - n_ch
    n_chunks = (n_blocks + N_EXPERTS + 1) // 2
    ci = jnp.arange(n_chunks, dtype=jnp.int32)
    owner = jnp.logical_and(ch_start[None, :] <= ci[:, None], ci[:, None] < ch_end[None, :])
    pick = lambda v: jnp.sum(jnp.where(owner, v[None, :], 0), axis=1)
    live = ci < ch_end[-1]
    local = ci - pick(ch_start)
    chunk_exp = jnp.where(live, pick(jnp.arange(N_EXPERTS, dtype=jnp.int32)), N_EXPERTS - 1).astype(jnp.int32)
    chunk_row = jnp.where(live, pick(pstart) + local * MOE_CH, 0).astype(jnp.int32)
    chunk_size = jnp.where(live, jnp.where(2 * local + 2 <= pick(n_tm), 2, 1), 0).astype(jnp.int32)
    has_rows = n_ch > 0
    e_ar = jnp.arange(N_EXPERTS, dtype=jnp.int32)
    w_slot = (jnp.cumsum(has_rows.astype(jnp.int32)) - 1) % 2
    later = jnp.logical_and(has_rows[None, :], e_ar[None, :] > e_ar[:, None])
    nxt = jnp.min(jnp.where(later, e_ar[None, :], N_EXPERTS), axis=1)
    nxt = jnp.where(nxt == N_EXPERTS, -1, nxt)
    weight_tbl = jnp.concatenate([jnp.logical_and(live, local == 0).astype(jnp.int32),
                                  pick(w_slot), jnp.where(live, pick(nxt), -1)]).astype(jnp.int32)
    ys = _moe(chunk_row, chunk_size, chunk_exp, weight_tbl, meta, xs, w1[l], b1[l][:, None, :], w2[l], b2[l][:, None, :])

    y_p, y_s = _combine(cnt8_flat, off_flat, tot8, ys, route, x1_all, mod_p, g2_rows,
                        final_g[None, :], n_prompt // SEQ_TILE, seq // SEQ_TILE)

    y_prompt = y_p.reshape(bp, seq, D_MODEL)
    y_sample = y_s.reshape(bd, tdec, D_MODEL)
    p_h = hlast_p[:, SUBLANES - 1, :][None]
    p_c = ulast_p[:, SUBLANES - (CONV_W - 1):, :][None]
    p_k = klast_p.reshape(1, bp, WINDOW, N_KV_HEADS, HEAD_DIM)
    p_v = vlast_p.reshape(1, bp, WINDOW, N_KV_HEADS, HEAD_DIM)
    s_h = hs_s.reshape(bd, tdec, LRU_WIDTH)[:, tdec - 1, :][None]
    s_c = u_s.reshape(bd, tdec, LRU_WIDTH)[:, tdec - (CONV_W - 1):, :][None]
    s_kk = s_k.reshape(1, bd, WINDOW, N_KV_HEADS, HEAD_DIM)
    s_vv = s_v.reshape(1, bd, WINDOW, N_KV_HEADS, HEAD_DIM)
    return (y_prompt, y_sample, p_h, p_c, p_k, p_v, s_h, s_c, s_kk, s_vv)
```

```python
import functools

import jax
import jax.numpy as jnp
from jax import lax
from jax.experimental import pallas as pl
from jax.experimental.pallas import tpu as pltpu

F32 = jnp.float32
BF16 = jnp.bfloat16

D_MODEL = 1024
LRU_WIDTH = 512
LRU_BLOCKS = 8
LRU_BLOCK_W = LRU_WIDTH // LRU_BLOCKS
CONV_W = 4
LRU_C = 8.0
HEAD_DIM = 64
N_HEADS = 8
N_KV_HEADS = 2
GROUP = N_HEADS // N_KV_HEADS
WINDOW = 128
ROPE_THETA = 10000.0
N_EXPERTS = 32
TOP_K = 4
D_FF = D_MODEL
SWIGLU_LIMIT = 7.0
SWIGLU_ALPHA = 1.702
NORM_EPS = 1e-5
PAST_LEN = 8192
Q_WIDTH = N_HEADS * HEAD_DIM
KV_WIDTH = N_KV_HEADS * HEAD_DIM
IN_WIDTH = 2 * LRU_WIDTH + Q_WIDTH + 2 * KV_WIDTH

LANES = 128
SUBLANES = 8
SEQ_TILE = 256
SAMPLE_BT = 32
MOE_TM = 256
MOE_CHUNK_CODES = (4, 2, 1)
MOE_CH = MOE_CHUNK_CODES[0] * MOE_TM
COMB_TT = 256
NEG_BIG = -1e30
VMEM_LIMIT = 56 * 1024 * 1024


def _rms(x, g):
    return x * lax.rsqrt(jnp.mean(x * x, axis=-1, keepdims=True) + NORM_EPS) * g


def _rms_mod(x, gain, shift):
    return x * lax.rsqrt(jnp.mean(x * x, axis=-1, keepdims=True) + NORM_EPS) * gain + shift


def _group_scan(a, b):
    rows, width = a.shape
    groups = rows // SUBLANES
    a3 = a.reshape(groups, SUBLANES, width)
    b3 = b.reshape(groups, SUBLANES, width)
    t = lax.broadcasted_iota(jnp.int32, (1, SUBLANES, 1), 1)
    d = 1
    while d < SUBLANES:
        keep = t >= d
        a_s = jnp.where(keep, pltpu.roll(a3, d, axis=1), 1.0)
        b_s = jnp.where(keep, pltpu.roll(b3, d, axis=1), 0.0)
        b3 = a3 * b_s + b3
        a3 = a3 * a_s
        d *= 2
    return a3.reshape(rows, width), b3.reshape(rows, width)


def _chain_groups(a_grp, b_grp, h_in):
    rows = a_grp.shape[0]
    out = []
    carry = h_in
    for g in range(rows // SUBLANES):
        sl = slice(g * SUBLANES, (g + 1) * SUBLANES)
        hg = b_grp[sl] + a_grp[sl] * carry
        out.append(hg)
        carry = hg[SUBLANES - 1:SUBLANES]
    return jnp.concatenate(out, axis=0)


def _rope128(x, cos, sin_signed, first_half):
    sw = jnp.where(first_half, pltpu.roll(x, LANES - HEAD_DIM // 2, axis=1), pltpu.roll(x, HEAD_DIM // 2, axis=1))
    return x * cos + sw * sin_signed


def _softplus(x):
    return jnp.maximum(x, 0.0) + jnp.log1p(jnp.exp(-jnp.abs(x)))


def _lru_coeffs(uc, w, first_pos_mask):
    ub = uc.astype(BF16)
    half = LRU_WIDTH // 2
    ra = jnp.concatenate([jnp.dot(ub[:, :half], w['ga'][0], preferred_element_type=F32),
                          jnp.dot(ub[:, half:], w['ga'][1], preferred_element_type=F32)], axis=1)
    rx = jnp.concatenate([jnp.dot(ub[:, :half], w['gx'][0], preferred_element_type=F32),
                          jnp.dot(ub[:, half:], w['gx'][1], preferred_element_type=F32)], axis=1)
    r = jax.nn.sigmoid(ra + w['lru_ba'][...])
    i = jax.nn.sigmoid(rx + w['lru_bx'][...])
    log_a = -LRU_C * r * _softplus(-w['lam'][...])
    a = jnp.exp(log_a)
    om = 1.0 - a * a
    mult = jnp.where(om > 0.0, om * lax.rsqrt(om), 0.0)
    if first_pos_mask is not None:
        mult = jnp.where(first_pos_mask, 1.0, mult)
    return a, mult * i * uc


def _conv_taps(u, s1, s2, s3, w):
    cw = w['conv_w']
    return w['conv_b'][...] + s3 * cw[0:1, :] + s2 * cw[1:2, :] + s1 * cw[2:3, :] + u * cw[3:4, :]


def _in_proj(x, mod, w):
    sh1, sc1 = mod
    h = _rms_mod(x, w['norm1_g'][...] * (1.0 + sc1), sh1)
    return jnp.dot(h.astype(BF16), w['w_in'][...], preferred_element_type=F32) + w['b_in'][...]


def _post_mix(x, mix, mod, w):
    g1, sh2, sc2 = mod
    x1 = x + g1 * (jnp.dot(mix.astype(BF16), w['w_out'][...], preferred_element_type=F32) + w['b_out'][...])
    h2 = _rms_mod(x1, w['norm2_g'][...] * (1.0 + sc2), sh2)
    h2_hi = h2.astype(BF16)
    h2_lo = (h2 - h2_hi.astype(F32)).astype(BF16)
    logits = (jnp.dot(h2_hi, w['wr_hi'][...], preferred_element_type=F32)
              + jnp.dot(h2_lo, w['wr_hi'][...], preferred_element_type=F32)
              + jnp.dot(h2_hi, w['wr_lo'][...], preferred_element_type=F32)) + w['b_router'][...]
    return x1, h2, logits


ROUTE_E, ROUTE_G, ROUTE_R = 0, TOP_K, 2 * TOP_K


def _lane_roll1(v, shift):
    return pltpu.roll(jnp.broadcast_to(v, (SUBLANES, LANES)), shift, axis=1)[0:1]


def _route_tile(lg):
    rows = lg.shape[0]
    lane = lax.broadcasted_iota(jnp.int32, (1, LANES), 1)
    e_of = lane % N_EXPERTS
    grp = lane // N_EXPERTS
    onehot = jnp.zeros((rows, LANES), F32)
    vals, ids = [], []
    for k in range(TOP_K):
        m = jnp.max(lg, axis=1, keepdims=True)
        idx = jnp.min(jnp.where(lg == m, e_of, N_EXPERTS), axis=1, keepdims=True)
        sel = e_of == idx
        lg = jnp.where(sel, -jnp.inf, lg)
        onehot = jnp.where(jnp.logical_and(sel, grp == k), 1.0, onehot)
        vals.append(m)
        ids.append(idx)
    ex = [jnp.exp(v - vals[0]) for v in vals]
    denom = ex[0] + ex[1] + ex[2] + ex[3]

    r_i = lax.broadcasted_iota(jnp.int32, (rows, rows), 0)
    c_i = lax.broadcasted_iota(jnp.int32, (rows, rows), 1)
    strict_lower = jnp.where(r_i > c_i, 1.0, 0.0).astype(BF16)
    prefix = jnp.dot(strict_lower, onehot.astype(BF16), preferred_element_type=F32)
    cnt = jnp.sum(onehot, axis=0, keepdims=True)
    base = jnp.zeros((1, LANES), F32)
    tot = cnt
    for s in range(1, TOP_K):
        rolled = _lane_roll1(cnt, s * N_EXPERTS)
        base = base + jnp.where(lane >= s * N_EXPERTS, rolled, 0.0)
        tot = tot + rolled
    pad_cnt = jnp.floor((tot + (SUBLANES - 1.0)) * (1.0 / SUBLANES)) * SUBLANES
    inc = pad_cnt
    d = 1
    while d < N_EXPERTS:
        inc = inc + jnp.where(e_of >= d, _lane_roll1(inc, d), 0.0)
        d *= 2
    strip_start = inc - pad_cnt
    ranked = onehot * (prefix + base + strip_start)

    route = jnp.zeros((rows, LANES), F32)
    for k in range(TOP_K):
        rank_k = jnp.sum(jnp.where(grp == k, ranked, 0.0), axis=1, keepdims=True)
        route = jnp.where(lane == ROUTE_E + k, ids[k].astype(F32), route)
        route = jnp.where(lane == ROUTE_G + k, ex[k] / denom, route)
        route = jnp.where(lane == ROUTE_R + k, rank_k, route)
    return route, tot


def _softmax_sink_pv(s, sink_col, v_bf16):
    m = jnp.maximum(jnp.max(s, axis=-1, keepdims=True), sink_col)
    p = jnp.exp(s - m)
    denom = jnp.sum(p, axis=-1, keepdims=True) + jnp.exp(sink_col - m)
    return p, denom


WEIGHT_NAMES = ('norm1_g', 'w_in', 'b_in', 'conv_w', 'conv_b', 'ga', 'gx', 'lru_ba', 'lru_bx', 'lam',
                'w_out', 'b_out', 'norm2_g', 'wr_hi', 'wr_lo', 'b_router')


def _ada_kernel(c_ref, w_ref, b_ref, o_ref):
    c = c_ref[...]
    s = c * jax.nn.sigmoid(c)
    o_ref[...] = jnp.dot(s, w_ref[...], preferred_element_type=F32, precision=lax.Precision.HIGHEST) + b_ref[...]


def _ada(c_all, w_ada, b_ada):
    rows = c_all.shape[0]
    return pl.pallas_call(
        _ada_kernel,
        grid=(6,),
        in_specs=[pl.BlockSpec((rows, D_MODEL), lambda i: (0, 0)),
                  pl.BlockSpec((D_MODEL, D_MODEL), lambda i: (0, i)),
                  pl.BlockSpec((1, D_MODEL), lambda i: (0, i))],
        out_specs=pl.BlockSpec((rows, D_MODEL), lambda i: (0, i)),
        out_shape=jax.ShapeDtypeStruct((rows, 6 * D_MODEL), F32),
        compiler_params=pltpu.CompilerParams(dimension_semantics=("arbitrary",), vmem_limit_bytes=VMEM_LIMIT),
        name="ada",
    )(c_all, w_ada, b_ada)


def _prompt_body(j, x_ref, mod_ref, cos_ref, sin_ref, sinks_ref, w, x1_ref, h2_ref, lg_ref,
                 hlast_ref, ulast_ref, klast_ref, vlast_ref, conv_c, h_c, k_c, v_c):
    ts = SEQ_TILE

    @pl.when(j == 0)
    def _():
        conv_c[...] = jnp.zeros_like(conv_c)
        h_c[...] = jnp.zeros_like(h_c)
        k_c[...] = jnp.zeros_like(k_c)
        v_c[...] = jnp.zeros_like(v_c)

    x = x_ref[...]
    mod = mod_ref[...]
    proj = _in_proj(x, (mod[0:1], mod[1:2]), w)
    u = proj[:, :LRU_WIDTH]
    gate = proj[:, LRU_WIDTH:2 * LRU_WIDTH]
    o2 = 2 * LRU_WIDTH

    rowid = lax.broadcasted_iota(jnp.int32, (ts, 1), 0)
    u_ext = jnp.concatenate([conv_c[...], u], axis=0)
    s1, s2, s3 = (pltpu.roll(u_ext, d, axis=0)[SUBLANES:] for d in (1, 2, 3))
    uc = _conv_taps(u, s1, s2, s3, w)
    conv_c[...] = u[ts - SUBLANES:]
    ulast_ref[...] = u[ts - SUBLANES:]

    first_pos = jnp.logical_and(rowid == 0, j == 0)
    a, bt = _lru_coeffs(uc, w, first_pos)
    hs = _chain_groups(*_group_scan(a, bt), h_c[0:1, :])
    h_tail = hs[ts - SUBLANES:]
    h_c[...] = jnp.broadcast_to(h_tail[SUBLANES - 1:SUBLANES, :], h_c.shape)
    hlast_ref[...] = h_tail
    lru_out = hs * jax.nn.gelu(gate)

    cos = cos_ref[...]
    sin = sin_ref[...]
    lane = lax.broadcasted_iota(jnp.int32, (1, LANES), 1)
    first_half = (lane % HEAD_DIM) < (HEAD_DIM // 2)
    qcols = [_rope128(proj[:, o2 + c * LANES:o2 + (c + 1) * LANES], cos, sin, first_half) * (HEAD_DIM ** -0.5)
             for c in range(4)]
    k = _rope128(proj[:, o2 + Q_WIDTH:o2 + Q_WIDTH + KV_WIDTH], cos, sin, first_half)
    v = proj[:, o2 + Q_WIDTH + KV_WIDTH:]
    k_ext = jnp.concatenate([k_c[...], k], axis=0).astype(BF16)
    v_ext = jnp.concatenate([v_c[...], v], axis=0).astype(BF16)
    k_c[...] = k[ts - WINDOW:]
    v_c[...] = v[ts - WINDOW:]
    klast_ref[...] = k[ts - WINDOW:]
    vlast_ref[...] = v[ts - WINDOW:]

    qi = lax.broadcasted_iota(jnp.int32, (WINDOW, 2 * WINDOW), 0)
    kj = lax.broadcasted_iota(jnp.int32, (WINDOW, 2 * WINDOW), 1)
    band = jnp.logical_and(kj > qi, kj <= qi + WINDOW)
    lane_lo = lane < HEAD_DIM
    grow = lax.broadcasted_iota(jnp.int32, (GROUP * WINDOW, 1), 0) // WINDOW
    attn_cols = [[] for _ in range(4)]
    for blk in range(ts // WINDOW):
        prev_ok = jnp.logical_or(j > 0, blk > 0)
        mask = jnp.logical_and(band, jnp.logical_or(kj >= WINDOW, prev_ok))
        mask4 = jnp.concatenate([mask] * GROUP, axis=0)
        kb = k_ext[blk * WINDOW:(blk + 2) * WINDOW]
        vb = v_ext[blk * WINDOW:(blk + 2) * WINDOW]
        outs = []
        for kv in range(N_KV_HEADS):
            sel = lane_lo if kv == 0 else jnp.logical_not(lane_lo)
            qs = jnp.concatenate(
                [jnp.where(sel, qc[blk * WINDOW:(blk + 1) * WINDOW], 0.0) for qc in qcols], axis=0).astype(BF16)
            s = lax.dot_general(qs, kb, (((1,), (1,)), ((), ())), preferred_element_type=F32)
            s = jnp.where(mask4, s, NEG_BIG)
            sink_col = jnp.zeros((GROUP * WINDOW, 1), F32)
            for g in range(GROUP):
                sink_col = jnp.where(grow == g, sinks_ref[kv * GROUP + g], sink_col)
            p, denom = _softmax_sink_pv(s, sink_col, vb)
            outs.append(jnp.dot(p.astype(BF16), vb, preferred_element_type=F32) / denom)
        for c in range(4):
            attn_cols[c].append(jnp.where(lane_lo, outs[0][c * WINDOW:(c + 1) * WINDOW],
                                          outs[1][c * WINDOW:(c + 1) * WINDOW]))
    attn = jnp.concatenate([jnp.concatenate(cols, axis=0) for cols in attn_cols], axis=1)

    mix = jnp.concatenate([lru_out, attn], axis=1)
    x1, h2, logits = _post_mix(x, mix, (mod[2:3], mod[3:4], mod[4:5]), w)
    x1_ref[...] = x1
    h2_ref[...] = h2
    route, tot = _route_tile(logits)
    lg_ref[0][...] = route
    lg_ref[1][...] = jnp.broadcast_to(tot, lg_ref[1].shape)


def _expand_rows(m, t):
    b, wd = m.shape
    return jnp.broadcast_to(m[:, None, :], (b, t, wd)).reshape(b * t, wd)


def _sample_body(x_ref, mod_ref, cos_ref, sin_ref, sinks_ref, h0_ref, cprev_ref, ck_ref, cv_ref, w,
                 x1_ref, h2_ref, lg_ref, g2_ref, hs_ref, u_ref, ko_ref, vo_ref):
    bt_, t = SAMPLE_BT, SUBLANES
    rows = bt_ * t

    x = x_ref[...]
    mods = [_expand_rows(mod_ref[i], t) for i in range(6)]
    proj = _in_proj(x, (mods[0], mods[1]), w)
    u = proj[:, :LRU_WIDTH]
    gate = proj[:, LRU_WIDTH:2 * LRU_WIDTH]
    o2 = 2 * LRU_WIDTH
    u_ref[...] = u

    rowid = lax.broadcasted_iota(jnp.int32, (rows, 1), 0) % t
    cprev = cprev_ref[...]
    taps = []
    for d in (1, 2, 3):
        taps.append(jnp.where(rowid >= d, pltpu.roll(u, d, axis=0),
                              pltpu.roll(cprev, (d - (CONV_W - 1)) % rows, axis=0)))
    uc = _conv_taps(u, taps[0], taps[1], taps[2], w)

    a, bt = _lru_coeffs(uc, w, None)
    bt = bt + a * h0_ref[...]
    _, hs = _group_scan(a, bt)
    hs_ref[...] = hs
    lru_out = hs * jax.nn.gelu(gate)

    cos = cos_ref[...]
    sin = sin_ref[...]
    lane = lax.broadcasted_iota(jnp.int32, (1, LANES), 1)
    first_half = (lane % HEAD_DIM) < (HEAD_DIM // 2)
    qcols = [_rope128(proj[:, o2 + c * LANES:o2 + (c + 1) * LANES], cos, sin, first_half) * (HEAD_DIM ** -0.5)
             for c in range(4)]
    k = _rope128(proj[:, o2 + Q_WIDTH:o2 + Q_WIDTH + KV_WIDTH], cos, sin, first_half)
    v = proj[:, o2 + Q_WIDTH + KV_WIDTH:]
    k3 = k.reshape(bt_, t, KV_WIDTH)
    v3 = v.reshape(bt_, t, KV_WIDTH)
    ck = ck_ref[...]
    cv = cv_ref[...]
    ko_ref[:, :WINDOW - t, :] = ck[:, t:, :]
    ko_ref[:, WINDOW - t:, :] = k3
    vo_ref[:, :WINDOW - t, :] = cv[:, t:, :]
    vo_ref[:, WINDOW - t:, :] = v3

    ckb, cvb, k3b, v3b = ck.astype(BF16), cv.astype(BF16), k3.astype(BF16), v3.astype(BF16)
    lane_lo = lane < HEAD_DIM
    gq = GROUP * t
    tq = lax.broadcasted_iota(jnp.int32, (1, gq, 1), 1) % t
    mask_c = lax.broadcasted_iota(jnp.int32, (1, gq, WINDOW), 2) > tq
    mask_n = lax.broadcasted_iota(jnp.int32, (1, gq, t), 2) <= tq
    grow = lax.broadcasted_iota(jnp.int32, (1, gq, 1), 1) // t
    bdims = (((2,), (2,)), ((0,), (0,)))
    pdims = (((2,), (1,)), ((0,), (0,)))
    outs = []
    for kv in range(N_KV_HEADS):
        sel = lane_lo if kv == 0 else jnp.logical_not(lane_lo)
        q3 = jnp.concatenate([jnp.where(sel, qc, 0.0).reshape(bt_, t, LANES) for qc in qcols], axis=1).astype(BF16)
        sc = lax.dot_general(q3, ckb, bdims, preferred_element_type=F32)
        sn = lax.dot_general(q3, k3b, bdims, preferred_element_type=F32)
        sc = jnp.where(mask_c, sc, NEG_BIG)
        sn = jnp.where(mask_n, sn, NEG_BIG)
        sink_col = jnp.zeros((1, gq, 1), F32)
        for g in range(GROUP):
            sink_col = jnp.where(grow == g, sinks_ref[kv * GROUP + g], sink_col)
        m = jnp.maximum(jnp.maximum(jnp.max(sc, axis=-1, keepdims=True), jnp.max(sn, axis=-1, keepdims=True)),
                        sink_col)
        pc = jnp.exp(sc - m)
        pn = jnp.exp(sn - m)
        denom = jnp.sum(pc, axis=-1, keepdims=True) + jnp.sum(pn, axis=-1, keepdims=True) + jnp.exp(sink_col - m)
        o = (lax.dot_general(pc.astype(BF16), cvb, pdims, preferred_element_type=F32)
             + lax.dot_general(pn.astype(BF16), v3b, pdims, preferred_element_type=F32)) / denom
        outs.append(o)
    attn = jnp.concatenate(
        [jnp.where(lane_lo, outs[0][:, c * t:(c + 1) * t, :], outs[1][:, c * t:(c + 1) * t, :]).reshape(rows, LANES)
         for c in range(4)], axis=1)

    mix = jnp.concatenate([lru_out, attn], axis=1)
    x1, h2, logits = _post_mix(x, mix, (mods[2], mods[3], mods[4]), w)
    x1_ref[...] = x1
    h2_ref[...] = h2
    route, tot = _route_tile(logits)
    lg_ref[0][...] = route
    lg_ref[1][...] = jnp.broadcast_to(tot, lg_ref[1].shape)
    g2_ref[...] = mods[5]


def _mixer_kernel(n_prompt_tiles, tiles_per_seq,
                  xp_ref, modp_ref, cosp_ref, sinp_ref, xs_ref, mods_ref, coss_ref, sins_ref, sinks_ref,
                  h0_ref, cprev_ref, ck_ref, cv_ref, *rest):
    nw = len(WEIGHT_NAMES)
    w = dict(zip(WEIGHT_NAMES, rest[:nw]))
    (x1_ref, h2_ref, route_ref, cnt_ref, hlast_ref, ulast_ref, klast_ref, vlast_ref,
     g2_ref, hs_ref, u_ref, ko_ref, vo_ref, conv_c, h_c, k_c, v_c) = rest[nw:]
    lg_ref = (route_ref, cnt_ref)
    i = pl.program_id(0)

    @pl.when(i < n_prompt_tiles)
    def _():
        _prompt_body(i % tiles_per_seq, xp_ref, modp_ref, cosp_ref, sinp_ref, sinks_ref, w, x1_ref, h2_ref, lg_ref,
                     hlast_ref, ulast_ref, klast_ref, vlast_ref, conv_c, h_c, k_c, v_c)

    @pl.when(i >= n_prompt_tiles)
    def _():
        _sample_body(xs_ref, mods_ref, coss_ref, sins_ref, sinks_ref, h0_ref, cprev_ref, ck_ref, cv_ref, w,
                     x1_ref, h2_ref, lg_ref, g2_ref, hs_ref, u_ref, ko_ref, vo_ref)


PACK_W = D_MODEL // 2
U32 = jnp.uint32


def _pack_bf16_pairs(x):
    xb = x.astype(BF16).astype(F32)
    lo = lax.bitcast_convert_type(xb[:, :PACK_W], U32)
    hi = lax.bitcast_convert_type(xb[:, PACK_W:], U32)
    return jnp.bitwise_or(jnp.bitwise_and(hi, jnp.uint32(0xFFFF0000)), lax.shift_right_logical(lo, jnp.uint32(16)))


def _unpack_bf16_pairs(w):
    lo = lax.bitcast_convert_type(lax.shift_left(w, jnp.uint32(16)), F32).astype(BF16)
    hi = lax.bitcast_convert_type(jnp.bitwise_and(w, jnp.uint32(0xFFFF0000)), F32).astype(BF16)
    return lo, hi


STRIP_SIZES = tuple(SUBLANES << b for b in range(6))
SORT_ROWS = SEQ_TILE * TOP_K + N_EXPERTS * SUBLANES
TILE_WAIT_SIZES = tuple(SUBLANES << b for b in range(8))


def _for_strips(cnt_ref, off_ref, tile, buf_slot, hbm, sem, to_hbm, act):
    def e_body(e, local):
        n = cnt_ref[tile * N_EXPERTS + e]
        glob = off_ref[tile * N_EXPERTS + e]
        done = 0
        for p in reversed(STRIP_SIZES):
            piece = n & p
            lo = pl.ds(pl.multiple_of(local + done, SUBLANES), p)
            gl = pl.ds(pl.multiple_of(glob + done, SUBLANES), p)

            @pl.when(piece != 0)
            def _():
                if to_hbm:
                    act(pltpu.make_async_copy(buf_slot.at[lo], hbm.at[gl], sem))
                else:
                    act(pltpu.make_async_copy(hbm.at[gl], buf_slot.at[lo], sem))
            done = done + piece
        return local + n
    lax.fori_loop(0, N_EXPERTS, e_body, 0)


def _wait_tile_rows(total, buf_slot, hbm, sem, to_hbm):
    for p in TILE_WAIT_SIZES:
        @pl.when((total & p) != 0)
        def _():
            if to_hbm:
                pltpu.make_async_copy(buf_slot.at[pl.ds(0, p)], hbm.at[pl.ds(0, p)], sem).wait()
            else:
                pltpu.make_async_copy(hbm.at[pl.ds(0, p)], buf_slot.at[pl.ds(0, p)], sem).wait()


def _dispatch_kernel(cnt_ref, off_ref, tot_ref, meta_ref, h2_ref, route_ref, xs_hbm, sbuf, zblk, sem, zsem):
    i = pl.program_id(0)
    nb = pl.num_programs(0)
    slot = i % 2
    n_blocks = xs_hbm.shape[0] // MOE_TM

    route_t = route_ref[...].T
    r_pos = lax.broadcasted_iota(jnp.int32, (SORT_ROWS, SEQ_TILE), 0).astype(F32)
    perm = jnp.zeros((SORT_ROWS, SEQ_TILE), F32)
    for k in range(TOP_K):
        perm = perm + jnp.where(r_pos == route_t[ROUTE_R + k:ROUTE_R + k + 1, :], 1.0, 0.0)
    sbuf[slot] = _pack_bf16_pairs(jnp.dot(perm.astype(BF16), h2_ref[...].astype(BF16), preferred_element_type=F32))

    _for_strips(cnt_ref, off_ref, i, sbuf.at[slot], xs_hbm, sem.at[slot], True, lambda cp: cp.start())

    @pl.when(i > 0)
    def _():
        _wait_tile_rows(tot_ref[jnp.maximum(i - 1, 0)], sbuf.at[1 - slot], xs_hbm, sem.at[1 - slot], True)

    @pl.when(i == nb - 1)
    def _():
        _wait_tile_rows(tot_ref[i], sbuf.at[slot], xs_hbm, sem.at[slot], True)
        zblk[...] = jnp.zeros_like(zblk)

        def for_region_tails(act):
            def e_body(e, carry):
                start = meta_ref[e]
                n = meta_ref[N_EXPERTS + e] - start
                done = 0
                for p in reversed(STRIP_SIZES[:-1]):
                    piece = n & p
                    rows = pl.ds(pl.multiple_of(start + done, SUBLANES), p)

                    @pl.when(piece != 0)
                    def _():
                        act(pltpu.make_async_copy(zblk.at[pl.ds(0, p)], xs_hbm.at[rows], zsem.at[0]))
                    done = done + piece
                return carry
            lax.fori_loop(0, N_EXPERTS, e_body, 0)

        def for_tail_blocks(act):
            def b_body(j, carry):
                act(pltpu.make_async_copy(zblk, xs_hbm.at[pl.ds(pl.multiple_of(j * MOE_TM, MOE_TM), MOE_TM)],
                                          zsem.at[0]))
                return carry
            lax.fori_loop(meta_ref[2 * N_EXPERTS], n_blocks, b_body, 0)

        for_region_tails(lambda cp: cp.start())
        for_tail_blocks(lambda cp: cp.start())
        for_region_tails(lambda cp: cp.wait())
        for_tail_blocks(lambda cp: cp.wait())


def _dispatch(cnt8, tile_off, tot8, meta, h2_all, route, n_rows):
    nt = tot8.shape[0]
    tt = SEQ_TILE
    grid_spec = pltpu.PrefetchScalarGridSpec(
        num_scalar_prefetch=4,
        grid=(nt,),
        in_specs=[pl.BlockSpec((tt, D_MODEL), lambda i, *_: (i, 0)),
                  pl.BlockSpec((tt, LANES), lambda i, *_: (i, 0))],
        out_specs=pl.BlockSpec(memory_space=pl.ANY),
        scratch_shapes=[pltpu.VMEM((2, SORT_ROWS, PACK_W), U32), pltpu.VMEM((MOE_TM, PACK_W), U32),
                        pltpu.SemaphoreType.DMA((2,)), pltpu.SemaphoreType.DMA((1,))],
    )
    return pl.pallas_call(
        _dispatch_kernel,
        grid_spec=grid_spec,
        out_shape=jax.ShapeDtypeStruct((n_rows, PACK_W), U32),
        compiler_params=pltpu.CompilerParams(dimension_semantics=("arbitrary",), vmem_limit_bytes=VMEM_LIMIT),
        name="dispatch",
    )(cnt8, tile_off, tot8, meta, h2_all, route)


def _expert_mlp(words, w1b, b1_ref, w2b, b2_ref, act_ref):
    xb = jnp.concatenate(_unpack_bf16_pairs(words), axis=1)
    q = D_FF // 4
    for c in range(4):
        zg = jnp.dot(xb, w1b[:, c * q:(c + 1) * q], preferred_element_type=F32) + b1_ref[:, c * q:(c + 1) * q]
        zl = (jnp.dot(xb, w1b[:, D_FF + c * q:D_FF + (c + 1) * q], preferred_element_type=F32)
              + b1_ref[:, D_FF + c * q:D_FF + (c + 1) * q])
        glu = jnp.minimum(zg, SWIGLU_LIMIT)
        lin = jnp.clip(zl, -SWIGLU_LIMIT, SWIGLU_LIMIT)
        act_ref[:, c * q:(c + 1) * q] = (glu * jax.nn.sigmoid(SWIGLU_ALPHA * glu) * (lin + 1.0)).astype(BF16)
    return _pack_bf16_pairs(jnp.dot(act_ref[...], w2b[...], preferred_element_type=F32) + b2_ref[...])


def _moe_kernel(row_ref, size_ref, exp_ref, wt_ref, meta_ref, xs_hbm, w1_hbm, b1_ref, w2_hbm, b2_ref, ys_hbm,
                xin, yout, w1f, w2f, w1b, w2b, act, isem, osem, wsem):
    i = pl.program_id(0)
    n = pl.num_programs(0)
    slot = i % 2
    tm = MOE_TM
    n_blocks = ys_hbm.shape[0] // tm

    def for_chunk(j, fn):
        r = pl.multiple_of(row_ref[j], tm)
        for code in MOE_CHUNK_CODES:
            @pl.when(size_ref[j] == code)
            def _():
                fn(r, code * tm)

    def in_copy(r, rows, s):
        return pltpu.make_async_copy(xs_hbm.at[pl.ds(r, rows)], xin.at[s, pl.ds(0, rows)], isem.at[s])

    def out_copy(r, rows, s):
        return pltpu.make_async_copy(yout.at[s, pl.ds(0, rows)], ys_hbm.at[pl.ds(r, rows)], osem.at[s])

    def start_in(j, s):
        for_chunk(j, lambda r, rows: in_copy(r, rows, s).start())

    def wait_in(j, s):
        for_chunk(j, lambda r, rows: in_copy(r, rows, s).wait())

    def start_out(j, s):
        for_chunk(j, lambda r, rows: out_copy(r, rows, s).start())

    def wait_out(j, s):
        for_chunk(j, lambda r, rows: out_copy(r, rows, s).wait())

    @pl.when(i == 0)
    def _():
        start_in(0, 0)

    @pl.when(i + 1 < n)
    def _():
        start_in(i + 1, 1 - slot)

    @pl.when(i >= 2)
    def _():
        wait_out(i - 2, slot)

    def weight_copies(e, ws):
        return (pltpu.make_async_copy(w1_hbm.at[e], w1f.at[ws], wsem.at[ws]),
                pltpu.make_async_copy(w2_hbm.at[e], w2f.at[ws], wsem.at[ws]))

    n_steps = n
    @pl.when(wt_ref[i] == 1)
    def _():
        ws = wt_ref[n_steps + i]
        nxt = wt_ref[2 * n_steps + i]

        @pl.when(i == 0)
        def _():
            for cp in weight_copies(exp_ref[0], 0):
                cp.start()

        for cp in weight_copies(exp_ref[i], ws):
            cp.wait()

        @pl.when(nxt >= 0)
        def _():
            for cp in weight_copies(nxt, 1 - ws):
                cp.start()

        chunk = 128
        def cast_body(c, carry):
            k0 = pl.multiple_of(c * chunk, chunk)
            w1b[pl.ds(k0, chunk), :] = w1f[ws, pl.ds(k0, chunk), :].astype(BF16)
            w2b[pl.ds(k0, chunk), :] = w2f[ws, pl.ds(k0, chunk), :].astype(BF16)
            return carry
        lax.fori_loop(0, D_MODEL // chunk, cast_body, 0)

    wait_in(i, slot)

    for code in MOE_CHUNK_CODES:
        rows = code * tm

        @pl.when(size_ref[i] == code)
        def _():
            yout[slot, 0:rows] = _expert_mlp(xin[slot, 0:rows], w1b, b1_ref, w2b, b2_ref, act.at[pl.ds(0, rows)])

    start_out(i, slot)

    @pl.when(i == n - 1)
    def _():
        @pl.when(i >= 1)
        def _():
            wait_out(i - 1, 1 - slot)
        wait_out(i, slot)
        yout[1, 0:tm] = jnp.zeros((tm, PACK_W), U32)

        def zero_block(j):
            return pltpu.make_async_copy(yout.at[1, pl.ds(0, tm)], ys_hbm.at[pl.ds(pl.multiple_of(j * tm, tm), tm)],
                                         osem.at[1])

        def start_body(j, carry):
            zero_block(j).start()
            return carry

        def wait_body(j, carry):
            zero_block(j).wait()
            return carry
        lax.fori_loop(meta_ref[2 * N_EXPERTS], n_blocks, start_body, 0)
        lax.fori_loop(meta_ref[2 * N_EXPERTS], n_blocks, wait_body, 0)


def _moe(chunk_row, chunk_size, chunk_exp, weight_tbl, meta, xs, w1, b1, w2, b2):
    by_expert = lambda i, row, size, exp, wt, m: (exp[i], 0, 0)
    grid_spec = pltpu.PrefetchScalarGridSpec(
        num_scalar_prefetch=5,
        grid=(chunk_row.shape[0],),
        in_specs=[
            pl.BlockSpec(memory_space=pl.ANY),
            pl.BlockSpec(memory_space=pl.ANY),
            pl.BlockSpec((None, 1, 2 * D_FF), by_expert),
            pl.BlockSpec(memory_space=pl.ANY),
            pl.BlockSpec((None, 1, D_MODEL), by_expert),
        ],
        out_specs=pl.BlockSpec(memory_space=pl.ANY),
        scratch_shapes=[pltpu.VMEM((2, MOE_CH, PACK_W), U32), pltpu.VMEM((2, MOE_CH, PACK_W), U32),
                        pltpu.VMEM((2, D_MODEL, 2 * D_FF), F32), pltpu.VMEM((2, D_FF, D_MODEL), F32),
                        pltpu.VMEM((D_MODEL, 2 * D_FF), BF16), pltpu.VMEM((D_FF, D_MODEL), BF16),
                        pltpu.VMEM((MOE_CH, D_FF), BF16),
                        pltpu.SemaphoreType.DMA((2,)), pltpu.SemaphoreType.DMA((2,)), pltpu.SemaphoreType.DMA((2,))],
    )
    return pl.pallas_call(
        _moe_kernel,
        grid_spec=grid_spec,
        out_shape=jax.ShapeDtypeStruct(xs.shape, U32),
        compiler_params=pltpu.CompilerParams(dimension_semantics=("arbitrary",), vmem_limit_bytes=VMEM_LIMIT),
        name="moe",
    )(chunk_row, chunk_size, chunk_exp, weight_tbl, meta, xs, w1, b1, w2, b2)


def _combine_kernel(n_prompt_tiles, cnt_ref, off_ref, tot_ref, ys_hbm, route_ref, x1_ref, modp_ref, g2s_ref, fg_ref,
                    op_ref, os_ref, buf, sem):
    i = pl.program_id(0)
    nb = pl.num_programs(0)
    slot = i % 2

    def fetch(tile, s):
        _for_strips(cnt_ref, off_ref, tile, buf.at[s], ys_hbm, sem.at[s], False, lambda cp: cp.start())

    @pl.when(i == 0)
    def _():
        buf[...] = jnp.zeros_like(buf)
        fetch(0, 0)

    @pl.when(i + 1 < nb)
    def _():
        fetch(i + 1, 1 - slot)

    _wait_tile_rows(tot_ref[i], buf.at[slot], ys_hbm, sem.at[slot], False)

    route = route_ref[...]
    c_pos = lax.broadcasted_iota(jnp.int32, (SEQ_TILE, SORT_ROWS), 1).astype(F32)
    gmat = jnp.zeros((SEQ_TILE, SORT_ROWS), F32)
    for k in range(TOP_K):
        gmat = gmat + jnp.where(c_pos == route[:, ROUTE_R + k:ROUTE_R + k + 1],
                                route[:, ROUTE_G + k:ROUTE_G + k + 1], 0.0)
    g_hi = gmat.astype(BF16)
    g_lo = (gmat - g_hi.astype(F32)).astype(BF16)
    ff = jnp.concatenate(
        [jnp.dot(g_hi, yb, preferred_element_type=F32) + jnp.dot(g_lo, yb, preferred_element_type=F32)
         for yb in _unpack_bf16_pairs(buf[slot])], axis=1)
    g2 = jnp.where(i < n_prompt_tiles, modp_ref[5:6, :], g2s_ref[...])
    x = x1_ref[...] + g2 * ff
    y = _rms(x, fg_ref[...])

    @pl.when(i < n_prompt_tiles)
    def _():
        op_ref[...] = y

    @pl.when(i >= n_prompt_tiles)
    def _():
        os_ref[...] = y


def _combine(cnt8, tile_off, tot8, ys, route, x1_all, mod_p, g2_rows, final_g, n_prompt_tiles, tiles_per_batch):
    nt = tot8.shape[0]
    tt = SEQ_TILE
    npt = n_prompt_tiles
    grid_spec = pltpu.PrefetchScalarGridSpec(
        num_scalar_prefetch=3,
        grid=(nt,),
        in_specs=[
            pl.BlockSpec(memory_space=pl.ANY),
            pl.BlockSpec((tt, LANES), lambda i, *_: (i, 0)),
            pl.BlockSpec((tt, D_MODEL), lambda i, *_: (i, 0)),
            pl.BlockSpec((None, 6, D_MODEL), lambda i, *_: (jnp.minimum(i, npt - 1) // tiles_per_batch, 0, 0)),
            pl.BlockSpec((tt, D_MODEL), lambda i, *_: (jnp.maximum(i - npt, 0), 0)),
            pl.BlockSpec((1, D_MODEL), lambda i, *_: (0, 0)),
        ],
        out_specs=(pl.BlockSpec((tt, D_MODEL), lambda i, *_: (jnp.minimum(i, npt - 1), 0)),
                   pl.BlockSpec((tt, D_MODEL), lambda i, *_: (jnp.maximum(i - npt, 0), 0))),
        scratch_shapes=[pltpu.VMEM((2, SORT_ROWS, PACK_W), U32), pltpu.SemaphoreType.DMA((2,))],
    )
    return pl.pallas_call(
        functools.partial(_combine_kernel, npt),
        grid_spec=grid_spec,
        out_shape=(jax.ShapeDtypeStruct((npt * tt, D_MODEL), F32),
                   jax.ShapeDtypeStruct(((nt - npt) * tt, D_MODEL), F32)),
        compiler_params=pltpu.CompilerParams(dimension_semantics=("arbitrary",), vmem_limit_bytes=VMEM_LIMIT),
        name="combine",
    )(cnt8, tile_off, tot8, ys, route, x1_all, mod_p, g2_rows, final_g)


def _block_diag_halves(wg):
    halves = []
    for hh in range(2):
        rows = []
        for bi in range(4):
            row = [wg[hh * 4 + bi] if bj == bi else jnp.zeros((LRU_BLOCK_W, LRU_BLOCK_W), wg.dtype) for bj in range(4)]
            rows.append(jnp.concatenate(row, axis=1))
        halves.append(jnp.concatenate(rows, axis=0))
    return jnp.stack(halves).astype(BF16)


def _rope_tables(pos):
    half = HEAD_DIM // 2
    inv = ROPE_THETA ** (-jnp.arange(half, dtype=F32) / half)
    ang = pos.astype(F32)[:, None] * inv[None, :]
    cos = jnp.cos(ang)
    sin = jnp.sin(ang)
    cos128 = jnp.concatenate([cos, cos, cos, cos], axis=1)
    sin128 = jnp.concatenate([-sin, sin, -sin, sin], axis=1)
    return cos128, sin128


def _full_spec(arr, grid_rank):
    zeros = (0,) * arr.ndim
    if grid_rank == 1:
        return pl.BlockSpec(arr.shape, lambda i: zeros)
    return pl.BlockSpec(arr.shape, lambda b, j: zeros)


def kernel(x_prompt, x_sample, state_lru_h, state_conv, cache_win_k, cache_win_v, c_prompt, c_sample, w_ada, b_ada, norm1_g, w_in, b_in, conv_w, conv_b, lru_wa, lru_ba, lru_wx, lru_bx, lru_lambda, attn_sinks, w_out, b_out, norm2_g, w_router, b_router, w1, b1, w2, b2, final_g):
    bp, seq, _ = x_prompt.shape
    bd, tdec, _ = x_sample.shape
    assert tdec == SUBLANES and seq % SEQ_TILE == 0 and bd % SAMPLE_BT == 0
    n_prompt = bp * seq
    n_sample = bd * tdec
    n_tok = n_prompt + n_sample
    l = 0

    head_perm = [h for c in range(4) for h in (c, GROUP + c)]
    o2 = 2 * LRU_WIDTH

    def permute_heads(arr, axis, start):
        take = lambda a, b: lax.slice_in_dim(arr, a, b, axis=axis)
        heads = [take(start + h * HEAD_DIM, start + (h + 1) * HEAD_DIM) for h in head_perm]
        return jnp.concatenate([take(0, start)] + heads + [take(start + Q_WIDTH, arr.shape[axis])], axis=axis)

    sinks_perm = attn_sinks[l]

    wr = jnp.tile(w_router[l], (1, TOP_K))
    wr_hi = wr.astype(BF16)
    weights = dict(
        norm1_g=norm1_g[l][None, :], w_in=permute_heads(w_in[l], 1, o2).astype(BF16),
        b_in=permute_heads(b_in[l], 0, o2)[None, :],
        conv_w=conv_w[l], conv_b=conv_b[l][None, :],
        ga=_block_diag_halves(lru_wa[l]), gx=_block_diag_halves(lru_wx[l]),
        lru_ba=lru_ba[l][None, :], lru_bx=lru_bx[l][None, :], lam=lru_lambda[l][None, :],
        w_out=permute_heads(w_out[l], 0, LRU_WIDTH).astype(BF16), b_out=b_out[l][None, :], norm2_g=norm2_g[l][None, :],
        wr_hi=wr_hi, wr_lo=(wr - wr_hi.astype(F32)).astype(BF16),
        b_router=jnp.tile(b_router[l], TOP_K)[None, :],
    )
    wlist = [weights[n] for n in WEIGHT_NAMES]

    mod_all = _ada(jnp.concatenate([c_prompt, c_sample], axis=0), w_ada[l], b_ada[l][None, :])
    mod_p = mod_all[:bp].reshape(bp, 6, D_MODEL)
    mod_s = mod_all[bp:].reshape(bd, 6, D_MODEL).transpose(1, 0, 2)

    cos_p, sin_p = _rope_tables(jnp.arange(seq, dtype=jnp.int32))
    cos_s, sin_s = _rope_tables(PAST_LEN + jnp.arange(tdec, dtype=jnp.int32))
    cos_s = jnp.tile(cos_s, (SAMPLE_BT, 1))
    sin_s = jnp.tile(sin_s, (SAMPLE_BT, 1))
    h0_rows = jnp.pad(state_lru_h[l][:, None, :], ((0, 0), (0, tdec - 1), (0, 0))).reshape(n_sample, LRU_WIDTH)
    cprev_rows = jnp.pad(state_conv[l], ((0, 0), (0, tdec - (CONV_W - 1)), (0, 0))).reshape(n_sample, LRU_WIDTH)
    ck = cache_win_k[l].reshape(bd, WINDOW, KV_WIDTH)
    cv = cache_win_v[l].reshape(bd, WINDOW, KV_WIDTH)
    nj = seq // SEQ_TILE
    npt = n_prompt // SEQ_TILE
    nst = n_sample // SEQ_TILE
    p_tile = lambda i: jnp.minimum(i, npt - 1)
    s_tile = lambda i: jnp.maximum(i - npt, 0)
    all_rows = lambda width: pl.BlockSpec((SEQ_TILE, width), lambda i: (i, 0))
    p_rows = lambda width: pl.BlockSpec((SEQ_TILE, width), lambda i: (p_tile(i), 0))
    s_rows = lambda width: pl.BlockSpec((SEQ_TILE, width), lambda i: (s_tile(i), 0))
    p_tail = lambda rows, width: pl.BlockSpec((None, rows, width), lambda i: (p_tile(i) // nj, 0, 0))
    cache_spec = pl.BlockSpec((SAMPLE_BT, WINDOW, KV_WIDTH), lambda i: (s_tile(i), 0, 0))
    mixer_out_shapes = (
        jax.ShapeDtypeStruct((n_tok, D_MODEL), F32),
        jax.ShapeDtypeStruct((n_tok, D_MODEL), F32),
        jax.ShapeDtypeStruct((n_tok, LANES), F32),
        jax.ShapeDtypeStruct((npt + nst, SUBLANES, LANES), F32),
        jax.ShapeDtypeStruct((bp, SUBLANES, LRU_WIDTH), F32),
        jax.ShapeDtypeStruct((bp, SUBLANES, LRU_WIDTH), F32),
        jax.ShapeDtypeStruct((bp, WINDOW, KV_WIDTH), F32),
        jax.ShapeDtypeStruct((bp, WINDOW, KV_WIDTH), F32),
        jax.ShapeDtypeStruct((n_sample, D_MODEL), F32),
        jax.ShapeDtypeStruct((n_sample, LRU_WIDTH), F32),
        jax.ShapeDtypeStruct((n_sample, LRU_WIDTH), F32),
        jax.ShapeDtypeStruct((bd, WINDOW, KV_WIDTH), F32),
        jax.ShapeDtypeStruct((bd, WINDOW, KV_WIDTH), F32),
    )
    (x1_all, h2_all, route, tile_cnt, hlast_p, ulast_p, klast_p, vlast_p, g2_rows, hs_s, u_s, s_k, s_v) = pl.pallas_call(
        functools.partial(_mixer_kernel, npt, nj),
        grid=(npt + nst,),
        in_specs=[p_rows(D_MODEL),
                  pl.BlockSpec((None, 6, D_MODEL), lambda i: (p_tile(i) // nj, 0, 0)),
                  pl.BlockSpec((SEQ_TILE, LANES), lambda i: (p_tile(i) % nj, 0)),
                  pl.BlockSpec((SEQ_TILE, LANES), lambda i: (p_tile(i) % nj, 0)),
                  s_rows(D_MODEL),
                  pl.BlockSpec((6, SAMPLE_BT, D_MODEL), lambda i: (0, s_tile(i), 0)),
                  pl.BlockSpec((SEQ_TILE, LANES), lambda i: (0, 0)),
                  pl.BlockSpec((SEQ_TILE, LANES), lambda i: (0, 0)),
                  pl.BlockSpec(memory_space=pltpu.SMEM),
                  s_rows(LRU_WIDTH), s_rows(LRU_WIDTH), cache_spec, cache_spec]
                 + [_full_spec(a, 1) for a in wlist],
        out_specs=(all_rows(D_MODEL), all_rows(D_MODEL), all_rows(LANES),
                   pl.BlockSpec((None, SUBLANES, LANES), lambda i: (i, 0, 0)),
                   p_tail(SUBLANES, LRU_WIDTH), p_tail(SUBLANES, LRU_WIDTH),
                   p_tail(WINDOW, KV_WIDTH), p_tail(WINDOW, KV_WIDTH),
                   s_rows(D_MODEL), s_rows(LRU_WIDTH), s_rows(LRU_WIDTH), cache_spec, cache_spec),
        out_shape=mixer_out_shapes,
        scratch_shapes=[pltpu.VMEM((SUBLANES, LRU_WIDTH), F32), pltpu.VMEM((SUBLANES, LRU_WIDTH), F32),
                        pltpu.VMEM((WINDOW, KV_WIDTH), F32), pltpu.VMEM((WINDOW, KV_WIDTH), F32)],
        compiler_params=pltpu.CompilerParams(dimension_semantics=("arbitrary",), vmem_limit_bytes=VMEM_LIMIT),
        name="mixer",
    )(x_prompt.reshape(n_prompt, D_MODEL), mod_p, cos_p, sin_p,
      x_sample.reshape(n_sample, D_MODEL), mod_s, cos_s, sin_s, sinks_perm, h0_rows, cprev_rows, ck, cv, *wlist)

    n_tiles = npt + nst
    n_assign = n_tok * TOP_K
    max_rows = n_assign + n_tiles * N_EXPERTS * (SUBLANES - 1) + N_EXPERTS * (MOE_TM - 1)
    n_blocks = -(-max_rows // MOE_TM)
    cnt = tile_cnt[:, 0, :N_EXPERTS].astype(jnp.int32)
    cnt8 = (cnt + SUBLANES - 1) // SUBLANES * SUBLANES
    counts = jnp.sum(cnt8, axis=0)
    pcounts = (counts + MOE_TM - 1) // MOE_TM * MOE_TM
    pend = jnp.cumsum(pcounts)
    pstart = pend - pcounts
    tile_off = pstart[None, :] + jnp.cumsum(cnt8, axis=0) - cnt8
    tot8 = jnp.sum(cnt8, axis=1)
    meta = jnp.concatenate([pstart + counts, pend, pend[-1:] // MOE_TM]).astype(jnp.int32)
    cnt8_flat = cnt8.reshape(-1)
    off_flat = tile_off.reshape(-1).astype(jnp.int32)

    xs = _dispatch(cnt8_flat, off_flat, tot8, meta, h2_all, route, n_blocks * MOE_TM)

    n_tm = pcounts // MOE_TM
    n_big = n_tm // 4
    has2 = (n_tm // 2) % 2
    n_ch = n_big + has2 + n_tm % 2
    ch_end = jnp.cumsum(n_ch)
    ch_start = ch_end - n_ch
    n_chunks = n_blocks // 4 + 2 * N_EXPERTS
    ci = jnp.arange(n_chunks, dtype=jnp.int32)
    owner = jnp.logical_and(ch_start[None, :] <= ci[:, None], ci[:, None] < ch_end[None, :])
    pick = lambda v: jnp.sum(jnp.where(owner, v[None, :], 0), axis=1)
    live = ci < ch_end[-1]
    local = ci - pick(ch_start)
    chunk_exp = jnp.where(live, pick(jnp.arange(N_EXPERTS, dtype=jnp.int32)), N_EXPERTS - 1).astype(jnp.int32)
    big_c, has2_c = pick(n_big), pick(has2)
    is_big = local < big_c
    is_two = jnp.logical_and(local == big_c, has2_c == 1)
    blocks_before = jnp.where(is_big, local * 4, big_c * 4 + jnp.where(is_two, 0, has2_c * 2))
    chunk_row = jnp.where(live, pick(pstart) + blocks_before * MOE_TM, 0).astype(jnp.int32)
    chunk_size = jnp.where(live, jnp.where(is_big, 4, jnp.where(is_two, 2, 1)), 0).astype(jnp.int32)
    has_rows = n_ch > 0
    e_ar = jnp.arange(N_EXPERTS, dtype=jnp.int32)
    w_slot = (jnp.cumsum(has_rows.astype(jnp.int32)) - 1) % 2
    later = jnp.logical_and(has_rows[None, :], e_ar[None, :] > e_ar[:, None])
    nxt = jnp.min(jnp.where(later, e_ar[None, :], N_EXPERTS), axis=1)
    nxt = jnp.where(nxt == N_EXPERTS, -1, nxt)
    weight_tbl = jnp.concatenate([jnp.logical_and(live, local == 0).astype(jnp.int32),
                                  pick(w_slot), jnp.where(live, pick(nxt), -1)]).astype(jnp.int32)
    ys = _moe(chunk_row, chunk_size, chunk_exp, weight_tbl, meta, xs, w1[l], b1[l][:, None, :], w2[l], b2[l][:, None, :])

    y_p, y_s = _combine(cnt8_flat, off_flat, tot8, ys, route, x1_all, mod_p, g2_rows,
                        final_g[None, :], n_prompt // SEQ_TILE, seq // SEQ_TILE)

    y_prompt = y_p.reshape(bp, seq, D_MODEL)
    y_sample = y_s.reshape(bd, tdec, D_MODEL)
    p_h = hlast_p[:, SUBLANES - 1, :][None]
    p_c = ulast_p[:, SUBLANES - (CONV_W - 1):, :][None]
    p_k = klast_p.reshape(1, bp, WINDOW, N_KV_HEADS, HEAD_DIM)
    p_v = vlast_p.reshape(1, bp, WINDOW, N_KV_HEADS, HEAD_DIM)
    s_h = hs_s.reshape(bd, tdec, LRU_WIDTH)[:, tdec - 1, :][None]
    s_c = u_s.reshape(bd, tdec, LRU_WIDTH)[:, tdec - (CONV_W - 1):, :][None]
    s_kk = s_k.reshape(1, bd, WINDOW, N_KV_HEADS, HEAD_DIM)
    s_vv = s_v.reshape(1, bd, WINDOW, N_KV_HEADS, HEAD_DIM)
    return (y_prompt, y_sample, p_h, p_c, p_k, p_v, s_h, s_c, s_kk, s_vv)
```

```python
import functools

import jax
import jax.numpy as jnp
from jax import lax
from jax.experimental import pallas as pl
from jax.experimental.pallas import tpu as pltpu

F32 = jnp.float32
BF16 = jnp.bfloat16

D_MODEL = 1024
LRU_WIDTH = 512
LRU_BLOCKS = 8
LRU_BLOCK_W = LRU_WIDTH // LRU_BLOCKS
CONV_W = 4
LRU_C = 8.0
HEAD_DIM = 64
N_HEADS = 8
N_KV_HEADS = 2
GROUP = N_HEADS // N_KV_HEADS
WINDOW = 128
ROPE_THETA = 10000.0
N_EXPERTS = 32
TOP_K = 4
D_FF = D_MODEL
SWIGLU_LIMIT = 7.0
SWIGLU_ALPHA = 1.702
NORM_EPS = 1e-5
PAST_LEN = 8192
Q_WIDTH = N_HEADS * HEAD_DIM
KV_WIDTH = N_KV_HEADS * HEAD_DIM
IN_WIDTH = 2 * LRU_WIDTH + Q_WIDTH + 2 * KV_WIDTH

LANES = 128
SUBLANES = 8
SEQ_TILE = 256
MIX_TILES = 2
SAMPLE_BT = 32
MOE_TM = 256
MOE_CHUNK_CODES = (4, 2, 1)
MOE_CH = MOE_CHUNK_CODES[0] * MOE_TM
NEG_BIG = -1e30
VMEM_LIMIT = 56 * 1024 * 1024


def _rms(x, g):
    return x * lax.rsqrt(jnp.mean(x * x, axis=-1, keepdims=True) + NORM_EPS) * g


def _rms_mod(x, gain, shift):
    return x * lax.rsqrt(jnp.mean(x * x, axis=-1, keepdims=True) + NORM_EPS) * gain + shift


def _group_scan(a, b):
    rows, width = a.shape
    groups = rows // SUBLANES
    a3 = a.reshape(groups, SUBLANES, width)
    b3 = b.reshape(groups, SUBLANES, width)
    t = lax.broadcasted_iota(jnp.int32, (1, SUBLANES, 1), 1)
    d = 1
    while d < SUBLANES:
        keep = t >= d
        a_s = jnp.where(keep, pltpu.roll(a3, d, axis=1), 1.0)
        b_s = jnp.where(keep, pltpu.roll(b3, d, axis=1), 0.0)
        b3 = a3 * b_s + b3
        a3 = a3 * a_s
        d *= 2
    return a3.reshape(rows, width), b3.reshape(rows, width)


def _chain_groups(a_grp, b_grp, h_in):
    rows = a_grp.shape[0]
    out = []
    carry = h_in
    for g in range(rows // SUBLANES):
        sl = slice(g * SUBLANES, (g + 1) * SUBLANES)
        hg = b_grp[sl] + a_grp[sl] * carry
        out.append(hg)
        carry = hg[SUBLANES - 1:SUBLANES]
    return jnp.concatenate(out, axis=0)


def _rope128(x, cos, sin_signed, first_half):
    sw = jnp.where(first_half, pltpu.roll(x, LANES - HEAD_DIM // 2, axis=1), pltpu.roll(x, HEAD_DIM // 2, axis=1))
    return x * cos + sw * sin_signed


def _softplus(x):
    return jnp.maximum(x, 0.0) + jnp.log1p(jnp.exp(-jnp.abs(x)))


def _lru_coeffs(uc, w, first_pos_mask):
    ub = uc.astype(BF16)
    half = LRU_WIDTH // 2
    ra = jnp.concatenate([jnp.dot(ub[:, :half], w['ga'][0], preferred_element_type=F32),
                          jnp.dot(ub[:, half:], w['ga'][1], preferred_element_type=F32)], axis=1)
    rx = jnp.concatenate([jnp.dot(ub[:, :half], w['gx'][0], preferred_element_type=F32),
                          jnp.dot(ub[:, half:], w['gx'][1], preferred_element_type=F32)], axis=1)
    r = jax.nn.sigmoid(ra + w['lru_ba'][...])
    i = jax.nn.sigmoid(rx + w['lru_bx'][...])
    log_a = -LRU_C * r * _softplus(-w['lam'][...])
    a = jnp.exp(log_a)
    om = 1.0 - a * a
    mult = jnp.where(om > 0.0, om * lax.rsqrt(om), 0.0)
    if first_pos_mask is not None:
        mult = jnp.where(first_pos_mask, 1.0, mult)
    return a, mult * i * uc


def _conv_taps(u, s1, s2, s3, w):
    cw = w['conv_w']
    return w['conv_b'][...] + s3 * cw[0:1, :] + s2 * cw[1:2, :] + s1 * cw[2:3, :] + u * cw[3:4, :]


def _in_proj(x, mod, w):
    sh1, sc1 = mod
    h = _rms_mod(x, w['norm1_g'][...] * (1.0 + sc1), sh1)
    return jnp.dot(h.astype(BF16), w['w_in'][...], preferred_element_type=F32) + w['b_in'][...]


def _post_mix(x, mix, mod, w):
    g1, sh2, sc2 = mod
    x1 = x + g1 * (jnp.dot(mix.astype(BF16), w['w_out'][...], preferred_element_type=F32) + w['b_out'][...])
    h2 = _rms_mod(x1, w['norm2_g'][...] * (1.0 + sc2), sh2)
    h2_hi = h2.astype(BF16)
    h2_lo = (h2 - h2_hi.astype(F32)).astype(BF16)
    logits = (jnp.dot(h2_hi, w['wr_hi'][...], preferred_element_type=F32)
              + jnp.dot(h2_lo, w['wr_hi'][...], preferred_element_type=F32)
              + jnp.dot(h2_hi, w['wr_lo'][...], preferred_element_type=F32)) + w['b_router'][...]
    return x1, h2, logits


ROUTE_E, ROUTE_G, ROUTE_R = 0, TOP_K, 2 * TOP_K


def _lane_roll1(v, shift):
    return pltpu.roll(jnp.broadcast_to(v, (SUBLANES, LANES)), shift, axis=1)[0:1]


def _route_tile(lg):
    rows = lg.shape[0]
    lane = lax.broadcasted_iota(jnp.int32, (1, LANES), 1)
    e_of = lane % N_EXPERTS
    grp = lane // N_EXPERTS
    onehot = jnp.zeros((rows, LANES), F32)
    vals, ids = [], []
    for k in range(TOP_K):
        m = jnp.max(lg, axis=1, keepdims=True)
        idx = jnp.min(jnp.where(lg == m, e_of, N_EXPERTS), axis=1, keepdims=True)
        sel = e_of == idx
        lg = jnp.where(sel, -jnp.inf, lg)
        onehot = jnp.where(jnp.logical_and(sel, grp == k), 1.0, onehot)
        vals.append(m)
        ids.append(idx)
    ex = [jnp.exp(v - vals[0]) for v in vals]
    denom = ex[0] + ex[1] + ex[2] + ex[3]

    r_i = lax.broadcasted_iota(jnp.int32, (rows, rows), 0)
    c_i = lax.broadcasted_iota(jnp.int32, (rows, rows), 1)
    strict_lower = jnp.where(r_i > c_i, 1.0, 0.0).astype(BF16)
    prefix = jnp.dot(strict_lower, onehot.astype(BF16), preferred_element_type=F32)
    cnt = jnp.sum(onehot, axis=0, keepdims=True)
    base = jnp.zeros((1, LANES), F32)
    tot = cnt
    for s in range(1, TOP_K):
        rolled = _lane_roll1(cnt, s * N_EXPERTS)
        base = base + jnp.where(lane >= s * N_EXPERTS, rolled, 0.0)
        tot = tot + rolled
    pad_cnt = jnp.floor((tot + (SUBLANES - 1.0)) * (1.0 / SUBLANES)) * SUBLANES
    inc = pad_cnt
    d = 1
    while d < N_EXPERTS:
        inc = inc + jnp.where(e_of >= d, _lane_roll1(inc, d), 0.0)
        d *= 2
    strip_start = inc - pad_cnt
    ranked = onehot * (prefix + base + strip_start)

    route = jnp.zeros((rows, LANES), F32)
    for k in range(TOP_K):
        rank_k = jnp.sum(jnp.where(grp == k, ranked, 0.0), axis=1, keepdims=True)
        route = jnp.where(lane == ROUTE_E + k, ids[k].astype(F32), route)
        route = jnp.where(lane == ROUTE_G + k, ex[k] / denom, route)
        route = jnp.where(lane == ROUTE_R + k, rank_k, route)
    return route, tot


def _softmax_sink_pv(s, sink_col, v_bf16):
    m = jnp.maximum(jnp.max(s, axis=-1, keepdims=True), sink_col)
    p = jnp.exp(s - m)
    denom = jnp.sum(p, axis=-1, keepdims=True) + jnp.exp(sink_col - m)
    return p, denom


WEIGHT_NAMES = ('norm1_g', 'w_in', 'b_in', 'conv_w', 'conv_b', 'ga', 'gx', 'lru_ba', 'lru_bx', 'lam',
                'w_out', 'b_out', 'norm2_g', 'wr_hi', 'wr_lo', 'b_router')


def _ada_kernel(c_ref, w_ref, b_ref, o_ref):
    c = c_ref[...]
    s = c * jax.nn.sigmoid(c)
    o_ref[...] = jnp.dot(s, w_ref[...], preferred_element_type=F32, precision=lax.Precision.HIGHEST) + b_ref[...]


def _ada(c_all, w_ada, b_ada):
    rows = c_all.shape[0]
    return pl.pallas_call(
        _ada_kernel,
        grid=(6,),
        in_specs=[pl.BlockSpec((rows, D_MODEL), lambda i: (0, 0)),
                  pl.BlockSpec((D_MODEL, D_MODEL), lambda i: (0, i)),
                  pl.BlockSpec((1, D_MODEL), lambda i: (0, i))],
        out_specs=pl.BlockSpec((rows, D_MODEL), lambda i: (0, i)),
        out_shape=jax.ShapeDtypeStruct((rows, 6 * D_MODEL), F32),
        compiler_params=pltpu.CompilerParams(dimension_semantics=("arbitrary",), vmem_limit_bytes=VMEM_LIMIT),
        name="ada",
    )(c_all, w_ada, b_ada)


def _prompt_body(seq_start, x_ref, mod_ref, cos_ref, sin_ref, sinks_ref, w, x1_ref, h2_ref, lg_ref,
                 hlast_ref, ulast_ref, klast_ref, vlast_ref, conv_c, h_c, k_c, v_c):
    ts = SEQ_TILE

    if seq_start is not None:
        @pl.when(seq_start)
        def _():
            conv_c[...] = jnp.zeros_like(conv_c)
            h_c[...] = jnp.zeros_like(h_c)
            k_c[...] = jnp.zeros_like(k_c)
            v_c[...] = jnp.zeros_like(v_c)

    x = x_ref[...]
    mod = mod_ref[...]
    proj = _in_proj(x, (mod[0:1], mod[1:2]), w)
    u = proj[:, :LRU_WIDTH]
    gate = proj[:, LRU_WIDTH:2 * LRU_WIDTH]
    o2 = 2 * LRU_WIDTH

    rowid = lax.broadcasted_iota(jnp.int32, (ts, 1), 0)
    u_ext = jnp.concatenate([conv_c[...], u], axis=0)
    s1, s2, s3 = (pltpu.roll(u_ext, d, axis=0)[SUBLANES:] for d in (1, 2, 3))
    uc = _conv_taps(u, s1, s2, s3, w)
    conv_c[...] = u[ts - SUBLANES:]
    ulast_ref[...] = u[ts - SUBLANES:]

    first_pos = None if seq_start is None else jnp.logical_and(rowid == 0, seq_start)
    a, bt = _lru_coeffs(uc, w, first_pos)
    hs = _chain_groups(*_group_scan(a, bt), h_c[0:1, :])
    h_tail = hs[ts - SUBLANES:]
    h_c[...] = jnp.broadcast_to(h_tail[SUBLANES - 1:SUBLANES, :], h_c.shape)
    hlast_ref[...] = h_tail
    lru_out = hs * jax.nn.gelu(gate)

    cos = cos_ref[...]
    sin = sin_ref[...]
    lane = lax.broadcasted_iota(jnp.int32, (1, LANES), 1)
    first_half = (lane % HEAD_DIM) < (HEAD_DIM // 2)
    qcols = [_rope128(proj[:, o2 + c * LANES:o2 + (c + 1) * LANES], cos, sin, first_half) * (HEAD_DIM ** -0.5)
             for c in range(4)]
    k = _rope128(proj[:, o2 + Q_WIDTH:o2 + Q_WIDTH + KV_WIDTH], cos, sin, first_half)
    v = proj[:, o2 + Q_WIDTH + KV_WIDTH:]
    k_ext = jnp.concatenate([k_c[...], k], axis=0).astype(BF16)
    v_ext = jnp.concatenate([v_c[...], v], axis=0).astype(BF16)
    k_c[...] = k[ts - WINDOW:]
    v_c[...] = v[ts - WINDOW:]
    klast_ref[...] = k[ts - WINDOW:]
    vlast_ref[...] = v[ts - WINDOW:]

    qi = lax.broadcasted_iota(jnp.int32, (WINDOW, 2 * WINDOW), 0)
    kj = lax.broadcasted_iota(jnp.int32, (WINDOW, 2 * WINDOW), 1)
    band = jnp.logical_and(kj > qi, kj <= qi + WINDOW)
    lane_lo = lane < HEAD_DIM
    grow = lax.broadcasted_iota(jnp.int32, (GROUP * WINDOW, 1), 0) // WINDOW
    attn_cols = [[] for _ in range(4)]
    for blk in range(ts // WINDOW):
        if seq_start is None or blk > 0:
            mask = band
        else:
            mask = jnp.logical_and(band, jnp.logical_or(kj >= WINDOW, jnp.logical_not(seq_start)))
        mask4 = jnp.concatenate([mask] * GROUP, axis=0)
        kb = k_ext[blk * WINDOW:(blk + 2) * WINDOW]
        vb = v_ext[blk * WINDOW:(blk + 2) * WINDOW]
        outs = []
        for kv in range(N_KV_HEADS):
            sel = lane_lo if kv == 0 else jnp.logical_not(lane_lo)
            qs = jnp.concatenate(
                [jnp.where(sel, qc[blk * WINDOW:(blk + 1) * WINDOW], 0.0) for qc in qcols], axis=0).astype(BF16)
            s = lax.dot_general(qs, kb, (((1,), (1,)), ((), ())), preferred_element_type=F32)
            s = jnp.where(mask4, s, NEG_BIG)
            sink_col = jnp.zeros((GROUP * WINDOW, 1), F32)
            for g in range(GROUP):
                sink_col = jnp.where(grow == g, sinks_ref[kv * GROUP + g], sink_col)
            p, denom = _softmax_sink_pv(s, sink_col, vb)
            outs.append(jnp.dot(p.astype(BF16), vb, preferred_element_type=F32) / denom)
        for c in range(4):
            attn_cols[c].append(jnp.where(lane_lo, outs[0][c * WINDOW:(c + 1) * WINDOW],
                                          outs[1][c * WINDOW:(c + 1) * WINDOW]))
    attn = jnp.concatenate([jnp.concatenate(cols, axis=0) for cols in attn_cols], axis=1)

    mix = jnp.concatenate([lru_out, attn], axis=1)
    x1, h2, logits = _post_mix(x, mix, (mod[2:3], mod[3:4], mod[4:5]), w)
    x1_ref[...] = x1
    h2_ref[...] = h2
    route, tot = _route_tile(logits)
    lg_ref[0][...] = route
    lg_ref[1][...] = jnp.broadcast_to(tot, lg_ref[1].shape)


def _expand_rows(m, t):
    b, wd = m.shape
    return jnp.broadcast_to(m[:, None, :], (b, t, wd)).reshape(b * t, wd)


def _sample_body(x_ref, mod_ref, cos_ref, sin_ref, sinks_ref, h0_ref, cprev_ref, ck_ref, cv_ref, w,
                 x1_ref, h2_ref, lg_ref, g2_ref, hs_ref, u_ref, ko_ref, vo_ref):
    bt_, t = SAMPLE_BT, SUBLANES
    rows = bt_ * t

    x = x_ref[...]
    mods = [_expand_rows(mod_ref[i], t) for i in range(6)]
    proj = _in_proj(x, (mods[0], mods[1]), w)
    u = proj[:, :LRU_WIDTH]
    gate = proj[:, LRU_WIDTH:2 * LRU_WIDTH]
    o2 = 2 * LRU_WIDTH
    u_ref[...] = u

    rowid = lax.broadcasted_iota(jnp.int32, (rows, 1), 0) % t
    cprev = cprev_ref[...]
    taps = []
    for d in (1, 2, 3):
        taps.append(jnp.where(rowid >= d, pltpu.roll(u, d, axis=0),
                              pltpu.roll(cprev, (d - (CONV_W - 1)) % rows, axis=0)))
    uc = _conv_taps(u, taps[0], taps[1], taps[2], w)

    a, bt = _lru_coeffs(uc, w, None)
    bt = bt + a * h0_ref[...]
    _, hs = _group_scan(a, bt)
    hs_ref[...] = hs
    lru_out = hs * jax.nn.gelu(gate)

    cos = cos_ref[...]
    sin = sin_ref[...]
    lane = lax.broadcasted_iota(jnp.int32, (1, LANES), 1)
    first_half = (lane % HEAD_DIM) < (HEAD_DIM // 2)
    qcols = [_rope128(proj[:, o2 + c * LANES:o2 + (c + 1) * LANES], cos, sin, first_half) * (HEAD_DIM ** -0.5)
             for c in range(4)]
    k = _rope128(proj[:, o2 + Q_WIDTH:o2 + Q_WIDTH + KV_WIDTH], cos, sin, first_half)
    v = proj[:, o2 + Q_WIDTH + KV_WIDTH:]
    k3 = k.reshape(bt_, t, KV_WIDTH)
    v3 = v.reshape(bt_, t, KV_WIDTH)
    ck = ck_ref[...]
    cv = cv_ref[...]
    ko_ref[:, :WINDOW - t, :] = ck[:, t:, :]
    ko_ref[:, WINDOW - t:, :] = k3
    vo_ref[:, :WINDOW - t, :] = cv[:, t:, :]
    vo_ref[:, WINDOW - t:, :] = v3

    ckb, cvb, k3b, v3b = ck.astype(BF16), cv.astype(BF16), k3.astype(BF16), v3.astype(BF16)
    lane_lo = lane < HEAD_DIM
    gq = GROUP * t
    tq = lax.broadcasted_iota(jnp.int32, (1, gq, 1), 1) % t
    mask_c = lax.broadcasted_iota(jnp.int32, (1, gq, WINDOW), 2) > tq
    mask_n = lax.broadcasted_iota(jnp.int32, (1, gq, t), 2) <= tq
    grow = lax.broadcasted_iota(jnp.int32, (1, gq, 1), 1) // t
    bdims = (((2,), (2,)), ((0,), (0,)))
    pdims = (((2,), (1,)), ((0,), (0,)))
    outs = []
    for kv in range(N_KV_HEADS):
        sel = lane_lo if kv == 0 else jnp.logical_not(lane_lo)
        q3 = jnp.concatenate([jnp.where(sel, qc, 0.0).reshape(bt_, t, LANES) for qc in qcols], axis=1).astype(BF16)
        sc = lax.dot_general(q3, ckb, bdims, preferred_element_type=F32)
        sn = lax.dot_general(q3, k3b, bdims, preferred_element_type=F32)
        sc = jnp.where(mask_c, sc, NEG_BIG)
        sn = jnp.where(mask_n, sn, NEG_BIG)
        sink_col = jnp.zeros((1, gq, 1), F32)
        for g in range(GROUP):
            sink_col = jnp.where(grow == g, sinks_ref[kv * GROUP + g], sink_col)
        m = jnp.maximum(jnp.maximum(jnp.max(sc, axis=-1, keepdims=True), jnp.max(sn, axis=-1, keepdims=True)),
                        sink_col)
        pc = jnp.exp(sc - m)
        pn = jnp.exp(sn - m)
        denom = jnp.sum(pc, axis=-1, keepdims=True) + jnp.sum(pn, axis=-1, keepdims=True) + jnp.exp(sink_col - m)
        o = (lax.dot_general(pc.astype(BF16), cvb, pdims, preferred_element_type=F32)
             + lax.dot_general(pn.astype(BF16), v3b, pdims, preferred_element_type=F32)) / denom
        outs.append(o)
    attn = jnp.concatenate(
        [jnp.where(lane_lo, outs[0][:, c * t:(c + 1) * t, :], outs[1][:, c * t:(c + 1) * t, :]).reshape(rows, LANES)
         for c in range(4)], axis=1)

    mix = jnp.concatenate([lru_out, attn], axis=1)
    x1, h2, logits = _post_mix(x, mix, (mods[2], mods[3], mods[4]), w)
    x1_ref[...] = x1
    h2_ref[...] = h2
    route, tot = _route_tile(logits)
    lg_ref[0][...] = route
    lg_ref[1][...] = jnp.broadcast_to(tot, lg_ref[1].shape)
    g2_ref[...] = mods[5]


def _prompt_kernel(steps_per_seq, x_ref, mod_ref, cos_ref, sin_ref, sinks_ref, *rest):
    nw = len(WEIGHT_NAMES)
    w = dict(zip(WEIGHT_NAMES, rest[:nw]))
    (x1_ref, h2_ref, route_ref, cnt_ref, hlast_ref, ulast_ref, klast_ref, vlast_ref,
     conv_c, h_c, k_c, v_c) = rest[nw:]
    seq_start = pl.program_id(0) % steps_per_seq == 0
    for sub in range(MIX_TILES):
        rows = pl.ds(sub * SEQ_TILE, SEQ_TILE)
        _prompt_body(seq_start if sub == 0 else None, x_ref.at[rows], mod_ref, cos_ref.at[rows], sin_ref.at[rows],
                     sinks_ref, w, x1_ref.at[rows], h2_ref.at[rows], (route_ref.at[rows], cnt_ref.at[sub]),
                     hlast_ref, ulast_ref, klast_ref, vlast_ref, conv_c, h_c, k_c, v_c)


def _sample_kernel(x_ref, mod_ref, cos_ref, sin_ref, sinks_ref, h0_ref, cprev_ref, ck_ref, cv_ref, *rest):
    nw = len(WEIGHT_NAMES)
    w = dict(zip(WEIGHT_NAMES, rest[:nw]))
    x1_ref, h2_ref, route_ref, cnt_ref, g2_ref, hs_ref, u_ref, ko_ref, vo_ref = rest[nw:]
    _sample_body(x_ref, mod_ref, cos_ref, sin_ref, sinks_ref, h0_ref, cprev_ref, ck_ref, cv_ref, w,
                 x1_ref, h2_ref, (route_ref, cnt_ref), g2_ref, hs_ref, u_ref, ko_ref, vo_ref)


PACK_W = D_MODEL // 2
U32 = jnp.uint32


def _pack_bf16_pairs(x):
    xb = x.astype(BF16).astype(F32)
    lo = lax.bitcast_convert_type(xb[:, :PACK_W], U32)
    hi = lax.bitcast_convert_type(xb[:, PACK_W:], U32)
    return jnp.bitwise_or(jnp.bitwise_and(hi, jnp.uint32(0xFFFF0000)), lax.shift_right_logical(lo, jnp.uint32(16)))


def _unpack_bf16_pairs(w):
    lo = lax.bitcast_convert_type(lax.shift_left(w, jnp.uint32(16)), F32).astype(BF16)
    hi = lax.bitcast_convert_type(jnp.bitwise_and(w, jnp.uint32(0xFFFF0000)), F32).astype(BF16)
    return lo, hi


STRIP_SIZES = tuple(SUBLANES << b for b in range(6))
SORT_ROWS = SEQ_TILE * TOP_K + N_EXPERTS * SUBLANES
TILE_WAIT_SIZES = tuple(SUBLANES << b for b in range(8))


def _for_strips(cnt_ref, off_ref, tile, buf_slot, hbm, sem, to_hbm, act):
    def e_body(e, local):
        n = cnt_ref[tile * N_EXPERTS + e]
        glob = off_ref[tile * N_EXPERTS + e]
        done = 0
        for p in reversed(STRIP_SIZES):
            piece = n & p
            lo = pl.ds(pl.multiple_of(local + done, SUBLANES), p)
            gl = pl.ds(pl.multiple_of(glob + done, SUBLANES), p)

            @pl.when(piece != 0)
            def _():
                if to_hbm:
                    act(pltpu.make_async_copy(buf_slot.at[lo], hbm.at[gl], sem))
                else:
                    act(pltpu.make_async_copy(hbm.at[gl], buf_slot.at[lo], sem))
            done = done + piece
        return local + n
    lax.fori_loop(0, N_EXPERTS, e_body, 0)


def _wait_tile_rows(total, buf_slot, hbm, sem, to_hbm):
    for p in TILE_WAIT_SIZES:
        @pl.when((total & p) != 0)
        def _():
            if to_hbm:
                pltpu.make_async_copy(buf_slot.at[pl.ds(0, p)], hbm.at[pl.ds(0, p)], sem).wait()
            else:
                pltpu.make_async_copy(hbm.at[pl.ds(0, p)], buf_slot.at[pl.ds(0, p)], sem).wait()


def _dispatch_kernel(n_prompt_tiles, cnt_ref, off_ref, tot_ref, meta_ref, h2p_ref, h2s_ref, routep_ref, routes_ref,
                     xs_hbm, sbuf, zblk, sem, zsem):
    i = pl.program_id(0)
    nb = pl.num_programs(0)
    slot = i % 2
    n_blocks = xs_hbm.shape[0] // MOE_TM

    is_prompt = i < n_prompt_tiles
    h2 = jnp.where(is_prompt, h2p_ref[...], h2s_ref[...])
    route_t = jnp.where(is_prompt, routep_ref[...], routes_ref[...]).T
    r_pos = lax.broadcasted_iota(jnp.int32, (SORT_ROWS, SEQ_TILE), 0).astype(F32)
    perm = jnp.zeros((SORT_ROWS, SEQ_TILE), F32)
    for k in range(TOP_K):
        perm = perm + jnp.where(r_pos == route_t[ROUTE_R + k:ROUTE_R + k + 1, :], 1.0, 0.0)
    sbuf[slot] = _pack_bf16_pairs(jnp.dot(perm.astype(BF16), h2.astype(BF16), preferred_element_type=F32))

    _for_strips(cnt_ref, off_ref, i, sbuf.at[slot], xs_hbm, sem.at[slot], True, lambda cp: cp.start())

    @pl.when(i > 0)
    def _():
        _wait_tile_rows(tot_ref[jnp.maximum(i - 1, 0)], sbuf.at[1 - slot], xs_hbm, sem.at[1 - slot], True)

    @pl.when(i == nb - 1)
    def _():
        _wait_tile_rows(tot_ref[i], sbuf.at[slot], xs_hbm, sem.at[slot], True)
        zblk[...] = jnp.zeros_like(zblk)

        def for_region_tails(act):
            def e_body(e, carry):
                start = meta_ref[e]
                n = meta_ref[N_EXPERTS + e] - start
                done = 0
                for p in reversed(STRIP_SIZES[:-1]):
                    piece = n & p
                    rows = pl.ds(pl.multiple_of(start + done, SUBLANES), p)

                    @pl.when(piece != 0)
                    def _():
                        act(pltpu.make_async_copy(zblk.at[pl.ds(0, p)], xs_hbm.at[rows], zsem.at[0]))
                    done = done + piece
                return carry
            lax.fori_loop(0, N_EXPERTS, e_body, 0)

        def for_tail_blocks(act):
            def b_body(j, carry):
                act(pltpu.make_async_copy(zblk, xs_hbm.at[pl.ds(pl.multiple_of(j * MOE_TM, MOE_TM), MOE_TM)],
                                          zsem.at[0]))
                return carry
            lax.fori_loop(meta_ref[2 * N_EXPERTS], n_blocks, b_body, 0)

        for_region_tails(lambda cp: cp.start())
        for_tail_blocks(lambda cp: cp.start())
        for_region_tails(lambda cp: cp.wait())
        for_tail_blocks(lambda cp: cp.wait())


def _dispatch(cnt8, tile_off, tot8, meta, h2_p, h2_s, route_p, route_s, n_rows):
    npt = h2_p.shape[0] // SEQ_TILE
    p_tile = lambda i, *_: (jnp.minimum(i, npt - 1), 0)
    s_tile = lambda i, *_: (jnp.maximum(i - npt, 0), 0)
    nt = tot8.shape[0]
    tt = SEQ_TILE
    grid_spec = pltpu.PrefetchScalarGridSpec(
        num_scalar_prefetch=4,
        grid=(nt,),
        in_specs=[pl.BlockSpec((tt, D_MODEL), p_tile), pl.BlockSpec((tt, D_MODEL), s_tile),
                  pl.BlockSpec((tt, LANES), p_tile), pl.BlockSpec((tt, LANES), s_tile)],
        out_specs=pl.BlockSpec(memory_space=pl.ANY),
        scratch_shapes=[pltpu.VMEM((2, SORT_ROWS, PACK_W), U32), pltpu.VMEM((MOE_TM, PACK_W), U32),
                        pltpu.SemaphoreType.DMA((2,)), pltpu.SemaphoreType.DMA((1,))],
    )
    return pl.pallas_call(
        functools.partial(_dispatch_kernel, npt),
        grid_spec=grid_spec,
        out_shape=jax.ShapeDtypeStruct((n_rows, PACK_W), U32),
        compiler_params=pltpu.CompilerParams(dimension_semantics=("arbitrary",), vmem_limit_bytes=VMEM_LIMIT),
        name="dispatch",
    )(cnt8, tile_off, tot8, meta, h2_p, h2_s, route_p, route_s)


def _expert_mlp(words, w1b, b1_ref, w2b, b2_ref, act_ref):
    xb = jnp.concatenate(_unpack_bf16_pairs(words), axis=1)
    q = D_FF // 4
    for c in range(4):
        zg = jnp.dot(xb, w1b[:, c * q:(c + 1) * q], preferred_element_type=F32) + b1_ref[:, c * q:(c + 1) * q]
        zl = (jnp.dot(xb, w1b[:, D_FF + c * q:D_FF + (c + 1) * q], preferred_element_type=F32)
              + b1_ref[:, D_FF + c * q:D_FF + (c + 1) * q])
        glu = jnp.minimum(zg, SWIGLU_LIMIT)
        lin = jnp.clip(zl, -SWIGLU_LIMIT, SWIGLU_LIMIT)
        act_ref[:, c * q:(c + 1) * q] = (glu * jax.nn.sigmoid(SWIGLU_ALPHA * glu) * (lin + 1.0)).astype(BF16)
    return _pack_bf16_pairs(jnp.dot(act_ref[...], w2b[...], preferred_element_type=F32) + b2_ref[...])


def _moe_kernel(row_ref, size_ref, exp_ref, wt_ref, meta_ref, xs_hbm, w1_hbm, b1_ref, w2_hbm, b2_ref, ys_hbm,
                xin, yout, w1f, w2f, w1b, w2b, act, isem, osem, wsem):
    i = pl.program_id(0)
    n = pl.num_programs(0)
    slot = i % 2
    tm = MOE_TM
    n_blocks = ys_hbm.shape[0] // tm

    def for_chunk(j, fn):
        r = pl.multiple_of(row_ref[j], tm)
        for code in MOE_CHUNK_CODES:
            @pl.when(size_ref[j] == code)
            def _():
                fn(r, code * tm)

    def in_copy(r, rows, s):
        return pltpu.make_async_copy(xs_hbm.at[pl.ds(r, rows)], xin.at[s, pl.ds(0, rows)], isem.at[s])

    def out_copy(r, rows, s):
        return pltpu.make_async_copy(yout.at[s, pl.ds(0, rows)], ys_hbm.at[pl.ds(r, rows)], osem.at[s])

    def start_in(j, s):
        for_chunk(j, lambda r, rows: in_copy(r, rows, s).start())

    def wait_in(j, s):
        for_chunk(j, lambda r, rows: in_copy(r, rows, s).wait())

    def start_out(j, s):
        for_chunk(j, lambda r, rows: out_copy(r, rows, s).start())

    def wait_out(j, s):
        for_chunk(j, lambda r, rows: out_copy(r, rows, s).wait())

    @pl.when(i == 0)
    def _():
        start_in(0, 0)

    @pl.when(i + 1 < n)
    def _():
        start_in(i + 1, 1 - slot)

    @pl.when(i >= 2)
    def _():
        wait_out(i - 2, slot)

    def weight_copies(e, ws):
        return (pltpu.make_async_copy(w1_hbm.at[e], w1f.at[ws], wsem.at[ws]),
                pltpu.make_async_copy(w2_hbm.at[e], w2f.at[ws], wsem.at[ws]))

    n_steps = n
    @pl.when(wt_ref[i] == 1)
    def _():
        ws = wt_ref[n_steps + i]
        nxt = wt_ref[2 * n_steps + i]

        @pl.when(i == 0)
        def _():
            for cp in weight_copies(exp_ref[0], 0):
                cp.start()

        for cp in weight_copies(exp_ref[i], ws):
            cp.wait()

        @pl.when(nxt >= 0)
        def _():
            for cp in weight_copies(nxt, 1 - ws):
                cp.start()

        chunk = 128
        def cast_body(c, carry):
            k0 = pl.multiple_of(c * chunk, chunk)
            w1b[pl.ds(k0, chunk), :] = w1f[ws, pl.ds(k0, chunk), :].astype(BF16)
            w2b[pl.ds(k0, chunk), :] = w2f[ws, pl.ds(k0, chunk), :].astype(BF16)
            return carry
        lax.fori_loop(0, D_MODEL // chunk, cast_body, 0)

    wait_in(i, slot)

    for code in MOE_CHUNK_CODES:
        rows = code * tm

        @pl.when(size_ref[i] == code)
        def _():
            yout[slot, 0:rows] = _expert_mlp(xin[slot, 0:rows], w1b, b1_ref, w2b, b2_ref, act.at[pl.ds(0, rows)])

    start_out(i, slot)

    @pl.when(i == n - 1)
    def _():
        @pl.when(i >= 1)
        def _():
            wait_out(i - 1, 1 - slot)
        wait_out(i, slot)
        yout[1, 0:tm] = jnp.zeros((tm, PACK_W), U32)

        def zero_block(j):
            return pltpu.make_async_copy(yout.at[1, pl.ds(0, tm)], ys_hbm.at[pl.ds(pl.multiple_of(j * tm, tm), tm)],
                                         osem.at[1])

        def start_body(j, carry):
            zero_block(j).start()
            return carry

        def wait_body(j, carry):
            zero_block(j).wait()
            return carry
        lax.fori_loop(meta_ref[2 * N_EXPERTS], n_blocks, start_body, 0)
        lax.fori_loop(meta_ref[2 * N_EXPERTS], n_blocks, wait_body, 0)


def _moe(chunk_row, chunk_size, chunk_exp, weight_tbl, meta, xs, w1, b1, w2, b2):
    by_expert = lambda i, row, size, exp, wt, m: (exp[i], 0, 0)
    grid_spec = pltpu.PrefetchScalarGridSpec(
        num_scalar_prefetch=5,
        grid=(chunk_row.shape[0],),
        in_specs=[
            pl.BlockSpec(memory_space=pl.ANY),
            pl.BlockSpec(memory_space=pl.ANY),
            pl.BlockSpec((None, 1, 2 * D_FF), by_expert),
            pl.BlockSpec(memory_space=pl.ANY),
            pl.BlockSpec((None, 1, D_MODEL), by_expert),
        ],
        out_specs=pl.BlockSpec(memory_space=pl.ANY),
        scratch_shapes=[pltpu.VMEM((2, MOE_CH, PACK_W), U32), pltpu.VMEM((2, MOE_CH, PACK_W), U32),
                        pltpu.VMEM((2, D_MODEL, 2 * D_FF), F32), pltpu.VMEM((2, D_FF, D_MODEL), F32),
                        pltpu.VMEM((D_MODEL, 2 * D_FF), BF16), pltpu.VMEM((D_FF, D_MODEL), BF16),
                        pltpu.VMEM((MOE_CH, D_FF), BF16),
                        pltpu.SemaphoreType.DMA((2,)), pltpu.SemaphoreType.DMA((2,)), pltpu.SemaphoreType.DMA((2,))],
    )
    return pl.pallas_call(
        _moe_kernel,
        grid_spec=grid_spec,
        out_shape=jax.ShapeDtypeStruct(xs.shape, U32),
        compiler_params=pltpu.CompilerParams(dimension_semantics=("arbitrary",), vmem_limit_bytes=VMEM_LIMIT),
        name="moe",
    )(chunk_row, chunk_size, chunk_exp, weight_tbl, meta, xs, w1, b1, w2, b2)


def _combine_kernel(n_prompt_tiles, cnt_ref, off_ref, tot_ref, ys_hbm, routep_ref, routes_ref, x1p_ref, x1s_ref,
                    modp_ref, g2s_ref, fg_ref, op_ref, os_ref, buf, sem):
    i = pl.program_id(0)
    nb = pl.num_programs(0)
    slot = i % 2

    def fetch(tile, s):
        _for_strips(cnt_ref, off_ref, tile, buf.at[s], ys_hbm, sem.at[s], False, lambda cp: cp.start())

    @pl.when(i == 0)
    def _():
        buf[...] = jnp.zeros_like(buf)
        fetch(0, 0)

    @pl.when(i + 1 < nb)
    def _():
        fetch(i + 1, 1 - slot)

    _wait_tile_rows(tot_ref[i], buf.at[slot], ys_hbm, sem.at[slot], False)

    is_prompt = i < n_prompt_tiles
    route = jnp.where(is_prompt, routep_ref[...], routes_ref[...])
    c_pos = lax.broadcasted_iota(jnp.int32, (SEQ_TILE, SORT_ROWS), 1).astype(F32)
    gmat = jnp.zeros((SEQ_TILE, SORT_ROWS), F32)
    for k in range(TOP_K):
        gmat = gmat + jnp.where(c_pos == route[:, ROUTE_R + k:ROUTE_R + k + 1],
                                route[:, ROUTE_G + k:ROUTE_G + k + 1], 0.0)
    g_hi = gmat.astype(BF16)
    g_lo = (gmat - g_hi.astype(F32)).astype(BF16)
    ff = jnp.concatenate(
        [jnp.dot(g_hi, yb, preferred_element_type=F32) + jnp.dot(g_lo, yb, preferred_element_type=F32)
         for yb in _unpack_bf16_pairs(buf[slot])], axis=1)
    g2 = jnp.where(i < n_prompt_tiles, modp_ref[5:6, :], g2s_ref[...])
    x = jnp.where(is_prompt, x1p_ref[...], x1s_ref[...]) + g2 * ff
    y = _rms(x, fg_ref[...])

    @pl.when(i < n_prompt_tiles)
    def _():
        op_ref[...] = y

    @pl.when(i >= n_prompt_tiles)
    def _():
        os_ref[...] = y


def _combine(cnt8, tile_off, tot8, ys, route_p, route_s, x1_p, x1_s, mod_p, g2_rows, final_g, tiles_per_batch):
    nt = tot8.shape[0]
    tt = SEQ_TILE
    npt = x1_p.shape[0] // tt
    p_tile = lambda i, *_: (jnp.minimum(i, npt - 1), 0)
    s_tile = lambda i, *_: (jnp.maximum(i - npt, 0), 0)
    grid_spec = pltpu.PrefetchScalarGridSpec(
        num_scalar_prefetch=3,
        grid=(nt,),
        in_specs=[
            pl.BlockSpec(memory_space=pl.ANY),
            pl.BlockSpec((tt, LANES), p_tile), pl.BlockSpec((tt, LANES), s_tile),
            pl.BlockSpec((tt, D_MODEL), p_tile), pl.BlockSpec((tt, D_MODEL), s_tile),
            pl.BlockSpec((None, 6, D_MODEL), lambda i, *_: (jnp.minimum(i, npt - 1) // tiles_per_batch, 0, 0)),
            pl.BlockSpec((tt, D_MODEL), s_tile),
            pl.BlockSpec((1, D_MODEL), lambda i, *_: (0, 0)),
        ],
        out_specs=(pl.BlockSpec((tt, D_MODEL), p_tile), pl.BlockSpec((tt, D_MODEL), s_tile)),
        scratch_shapes=[pltpu.VMEM((2, SORT_ROWS, PACK_W), U32), pltpu.SemaphoreType.DMA((2,))],
    )
    return pl.pallas_call(
        functools.partial(_combine_kernel, npt),
        grid_spec=grid_spec,
        out_shape=(jax.ShapeDtypeStruct((npt * tt, D_MODEL), F32),
                   jax.ShapeDtypeStruct(((nt - npt) * tt, D_MODEL), F32)),
        compiler_params=pltpu.CompilerParams(dimension_semantics=("arbitrary",), vmem_limit_bytes=VMEM_LIMIT),
        name="combine",
    )(cnt8, tile_off, tot8, ys, route_p, route_s, x1_p, x1_s, mod_p, g2_rows, final_g)


def _block_diag_halves(wg):
    halves = []
    for hh in range(2):
        rows = []
        for bi in range(4):
            row = [wg[hh * 4 + bi] if bj == bi else jnp.zeros((LRU_BLOCK_W, LRU_BLOCK_W), wg.dtype) for bj in range(4)]
            rows.append(jnp.concatenate(row, axis=1))
        halves.append(jnp.concatenate(rows, axis=0))
    return jnp.stack(halves).astype(BF16)


def _rope_tables(pos):
    half = HEAD_DIM // 2
    inv = ROPE_THETA ** (-jnp.arange(half, dtype=F32) / half)
    ang = pos.astype(F32)[:, None] * inv[None, :]
    cos = jnp.cos(ang)
    sin = jnp.sin(ang)
    cos128 = jnp.concatenate([cos, cos, cos, cos], axis=1)
    sin128 = jnp.concatenate([-sin, sin, -sin, sin], axis=1)
    return cos128, sin128


def _full_spec(arr, grid_rank):
    zeros = (0,) * arr.ndim
    if grid_rank == 1:
        return pl.BlockSpec(arr.shape, lambda i: zeros)
    return pl.BlockSpec(arr.shape, lambda b, j: zeros)


def kernel(x_prompt, x_sample, state_lru_h, state_conv, cache_win_k, cache_win_v, c_prompt, c_sample, w_ada, b_ada, norm1_g, w_in, b_in, conv_w, conv_b, lru_wa, lru_ba, lru_wx, lru_bx, lru_lambda, attn_sinks, w_out, b_out, norm2_g, w_router, b_router, w1, b1, w2, b2, final_g):
    bp, seq, _ = x_prompt.shape
    bd, tdec, _ = x_sample.shape
    assert tdec == SUBLANES and seq % (MIX_TILES * SEQ_TILE) == 0 and bd % SAMPLE_BT == 0
    assert SAMPLE_BT * tdec == SEQ_TILE
    n_prompt = bp * seq
    n_sample = bd * tdec
    n_tok = n_prompt + n_sample
    l = 0

    head_perm = [h for c in range(4) for h in (c, GROUP + c)]
    o2 = 2 * LRU_WIDTH

    def permute_heads(arr, axis, start):
        take = lambda a, b: lax.slice_in_dim(arr, a, b, axis=axis)
        heads = [take(start + h * HEAD_DIM, start + (h + 1) * HEAD_DIM) for h in head_perm]
        return jnp.concatenate([take(0, start)] + heads + [take(start + Q_WIDTH, arr.shape[axis])], axis=axis)

    sinks_perm = attn_sinks[l]

    wr = jnp.tile(w_router[l], (1, TOP_K))
    wr_hi = wr.astype(BF16)
    weights = dict(
        norm1_g=norm1_g[l][None, :], w_in=permute_heads(w_in[l], 1, o2).astype(BF16),
        b_in=permute_heads(b_in[l], 0, o2)[None, :],
        conv_w=conv_w[l], conv_b=conv_b[l][None, :],
        ga=_block_diag_halves(lru_wa[l]), gx=_block_diag_halves(lru_wx[l]),
        lru_ba=lru_ba[l][None, :], lru_bx=lru_bx[l][None, :], lam=lru_lambda[l][None, :],
        w_out=permute_heads(w_out[l], 0, LRU_WIDTH).astype(BF16), b_out=b_out[l][None, :], norm2_g=norm2_g[l][None, :],
        wr_hi=wr_hi, wr_lo=(wr - wr_hi.astype(F32)).astype(BF16),
        b_router=jnp.tile(b_router[l], TOP_K)[None, :],
    )
    wlist = [weights[n] for n in WEIGHT_NAMES]

    mod_all = _ada(jnp.concatenate([c_prompt, c_sample], axis=0), w_ada[l], b_ada[l][None, :])
    mod_p = mod_all[:bp].reshape(bp, 6, D_MODEL)
    mod_s = mod_all[bp:].reshape(bd, 6, D_MODEL).transpose(1, 0, 2)

    cos_p, sin_p = _rope_tables(jnp.arange(seq, dtype=jnp.int32))
    cos_s, sin_s = _rope_tables(PAST_LEN + jnp.arange(tdec, dtype=jnp.int32))
    cos_s = jnp.tile(cos_s, (SAMPLE_BT, 1))
    sin_s = jnp.tile(sin_s, (SAMPLE_BT, 1))
    h0_rows = jnp.pad(state_lru_h[l][:, None, :], ((0, 0), (0, tdec - 1), (0, 0))).reshape(n_sample, LRU_WIDTH)
    cprev_rows = jnp.pad(state_conv[l], ((0, 0), (0, tdec - (CONV_W - 1)), (0, 0))).reshape(n_sample, LRU_WIDTH)
    ck = cache_win_k[l].reshape(bd, WINDOW, KV_WIDTH)
    cv = cache_win_v[l].reshape(bd, WINDOW, KV_WIDTH)
    npt = n_prompt // SEQ_TILE
    nst = n_sample // SEQ_TILE
    mix_rows = MIX_TILES * SEQ_TILE
    steps_per_seq = seq // mix_rows
    wspecs = [_full_spec(a, 1) for a in wlist]

    rows_p = lambda width: pl.BlockSpec((mix_rows, width), lambda i: (i, 0))
    tail_p = lambda rows, width: pl.BlockSpec((None, rows, width), lambda i: (i // steps_per_seq, 0, 0))
    (x1_p, h2_p, route_p, cnt_p, hlast_p, ulast_p, klast_p, vlast_p) = pl.pallas_call(
        functools.partial(_prompt_kernel, steps_per_seq),
        grid=(n_prompt // mix_rows,),
        in_specs=[rows_p(D_MODEL),
                  pl.BlockSpec((None, 6, D_MODEL), lambda i: (i // steps_per_seq, 0, 0)),
                  pl.BlockSpec((mix_rows, LANES), lambda i: (i % steps_per_seq, 0)),
                  pl.BlockSpec((mix_rows, LANES), lambda i: (i % steps_per_seq, 0)),
                  pl.BlockSpec(memory_space=pltpu.SMEM)] + wspecs,
        out_specs=(rows_p(D_MODEL), rows_p(D_MODEL), rows_p(LANES),
                   pl.BlockSpec((MIX_TILES, SUBLANES, LANES), lambda i: (i, 0, 0)),
                   tail_p(SUBLANES, LRU_WIDTH), tail_p(SUBLANES, LRU_WIDTH),
                   tail_p(WINDOW, KV_WIDTH), tail_p(WINDOW, KV_WIDTH)),
        out_shape=(
            jax.ShapeDtypeStruct((n_prompt, D_MODEL), F32),
            jax.ShapeDtypeStruct((n_prompt, D_MODEL), F32),
            jax.ShapeDtypeStruct((n_prompt, LANES), F32),
            jax.ShapeDtypeStruct((npt, SUBLANES, LANES), F32),
            jax.ShapeDtypeStruct((bp, SUBLANES, LRU_WIDTH), F32),
            jax.ShapeDtypeStruct((bp, SUBLANES, LRU_WIDTH), F32),
            jax.ShapeDtypeStruct((bp, WINDOW, KV_WIDTH), F32),
            jax.ShapeDtypeStruct((bp, WINDOW, KV_WIDTH), F32),
        ),
        scratch_shapes=[pltpu.VMEM((SUBLANES, LRU_WIDTH), F32), pltpu.VMEM((SUBLANES, LRU_WIDTH), F32),
                        pltpu.VMEM((WINDOW, KV_WIDTH), F32), pltpu.VMEM((WINDOW, KV_WIDTH), F32)],
        compiler_params=pltpu.CompilerParams(dimension_semantics=("arbitrary",), vmem_limit_bytes=VMEM_LIMIT),
        name="prompt_mixer",
    )(x_prompt.reshape(n_prompt, D_MODEL), mod_p, cos_p, sin_p, sinks_perm, *wlist)

    rows_s = lambda width: pl.BlockSpec((SEQ_TILE, width), lambda i: (i, 0))
    cache_spec = pl.BlockSpec((SAMPLE_BT, WINDOW, KV_WIDTH), lambda i: (i, 0, 0))
    (x1_s, h2_s, route_s, cnt_s, g2_rows, hs_s, u_s, s_k, s_v) = pl.pallas_call(
        _sample_kernel,
        grid=(nst,),
        in_specs=[rows_s(D_MODEL),
                  pl.BlockSpec((6, SAMPLE_BT, D_MODEL), lambda i: (0, i, 0)),
                  pl.BlockSpec((SEQ_TILE, LANES), lambda i: (0, 0)),
                  pl.BlockSpec((SEQ_TILE, LANES), lambda i: (0, 0)),
                  pl.BlockSpec(memory_space=pltpu.SMEM),
                  rows_s(LRU_WIDTH), rows_s(LRU_WIDTH), cache_spec, cache_spec] + wspecs,
        out_specs=(rows_s(D_MODEL), rows_s(D_MODEL), rows_s(LANES),
                   pl.BlockSpec((None, SUBLANES, LANES), lambda i: (i, 0, 0)),
                   rows_s(D_MODEL), rows_s(LRU_WIDTH), rows_s(LRU_WIDTH), cache_spec, cache_spec),
        out_shape=(
            jax.ShapeDtypeStruct((n_sample, D_MODEL), F32), jax.ShapeDtypeStruct((n_sample, D_MODEL), F32),
            jax.ShapeDtypeStruct((n_sample, LANES), F32), jax.ShapeDtypeStruct((nst, SUBLANES, LANES), F32),
            jax.ShapeDtypeStruct((n_sample, D_MODEL), F32),
            jax.ShapeDtypeStruct((n_sample, LRU_WIDTH), F32),
            jax.ShapeDtypeStruct((n_sample, LRU_WIDTH), F32),
            jax.ShapeDtypeStruct((bd, WINDOW, KV_WIDTH), F32),
            jax.ShapeDtypeStruct((bd, WINDOW, KV_WIDTH), F32),
        ),
        compiler_params=pltpu.CompilerParams(dimension_semantics=("arbitrary",), vmem_limit_bytes=VMEM_LIMIT),
        name="sample_mixer",
    )(x_sample.reshape(n_sample, D_MODEL), mod_s, cos_s, sin_s, sinks_perm, h0_rows, cprev_rows, ck, cv, *wlist)
    tile_cnt = jnp.concatenate([cnt_p, cnt_s], axis=0)

    n_tiles = npt + nst
    n_assign = n_tok * TOP_K
    max_rows = n_assign + n_tiles * N_EXPERTS * (SUBLANES - 1) + N_EXPERTS * (MOE_TM - 1)
    n_blocks = -(-max_rows // MOE_TM)
    cnt = tile_cnt[:, 0, :N_EXPERTS].astype(jnp.int32)
    cnt8 = (cnt + SUBLANES - 1) // SUBLANES * SUBLANES
    counts = jnp.sum(cnt8, axis=0)
    pcounts = (counts + MOE_TM - 1) // MOE_TM * MOE_TM
    pend = jnp.cumsum(pcounts)
    pstart = pend - pcounts
    tile_off = pstart[None, :] + jnp.cumsum(cnt8, axis=0) - cnt8
    tot8 = jnp.sum(cnt8, axis=1)
    meta = jnp.concatenate([pstart + counts, pend, pend[-1:] // MOE_TM]).astype(jnp.int32)
    cnt8_flat = cnt8.reshape(-1)
    off_flat = tile_off.reshape(-1).astype(jnp.int32)

    xs = _dispatch(cnt8_flat, off_flat, tot8, meta, h2_p, h2_s, route_p, route_s, n_blocks * MOE_TM)

    n_tm = pcounts // MOE_TM
    n_big = n_tm // 4
    has2 = (n_tm // 2) % 2
    n_ch = n_big + has2 + n_tm % 2
    ch_end = jnp.cumsum(n_ch)
    ch_start = ch_end - n_ch
    n_chunks = n_blocks // 4 + 2 * N_EXPERTS
    ci = jnp.arange(n_chunks, dtype=jnp.int32)
    owner = jnp.logical_and(ch_start[None, :] <= ci[:, None], ci[:, None] < ch_end[None, :])
    pick = lambda v: jnp.sum(jnp.where(owner, v[None, :], 0), axis=1)
    live = ci < ch_end[-1]
    local = ci - pick(ch_start)
    chunk_exp = jnp.where(live, pick(jnp.arange(N_EXPERTS, dtype=jnp.int32)), N_EXPERTS - 1).astype(jnp.int32)
    big_c, has2_c = pick(n_big), pick(has2)
    is_big = local < big_c
    is_two = jnp.logical_and(local == big_c, has2_c == 1)
    blocks_before = jnp.where(is_big, local * 4, big_c * 4 + jnp.where(is_two, 0, has2_c * 2))
    chunk_row = jnp.where(live, pick(pstart) + blocks_before * MOE_TM, 0).astype(jnp.int32)
    chunk_size = jnp.where(live, jnp.where(is_big, 4, jnp.where(is_two, 2, 1)), 0).astype(jnp.int32)
    has_rows = n_ch > 0
    e_ar = jnp.arange(N_EXPERTS, dtype=jnp.int32)
    w_slot = (jnp.cumsum(has_rows.astype(jnp.int32)) - 1) % 2
    later = jnp.logical_and(has_rows[None, :], e_ar[None, :] > e_ar[:, None])
    nxt = jnp.min(jnp.where(later, e_ar[None, :], N_EXPERTS), axis=1)
    nxt = jnp.where(nxt == N_EXPERTS, -1, nxt)
    weight_tbl = jnp.concatenate([jnp.logical_and(live, local == 0).astype(jnp.int32),
                                  pick(w_slot), jnp.where(live, pick(nxt), -1)]).astype(jnp.int32)
    ys = _moe(chunk_row, chunk_size, chunk_exp, weight_tbl, meta, xs, w1[l], b1[l][:, None, :], w2[l], b2[l][:, None, :])

    y_p, y_s = _combine(cnt8_flat, off_flat, tot8, ys, route_p, route_s, x1_p, x1_s, mod_p, g2_rows,
                        final_g[None, :], seq // SEQ_TILE)

    y_prompt = y_p.reshape(bp, seq, D_MODEL)
    y_sample = y_s.reshape(bd, tdec, D_MODEL)
    p_h = hlast_p[:, SUBLANES - 1, :][None]
    p_c = ulast_p[:, SUBLANES - (CONV_W - 1):, :][None]
    p_k = klast_p.reshape(1, bp, WINDOW, N_KV_HEADS, HEAD_DIM)
    p_v = vlast_p.reshape(1, bp, WINDOW, N_KV_HEADS, HEAD_DIM)
    s_h = hs_s.reshape(bd, tdec, LRU_WIDTH)[:, tdec - 1, :][None]
    s_c = u_s.reshape(bd, tdec, LRU_WIDTH)[:, tdec - (CONV_W - 1):, :][None]
    s_kk = s_k.reshape(1, bd, WINDOW, N_KV_HEADS, HEAD_DIM)
    s_vv = s_v.reshape(1, bd, WINDOW, N_KV_HEADS, HEAD_DIM)
    return (y_prompt, y_sample, p_h, p_c, p_k, p_v, s_h, s_c, s_kk, s_vv)
```

```python
import functools

import jax
import jax.numpy as jnp
from jax import lax
from jax.experimental import pallas as pl
from jax.experimental.pallas import tpu as pltpu

F32 = jnp.float32
BF16 = jnp.bfloat16

D_MODEL = 1024
LRU_WIDTH = 512
LRU_BLOCKS = 8
LRU_BLOCK_W = LRU_WIDTH // LRU_BLOCKS
CONV_W = 4
LRU_C = 8.0
HEAD_DIM = 64
N_HEADS = 8
N_KV_HEADS = 2
GROUP = N_HEADS // N_KV_HEADS
WINDOW = 128
ROPE_THETA = 10000.0
N_EXPERTS = 32
TOP_K = 4
D_FF = D_MODEL
SWIGLU_LIMIT = 7.0
SWIGLU_ALPHA = 1.702
NORM_EPS = 1e-5
PAST_LEN = 8192
Q_WIDTH = N_HEADS * HEAD_DIM
KV_WIDTH = N_KV_HEADS * HEAD_DIM
IN_WIDTH = 2 * LRU_WIDTH + Q_WIDTH + 2 * KV_WIDTH

LANES = 128
SUBLANES = 8
SEQ_TILE = 256
MIX_TILES = 2
SAMPLE_BT = 32
MOE_TM = 256
MOE_CHUNK_CODES = (4, 2, 1)
MOE_CH = MOE_CHUNK_CODES[0] * MOE_TM
NEG_BIG = -1e30
VMEM_LIMIT = 56 * 1024 * 1024


def _rms(x, g):
    return x * lax.rsqrt(jnp.mean(x * x, axis=-1, keepdims=True) + NORM_EPS) * g


def _rms_mod(x, gain, shift):
    return x * lax.rsqrt(jnp.mean(x * x, axis=-1, keepdims=True) + NORM_EPS) * gain + shift


def _group_scan(a, b):
    rows, width = a.shape
    groups = rows // SUBLANES
    a3 = a.reshape(groups, SUBLANES, width)
    b3 = b.reshape(groups, SUBLANES, width)
    t = lax.broadcasted_iota(jnp.int32, (1, SUBLANES, 1), 1)
    d = 1
    while d < SUBLANES:
        keep = t >= d
        a_s = jnp.where(keep, pltpu.roll(a3, d, axis=1), 1.0)
        b_s = jnp.where(keep, pltpu.roll(b3, d, axis=1), 0.0)
        b3 = a3 * b_s + b3
        a3 = a3 * a_s
        d *= 2
    return a3.reshape(rows, width), b3.reshape(rows, width)


def _chain_groups(a_grp, b_grp, h_in):
    rows = a_grp.shape[0]
    out = []
    carry = h_in
    for g in range(rows // SUBLANES):
        sl = slice(g * SUBLANES, (g + 1) * SUBLANES)
        hg = b_grp[sl] + a_grp[sl] * carry
        out.append(hg)
        carry = hg[SUBLANES - 1:SUBLANES]
    return jnp.concatenate(out, axis=0)


def _rope128(x, cos, sin_signed, first_half):
    sw = jnp.where(first_half, pltpu.roll(x, LANES - HEAD_DIM // 2, axis=1), pltpu.roll(x, HEAD_DIM // 2, axis=1))
    return x * cos + sw * sin_signed


def _softplus(x):
    return jnp.maximum(x, 0.0) + jnp.log1p(jnp.exp(-jnp.abs(x)))


def _lru_coeffs(uc, w, first_pos_mask):
    ub = uc.astype(BF16)
    half = LRU_WIDTH // 2
    ra = jnp.concatenate([jnp.dot(ub[:, :half], w['ga'][0], preferred_element_type=F32),
                          jnp.dot(ub[:, half:], w['ga'][1], preferred_element_type=F32)], axis=1)
    rx = jnp.concatenate([jnp.dot(ub[:, :half], w['gx'][0], preferred_element_type=F32),
                          jnp.dot(ub[:, half:], w['gx'][1], preferred_element_type=F32)], axis=1)
    r = jax.nn.sigmoid(ra + w['lru_ba'][...])
    i = jax.nn.sigmoid(rx + w['lru_bx'][...])
    log_a = -LRU_C * r * _softplus(-w['lam'][...])
    a = jnp.exp(log_a)
    om = 1.0 - a * a
    mult = jnp.where(om > 0.0, om * lax.rsqrt(om), 0.0)
    if first_pos_mask is not None:
        mult = jnp.where(first_pos_mask, 1.0, mult)
    return a, mult * i * uc


def _conv_taps(u, s1, s2, s3, w):
    cw = w['conv_w']
    return w['conv_b'][...] + s3 * cw[0:1, :] + s2 * cw[1:2, :] + s1 * cw[2:3, :] + u * cw[3:4, :]


def _in_proj(x, mod, w):
    sh1, sc1 = mod
    h = _rms_mod(x, w['norm1_g'][...] * (1.0 + sc1), sh1)
    return jnp.dot(h.astype(BF16), w['w_in'][...], preferred_element_type=F32) + w['b_in'][...]


def _post_mix(x, mix, mod, w):
    g1, sh2, sc2 = mod
    x1 = x + g1 * (jnp.dot(mix.astype(BF16), w['w_out'][...], preferred_element_type=F32) + w['b_out'][...])
    h2 = _rms_mod(x1, w['norm2_g'][...] * (1.0 + sc2), sh2)
    h2_hi = h2.astype(BF16)
    h2_lo = (h2 - h2_hi.astype(F32)).astype(BF16)
    logits = (jnp.dot(h2_hi, w['wr_hi'][...], preferred_element_type=F32)
              + jnp.dot(h2_lo, w['wr_hi'][...], preferred_element_type=F32)
              + jnp.dot(h2_hi, w['wr_lo'][...], preferred_element_type=F32)) + w['b_router'][...]
    return x1, h2, logits


ROUTE_E, ROUTE_G, ROUTE_R = 0, TOP_K, 2 * TOP_K


def _lane_roll1(v, shift):
    return pltpu.roll(jnp.broadcast_to(v, (SUBLANES, LANES)), shift, axis=1)[0:1]


def _route_tile(lg):
    rows = lg.shape[0]
    lane = lax.broadcasted_iota(jnp.int32, (1, LANES), 1)
    e_of = lane % N_EXPERTS
    grp = lane // N_EXPERTS
    e_id = e_of.astype(F32)
    onehot = jnp.zeros((rows, LANES), F32)
    vals, ids = [], []
    for k in range(TOP_K):
        m = jnp.max(lg, axis=1, keepdims=True)
        idx = jnp.min(jnp.where(lg == m, e_id, float(N_EXPERTS)), axis=1, keepdims=True)
        sel = e_id == idx
        lg = jnp.where(sel, -jnp.inf, lg)
        onehot = jnp.where(jnp.logical_and(sel, grp == k), 1.0, onehot)
        vals.append(m)
        ids.append(idx)
    ex = [jnp.exp(v - vals[0]) for v in vals]
    denom = ex[0] + ex[1] + ex[2] + ex[3]

    r_i = lax.broadcasted_iota(jnp.int32, (rows, rows), 0)
    c_i = lax.broadcasted_iota(jnp.int32, (rows, rows), 1)
    strict_lower = jnp.where(r_i > c_i, 1.0, 0.0).astype(BF16)
    prefix = jnp.dot(strict_lower, onehot.astype(BF16), preferred_element_type=F32)
    cnt = jnp.sum(onehot, axis=0, keepdims=True)
    base = jnp.zeros((1, LANES), F32)
    tot = cnt
    for s in range(1, TOP_K):
        rolled = _lane_roll1(cnt, s * N_EXPERTS)
        base = base + jnp.where(lane >= s * N_EXPERTS, rolled, 0.0)
        tot = tot + rolled
    pad_cnt = jnp.floor((tot + (SUBLANES - 1.0)) * (1.0 / SUBLANES)) * SUBLANES
    inc = pad_cnt
    d = 1
    while d < N_EXPERTS:
        inc = inc + jnp.where(e_of >= d, _lane_roll1(inc, d), 0.0)
        d *= 2
    strip_start = inc - pad_cnt
    ranked = onehot * (prefix + base + strip_start)

    route = jnp.zeros((rows, LANES), F32)
    for k in range(TOP_K):
        rank_k = jnp.sum(jnp.where(grp == k, ranked, 0.0), axis=1, keepdims=True)
        route = jnp.where(lane == ROUTE_E + k, ids[k], route)
        route = jnp.where(lane == ROUTE_G + k, ex[k] / denom, route)
        route = jnp.where(lane == ROUTE_R + k, rank_k, route)
    return route, tot


def _softmax_sink_pv(s, sink_col, v_bf16):
    m = jnp.maximum(jnp.max(s, axis=-1, keepdims=True), sink_col)
    p = jnp.exp(s - m)
    denom = jnp.sum(p, axis=-1, keepdims=True) + jnp.exp(sink_col - m)
    return p, denom


WEIGHT_NAMES = ('norm1_g', 'w_in', 'b_in', 'conv_w', 'conv_b', 'ga', 'gx', 'lru_ba', 'lru_bx', 'lam',
                'w_out', 'b_out', 'norm2_g', 'wr_hi', 'wr_lo', 'b_router')


def _ada_kernel(c_ref, w_ref, b_ref, o_ref):
    c = c_ref[...]
    s = c * jax.nn.sigmoid(c)
    o_ref[...] = jnp.dot(s, w_ref[...], preferred_element_type=F32, precision=lax.Precision.HIGHEST) + b_ref[...]


def _ada(c_all, w_ada, b_ada):
    rows = c_all.shape[0]
    return pl.pallas_call(
        _ada_kernel,
        grid=(6,),
        in_specs=[pl.BlockSpec((rows, D_MODEL), lambda i: (0, 0)),
                  pl.BlockSpec((D_MODEL, D_MODEL), lambda i: (0, i)),
                  pl.BlockSpec((1, D_MODEL), lambda i: (0, i))],
        out_specs=pl.BlockSpec((rows, D_MODEL), lambda i: (0, i)),
        out_shape=jax.ShapeDtypeStruct((rows, 6 * D_MODEL), F32),
        compiler_params=pltpu.CompilerParams(dimension_semantics=("arbitrary",), vmem_limit_bytes=VMEM_LIMIT),
        name="ada",
    )(c_all, w_ada, b_ada)


def _prompt_body(seq_start, x_ref, mod_ref, cos_ref, sin_ref, sinks_ref, w, x1_ref, h2_ref, lg_ref,
                 hlast_ref, ulast_ref, klast_ref, vlast_ref, conv_c, h_c, k_c, v_c):
    ts = SEQ_TILE

    if seq_start is not None:
        @pl.when(seq_start)
        def _():
            conv_c[...] = jnp.zeros_like(conv_c)
            h_c[...] = jnp.zeros_like(h_c)
            k_c[...] = jnp.zeros_like(k_c)
            v_c[...] = jnp.zeros_like(v_c)

    x = x_ref[...]
    mod = mod_ref[...]
    proj = _in_proj(x, (mod[0:1], mod[1:2]), w)
    u = proj[:, :LRU_WIDTH]
    gate = proj[:, LRU_WIDTH:2 * LRU_WIDTH]
    o2 = 2 * LRU_WIDTH

    rowid = lax.broadcasted_iota(jnp.int32, (ts, 1), 0)
    u_ext = jnp.concatenate([conv_c[...], u], axis=0)
    s1, s2, s3 = (pltpu.roll(u_ext, d, axis=0)[SUBLANES:] for d in (1, 2, 3))
    uc = _conv_taps(u, s1, s2, s3, w)
    conv_c[...] = u[ts - SUBLANES:]
    ulast_ref[...] = u[ts - SUBLANES:]

    first_pos = None if seq_start is None else jnp.logical_and(rowid == 0, seq_start)
    a, bt = _lru_coeffs(uc, w, first_pos)
    hs = _chain_groups(*_group_scan(a, bt), h_c[0:1, :])
    h_tail = hs[ts - SUBLANES:]
    h_c[...] = jnp.broadcast_to(h_tail[SUBLANES - 1:SUBLANES, :], h_c.shape)
    hlast_ref[...] = h_tail
    lru_out = hs * jax.nn.gelu(gate)

    cos = cos_ref[...]
    sin = sin_ref[...]
    lane = lax.broadcasted_iota(jnp.int32, (1, LANES), 1)
    first_half = (lane % HEAD_DIM) < (HEAD_DIM // 2)
    qcols = [_rope128(proj[:, o2 + c * LANES:o2 + (c + 1) * LANES], cos, sin, first_half) * (HEAD_DIM ** -0.5)
             for c in range(4)]
    k = _rope128(proj[:, o2 + Q_WIDTH:o2 + Q_WIDTH + KV_WIDTH], cos, sin, first_half)
    v = proj[:, o2 + Q_WIDTH + KV_WIDTH:]
    k_ext = jnp.concatenate([k_c[...], k], axis=0).astype(BF16)
    v_ext = jnp.concatenate([v_c[...], v], axis=0).astype(BF16)
    k_c[...] = k[ts - WINDOW:]
    v_c[...] = v[ts - WINDOW:]
    klast_ref[...] = k[ts - WINDOW:]
    vlast_ref[...] = v[ts - WINDOW:]

    qi = lax.broadcasted_iota(jnp.int32, (WINDOW, 2 * WINDOW), 0)
    kj = lax.broadcasted_iota(jnp.int32, (WINDOW, 2 * WINDOW), 1)
    band = jnp.logical_and(kj > qi, kj <= qi + WINDOW)
    lane_lo = lane < HEAD_DIM
    grow = lax.broadcasted_iota(jnp.int32, (GROUP * WINDOW, 1), 0) // WINDOW
    attn_cols = [[] for _ in range(4)]
    for blk in range(ts // WINDOW):
        if seq_start is None or blk > 0:
            mask = band
        else:
            mask = jnp.logical_and(band, jnp.logical_or(kj >= WINDOW, jnp.logical_not(seq_start)))
        mask4 = jnp.concatenate([mask] * GROUP, axis=0)
        kb = k_ext[blk * WINDOW:(blk + 2) * WINDOW]
        vb = v_ext[blk * WINDOW:(blk + 2) * WINDOW]
        outs = []
        for kv in range(N_KV_HEADS):
            sel = lane_lo if kv == 0 else jnp.logical_not(lane_lo)
            qs = jnp.concatenate(
                [jnp.where(sel, qc[blk * WINDOW:(blk + 1) * WINDOW], 0.0) for qc in qcols], axis=0).astype(BF16)
            s = lax.dot_general(qs, kb, (((1,), (1,)), ((), ())), preferred_element_type=F32)
            s = jnp.where(mask4, s, NEG_BIG)
            sink_col = jnp.zeros((GROUP * WINDOW, 1), F32)
            for g in range(GROUP):
                sink_col = jnp.where(grow == g, sinks_ref[kv * GROUP + g], sink_col)
            p, denom = _softmax_sink_pv(s, sink_col, vb)
            outs.append(jnp.dot(p.astype(BF16), vb, preferred_element_type=F32) / denom)
        for c in range(4):
            attn_cols[c].append(jnp.where(lane_lo, outs[0][c * WINDOW:(c + 1) * WINDOW],
                                          outs[1][c * WINDOW:(c + 1) * WINDOW]))
    attn = jnp.concatenate([jnp.concatenate(cols, axis=0) for cols in attn_cols], axis=1)

    mix = jnp.concatenate([lru_out, attn], axis=1)
    x1, h2, logits = _post_mix(x, mix, (mod[2:3], mod[3:4], mod[4:5]), w)
    x1_ref[...] = x1
    h2_ref[...] = h2
    route, tot = _route_tile(logits)
    lg_ref[0][...] = route
    lg_ref[1][...] = jnp.broadcast_to(tot, lg_ref[1].shape)


def _expand_rows(m, t):
    b, wd = m.shape
    return jnp.broadcast_to(m[:, None, :], (b, t, wd)).reshape(b * t, wd)


def _sample_body(x_ref, mod_ref, cos_ref, sin_ref, sinks_ref, h0_ref, cprev_ref, ck_ref, cv_ref, w,
                 x1_ref, h2_ref, lg_ref, g2_ref, hs_ref, u_ref, ko_ref, vo_ref):
    bt_, t = SAMPLE_BT, SUBLANES
    rows = bt_ * t

    x = x_ref[...]
    mods = [_expand_rows(mod_ref[i], t) for i in range(6)]
    proj = _in_proj(x, (mods[0], mods[1]), w)
    u = proj[:, :LRU_WIDTH]
    gate = proj[:, LRU_WIDTH:2 * LRU_WIDTH]
    o2 = 2 * LRU_WIDTH
    u_ref[...] = u

    rowid = lax.broadcasted_iota(jnp.int32, (rows, 1), 0) % t
    cprev = cprev_ref[...]
    taps = []
    for d in (1, 2, 3):
        taps.append(jnp.where(rowid >= d, pltpu.roll(u, d, axis=0),
                              pltpu.roll(cprev, (d - (CONV_W - 1)) % rows, axis=0)))
    uc = _conv_taps(u, taps[0], taps[1], taps[2], w)

    a, bt = _lru_coeffs(uc, w, None)
    bt = bt + a * h0_ref[...]
    _, hs = _group_scan(a, bt)
    hs_ref[...] = hs
    lru_out = hs * jax.nn.gelu(gate)

    cos = cos_ref[...]
    sin = sin_ref[...]
    lane = lax.broadcasted_iota(jnp.int32, (1, LANES), 1)
    first_half = (lane % HEAD_DIM) < (HEAD_DIM // 2)
    qcols = [_rope128(proj[:, o2 + c * LANES:o2 + (c + 1) * LANES], cos, sin, first_half) * (HEAD_DIM ** -0.5)
             for c in range(4)]
    k = _rope128(proj[:, o2 + Q_WIDTH:o2 + Q_WIDTH + KV_WIDTH], cos, sin, first_half)
    v = proj[:, o2 + Q_WIDTH + KV_WIDTH:]
    k3 = k.reshape(bt_, t, KV_WIDTH)
    v3 = v.reshape(bt_, t, KV_WIDTH)
    ck = ck_ref[...]
    cv = cv_ref[...]
    ko_ref[:, :WINDOW - t, :] = ck[:, t:, :]
    ko_ref[:, WINDOW - t:, :] = k3
    vo_ref[:, :WINDOW - t, :] = cv[:, t:, :]
    vo_ref[:, WINDOW - t:, :] = v3

    ckb, cvb, k3b, v3b = ck.astype(BF16), cv.astype(BF16), k3.astype(BF16), v3.astype(BF16)
    lane_lo = lane < HEAD_DIM
    gq = GROUP * t
    tq = lax.broadcasted_iota(jnp.int32, (1, gq, 1), 1) % t
    mask_c = lax.broadcasted_iota(jnp.int32, (1, gq, WINDOW), 2) > tq
    mask_n = lax.broadcasted_iota(jnp.int32, (1, gq, t), 2) <= tq
    grow = lax.broadcasted_iota(jnp.int32, (1, gq, 1), 1) // t
    bdims = (((2,), (2,)), ((0,), (0,)))
    pdims = (((2,), (1,)), ((0,), (0,)))
    outs = []
    for kv in range(N_KV_HEADS):
        sel = lane_lo if kv == 0 else jnp.logical_not(lane_lo)
        q3 = jnp.concatenate([jnp.where(sel, qc, 0.0).reshape(bt_, t, LANES) for qc in qcols], axis=1).astype(BF16)
        sc = lax.dot_general(q3, ckb, bdims, preferred_element_type=F32)
        sn = lax.dot_general(q3, k3b, bdims, preferred_element_type=F32)
        sc = jnp.where(mask_c, sc, NEG_BIG)
        sn = jnp.where(mask_n, sn, NEG_BIG)
        sink_col = jnp.zeros((1, gq, 1), F32)
        for g in range(GROUP):
            sink_col = jnp.where(grow == g, sinks_ref[kv * GROUP + g], sink_col)
        m = jnp.maximum(jnp.maximum(jnp.max(sc, axis=-1, keepdims=True), jnp.max(sn, axis=-1, keepdims=True)),
                        sink_col)
        pc = jnp.exp(sc - m)
        pn = jnp.exp(sn - m)
        denom = jnp.sum(pc, axis=-1, keepdims=True) + jnp.sum(pn, axis=-1, keepdims=True) + jnp.exp(sink_col - m)
        o = (lax.dot_general(pc.astype(BF16), cvb, pdims, preferred_element_type=F32)
             + lax.dot_general(pn.astype(BF16), v3b, pdims, preferred_element_type=F32)) / denom
        outs.append(o)
    attn = jnp.concatenate(
        [jnp.where(lane_lo, outs[0][:, c * t:(c + 1) * t, :], outs[1][:, c * t:(c + 1) * t, :]).reshape(rows, LANES)
         for c in range(4)], axis=1)

    mix = jnp.concatenate([lru_out, attn], axis=1)
    x1, h2, logits = _post_mix(x, mix, (mods[2], mods[3], mods[4]), w)
    x1_ref[...] = x1
    h2_ref[...] = h2
    route, tot = _route_tile(logits)
    lg_ref[0][...] = route
    lg_ref[1][...] = jnp.broadcast_to(tot, lg_ref[1].shape)
    g2_ref[...] = mods[5]


def _prompt_kernel(steps_per_seq, x_ref, mod_ref, cos_ref, sin_ref, sinks_ref, *rest):
    nw = len(WEIGHT_NAMES)
    w = dict(zip(WEIGHT_NAMES, rest[:nw]))
    (x1_ref, h2_ref, route_ref, cnt_ref, hlast_ref, ulast_ref, klast_ref, vlast_ref,
     conv_c, h_c, k_c, v_c) = rest[nw:]
    seq_start = pl.program_id(0) % steps_per_seq == 0
    for sub in range(MIX_TILES):
        rows = pl.ds(sub * SEQ_TILE, SEQ_TILE)
        _prompt_body(seq_start if sub == 0 else None, x_ref.at[rows], mod_ref, cos_ref.at[rows], sin_ref.at[rows],
                     sinks_ref, w, x1_ref.at[rows], h2_ref.at[rows], (route_ref.at[rows], cnt_ref.at[sub]),
                     hlast_ref, ulast_ref, klast_ref, vlast_ref, conv_c, h_c, k_c, v_c)


def _sample_kernel(x_ref, mod_ref, cos_ref, sin_ref, sinks_ref, h0_ref, cprev_ref, ck_ref, cv_ref, *rest):
    nw = len(WEIGHT_NAMES)
    w = dict(zip(WEIGHT_NAMES, rest[:nw]))
    x1_ref, h2_ref, route_ref, cnt_ref, g2_ref, hs_ref, u_ref, ko_ref, vo_ref = rest[nw:]
    _sample_body(x_ref, mod_ref, cos_ref, sin_ref, sinks_ref, h0_ref, cprev_ref, ck_ref, cv_ref, w,
                 x1_ref, h2_ref, (route_ref, cnt_ref), g2_ref, hs_ref, u_ref, ko_ref, vo_ref)


PACK_W = D_MODEL // 2
U32 = jnp.uint32


def _pack_bf16_pairs(x):
    xb = x.astype(BF16).astype(F32)
    lo = lax.bitcast_convert_type(xb[:, :PACK_W], U32)
    hi = lax.bitcast_convert_type(xb[:, PACK_W:], U32)
    return jnp.bitwise_or(jnp.bitwise_and(hi, jnp.uint32(0xFFFF0000)), lax.shift_right_logical(lo, jnp.uint32(16)))


def _unpack_bf16_pairs(w):
    lo = lax.bitcast_convert_type(lax.shift_left(w, jnp.uint32(16)), F32).astype(BF16)
    hi = lax.bitcast_convert_type(jnp.bitwise_and(w, jnp.uint32(0xFFFF0000)), F32).astype(BF16)
    return lo, hi


STRIP_SIZES = tuple(SUBLANES << b for b in range(6))
STRIP_LARGE = 64
SORT_ROWS = SEQ_TILE * TOP_K + N_EXPERTS * SUBLANES
TILE_WAIT_SIZES = tuple(SUBLANES << b for b in range(8))


def _for_strips(cnt_ref, off_ref, tile, buf_slot, hbm, sem, to_hbm, act):
    def e_body(e, local):
        n = cnt_ref[tile * N_EXPERTS + e]
        glob = off_ref[tile * N_EXPERTS + e]

        def pieces(sizes, done):
            for p in sizes:
                piece = n & p
                lo = pl.ds(pl.multiple_of(local + done, SUBLANES), p)
                gl = pl.ds(pl.multiple_of(glob + done, SUBLANES), p)

                @pl.when(piece != 0)
                def _():
                    if to_hbm:
                        act(pltpu.make_async_copy(buf_slot.at[lo], hbm.at[gl], sem))
                    else:
                        act(pltpu.make_async_copy(hbm.at[gl], buf_slot.at[lo], sem))
                done = done + piece

        large = tuple(p for p in reversed(STRIP_SIZES) if p >= STRIP_LARGE)
        small = tuple(p for p in reversed(STRIP_SIZES) if p < STRIP_LARGE)
        n_large = n & (-STRIP_LARGE)

        @pl.when(n_large != 0)
        def _():
            pieces(large, 0)
        pieces(small, n_large)
        return local + n
    lax.fori_loop(0, N_EXPERTS, e_body, 0)


def _wait_tile_rows(total, buf_slot, hbm, sem, to_hbm):
    for p in TILE_WAIT_SIZES:
        @pl.when((total & p) != 0)
        def _():
            if to_hbm:
                pltpu.make_async_copy(buf_slot.at[pl.ds(0, p)], hbm.at[pl.ds(0, p)], sem).wait()
            else:
                pltpu.make_async_copy(hbm.at[pl.ds(0, p)], buf_slot.at[pl.ds(0, p)], sem).wait()


def _dispatch_kernel(n_prompt_tiles, cnt_ref, off_ref, tot_ref, meta_ref, h2p_ref, h2s_ref, routep_ref, routes_ref,
                     xs_hbm, sbuf, zblk, sem, zsem):
    i = pl.program_id(0)
    nb = pl.num_programs(0)
    slot = i % 2
    n_blocks = xs_hbm.shape[0] // MOE_TM

    is_prompt = i < n_prompt_tiles
    h2 = jnp.where(is_prompt, h2p_ref[...], h2s_ref[...])
    route_t = jnp.where(is_prompt, routep_ref[...], routes_ref[...]).T
    r_pos = lax.broadcasted_iota(jnp.int32, (SORT_ROWS, SEQ_TILE), 0).astype(F32)
    perm = jnp.zeros((SORT_ROWS, SEQ_TILE), F32)
    for k in range(TOP_K):
        perm = perm + jnp.where(r_pos == route_t[ROUTE_R + k:ROUTE_R + k + 1, :], 1.0, 0.0)
    sbuf[slot] = _pack_bf16_pairs(jnp.dot(perm.astype(BF16), h2.astype(BF16), preferred_element_type=F32))

    _for_strips(cnt_ref, off_ref, i, sbuf.at[slot], xs_hbm, sem.at[slot], True, lambda cp: cp.start())

    @pl.when(i > 0)
    def _():
        _wait_tile_rows(tot_ref[jnp.maximum(i - 1, 0)], sbuf.at[1 - slot], xs_hbm, sem.at[1 - slot], True)

    @pl.when(i == nb - 1)
    def _():
        _wait_tile_rows(tot_ref[i], sbuf.at[slot], xs_hbm, sem.at[slot], True)
        zblk[...] = jnp.zeros_like(zblk)

        def for_region_tails(act):
            def e_body(e, carry):
                start = meta_ref[e]
                n = meta_ref[N_EXPERTS + e] - start
                done = 0
                for p in reversed(STRIP_SIZES[:-1]):
                    piece = n & p
                    rows = pl.ds(pl.multiple_of(start + done, SUBLANES), p)

                    @pl.when(piece != 0)
                    def _():
                        act(pltpu.make_async_copy(zblk.at[pl.ds(0, p)], xs_hbm.at[rows], zsem.at[0]))
                    done = done + piece
                return carry
            lax.fori_loop(0, N_EXPERTS, e_body, 0)

        def for_tail_blocks(act):
            def b_body(j, carry):
                act(pltpu.make_async_copy(zblk, xs_hbm.at[pl.ds(pl.multiple_of(j * MOE_TM, MOE_TM), MOE_TM)],
                                          zsem.at[0]))
                return carry
            lax.fori_loop(meta_ref[2 * N_EXPERTS], n_blocks, b_body, 0)

        for_region_tails(lambda cp: cp.start())
        for_tail_blocks(lambda cp: cp.start())
        for_region_tails(lambda cp: cp.wait())
        for_tail_blocks(lambda cp: cp.wait())


def _dispatch(cnt8, tile_off, tot8, meta, h2_p, h2_s, route_p, route_s, n_rows):
    npt = h2_p.shape[0] // SEQ_TILE
    p_tile = lambda i, *_: (jnp.minimum(i, npt - 1), 0)
    s_tile = lambda i, *_: (jnp.maximum(i - npt, 0), 0)
    nt = tot8.shape[0]
    tt = SEQ_TILE
    grid_spec = pltpu.PrefetchScalarGridSpec(
        num_scalar_prefetch=4,
        grid=(nt,),
        in_specs=[pl.BlockSpec((tt, D_MODEL), p_tile), pl.BlockSpec((tt, D_MODEL), s_tile),
                  pl.BlockSpec((tt, LANES), p_tile), pl.BlockSpec((tt, LANES), s_tile)],
        out_specs=pl.BlockSpec(memory_space=pl.ANY),
        scratch_shapes=[pltpu.VMEM((2, SORT_ROWS, PACK_W), U32), pltpu.VMEM((MOE_TM, PACK_W), U32),
                        pltpu.SemaphoreType.DMA((2,)), pltpu.SemaphoreType.DMA((1,))],
    )
    return pl.pallas_call(
        functools.partial(_dispatch_kernel, npt),
        grid_spec=grid_spec,
        out_shape=jax.ShapeDtypeStruct((n_rows, PACK_W), U32),
        compiler_params=pltpu.CompilerParams(dimension_semantics=("arbitrary",), vmem_limit_bytes=VMEM_LIMIT),
        name="dispatch",
    )(cnt8, tile_off, tot8, meta, h2_p, h2_s, route_p, route_s)


def _expert_mlp(words, w1b, b1_ref, w2b, b2_ref, act_ref):
    xb = jnp.concatenate(_unpack_bf16_pairs(words), axis=1)
    q = D_FF // 4
    for c in range(4):
        zg = jnp.dot(xb, w1b[:, c * q:(c + 1) * q], preferred_element_type=F32) + b1_ref[:, c * q:(c + 1) * q]
        zl = (jnp.dot(xb, w1b[:, D_FF + c * q:D_FF + (c + 1) * q], preferred_element_type=F32)
              + b1_ref[:, D_FF + c * q:D_FF + (c + 1) * q])
        glu = jnp.minimum(zg, SWIGLU_LIMIT)
        lin = jnp.clip(zl, -SWIGLU_LIMIT, SWIGLU_LIMIT)
        act_ref[:, c * q:(c + 1) * q] = (glu * jax.nn.sigmoid(SWIGLU_ALPHA * glu) * (lin + 1.0)).astype(BF16)
    return _pack_bf16_pairs(jnp.dot(act_ref[...], w2b[...], preferred_element_type=F32) + b2_ref[...])


def _moe_kernel(row_ref, size_ref, exp_ref, wt_ref, meta_ref, xs_hbm, w1_hbm, b1_ref, w2_hbm, b2_ref, ys_hbm,
                xin, yout, w1f, w2f, w1b, w2b, act, isem, osem, wsem):
    i = pl.program_id(0)
    n = pl.num_programs(0)
    slot = i % 2
    tm = MOE_TM
    n_blocks = ys_hbm.shape[0] // tm

    def for_chunk(j, fn):
        r = pl.multiple_of(row_ref[j], tm)
        for code in MOE_CHUNK_CODES:
            @pl.when(size_ref[j] == code)
            def _():
                fn(r, code * tm)

    def in_copy(r, rows, s):
        return pltpu.make_async_copy(xs_hbm.at[pl.ds(r, rows)], xin.at[s, pl.ds(0, rows)], isem.at[s])

    def out_copy(r, rows, s):
        return pltpu.make_async_copy(yout.at[s, pl.ds(0, rows)], ys_hbm.at[pl.ds(r, rows)], osem.at[s])

    def start_in(j, s):
        for_chunk(j, lambda r, rows: in_copy(r, rows, s).start())

    def wait_in(j, s):
        for_chunk(j, lambda r, rows: in_copy(r, rows, s).wait())

    def start_out(j, s):
        for_chunk(j, lambda r, rows: out_copy(r, rows, s).start())

    def wait_out(j, s):
        for_chunk(j, lambda r, rows: out_copy(r, rows, s).wait())

    @pl.when(i == 0)
    def _():
        start_in(0, 0)

    @pl.when(i + 1 < n)
    def _():
        start_in(i + 1, 1 - slot)

    @pl.when(i >= 2)
    def _():
        wait_out(i - 2, slot)

    def weight_copies(e, ws):
        return (pltpu.make_async_copy(w1_hbm.at[e], w1f.at[ws], wsem.at[ws]),
                pltpu.make_async_copy(w2_hbm.at[e], w2f.at[ws], wsem.at[ws]))

    n_steps = n
    @pl.when(wt_ref[i] == 1)
    def _():
        ws = wt_ref[n_steps + i]
        nxt = wt_ref[2 * n_steps + i]

        @pl.when(i == 0)
        def _():
            for cp in weight_copies(exp_ref[0], 0):
                cp.start()

        for cp in weight_copies(exp_ref[i], ws):
            cp.wait()

        @pl.when(nxt >= 0)
        def _():
            for cp in weight_copies(nxt, 1 - ws):
                cp.start()

        chunk = 128
        def cast_body(c, carry):
            k0 = pl.multiple_of(c * chunk, chunk)
            w1b[pl.ds(k0, chunk), :] = w1f[ws, pl.ds(k0, chunk), :].astype(BF16)
            w2b[pl.ds(k0, chunk), :] = w2f[ws, pl.ds(k0, chunk), :].astype(BF16)
            return carry
        lax.fori_loop(0, D_MODEL // chunk, cast_body, 0)

    wait_in(i, slot)

    for code in MOE_CHUNK_CODES:
        rows = code * tm

        @pl.when(size_ref[i] == code)
        def _():
            yout[slot, 0:rows] = _expert_mlp(xin[slot, 0:rows], w1b, b1_ref.at[exp_ref[i]], w2b,
                                             b2_ref.at[exp_ref[i]], act.at[pl.ds(0, rows)])

    start_out(i, slot)

    @pl.when(i == n - 1)
    def _():
        @pl.when(i >= 1)
        def _():
            wait_out(i - 1, 1 - slot)
        wait_out(i, slot)
        yout[1, 0:tm] = jnp.zeros((tm, PACK_W), U32)

        def zero_block(j):
            return pltpu.make_async_copy(yout.at[1, pl.ds(0, tm)], ys_hbm.at[pl.ds(pl.multiple_of(j * tm, tm), tm)],
                                         osem.at[1])

        def start_body(j, carry):
            zero_block(j).start()
            return carry

        def wait_body(j, carry):
            zero_block(j).wait()
            return carry
        lax.fori_loop(meta_ref[2 * N_EXPERTS], n_blocks, start_body, 0)
        lax.fori_loop(meta_ref[2 * N_EXPERTS], n_blocks, wait_body, 0)


def _moe(chunk_row, chunk_size, chunk_exp, weight_tbl, meta, xs, w1, b1, w2, b2):
    whole = lambda i, *_: (0, 0, 0)
    grid_spec = pltpu.PrefetchScalarGridSpec(
        num_scalar_prefetch=5,
        grid=(chunk_row.shape[0],),
        in_specs=[
            pl.BlockSpec(memory_space=pl.ANY),
            pl.BlockSpec(memory_space=pl.ANY),
            pl.BlockSpec((N_EXPERTS, 1, 2 * D_FF), whole),
            pl.BlockSpec(memory_space=pl.ANY),
            pl.BlockSpec((N_EXPERTS, 1, D_MODEL), whole),
        ],
        out_specs=pl.BlockSpec(memory_space=pl.ANY),
        scratch_shapes=[pltpu.VMEM((2, MOE_CH, PACK_W), U32), pltpu.VMEM((2, MOE_CH, PACK_W), U32),
                        pltpu.VMEM((2, D_MODEL, 2 * D_FF), F32), pltpu.VMEM((2, D_FF, D_MODEL), F32),
                        pltpu.VMEM((D_MODEL, 2 * D_FF), BF16), pltpu.VMEM((D_FF, D_MODEL), BF16),
                        pltpu.VMEM((MOE_CH, D_FF), BF16),
                        pltpu.SemaphoreType.DMA((2,)), pltpu.SemaphoreType.DMA((2,)), pltpu.SemaphoreType.DMA((2,))],
    )
    return pl.pallas_call(
        _moe_kernel,
        grid_spec=grid_spec,
        out_shape=jax.ShapeDtypeStruct(xs.shape, U32),
        compiler_params=pltpu.CompilerParams(dimension_semantics=("arbitrary",), vmem_limit_bytes=VMEM_LIMIT),
        name="moe",
    )(chunk_row, chunk_size, chunk_exp, weight_tbl, meta, xs, w1, b1, w2, b2)


def _combine_kernel(n_prompt_tiles, cnt_ref, off_ref, tot_ref, ys_hbm, routep_ref, routes_ref, x1p_ref, x1s_ref,
                    modp_ref, g2s_ref, fg_ref, op_ref, os_ref, buf, sem):
    i = pl.program_id(0)
    nb = pl.num_programs(0)
    slot = i % 2

    def fetch(tile, s):
        _for_strips(cnt_ref, off_ref, tile, buf.at[s], ys_hbm, sem.at[s], False, lambda cp: cp.start())

    @pl.when(i == 0)
    def _():
        buf[...] = jnp.zeros_like(buf)
        fetch(0, 0)

    @pl.when(i + 1 < nb)
    def _():
        fetch(i + 1, 1 - slot)

    _wait_tile_rows(tot_ref[i], buf.at[slot], ys_hbm, sem.at[slot], False)

    is_prompt = i < n_prompt_tiles
    route = jnp.where(is_prompt, routep_ref[...], routes_ref[...])
    c_pos = lax.broadcasted_iota(jnp.int32, (SEQ_TILE, SORT_ROWS), 1).astype(F32)
    gmat = jnp.zeros((SEQ_TILE, SORT_ROWS), F32)
    for k in range(TOP_K):
        gmat = gmat + jnp.where(c_pos == route[:, ROUTE_R + k:ROUTE_R + k + 1],
                                route[:, ROUTE_G + k:ROUTE_G + k + 1], 0.0)
    g_hi = gmat.astype(BF16)
    g_lo = (gmat - g_hi.astype(F32)).astype(BF16)
    ff = jnp.concatenate(
        [jnp.dot(g_hi, yb, preferred_element_type=F32) + jnp.dot(g_lo, yb, preferred_element_type=F32)
         for yb in _unpack_bf16_pairs(buf[slot])], axis=1)
    g2 = jnp.where(i < n_prompt_tiles, modp_ref[5:6, :], g2s_ref[...])
    x = jnp.where(is_prompt, x1p_ref[...], x1s_ref[...]) + g2 * ff
    y = _rms(x, fg_ref[...])

    @pl.when(i < n_prompt_tiles)
    def _():
        op_ref[...] = y

    @pl.when(i >= n_prompt_tiles)
    def _():
        os_ref[...] = y


def _combine(cnt8, tile_off, tot8, ys, route_p, route_s, x1_p, x1_s, mod_p, g2_rows, final_g, tiles_per_batch):
    nt = tot8.shape[0]
    tt = SEQ_TILE
    npt = x1_p.shape[0] // tt
    p_tile = lambda i, *_: (jnp.minimum(i, npt - 1), 0)
    s_tile = lambda i, *_: (jnp.maximum(i - npt, 0), 0)
    grid_spec = pltpu.PrefetchScalarGridSpec(
        num_scalar_prefetch=3,
        grid=(nt,),
        in_specs=[
            pl.BlockSpec(memory_space=pl.ANY),
            pl.BlockSpec((tt, LANES), p_tile), pl.BlockSpec((tt, LANES), s_tile),
            pl.BlockSpec((tt, D_MODEL), p_tile), pl.BlockSpec((tt, D_MODEL), s_tile),
            pl.BlockSpec((None, 6, D_MODEL), lambda i, *_: (jnp.minimum(i, npt - 1) // tiles_per_batch, 0, 0)),
            pl.BlockSpec((tt, D_MODEL), s_tile),
            pl.BlockSpec((1, D_MODEL), lambda i, *_: (0, 0)),
        ],
        out_specs=(pl.BlockSpec((tt, D_MODEL), p_tile), pl.BlockSpec((tt, D_MODEL), s_tile)),
        scratch_shapes=[pltpu.VMEM((2, SORT_ROWS, PACK_W), U32), pltpu.SemaphoreType.DMA((2,))],
    )
    return pl.pallas_call(
        functools.partial(_combine_kernel, npt),
        grid_spec=grid_spec,
        out_shape=(jax.ShapeDtypeStruct((npt * tt, D_MODEL), F32),
                   jax.ShapeDtypeStruct(((nt - npt) * tt, D_MODEL), F32)),
        compiler_params=pltpu.CompilerParams(dimension_semantics=("arbitrary",), vmem_limit_bytes=VMEM_LIMIT),
        name="combine",
    )(cnt8, tile_off, tot8, ys, route_p, route_s, x1_p, x1_s, mod_p, g2_rows, final_g)


def _block_diag_halves(wg):
    halves = []
    for hh in range(2):
        rows = []
        for bi in range(4):
            row = [wg[hh * 4 + bi] if bj == bi else jnp.zeros((LRU_BLOCK_W, LRU_BLOCK_W), wg.dtype) for bj in range(4)]
            rows.append(jnp.concatenate(row, axis=1))
        halves.append(jnp.concatenate(rows, axis=0))
    return jnp.stack(halves).astype(BF16)


def _rope_tables(pos):
    half = HEAD_DIM // 2
    inv = ROPE_THETA ** (-jnp.arange(half, dtype=F32) / half)
    ang = pos.astype(F32)[:, None] * inv[None, :]
    cos = jnp.cos(ang)
    sin = jnp.sin(ang)
    cos128 = jnp.concatenate([cos, cos, cos, cos], axis=1)
    sin128 = jnp.concatenate([-sin, sin, -sin, sin], axis=1)
    return cos128, sin128


def _full_spec(arr, grid_rank):
    zeros = (0,) * arr.ndim
    if grid_rank == 1:
        return pl.BlockSpec(arr.shape, lambda i: zeros)
    return pl.BlockSpec(arr.shape, lambda b, j: zeros)


def kernel(x_prompt, x_sample, state_lru_h, state_conv, cache_win_k, cache_win_v, c_prompt, c_sample, w_ada, b_ada, norm1_g, w_in, b_in, conv_w, conv_b, lru_wa, lru_ba, lru_wx, lru_bx, lru_lambda, attn_sinks, w_out, b_out, norm2_g, w_router, b_router, w1, b1, w2, b2, final_g):
    bp, seq, _ = x_prompt.shape
    bd, tdec, _ = x_sample.shape
    assert tdec == SUBLANES and seq % (MIX_TILES * SEQ_TILE) == 0 and bd % SAMPLE_BT == 0
    assert SAMPLE_BT * tdec == SEQ_TILE
    n_prompt = bp * seq
    n_sample = bd * tdec
    n_tok = n_prompt + n_sample
    l = 0

    head_perm = [h for c in range(4) for h in (c, GROUP + c)]
    o2 = 2 * LRU_WIDTH

    def permute_heads(arr, axis, start):
        take = lambda a, b: lax.slice_in_dim(arr, a, b, axis=axis)
        heads = [take(start + h * HEAD_DIM, start + (h + 1) * HEAD_DIM) for h in head_perm]
        return jnp.concatenate([take(0, start)] + heads + [take(start + Q_WIDTH, arr.shape[axis])], axis=axis)

    sinks_perm = attn_sinks[l]

    wr = jnp.tile(w_router[l], (1, TOP_K))
    wr_hi = wr.astype(BF16)
    weights = dict(
        norm1_g=norm1_g[l][None, :], w_in=permute_heads(w_in[l], 1, o2).astype(BF16),
        b_in=permute_heads(b_in[l], 0, o2)[None, :],
        conv_w=conv_w[l], conv_b=conv_b[l][None, :],
        ga=_block_diag_halves(lru_wa[l]), gx=_block_diag_halves(lru_wx[l]),
        lru_ba=lru_ba[l][None, :], lru_bx=lru_bx[l][None, :], lam=lru_lambda[l][None, :],
        w_out=permute_heads(w_out[l], 0, LRU_WIDTH).astype(BF16), b_out=b_out[l][None, :], norm2_g=norm2_g[l][None, :],
        wr_hi=wr_hi, wr_lo=(wr - wr_hi.astype(F32)).astype(BF16),
        b_router=jnp.tile(b_router[l], TOP_K)[None, :],
    )
    wlist = [weights[n] for n in WEIGHT_NAMES]

    mod_all = _ada(jnp.concatenate([c_prompt, c_sample], axis=0), w_ada[l], b_ada[l][None, :])
    mod_p = mod_all[:bp].reshape(bp, 6, D_MODEL)
    mod_s = mod_all[bp:].reshape(bd, 6, D_MODEL).transpose(1, 0, 2)

    cos_p, sin_p = _rope_tables(jnp.arange(seq, dtype=jnp.int32))
    cos_s, sin_s = _rope_tables(PAST_LEN + jnp.arange(tdec, dtype=jnp.int32))
    cos_s = jnp.tile(cos_s, (SAMPLE_BT, 1))
    sin_s = jnp.tile(sin_s, (SAMPLE_BT, 1))
    h0_rows = jnp.pad(state_lru_h[l][:, None, :], ((0, 0), (0, tdec - 1), (0, 0))).reshape(n_sample, LRU_WIDTH)
    cprev_rows = jnp.pad(state_conv[l], ((0, 0), (0, tdec - (CONV_W - 1)), (0, 0))).reshape(n_sample, LRU_WIDTH)
    ck = cache_win_k[l].reshape(bd, WINDOW, KV_WIDTH)
    cv = cache_win_v[l].reshape(bd, WINDOW, KV_WIDTH)
    npt = n_prompt // SEQ_TILE
    nst = n_sample // SEQ_TILE
    mix_rows = MIX_TILES * SEQ_TILE
    steps_per_seq = seq // mix_rows
    wspecs = [_full_spec(a, 1) for a in wlist]

    rows_p = lambda width: pl.BlockSpec((mix_rows, width), lambda i: (i, 0))
    tail_p = lambda rows, width: pl.BlockSpec((None, rows, width), lambda i: (i // steps_per_seq, 0, 0))
    (x1_p, h2_p, route_p, cnt_p, hlast_p, ulast_p, klast_p, vlast_p) = pl.pallas_call(
        functools.partial(_prompt_kernel, steps_per_seq),
        grid=(n_prompt // mix_rows,),
        in_specs=[rows_p(D_MODEL),
                  pl.BlockSpec((None, 6, D_MODEL), lambda i: (i // steps_per_seq, 0, 0)),
                  pl.BlockSpec((mix_rows, LANES), lambda i: (i % steps_per_seq, 0)),
                  pl.BlockSpec((mix_rows, LANES), lambda i: (i % steps_per_seq, 0)),
                  pl.BlockSpec(memory_space=pltpu.SMEM)] + wspecs,
        out_specs=(rows_p(D_MODEL), rows_p(D_MODEL), rows_p(LANES),
                   pl.BlockSpec((MIX_TILES, SUBLANES, LANES), lambda i: (i, 0, 0)),
                   tail_p(SUBLANES, LRU_WIDTH), tail_p(SUBLANES, LRU_WIDTH),
                   tail_p(WINDOW, KV_WIDTH), tail_p(WINDOW, KV_WIDTH)),
        out_shape=(
            jax.ShapeDtypeStruct((n_prompt, D_MODEL), F32),
            jax.ShapeDtypeStruct((n_prompt, D_MODEL), F32),
            jax.ShapeDtypeStruct((n_prompt, LANES), F32),
            jax.ShapeDtypeStruct((npt, SUBLANES, LANES), F32),
            jax.ShapeDtypeStruct((bp, SUBLANES, LRU_WIDTH), F32),
            jax.ShapeDtypeStruct((bp, SUBLANES, LRU_WIDTH), F32),
            jax.ShapeDtypeStruct((bp, WINDOW, KV_WIDTH), F32),
            jax.ShapeDtypeStruct((bp, WINDOW, KV_WIDTH), F32),
        ),
        scratch_shapes=[pltpu.VMEM((SUBLANES, LRU_WIDTH), F32), pltpu.VMEM((SUBLANES, LRU_WIDTH), F32),
                        pltpu.VMEM((WINDOW, KV_WIDTH), F32), pltpu.VMEM((WINDOW, KV_WIDTH), F32)],
        compiler_params=pltpu.CompilerParams(dimension_semantics=("arbitrary",), vmem_limit_bytes=VMEM_LIMIT),
        name="prompt_mixer",
    )(x_prompt.reshape(n_prompt, D_MODEL), mod_p, cos_p, sin_p, sinks_perm, *wlist)

    rows_s = lambda width: pl.BlockSpec((SEQ_TILE, width), lambda i: (i, 0))
    cache_spec = pl.BlockSpec((SAMPLE_BT, WINDOW, KV_WIDTH), lambda i: (i, 0, 0))
    (x1_s, h2_s, route_s, cnt_s, g2_rows, hs_s, u_s, s_k, s_v) = pl.pallas_call(
        _sample_kernel,
        grid=(nst,),
        in_specs=[rows_s(D_MODEL),
                  pl.BlockSpec((6, SAMPLE_BT, D_MODEL), lambda i: (0, i, 0)),
                  pl.BlockSpec((SEQ_TILE, LANES), lambda i: (0, 0)),
                  pl.BlockSpec((SEQ_TILE, LANES), lambda i: (0, 0)),
                  pl.BlockSpec(memory_space=pltpu.SMEM),
                  rows_s(LRU_WIDTH), rows_s(LRU_WIDTH), cache_spec, cache_spec] + wspecs,
        out_specs=(rows_s(D_MODEL), rows_s(D_MODEL), rows_s(LANES),
                   pl.BlockSpec((None, SUBLANES, LANES), lambda i: (i, 0, 0)),
                   rows_s(D_MODEL), rows_s(LRU_WIDTH), rows_s(LRU_WIDTH), cache_spec, cache_spec),
        out_shape=(
            jax.ShapeDtypeStruct((n_sample, D_MODEL), F32), jax.ShapeDtypeStruct((n_sample, D_MODEL), F32),
            jax.ShapeDtypeStruct((n_sample, LANES), F32), jax.ShapeDtypeStruct((nst, SUBLANES, LANES), F32),
            jax.ShapeDtypeStruct((n_sample, D_MODEL), F32),
            jax.ShapeDtypeStruct((n_sample, LRU_WIDTH), F32),
            jax.ShapeDtypeStruct((n_sample, LRU_WIDTH), F32),
            jax.ShapeDtypeStruct((bd, WINDOW, KV_WIDTH), F32),
            jax.ShapeDtypeStruct((bd, WINDOW, KV_WIDTH), F32),
        ),
        compiler_params=pltpu.CompilerParams(dimension_semantics=("arbitrary",), vmem_limit_bytes=VMEM_LIMIT),
        name="sample_mixer",
    )(x_sample.reshape(n_sample, D_MODEL), mod_s, cos_s, sin_s, sinks_perm, h0_rows, cprev_rows, ck, cv, *wlist)
    tile_cnt = jnp.concatenate([cnt_p, cnt_s], axis=0)

    n_tiles = npt + nst
    n_assign = n_tok * TOP_K
    max_rows = n_assign + n_tiles * N_EXPERTS * (SUBLANES - 1) + N_EXPERTS * (MOE_TM - 1)
    n_blocks = -(-max_rows // MOE_TM)
    cnt = tile_cnt[:, 0, :N_EXPERTS].astype(jnp.int32)
    cnt8 = (cnt + SUBLANES - 1) // SUBLANES * SUBLANES
    counts = jnp.sum(cnt8, axis=0)
    pcounts = (counts + MOE_TM - 1) // MOE_TM * MOE_TM
    pend = jnp.cumsum(pcounts)
    pstart = pend - pcounts
    tile_off = pstart[None, :] + jnp.cumsum(cnt8, axis=0) - cnt8
    tot8 = jnp.sum(cnt8, axis=1)
    meta = jnp.concatenate([pstart + counts, pend, pend[-1:] // MOE_TM]).astype(jnp.int32)
    cnt8_flat = cnt8.reshape(-1)
    off_flat = tile_off.reshape(-1).astype(jnp.int32)

    xs = _dispatch(cnt8_flat, off_flat, tot8, meta, h2_p, h2_s, route_p, route_s, n_blocks * MOE_TM)

    n_tm = pcounts // MOE_TM
    n_big = n_tm // 4
    has2 = (n_tm // 2) % 2
    n_ch = n_big + has2 + n_tm % 2
    ch_end = jnp.cumsum(n_ch)
    ch_start = ch_end - n_ch
    n_chunks = n_blocks // 4 + 2 * N_EXPERTS
    ci = jnp.arange(n_chunks, dtype=jnp.int32)
    owner = jnp.logical_and(ch_start[None, :] <= ci[:, None], ci[:, None] < ch_end[None, :])
    pick = lambda v: jnp.sum(jnp.where(owner, v[None, :], 0), axis=1)
    live = ci < ch_end[-1]
    local = ci - pick(ch_start)
    chunk_exp = jnp.where(live, pick(jnp.arange(N_EXPERTS, dtype=jnp.int32)), N_EXPERTS - 1).astype(jnp.int32)
    big_c, has2_c = pick(n_big), pick(has2)
    is_big = local < big_c
    is_two = jnp.logical_and(local == big_c, has2_c == 1)
    blocks_before = jnp.where(is_big, local * 4, big_c * 4 + jnp.where(is_two, 0, has2_c * 2))
    chunk_row = jnp.where(live, pick(pstart) + blocks_before * MOE_TM, 0).astype(jnp.int32)
    chunk_size = jnp.where(live, jnp.where(is_big, 4, jnp.where(is_two, 2, 1)), 0).astype(jnp.int32)
    has_rows = n_ch > 0
    e_ar = jnp.arange(N_EXPERTS, dtype=jnp.int32)
    w_slot = (jnp.cumsum(has_rows.astype(jnp.int32)) - 1) % 2
    later = jnp.logical_and(has_rows[None, :], e_ar[None, :] > e_ar[:, None])
    nxt = jnp.min(jnp.where(later, e_ar[None, :], N_EXPERTS), axis=1)
    nxt = jnp.where(nxt == N_EXPERTS, -1, nxt)
    weight_tbl = jnp.concatenate([jnp.logical_and(live, local == 0).astype(jnp.int32),
                                  pick(w_slot), jnp.where(live, pick(nxt), -1)]).astype(jnp.int32)
    ys = _moe(chunk_row, chunk_size, chunk_exp, weight_tbl, meta, xs, w1[l], b1[l][:, None, :], w2[l], b2[l][:, None, :])

    y_p, y_s = _combine(cnt8_flat, off_flat, tot8, ys, route_p, route_s, x1_p, x1_s, mod_p, g2_rows,
                        final_g[None, :], seq // SEQ_TILE)

    y_prompt = y_p.reshape(bp, seq, D_MODEL)
    y_sample = y_s.reshape(bd, tdec, D_MODEL)
    p_h = hlast_p[:, SUBLANES - 1, :][None]
    p_c = ulast_p[:, SUBLANES - (CONV_W - 1):, :][None]
    p_k = klast_p.reshape(1, bp, WINDOW, N_KV_HEADS, HEAD_DIM)
    p_v = vlast_p.reshape(1, bp, WINDOW, N_KV_HEADS, HEAD_DIM)
    s_h = hs_s.reshape(bd, tdec, LRU_WIDTH)[:, tdec - 1, :][None]
    s_c = u_s.reshape(bd, tdec, LRU_WIDTH)[:, tdec - (CONV_W - 1):, :][None]
    s_kk = s_k.reshape(1, bd, WINDOW, N_KV_HEADS, HEAD_DIM)
    s_vv = s_v.reshape(1, bd, WINDOW, N_KV_HEADS, HEAD_DIM)
    return (y_prompt, y_sample, p_h, p_c, p_k, p_v, s_h, s_c, s_kk, s_vv)
```

```python
import functools

import jax
import jax.numpy as jnp
from jax import lax
from jax.experimental import pallas as pl
from jax.experimental.pallas import tpu as pltpu

F32 = jnp.float32
BF16 = jnp.bfloat16

D_MODEL = 1024
LRU_WIDTH = 512
LRU_BLOCKS = 8
LRU_BLOCK_W = LRU_WIDTH // LRU_BLOCKS
CONV_W = 4
LRU_C = 8.0
HEAD_DIM = 64
N_HEADS = 8
N_KV_HEADS = 2
GROUP = N_HEADS // N_KV_HEADS
WINDOW = 128
ROPE_THETA = 10000.0
N_EXPERTS = 32
TOP_K = 4
D_FF = D_MODEL
SWIGLU_LIMIT = 7.0
SWIGLU_ALPHA = 1.702
NORM_EPS = 1e-5
PAST_LEN = 8192
Q_WIDTH = N_HEADS * HEAD_DIM
KV_WIDTH = N_KV_HEADS * HEAD_DIM
IN_WIDTH = 2 * LRU_WIDTH + Q_WIDTH + 2 * KV_WIDTH

LANES = 128
SUBLANES = 8
SEQ_TILE = 256
MIX_TILES = 2
SAMPLE_BT = 32
MOE_TM = 256
MOE_CHUNK_CODES = (4, 2, 1)
MOE_CH = MOE_CHUNK_CODES[0] * MOE_TM
NEG_BIG = -1e30
VMEM_LIMIT = 56 * 1024 * 1024


def _rms(x, g):
    return x * lax.rsqrt(jnp.mean(x * x, axis=-1, keepdims=True) + NORM_EPS) * g


def _rms_mod(x, gain, shift):
    return x * lax.rsqrt(jnp.mean(x * x, axis=-1, keepdims=True) + NORM_EPS) * gain + shift


def _group_scan(a, b):
    rows, width = a.shape
    groups = rows // SUBLANES
    a3 = a.reshape(groups, SUBLANES, width)
    b3 = b.reshape(groups, SUBLANES, width)
    t = lax.broadcasted_iota(jnp.int32, (1, SUBLANES, 1), 1)
    d = 1
    while d < SUBLANES:
        keep = t >= d
        a_s = jnp.where(keep, pltpu.roll(a3, d, axis=1), 1.0)
        b_s = jnp.where(keep, pltpu.roll(b3, d, axis=1), 0.0)
        b3 = a3 * b_s + b3
        a3 = a3 * a_s
        d *= 2
    return a3.reshape(rows, width), b3.reshape(rows, width)


def _chain_groups(a_grp, b_grp, h_in):
    rows = a_grp.shape[0]
    out = []
    carry = h_in
    for g in range(rows // SUBLANES):
        sl = slice(g * SUBLANES, (g + 1) * SUBLANES)
        hg = b_grp[sl] + a_grp[sl] * carry
        out.append(hg)
        carry = hg[SUBLANES - 1:SUBLANES]
    return jnp.concatenate(out, axis=0)


def _rope128(x, cos, sin_signed, first_half):
    sw = jnp.where(first_half, pltpu.roll(x, LANES - HEAD_DIM // 2, axis=1), pltpu.roll(x, HEAD_DIM // 2, axis=1))
    return x * cos + sw * sin_signed


def _softplus(x):
    return jnp.maximum(x, 0.0) + jnp.log1p(jnp.exp(-jnp.abs(x)))


def _lru_coeffs(uc, w, first_pos_mask):
    ub = uc.astype(BF16)
    half = LRU_WIDTH // 2
    ra = jnp.concatenate([jnp.dot(ub[:, :half], w['ga'][0], preferred_element_type=F32),
                          jnp.dot(ub[:, half:], w['ga'][1], preferred_element_type=F32)], axis=1)
    rx = jnp.concatenate([jnp.dot(ub[:, :half], w['gx'][0], preferred_element_type=F32),
                          jnp.dot(ub[:, half:], w['gx'][1], preferred_element_type=F32)], axis=1)
    r = jax.nn.sigmoid(ra + w['lru_ba'][...])
    i = jax.nn.sigmoid(rx + w['lru_bx'][...])
    log_a = -LRU_C * r * _softplus(-w['lam'][...])
    a = jnp.exp(log_a)
    om = 1.0 - a * a
    mult = jnp.where(om > 0.0, om * lax.rsqrt(om), 0.0)
    if first_pos_mask is not None:
        mult = jnp.where(first_pos_mask, 1.0, mult)
    return a, mult * i * uc


def _conv_taps(u, s1, s2, s3, w):
    cw = w['conv_w']
    return w['conv_b'][...] + s3 * cw[0:1, :] + s2 * cw[1:2, :] + s1 * cw[2:3, :] + u * cw[3:4, :]


def _in_proj(x, mod, w):
    sh1, sc1 = mod
    h = _rms_mod(x, w['norm1_g'][...] * (1.0 + sc1), sh1)
    return jnp.dot(h.astype(BF16), w['w_in'][...], preferred_element_type=F32) + w['b_in'][...]


def _post_mix(x, mix, mod, w):
    g1, sh2, sc2 = mod
    x1 = x + g1 * (jnp.dot(mix.astype(BF16), w['w_out'][...], preferred_element_type=F32) + w['b_out'][...])
    h2 = _rms_mod(x1, w['norm2_g'][...] * (1.0 + sc2), sh2)
    h2_hi = h2.astype(BF16)
    h2_lo = (h2 - h2_hi.astype(F32)).astype(BF16)
    logits = (jnp.dot(h2_hi, w['wr_hi'][...], preferred_element_type=F32)
              + jnp.dot(h2_lo, w['wr_hi'][...], preferred_element_type=F32)
              + jnp.dot(h2_hi, w['wr_lo'][...], preferred_element_type=F32)) + w['b_router'][...]
    return x1, h2, logits


ROUTE_E, ROUTE_G, ROUTE_R = 0, TOP_K, 2 * TOP_K


def _lane_roll1(v, shift):
    return pltpu.roll(jnp.broadcast_to(v, (SUBLANES, LANES)), shift, axis=1)[0:1]


def _route_tile(lg):
    rows = lg.shape[0]
    lane = lax.broadcasted_iota(jnp.int32, (1, LANES), 1)
    e_of = lane % N_EXPERTS
    grp = lane // N_EXPERTS
    e_id = e_of.astype(F32)
    onehot = jnp.zeros((rows, LANES), F32)
    vals, ids = [], []
    for k in range(TOP_K):
        m = jnp.max(lg, axis=1, keepdims=True)
        idx = jnp.min(jnp.where(lg == m, e_id, float(N_EXPERTS)), axis=1, keepdims=True)
        sel = e_id == idx
        lg = jnp.where(sel, -jnp.inf, lg)
        onehot = jnp.where(jnp.logical_and(sel, grp == k), 1.0, onehot)
        vals.append(m)
        ids.append(idx)
    ex = [jnp.exp(v - vals[0]) for v in vals]
    denom = ex[0] + ex[1] + ex[2] + ex[3]

    r_i = lax.broadcasted_iota(jnp.int32, (rows, rows), 0)
    c_i = lax.broadcasted_iota(jnp.int32, (rows, rows), 1)
    strict_lower = jnp.where(r_i > c_i, 1.0, 0.0).astype(BF16)
    prefix = jnp.dot(strict_lower, onehot.astype(BF16), preferred_element_type=F32)
    cnt = jnp.sum(onehot, axis=0, keepdims=True)
    base = jnp.zeros((1, LANES), F32)
    tot = cnt
    for s in range(1, TOP_K):
        rolled = _lane_roll1(cnt, s * N_EXPERTS)
        base = base + jnp.where(lane >= s * N_EXPERTS, rolled, 0.0)
        tot = tot + rolled
    pad_cnt = jnp.floor((tot + (SUBLANES - 1.0)) * (1.0 / SUBLANES)) * SUBLANES
    inc = pad_cnt
    d = 1
    while d < N_EXPERTS:
        inc = inc + jnp.where(e_of >= d, _lane_roll1(inc, d), 0.0)
        d *= 2
    strip_start = inc - pad_cnt
    ranked = onehot * (prefix + base + strip_start)

    route = jnp.zeros((rows, LANES), F32)
    for k in range(TOP_K):
        rank_k = jnp.sum(jnp.where(grp == k, ranked, 0.0), axis=1, keepdims=True)
        route = jnp.where(lane == ROUTE_E + k, ids[k], route)
        route = jnp.where(lane == ROUTE_G + k, ex[k] / denom, route)
        route = jnp.where(lane == ROUTE_R + k, rank_k, route)
    return route, tot


def _softmax_sink_pv(s, sink_col, v_bf16):
    m = jnp.maximum(jnp.max(s, axis=-1, keepdims=True), sink_col)
    p = jnp.exp(s - m)
    denom = jnp.sum(p, axis=-1, keepdims=True) + jnp.exp(sink_col - m)
    return p, denom


WEIGHT_NAMES = ('norm1_g', 'w_in', 'b_in', 'conv_w', 'conv_b', 'ga', 'gx', 'lru_ba', 'lru_bx', 'lam',
                'w_out', 'b_out', 'norm2_g', 'wr_hi', 'wr_lo', 'b_router')


def _ada_kernel(c_ref, w_ref, b_ref, o_ref):
    c = c_ref[...]
    s = c * jax.nn.sigmoid(c)
    o_ref[...] = jnp.dot(s, w_ref[...], preferred_element_type=F32, precision=lax.Precision.HIGHEST) + b_ref[...]


def _ada(c_all, w_ada, b_ada):
    rows = c_all.shape[0]
    return pl.pallas_call(
        _ada_kernel,
        grid=(6,),
        in_specs=[pl.BlockSpec((rows, D_MODEL), lambda i: (0, 0)),
                  pl.BlockSpec((D_MODEL, D_MODEL), lambda i: (0, i)),
                  pl.BlockSpec((1, D_MODEL), lambda i: (0, i))],
        out_specs=pl.BlockSpec((rows, D_MODEL), lambda i: (0, i)),
        out_shape=jax.ShapeDtypeStruct((rows, 6 * D_MODEL), F32),
        compiler_params=pltpu.CompilerParams(dimension_semantics=("arbitrary",), vmem_limit_bytes=VMEM_LIMIT),
        name="ada",
    )(c_all, w_ada, b_ada)


def _prompt_body(seq_start, x_ref, mod_ref, cos_ref, sin_ref, sinks_ref, w, x1_ref, h2_ref, lg_ref,
                 hlast_ref, ulast_ref, klast_ref, vlast_ref, conv_c, h_c, k_c, v_c):
    ts = SEQ_TILE

    if seq_start is not None:
        @pl.when(seq_start)
        def _():
            conv_c[...] = jnp.zeros_like(conv_c)
            h_c[...] = jnp.zeros_like(h_c)
            k_c[...] = jnp.zeros_like(k_c)
            v_c[...] = jnp.zeros_like(v_c)

    x = x_ref[...]
    mod = mod_ref[...]
    proj = _in_proj(x, (mod[0:1], mod[1:2]), w)
    u = proj[:, :LRU_WIDTH]
    gate = proj[:, LRU_WIDTH:2 * LRU_WIDTH]
    o2 = 2 * LRU_WIDTH

    rowid = lax.broadcasted_iota(jnp.int32, (ts, 1), 0)
    u_ext = jnp.concatenate([conv_c[...], u], axis=0)
    s1, s2, s3 = (pltpu.roll(u_ext, d, axis=0)[SUBLANES:] for d in (1, 2, 3))
    uc = _conv_taps(u, s1, s2, s3, w)
    conv_c[...] = u[ts - SUBLANES:]
    ulast_ref[...] = u[ts - SUBLANES:]

    first_pos = None if seq_start is None else jnp.logical_and(rowid == 0, seq_start)
    a, bt = _lru_coeffs(uc, w, first_pos)
    hs = _chain_groups(*_group_scan(a, bt), h_c[0:1, :])
    h_tail = hs[ts - SUBLANES:]
    h_c[...] = jnp.broadcast_to(h_tail[SUBLANES - 1:SUBLANES, :], h_c.shape)
    hlast_ref[...] = h_tail
    lru_out = hs * jax.nn.gelu(gate)

    cos = cos_ref[...]
    sin = sin_ref[...]
    lane = lax.broadcasted_iota(jnp.int32, (1, LANES), 1)
    first_half = (lane % HEAD_DIM) < (HEAD_DIM // 2)
    qcols = [_rope128(proj[:, o2 + c * LANES:o2 + (c + 1) * LANES], cos, sin, first_half) * (HEAD_DIM ** -0.5)
             for c in range(4)]
    k = _rope128(proj[:, o2 + Q_WIDTH:o2 + Q_WIDTH + KV_WIDTH], cos, sin, first_half)
    v = proj[:, o2 + Q_WIDTH + KV_WIDTH:]
    k_ext = jnp.concatenate([k_c[...], k], axis=0).astype(BF16)
    v_ext = jnp.concatenate([v_c[...], v], axis=0).astype(BF16)
    k_c[...] = k[ts - WINDOW:]
    v_c[...] = v[ts - WINDOW:]
    klast_ref[...] = k[ts - WINDOW:]
    vlast_ref[...] = v[ts - WINDOW:]

    qi = lax.broadcasted_iota(jnp.int32, (WINDOW, 2 * WINDOW), 0)
    kj = lax.broadcasted_iota(jnp.int32, (WINDOW, 2 * WINDOW), 1)
    band = jnp.logical_and(kj > qi, kj <= qi + WINDOW)
    lane_lo = lane < HEAD_DIM
    grow = lax.broadcasted_iota(jnp.int32, (GROUP * WINDOW, 1), 0) // WINDOW
    attn_cols = [[] for _ in range(4)]
    for blk in range(ts // WINDOW):
        if seq_start is None or blk > 0:
            mask = band
        else:
            mask = jnp.logical_and(band, jnp.logical_or(kj >= WINDOW, jnp.logical_not(seq_start)))
        mask4 = jnp.concatenate([mask] * GROUP, axis=0)
        kb = k_ext[blk * WINDOW:(blk + 2) * WINDOW]
        vb = v_ext[blk * WINDOW:(blk + 2) * WINDOW]
        outs = []
        for kv in range(N_KV_HEADS):
            sel = lane_lo if kv == 0 else jnp.logical_not(lane_lo)
            qs = jnp.concatenate(
                [jnp.where(sel, qc[blk * WINDOW:(blk + 1) * WINDOW], 0.0) for qc in qcols], axis=0).astype(BF16)
            s = lax.dot_general(qs, kb, (((1,), (1,)), ((), ())), preferred_element_type=F32)
            s = jnp.where(mask4, s, NEG_BIG)
            sink_col = jnp.zeros((GROUP * WINDOW, 1), F32)
            for g in range(GROUP):
                sink_col = jnp.where(grow == g, sinks_ref[kv * GROUP + g], sink_col)
            p, denom = _softmax_sink_pv(s, sink_col, vb)
            outs.append(jnp.dot(p.astype(BF16), vb, preferred_element_type=F32) / denom)
        for c in range(4):
            attn_cols[c].append(jnp.where(lane_lo, outs[0][c * WINDOW:(c + 1) * WINDOW],
                                          outs[1][c * WINDOW:(c + 1) * WINDOW]))
    attn = jnp.concatenate([jnp.concatenate(cols, axis=0) for cols in attn_cols], axis=1)

    mix = jnp.concatenate([lru_out, attn], axis=1)
    x1, h2, logits = _post_mix(x, mix, (mod[2:3], mod[3:4], mod[4:5]), w)
    x1_ref[...] = x1
    h2_ref[...] = h2
    route, tot = _route_tile(logits)
    lg_ref[0][...] = route
    lg_ref[1][...] = jnp.broadcast_to(tot, lg_ref[1].shape)


def _expand_rows(m, t):
    b, wd = m.shape
    return jnp.broadcast_to(m[:, None, :], (b, t, wd)).reshape(b * t, wd)


def _sample_body(x_ref, mod_ref, cos_ref, sin_ref, sinks_ref, h0_ref, cprev_ref, ck_ref, cv_ref, w,
                 x1_ref, h2_ref, lg_ref, g2_ref, hs_ref, u_ref, ko_ref, vo_ref):
    bt_, t = SAMPLE_BT, SUBLANES
    rows = bt_ * t

    x = x_ref[...]
    mods = [_expand_rows(mod_ref[i], t) for i in range(6)]
    proj = _in_proj(x, (mods[0], mods[1]), w)
    u = proj[:, :LRU_WIDTH]
    gate = proj[:, LRU_WIDTH:2 * LRU_WIDTH]
    o2 = 2 * LRU_WIDTH
    u_ref[...] = u

    rowid = lax.broadcasted_iota(jnp.int32, (rows, 1), 0) % t
    cprev = cprev_ref[...]
    taps = []
    for d in (1, 2, 3):
        taps.append(jnp.where(rowid >= d, pltpu.roll(u, d, axis=0),
                              pltpu.roll(cprev, (d - (CONV_W - 1)) % rows, axis=0)))
    uc = _conv_taps(u, taps[0], taps[1], taps[2], w)

    a, bt = _lru_coeffs(uc, w, None)
    bt = bt + a * h0_ref[...]
    _, hs = _group_scan(a, bt)
    hs_ref[...] = hs
    lru_out = hs * jax.nn.gelu(gate)

    cos = cos_ref[...]
    sin = sin_ref[...]
    lane = lax.broadcasted_iota(jnp.int32, (1, LANES), 1)
    first_half = (lane % HEAD_DIM) < (HEAD_DIM // 2)
    qcols = [_rope128(proj[:, o2 + c * LANES:o2 + (c + 1) * LANES], cos, sin, first_half) * (HEAD_DIM ** -0.5)
             for c in range(4)]
    k = _rope128(proj[:, o2 + Q_WIDTH:o2 + Q_WIDTH + KV_WIDTH], cos, sin, first_half)
    v = proj[:, o2 + Q_WIDTH + KV_WIDTH:]
    k3 = k.reshape(bt_, t, KV_WIDTH)
    v3 = v.reshape(bt_, t, KV_WIDTH)
    ck = ck_ref[...]
    cv = cv_ref[...]
    ko_ref[:, :WINDOW - t, :] = ck[:, t:, :]
    ko_ref[:, WINDOW - t:, :] = k3
    vo_ref[:, :WINDOW - t, :] = cv[:, t:, :]
    vo_ref[:, WINDOW - t:, :] = v3

    ckb, cvb, k3b, v3b = ck.astype(BF16), cv.astype(BF16), k3.astype(BF16), v3.astype(BF16)
    lane_lo = lane < HEAD_DIM
    gq = GROUP * t
    tq = lax.broadcasted_iota(jnp.int32, (1, gq, 1), 1) % t
    mask_c = lax.broadcasted_iota(jnp.int32, (1, gq, WINDOW), 2) > tq
    mask_n = lax.broadcasted_iota(jnp.int32, (1, gq, t), 2) <= tq
    grow = lax.broadcasted_iota(jnp.int32, (1, gq, 1), 1) // t
    bdims = (((2,), (2,)), ((0,), (0,)))
    pdims = (((2,), (1,)), ((0,), (0,)))
    outs = []
    for kv in range(N_KV_HEADS):
        sel = lane_lo if kv == 0 else jnp.logical_not(lane_lo)
        q3 = jnp.concatenate([jnp.where(sel, qc, 0.0).reshape(bt_, t, LANES) for qc in qcols], axis=1).astype(BF16)
        sc = lax.dot_general(q3, ckb, bdims, preferred_element_type=F32)
        sn = lax.dot_general(q3, k3b, bdims, preferred_element_type=F32)
        sc = jnp.where(mask_c, sc, NEG_BIG)
        sn = jnp.where(mask_n, sn, NEG_BIG)
        sink_col = jnp.zeros((1, gq, 1), F32)
        for g in range(GROUP):
            sink_col = jnp.where(grow == g, sinks_ref[kv * GROUP + g], sink_col)
        m = jnp.maximum(jnp.maximum(jnp.max(sc, axis=-1, keepdims=True), jnp.max(sn, axis=-1, keepdims=True)),
                        sink_col)
        pc = jnp.exp(sc - m)
        pn = jnp.exp(sn - m)
        denom = jnp.sum(pc, axis=-1, keepdims=True) + jnp.sum(pn, axis=-1, keepdims=True) + jnp.exp(sink_col - m)
        o = (lax.dot_general(pc.astype(BF16), cvb, pdims, preferred_element_type=F32)
             + lax.dot_general(pn.astype(BF16), v3b, pdims, preferred_element_type=F32)) / denom
        outs.append(o)
    attn = jnp.concatenate(
        [jnp.where(lane_lo, outs[0][:, c * t:(c + 1) * t, :], outs[1][:, c * t:(c + 1) * t, :]).reshape(rows, LANES)
         for c in range(4)], axis=1)

    mix = jnp.concatenate([lru_out, attn], axis=1)
    x1, h2, logits = _post_mix(x, mix, (mods[2], mods[3], mods[4]), w)
    x1_ref[...] = x1
    h2_ref[...] = h2
    route, tot = _route_tile(logits)
    lg_ref[0][...] = route
    lg_ref[1][...] = jnp.broadcast_to(tot, lg_ref[1].shape)
    g2_ref[...] = mods[5]


def _prompt_kernel(steps_per_seq, x_ref, mod_ref, cos_ref, sin_ref, sinks_ref, *rest):
    nw = len(WEIGHT_NAMES)
    w = dict(zip(WEIGHT_NAMES, rest[:nw]))
    (x1_ref, h2_ref, route_ref, cnt_ref, hlast_ref, ulast_ref, klast_ref, vlast_ref,
     conv_c, h_c, k_c, v_c) = rest[nw:]
    seq_start = pl.program_id(0) % steps_per_seq == 0
    for sub in range(MIX_TILES):
        rows = pl.ds(sub * SEQ_TILE, SEQ_TILE)
        _prompt_body(seq_start if sub == 0 else None, x_ref.at[rows], mod_ref, cos_ref.at[rows], sin_ref.at[rows],
                     sinks_ref, w, x1_ref.at[rows], h2_ref.at[rows], (route_ref.at[rows], cnt_ref.at[sub]),
                     hlast_ref, ulast_ref, klast_ref, vlast_ref, conv_c, h_c, k_c, v_c)


def _sample_kernel(x_ref, mod_ref, cos_ref, sin_ref, sinks_ref, h0_ref, cprev_ref, ck_ref, cv_ref, *rest):
    nw = len(WEIGHT_NAMES)
    w = dict(zip(WEIGHT_NAMES, rest[:nw]))
    x1_ref, h2_ref, route_ref, cnt_ref, g2_ref, hs_ref, u_ref, ko_ref, vo_ref = rest[nw:]
    _sample_body(x_ref, mod_ref, cos_ref, sin_ref, sinks_ref, h0_ref, cprev_ref, ck_ref, cv_ref, w,
                 x1_ref, h2_ref, (route_ref, cnt_ref), g2_ref, hs_ref, u_ref, ko_ref, vo_ref)


PACK_W = D_MODEL // 2
U32 = jnp.uint32


def _pack_bf16_pairs(x):
    xb = x.astype(BF16).astype(F32)
    lo = lax.bitcast_convert_type(xb[:, :PACK_W], U32)
    hi = lax.bitcast_convert_type(xb[:, PACK_W:], U32)
    return jnp.bitwise_or(jnp.bitwise_and(hi, jnp.uint32(0xFFFF0000)), lax.shift_right_logical(lo, jnp.uint32(16)))


def _unpack_bf16_pairs(w):
    lo = lax.bitcast_convert_type(lax.shift_left(w, jnp.uint32(16)), F32).astype(BF16)
    hi = lax.bitcast_convert_type(jnp.bitwise_and(w, jnp.uint32(0xFFFF0000)), F32).astype(BF16)
    return lo, hi


STRIP_SIZES = tuple(SUBLANES << b for b in range(6))
STRIP_LARGE = 64
SORT_ROWS = SEQ_TILE * TOP_K + N_EXPERTS * SUBLANES
TILE_WAIT_SIZES = tuple(SUBLANES << b for b in range(8))


def _for_strips(cnt_ref, off_ref, tile, buf_slot, hbm, sem, to_hbm, act):
    def e_body(e, local):
        n = cnt_ref[tile * N_EXPERTS + e]
        glob = off_ref[tile * N_EXPERTS + e]

        def pieces(sizes, done):
            for p in sizes:
                piece = n & p
                lo = pl.ds(pl.multiple_of(local + done, SUBLANES), p)
                gl = pl.ds(pl.multiple_of(glob + done, SUBLANES), p)

                @pl.when(piece != 0)
                def _():
                    if to_hbm:
                        act(pltpu.make_async_copy(buf_slot.at[lo], hbm.at[gl], sem))
                    else:
                        act(pltpu.make_async_copy(hbm.at[gl], buf_slot.at[lo], sem))
                done = done + piece

        large = tuple(p for p in reversed(STRIP_SIZES) if p >= STRIP_LARGE)
        small = tuple(p for p in reversed(STRIP_SIZES) if p < STRIP_LARGE)
        n_large = n & (-STRIP_LARGE)

        @pl.when(n_large != 0)
        def _():
            pieces(large, 0)
        pieces(small, n_large)
        return local + n
    lax.fori_loop(0, N_EXPERTS, e_body, 0)


def _wait_tile_rows(total, buf_slot, hbm, sem, to_hbm):
    for p in TILE_WAIT_SIZES:
        @pl.when((total & p) != 0)
        def _():
            if to_hbm:
                pltpu.make_async_copy(buf_slot.at[pl.ds(0, p)], hbm.at[pl.ds(0, p)], sem).wait()
            else:
                pltpu.make_async_copy(hbm.at[pl.ds(0, p)], buf_slot.at[pl.ds(0, p)], sem).wait()


def _dispatch_kernel(n_prompt_tiles, cnt_ref, off_ref, tot_ref, meta_ref, h2p_ref, h2s_ref, routep_ref, routes_ref,
                     xs_hbm, sbuf, zblk, sem, zsem):
    i = pl.program_id(0)
    nb = pl.num_programs(0)
    slot = i % 2
    n_blocks = xs_hbm.shape[0] // MOE_TM

    is_prompt = i < n_prompt_tiles
    h2 = jnp.where(is_prompt, h2p_ref[...], h2s_ref[...])
    route_t = jnp.where(is_prompt, routep_ref[...], routes_ref[...]).T
    r_pos = lax.broadcasted_iota(jnp.int32, (SORT_ROWS, SEQ_TILE), 0).astype(F32)
    perm = jnp.zeros((SORT_ROWS, SEQ_TILE), F32)
    for k in range(TOP_K):
        perm = perm + jnp.where(r_pos == route_t[ROUTE_R + k:ROUTE_R + k + 1, :], 1.0, 0.0)
    sbuf[slot] = _pack_bf16_pairs(jnp.dot(perm.astype(BF16), h2.astype(BF16), preferred_element_type=F32))

    _for_strips(cnt_ref, off_ref, i, sbuf.at[slot], xs_hbm, sem.at[slot], True, lambda cp: cp.start())

    @pl.when(i > 0)
    def _():
        _wait_tile_rows(tot_ref[jnp.maximum(i - 1, 0)], sbuf.at[1 - slot], xs_hbm, sem.at[1 - slot], True)

    @pl.when(i == nb - 1)
    def _():
        _wait_tile_rows(tot_ref[i], sbuf.at[slot], xs_hbm, sem.at[slot], True)
        zblk[...] = jnp.zeros_like(zblk)

        def for_region_tails(act):
            def e_body(e, carry):
                start = meta_ref[e]
                n = meta_ref[N_EXPERTS + e] - start
                done = 0
                for p in reversed(STRIP_SIZES[:-1]):
                    piece = n & p
                    rows = pl.ds(pl.multiple_of(start + done, SUBLANES), p)

                    @pl.when(piece != 0)
                    def _():
                        act(pltpu.make_async_copy(zblk.at[pl.ds(0, p)], xs_hbm.at[rows], zsem.at[0]))
                    done = done + piece
                return carry
            lax.fori_loop(0, N_EXPERTS, e_body, 0)

        def for_tail_blocks(act):
            def b_body(j, carry):
                act(pltpu.make_async_copy(zblk, xs_hbm.at[pl.ds(pl.multiple_of(j * MOE_TM, MOE_TM), MOE_TM)],
                                          zsem.at[0]))
                return carry
            lax.fori_loop(meta_ref[2 * N_EXPERTS], n_blocks, b_body, 0)

        for_region_tails(lambda cp: cp.start())
        for_tail_blocks(lambda cp: cp.start())
        for_region_tails(lambda cp: cp.wait())
        for_tail_blocks(lambda cp: cp.wait())


def _dispatch(cnt8, tile_off, tot8, meta, h2_p, h2_s, route_p, route_s, n_rows):
    npt = h2_p.shape[0] // SEQ_TILE
    p_tile = lambda i, *_: (jnp.minimum(i, npt - 1), 0)
    s_tile = lambda i, *_: (jnp.maximum(i - npt, 0), 0)
    nt = tot8.shape[0]
    tt = SEQ_TILE
    grid_spec = pltpu.PrefetchScalarGridSpec(
        num_scalar_prefetch=4,
        grid=(nt,),
        in_specs=[pl.BlockSpec((tt, D_MODEL), p_tile), pl.BlockSpec((tt, D_MODEL), s_tile),
                  pl.BlockSpec((tt, LANES), p_tile), pl.BlockSpec((tt, LANES), s_tile)],
        out_specs=pl.BlockSpec(memory_space=pl.ANY),
        scratch_shapes=[pltpu.VMEM((2, SORT_ROWS, PACK_W), U32), pltpu.VMEM((MOE_TM, PACK_W), U32),
                        pltpu.SemaphoreType.DMA((2,)), pltpu.SemaphoreType.DMA((1,))],
    )
    return pl.pallas_call(
        functools.partial(_dispatch_kernel, npt),
        grid_spec=grid_spec,
        out_shape=jax.ShapeDtypeStruct((n_rows, PACK_W), U32),
        compiler_params=pltpu.CompilerParams(dimension_semantics=("arbitrary",), vmem_limit_bytes=VMEM_LIMIT),
        name="dispatch",
    )(cnt8, tile_off, tot8, meta, h2_p, h2_s, route_p, route_s)


def _expert_mlp(words, w1b, b1_ref, w2b, b2_ref, act_ref):
    xb = jnp.concatenate(_unpack_bf16_pairs(words), axis=1)
    q = D_FF // 4
    for c in range(4):
        zg = jnp.dot(xb, w1b[:, c * q:(c + 1) * q], preferred_element_type=F32) + b1_ref[:, c * q:(c + 1) * q]
        zl = (jnp.dot(xb, w1b[:, D_FF + c * q:D_FF + (c + 1) * q], preferred_element_type=F32)
              + b1_ref[:, D_FF + c * q:D_FF + (c + 1) * q])
        glu = jnp.minimum(zg, SWIGLU_LIMIT)
        lin = jnp.clip(zl, -SWIGLU_LIMIT, SWIGLU_LIMIT)
        act_ref[:, c * q:(c + 1) * q] = (glu * jax.nn.sigmoid(SWIGLU_ALPHA * glu) * (lin + 1.0)).astype(BF16)
    return _pack_bf16_pairs(jnp.dot(act_ref[...], w2b[...], preferred_element_type=F32) + b2_ref[...])


def _moe_kernel(row_ref, size_ref, exp_ref, wt_ref, meta_ref, xs_hbm, w1_hbm, b1_ref, w2_hbm, b2_ref, ys_hbm,
                xin, yout, w1f, w2f, w1b, w2b, act, isem, osem, wsem):
    i = pl.program_id(0)
    n = pl.num_programs(0)
    slot = i % 2
    tm = MOE_TM
    n_blocks = ys_hbm.shape[0] // tm

    def for_chunk(j, fn):
        r = pl.multiple_of(row_ref[j], tm)
        for code in MOE_CHUNK_CODES:
            @pl.when(size_ref[j] == code)
            def _():
                fn(r, code * tm)

    def in_copy(r, rows, s):
        return pltpu.make_async_copy(xs_hbm.at[pl.ds(r, rows)], xin.at[s, pl.ds(0, rows)], isem.at[s])

    def out_copy(r, rows, s):
        return pltpu.make_async_copy(yout.at[s, pl.ds(0, rows)], ys_hbm.at[pl.ds(r, rows)], osem.at[s])

    def start_in(j, s):
        for_chunk(j, lambda r, rows: in_copy(r, rows, s).start())

    def wait_in(j, s):
        for_chunk(j, lambda r, rows: in_copy(r, rows, s).wait())

    def start_out(j, s):
        for_chunk(j, lambda r, rows: out_copy(r, rows, s).start())

    def wait_out(j, s):
        for_chunk(j, lambda r, rows: out_copy(r, rows, s).wait())

    @pl.when(i == 0)
    def _():
        start_in(0, 0)

    @pl.when(i + 1 < n)
    def _():
        start_in(i + 1, 1 - slot)

    @pl.when(i >= 2)
    def _():
        wait_out(i - 2, slot)

    def weight_copies(e, ws):
        return (pltpu.make_async_copy(w1_hbm.at[e], w1f.at[ws], wsem.at[ws]),
                pltpu.make_async_copy(w2_hbm.at[e], w2f.at[ws], wsem.at[ws]))

    n_steps = n
    @pl.when(wt_ref[i] == 1)
    def _():
        ws = wt_ref[n_steps + i]
        nxt = wt_ref[2 * n_steps + i]

        @pl.when(i == 0)
        def _():
            for cp in weight_copies(exp_ref[0], 0):
                cp.start()

        for cp in weight_copies(exp_ref[i], ws):
            cp.wait()

        @pl.when(nxt >= 0)
        def _():
            for cp in weight_copies(nxt, 1 - ws):
                cp.start()

        chunk = 128
        def cast_body(c, carry):
            k0 = pl.multiple_of(c * chunk, chunk)
            w1b[pl.ds(k0, chunk), :] = w1f[ws, pl.ds(k0, chunk), :].astype(BF16)
            w2b[pl.ds(k0, chunk), :] = w2f[ws, pl.ds(k0, chunk), :].astype(BF16)
            return carry
        lax.fori_loop(0, D_MODEL // chunk, cast_body, 0)

    wait_in(i, slot)

    for code in MOE_CHUNK_CODES:
        rows = code * tm

        @pl.when(size_ref[i] == code)
        def _():
            yout[slot, 0:rows] = _expert_mlp(xin[slot, 0:rows], w1b, b1_ref.at[exp_ref[i]], w2b,
                                             b2_ref.at[exp_ref[i]], act.at[pl.ds(0, rows)])

    start_out(i, slot)

    @pl.when(i == n - 1)
    def _():
        @pl.when(i >= 1)
        def _():
            wait_out(i - 1, 1 - slot)
        wait_out(i, slot)
        yout[1, 0:tm] = jnp.zeros((tm, PACK_W), U32)

        def zero_block(j):
            return pltpu.make_async_copy(yout.at[1, pl.ds(0, tm)], ys_hbm.at[pl.ds(pl.multiple_of(j * tm, tm), tm)],
                                         osem.at[1])

        def start_body(j, carry):
            zero_block(j).start()
            return carry

        def wait_body(j, carry):
            zero_block(j).wait()
            return carry
        lax.fori_loop(meta_ref[2 * N_EXPERTS], n_blocks, start_body, 0)
        lax.fori_loop(meta_ref[2 * N_EXPERTS], n_blocks, wait_body, 0)


def _moe(chunk_row, chunk_size, chunk_exp, weight_tbl, meta, xs, w1, b1, w2, b2):
    whole = lambda i, *_: (0, 0, 0)
    grid_spec = pltpu.PrefetchScalarGridSpec(
        num_scalar_prefetch=5,
        grid=(chunk_row.shape[0],),
        in_specs=[
            pl.BlockSpec(memory_space=pl.ANY),
            pl.BlockSpec(memory_space=pl.ANY),
            pl.BlockSpec((N_EXPERTS, 1, 2 * D_FF), whole),
            pl.BlockSpec(memory_space=pl.ANY),
            pl.BlockSpec((N_EXPERTS, 1, D_MODEL), whole),
        ],
        out_specs=pl.BlockSpec(memory_space=pl.ANY),
        scratch_shapes=[pltpu.VMEM((2, MOE_CH, PACK_W), U32), pltpu.VMEM((2, MOE_CH, PACK_W), U32),
                        pltpu.VMEM((2, D_MODEL, 2 * D_FF), F32), pltpu.VMEM((2, D_FF, D_MODEL), F32),
                        pltpu.VMEM((D_MODEL, 2 * D_FF), BF16), pltpu.VMEM((D_FF, D_MODEL), BF16),
                        pltpu.VMEM((MOE_CH, D_FF), BF16),
                        pltpu.SemaphoreType.DMA((2,)), pltpu.SemaphoreType.DMA((2,)), pltpu.SemaphoreType.DMA((2,))],
    )
    return pl.pallas_call(
        _moe_kernel,
        grid_spec=grid_spec,
        out_shape=jax.ShapeDtypeStruct(xs.shape, U32),
        compiler_params=pltpu.CompilerParams(dimension_semantics=("arbitrary",), vmem_limit_bytes=VMEM_LIMIT),
        name="moe",
    )(chunk_row, chunk_size, chunk_exp, weight_tbl, meta, xs, w1, b1, w2, b2)


def _combine_kernel(n_prompt_tiles, cnt_ref, off_ref, tot_ref, ys_hbm, routep_ref, routes_ref, x1p_ref, x1s_ref,
                    modp_ref, g2s_ref, fg_ref, op_ref, os_ref, buf, sem):
    i = pl.program_id(0)
    nb = pl.num_programs(0)
    slot = i % 2

    def fetch(tile, s):
        _for_strips(cnt_ref, off_ref, tile, buf.at[s], ys_hbm, sem.at[s], False, lambda cp: cp.start())

    @pl.when(i == 0)
    def _():
        buf[...] = jnp.zeros_like(buf)
        fetch(0, 0)

    @pl.when(i + 1 < nb)
    def _():
        fetch(i + 1, 1 - slot)

    _wait_tile_rows(tot_ref[i], buf.at[slot], ys_hbm, sem.at[slot], False)

    is_prompt = i < n_prompt_tiles
    route = jnp.where(is_prompt, routep_ref[...], routes_ref[...])
    c_pos = lax.broadcasted_iota(jnp.int32, (SEQ_TILE, SORT_ROWS), 1).astype(F32)
    gmat = jnp.zeros((SEQ_TILE, SORT_ROWS), F32)
    for k in range(TOP_K):
        gmat = gmat + jnp.where(c_pos == route[:, ROUTE_R + k:ROUTE_R + k + 1],
                                route[:, ROUTE_G + k:ROUTE_G + k + 1], 0.0)
    g_bf = gmat.astype(BF16)
    ff = jnp.concatenate([jnp.dot(g_bf, yb, preferred_element_type=F32) for yb in _unpack_bf16_pairs(buf[slot])],
                         axis=1)
    g2 = jnp.where(i < n_prompt_tiles, modp_ref[5:6, :], g2s_ref[...])
    x = jnp.where(is_prompt, x1p_ref[...], x1s_ref[...]) + g2 * ff
    y = _rms(x, fg_ref[...])

    @pl.when(i < n_prompt_tiles)
    def _():
        op_ref[...] = y

    @pl.when(i >= n_prompt_tiles)
    def _():
        os_ref[...] = y


def _combine(cnt8, tile_off, tot8, ys, route_p, route_s, x1_p, x1_s, mod_p, g2_rows, final_g, tiles_per_batch):
    nt = tot8.shape[0]
    tt = SEQ_TILE
    npt = x1_p.shape[0] // tt
    p_tile = lambda i, *_: (jnp.minimum(i, npt - 1), 0)
    s_tile = lambda i, *_: (jnp.maximum(i - npt, 0), 0)
    grid_spec = pltpu.PrefetchScalarGridSpec(
        num_scalar_prefetch=3,
        grid=(nt,),
        in_specs=[
            pl.BlockSpec(memory_space=pl.ANY),
            pl.BlockSpec((tt, LANES), p_tile), pl.BlockSpec((tt, LANES), s_tile),
            pl.BlockSpec((tt, D_MODEL), p_tile), pl.BlockSpec((tt, D_MODEL), s_tile),
            pl.BlockSpec((None, 6, D_MODEL), lambda i, *_: (jnp.minimum(i, npt - 1) // tiles_per_batch, 0, 0)),
            pl.BlockSpec((tt, D_MODEL), s_tile),
            pl.BlockSpec((1, D_MODEL), lambda i, *_: (0, 0)),
        ],
        out_specs=(pl.BlockSpec((tt, D_MODEL), p_tile), pl.BlockSpec((tt, D_MODEL), s_tile)),
        scratch_shapes=[pltpu.VMEM((2, SORT_ROWS, PACK_W), U32), pltpu.SemaphoreType.DMA((2,))],
    )
    return pl.pallas_call(
        functools.partial(_combine_kernel, npt),
        grid_spec=grid_spec,
        out_shape=(jax.ShapeDtypeStruct((npt * tt, D_MODEL), F32),
                   jax.ShapeDtypeStruct(((nt - npt) * tt, D_MODEL), F32)),
        compiler_params=pltpu.CompilerParams(dimension_semantics=("arbitrary",), vmem_limit_bytes=VMEM_LIMIT),
        name="combine",
    )(cnt8, tile_off, tot8, ys, route_p, route_s, x1_p, x1_s, mod_p, g2_rows, final_g)


def _block_diag_halves(wg):
    halves = []
    for hh in range(2):
        rows = []
        for bi in range(4):
            row = [wg[hh * 4 + bi] if bj == bi else jnp.zeros((LRU_BLOCK_W, LRU_BLOCK_W), wg.dtype) for bj in range(4)]
            rows.append(jnp.concatenate(row, axis=1))
        halves.append(jnp.concatenate(rows, axis=0))
    return jnp.stack(halves).astype(BF16)


def _rope_tables(pos):
    half = HEAD_DIM // 2
    inv = ROPE_THETA ** (-jnp.arange(half, dtype=F32) / half)
    ang = pos.astype(F32)[:, None] * inv[None, :]
    cos = jnp.cos(ang)
    sin = jnp.sin(ang)
    cos128 = jnp.concatenate([cos, cos, cos, cos], axis=1)
    sin128 = jnp.concatenate([-sin, sin, -sin, sin], axis=1)
    return cos128, sin128


def _full_spec(arr, grid_rank):
    zeros = (0,) * arr.ndim
    if grid_rank == 1:
        return pl.BlockSpec(arr.shape, lambda i: zeros)
    return pl.BlockSpec(arr.shape, lambda b, j: zeros)


def kernel(x_prompt, x_sample, state_lru_h, state_conv, cache_win_k, cache_win_v, c_prompt, c_sample, w_ada, b_ada, norm1_g, w_in, b_in, conv_w, conv_b, lru_wa, lru_ba, lru_wx, lru_bx, lru_lambda, attn_sinks, w_out, b_out, norm2_g, w_router, b_router, w1, b1, w2, b2, final_g):
    bp, seq, _ = x_prompt.shape
    bd, tdec, _ = x_sample.shape
    assert tdec == SUBLANES and seq % (MIX_TILES * SEQ_TILE) == 0 and bd % SAMPLE_BT == 0
    assert SAMPLE_BT * tdec == SEQ_TILE
    n_prompt = bp * seq
    n_sample = bd * tdec
    n_tok = n_prompt + n_sample
    l = 0

    head_perm = [h for c in range(4) for h in (c, GROUP + c)]
    o2 = 2 * LRU_WIDTH

    def permute_heads(arr, axis, start):
        take = lambda a, b: lax.slice_in_dim(arr, a, b, axis=axis)
        heads = [take(start + h * HEAD_DIM, start + (h + 1) * HEAD_DIM) for h in head_perm]
        return jnp.concatenate([take(0, start)] + heads + [take(start + Q_WIDTH, arr.shape[axis])], axis=axis)

    sinks_perm = attn_sinks[l]

    wr = jnp.tile(w_router[l], (1, TOP_K))
    wr_hi = wr.astype(BF16)
    weights = dict(
        norm1_g=norm1_g[l][None, :], w_in=permute_heads(w_in[l], 1, o2).astype(BF16),
        b_in=permute_heads(b_in[l], 0, o2)[None, :],
        conv_w=conv_w[l], conv_b=conv_b[l][None, :],
        ga=_block_diag_halves(lru_wa[l]), gx=_block_diag_halves(lru_wx[l]),
        lru_ba=lru_ba[l][None, :], lru_bx=lru_bx[l][None, :], lam=lru_lambda[l][None, :],
        w_out=permute_heads(w_out[l], 0, LRU_WIDTH).astype(BF16), b_out=b_out[l][None, :], norm2_g=norm2_g[l][None, :],
        wr_hi=wr_hi, wr_lo=(wr - wr_hi.astype(F32)).astype(BF16),
        b_router=jnp.tile(b_router[l], TOP_K)[None, :],
    )
    wlist = [weights[n] for n in WEIGHT_NAMES]

    mod_all = _ada(jnp.concatenate([c_prompt, c_sample], axis=0), w_ada[l], b_ada[l][None, :])
    mod_p = mod_all[:bp].reshape(bp, 6, D_MODEL)
    mod_s = mod_all[bp:].reshape(bd, 6, D_MODEL).transpose(1, 0, 2)

    cos_p, sin_p = _rope_tables(jnp.arange(seq, dtype=jnp.int32))
    cos_s, sin_s = _rope_tables(PAST_LEN + jnp.arange(tdec, dtype=jnp.int32))
    cos_s = jnp.tile(cos_s, (SAMPLE_BT, 1))
    sin_s = jnp.tile(sin_s, (SAMPLE_BT, 1))
    h0_rows = jnp.pad(state_lru_h[l][:, None, :], ((0, 0), (0, tdec - 1), (0, 0))).reshape(n_sample, LRU_WIDTH)
    cprev_rows = jnp.pad(state_conv[l], ((0, 0), (0, tdec - (CONV_W - 1)), (0, 0))).reshape(n_sample, LRU_WIDTH)
    ck = cache_win_k[l].reshape(bd, WINDOW, KV_WIDTH)
    cv = cache_win_v[l].reshape(bd, WINDOW, KV_WIDTH)
    npt = n_prompt // SEQ_TILE
    nst = n_sample // SEQ_TILE
    mix_rows = MIX_TILES * SEQ_TILE
    steps_per_seq = seq // mix_rows
    wspecs = [_full_spec(a, 1) for a in wlist]

    rows_p = lambda width: pl.BlockSpec((mix_rows, width), lambda i: (i, 0))
    tail_p = lambda rows, width: pl.BlockSpec((None, rows, width), lambda i: (i // steps_per_seq, 0, 0))
    (x1_p, h2_p, route_p, cnt_p, hlast_p, ulast_p, klast_p, vlast_p) = pl.pallas_call(
        functools.partial(_prompt_kernel, steps_per_seq),
        grid=(n_prompt // mix_rows,),
        in_specs=[rows_p(D_MODEL),
                  pl.BlockSpec((None, 6, D_MODEL), lambda i: (i // steps_per_seq, 0, 0)),
                  pl.BlockSpec((mix_rows, LANES), lambda i: (i % steps_per_seq, 0)),
                  pl.BlockSpec((mix_rows, LANES), lambda i: (i % steps_per_seq, 0)),
                  pl.BlockSpec(memory_space=pltpu.SMEM)] + wspecs,
        out_specs=(rows_p(D_MODEL), rows_p(D_MODEL), rows_p(LANES),
                   pl.BlockSpec((MIX_TILES, SUBLANES, LANES), lambda i: (i, 0, 0)),
                   tail_p(SUBLANES, LRU_WIDTH), tail_p(SUBLANES, LRU_WIDTH),
                   tail_p(WINDOW, KV_WIDTH), tail_p(WINDOW, KV_WIDTH)),
        out_shape=(
            jax.ShapeDtypeStruct((n_prompt, D_MODEL), F32),
            jax.ShapeDtypeStruct((n_prompt, D_MODEL), F32),
            jax.ShapeDtypeStruct((n_prompt, LANES), F32),
            jax.ShapeDtypeStruct((npt, SUBLANES, LANES), F32),
            jax.ShapeDtypeStruct((bp, SUBLANES, LRU_WIDTH), F32),
            jax.ShapeDtypeStruct((bp, SUBLANES, LRU_WIDTH), F32),
            jax.ShapeDtypeStruct((bp, WINDOW, KV_WIDTH), F32),
            jax.ShapeDtypeStruct((bp, WINDOW, KV_WIDTH), F32),
        ),
        scratch_shapes=[pltpu.VMEM((SUBLANES, LRU_WIDTH), F32), pltpu.VMEM((SUBLANES, LRU_WIDTH), F32),
                        pltpu.VMEM((WINDOW, KV_WIDTH), F32), pltpu.VMEM((WINDOW, KV_WIDTH), F32)],
        compiler_params=pltpu.CompilerParams(dimension_semantics=("arbitrary",), vmem_limit_bytes=VMEM_LIMIT),
        name="prompt_mixer",
    )(x_prompt.reshape(n_prompt, D_MODEL), mod_p, cos_p, sin_p, sinks_perm, *wlist)

    rows_s = lambda width: pl.BlockSpec((SEQ_TILE, width), lambda i: (i, 0))
    cache_spec = pl.BlockSpec((SAMPLE_BT, WINDOW, KV_WIDTH), lambda i: (i, 0, 0))
    (x1_s, h2_s, route_s, cnt_s, g2_rows, hs_s, u_s, s_k, s_v) = pl.pallas_call(
        _sample_kernel,
        grid=(nst,),
        in_specs=[rows_s(D_MODEL),
                  pl.BlockSpec((6, SAMPLE_BT, D_MODEL), lambda i: (0, i, 0)),
                  pl.BlockSpec((SEQ_TILE, LANES), lambda i: (0, 0)),
                  pl.BlockSpec((SEQ_TILE, LANES), lambda i: (0, 0)),
                  pl.BlockSpec(memory_space=pltpu.SMEM),
                  rows_s(LRU_WIDTH), rows_s(LRU_WIDTH), cache_spec, cache_spec] + wspecs,
        out_specs=(rows_s(D_MODEL), rows_s(D_MODEL), rows_s(LANES),
                   pl.BlockSpec((None, SUBLANES, LANES), lambda i: (i, 0, 0)),
                   rows_s(D_MODEL), rows_s(LRU_WIDTH), rows_s(LRU_WIDTH), cache_spec, cache_spec),
        out_shape=(
            jax.ShapeDtypeStruct((n_sample, D_MODEL), F32), jax.ShapeDtypeStruct((n_sample, D_MODEL), F32),
            jax.ShapeDtypeStruct((n_sample, LANES), F32), jax.ShapeDtypeStruct((nst, SUBLANES, LANES), F32),
            jax.ShapeDtypeStruct((n_sample, D_MODEL), F32),
            jax.ShapeDtypeStruct((n_sample, LRU_WIDTH), F32),
            jax.ShapeDtypeStruct((n_sample, LRU_WIDTH), F32),
            jax.ShapeDtypeStruct((bd, WINDOW, KV_WIDTH), F32),
            jax.ShapeDtypeStruct((bd, WINDOW, KV_WIDTH), F32),
        ),
        compiler_params=pltpu.CompilerParams(dimension_semantics=("arbitrary",), vmem_limit_bytes=VMEM_LIMIT),
        name="sample_mixer",
    )(x_sample.reshape(n_sample, D_MODEL), mod_s, cos_s, sin_s, sinks_perm, h0_rows, cprev_rows, ck, cv, *wlist)
    tile_cnt = jnp.concatenate([cnt_p, cnt_s], axis=0)

    n_tiles = npt + nst
    n_assign = n_tok * TOP_K
    max_rows = n_assign + n_tiles * N_EXPERTS * (SUBLANES - 1) + N_EXPERTS * (MOE_TM - 1)
    n_blocks = -(-max_rows // MOE_TM)
    cnt = tile_cnt[:, 0, :N_EXPERTS].astype(jnp.int32)
    cnt8 = (cnt + SUBLANES - 1) // SUBLANES * SUBLANES
    counts = jnp.sum(cnt8, axis=0)
    pcounts = (counts + MOE_TM - 1) // MOE_TM * MOE_TM
    pend = jnp.cumsum(pcounts)
    pstart = pend - pcounts
    tile_off = pstart[None, :] + jnp.cumsum(cnt8, axis=0) - cnt8
    tot8 = jnp.sum(cnt8, axis=1)
    meta = jnp.concatenate([pstart + counts, pend, pend[-1:] // MOE_TM]).astype(jnp.int32)
    cnt8_flat = cnt8.reshape(-1)
    off_flat = tile_off.reshape(-1).astype(jnp.int32)

    xs = _dispatch(cnt8_flat, off_flat, tot8, meta, h2_p, h2_s, route_p, route_s, n_blocks * MOE_TM)

    n_tm = pcounts // MOE_TM
    n_big = n_tm // 4
    has2 = (n_tm // 2) % 2
    n_ch = n_big + has2 + n_tm % 2
    ch_end = jnp.cumsum(n_ch)
    ch_start = ch_end - n_ch
    n_chunks = n_blocks // 4 + 2 * N_EXPERTS
    ci = jnp.arange(n_chunks, dtype=jnp.int32)
    owner = jnp.logical_and(ch_start[None, :] <= ci[:, None], ci[:, None] < ch_end[None, :])
    pick = lambda v: jnp.sum(jnp.where(owner, v[None, :], 0), axis=1)
    live = ci < ch_end[-1]
    local = ci - pick(ch_start)
    chunk_exp = jnp.where(live, pick(jnp.arange(N_EXPERTS, dtype=jnp.int32)), N_EXPERTS - 1).astype(jnp.int32)
    big_c, has2_c = pick(n_big), pick(has2)
    is_big = local < big_c
    is_two = jnp.logical_and(local == big_c, has2_c == 1)
    blocks_before = jnp.where(is_big, local * 4, big_c * 4 + jnp.where(is_two, 0, has2_c * 2))
    chunk_row = jnp.where(live, pick(pstart) + blocks_before * MOE_TM, 0).astype(jnp.int32)
    chunk_size = jnp.where(live, jnp.where(is_big, 4, jnp.where(is_two, 2, 1)), 0).astype(jnp.int32)
    has_rows = n_ch > 0
    e_ar = jnp.arange(N_EXPERTS, dtype=jnp.int32)
    w_slot = (jnp.cumsum(has_rows.astype(jnp.int32)) - 1) % 2
    later = jnp.logical_and(has_rows[None, :], e_ar[None, :] > e_ar[:, None])
    nxt = jnp.min(jnp.where(later, e_ar[None, :], N_EXPERTS), axis=1)
    nxt = jnp.where(nxt == N_EXPERTS, -1, nxt)
    weight_tbl = jnp.concatenate([jnp.logical_and(live, local == 0).astype(jnp.int32),
                                  pick(w_slot), jnp.where(live, pick(nxt), -1)]).astype(jnp.int32)
    ys = _moe(chunk_row, chunk_size, chunk_exp, weight_tbl, meta, xs, w1[l], b1[l][:, None, :], w2[l], b2[l][:, None, :])

    y_p, y_s = _combine(cnt8_flat, off_flat, tot8, ys, route_p, route_s, x1_p, x1_s, mod_p, g2_rows,
                        final_g[None, :], seq // SEQ_TILE)

    y_prompt = y_p.reshape(bp, seq, D_MODEL)
    y_sample = y_s.reshape(bd, tdec, D_MODEL)
    p_h = hlast_p[:, SUBLANES - 1, :][None]
    p_c = ulast_p[:, SUBLANES - (CONV_W - 1):, :][None]
    p_k = klast_p.reshape(1, bp, WINDOW, N_KV_HEADS, HEAD_DIM)
    p_v = vlast_p.reshape(1, bp, WINDOW, N_KV_HEADS, HEAD_DIM)
    s_h = hs_s.reshape(bd, tdec, LRU_WIDTH)[:, tdec - 1, :][None]
    s_c = u_s.reshape(bd, tdec, LRU_WIDTH)[:, tdec - (CONV_W - 1):, :][None]
    s_kk = s_k.reshape(1, bd, WINDOW, N_KV_HEADS, HEAD_DIM)
    s_vv = s_v.reshape(1, bd, WINDOW, N_KV_HEADS, HEAD_DIM)
    return (y_prompt, y_sample, p_h, p_c, p_k, p_v, s_h, s_c, s_kk, s_vv)
```

```python
import functools

import jax
import jax.numpy as jnp
from jax import lax
from jax.experimental import pallas as pl
from jax.experimental.pallas import tpu as pltpu

F32 = jnp.float32
BF16 = jnp.bfloat16

D_MODEL = 1024
LRU_WIDTH = 512
LRU_BLOCKS = 8
LRU_BLOCK_W = LRU_WIDTH // LRU_BLOCKS
CONV_W = 4
LRU_C = 8.0
HEAD_DIM = 64
N_HEADS = 8
N_KV_HEADS = 2
GROUP = N_HEADS // N_KV_HEADS
WINDOW = 128
ROPE_THETA = 10000.0
N_EXPERTS = 32
TOP_K = 4
D_FF = D_MODEL
SWIGLU_LIMIT = 7.0
SWIGLU_ALPHA = 1.702
NORM_EPS = 1e-5
PAST_LEN = 8192
Q_WIDTH = N_HEADS * HEAD_DIM
KV_WIDTH = N_KV_HEADS * HEAD_DIM
IN_WIDTH = 2 * LRU_WIDTH + Q_WIDTH + 2 * KV_WIDTH

LANES = 128
SUBLANES = 8
SEQ_TILE = 256
MIX_TILES = 2
SAMPLE_BT = 32
MOE_TM = 256
MOE_CHUNK_CODES = (4, 2, 1)
MOE_CH = MOE_CHUNK_CODES[0] * MOE_TM
NEG_BIG = -1e30
VMEM_LIMIT = 56 * 1024 * 1024


def _rms(x, g):
    return x * lax.rsqrt(jnp.mean(x * x, axis=-1, keepdims=True) + NORM_EPS) * g


def _rms_mod(x, gain, shift):
    return x * lax.rsqrt(jnp.mean(x * x, axis=-1, keepdims=True) + NORM_EPS) * gain + shift


def _group_scan(a, b):
    rows, width = a.shape
    groups = rows // SUBLANES
    a3 = a.reshape(groups, SUBLANES, width)
    b3 = b.reshape(groups, SUBLANES, width)
    t = lax.broadcasted_iota(jnp.int32, (1, SUBLANES, 1), 1)
    d = 1
    while d < SUBLANES:
        keep = t >= d
        a_s = jnp.where(keep, pltpu.roll(a3, d, axis=1), 1.0)
        b_s = jnp.where(keep, pltpu.roll(b3, d, axis=1), 0.0)
        b3 = a3 * b_s + b3
        a3 = a3 * a_s
        d *= 2
    return a3.reshape(rows, width), b3.reshape(rows, width)


def _chain_groups(a_grp, b_grp, h_in):
    rows = a_grp.shape[0]
    out = []
    carry = h_in
    for g in range(rows // SUBLANES):
        sl = slice(g * SUBLANES, (g + 1) * SUBLANES)
        hg = b_grp[sl] + a_grp[sl] * carry
        out.append(hg)
        carry = hg[SUBLANES - 1:SUBLANES]
    return jnp.concatenate(out, axis=0)


def _rope128(x, cos, sin_signed, first_half):
    sw = jnp.where(first_half, pltpu.roll(x, LANES - HEAD_DIM // 2, axis=1), pltpu.roll(x, HEAD_DIM // 2, axis=1))
    return x * cos + sw * sin_signed


def _softplus(x):
    return jnp.maximum(x, 0.0) + jnp.log1p(jnp.exp(-jnp.abs(x)))


def _lru_coeffs(uc, w, first_pos_mask):
    ub = uc.astype(BF16)
    half = LRU_WIDTH // 2
    ra = jnp.concatenate([jnp.dot(ub[:, :half], w['ga'][0], preferred_element_type=F32),
                          jnp.dot(ub[:, half:], w['ga'][1], preferred_element_type=F32)], axis=1)
    rx = jnp.concatenate([jnp.dot(ub[:, :half], w['gx'][0], preferred_element_type=F32),
                          jnp.dot(ub[:, half:], w['gx'][1], preferred_element_type=F32)], axis=1)
    r = jax.nn.sigmoid(ra + w['lru_ba'][...])
    i = jax.nn.sigmoid(rx + w['lru_bx'][...])
    log_a = -LRU_C * r * _softplus(-w['lam'][...])
    a = jnp.exp(log_a)
    om = 1.0 - a * a
    mult = jnp.where(om > 0.0, om * lax.rsqrt(om), 0.0)
    if first_pos_mask is not None:
        mult = jnp.where(first_pos_mask, 1.0, mult)
    return a, mult * i * uc


def _conv_taps(u, s1, s2, s3, w):
    cw = w['conv_w']
    return w['conv_b'][...] + s3 * cw[0:1, :] + s2 * cw[1:2, :] + s1 * cw[2:3, :] + u * cw[3:4, :]


def _in_proj(x, mod, w):
    sh1, sc1 = mod
    h = _rms_mod(x, w['norm1_g'][...] * (1.0 + sc1), sh1)
    return jnp.dot(h.astype(BF16), w['w_in'][...], preferred_element_type=F32) + w['b_in'][...]


def _post_mix(x, mix, mod, w):
    g1, sh2, sc2 = mod
    x1 = x + g1 * (jnp.dot(mix.astype(BF16), w['w_out'][...], preferred_element_type=F32) + w['b_out'][...])
    h2 = _rms_mod(x1, w['norm2_g'][...] * (1.0 + sc2), sh2)
    h2_hi = h2.astype(BF16)
    h2_lo = (h2 - h2_hi.astype(F32)).astype(BF16)
    logits = (jnp.dot(h2_hi, w['wr_hi'][...], preferred_element_type=F32)
              + jnp.dot(h2_lo, w['wr_hi'][...], preferred_element_type=F32)
              + jnp.dot(h2_hi, w['wr_lo'][...], preferred_element_type=F32)) + w['b_router'][...]
    return x1, h2, logits


ROUTE_E, ROUTE_G, ROUTE_R = 0, TOP_K, 2 * TOP_K


def _lane_roll1(v, shift):
    return pltpu.roll(jnp.broadcast_to(v, (SUBLANES, LANES)), shift, axis=1)[0:1]


def _route_tile(lg):
    rows = lg.shape[0]
    lane = lax.broadcasted_iota(jnp.int32, (1, LANES), 1)
    e_of = lane % N_EXPERTS
    grp = lane // N_EXPERTS
    e_id = e_of.astype(F32)
    onehot = jnp.zeros((rows, LANES), F32)
    vals, ids = [], []
    for k in range(TOP_K):
        m = jnp.max(lg, axis=1, keepdims=True)
        idx = jnp.min(jnp.where(lg == m, e_id, float(N_EXPERTS)), axis=1, keepdims=True)
        sel = e_id == idx
        lg = jnp.where(sel, -jnp.inf, lg)
        onehot = jnp.where(jnp.logical_and(sel, grp == k), 1.0, onehot)
        vals.append(m)
        ids.append(idx)
    ex = [jnp.exp(v - vals[0]) for v in vals]
    denom = ex[0] + ex[1] + ex[2] + ex[3]

    r_i = lax.broadcasted_iota(jnp.int32, (rows, rows), 0)
    c_i = lax.broadcasted_iota(jnp.int32, (rows, rows), 1)
    strict_lower = jnp.where(r_i > c_i, 1.0, 0.0).astype(BF16)
    prefix = jnp.dot(strict_lower, onehot.astype(BF16), preferred_element_type=F32)
    cnt = jnp.sum(onehot, axis=0, keepdims=True)
    base = jnp.zeros((1, LANES), F32)
    tot = cnt
    for s in range(1, TOP_K):
        rolled = _lane_roll1(cnt, s * N_EXPERTS)
        base = base + jnp.where(lane >= s * N_EXPERTS, rolled, 0.0)
        tot = tot + rolled
    pad_cnt = jnp.floor((tot + (SUBLANES - 1.0)) * (1.0 / SUBLANES)) * SUBLANES
    inc = pad_cnt
    d = 1
    while d < N_EXPERTS:
        inc = inc + jnp.where(e_of >= d, _lane_roll1(inc, d), 0.0)
        d *= 2
    strip_start = inc - pad_cnt
    ranked = onehot * (prefix + base + strip_start)

    route = jnp.zeros((rows, LANES), F32)
    for k in range(TOP_K):
        rank_k = jnp.sum(jnp.where(grp == k, ranked, 0.0), axis=1, keepdims=True)
        route = jnp.where(lane == ROUTE_E + k, ids[k], route)
        route = jnp.where(lane == ROUTE_G + k, ex[k] / denom, route)
        route = jnp.where(lane == ROUTE_R + k, rank_k, route)
    return route, tot


def _softmax_sink_pv(s, sink_col, v_bf16):
    m = jnp.maximum(jnp.max(s, axis=-1, keepdims=True), sink_col)
    p = jnp.exp(s - m)
    denom = jnp.sum(p, axis=-1, keepdims=True) + jnp.exp(sink_col - m)
    return p, denom


WEIGHT_NAMES = ('norm1_g', 'w_in', 'b_in', 'conv_w', 'conv_b', 'ga', 'gx', 'lru_ba', 'lru_bx', 'lam',
                'w_out', 'b_out', 'norm2_g', 'wr_hi', 'wr_lo', 'b_router')


def _ada_kernel(c_ref, w_ref, b_ref, o_ref):
    c = c_ref[...]
    s = c * jax.nn.sigmoid(c)
    o_ref[...] = jnp.dot(s, w_ref[...], preferred_element_type=F32, precision=lax.Precision.HIGHEST) + b_ref[...]


def _ada(c_all, w_ada, b_ada):
    rows = c_all.shape[0]
    return pl.pallas_call(
        _ada_kernel,
        grid=(6,),
        in_specs=[pl.BlockSpec((rows, D_MODEL), lambda i: (0, 0)),
                  pl.BlockSpec((D_MODEL, D_MODEL), lambda i: (0, i)),
                  pl.BlockSpec((1, D_MODEL), lambda i: (0, i))],
        out_specs=pl.BlockSpec((rows, D_MODEL), lambda i: (0, i)),
        out_shape=jax.ShapeDtypeStruct((rows, 6 * D_MODEL), F32),
        compiler_params=pltpu.CompilerParams(dimension_semantics=("arbitrary",), vmem_limit_bytes=VMEM_LIMIT),
        name="ada",
    )(c_all, w_ada, b_ada)


def _prompt_body(seq_start, x_ref, mod_ref, cos_ref, sin_ref, sinks_ref, w, x1_ref, h2_ref, lg_ref,
                 hlast_ref, ulast_ref, klast_ref, vlast_ref, conv_c, h_c, k_c, v_c):
    ts = SEQ_TILE

    if seq_start is not None:
        @pl.when(seq_start)
        def _():
            conv_c[...] = jnp.zeros_like(conv_c)
            h_c[...] = jnp.zeros_like(h_c)
            k_c[...] = jnp.zeros_like(k_c)
            v_c[...] = jnp.zeros_like(v_c)

    x = x_ref[...]
    mod = mod_ref[...]
    proj = _in_proj(x, (mod[0:1], mod[1:2]), w)
    u = proj[:, :LRU_WIDTH]
    gate = proj[:, LRU_WIDTH:2 * LRU_WIDTH]
    o2 = 2 * LRU_WIDTH

    rowid = lax.broadcasted_iota(jnp.int32, (ts, 1), 0)
    u_ext = jnp.concatenate([conv_c[...], u], axis=0)
    s1, s2, s3 = (pltpu.roll(u_ext, d, axis=0)[SUBLANES:] for d in (1, 2, 3))
    uc = _conv_taps(u, s1, s2, s3, w)
    conv_c[...] = u[ts - SUBLANES:]
    ulast_ref[...] = u[ts - SUBLANES:]

    first_pos = None if seq_start is None else jnp.logical_and(rowid == 0, seq_start)
    a, bt = _lru_coeffs(uc, w, first_pos)
    hs = _chain_groups(*_group_scan(a, bt), h_c[0:1, :])
    h_tail = hs[ts - SUBLANES:]
    h_c[...] = jnp.broadcast_to(h_tail[SUBLANES - 1:SUBLANES, :], h_c.shape)
    hlast_ref[...] = h_tail
    lru_out = hs * jax.nn.gelu(gate)

    cos = cos_ref[...]
    sin = sin_ref[...]
    lane = lax.broadcasted_iota(jnp.int32, (1, LANES), 1)
    first_half = (lane % HEAD_DIM) < (HEAD_DIM // 2)
    qcols = [_rope128(proj[:, o2 + c * LANES:o2 + (c + 1) * LANES], cos, sin, first_half) * (HEAD_DIM ** -0.5)
             for c in range(4)]
    k = _rope128(proj[:, o2 + Q_WIDTH:o2 + Q_WIDTH + KV_WIDTH], cos, sin, first_half)
    v = proj[:, o2 + Q_WIDTH + KV_WIDTH:]
    k_ext = jnp.concatenate([k_c[...], k], axis=0).astype(BF16)
    v_ext = jnp.concatenate([v_c[...], v], axis=0).astype(BF16)
    k_c[...] = k[ts - WINDOW:]
    v_c[...] = v[ts - WINDOW:]
    klast_ref[...] = k[ts - WINDOW:]
    vlast_ref[...] = v[ts - WINDOW:]

    qi = lax.broadcasted_iota(jnp.int32, (WINDOW, 2 * WINDOW), 0)
    kj = lax.broadcasted_iota(jnp.int32, (WINDOW, 2 * WINDOW), 1)
    band = jnp.logical_and(kj > qi, kj <= qi + WINDOW)
    lane_lo = lane < HEAD_DIM
    grow = lax.broadcasted_iota(jnp.int32, (GROUP * WINDOW, 1), 0) // WINDOW
    attn_cols = [[] for _ in range(4)]
    for blk in range(ts // WINDOW):
        if seq_start is None or blk > 0:
            mask = band
        else:
            mask = jnp.logical_and(band, jnp.logical_or(kj >= WINDOW, jnp.logical_not(seq_start)))
        mask4 = jnp.concatenate([mask] * GROUP, axis=0)
        kb = k_ext[blk * WINDOW:(blk + 2) * WINDOW]
        vb = v_ext[blk * WINDOW:(blk + 2) * WINDOW]
        outs = []
        for kv in range(N_KV_HEADS):
            sel = lane_lo if kv == 0 else jnp.logical_not(lane_lo)
            qs = jnp.concatenate(
                [jnp.where(sel, qc[blk * WINDOW:(blk + 1) * WINDOW], 0.0) for qc in qcols], axis=0).astype(BF16)
            s = lax.dot_general(qs, kb, (((1,), (1,)), ((), ())), preferred_element_type=F32)
            s = jnp.where(mask4, s, NEG_BIG)
            sink_col = jnp.zeros((GROUP * WINDOW, 1), F32)
            for g in range(GROUP):
                sink_col = jnp.where(grow == g, sinks_ref[kv * GROUP + g], sink_col)
            p, denom = _softmax_sink_pv(s, sink_col, vb)
            outs.append(jnp.dot(p.astype(BF16), vb, preferred_element_type=F32) / denom)
        for c in range(4):
            attn_cols[c].append(jnp.where(lane_lo, outs[0][c * WINDOW:(c + 1) * WINDOW],
                                          outs[1][c * WINDOW:(c + 1) * WINDOW]))
    attn = jnp.concatenate([jnp.concatenate(cols, axis=0) for cols in attn_cols], axis=1)

    mix = jnp.concatenate([lru_out, attn], axis=1)
    x1, h2, logits = _post_mix(x, mix, (mod[2:3], mod[3:4], mod[4:5]), w)
    x1_ref[...] = x1
    h2_ref[...] = h2
    route, tot = _route_tile(logits)
    lg_ref[0][...] = route
    lg_ref[1][...] = jnp.broadcast_to(tot, lg_ref[1].shape)


def _expand_rows(m, t):
    b, wd = m.shape
    return jnp.broadcast_to(m[:, None, :], (b, t, wd)).reshape(b * t, wd)


def _sample_body(x_ref, mod_ref, cos_ref, sin_ref, sinks_ref, h0_ref, cprev_ref, ck_ref, cv_ref, w,
                 x1_ref, h2_ref, lg_ref, g2_ref, hs_ref, u_ref, ko_ref, vo_ref):
    bt_, t = SAMPLE_BT, SUBLANES
    rows = bt_ * t

    x = x_ref[...]
    mods = [_expand_rows(mod_ref[i], t) for i in range(6)]
    proj = _in_proj(x, (mods[0], mods[1]), w)
    u = proj[:, :LRU_WIDTH]
    gate = proj[:, LRU_WIDTH:2 * LRU_WIDTH]
    o2 = 2 * LRU_WIDTH
    u_ref[...] = u

    rowid = lax.broadcasted_iota(jnp.int32, (rows, 1), 0) % t
    cprev = cprev_ref[...]
    taps = []
    for d in (1, 2, 3):
        taps.append(jnp.where(rowid >= d, pltpu.roll(u, d, axis=0),
                              pltpu.roll(cprev, (d - (CONV_W - 1)) % rows, axis=0)))
    uc = _conv_taps(u, taps[0], taps[1], taps[2], w)

    a, bt = _lru_coeffs(uc, w, None)
    bt = bt + a * h0_ref[...]
    _, hs = _group_scan(a, bt)
    hs_ref[...] = hs
    lru_out = hs * jax.nn.gelu(gate)

    cos = cos_ref[...]
    sin = sin_ref[...]
    lane = lax.broadcasted_iota(jnp.int32, (1, LANES), 1)
    first_half = (lane % HEAD_DIM) < (HEAD_DIM // 2)
    qcols = [_rope128(proj[:, o2 + c * LANES:o2 + (c + 1) * LANES], cos, sin, first_half) * (HEAD_DIM ** -0.5)
             for c in range(4)]
    k = _rope128(proj[:, o2 + Q_WIDTH:o2 + Q_WIDTH + KV_WIDTH], cos, sin, first_half)
    v = proj[:, o2 + Q_WIDTH + KV_WIDTH:]
    k3 = k.reshape(bt_, t, KV_WIDTH)
    v3 = v.reshape(bt_, t, KV_WIDTH)
    ck = ck_ref[...]
    cv = cv_ref[...]
    ko_ref[:, :WINDOW - t, :] = ck[:, t:, :]
    ko_ref[:, WINDOW - t:, :] = k3
    vo_ref[:, :WINDOW - t, :] = cv[:, t:, :]
    vo_ref[:, WINDOW - t:, :] = v3

    ckb, cvb, k3b, v3b = ck.astype(BF16), cv.astype(BF16), k3.astype(BF16), v3.astype(BF16)
    lane_lo = lane < HEAD_DIM
    gq = GROUP * t
    tq = lax.broadcasted_iota(jnp.int32, (1, gq, 1), 1) % t
    mask_c = lax.broadcasted_iota(jnp.int32, (1, gq, WINDOW), 2) > tq
    mask_n = lax.broadcasted_iota(jnp.int32, (1, gq, t), 2) <= tq
    grow = lax.broadcasted_iota(jnp.int32, (1, gq, 1), 1) // t
    bdims = (((2,), (2,)), ((0,), (0,)))
    pdims = (((2,), (1,)), ((0,), (0,)))
    outs = []
    for kv in range(N_KV_HEADS):
        sel = lane_lo if kv == 0 else jnp.logical_not(lane_lo)
        q3 = jnp.concatenate([jnp.where(sel, qc, 0.0).reshape(bt_, t, LANES) for qc in qcols], axis=1).astype(BF16)
        sc = lax.dot_general(q3, ckb, bdims, preferred_element_type=F32)
        sn = lax.dot_general(q3, k3b, bdims, preferred_element_type=F32)
        sc = jnp.where(mask_c, sc, NEG_BIG)
        sn = jnp.where(mask_n, sn, NEG_BIG)
        sink_col = jnp.zeros((1, gq, 1), F32)
        for g in range(GROUP):
            sink_col = jnp.where(grow == g, sinks_ref[kv * GROUP + g], sink_col)
        m = jnp.maximum(jnp.maximum(jnp.max(sc, axis=-1, keepdims=True), jnp.max(sn, axis=-1, keepdims=True)),
                        sink_col)
        pc = jnp.exp(sc - m)
        pn = jnp.exp(sn - m)
        denom = jnp.sum(pc, axis=-1, keepdims=True) + jnp.sum(pn, axis=-1, keepdims=True) + jnp.exp(sink_col - m)
        o = (lax.dot_general(pc.astype(BF16), cvb, pdims, preferred_element_type=F32)
             + lax.dot_general(pn.astype(BF16), v3b, pdims, preferred_element_type=F32)) / denom
        outs.append(o)
    attn = jnp.concatenate(
        [jnp.where(lane_lo, outs[0][:, c * t:(c + 1) * t, :], outs[1][:, c * t:(c + 1) * t, :]).reshape(rows, LANES)
         for c in range(4)], axis=1)

    mix = jnp.concatenate([lru_out, attn], axis=1)
    x1, h2, logits = _post_mix(x, mix, (mods[2], mods[3], mods[4]), w)
    x1_ref[...] = x1
    h2_ref[...] = h2
    route, tot = _route_tile(logits)
    lg_ref[0][...] = route
    lg_ref[1][...] = jnp.broadcast_to(tot, lg_ref[1].shape)
    g2_ref[...] = mods[5]


def _prompt_kernel(steps_per_seq, x_ref, mod_ref, cos_ref, sin_ref, sinks_ref, *rest):
    nw = len(WEIGHT_NAMES)
    w = dict(zip(WEIGHT_NAMES, rest[:nw]))
    (x1_ref, h2_ref, route_ref, cnt_ref, hlast_ref, ulast_ref, klast_ref, vlast_ref,
     conv_c, h_c, k_c, v_c) = rest[nw:]
    seq_start = pl.program_id(0) % steps_per_seq == 0
    for sub in range(MIX_TILES):
        rows = pl.ds(sub * SEQ_TILE, SEQ_TILE)
        _prompt_body(seq_start if sub == 0 else None, x_ref.at[rows], mod_ref, cos_ref.at[rows], sin_ref.at[rows],
                     sinks_ref, w, x1_ref.at[rows], h2_ref.at[rows], (route_ref.at[rows], cnt_ref.at[sub]),
                     hlast_ref, ulast_ref, klast_ref, vlast_ref, conv_c, h_c, k_c, v_c)


def _sample_kernel(x_ref, mod_ref, cos_ref, sin_ref, sinks_ref, h0_ref, cprev_ref, ck_ref, cv_ref, *rest):
    nw = len(WEIGHT_NAMES)
    w = dict(zip(WEIGHT_NAMES, rest[:nw]))
    x1_ref, h2_ref, route_ref, cnt_ref, g2_ref, hs_ref, u_ref, ko_ref, vo_ref = rest[nw:]
    _sample_body(x_ref, mod_ref, cos_ref, sin_ref, sinks_ref, h0_ref, cprev_ref, ck_ref, cv_ref, w,
                 x1_ref, h2_ref, (route_ref, cnt_ref), g2_ref, hs_ref, u_ref, ko_ref, vo_ref)


PACK_W = D_MODEL // 2
U32 = jnp.int32


def _pack_bf16_pairs(x):
    return pltpu.pack_elementwise([x[:, :PACK_W], x[:, PACK_W:]], packed_dtype=BF16)


def _unpack_bf16_pairs(w):
    return tuple(pltpu.unpack_elementwise(w, index=k, packed_dtype=BF16, unpacked_dtype=F32).astype(BF16)
                 for k in range(2))


STRIP_SIZES = tuple(SUBLANES << b for b in range(6))
STRIP_LARGE = 64
SORT_ROWS = SEQ_TILE * TOP_K + N_EXPERTS * SUBLANES
TILE_WAIT_SIZES = tuple(SUBLANES << b for b in range(8))


def _for_strips(cnt_ref, off_ref, tile, buf_slot, hbm, sem, to_hbm, act):
    def e_body(e, local):
        n = cnt_ref[tile * N_EXPERTS + e]
        glob = off_ref[tile * N_EXPERTS + e]

        def pieces(sizes, done):
            for p in sizes:
                piece = n & p
                lo = pl.ds(pl.multiple_of(local + done, SUBLANES), p)
                gl = pl.ds(pl.multiple_of(glob + done, SUBLANES), p)

                @pl.when(piece != 0)
                def _():
                    if to_hbm:
                        act(pltpu.make_async_copy(buf_slot.at[lo], hbm.at[gl], sem))
                    else:
                        act(pltpu.make_async_copy(hbm.at[gl], buf_slot.at[lo], sem))
                done = done + piece

        large = tuple(p for p in reversed(STRIP_SIZES) if p >= STRIP_LARGE)
        small = tuple(p for p in reversed(STRIP_SIZES) if p < STRIP_LARGE)
        n_large = n & (-STRIP_LARGE)

        @pl.when(n_large != 0)
        def _():
            pieces(large, 0)
        pieces(small, n_large)
        return local + n
    lax.fori_loop(0, N_EXPERTS, e_body, 0)


def _wait_tile_rows(total, buf_slot, hbm, sem, to_hbm):
    for p in TILE_WAIT_SIZES:
        @pl.when((total & p) != 0)
        def _():
            if to_hbm:
                pltpu.make_async_copy(buf_slot.at[pl.ds(0, p)], hbm.at[pl.ds(0, p)], sem).wait()
            else:
                pltpu.make_async_copy(hbm.at[pl.ds(0, p)], buf_slot.at[pl.ds(0, p)], sem).wait()


def _dispatch_kernel(n_prompt_tiles, cnt_ref, off_ref, tot_ref, meta_ref, h2p_ref, h2s_ref, routep_ref, routes_ref,
                     xs_hbm, sbuf, zblk, sem, zsem):
    i = pl.program_id(0)
    nb = pl.num_programs(0)
    slot = i % 2
    n_blocks = xs_hbm.shape[0] // MOE_TM

    is_prompt = i < n_prompt_tiles
    h2 = jnp.where(is_prompt, h2p_ref[...], h2s_ref[...])
    route_t = jnp.where(is_prompt, routep_ref[...], routes_ref[...]).T
    r_pos = lax.broadcasted_iota(jnp.int32, (SORT_ROWS, SEQ_TILE), 0).astype(F32)
    perm = jnp.zeros((SORT_ROWS, SEQ_TILE), F32)
    for k in range(TOP_K):
        perm = perm + jnp.where(r_pos == route_t[ROUTE_R + k:ROUTE_R + k + 1, :], 1.0, 0.0)
    sbuf[slot] = _pack_bf16_pairs(jnp.dot(perm.astype(BF16), h2.astype(BF16), preferred_element_type=F32))

    _for_strips(cnt_ref, off_ref, i, sbuf.at[slot], xs_hbm, sem.at[slot], True, lambda cp: cp.start())

    @pl.when(i > 0)
    def _():
        _wait_tile_rows(tot_ref[jnp.maximum(i - 1, 0)], sbuf.at[1 - slot], xs_hbm, sem.at[1 - slot], True)

    @pl.when(i == nb - 1)
    def _():
        _wait_tile_rows(tot_ref[i], sbuf.at[slot], xs_hbm, sem.at[slot], True)
        zblk[...] = jnp.zeros_like(zblk)

        def for_region_tails(act):
            def e_body(e, carry):
                start = meta_ref[e]
                n = meta_ref[N_EXPERTS + e] - start
                done = 0
                for p in reversed(STRIP_SIZES[:-1]):
                    piece = n & p
                    rows = pl.ds(pl.multiple_of(start + done, SUBLANES), p)

                    @pl.when(piece != 0)
                    def _():
                        act(pltpu.make_async_copy(zblk.at[pl.ds(0, p)], xs_hbm.at[rows], zsem.at[0]))
                    done = done + piece
                return carry
            lax.fori_loop(0, N_EXPERTS, e_body, 0)

        def for_tail_blocks(act):
            def b_body(j, carry):
                act(pltpu.make_async_copy(zblk, xs_hbm.at[pl.ds(pl.multiple_of(j * MOE_TM, MOE_TM), MOE_TM)],
                                          zsem.at[0]))
                return carry
            lax.fori_loop(meta_ref[2 * N_EXPERTS], n_blocks, b_body, 0)

        for_region_tails(lambda cp: cp.start())
        for_tail_blocks(lambda cp: cp.start())
        for_region_tails(lambda cp: cp.wait())
        for_tail_blocks(lambda cp: cp.wait())


def _dispatch(cnt8, tile_off, tot8, meta, h2_p, h2_s, route_p, route_s, n_rows):
    npt = h2_p.shape[0] // SEQ_TILE
    p_tile = lambda i, *_: (jnp.minimum(i, npt - 1), 0)
    s_tile = lambda i, *_: (jnp.maximum(i - npt, 0), 0)
    nt = tot8.shape[0]
    tt = SEQ_TILE
    grid_spec = pltpu.PrefetchScalarGridSpec(
        num_scalar_prefetch=4,
        grid=(nt,),
        in_specs=[pl.BlockSpec((tt, D_MODEL), p_tile), pl.BlockSpec((tt, D_MODEL), s_tile),
                  pl.BlockSpec((tt, LANES), p_tile), pl.BlockSpec((tt, LANES), s_tile)],
        out_specs=pl.BlockSpec(memory_space=pl.ANY),
        scratch_shapes=[pltpu.VMEM((2, SORT_ROWS, PACK_W), U32), pltpu.VMEM((MOE_TM, PACK_W), U32),
                        pltpu.SemaphoreType.DMA((2,)), pltpu.SemaphoreType.DMA((1,))],
    )
    return pl.pallas_call(
        functools.partial(_dispatch_kernel, npt),
        grid_spec=grid_spec,
        out_shape=jax.ShapeDtypeStruct((n_rows, PACK_W), U32),
        compiler_params=pltpu.CompilerParams(dimension_semantics=("arbitrary",), vmem_limit_bytes=VMEM_LIMIT),
        name="dispatch",
    )(cnt8, tile_off, tot8, meta, h2_p, h2_s, route_p, route_s)


def _expert_mlp(words, w1b, b1_ref, w2b, b2_ref, act_ref):
    xb = jnp.concatenate(_unpack_bf16_pairs(words), axis=1)
    q = D_FF // 4
    for c in range(4):
        zg = jnp.dot(xb, w1b[:, c * q:(c + 1) * q], preferred_element_type=F32) + b1_ref[:, c * q:(c + 1) * q]
        zl = (jnp.dot(xb, w1b[:, D_FF + c * q:D_FF + (c + 1) * q], preferred_element_type=F32)
              + b1_ref[:, D_FF + c * q:D_FF + (c + 1) * q])
        glu = jnp.minimum(zg, SWIGLU_LIMIT)
        lin = jnp.clip(zl, -SWIGLU_LIMIT, SWIGLU_LIMIT)
        act_ref[:, c * q:(c + 1) * q] = (glu * jax.nn.sigmoid(SWIGLU_ALPHA * glu) * (lin + 1.0)).astype(BF16)
    return _pack_bf16_pairs(jnp.dot(act_ref[...], w2b[...], preferred_element_type=F32) + b2_ref[...])


def _moe_kernel(row_ref, size_ref, exp_ref, wt_ref, meta_ref, xs_hbm, w1_hbm, b1_ref, w2_hbm, b2_ref, ys_hbm,
                xin, yout, w1f, w2f, w1b, w2b, act, isem, osem, wsem):
    i = pl.program_id(0)
    n = pl.num_programs(0)
    slot = i % 2
    tm = MOE_TM
    n_blocks = ys_hbm.shape[0] // tm

    def for_chunk(j, fn):
        r = pl.multiple_of(row_ref[j], tm)
        for code in MOE_CHUNK_CODES:
            @pl.when(size_ref[j] == code)
            def _():
                fn(r, code * tm)

    def in_copy(r, rows, s):
        return pltpu.make_async_copy(xs_hbm.at[pl.ds(r, rows)], xin.at[s, pl.ds(0, rows)], isem.at[s])

    def out_copy(r, rows, s):
        return pltpu.make_async_copy(yout.at[s, pl.ds(0, rows)], ys_hbm.at[pl.ds(r, rows)], osem.at[s])

    def start_in(j, s):
        for_chunk(j, lambda r, rows: in_copy(r, rows, s).start())

    def wait_in(j, s):
        for_chunk(j, lambda r, rows: in_copy(r, rows, s).wait())

    def start_out(j, s):
        for_chunk(j, lambda r, rows: out_copy(r, rows, s).start())

    def wait_out(j, s):
        for_chunk(j, lambda r, rows: out_copy(r, rows, s).wait())

    @pl.when(i == 0)
    def _():
        start_in(0, 0)

    @pl.when(i + 1 < n)
    def _():
        start_in(i + 1, 1 - slot)

    @pl.when(i >= 2)
    def _():
        wait_out(i - 2, slot)

    def weight_copies(e, ws):
        return (pltpu.make_async_copy(w1_hbm.at[e], w1f.at[ws], wsem.at[ws]),
                pltpu.make_async_copy(w2_hbm.at[e], w2f.at[ws], wsem.at[ws]))

    n_steps = n
    @pl.when(wt_ref[i] == 1)
    def _():
        ws = wt_ref[n_steps + i]
        nxt = wt_ref[2 * n_steps + i]

        @pl.when(i == 0)
        def _():
            for cp in weight_copies(exp_ref[0], 0):
                cp.start()

        for cp in weight_copies(exp_ref[i], ws):
            cp.wait()

        @pl.when(nxt >= 0)
        def _():
            for cp in weight_copies(nxt, 1 - ws):
                cp.start()

        chunk = 128
        def cast_body(c, carry):
            k0 = pl.multiple_of(c * chunk, chunk)
            w1b[pl.ds(k0, chunk), :] = w1f[ws, pl.ds(k0, chunk), :].astype(BF16)
            w2b[pl.ds(k0, chunk), :] = w2f[ws, pl.ds(k0, chunk), :].astype(BF16)
            return carry
        lax.fori_loop(0, D_MODEL // chunk, cast_body, 0)

    wait_in(i, slot)

    for code in MOE_CHUNK_CODES:
        rows = code * tm

        @pl.when(size_ref[i] == code)
        def _():
            yout[slot, 0:rows] = _expert_mlp(xin[slot, 0:rows], w1b, b1_ref.at[exp_ref[i]], w2b,
                                             b2_ref.at[exp_ref[i]], act.at[pl.ds(0, rows)])

    start_out(i, slot)

    @pl.when(i == n - 1)
    def _():
        @pl.when(i >= 1)
        def _():
            wait_out(i - 1, 1 - slot)
        wait_out(i, slot)
        yout[1, 0:tm] = jnp.zeros((tm, PACK_W), U32)

        def zero_block(j):
            return pltpu.make_async_copy(yout.at[1, pl.ds(0, tm)], ys_hbm.at[pl.ds(pl.multiple_of(j * tm, tm), tm)],
                                         osem.at[1])

        def start_body(j, carry):
            zero_block(j).start()
            return carry

        def wait_body(j, carry):
            zero_block(j).wait()
            return carry
        lax.fori_loop(meta_ref[2 * N_EXPERTS], n_blocks, start_body, 0)
        lax.fori_loop(meta_ref[2 * N_EXPERTS], n_blocks, wait_body, 0)


def _moe(chunk_row, chunk_size, chunk_exp, weight_tbl, meta, xs, w1, b1, w2, b2):
    whole = lambda i, *_: (0, 0, 0)
    grid_spec = pltpu.PrefetchScalarGridSpec(
        num_scalar_prefetch=5,
        grid=(chunk_row.shape[0],),
        in_specs=[
            pl.BlockSpec(memory_space=pl.ANY),
            pl.BlockSpec(memory_space=pl.ANY),
            pl.BlockSpec((N_EXPERTS, 1, 2 * D_FF), whole),
            pl.BlockSpec(memory_space=pl.ANY),
            pl.BlockSpec((N_EXPERTS, 1, D_MODEL), whole),
        ],
        out_specs=pl.BlockSpec(memory_space=pl.ANY),
        scratch_shapes=[pltpu.VMEM((2, MOE_CH, PACK_W), U32), pltpu.VMEM((2, MOE_CH, PACK_W), U32),
                        pltpu.VMEM((2, D_MODEL, 2 * D_FF), F32), pltpu.VMEM((2, D_FF, D_MODEL), F32),
                        pltpu.VMEM((D_MODEL, 2 * D_FF), BF16), pltpu.VMEM((D_FF, D_MODEL), BF16),
                        pltpu.VMEM((MOE_CH, D_FF), BF16),
                        pltpu.SemaphoreType.DMA((2,)), pltpu.SemaphoreType.DMA((2,)), pltpu.SemaphoreType.DMA((2,))],
    )
    return pl.pallas_call(
        _moe_kernel,
        grid_spec=grid_spec,
        out_shape=jax.ShapeDtypeStruct(xs.shape, U32),
        compiler_params=pltpu.CompilerParams(dimension_semantics=("arbitrary",), vmem_limit_bytes=VMEM_LIMIT),
        name="moe",
    )(chunk_row, chunk_size, chunk_exp, weight_tbl, meta, xs, w1, b1, w2, b2)


def _combine_kernel(n_prompt_tiles, cnt_ref, off_ref, tot_ref, ys_hbm, routep_ref, routes_ref, x1p_ref, x1s_ref,
                    modp_ref, g2s_ref, fg_ref, op_ref, os_ref, buf, sem):
    i = pl.program_id(0)
    nb = pl.num_programs(0)
    slot = i % 2

    def fetch(tile, s):
        _for_strips(cnt_ref, off_ref, tile, buf.at[s], ys_hbm, sem.at[s], False, lambda cp: cp.start())

    @pl.when(i == 0)
    def _():
        buf[...] = jnp.zeros_like(buf)
        fetch(0, 0)

    @pl.when(i + 1 < nb)
    def _():
        fetch(i + 1, 1 - slot)

    _wait_tile_rows(tot_ref[i], buf.at[slot], ys_hbm, sem.at[slot], False)

    is_prompt = i < n_prompt_tiles
    route = jnp.where(is_prompt, routep_ref[...], routes_ref[...])
    c_pos = lax.broadcasted_iota(jnp.int32, (SEQ_TILE, SORT_ROWS), 1).astype(F32)
    gmat = jnp.zeros((SEQ_TILE, SORT_ROWS), F32)
    for k in range(TOP_K):
        gmat = gmat + jnp.where(c_pos == route[:, ROUTE_R + k:ROUTE_R + k + 1],
                                route[:, ROUTE_G + k:ROUTE_G + k + 1], 0.0)
    g_bf = gmat.astype(BF16)
    ff = jnp.concatenate([jnp.dot(g_bf, yb, preferred_element_type=F32) for yb in _unpack_bf16_pairs(buf[slot])],
                         axis=1)
    g2 = jnp.where(i < n_prompt_tiles, modp_ref[5:6, :], g2s_ref[...])
    x = jnp.where(is_prompt, x1p_ref[...], x1s_ref[...]) + g2 * ff
    y = _rms(x, fg_ref[...])

    @pl.when(i < n_prompt_tiles)
    def _():
        op_ref[...] = y

    @pl.when(i >= n_prompt_tiles)
    def _():
        os_ref[...] = y


def _combine(cnt8, tile_off, tot8, ys, route_p, route_s, x1_p, x1_s, mod_p, g2_rows, final_g, tiles_per_batch):
    nt = tot8.shape[0]
    tt = SEQ_TILE
    npt = x1_p.shape[0] // tt
    p_tile = lambda i, *_: (jnp.minimum(i, npt - 1), 0)
    s_tile = lambda i, *_: (jnp.maximum(i - npt, 0), 0)
    grid_spec = pltpu.PrefetchScalarGridSpec(
        num_scalar_prefetch=3,
        grid=(nt,),
        in_specs=[
            pl.BlockSpec(memory_space=pl.ANY),
            pl.BlockSpec((tt, LANES), p_tile), pl.BlockSpec((tt, LANES), s_tile),
            pl.BlockSpec((tt, D_MODEL), p_tile), pl.BlockSpec((tt, D_MODEL), s_tile),
            pl.BlockSpec((None, 6, D_MODEL), lambda i, *_: (jnp.minimum(i, npt - 1) // tiles_per_batch, 0, 0)),
            pl.BlockSpec((tt, D_MODEL), s_tile),
            pl.BlockSpec((1, D_MODEL), lambda i, *_: (0, 0)),
        ],
        out_specs=(pl.BlockSpec((tt, D_MODEL), p_tile), pl.BlockSpec((tt, D_MODEL), s_tile)),
        scratch_shapes=[pltpu.VMEM((2, SORT_ROWS, PACK_W), U32), pltpu.SemaphoreType.DMA((2,))],
    )
    return pl.pallas_call(
        functools.partial(_combine_kernel, npt),
        grid_spec=grid_spec,
        out_shape=(jax.ShapeDtypeStruct((npt * tt, D_MODEL), F32),
                   jax.ShapeDtypeStruct(((nt - npt) * tt, D_MODEL), F32)),
        compiler_params=pltpu.CompilerParams(dimension_semantics=("arbitrary",), vmem_limit_bytes=VMEM_LIMIT),
        name="combine",
    )(cnt8, tile_off, tot8, ys, route_p, route_s, x1_p, x1_s, mod_p, g2_rows, final_g)


def _block_diag_halves(wg):
    halves = []
    for hh in range(2):
        rows = []
        for bi in range(4):
            row = [wg[hh * 4 + bi] if bj == bi else jnp.zeros((LRU_BLOCK_W, LRU_BLOCK_W), wg.dtype) for bj in range(4)]
            rows.append(jnp.concatenate(row, axis=1))
        halves.append(jnp.concatenate(rows, axis=0))
    return jnp.stack(halves).astype(BF16)


def _rope_tables(pos):
    half = HEAD_DIM // 2
    inv = ROPE_THETA ** (-jnp.arange(half, dtype=F32) / half)
    ang = pos.astype(F32)[:, None] * inv[None, :]
    cos = jnp.cos(ang)
    sin = jnp.sin(ang)
    cos128 = jnp.concatenate([cos, cos, cos, cos], axis=1)
    sin128 = jnp.concatenate([-sin, sin, -sin, sin], axis=1)
    return cos128, sin128


def _full_spec(arr, grid_rank):
    zeros = (0,) * arr.ndim
    if grid_rank == 1:
        return pl.BlockSpec(arr.shape, lambda i: zeros)
    return pl.BlockSpec(arr.shape, lambda b, j: zeros)


def kernel(x_prompt, x_sample, state_lru_h, state_conv, cache_win_k, cache_win_v, c_prompt, c_sample, w_ada, b_ada, norm1_g, w_in, b_in, conv_w, conv_b, lru_wa, lru_ba, lru_wx, lru_bx, lru_lambda, attn_sinks, w_out, b_out, norm2_g, w_router, b_router, w1, b1, w2, b2, final_g):
    bp, seq, _ = x_prompt.shape
    bd, tdec, _ = x_sample.shape
    assert tdec == SUBLANES and seq % (MIX_TILES * SEQ_TILE) == 0 and bd % SAMPLE_BT == 0
    assert SAMPLE_BT * tdec == SEQ_TILE
    n_prompt = bp * seq
    n_sample = bd * tdec
    n_tok = n_prompt + n_sample
    l = 0

    head_perm = [h for c in range(4) for h in (c, GROUP + c)]
    o2 = 2 * LRU_WIDTH

    def permute_heads(arr, axis, start):
        take = lambda a, b: lax.slice_in_dim(arr, a, b, axis=axis)
        heads = [take(start + h * HEAD_DIM, start + (h + 1) * HEAD_DIM) for h in head_perm]
        return jnp.concatenate([take(0, start)] + heads + [take(start + Q_WIDTH, arr.shape[axis])], axis=axis)

    sinks_perm = attn_sinks[l]

    wr = jnp.tile(w_router[l], (1, TOP_K))
    wr_hi = wr.astype(BF16)
    weights = dict(
        norm1_g=norm1_g[l][None, :], w_in=permute_heads(w_in[l], 1, o2).astype(BF16),
        b_in=permute_heads(b_in[l], 0, o2)[None, :],
        conv_w=conv_w[l], conv_b=conv_b[l][None, :],
        ga=_block_diag_halves(lru_wa[l]), gx=_block_diag_halves(lru_wx[l]),
        lru_ba=lru_ba[l][None, :], lru_bx=lru_bx[l][None, :], lam=lru_lambda[l][None, :],
        w_out=permute_heads(w_out[l], 0, LRU_WIDTH).astype(BF16), b_out=b_out[l][None, :], norm2_g=norm2_g[l][None, :],
        wr_hi=wr_hi, wr_lo=(wr - wr_hi.astype(F32)).astype(BF16),
        b_router=jnp.tile(b_router[l], TOP_K)[None, :],
    )
    wlist = [weights[n] for n in WEIGHT_NAMES]

    mod_all = _ada(jnp.concatenate([c_prompt, c_sample], axis=0), w_ada[l], b_ada[l][None, :])
    mod_p = mod_all[:bp].reshape(bp, 6, D_MODEL)
    mod_s = mod_all[bp:].reshape(bd, 6, D_MODEL).transpose(1, 0, 2)

    cos_p, sin_p = _rope_tables(jnp.arange(seq, dtype=jnp.int32))
    cos_s, sin_s = _rope_tables(PAST_LEN + jnp.arange(tdec, dtype=jnp.int32))
    cos_s = jnp.tile(cos_s, (SAMPLE_BT, 1))
    sin_s = jnp.tile(sin_s, (SAMPLE_BT, 1))
    h0_rows = jnp.pad(state_lru_h[l][:, None, :], ((0, 0), (0, tdec - 1), (0, 0))).reshape(n_sample, LRU_WIDTH)
    cprev_rows = jnp.pad(state_conv[l], ((0, 0), (0, tdec - (CONV_W - 1)), (0, 0))).reshape(n_sample, LRU_WIDTH)
    ck = cache_win_k[l].reshape(bd, WINDOW, KV_WIDTH)
    cv = cache_win_v[l].reshape(bd, WINDOW, KV_WIDTH)
    npt = n_prompt // SEQ_TILE
    nst = n_sample // SEQ_TILE
    mix_rows = MIX_TILES * SEQ_TILE
    steps_per_seq = seq // mix_rows
    wspecs = [_full_spec(a, 1) for a in wlist]

    rows_p = lambda width: pl.BlockSpec((mix_rows, width), lambda i: (i, 0))
    tail_p = lambda rows, width: pl.BlockSpec((None, rows, width), lambda i: (i // steps_per_seq, 0, 0))
    (x1_p, h2_p, route_p, cnt_p, hlast_p, ulast_p, klast_p, vlast_p) = pl.pallas_call(
        functools.partial(_prompt_kernel, steps_per_seq),
        grid=(n_prompt // mix_rows,),
        in_specs=[rows_p(D_MODEL),
                  pl.BlockSpec((None, 6, D_MODEL), lambda i: (i // steps_per_seq, 0, 0)),
                  pl.BlockSpec((mix_rows, LANES), lambda i: (i % steps_per_seq, 0)),
                  pl.BlockSpec((mix_rows, LANES), lambda i: (i % steps_per_seq, 0)),
                  pl.BlockSpec(memory_space=pltpu.SMEM)] + wspecs,
        out_specs=(rows_p(D_MODEL), rows_p(D_MODEL), rows_p(LANES),
                   pl.BlockSpec((MIX_TILES, SUBLANES, LANES), lambda i: (i, 0, 0)),
                   tail_p(SUBLANES, LRU_WIDTH), tail_p(SUBLANES, LRU_WIDTH),
                   tail_p(WINDOW, KV_WIDTH), tail_p(WINDOW, KV_WIDTH)),
        out_shape=(
            jax.ShapeDtypeStruct((n_prompt, D_MODEL), F32),
            jax.ShapeDtypeStruct((n_prompt, D_MODEL), F32),
            jax.ShapeDtypeStruct((n_prompt, LANES), F32),
            jax.ShapeDtypeStruct((npt, SUBLANES, LANES), F32),
            jax.ShapeDtypeStruct((bp, SUBLANES, LRU_WIDTH), F32),
            jax.ShapeDtypeStruct((bp, SUBLANES, LRU_WIDTH), F32),
            jax.ShapeDtypeStruct((bp, WINDOW, KV_WIDTH), F32),
            jax.ShapeDtypeStruct((bp, WINDOW, KV_WIDTH), F32),
        ),
        scratch_shapes=[pltpu.VMEM((SUBLANES, LRU_WIDTH), F32), pltpu.VMEM((SUBLANES, LRU_WIDTH), F32),
                        pltpu.VMEM((WINDOW, KV_WIDTH), F32), pltpu.VMEM((WINDOW, KV_WIDTH), F32)],
        compiler_params=pltpu.CompilerParams(dimension_semantics=("arbitrary",), vmem_limit_bytes=VMEM_LIMIT),
        name="prompt_mixer",
    )(x_prompt.reshape(n_prompt, D_MODEL), mod_p, cos_p, sin_p, sinks_perm, *wlist)

    rows_s = lambda width: pl.BlockSpec((SEQ_TILE, width), lambda i: (i, 0))
    cache_spec = pl.BlockSpec((SAMPLE_BT, WINDOW, KV_WIDTH), lambda i: (i, 0, 0))
    (x1_s, h2_s, route_s, cnt_s, g2_rows, hs_s, u_s, s_k, s_v) = pl.pallas_call(
        _sample_kernel,
        grid=(nst,),
        in_specs=[rows_s(D_MODEL),
                  pl.BlockSpec((6, SAMPLE_BT, D_MODEL), lambda i: (0, i, 0)),
                  pl.BlockSpec((SEQ_TILE, LANES), lambda i: (0, 0)),
                  pl.BlockSpec((SEQ_TILE, LANES), lambda i: (0, 0)),
                  pl.BlockSpec(memory_space=pltpu.SMEM),
                  rows_s(LRU_WIDTH), rows_s(LRU_WIDTH), cache_spec, cache_spec] + wspecs,
        out_specs=(rows_s(D_MODEL), rows_s(D_MODEL), rows_s(LANES),
                   pl.BlockSpec((None, SUBLANES, LANES), lambda i: (i, 0, 0)),
                   rows_s(D_MODEL), rows_s(LRU_WIDTH), rows_s(LRU_WIDTH), cache_spec, cache_spec),
        out_shape=(
            jax.ShapeDtypeStruct((n_sample, D_MODEL), F32), jax.ShapeDtypeStruct((n_sample, D_MODEL), F32),
            jax.ShapeDtypeStruct((n_sample, LANES), F32), jax.ShapeDtypeStruct((nst, SUBLANES, LANES), F32),
            jax.ShapeDtypeStruct((n_sample, D_MODEL), F32),
            jax.ShapeDtypeStruct((n_sample, LRU_WIDTH), F32),
            jax.ShapeDtypeStruct((n_sample, LRU_WIDTH), F32),
            jax.ShapeDtypeStruct((bd, WINDOW, KV_WIDTH), F32),
            jax.ShapeDtypeStruct((bd, WINDOW, KV_WIDTH), F32),
        ),
        compiler_params=pltpu.CompilerParams(dimension_semantics=("arbitrary",), vmem_limit_bytes=VMEM_LIMIT),
        name="sample_mixer",
    )(x_sample.reshape(n_sample, D_MODEL), mod_s, cos_s, sin_s, sinks_perm, h0_rows, cprev_rows, ck, cv, *wlist)
    tile_cnt = jnp.concatenate([cnt_p, cnt_s], axis=0)

    n_tiles = npt + nst
    n_assign = n_tok * TOP_K
    max_rows = n_assign + n_tiles * N_EXPERTS * (SUBLANES - 1) + N_EXPERTS * (MOE_TM - 1)
    n_blocks = -(-max_rows // MOE_TM)
    cnt = tile_cnt[:, 0, :N_EXPERTS].astype(jnp.int32)
    cnt8 = (cnt + SUBLANES - 1) // SUBLANES * SUBLANES
    counts = jnp.sum(cnt8, axis=0)
    pcounts = (counts + MOE_TM - 1) // MOE_TM * MOE_TM
    pend = jnp.cumsum(pcounts)
    pstart = pend - pcounts
    tile_off = pstart[None, :] + jnp.cumsum(cnt8, axis=0) - cnt8
    tot8 = jnp.sum(cnt8, axis=1)
    meta = jnp.concatenate([pstart + counts, pend, pend[-1:] // MOE_TM]).astype(jnp.int32)
    cnt8_flat = cnt8.reshape(-1)
    off_flat = tile_off.reshape(-1).astype(jnp.int32)

    xs = _dispatch(cnt8_flat, off_flat, tot8, meta, h2_p, h2_s, route_p, route_s, n_blocks * MOE_TM)

    n_tm = pcounts // MOE_TM
    n_big = n_tm // 4
    has2 = (n_tm // 2) % 2
    n_ch = n_big + has2 + n_tm % 2
    ch_end = jnp.cumsum(n_ch)
    ch_start = ch_end - n_ch
    n_chunks = n_blocks // 4 + 2 * N_EXPERTS
    ci = jnp.arange(n_chunks, dtype=jnp.int32)
    owner = jnp.logical_and(ch_start[None, :] <= ci[:, None], ci[:, None] < ch_end[None, :])
    pick = lambda v: jnp.sum(jnp.where(owner, v[None, :], 0), axis=1)
    live = ci < ch_end[-1]
    local = ci - pick(ch_start)
    chunk_exp = jnp.where(live, pick(jnp.arange(N_EXPERTS, dtype=jnp.int32)), N_EXPERTS - 1).astype(jnp.int32)
    big_c, has2_c = pick(n_big), pick(has2)
    is_big = local < big_c
    is_two = jnp.logical_and(local == big_c, has2_c == 1)
    blocks_before = jnp.where(is_big, local * 4, big_c * 4 + jnp.where(is_two, 0, has2_c * 2))
    chunk_row = jnp.where(live, pick(pstart) + blocks_before * MOE_TM, 0).astype(jnp.int32)
    chunk_size = jnp.where(live, jnp.where(is_big, 4, jnp.where(is_two, 2, 1)), 0).astype(jnp.int32)
    has_rows = n_ch > 0
    e_ar = jnp.arange(N_EXPERTS, dtype=jnp.int32)
    w_slot = (jnp.cumsum(has_rows.astype(jnp.int32)) - 1) % 2
    later = jnp.logical_and(has_rows[None, :], e_ar[None, :] > e_ar[:, None])
    nxt = jnp.min(jnp.where(later, e_ar[None, :], N_EXPERTS), axis=1)
    nxt = jnp.where(nxt == N_EXPERTS, -1, nxt)
    weight_tbl = jnp.concatenate([jnp.logical_and(live, local == 0).astype(jnp.int32),
                                  pick(w_slot), jnp.where(live, pick(nxt), -1)]).astype(jnp.int32)
    ys = _moe(chunk_row, chunk_size, chunk_exp, weight_tbl, meta, xs, w1[l], b1[l][:, None, :], w2[l], b2[l][:, None, :])

    y_p, y_s = _combine(cnt8_flat, off_flat, tot8, ys, route_p, route_s, x1_p, x1_s, mod_p, g2_rows,
                        final_g[None, :], seq // SEQ_TILE)

    y_prompt = y_p.reshape(bp, seq, D_MODEL)
    y_sample = y_s.reshape(bd, tdec, D_MODEL)
    p_h = hlast_p[:, SUBLANES - 1, :][None]
    p_c = ulast_p[:, SUBLANES - (CONV_W - 1):, :][None]
    p_k = klast_p.reshape(1, bp, WINDOW, N_KV_HEADS, HEAD_DIM)
    p_v = vlast_p.reshape(1, bp, WINDOW, N_KV_HEADS, HEAD_DIM)
    s_h = hs_s.reshape(bd, tdec, LRU_WIDTH)[:, tdec - 1, :][None]
    s_c = u_s.reshape(bd, tdec, LRU_WIDTH)[:, tdec - (CONV_W - 1):, :][None]
    s_kk = s_k.reshape(1, bd, WINDOW, N_KV_HEADS, HEAD_DIM)
    s_vv = s_v.reshape(1, bd, WINDOW, N_KV_HEADS, HEAD_DIM)
    return (y_prompt, y_sample, p_h, p_c, p_k, p_v, s_h, s_c, s_kk, s_vv)
```

```python
import functools

import jax
import jax.numpy as jnp
from jax import lax
from jax.experimental import pallas as pl
from jax.experimental.pallas import tpu as pltpu

F32 = jnp.float32
BF16 = jnp.bfloat16

D_MODEL = 1024
LRU_WIDTH = 512
LRU_BLOCKS = 8
LRU_BLOCK_W = LRU_WIDTH // LRU_BLOCKS
CONV_W = 4
LRU_C = 8.0
HEAD_DIM = 64
N_HEADS = 8
N_KV_HEADS = 2
GROUP = N_HEADS // N_KV_HEADS
WINDOW = 128
ROPE_THETA = 10000.0
N_EXPERTS = 32
TOP_K = 4
D_FF = D_MODEL
SWIGLU_LIMIT = 7.0
SWIGLU_ALPHA = 1.702
NORM_EPS = 1e-5
PAST_LEN = 8192
Q_WIDTH = N_HEADS * HEAD_DIM
KV_WIDTH = N_KV_HEADS * HEAD_DIM
IN_WIDTH = 2 * LRU_WIDTH + Q_WIDTH + 2 * KV_WIDTH

LANES = 128
SUBLANES = 8
SEQ_TILE = 256
MIX_TILES = 2
SAMPLE_BT = 32
MOE_TM = 256
MOE_CHUNK_CODES = (4, 2, 1)
MOE_CH = MOE_CHUNK_CODES[0] * MOE_TM
NEG_BIG = -1e30
VMEM_LIMIT = 56 * 1024 * 1024


def _rms(x, g):
    return x * lax.rsqrt(jnp.mean(x * x, axis=-1, keepdims=True) + NORM_EPS) * g


def _rms_mod(x, gain, shift):
    return x * lax.rsqrt(jnp.mean(x * x, axis=-1, keepdims=True) + NORM_EPS) * gain + shift


def _group_scan(a, b):
    rows, width = a.shape
    groups = rows // SUBLANES
    a3 = a.reshape(groups, SUBLANES, width)
    b3 = b.reshape(groups, SUBLANES, width)
    t = lax.broadcasted_iota(jnp.int32, (1, SUBLANES, 1), 1)
    d = 1
    while d < SUBLANES:
        keep = t >= d
        a_s = jnp.where(keep, pltpu.roll(a3, d, axis=1), 1.0)
        b_s = jnp.where(keep, pltpu.roll(b3, d, axis=1), 0.0)
        b3 = a3 * b_s + b3
        a3 = a3 * a_s
        d *= 2
    return a3.reshape(rows, width), b3.reshape(rows, width)


def _chain_groups(a_grp, b_grp, h_in):
    rows = a_grp.shape[0]
    out = []
    carry = h_in
    for g in range(rows // SUBLANES):
        sl = slice(g * SUBLANES, (g + 1) * SUBLANES)
        hg = b_grp[sl] + a_grp[sl] * carry
        out.append(hg)
        carry = hg[SUBLANES - 1:SUBLANES]
    return jnp.concatenate(out, axis=0)


def _rope128(x, cos, sin_signed, first_half):
    sw = jnp.where(first_half, pltpu.roll(x, LANES - HEAD_DIM // 2, axis=1), pltpu.roll(x, HEAD_DIM // 2, axis=1))
    return x * cos + sw * sin_signed


def _softplus(x):
    return jnp.maximum(x, 0.0) + jnp.log1p(jnp.exp(-jnp.abs(x)))


def _lru_coeffs(uc, w, first_pos_mask):
    ub = uc.astype(BF16)
    half = LRU_WIDTH // 2
    ra = jnp.concatenate([jnp.dot(ub[:, :half], w['ga'][0], preferred_element_type=F32),
                          jnp.dot(ub[:, half:], w['ga'][1], preferred_element_type=F32)], axis=1)
    rx = jnp.concatenate([jnp.dot(ub[:, :half], w['gx'][0], preferred_element_type=F32),
                          jnp.dot(ub[:, half:], w['gx'][1], preferred_element_type=F32)], axis=1)
    r = jax.nn.sigmoid(ra + w['lru_ba'][...])
    i = jax.nn.sigmoid(rx + w['lru_bx'][...])
    log_a = -LRU_C * r * _softplus(-w['lam'][...])
    a = jnp.exp(log_a)
    om = 1.0 - a * a
    mult = jnp.where(om > 0.0, om * lax.rsqrt(om), 0.0)
    if first_pos_mask is not None:
        mult = jnp.where(first_pos_mask, 1.0, mult)
    return a, mult * i * uc


def _conv_taps(u, s1, s2, s3, w):
    cw = w['conv_w']
    return w['conv_b'][...] + s3 * cw[0:1, :] + s2 * cw[1:2, :] + s1 * cw[2:3, :] + u * cw[3:4, :]


def _in_proj(x, mod, w):
    sh1, sc1 = mod
    h = _rms_mod(x, w['norm1_g'][...] * (1.0 + sc1), sh1)
    return jnp.dot(h.astype(BF16), w['w_in'][...], preferred_element_type=F32) + w['b_in'][...]


def _post_mix(x, mix, mod, w):
    g1, sh2, sc2 = mod
    x1 = x + g1 * (jnp.dot(mix.astype(BF16), w['w_out'][...], preferred_element_type=F32) + w['b_out'][...])
    h2 = _rms_mod(x1, w['norm2_g'][...] * (1.0 + sc2), sh2)
    h2_hi = h2.astype(BF16)
    h2_lo = (h2 - h2_hi.astype(F32)).astype(BF16)
    logits = (jnp.dot(h2_hi, w['wr_hi'][...], preferred_element_type=F32)
              + jnp.dot(h2_lo, w['wr_hi'][...], preferred_element_type=F32)
              + jnp.dot(h2_hi, w['wr_lo'][...], preferred_element_type=F32)) + w['b_router'][...]
    return x1, h2, logits


ROUTE_E, ROUTE_G, ROUTE_R = 0, TOP_K, 2 * TOP_K


def _lane_roll1(v, shift):
    return pltpu.roll(jnp.broadcast_to(v, (SUBLANES, LANES)), shift, axis=1)[0:1]


def _route_tile(lg):
    rows = lg.shape[0]
    lane = lax.broadcasted_iota(jnp.int32, (1, LANES), 1)
    e_of = lane % N_EXPERTS
    grp = lane // N_EXPERTS
    e_id = e_of.astype(F32)
    onehot = jnp.zeros((rows, LANES), F32)
    vals, ids = [], []
    for k in range(TOP_K):
        m = jnp.max(lg, axis=1, keepdims=True)
        idx = jnp.min(jnp.where(lg == m, e_id, float(N_EXPERTS)), axis=1, keepdims=True)
        sel = e_id == idx
        lg = jnp.where(sel, -jnp.inf, lg)
        onehot = jnp.where(jnp.logical_and(sel, grp == k), 1.0, onehot)
        vals.append(m)
        ids.append(idx)
    ex = [jnp.exp(v - vals[0]) for v in vals]
    denom = ex[0] + ex[1] + ex[2] + ex[3]

    r_i = lax.broadcasted_iota(jnp.int32, (rows, rows), 0)
    c_i = lax.broadcasted_iota(jnp.int32, (rows, rows), 1)
    strict_lower = jnp.where(r_i > c_i, 1.0, 0.0).astype(BF16)
    prefix = jnp.dot(strict_lower, onehot.astype(BF16), preferred_element_type=F32)
    cnt = jnp.sum(onehot, axis=0, keepdims=True)
    base = jnp.zeros((1, LANES), F32)
    tot = cnt
    for s in range(1, TOP_K):
        rolled = _lane_roll1(cnt, s * N_EXPERTS)
        base = base + jnp.where(lane >= s * N_EXPERTS, rolled, 0.0)
        tot = tot + rolled
    pad_cnt = jnp.floor((tot + (SUBLANES - 1.0)) * (1.0 / SUBLANES)) * SUBLANES
    inc = pad_cnt
    d = 1
    while d < N_EXPERTS:
        inc = inc + jnp.where(e_of >= d, _lane_roll1(inc, d), 0.0)
        d *= 2
    strip_start = inc - pad_cnt
    ranked = onehot * (prefix + base + strip_start)

    route = jnp.zeros((rows, LANES), F32)
    for k in range(TOP_K):
        rank_k = jnp.sum(jnp.where(grp == k, ranked, 0.0), axis=1, keepdims=True)
        route = jnp.where(lane == ROUTE_E + k, ids[k], route)
        route = jnp.where(lane == ROUTE_G + k, ex[k] / denom, route)
        route = jnp.where(lane == ROUTE_R + k, rank_k, route)
    return route, tot


def _softmax_sink_pv(s, sink_col, v_bf16):
    m = jnp.maximum(jnp.max(s, axis=-1, keepdims=True), sink_col)
    p = jnp.exp(s - m)
    denom = jnp.sum(p, axis=-1, keepdims=True) + jnp.exp(sink_col - m)
    return p, denom


WEIGHT_NAMES = ('norm1_g', 'w_in', 'b_in', 'conv_w', 'conv_b', 'ga', 'gx', 'lru_ba', 'lru_bx', 'lam',
                'w_out', 'b_out', 'norm2_g', 'wr_hi', 'wr_lo', 'b_router')


def _split_bf16(x):
    hi = x.astype(BF16)
    return hi, (x - hi.astype(F32)).astype(BF16)


def _ada_kernel(n_prompt_rows, c_ref, w_ref, b_ref, op_ref, os_ref):
    c = c_ref[...]
    s_hi, s_lo = _split_bf16(c * jax.nn.sigmoid(c))
    w_hi, w_lo = _split_bf16(w_ref[...])
    mod = (jnp.dot(s_hi, w_hi, preferred_element_type=F32) + jnp.dot(s_lo, w_hi, preferred_element_type=F32)
           + jnp.dot(s_hi, w_lo, preferred_element_type=F32)) + b_ref[...]
    op_ref[...] = mod[:n_prompt_rows]
    os_ref[...] = mod[n_prompt_rows:]


def _ada(c_all, n_prompt_rows, w_ada, b_ada):
    rows = c_all.shape[0]
    n_s = rows - n_prompt_rows
    assert n_prompt_rows % SUBLANES == 0
    return pl.pallas_call(
        functools.partial(_ada_kernel, n_prompt_rows),
        grid=(6,),
        in_specs=[pl.BlockSpec((rows, D_MODEL), lambda i: (0, 0)),
                  pl.BlockSpec((D_MODEL, D_MODEL), lambda i: (0, i)),
                  pl.BlockSpec((1, D_MODEL), lambda i: (0, i))],
        out_specs=(pl.BlockSpec((n_prompt_rows, D_MODEL), lambda i: (0, i)),
                   pl.BlockSpec((None, n_s, D_MODEL), lambda i: (i, 0, 0))),
        out_shape=(jax.ShapeDtypeStruct((n_prompt_rows, 6 * D_MODEL), F32),
                   jax.ShapeDtypeStruct((6, n_s, D_MODEL), F32)),
        compiler_params=pltpu.CompilerParams(dimension_semantics=("arbitrary",), vmem_limit_bytes=VMEM_LIMIT),
        name="ada",
    )(c_all, w_ada, b_ada)


def _prompt_body(seq_start, x_ref, mod_ref, cos_ref, sin_ref, sinks_ref, w, x1_ref, h2_ref, lg_ref,
                 hlast_ref, ulast_ref, klast_ref, vlast_ref, conv_c, h_c, k_c, v_c):
    ts = SEQ_TILE

    if seq_start is not None:
        @pl.when(seq_start)
        def _():
            conv_c[...] = jnp.zeros_like(conv_c)
            h_c[...] = jnp.zeros_like(h_c)
            k_c[...] = jnp.zeros_like(k_c)
            v_c[...] = jnp.zeros_like(v_c)

    x = x_ref[...]
    mod = mod_ref[...]
    proj = _in_proj(x, (mod[0:1], mod[1:2]), w)
    u = proj[:, :LRU_WIDTH]
    gate = proj[:, LRU_WIDTH:2 * LRU_WIDTH]
    o2 = 2 * LRU_WIDTH

    rowid = lax.broadcasted_iota(jnp.int32, (ts, 1), 0)
    u_ext = jnp.concatenate([conv_c[...], u], axis=0)
    s1, s2, s3 = (pltpu.roll(u_ext, d, axis=0)[SUBLANES:] for d in (1, 2, 3))
    uc = _conv_taps(u, s1, s2, s3, w)
    conv_c[...] = u[ts - SUBLANES:]
    ulast_ref[...] = u[ts - SUBLANES:]

    first_pos = None if seq_start is None else jnp.logical_and(rowid == 0, seq_start)
    a, bt = _lru_coeffs(uc, w, first_pos)
    hs = _chain_groups(*_group_scan(a, bt), h_c[0:1, :])
    h_tail = hs[ts - SUBLANES:]
    h_c[...] = jnp.broadcast_to(h_tail[SUBLANES - 1:SUBLANES, :], h_c.shape)
    hlast_ref[...] = h_tail
    lru_out = hs * jax.nn.gelu(gate)

    cos = cos_ref[...]
    sin = sin_ref[...]
    lane = lax.broadcasted_iota(jnp.int32, (1, LANES), 1)
    first_half = (lane % HEAD_DIM) < (HEAD_DIM // 2)
    qcols = [_rope128(proj[:, o2 + c * LANES:o2 + (c + 1) * LANES], cos, sin, first_half) * (HEAD_DIM ** -0.5)
             for c in range(4)]
    k = _rope128(proj[:, o2 + Q_WIDTH:o2 + Q_WIDTH + KV_WIDTH], cos, sin, first_half)
    v = proj[:, o2 + Q_WIDTH + KV_WIDTH:]
    k_ext = jnp.concatenate([k_c[...], k], axis=0).astype(BF16)
    v_ext = jnp.concatenate([v_c[...], v], axis=0).astype(BF16)
    k_c[...] = k[ts - WINDOW:]
    v_c[...] = v[ts - WINDOW:]
    klast_ref[...] = k[ts - WINDOW:]
    vlast_ref[...] = v[ts - WINDOW:]

    qi = lax.broadcasted_iota(jnp.int32, (WINDOW, 2 * WINDOW), 0)
    kj = lax.broadcasted_iota(jnp.int32, (WINDOW, 2 * WINDOW), 1)
    band = jnp.logical_and(kj > qi, kj <= qi + WINDOW)
    lane_lo = lane < HEAD_DIM
    grow = lax.broadcasted_iota(jnp.int32, (GROUP * WINDOW, 1), 0) // WINDOW
    attn_cols = [[] for _ in range(4)]
    for blk in range(ts // WINDOW):
        if seq_start is None or blk > 0:
            mask = band
        else:
            mask = jnp.logical_and(band, jnp.logical_or(kj >= WINDOW, jnp.logical_not(seq_start)))
        mask4 = jnp.concatenate([mask] * GROUP, axis=0)
        kb = k_ext[blk * WINDOW:(blk + 2) * WINDOW]
        vb = v_ext[blk * WINDOW:(blk + 2) * WINDOW]
        outs = []
        for kv in range(N_KV_HEADS):
            sel = lane_lo if kv == 0 else jnp.logical_not(lane_lo)
            qs = jnp.concatenate(
                [jnp.where(sel, qc[blk * WINDOW:(blk + 1) * WINDOW], 0.0) for qc in qcols], axis=0).astype(BF16)
            s = lax.dot_general(qs, kb, (((1,), (1,)), ((), ())), preferred_element_type=F32)
            s = jnp.where(mask4, s, NEG_BIG)
            sink_col = jnp.zeros((GROUP * WINDOW, 1), F32)
            for g in range(GROUP):
                sink_col = jnp.where(grow == g, sinks_ref[kv * GROUP + g], sink_col)
            p, denom = _softmax_sink_pv(s, sink_col, vb)
            outs.append(jnp.dot(p.astype(BF16), vb, preferred_element_type=F32) / denom)
        for c in range(4):
            attn_cols[c].append(jnp.where(lane_lo, outs[0][c * WINDOW:(c + 1) * WINDOW],
                                          outs[1][c * WINDOW:(c + 1) * WINDOW]))
    attn = jnp.concatenate([jnp.concatenate(cols, axis=0) for cols in attn_cols], axis=1)

    mix = jnp.concatenate([lru_out, attn], axis=1)
    x1, h2, logits = _post_mix(x, mix, (mod[2:3], mod[3:4], mod[4:5]), w)
    x1_ref[...] = x1
    h2_ref[...] = h2
    route, tot = _route_tile(logits)
    lg_ref[0][...] = route
    lg_ref[1][...] = jnp.broadcast_to(tot, lg_ref[1].shape)


def _expand_rows(m, t):
    b, wd = m.shape
    return jnp.broadcast_to(m[:, None, :], (b, t, wd)).reshape(b * t, wd)


def _sample_body(x_ref, mod_ref, cos_ref, sin_ref, sinks_ref, h0_ref, cprev_ref, ck_ref, cv_ref, w,
                 x1_ref, h2_ref, lg_ref, g2_ref, hs_ref, u_ref, ko_ref, vo_ref):
    bt_, t = SAMPLE_BT, SUBLANES
    rows = bt_ * t

    x = x_ref[...]
    mods = [_expand_rows(mod_ref[i], t) for i in range(6)]
    proj = _in_proj(x, (mods[0], mods[1]), w)
    u = proj[:, :LRU_WIDTH]
    gate = proj[:, LRU_WIDTH:2 * LRU_WIDTH]
    o2 = 2 * LRU_WIDTH
    u_ref[...] = u

    rowid = lax.broadcasted_iota(jnp.int32, (rows, 1), 0) % t
    cprev = cprev_ref[...]
    taps = []
    for d in (1, 2, 3):
        taps.append(jnp.where(rowid >= d, pltpu.roll(u, d, axis=0),
                              pltpu.roll(cprev, (d - (CONV_W - 1)) % rows, axis=0)))
    uc = _conv_taps(u, taps[0], taps[1], taps[2], w)

    a, bt = _lru_coeffs(uc, w, None)
    bt = bt + a * h0_ref[...]
    _, hs = _group_scan(a, bt)
    hs_ref[...] = hs
    lru_out = hs * jax.nn.gelu(gate)

    cos = cos_ref[...]
    sin = sin_ref[...]
    lane = lax.broadcasted_iota(jnp.int32, (1, LANES), 1)
    first_half = (lane % HEAD_DIM) < (HEAD_DIM // 2)
    qcols = [_rope128(proj[:, o2 + c * LANES:o2 + (c + 1) * LANES], cos, sin, first_half) * (HEAD_DIM ** -0.5)
             for c in range(4)]
    k = _rope128(proj[:, o2 + Q_WIDTH:o2 + Q_WIDTH + KV_WIDTH], cos, sin, first_half)
    v = proj[:, o2 + Q_WIDTH + KV_WIDTH:]
    k3 = k.reshape(bt_, t, KV_WIDTH)
    v3 = v.reshape(bt_, t, KV_WIDTH)
    ck = ck_ref[...]
    cv = cv_ref[...]
    ko_ref[:, :WINDOW - t, :] = ck[:, t:, :]
    ko_ref[:, WINDOW - t:, :] = k3
    vo_ref[:, :WINDOW - t, :] = cv[:, t:, :]
    vo_ref[:, WINDOW - t:, :] = v3

    ckb, cvb, k3b, v3b = ck.astype(BF16), cv.astype(BF16), k3.astype(BF16), v3.astype(BF16)
    lane_lo = lane < HEAD_DIM
    gq = GROUP * t
    tq = lax.broadcasted_iota(jnp.int32, (1, gq, 1), 1) % t
    mask_c = lax.broadcasted_iota(jnp.int32, (1, gq, WINDOW), 2) > tq
    mask_n = lax.broadcasted_iota(jnp.int32, (1, gq, t), 2) <= tq
    grow = lax.broadcasted_iota(jnp.int32, (1, gq, 1), 1) // t
    bdims = (((2,), (2,)), ((0,), (0,)))
    pdims = (((2,), (1,)), ((0,), (0,)))
    outs = []
    for kv in range(N_KV_HEADS):
        sel = lane_lo if kv == 0 else jnp.logical_not(lane_lo)
        q3 = jnp.concatenate([jnp.where(sel, qc, 0.0).reshape(bt_, t, LANES) for qc in qcols], axis=1).astype(BF16)
        sc = lax.dot_general(q3, ckb, bdims, preferred_element_type=F32)
        sn = lax.dot_general(q3, k3b, bdims, preferred_element_type=F32)
        sc = jnp.where(mask_c, sc, NEG_BIG)
        sn = jnp.where(mask_n, sn, NEG_BIG)
        sink_col = jnp.zeros((1, gq, 1), F32)
        for g in range(GROUP):
            sink_col = jnp.where(grow == g, sinks_ref[kv * GROUP + g], sink_col)
        m = jnp.maximum(jnp.maximum(jnp.max(sc, axis=-1, keepdims=True), jnp.max(sn, axis=-1, keepdims=True)),
                        sink_col)
        pc = jnp.exp(sc - m)
        pn = jnp.exp(sn - m)
        denom = jnp.sum(pc, axis=-1, keepdims=True) + jnp.sum(pn, axis=-1, keepdims=True) + jnp.exp(sink_col - m)
        o = (lax.dot_general(pc.astype(BF16), cvb, pdims, preferred_element_type=F32)
             + lax.dot_general(pn.astype(BF16), v3b, pdims, preferred_element_type=F32)) / denom
        outs.append(o)
    attn = jnp.concatenate(
        [jnp.where(lane_lo, outs[0][:, c * t:(c + 1) * t, :], outs[1][:, c * t:(c + 1) * t, :]).reshape(rows, LANES)
         for c in range(4)], axis=1)

    mix = jnp.concatenate([lru_out, attn], axis=1)
    x1, h2, logits = _post_mix(x, mix, (mods[2], mods[3], mods[4]), w)
    x1_ref[...] = x1
    h2_ref[...] = h2
    route, tot = _route_tile(logits)
    lg_ref[0][...] = route
    lg_ref[1][...] = jnp.broadcast_to(tot, lg_ref[1].shape)
    g2_ref[...] = mods[5]


def _prompt_kernel(steps_per_seq, x_ref, mod_ref, cos_ref, sin_ref, sinks_ref, *rest):
    nw = len(WEIGHT_NAMES)
    w = dict(zip(WEIGHT_NAMES, rest[:nw]))
    (x1_ref, h2_ref, route_ref, cnt_ref, hlast_ref, ulast_ref, klast_ref, vlast_ref,
     conv_c, h_c, k_c, v_c) = rest[nw:]
    seq_start = pl.program_id(0) % steps_per_seq == 0
    for sub in range(MIX_TILES):
        rows = pl.ds(sub * SEQ_TILE, SEQ_TILE)
        _prompt_body(seq_start if sub == 0 else None, x_ref.at[rows], mod_ref, cos_ref.at[rows], sin_ref.at[rows],
                     sinks_ref, w, x1_ref.at[rows], h2_ref.at[rows], (route_ref.at[rows], cnt_ref.at[sub]),
                     hlast_ref, ulast_ref, klast_ref, vlast_ref, conv_c, h_c, k_c, v_c)


def _sample_kernel(x_ref, mod_ref, cos_ref, sin_ref, sinks_ref, h0_ref, cprev_ref, ck_ref, cv_ref, *rest):
    nw = len(WEIGHT_NAMES)
    w = dict(zip(WEIGHT_NAMES, rest[:nw]))
    x1_ref, h2_ref, route_ref, cnt_ref, g2_ref, hs_ref, u_ref, ko_ref, vo_ref = rest[nw:]
    _sample_body(x_ref, mod_ref, cos_ref, sin_ref, sinks_ref, h0_ref, cprev_ref, ck_ref, cv_ref, w,
                 x1_ref, h2_ref, (route_ref, cnt_ref), g2_ref, hs_ref, u_ref, ko_ref, vo_ref)


PACK_W = D_MODEL // 2
U32 = jnp.int32


def _pack_bf16_pairs(x):
    return pltpu.pack_elementwise([x[:, :PACK_W], x[:, PACK_W:]], packed_dtype=BF16)


def _unpack_bf16_pairs(w):
    return tuple(pltpu.unpack_elementwise(w, index=k, packed_dtype=BF16, unpacked_dtype=F32).astype(BF16)
                 for k in range(2))


STRIP_SIZES = tuple(SUBLANES << b for b in range(6))
STRIP_LARGE = 64
SORT_ROWS = SEQ_TILE * TOP_K + N_EXPERTS * SUBLANES
TILE_WAIT_SIZES = tuple(SUBLANES << b for b in range(8))


def _for_strips(cnt_ref, off_ref, tile, buf_slot, hbm, sem, to_hbm, act):
    def e_body(e, local):
        n = cnt_ref[tile * N_EXPERTS + e]
        glob = off_ref[tile * N_EXPERTS + e]

        def pieces(sizes, done):
            for p in sizes:
                piece = n & p
                lo = pl.ds(pl.multiple_of(local + done, SUBLANES), p)
                gl = pl.ds(pl.multiple_of(glob + done, SUBLANES), p)

                @pl.when(piece != 0)
                def _():
                    if to_hbm:
                        act(pltpu.make_async_copy(buf_slot.at[lo], hbm.at[gl], sem))
                    else:
                        act(pltpu.make_async_copy(hbm.at[gl], buf_slot.at[lo], sem))
                done = done + piece

        large = tuple(p for p in reversed(STRIP_SIZES) if p >= STRIP_LARGE)
        small = tuple(p for p in reversed(STRIP_SIZES) if p < STRIP_LARGE)
        n_large = n & (-STRIP_LARGE)

        @pl.when(n_large != 0)
        def _():
            pieces(large, 0)
        pieces(small, n_large)
        return local + n
    lax.fori_loop(0, N_EXPERTS, e_body, 0)


def _wait_tile_rows(total, buf_slot, hbm, sem, to_hbm):
    for p in TILE_WAIT_SIZES:
        @pl.when((total & p) != 0)
        def _():
            if to_hbm:
                pltpu.make_async_copy(buf_slot.at[pl.ds(0, p)], hbm.at[pl.ds(0, p)], sem).wait()
            else:
                pltpu.make_async_copy(hbm.at[pl.ds(0, p)], buf_slot.at[pl.ds(0, p)], sem).wait()


def _dispatch_kernel(n_prompt_tiles, cnt_ref, off_ref, tot_ref, meta_ref, h2p_ref, h2s_ref, routep_ref, routes_ref,
                     xs_hbm, sbuf, zblk, sem, zsem):
    i = pl.program_id(0)
    nb = pl.num_programs(0)
    slot = i % 2
    n_blocks = xs_hbm.shape[0] // MOE_TM

    is_prompt = i < n_prompt_tiles
    h2 = jnp.where(is_prompt, h2p_ref[...], h2s_ref[...])
    route_t = jnp.where(is_prompt, routep_ref[...], routes_ref[...]).T
    r_pos = lax.broadcasted_iota(jnp.int32, (SORT_ROWS, SEQ_TILE), 0).astype(F32)
    perm = jnp.zeros((SORT_ROWS, SEQ_TILE), F32)
    for k in range(TOP_K):
        perm = perm + jnp.where(r_pos == route_t[ROUTE_R + k:ROUTE_R + k + 1, :], 1.0, 0.0)
    sbuf[slot] = _pack_bf16_pairs(jnp.dot(perm.astype(BF16), h2.astype(BF16), preferred_element_type=F32))

    _for_strips(cnt_ref, off_ref, i, sbuf.at[slot], xs_hbm, sem.at[slot], True, lambda cp: cp.start())

    @pl.when(i > 0)
    def _():
        _wait_tile_rows(tot_ref[jnp.maximum(i - 1, 0)], sbuf.at[1 - slot], xs_hbm, sem.at[1 - slot], True)

    @pl.when(i == nb - 1)
    def _():
        _wait_tile_rows(tot_ref[i], sbuf.at[slot], xs_hbm, sem.at[slot], True)
        zblk[...] = jnp.zeros_like(zblk)

        def for_region_tails(act):
            def e_body(e, carry):
                start = meta_ref[e]
                n = meta_ref[N_EXPERTS + e] - start
                done = 0
                for p in reversed(STRIP_SIZES[:-1]):
                    piece = n & p
                    rows = pl.ds(pl.multiple_of(start + done, SUBLANES), p)

                    @pl.when(piece != 0)
                    def _():
                        act(pltpu.make_async_copy(zblk.at[pl.ds(0, p)], xs_hbm.at[rows], zsem.at[0]))
                    done = done + piece
                return carry
            lax.fori_loop(0, N_EXPERTS, e_body, 0)

        def for_tail_blocks(act):
            def b_body(j, carry):
                act(pltpu.make_async_copy(zblk, xs_hbm.at[pl.ds(pl.multiple_of(j * MOE_TM, MOE_TM), MOE_TM)],
                                          zsem.at[0]))
                return carry
            lax.fori_loop(meta_ref[2 * N_EXPERTS], n_blocks, b_body, 0)

        for_region_tails(lambda cp: cp.start())
        for_tail_blocks(lambda cp: cp.start())
        for_region_tails(lambda cp: cp.wait())
        for_tail_blocks(lambda cp: cp.wait())


def _dispatch(cnt8, tile_off, tot8, meta, h2_p, h2_s, route_p, route_s, n_rows):
    npt = h2_p.shape[0] // SEQ_TILE
    p_tile = lambda i, *_: (jnp.minimum(i, npt - 1), 0)
    s_tile = lambda i, *_: (jnp.maximum(i - npt, 0), 0)
    nt = tot8.shape[0]
    tt = SEQ_TILE
    grid_spec = pltpu.PrefetchScalarGridSpec(
        num_scalar_prefetch=4,
        grid=(nt,),
        in_specs=[pl.BlockSpec((tt, D_MODEL), p_tile), pl.BlockSpec((tt, D_MODEL), s_tile),
                  pl.BlockSpec((tt, LANES), p_tile), pl.BlockSpec((tt, LANES), s_tile)],
        out_specs=pl.BlockSpec(memory_space=pl.ANY),
        scratch_shapes=[pltpu.VMEM((2, SORT_ROWS, PACK_W), U32), pltpu.VMEM((MOE_TM, PACK_W), U32),
                        pltpu.SemaphoreType.DMA((2,)), pltpu.SemaphoreType.DMA((1,))],
    )
    return pl.pallas_call(
        functools.partial(_dispatch_kernel, npt),
        grid_spec=grid_spec,
        out_shape=jax.ShapeDtypeStruct((n_rows, PACK_W), U32),
        compiler_params=pltpu.CompilerParams(dimension_semantics=("arbitrary",), vmem_limit_bytes=VMEM_LIMIT),
        name="dispatch",
    )(cnt8, tile_off, tot8, meta, h2_p, h2_s, route_p, route_s)


def _expert_mlp(words, w1b, b1_ref, w2b, b2_ref, act_ref):
    xb = jnp.concatenate(_unpack_bf16_pairs(words), axis=1)
    q = D_FF // 4
    for c in range(4):
        zg = jnp.dot(xb, w1b[:, c * q:(c + 1) * q], preferred_element_type=F32) + b1_ref[:, c * q:(c + 1) * q]
        zl = (jnp.dot(xb, w1b[:, D_FF + c * q:D_FF + (c + 1) * q], preferred_element_type=F32)
              + b1_ref[:, D_FF + c * q:D_FF + (c + 1) * q])
        glu = jnp.minimum(zg, SWIGLU_LIMIT)
        lin = jnp.clip(zl, -SWIGLU_LIMIT, SWIGLU_LIMIT)
        act_ref[:, c * q:(c + 1) * q] = (glu * jax.nn.sigmoid(SWIGLU_ALPHA * glu) * (lin + 1.0)).astype(BF16)
    return _pack_bf16_pairs(jnp.dot(act_ref[...], w2b[...], preferred_element_type=F32) + b2_ref[...])


def _moe_kernel(row_ref, size_ref, exp_ref, wt_ref, meta_ref, xs_hbm, w1_hbm, b1_ref, w2_hbm, b2_ref, ys_hbm,
                xin, yout, w1f, w2f, w1b, w2b, act, isem, osem, wsem):
    i = pl.program_id(0)
    n = pl.num_programs(0)
    slot = i % 2
    tm = MOE_TM
    n_blocks = ys_hbm.shape[0] // tm

    def for_chunk(j, fn):
        r = pl.multiple_of(row_ref[j], tm)
        for code in MOE_CHUNK_CODES:
            @pl.when(size_ref[j] == code)
            def _():
                fn(r, code * tm)

    def in_copy(r, rows, s):
        return pltpu.make_async_copy(xs_hbm.at[pl.ds(r, rows)], xin.at[s, pl.ds(0, rows)], isem.at[s])

    def out_copy(r, rows, s):
        return pltpu.make_async_copy(yout.at[s, pl.ds(0, rows)], ys_hbm.at[pl.ds(r, rows)], osem.at[s])

    def start_in(j, s):
        for_chunk(j, lambda r, rows: in_copy(r, rows, s).start())

    def wait_in(j, s):
        for_chunk(j, lambda r, rows: in_copy(r, rows, s).wait())

    def start_out(j, s):
        for_chunk(j, lambda r, rows: out_copy(r, rows, s).start())

    def wait_out(j, s):
        for_chunk(j, lambda r, rows: out_copy(r, rows, s).wait())

    @pl.when(i == 0)
    def _():
        start_in(0, 0)

    @pl.when(i + 1 < n)
    def _():
        start_in(i + 1, 1 - slot)

    @pl.when(i >= 2)
    def _():
        wait_out(i - 2, slot)

    def weight_copies(e, ws):
        return (pltpu.make_async_copy(w1_hbm.at[e], w1f.at[ws], wsem.at[ws]),
                pltpu.make_async_copy(w2_hbm.at[e], w2f.at[ws], wsem.at[ws]))

    n_steps = n
    @pl.when(wt_ref[i] == 1)
    def _():
        ws = wt_ref[n_steps + i]
        nxt = wt_ref[2 * n_steps + i]

        @pl.when(i == 0)
        def _():
            for cp in weight_copies(exp_ref[0], 0):
                cp.start()

        for cp in weight_copies(exp_ref[i], ws):
            cp.wait()

        @pl.when(nxt >= 0)
        def _():
            for cp in weight_copies(nxt, 1 - ws):
                cp.start()

        chunk = 128
        def cast_body(c, carry):
            k0 = pl.multiple_of(c * chunk, chunk)
            w1b[pl.ds(k0, chunk), :] = w1f[ws, pl.ds(k0, chunk), :].astype(BF16)
            w2b[pl.ds(k0, chunk), :] = w2f[ws, pl.ds(k0, chunk), :].astype(BF16)
            return carry
        lax.fori_loop(0, D_MODEL // chunk, cast_body, 0)

    wait_in(i, slot)

    for code in MOE_CHUNK_CODES:
        rows = code * tm

        @pl.when(size_ref[i] == code)
        def _():
            yout[slot, 0:rows] = _expert_mlp(xin[slot, 0:rows], w1b, b1_ref.at[exp_ref[i]], w2b,
                                             b2_ref.at[exp_ref[i]], act.at[pl.ds(0, rows)])

    start_out(i, slot)

    @pl.when(i == n - 1)
    def _():
        @pl.when(i >= 1)
        def _():
            wait_out(i - 1, 1 - slot)
        wait_out(i, slot)
        yout[1, 0:tm] = jnp.zeros((tm, PACK_W), U32)

        def zero_block(j):
            return pltpu.make_async_copy(yout.at[1, pl.ds(0, tm)], ys_hbm.at[pl.ds(pl.multiple_of(j * tm, tm), tm)],
                                         osem.at[1])

        def start_body(j, carry):
            zero_block(j).start()
            return carry

        def wait_body(j, carry):
            zero_block(j).wait()
            return carry
        lax.fori_loop(meta_ref[2 * N_EXPERTS], n_blocks, start_body, 0)
        lax.fori_loop(meta_ref[2 * N_EXPERTS], n_blocks, wait_body, 0)


def _moe(chunk_row, chunk_size, chunk_exp, weight_tbl, meta, xs, w1, b1, w2, b2):
    whole = lambda i, *_: (0, 0, 0)
    grid_spec = pltpu.PrefetchScalarGridSpec(
        num_scalar_prefetch=5,
        grid=(chunk_row.shape[0],),
        in_specs=[
            pl.BlockSpec(memory_space=pl.ANY),
            pl.BlockSpec(memory_space=pl.ANY),
            pl.BlockSpec((N_EXPERTS, 1, 2 * D_FF), whole),
            pl.BlockSpec(memory_space=pl.ANY),
            pl.BlockSpec((N_EXPERTS, 1, D_MODEL), whole),
        ],
        out_specs=pl.BlockSpec(memory_space=pl.ANY),
        scratch_shapes=[pltpu.VMEM((2, MOE_CH, PACK_W), U32), pltpu.VMEM((2, MOE_CH, PACK_W), U32),
                        pltpu.VMEM((2, D_MODEL, 2 * D_FF), F32), pltpu.VMEM((2, D_FF, D_MODEL), F32),
                        pltpu.VMEM((D_MODEL, 2 * D_FF), BF16), pltpu.VMEM((D_FF, D_MODEL), BF16),
                        pltpu.VMEM((MOE_CH, D_FF), BF16),
                        pltpu.SemaphoreType.DMA((2,)), pltpu.SemaphoreType.DMA((2,)), pltpu.SemaphoreType.DMA((2,))],
    )
    return pl.pallas_call(
        _moe_kernel,
        grid_spec=grid_spec,
        out_shape=jax.ShapeDtypeStruct(xs.shape, U32),
        compiler_params=pltpu.CompilerParams(dimension_semantics=("arbitrary",), vmem_limit_bytes=VMEM_LIMIT),
        name="moe",
    )(chunk_row, chunk_size, chunk_exp, weight_tbl, meta, xs, w1, b1, w2, b2)


def _combine_kernel(n_prompt_tiles, cnt_ref, off_ref, tot_ref, ys_hbm, routep_ref, routes_ref, x1p_ref, x1s_ref,
                    modp_ref, g2s_ref, fg_ref, op_ref, os_ref, buf, sem):
    i = pl.program_id(0)
    nb = pl.num_programs(0)
    slot = i % 2

    def fetch(tile, s):
        _for_strips(cnt_ref, off_ref, tile, buf.at[s], ys_hbm, sem.at[s], False, lambda cp: cp.start())

    @pl.when(i == 0)
    def _():
        buf[...] = jnp.zeros_like(buf)
        fetch(0, 0)

    @pl.when(i + 1 < nb)
    def _():
        fetch(i + 1, 1 - slot)

    _wait_tile_rows(tot_ref[i], buf.at[slot], ys_hbm, sem.at[slot], False)

    is_prompt = i < n_prompt_tiles
    route = jnp.where(is_prompt, routep_ref[...], routes_ref[...])
    c_pos = lax.broadcasted_iota(jnp.int32, (SEQ_TILE, SORT_ROWS), 1).astype(F32)
    gmat = jnp.zeros((SEQ_TILE, SORT_ROWS), F32)
    for k in range(TOP_K):
        gmat = gmat + jnp.where(c_pos == route[:, ROUTE_R + k:ROUTE_R + k + 1],
                                route[:, ROUTE_G + k:ROUTE_G + k + 1], 0.0)
    g_bf = gmat.astype(BF16)
    ff = jnp.concatenate([jnp.dot(g_bf, yb, preferred_element_type=F32) for yb in _unpack_bf16_pairs(buf[slot])],
                         axis=1)
    g2 = jnp.where(i < n_prompt_tiles, modp_ref[5:6, :], g2s_ref[...])
    x = jnp.where(is_prompt, x1p_ref[...], x1s_ref[...]) + g2 * ff
    y = _rms(x, fg_ref[...])

    @pl.when(i < n_prompt_tiles)
    def _():
        op_ref[...] = y

    @pl.when(i >= n_prompt_tiles)
    def _():
        os_ref[...] = y


def _combine(cnt8, tile_off, tot8, ys, route_p, route_s, x1_p, x1_s, mod_p, g2_rows, final_g, tiles_per_batch):
    nt = tot8.shape[0]
    tt = SEQ_TILE
    npt = x1_p.shape[0] // tt
    p_tile = lambda i, *_: (jnp.minimum(i, npt - 1), 0)
    s_tile = lambda i, *_: (jnp.maximum(i - npt, 0), 0)
    grid_spec = pltpu.PrefetchScalarGridSpec(
        num_scalar_prefetch=3,
        grid=(nt,),
        in_specs=[
            pl.BlockSpec(memory_space=pl.ANY),
            pl.BlockSpec((tt, LANES), p_tile), pl.BlockSpec((tt, LANES), s_tile),
            pl.BlockSpec((tt, D_MODEL), p_tile), pl.BlockSpec((tt, D_MODEL), s_tile),
            pl.BlockSpec((None, 6, D_MODEL), lambda i, *_: (jnp.minimum(i, npt - 1) // tiles_per_batch, 0, 0)),
            pl.BlockSpec((tt, D_MODEL), s_tile),
            pl.BlockSpec((1, D_MODEL), lambda i, *_: (0, 0)),
        ],
        out_specs=(pl.BlockSpec((tt, D_MODEL), p_tile), pl.BlockSpec((tt, D_MODEL), s_tile)),
        scratch_shapes=[pltpu.VMEM((2, SORT_ROWS, PACK_W), U32), pltpu.SemaphoreType.DMA((2,))],
    )
    return pl.pallas_call(
        functools.partial(_combine_kernel, npt),
        grid_spec=grid_spec,
        out_shape=(jax.ShapeDtypeStruct((npt * tt, D_MODEL), F32),
                   jax.ShapeDtypeStruct(((nt - npt) * tt, D_MODEL), F32)),
        compiler_params=pltpu.CompilerParams(dimension_semantics=("arbitrary",), vmem_limit_bytes=VMEM_LIMIT),
        name="combine",
    )(cnt8, tile_off, tot8, ys, route_p, route_s, x1_p, x1_s, mod_p, g2_rows, final_g)


def _block_diag_halves(wg):
    per_half = LRU_BLOCKS // 2
    w4 = wg.reshape(2, per_half, LRU_BLOCK_W, LRU_BLOCK_W)
    on_diag = jnp.eye(per_half, dtype=bool)[None, :, None, :, None]
    dense = jnp.where(on_diag, w4[:, :, :, None, :], 0.0)
    return dense.reshape(2, per_half * LRU_BLOCK_W, per_half * LRU_BLOCK_W).astype(BF16)


def _rope_tables(pos):
    half = HEAD_DIM // 2
    inv = ROPE_THETA ** (-jnp.arange(half, dtype=F32) / half)
    ang = pos.astype(F32)[:, None] * inv[None, :]
    cos = jnp.cos(ang)
    sin = jnp.sin(ang)
    cos128 = jnp.concatenate([cos, cos, cos, cos], axis=1)
    sin128 = jnp.concatenate([-sin, sin, -sin, sin], axis=1)
    return cos128, sin128


def _full_spec(arr, grid_rank):
    zeros = (0,) * arr.ndim
    if grid_rank == 1:
        return pl.BlockSpec(arr.shape, lambda i: zeros)
    return pl.BlockSpec(arr.shape, lambda b, j: zeros)


def kernel(x_prompt, x_sample, state_lru_h, state_conv, cache_win_k, cache_win_v, c_prompt, c_sample, w_ada, b_ada, norm1_g, w_in, b_in, conv_w, conv_b, lru_wa, lru_ba, lru_wx, lru_bx, lru_lambda, attn_sinks, w_out, b_out, norm2_g, w_router, b_router, w1, b1, w2, b2, final_g):
    bp, seq, _ = x_prompt.shape
    bd, tdec, _ = x_sample.shape
    assert tdec == SUBLANES and seq % (MIX_TILES * SEQ_TILE) == 0 and bd % SAMPLE_BT == 0
    assert SAMPLE_BT * tdec == SEQ_TILE
    n_prompt = bp * seq
    n_sample = bd * tdec
    n_tok = n_prompt + n_sample
    l = 0

    head_perm = [h for c in range(4) for h in (c, GROUP + c)]
    o2 = 2 * LRU_WIDTH

    def permute_heads(arr, axis, start):
        take = lambda a, b: lax.slice_in_dim(arr, a, b, axis=axis)
        heads = [take(start + h * HEAD_DIM, start + (h + 1) * HEAD_DIM) for h in head_perm]
        return jnp.concatenate([take(0, start)] + heads + [take(start + Q_WIDTH, arr.shape[axis])], axis=axis)

    sinks_perm = attn_sinks[l]

    wr = jnp.tile(w_router[l], (1, TOP_K))
    wr_hi = wr.astype(BF16)
    weights = dict(
        norm1_g=norm1_g[l][None, :], w_in=permute_heads(w_in[l], 1, o2).astype(BF16),
        b_in=permute_heads(b_in[l], 0, o2)[None, :],
        conv_w=conv_w[l], conv_b=conv_b[l][None, :],
        ga=_block_diag_halves(lru_wa[l]), gx=_block_diag_halves(lru_wx[l]),
        lru_ba=lru_ba[l][None, :], lru_bx=lru_bx[l][None, :], lam=lru_lambda[l][None, :],
        w_out=permute_heads(w_out[l], 0, LRU_WIDTH).astype(BF16), b_out=b_out[l][None, :], norm2_g=norm2_g[l][None, :],
        wr_hi=wr_hi, wr_lo=(wr - wr_hi.astype(F32)).astype(BF16),
        b_router=jnp.tile(b_router[l], TOP_K)[None, :],
    )
    wlist = [weights[n] for n in WEIGHT_NAMES]

    mod_p, mod_s = _ada(jnp.concatenate([c_prompt, c_sample], axis=0), bp, w_ada[l], b_ada[l][None, :])
    mod_p = mod_p.reshape(bp, 6, D_MODEL)

    cos_p, sin_p = _rope_tables(jnp.arange(seq, dtype=jnp.int32))
    cos_s, sin_s = _rope_tables(PAST_LEN + jnp.arange(tdec, dtype=jnp.int32))
    cos_s = jnp.tile(cos_s, (SAMPLE_BT, 1))
    sin_s = jnp.tile(sin_s, (SAMPLE_BT, 1))
    h0_rows = jnp.pad(state_lru_h[l][:, None, :], ((0, 0), (0, tdec - 1), (0, 0))).reshape(n_sample, LRU_WIDTH)
    cprev_rows = jnp.pad(state_conv[l], ((0, 0), (0, tdec - (CONV_W - 1)), (0, 0))).reshape(n_sample, LRU_WIDTH)
    ck = cache_win_k[l].reshape(bd, WINDOW, KV_WIDTH)
    cv = cache_win_v[l].reshape(bd, WINDOW, KV_WIDTH)
    npt = n_prompt // SEQ_TILE
    nst = n_sample // SEQ_TILE
    mix_rows = MIX_TILES * SEQ_TILE
    steps_per_seq = seq // mix_rows
    wspecs = [_full_spec(a, 1) for a in wlist]

    rows_p = lambda width: pl.BlockSpec((mix_rows, width), lambda i: (i, 0))
    tail_p = lambda rows, width: pl.BlockSpec((None, rows, width), lambda i: (i // steps_per_seq, 0, 0))
    (x1_p, h2_p, route_p, cnt_p, hlast_p, ulast_p, klast_p, vlast_p) = pl.pallas_call(
        functools.partial(_prompt_kernel, steps_per_seq),
        grid=(n_prompt // mix_rows,),
        in_specs=[rows_p(D_MODEL),
                  pl.BlockSpec((None, 6, D_MODEL), lambda i: (i // steps_per_seq, 0, 0)),
                  pl.BlockSpec((mix_rows, LANES), lambda i: (i % steps_per_seq, 0)),
                  pl.BlockSpec((mix_rows, LANES), lambda i: (i % steps_per_seq, 0)),
                  pl.BlockSpec(memory_space=pltpu.SMEM)] + wspecs,
        out_specs=(rows_p(D_MODEL), rows_p(D_MODEL), rows_p(LANES),
                   pl.BlockSpec((MIX_TILES, SUBLANES, LANES), lambda i: (i, 0, 0)),
                   tail_p(SUBLANES, LRU_WIDTH), tail_p(SUBLANES, LRU_WIDTH),
                   tail_p(WINDOW, KV_WIDTH), tail_p(WINDOW, KV_WIDTH)),
        out_shape=(
            jax.ShapeDtypeStruct((n_prompt, D_MODEL), F32),
            jax.ShapeDtypeStruct((n_prompt, D_MODEL), F32),
            jax.ShapeDtypeStruct((n_prompt, LANES), F32),
            jax.ShapeDtypeStruct((npt, SUBLANES, LANES), F32),
            jax.ShapeDtypeStruct((bp, SUBLANES, LRU_WIDTH), F32),
            jax.ShapeDtypeStruct((bp, SUBLANES, LRU_WIDTH), F32),
            jax.ShapeDtypeStruct((bp, WINDOW, KV_WIDTH), F32),
            jax.ShapeDtypeStruct((bp, WINDOW, KV_WIDTH), F32),
        ),
        scratch_shapes=[pltpu.VMEM((SUBLANES, LRU_WIDTH), F32), pltpu.VMEM((SUBLANES, LRU_WIDTH), F32),
                        pltpu.VMEM((WINDOW, KV_WIDTH), F32), pltpu.VMEM((WINDOW, KV_WIDTH), F32)],
        compiler_params=pltpu.CompilerParams(dimension_semantics=("arbitrary",), vmem_limit_bytes=VMEM_LIMIT),
        name="prompt_mixer",
    )(x_prompt.reshape(n_prompt, D_MODEL), mod_p, cos_p, sin_p, sinks_perm, *wlist)

    rows_s = lambda width: pl.BlockSpec((SEQ_TILE, width), lambda i: (i, 0))
    cache_spec = pl.BlockSpec((SAMPLE_BT, WINDOW, KV_WIDTH), lambda i: (i, 0, 0))
    (x1_s, h2_s, route_s, cnt_s, g2_rows, hs_s, u_s, s_k, s_v) = pl.pallas_call(
        _sample_kernel,
        grid=(nst,),
        in_specs=[rows_s(D_MODEL),
                  pl.BlockSpec((6, SAMPLE_BT, D_MODEL), lambda i: (0, i, 0)),
                  pl.BlockSpec((SEQ_TILE, LANES), lambda i: (0, 0)),
                  pl.BlockSpec((SEQ_TILE, LANES), lambda i: (0, 0)),
                  pl.BlockSpec(memory_space=pltpu.SMEM),
                  rows_s(LRU_WIDTH), rows_s(LRU_WIDTH), cache_spec, cache_spec] + wspecs,
        out_specs=(rows_s(D_MODEL), rows_s(D_MODEL), rows_s(LANES),
                   pl.BlockSpec((None, SUBLANES, LANES), lambda i: (i, 0, 0)),
                   rows_s(D_MODEL), rows_s(LRU_WIDTH), rows_s(LRU_WIDTH), cache_spec, cache_spec),
        out_shape=(
            jax.ShapeDtypeStruct((n_sample, D_MODEL), F32), jax.ShapeDtypeStruct((n_sample, D_MODEL), F32),
            jax.ShapeDtypeStruct((n_sample, LANES), F32), jax.ShapeDtypeStruct((nst, SUBLANES, LANES), F32),
            jax.ShapeDtypeStruct((n_sample, D_MODEL), F32),
            jax.ShapeDtypeStruct((n_sample, LRU_WIDTH), F32),
            jax.ShapeDtypeStruct((n_sample, LRU_WIDTH), F32),
            jax.ShapeDtypeStruct((bd, WINDOW, KV_WIDTH), F32),
            jax.ShapeDtypeStruct((bd, WINDOW, KV_WIDTH), F32),
        ),
        compiler_params=pltpu.CompilerParams(dimension_semantics=("arbitrary",), vmem_limit_bytes=VMEM_LIMIT),
        name="sample_mixer",
    )(x_sample.reshape(n_sample, D_MODEL), mod_s, cos_s, sin_s, sinks_perm, h0_rows, cprev_rows, ck, cv, *wlist)
    tile_cnt = jnp.concatenate([cnt_p, cnt_s], axis=0)

    n_tiles = npt + nst
    n_assign = n_tok * TOP_K
    max_rows = n_assign + n_tiles * N_EXPERTS * (SUBLANES - 1) + N_EXPERTS * (MOE_TM - 1)
    n_blocks = -(-max_rows // MOE_TM)
    cnt = tile_cnt[:, 0, :N_EXPERTS].astype(jnp.int32)
    cnt8 = (cnt + SUBLANES - 1) // SUBLANES * SUBLANES
    counts = jnp.sum(cnt8, axis=0)
    pcounts = (counts + MOE_TM - 1) // MOE_TM * MOE_TM
    pend = jnp.cumsum(pcounts)
    pstart = pend - pcounts
    tile_off = pstart[None, :] + jnp.cumsum(cnt8, axis=0) - cnt8
    tot8 = jnp.sum(cnt8, axis=1)
    meta = jnp.concatenate([pstart + counts, pend, pend[-1:] // MOE_TM]).astype(jnp.int32)
    cnt8_flat = cnt8.reshape(-1)
    off_flat = tile_off.reshape(-1).astype(jnp.int32)

    xs = _dispatch(cnt8_flat, off_flat, tot8, meta, h2_p, h2_s, route_p, route_s, n_blocks * MOE_TM)

    n_tm = pcounts // MOE_TM
    n_big = n_tm // 4
    has2 = (n_tm // 2) % 2
    n_ch = n_big + has2 + n_tm % 2
    ch_end = jnp.cumsum(n_ch)
    ch_start = ch_end - n_ch
    n_chunks = n_blocks // 4 + 2 * N_EXPERTS
    ci = jnp.arange(n_chunks, dtype=jnp.int32)
    owner = jnp.logical_and(ch_start[None, :] <= ci[:, None], ci[:, None] < ch_end[None, :])
    pick = lambda v: jnp.sum(jnp.where(owner, v[None, :], 0), axis=1)
    live = ci < ch_end[-1]
    local = ci - pick(ch_start)
    chunk_exp = jnp.where(live, pick(jnp.arange(N_EXPERTS, dtype=jnp.int32)), N_EXPERTS - 1).astype(jnp.int32)
    big_c, has2_c = pick(n_big), pick(has2)
    is_big = local < big_c
    is_two = jnp.logical_and(local == big_c, has2_c == 1)
    blocks_before = jnp.where(is_big, local * 4, big_c * 4 + jnp.where(is_two, 0, has2_c * 2))
    chunk_row = jnp.where(live, pick(pstart) + blocks_before * MOE_TM, 0).astype(jnp.int32)
    chunk_size = jnp.where(live, jnp.where(is_big, 4, jnp.where(is_two, 2, 1)), 0).astype(jnp.int32)
    has_rows = n_ch > 0
    e_ar = jnp.arange(N_EXPERTS, dtype=jnp.int32)
    w_slot = (jnp.cumsum(has_rows.astype(jnp.int32)) - 1) % 2
    later = jnp.logical_and(has_rows[None, :], e_ar[None, :] > e_ar[:, None])
    nxt = jnp.min(jnp.where(later, e_ar[None, :], N_EXPERTS), axis=1)
    nxt = jnp.where(nxt == N_EXPERTS, -1, nxt)
    weight_tbl = jnp.concatenate([jnp.logical_and(live, local == 0).astype(jnp.int32),
                                  pick(w_slot), jnp.where(live, pick(nxt), -1)]).astype(jnp.int32)
    ys = _moe(chunk_row, chunk_size, chunk_exp, weight_tbl, meta, xs, w1[l], b1[l][:, None, :], w2[l], b2[l][:, None, :])

    y_p, y_s = _combine(cnt8_flat, off_flat, tot8, ys, route_p, route_s, x1_p, x1_s, mod_p, g2_rows,
                        final_g[None, :], seq // SEQ_TILE)

    y_prompt = y_p.reshape(bp, seq, D_MODEL)
    y_sample = y_s.reshape(bd, tdec, D_MODEL)
    p_h = hlast_p[:, SUBLANES - 1, :][None]
    p_c = ulast_p[:, SUBLANES - (CONV_W - 1):, :][None]
    p_k = klast_p.reshape(1, bp, WINDOW, N_KV_HEADS, HEAD_DIM)
    p_v = vlast_p.reshape(1, bp, WINDOW, N_KV_HEADS, HEAD_DIM)
    s_h = hs_s.reshape(bd, tdec, LRU_WIDTH)[:, tdec - 1, :][None]
    s_c = u_s.reshape(bd, tdec, LRU_WIDTH)[:, tdec - (CONV_W - 1):, :][None]
    s_kk = s_k.reshape(1, bd, WINDOW, N_KV_HEADS, HEAD_DIM)
    s_vv = s_v.reshape(1, bd, WINDOW, N_KV_HEADS, HEAD_DIM)
    return (y_prompt, y_sample, p_h, p_c, p_k, p_v, s_h, s_c, s_kk, s_vv)
```

```python
import functools

import jax
import jax.numpy as jnp
from jax import lax
from jax.experimental import pallas as pl
from jax.experimental.pallas import tpu as pltpu

F32 = jnp.float32
BF16 = jnp.bfloat16

D_MODEL = 1024
LRU_WIDTH = 512
LRU_BLOCKS = 8
LRU_BLOCK_W = LRU_WIDTH // LRU_BLOCKS
CONV_W = 4
LRU_C = 8.0
HEAD_DIM = 64
N_HEADS = 8
N_KV_HEADS = 2
GROUP = N_HEADS // N_KV_HEADS
WINDOW = 128
ROPE_THETA = 10000.0
N_EXPERTS = 32
TOP_K = 4
D_FF = D_MODEL
SWIGLU_LIMIT = 7.0
SWIGLU_ALPHA = 1.702
NORM_EPS = 1e-5
PAST_LEN = 8192
Q_WIDTH = N_HEADS * HEAD_DIM
KV_WIDTH = N_KV_HEADS * HEAD_DIM
IN_WIDTH = 2 * LRU_WIDTH + Q_WIDTH + 2 * KV_WIDTH

LANES = 128
SUBLANES = 8
SEQ_TILE = 256
MIX_TILES = 2
SAMPLE_BT = 32
MOE_TM = 128
MOE_CHUNK_CODES = (8, 4, 2, 1)
MOE_CH = MOE_CHUNK_CODES[0] * MOE_TM
NEG_BIG = -1e30
VMEM_LIMIT = 56 * 1024 * 1024


def _rms(x, g):
    return x * lax.rsqrt(jnp.mean(x * x, axis=-1, keepdims=True) + NORM_EPS) * g


def _rms_mod(x, gain, shift):
    return x * lax.rsqrt(jnp.mean(x * x, axis=-1, keepdims=True) + NORM_EPS) * gain + shift


def _group_scan(a, b):
    rows, width = a.shape
    groups = rows // SUBLANES
    a3 = a.reshape(groups, SUBLANES, width)
    b3 = b.reshape(groups, SUBLANES, width)
    t = lax.broadcasted_iota(jnp.int32, (1, SUBLANES, 1), 1)
    d = 1
    while d < SUBLANES:
        keep = t >= d
        a_s = jnp.where(keep, pltpu.roll(a3, d, axis=1), 1.0)
        b_s = jnp.where(keep, pltpu.roll(b3, d, axis=1), 0.0)
        b3 = a3 * b_s + b3
        a3 = a3 * a_s
        d *= 2
    return a3.reshape(rows, width), b3.reshape(rows, width)


def _chain_groups(a_grp, b_grp, h_in):
    rows = a_grp.shape[0]
    out = []
    carry = h_in
    for g in range(rows // SUBLANES):
        sl = slice(g * SUBLANES, (g + 1) * SUBLANES)
        hg = b_grp[sl] + a_grp[sl] * carry
        out.append(hg)
        carry = hg[SUBLANES - 1:SUBLANES]
    return jnp.concatenate(out, axis=0)


def _rope128(x, cos, sin_signed, first_half):
    sw = jnp.where(first_half, pltpu.roll(x, LANES - HEAD_DIM // 2, axis=1), pltpu.roll(x, HEAD_DIM // 2, axis=1))
    return x * cos + sw * sin_signed


def _softplus(x):
    return jnp.maximum(x, 0.0) + jnp.log1p(jnp.exp(-jnp.abs(x)))


def _lru_coeffs(uc, w, first_pos_mask):
    ub = uc.astype(BF16)
    half = LRU_WIDTH // 2
    ra = jnp.concatenate([jnp.dot(ub[:, :half], w['ga'][0], preferred_element_type=F32),
                          jnp.dot(ub[:, half:], w['ga'][1], preferred_element_type=F32)], axis=1)
    rx = jnp.concatenate([jnp.dot(ub[:, :half], w['gx'][0], preferred_element_type=F32),
                          jnp.dot(ub[:, half:], w['gx'][1], preferred_element_type=F32)], axis=1)
    r = jax.nn.sigmoid(ra + w['lru_ba'][...])
    i = jax.nn.sigmoid(rx + w['lru_bx'][...])
    log_a = -LRU_C * r * _softplus(-w['lam'][...])
    a = jnp.exp(log_a)
    om = 1.0 - a * a
    mult = jnp.where(om > 0.0, om * lax.rsqrt(om), 0.0)
    if first_pos_mask is not None:
        mult = jnp.where(first_pos_mask, 1.0, mult)
    return a, mult * i * uc


def _conv_taps(u, s1, s2, s3, w):
    cw = w['conv_w']
    return w['conv_b'][...] + s3 * cw[0:1, :] + s2 * cw[1:2, :] + s1 * cw[2:3, :] + u * cw[3:4, :]


def _in_proj(x, mod, w):
    sh1, sc1 = mod
    h = _rms_mod(x, w['norm1_g'][...] * (1.0 + sc1), sh1)
    return jnp.dot(h.astype(BF16), w['w_in'][...], preferred_element_type=F32) + w['b_in'][...]


def _post_mix(x, mix, mod, w):
    g1, sh2, sc2 = mod
    x1 = x + g1 * (jnp.dot(mix.astype(BF16), w['w_out'][...], preferred_element_type=F32) + w['b_out'][...])
    h2 = _rms_mod(x1, w['norm2_g'][...] * (1.0 + sc2), sh2)
    h2_hi = h2.astype(BF16)
    h2_lo = (h2 - h2_hi.astype(F32)).astype(BF16)
    logits = (jnp.dot(h2_hi, w['wr_hi'][...], preferred_element_type=F32)
              + jnp.dot(h2_lo, w['wr_hi'][...], preferred_element_type=F32)
              + jnp.dot(h2_hi, w['wr_lo'][...], preferred_element_type=F32)) + w['b_router'][...]
    return x1, h2, logits


ROUTE_E, ROUTE_G, ROUTE_R = 0, TOP_K, 2 * TOP_K


def _lane_roll1(v, shift):
    return pltpu.roll(jnp.broadcast_to(v, (SUBLANES, LANES)), shift, axis=1)[0:1]


def _route_tile(lg):
    rows = lg.shape[0]
    lane = lax.broadcasted_iota(jnp.int32, (1, LANES), 1)
    e_of = lane % N_EXPERTS
    grp = lane // N_EXPERTS
    e_id = e_of.astype(F32)
    onehot = jnp.zeros((rows, LANES), F32)
    vals, ids = [], []
    for k in range(TOP_K):
        m = jnp.max(lg, axis=1, keepdims=True)
        idx = jnp.min(jnp.where(lg == m, e_id, float(N_EXPERTS)), axis=1, keepdims=True)
        sel = e_id == idx
        lg = jnp.where(sel, -jnp.inf, lg)
        onehot = jnp.where(jnp.logical_and(sel, grp == k), 1.0, onehot)
        vals.append(m)
        ids.append(idx)
    ex = [jnp.exp(v - vals[0]) for v in vals]
    denom = ex[0] + ex[1] + ex[2] + ex[3]

    r_i = lax.broadcasted_iota(jnp.int32, (rows, rows), 0)
    c_i = lax.broadcasted_iota(jnp.int32, (rows, rows), 1)
    strict_lower = jnp.where(r_i > c_i, 1.0, 0.0).astype(BF16)
    prefix = jnp.dot(strict_lower, onehot.astype(BF16), preferred_element_type=F32)
    cnt = jnp.sum(onehot, axis=0, keepdims=True)
    base = jnp.zeros((1, LANES), F32)
    tot = cnt
    for s in range(1, TOP_K):
        rolled = _lane_roll1(cnt, s * N_EXPERTS)
        base = base + jnp.where(lane >= s * N_EXPERTS, rolled, 0.0)
        tot = tot + rolled
    pad_cnt = jnp.floor((tot + (SUBLANES - 1.0)) * (1.0 / SUBLANES)) * SUBLANES
    inc = pad_cnt
    d = 1
    while d < N_EXPERTS:
        inc = inc + jnp.where(e_of >= d, _lane_roll1(inc, d), 0.0)
        d *= 2
    strip_start = inc - pad_cnt
    ranked = onehot * (prefix + base + strip_start)

    route = jnp.zeros((rows, LANES), F32)
    for k in range(TOP_K):
        rank_k = jnp.sum(jnp.where(grp == k, ranked, 0.0), axis=1, keepdims=True)
        route = jnp.where(lane == ROUTE_E + k, ids[k], route)
        route = jnp.where(lane == ROUTE_G + k, ex[k] / denom, route)
        route = jnp.where(lane == ROUTE_R + k, rank_k, route)
    return route, tot


def _softmax_with_sink(s, sink_col):
    m = jnp.maximum(jnp.max(s, axis=-1, keepdims=True), sink_col)
    p = jnp.exp(s - m)
    denom = jnp.sum(p, axis=-1, keepdims=True) + jnp.exp(sink_col - m)
    return p, denom


WEIGHT_NAMES = ('norm1_g', 'w_in', 'b_in', 'conv_w', 'conv_b', 'ga', 'gx', 'lru_ba', 'lru_bx', 'lam',
                'w_out', 'b_out', 'norm2_g', 'wr_hi', 'wr_lo', 'b_router')


def _split_bf16(x):
    hi = x.astype(BF16)
    return hi, (x - hi.astype(F32)).astype(BF16)


def _ada_kernel(n_prompt_rows, c_ref, w_ref, b_ref, op_ref, os_ref):
    c = c_ref[...]
    s_hi, s_lo = _split_bf16(c * jax.nn.sigmoid(c))
    w_hi, w_lo = _split_bf16(w_ref[...])
    mod = (jnp.dot(s_hi, w_hi, preferred_element_type=F32) + jnp.dot(s_lo, w_hi, preferred_element_type=F32)
           + jnp.dot(s_hi, w_lo, preferred_element_type=F32)) + b_ref[...]
    op_ref[...] = mod[:n_prompt_rows]
    os_ref[...] = mod[n_prompt_rows:]


def _ada(c_all, n_prompt_rows, w_ada, b_ada):
    rows = c_all.shape[0]
    n_s = rows - n_prompt_rows
    assert n_prompt_rows % SUBLANES == 0
    return pl.pallas_call(
        functools.partial(_ada_kernel, n_prompt_rows),
        grid=(6,),
        in_specs=[pl.BlockSpec((rows, D_MODEL), lambda i: (0, 0)),
                  pl.BlockSpec((D_MODEL, D_MODEL), lambda i: (0, i)),
                  pl.BlockSpec((1, D_MODEL), lambda i: (0, i))],
        out_specs=(pl.BlockSpec((n_prompt_rows, D_MODEL), lambda i: (0, i)),
                   pl.BlockSpec((None, n_s, D_MODEL), lambda i: (i, 0, 0))),
        out_shape=(jax.ShapeDtypeStruct((n_prompt_rows, 6 * D_MODEL), F32),
                   jax.ShapeDtypeStruct((6, n_s, D_MODEL), F32)),
        compiler_params=pltpu.CompilerParams(dimension_semantics=("arbitrary",), vmem_limit_bytes=VMEM_LIMIT),
        name="ada",
    )(c_all, w_ada, b_ada)


def _prompt_body(seq_start, x_ref, mod_ref, cos_ref, sin_ref, sinks_ref, w, x1_ref, h2_ref, lg_ref,
                 hlast_ref, ulast_ref, klast_ref, vlast_ref, conv_c, h_c, k_c, v_c):
    ts = SEQ_TILE

    if seq_start is not None:
        @pl.when(seq_start)
        def _():
            conv_c[...] = jnp.zeros_like(conv_c)
            h_c[...] = jnp.zeros_like(h_c)
            k_c[...] = jnp.zeros_like(k_c)
            v_c[...] = jnp.zeros_like(v_c)

    x = x_ref[...]
    mod = mod_ref[...]
    proj = _in_proj(x, (mod[0:1], mod[1:2]), w)
    u = proj[:, :LRU_WIDTH]
    gate = proj[:, LRU_WIDTH:2 * LRU_WIDTH]
    o2 = 2 * LRU_WIDTH

    rowid = lax.broadcasted_iota(jnp.int32, (ts, 1), 0)
    u_ext = jnp.concatenate([conv_c[...], u], axis=0)
    s1, s2, s3 = (pltpu.roll(u_ext, d, axis=0)[SUBLANES:] for d in (1, 2, 3))
    uc = _conv_taps(u, s1, s2, s3, w)
    conv_c[...] = u[ts - SUBLANES:]
    ulast_ref[...] = u[ts - SUBLANES:]

    first_pos = None if seq_start is None else jnp.logical_and(rowid == 0, seq_start)
    a, bt = _lru_coeffs(uc, w, first_pos)
    hs = _chain_groups(*_group_scan(a, bt), h_c[0:1, :])
    h_tail = hs[ts - SUBLANES:]
    h_c[...] = jnp.broadcast_to(h_tail[SUBLANES - 1:SUBLANES, :], h_c.shape)
    hlast_ref[...] = h_tail
    lru_out = hs * jax.nn.gelu(gate)

    cos = cos_ref[...]
    sin = sin_ref[...]
    lane = lax.broadcasted_iota(jnp.int32, (1, LANES), 1)
    first_half = (lane % HEAD_DIM) < (HEAD_DIM // 2)
    qcols = [_rope128(proj[:, o2 + c * LANES:o2 + (c + 1) * LANES], cos, sin, first_half) * (HEAD_DIM ** -0.5)
             for c in range(4)]
    k = _rope128(proj[:, o2 + Q_WIDTH:o2 + Q_WIDTH + KV_WIDTH], cos, sin, first_half)
    v = proj[:, o2 + Q_WIDTH + KV_WIDTH:]
    k_ext = jnp.concatenate([k_c[...], k], axis=0).astype(BF16)
    v_ext = jnp.concatenate([v_c[...], v], axis=0).astype(BF16)
    k_c[...] = k[ts - WINDOW:]
    v_c[...] = v[ts - WINDOW:]
    klast_ref[...] = k[ts - WINDOW:]
    vlast_ref[...] = v[ts - WINDOW:]

    qi = lax.broadcasted_iota(jnp.int32, (WINDOW, 2 * WINDOW), 0)
    kj = lax.broadcasted_iota(jnp.int32, (WINDOW, 2 * WINDOW), 1)
    band = jnp.logical_and(kj > qi, kj <= qi + WINDOW)
    lane_lo = lane < HEAD_DIM
    grow = lax.broadcasted_iota(jnp.int32, (GROUP * WINDOW, 1), 0) // WINDOW
    attn_cols = [[] for _ in range(4)]
    for blk in range(ts // WINDOW):
        if seq_start is None or blk > 0:
            mask = band
        else:
            mask = jnp.logical_and(band, jnp.logical_or(kj >= WINDOW, jnp.logical_not(seq_start)))
        mask4 = jnp.concatenate([mask] * GROUP, axis=0)
        kb = k_ext[blk * WINDOW:(blk + 2) * WINDOW]
        vb = v_ext[blk * WINDOW:(blk + 2) * WINDOW]
        outs = []
        for kv in range(N_KV_HEADS):
            sel = lane_lo if kv == 0 else jnp.logical_not(lane_lo)
            qs = jnp.concatenate(
                [jnp.where(sel, qc[blk * WINDOW:(blk + 1) * WINDOW], 0.0) for qc in qcols], axis=0).astype(BF16)
            s = lax.dot_general(qs, kb, (((1,), (1,)), ((), ())), preferred_element_type=F32)
            s = jnp.where(mask4, s, NEG_BIG)
            sink_col = jnp.zeros((GROUP * WINDOW, 1), F32)
            for g in range(GROUP):
                sink_col = jnp.where(grow == g, sinks_ref[kv * GROUP + g], sink_col)
            p, denom = _softmax_with_sink(s, sink_col)
            outs.append(jnp.dot(p.astype(BF16), vb, preferred_element_type=F32) / denom)
        for c in range(4):
            attn_cols[c].append(jnp.where(lane_lo, outs[0][c * WINDOW:(c + 1) * WINDOW],
                                          outs[1][c * WINDOW:(c + 1) * WINDOW]))
    attn = jnp.concatenate([jnp.concatenate(cols, axis=0) for cols in attn_cols], axis=1)

    mix = jnp.concatenate([lru_out, attn], axis=1)
    x1, h2, logits = _post_mix(x, mix, (mod[2:3], mod[3:4], mod[4:5]), w)
    x1_ref[...] = x1
    h2_ref[...] = h2
    route, tot = _route_tile(logits)
    lg_ref[0][...] = route
    lg_ref[1][...] = jnp.broadcast_to(tot, lg_ref[1].shape)


def _expand_rows(m, t):
    b, wd = m.shape
    return jnp.broadcast_to(m[:, None, :], (b, t, wd)).reshape(b * t, wd)


def _sample_body(x_ref, mod_ref, cos_ref, sin_ref, sinks_ref, h0_ref, cprev_ref, ck_ref, cv_ref, w,
                 x1_ref, h2_ref, lg_ref, g2_ref, hs_ref, u_ref, ko_ref, vo_ref):
    bt_, t = SAMPLE_BT, SUBLANES
    rows = bt_ * t

    x = x_ref[...]
    mods = [_expand_rows(mod_ref[i], t) for i in range(6)]
    proj = _in_proj(x, (mods[0], mods[1]), w)
    u = proj[:, :LRU_WIDTH]
    gate = proj[:, LRU_WIDTH:2 * LRU_WIDTH]
    o2 = 2 * LRU_WIDTH
    u_ref[...] = u

    rowid = lax.broadcasted_iota(jnp.int32, (rows, 1), 0) % t
    cprev = cprev_ref[...]
    taps = []
    for d in (1, 2, 3):
        taps.append(jnp.where(rowid >= d, pltpu.roll(u, d, axis=0),
                              pltpu.roll(cprev, (d - (CONV_W - 1)) % rows, axis=0)))
    uc = _conv_taps(u, taps[0], taps[1], taps[2], w)

    a, bt = _lru_coeffs(uc, w, None)
    bt = bt + a * h0_ref[...]
    _, hs = _group_scan(a, bt)
    hs_ref[...] = hs
    lru_out = hs * jax.nn.gelu(gate)

    cos = cos_ref[...]
    sin = sin_ref[...]
    lane = lax.broadcasted_iota(jnp.int32, (1, LANES), 1)
    first_half = (lane % HEAD_DIM) < (HEAD_DIM // 2)
    qcols = [_rope128(proj[:, o2 + c * LANES:o2 + (c + 1) * LANES], cos, sin, first_half) * (HEAD_DIM ** -0.5)
             for c in range(4)]
    k = _rope128(proj[:, o2 + Q_WIDTH:o2 + Q_WIDTH + KV_WIDTH], cos, sin, first_half)
    v = proj[:, o2 + Q_WIDTH + KV_WIDTH:]
    k3 = k.reshape(bt_, t, KV_WIDTH)
    v3 = v.reshape(bt_, t, KV_WIDTH)
    ck = ck_ref[...]
    cv = cv_ref[...]
    ko_ref[:, :WINDOW - t, :] = ck[:, t:, :]
    ko_ref[:, WINDOW - t:, :] = k3
    vo_ref[:, :WINDOW - t, :] = cv[:, t:, :]
    vo_ref[:, WINDOW - t:, :] = v3

    ckb, cvb, k3b, v3b = ck.astype(BF16), cv.astype(BF16), k3.astype(BF16), v3.astype(BF16)
    lane_lo = lane < HEAD_DIM
    gq = GROUP * t
    tq = lax.broadcasted_iota(jnp.int32, (1, gq, 1), 1) % t
    mask_c = lax.broadcasted_iota(jnp.int32, (1, gq, WINDOW), 2) > tq
    mask_n = lax.broadcasted_iota(jnp.int32, (1, gq, t), 2) <= tq
    grow = lax.broadcasted_iota(jnp.int32, (1, gq, 1), 1) // t
    bdims = (((2,), (2,)), ((0,), (0,)))
    pdims = (((2,), (1,)), ((0,), (0,)))
    outs = []
    for kv in range(N_KV_HEADS):
        sel = lane_lo if kv == 0 else jnp.logical_not(lane_lo)
        q3 = jnp.concatenate([jnp.where(sel, qc, 0.0).reshape(bt_, t, LANES) for qc in qcols], axis=1).astype(BF16)
        sc = lax.dot_general(q3, ckb, bdims, preferred_element_type=F32)
        sn = lax.dot_general(q3, k3b, bdims, preferred_element_type=F32)
        sc = jnp.where(mask_c, sc, NEG_BIG)
        sn = jnp.where(mask_n, sn, NEG_BIG)
        sink_col = jnp.zeros((1, gq, 1), F32)
        for g in range(GROUP):
            sink_col = jnp.where(grow == g, sinks_ref[kv * GROUP + g], sink_col)
        m = jnp.maximum(jnp.maximum(jnp.max(sc, axis=-1, keepdims=True), jnp.max(sn, axis=-1, keepdims=True)),
                        sink_col)
        pc = jnp.exp(sc - m)
        pn = jnp.exp(sn - m)
        denom = jnp.sum(pc, axis=-1, keepdims=True) + jnp.sum(pn, axis=-1, keepdims=True) + jnp.exp(sink_col - m)
        o = (lax.dot_general(pc.astype(BF16), cvb, pdims, preferred_element_type=F32)
             + lax.dot_general(pn.astype(BF16), v3b, pdims, preferred_element_type=F32)) / denom
        outs.append(o)
    attn = jnp.concatenate(
        [jnp.where(lane_lo, outs[0][:, c * t:(c + 1) * t, :], outs[1][:, c * t:(c + 1) * t, :]).reshape(rows, LANES)
         for c in range(4)], axis=1)

    mix = jnp.concatenate([lru_out, attn], axis=1)
    x1, h2, logits = _post_mix(x, mix, (mods[2], mods[3], mods[4]), w)
    x1_ref[...] = x1
    h2_ref[...] = h2
    route, tot = _route_tile(logits)
    lg_ref[0][...] = route
    lg_ref[1][...] = jnp.broadcast_to(tot, lg_ref[1].shape)
    g2_ref[...] = mods[5]


def _prompt_kernel(steps_per_seq, x_ref, mod_ref, cos_ref, sin_ref, sinks_ref, *rest):
    nw = len(WEIGHT_NAMES)
    w = dict(zip(WEIGHT_NAMES, rest[:nw]))
    (x1_ref, h2_ref, route_ref, cnt_ref, hlast_ref, ulast_ref, klast_ref, vlast_ref,
     conv_c, h_c, k_c, v_c) = rest[nw:]
    seq_start = pl.program_id(0) % steps_per_seq == 0
    for sub in range(MIX_TILES):
        rows = pl.ds(sub * SEQ_TILE, SEQ_TILE)
        _prompt_body(seq_start if sub == 0 else None, x_ref.at[rows], mod_ref, cos_ref.at[rows], sin_ref.at[rows],
                     sinks_ref, w, x1_ref.at[rows], h2_ref.at[rows], (route_ref.at[rows], cnt_ref.at[sub]),
                     hlast_ref, ulast_ref, klast_ref, vlast_ref, conv_c, h_c, k_c, v_c)


def _sample_kernel(x_ref, mod_ref, cos_ref, sin_ref, sinks_ref, h0_ref, cprev_ref, ck_ref, cv_ref, *rest):
    nw = len(WEIGHT_NAMES)
    w = dict(zip(WEIGHT_NAMES, rest[:nw]))
    x1_ref, h2_ref, route_ref, cnt_ref, g2_ref, hs_ref, u_ref, ko_ref, vo_ref = rest[nw:]
    _sample_body(x_ref, mod_ref, cos_ref, sin_ref, sinks_ref, h0_ref, cprev_ref, ck_ref, cv_ref, w,
                 x1_ref, h2_ref, (route_ref, cnt_ref), g2_ref, hs_ref, u_ref, ko_ref, vo_ref)


PACK_W = D_MODEL // 2
U32 = jnp.int32


def _pack_bf16_pairs(x):
    return pltpu.pack_elementwise([x[:, :PACK_W], x[:, PACK_W:]], packed_dtype=BF16)


def _unpack_bf16_pairs(w):
    return tuple(pltpu.unpack_elementwise(w, index=k, packed_dtype=BF16, unpacked_dtype=F32).astype(BF16)
                 for k in range(2))


STRIP_SIZES = tuple(SUBLANES << b for b in range(6))
STRIP_LARGE = 64
SORT_ROWS = SEQ_TILE * TOP_K + N_EXPERTS * SUBLANES
TILE_WAIT_SIZES = tuple(SUBLANES << b for b in range(8))


def _for_strips(cnt_ref, off_ref, tile, buf_slot, hbm, sem, to_hbm, act):
    def e_body(e, local):
        n = cnt_ref[tile * N_EXPERTS + e]
        glob = off_ref[tile * N_EXPERTS + e]

        def pieces(sizes, done):
            for p in sizes:
                piece = n & p
                lo = pl.ds(pl.multiple_of(local + done, SUBLANES), p)
                gl = pl.ds(pl.multiple_of(glob + done, SUBLANES), p)

                @pl.when(piece != 0)
                def _():
                    if to_hbm:
                        act(pltpu.make_async_copy(buf_slot.at[lo], hbm.at[gl], sem))
                    else:
                        act(pltpu.make_async_copy(hbm.at[gl], buf_slot.at[lo], sem))
                done = done + piece

        large = tuple(p for p in reversed(STRIP_SIZES) if p >= STRIP_LARGE)
        small = tuple(p for p in reversed(STRIP_SIZES) if p < STRIP_LARGE)
        n_large = n & (-STRIP_LARGE)

        @pl.when(n_large != 0)
        def _():
            pieces(large, 0)
        pieces(small, n_large)
        return local + n
    lax.fori_loop(0, N_EXPERTS, e_body, 0)


def _wait_tile_rows(total, buf_slot, hbm, sem, to_hbm):
    for p in TILE_WAIT_SIZES:
        @pl.when((total & p) != 0)
        def _():
            if to_hbm:
                pltpu.make_async_copy(buf_slot.at[pl.ds(0, p)], hbm.at[pl.ds(0, p)], sem).wait()
            else:
                pltpu.make_async_copy(hbm.at[pl.ds(0, p)], buf_slot.at[pl.ds(0, p)], sem).wait()


def _dispatch_kernel(n_prompt_tiles, cnt_ref, off_ref, tot_ref, meta_ref, h2p_ref, h2s_ref, routep_ref, routes_ref,
                     xs_hbm, sbuf, zblk, sem, zsem):
    i = pl.program_id(0)
    nb = pl.num_programs(0)
    slot = i % 2
    n_blocks = xs_hbm.shape[0] // MOE_TM

    is_prompt = i < n_prompt_tiles
    h2 = jnp.where(is_prompt, h2p_ref[...], h2s_ref[...])
    route_t = jnp.where(is_prompt, routep_ref[...], routes_ref[...]).T
    r_pos = lax.broadcasted_iota(jnp.int32, (SORT_ROWS, SEQ_TILE), 0).astype(F32)
    perm = jnp.zeros((SORT_ROWS, SEQ_TILE), F32)
    for k in range(TOP_K):
        perm = perm + jnp.where(r_pos == route_t[ROUTE_R + k:ROUTE_R + k + 1, :], 1.0, 0.0)
    sbuf[slot] = _pack_bf16_pairs(jnp.dot(perm.astype(BF16), h2.astype(BF16), preferred_element_type=F32))

    _for_strips(cnt_ref, off_ref, i, sbuf.at[slot], xs_hbm, sem.at[slot], True, lambda cp: cp.start())

    @pl.when(i > 0)
    def _():
        _wait_tile_rows(tot_ref[jnp.maximum(i - 1, 0)], sbuf.at[1 - slot], xs_hbm, sem.at[1 - slot], True)

    @pl.when(i == nb - 1)
    def _():
        _wait_tile_rows(tot_ref[i], sbuf.at[slot], xs_hbm, sem.at[slot], True)
        zblk[...] = jnp.zeros_like(zblk)

        def for_region_tails(act):
            def e_body(e, carry):
                start = meta_ref[e]
                n = meta_ref[N_EXPERTS + e] - start
                done = 0
                for p in reversed([q for q in STRIP_SIZES if q < MOE_TM]):
                    piece = n & p
                    rows = pl.ds(pl.multiple_of(start + done, SUBLANES), p)

                    @pl.when(piece != 0)
                    def _():
                        act(pltpu.make_async_copy(zblk.at[pl.ds(0, p)], xs_hbm.at[rows], zsem.at[0]))
                    done = done + piece
                return carry
            lax.fori_loop(0, N_EXPERTS, e_body, 0)

        def for_tail_blocks(act):
            def b_body(j, carry):
                act(pltpu.make_async_copy(zblk, xs_hbm.at[pl.ds(pl.multiple_of(j * MOE_TM, MOE_TM), MOE_TM)],
                                          zsem.at[0]))
                return carry
            lax.fori_loop(meta_ref[2 * N_EXPERTS], n_blocks, b_body, 0)

        for_region_tails(lambda cp: cp.start())
        for_tail_blocks(lambda cp: cp.start())
        for_region_tails(lambda cp: cp.wait())
        for_tail_blocks(lambda cp: cp.wait())


def _dispatch(cnt8, tile_off, tot8, meta, h2_p, h2_s, route_p, route_s, n_rows):
    npt = h2_p.shape[0] // SEQ_TILE
    p_tile = lambda i, *_: (jnp.minimum(i, npt - 1), 0)
    s_tile = lambda i, *_: (jnp.maximum(i - npt, 0), 0)
    nt = tot8.shape[0]
    tt = SEQ_TILE
    grid_spec = pltpu.PrefetchScalarGridSpec(
        num_scalar_prefetch=4,
        grid=(nt,),
        in_specs=[pl.BlockSpec((tt, D_MODEL), p_tile), pl.BlockSpec((tt, D_MODEL), s_tile),
                  pl.BlockSpec((tt, LANES), p_tile), pl.BlockSpec((tt, LANES), s_tile)],
        out_specs=pl.BlockSpec(memory_space=pl.ANY),
        scratch_shapes=[pltpu.VMEM((2, SORT_ROWS, PACK_W), U32), pltpu.VMEM((MOE_TM, PACK_W), U32),
                        pltpu.SemaphoreType.DMA((2,)), pltpu.SemaphoreType.DMA((1,))],
    )
    return pl.pallas_call(
        functools.partial(_dispatch_kernel, npt),
        grid_spec=grid_spec,
        out_shape=jax.ShapeDtypeStruct((n_rows, PACK_W), U32),
        compiler_params=pltpu.CompilerParams(dimension_semantics=("arbitrary",), vmem_limit_bytes=VMEM_LIMIT),
        name="dispatch",
    )(cnt8, tile_off, tot8, meta, h2_p, h2_s, route_p, route_s)


def _expert_mlp(words, w1b, b1_ref, w2b, b2_ref, act_ref):
    xb = jnp.concatenate(_unpack_bf16_pairs(words), axis=1)
    q = D_FF // 4
    for c in range(4):
        zg = jnp.dot(xb, w1b[:, c * q:(c + 1) * q], preferred_element_type=F32) + b1_ref[:, c * q:(c + 1) * q]
        zl = (jnp.dot(xb, w1b[:, D_FF + c * q:D_FF + (c + 1) * q], preferred_element_type=F32)
              + b1_ref[:, D_FF + c * q:D_FF + (c + 1) * q])
        glu = jnp.minimum(zg, SWIGLU_LIMIT)
        lin = jnp.clip(zl, -SWIGLU_LIMIT, SWIGLU_LIMIT)
        act_ref[:, c * q:(c + 1) * q] = (glu * jax.nn.sigmoid(SWIGLU_ALPHA * glu) * (lin + 1.0)).astype(BF16)
    return _pack_bf16_pairs(jnp.dot(act_ref[...], w2b[...], preferred_element_type=F32) + b2_ref[...])


def _moe_kernel(row_ref, size_ref, exp_ref, wt_ref, meta_ref, xs_hbm, w1_hbm, b1_ref, w2_hbm, b2_ref, ys_hbm,
                xin, yout, w1f, w2f, w1b, w2b, act, isem, osem, wsem):
    i = pl.program_id(0)
    n = pl.num_programs(0)
    slot = i % 2
    tm = MOE_TM
    n_blocks = ys_hbm.shape[0] // tm

    def for_chunk(j, fn):
        r = pl.multiple_of(row_ref[j], tm)
        for code in MOE_CHUNK_CODES:
            @pl.when(size_ref[j] == code)
            def _():
                fn(r, code * tm)

    def in_copy(r, rows, s):
        return pltpu.make_async_copy(xs_hbm.at[pl.ds(r, rows)], xin.at[s, pl.ds(0, rows)], isem.at[s])

    def out_copy(r, rows, s):
        return pltpu.make_async_copy(yout.at[s, pl.ds(0, rows)], ys_hbm.at[pl.ds(r, rows)], osem.at[s])

    def start_in(j, s):
        for_chunk(j, lambda r, rows: in_copy(r, rows, s).start())

    def wait_in(j, s):
        for_chunk(j, lambda r, rows: in_copy(r, rows, s).wait())

    def start_out(j, s):
        for_chunk(j, lambda r, rows: out_copy(r, rows, s).start())

    def wait_out(j, s):
        for_chunk(j, lambda r, rows: out_copy(r, rows, s).wait())

    @pl.when(i == 0)
    def _():
        start_in(0, 0)

    @pl.when(i + 1 < n)
    def _():
        start_in(i + 1, 1 - slot)

    @pl.when(i >= 2)
    def _():
        wait_out(i - 2, slot)

    def weight_copies(e, ws):
        return (pltpu.make_async_copy(w1_hbm.at[e], w1f.at[ws], wsem.at[ws]),
                pltpu.make_async_copy(w2_hbm.at[e], w2f.at[ws], wsem.at[ws]))

    n_steps = n
    @pl.when(wt_ref[i] == 1)
    def _():
        ws = wt_ref[n_steps + i]
        nxt = wt_ref[2 * n_steps + i]

        @pl.when(i == 0)
        def _():
            for cp in weight_copies(exp_ref[0], 0):
                cp.start()

        for cp in weight_copies(exp_ref[i], ws):
            cp.wait()

        @pl.when(nxt >= 0)
        def _():
            for cp in weight_copies(nxt, 1 - ws):
                cp.start()

        chunk = 128
        def cast_body(c, carry):
            k0 = pl.multiple_of(c * chunk, chunk)
            w1b[pl.ds(k0, chunk), :] = w1f[ws, pl.ds(k0, chunk), :].astype(BF16)
            w2b[pl.ds(k0, chunk), :] = w2f[ws, pl.ds(k0, chunk), :].astype(BF16)
            return carry
        lax.fori_loop(0, D_MODEL // chunk, cast_body, 0)

    wait_in(i, slot)

    for code in MOE_CHUNK_CODES:
        rows = code * tm

        @pl.when(size_ref[i] == code)
        def _():
            yout[slot, 0:rows] = _expert_mlp(xin[slot, 0:rows], w1b, b1_ref.at[exp_ref[i]], w2b,
                                             b2_ref.at[exp_ref[i]], act.at[pl.ds(0, rows)])

    start_out(i, slot)

    @pl.when(i == n - 1)
    def _():
        @pl.when(i >= 1)
        def _():
            wait_out(i - 1, 1 - slot)
        wait_out(i, slot)
        yout[1, 0:tm] = jnp.zeros((tm, PACK_W), U32)

        def zero_block(j):
            return pltpu.make_async_copy(yout.at[1, pl.ds(0, tm)], ys_hbm.at[pl.ds(pl.multiple_of(j * tm, tm), tm)],
                                         osem.at[1])

        def start_body(j, carry):
            zero_block(j).start()
            return carry

        def wait_body(j, carry):
            zero_block(j).wait()
            return carry
        lax.fori_loop(meta_ref[2 * N_EXPERTS], n_blocks, start_body, 0)
        lax.fori_loop(meta_ref[2 * N_EXPERTS], n_blocks, wait_body, 0)


def _moe(chunk_row, chunk_size, chunk_exp, weight_tbl, meta, xs, w1, b1, w2, b2):
    whole = lambda i, *_: (0, 0, 0)
    grid_spec = pltpu.PrefetchScalarGridSpec(
        num_scalar_prefetch=5,
        grid=(chunk_row.shape[0],),
        in_specs=[
            pl.BlockSpec(memory_space=pl.ANY),
            pl.BlockSpec(memory_space=pl.ANY),
            pl.BlockSpec((N_EXPERTS, 1, 2 * D_FF), whole),
            pl.BlockSpec(memory_space=pl.ANY),
            pl.BlockSpec((N_EXPERTS, 1, D_MODEL), whole),
        ],
        out_specs=pl.BlockSpec(memory_space=pl.ANY),
        scratch_shapes=[pltpu.VMEM((2, MOE_CH, PACK_W), U32), pltpu.VMEM((2, MOE_CH, PACK_W), U32),
                        pltpu.VMEM((2, D_MODEL, 2 * D_FF), F32), pltpu.VMEM((2, D_FF, D_MODEL), F32),
                        pltpu.VMEM((D_MODEL, 2 * D_FF), BF16), pltpu.VMEM((D_FF, D_MODEL), BF16),
                        pltpu.VMEM((MOE_CH, D_FF), BF16),
                        pltpu.SemaphoreType.DMA((2,)), pltpu.SemaphoreType.DMA((2,)), pltpu.SemaphoreType.DMA((2,))],
    )
    return pl.pallas_call(
        _moe_kernel,
        grid_spec=grid_spec,
        out_shape=jax.ShapeDtypeStruct(xs.shape, U32),
        compiler_params=pltpu.CompilerParams(dimension_semantics=("arbitrary",), vmem_limit_bytes=VMEM_LIMIT),
        name="moe",
    )(chunk_row, chunk_size, chunk_exp, weight_tbl, meta, xs, w1, b1, w2, b2)


def _combine_kernel(n_prompt_tiles, cnt_ref, off_ref, tot_ref, ys_hbm, routep_ref, routes_ref, x1p_ref, x1s_ref,
                    modp_ref, g2s_ref, fg_ref, op_ref, os_ref, buf, sem):
    i = pl.program_id(0)
    nb = pl.num_programs(0)
    slot = i % 2

    def fetch(tile, s):
        _for_strips(cnt_ref, off_ref, tile, buf.at[s], ys_hbm, sem.at[s], False, lambda cp: cp.start())

    @pl.when(i == 0)
    def _():
        buf[...] = jnp.zeros_like(buf)
        fetch(0, 0)

    @pl.when(i + 1 < nb)
    def _():
        fetch(i + 1, 1 - slot)

    _wait_tile_rows(tot_ref[i], buf.at[slot], ys_hbm, sem.at[slot], False)

    is_prompt = i < n_prompt_tiles
    route = jnp.where(is_prompt, routep_ref[...], routes_ref[...])
    c_pos = lax.broadcasted_iota(jnp.int32, (SEQ_TILE, SORT_ROWS), 1).astype(F32)
    gmat = jnp.zeros((SEQ_TILE, SORT_ROWS), F32)
    for k in range(TOP_K):
        gmat = gmat + jnp.where(c_pos == route[:, ROUTE_R + k:ROUTE_R + k + 1],
                                route[:, ROUTE_G + k:ROUTE_G + k + 1], 0.0)
    g_bf = gmat.astype(BF16)
    ff = jnp.concatenate([jnp.dot(g_bf, yb, preferred_element_type=F32) for yb in _unpack_bf16_pairs(buf[slot])],
                         axis=1)
    g2 = jnp.where(i < n_prompt_tiles, modp_ref[5:6, :], g2s_ref[...])
    x = jnp.where(is_prompt, x1p_ref[...], x1s_ref[...]) + g2 * ff
    y = _rms(x, fg_ref[...])

    @pl.when(i < n_prompt_tiles)
    def _():
        op_ref[...] = y

    @pl.when(i >= n_prompt_tiles)
    def _():
        os_ref[...] = y


def _combine(cnt8, tile_off, tot8, ys, route_p, route_s, x1_p, x1_s, mod_p, g2_rows, final_g, tiles_per_batch):
    nt = tot8.shape[0]
    tt = SEQ_TILE
    npt = x1_p.shape[0] // tt
    p_tile = lambda i, *_: (jnp.minimum(i, npt - 1), 0)
    s_tile = lambda i, *_: (jnp.maximum(i - npt, 0), 0)
    grid_spec = pltpu.PrefetchScalarGridSpec(
        num_scalar_prefetch=3,
        grid=(nt,),
        in_specs=[
            pl.BlockSpec(memory_space=pl.ANY),
            pl.BlockSpec((tt, LANES), p_tile), pl.BlockSpec((tt, LANES), s_tile),
            pl.BlockSpec((tt, D_MODEL), p_tile), pl.BlockSpec((tt, D_MODEL), s_tile),
            pl.BlockSpec((None, 6, D_MODEL), lambda i, *_: (jnp.minimum(i, npt - 1) // tiles_per_batch, 0, 0)),
            pl.BlockSpec((tt, D_MODEL), s_tile),
            pl.BlockSpec((1, D_MODEL), lambda i, *_: (0, 0)),
        ],
        out_specs=(pl.BlockSpec((tt, D_MODEL), p_tile), pl.BlockSpec((tt, D_MODEL), s_tile)),
        scratch_shapes=[pltpu.VMEM((2, SORT_ROWS, PACK_W), U32), pltpu.SemaphoreType.DMA((2,))],
    )
    return pl.pallas_call(
        functools.partial(_combine_kernel, npt),
        grid_spec=grid_spec,
        out_shape=(jax.ShapeDtypeStruct((npt * tt, D_MODEL), F32),
                   jax.ShapeDtypeStruct(((nt - npt) * tt, D_MODEL), F32)),
        compiler_params=pltpu.CompilerParams(dimension_semantics=("arbitrary",), vmem_limit_bytes=VMEM_LIMIT),
        name="combine",
    )(cnt8, tile_off, tot8, ys, route_p, route_s, x1_p, x1_s, mod_p, g2_rows, final_g)


def _block_diag_halves(wg):
    per_half = LRU_BLOCKS // 2
    w4 = wg.reshape(2, per_half, LRU_BLOCK_W, LRU_BLOCK_W)
    on_diag = jnp.eye(per_half, dtype=bool)[None, :, None, :, None]
    dense = jnp.where(on_diag, w4[:, :, :, None, :], 0.0)
    return dense.reshape(2, per_half * LRU_BLOCK_W, per_half * LRU_BLOCK_W).astype(BF16)


def _rope_tables(pos):
    half = HEAD_DIM // 2
    inv = ROPE_THETA ** (-jnp.arange(half, dtype=F32) / half)
    ang = pos.astype(F32)[:, None] * inv[None, :]
    cos = jnp.cos(ang)
    sin = jnp.sin(ang)
    cos128 = jnp.concatenate([cos, cos, cos, cos], axis=1)
    sin128 = jnp.concatenate([-sin, sin, -sin, sin], axis=1)
    return cos128, sin128


def _resident_spec(arr):
    zeros = (0,) * arr.ndim
    return pl.BlockSpec(arr.shape, lambda i: zeros)


def kernel(x_prompt, x_sample, state_lru_h, state_conv, cache_win_k, cache_win_v, c_prompt, c_sample, w_ada, b_ada, norm1_g, w_in, b_in, conv_w, conv_b, lru_wa, lru_ba, lru_wx, lru_bx, lru_lambda, attn_sinks, w_out, b_out, norm2_g, w_router, b_router, w1, b1, w2, b2, final_g):
    bp, seq, _ = x_prompt.shape
    bd, tdec, _ = x_sample.shape
    assert tdec == SUBLANES and seq % (MIX_TILES * SEQ_TILE) == 0 and bd % SAMPLE_BT == 0
    assert SAMPLE_BT * tdec == SEQ_TILE
    n_prompt = bp * seq
    n_sample = bd * tdec
    n_tok = n_prompt + n_sample
    l = 0

    head_perm = [h for c in range(4) for h in (c, GROUP + c)]
    o2 = 2 * LRU_WIDTH

    def permute_heads(arr, axis, start):
        take = lambda a, b: lax.slice_in_dim(arr, a, b, axis=axis)
        heads = [take(start + h * HEAD_DIM, start + (h + 1) * HEAD_DIM) for h in head_perm]
        return jnp.concatenate([take(0, start)] + heads + [take(start + Q_WIDTH, arr.shape[axis])], axis=axis)

    sinks_perm = attn_sinks[l]

    wr = jnp.tile(w_router[l], (1, TOP_K))
    wr_hi = wr.astype(BF16)
    weights = dict(
        norm1_g=norm1_g[l][None, :], w_in=permute_heads(w_in[l], 1, o2).astype(BF16),
        b_in=permute_heads(b_in[l], 0, o2)[None, :],
        conv_w=conv_w[l], conv_b=conv_b[l][None, :],
        ga=_block_diag_halves(lru_wa[l]), gx=_block_diag_halves(lru_wx[l]),
        lru_ba=lru_ba[l][None, :], lru_bx=lru_bx[l][None, :], lam=lru_lambda[l][None, :],
        w_out=permute_heads(w_out[l], 0, LRU_WIDTH).astype(BF16), b_out=b_out[l][None, :], norm2_g=norm2_g[l][None, :],
        wr_hi=wr_hi, wr_lo=(wr - wr_hi.astype(F32)).astype(BF16),
        b_router=jnp.tile(b_router[l], TOP_K)[None, :],
    )
    wlist = [weights[n] for n in WEIGHT_NAMES]

    mod_p, mod_s = _ada(jnp.concatenate([c_prompt, c_sample], axis=0), bp, w_ada[l], b_ada[l][None, :])
    mod_p = mod_p.reshape(bp, 6, D_MODEL)

    cos_p, sin_p = _rope_tables(jnp.arange(seq, dtype=jnp.int32))
    cos_s, sin_s = _rope_tables(PAST_LEN + jnp.arange(tdec, dtype=jnp.int32))
    cos_s = jnp.tile(cos_s, (SAMPLE_BT, 1))
    sin_s = jnp.tile(sin_s, (SAMPLE_BT, 1))
    h0_rows = jnp.pad(state_lru_h[l][:, None, :], ((0, 0), (0, tdec - 1), (0, 0))).reshape(n_sample, LRU_WIDTH)
    cprev_rows = jnp.pad(state_conv[l], ((0, 0), (0, tdec - (CONV_W - 1)), (0, 0))).reshape(n_sample, LRU_WIDTH)
    ck = cache_win_k[l].reshape(bd, WINDOW, KV_WIDTH)
    cv = cache_win_v[l].reshape(bd, WINDOW, KV_WIDTH)
    npt = n_prompt // SEQ_TILE
    nst = n_sample // SEQ_TILE
    mix_rows = MIX_TILES * SEQ_TILE
    steps_per_seq = seq // mix_rows
    wspecs = [_resident_spec(a) for a in wlist]

    rows_p = lambda width: pl.BlockSpec((mix_rows, width), lambda i: (i, 0))
    tail_p = lambda rows, width: pl.BlockSpec((None, rows, width), lambda i: (i // steps_per_seq, 0, 0))
    (x1_p, h2_p, route_p, cnt_p, hlast_p, ulast_p, klast_p, vlast_p) = pl.pallas_call(
        functools.partial(_prompt_kernel, steps_per_seq),
        grid=(n_prompt // mix_rows,),
        in_specs=[rows_p(D_MODEL),
                  pl.BlockSpec((None, 6, D_MODEL), lambda i: (i // steps_per_seq, 0, 0)),
                  pl.BlockSpec((mix_rows, LANES), lambda i: (i % steps_per_seq, 0)),
                  pl.BlockSpec((mix_rows, LANES), lambda i: (i % steps_per_seq, 0)),
                  pl.BlockSpec(memory_space=pltpu.SMEM)] + wspecs,
        out_specs=(rows_p(D_MODEL), rows_p(D_MODEL), rows_p(LANES),
                   pl.BlockSpec((MIX_TILES, SUBLANES, LANES), lambda i: (i, 0, 0)),
                   tail_p(SUBLANES, LRU_WIDTH), tail_p(SUBLANES, LRU_WIDTH),
                   tail_p(WINDOW, KV_WIDTH), tail_p(WINDOW, KV_WIDTH)),
        out_shape=(
            jax.ShapeDtypeStruct((n_prompt, D_MODEL), F32),
            jax.ShapeDtypeStruct((n_prompt, D_MODEL), F32),
            jax.ShapeDtypeStruct((n_prompt, LANES), F32),
            jax.ShapeDtypeStruct((npt, SUBLANES, LANES), F32),
            jax.ShapeDtypeStruct((bp, SUBLANES, LRU_WIDTH), F32),
            jax.ShapeDtypeStruct((bp, SUBLANES, LRU_WIDTH), F32),
            jax.ShapeDtypeStruct((bp, WINDOW, KV_WIDTH), F32),
            jax.ShapeDtypeStruct((bp, WINDOW, KV_WIDTH), F32),
        ),
        scratch_shapes=[pltpu.VMEM((SUBLANES, LRU_WIDTH), F32), pltpu.VMEM((SUBLANES, LRU_WIDTH), F32),
                        pltpu.VMEM((WINDOW, KV_WIDTH), F32), pltpu.VMEM((WINDOW, KV_WIDTH), F32)],
        compiler_params=pltpu.CompilerParams(dimension_semantics=("arbitrary",), vmem_limit_bytes=VMEM_LIMIT),
        name="prompt_mixer",
    )(x_prompt.reshape(n_prompt, D_MODEL), mod_p, cos_p, sin_p, sinks_perm, *wlist)

    rows_s = lambda width: pl.BlockSpec((SEQ_TILE, width), lambda i: (i, 0))
    cache_spec = pl.BlockSpec((SAMPLE_BT, WINDOW, KV_WIDTH), lambda i: (i, 0, 0))
    (x1_s, h2_s, route_s, cnt_s, g2_rows, hs_s, u_s, s_k, s_v) = pl.pallas_call(
        _sample_kernel,
        grid=(nst,),
        in_specs=[rows_s(D_MODEL),
                  pl.BlockSpec((6, SAMPLE_BT, D_MODEL), lambda i: (0, i, 0)),
                  pl.BlockSpec((SEQ_TILE, LANES), lambda i: (0, 0)),
                  pl.BlockSpec((SEQ_TILE, LANES), lambda i: (0, 0)),
                  pl.BlockSpec(memory_space=pltpu.SMEM),
                  rows_s(LRU_WIDTH), rows_s(LRU_WIDTH), cache_spec, cache_spec] + wspecs,
        out_specs=(rows_s(D_MODEL), rows_s(D_MODEL), rows_s(LANES),
                   pl.BlockSpec((None, SUBLANES, LANES), lambda i: (i, 0, 0)),
                   rows_s(D_MODEL), rows_s(LRU_WIDTH), rows_s(LRU_WIDTH), cache_spec, cache_spec),
        out_shape=(
            jax.ShapeDtypeStruct((n_sample, D_MODEL), F32), jax.ShapeDtypeStruct((n_sample, D_MODEL), F32),
            jax.ShapeDtypeStruct((n_sample, LANES), F32), jax.ShapeDtypeStruct((nst, SUBLANES, LANES), F32),
            jax.ShapeDtypeStruct((n_sample, D_MODEL), F32),
            jax.ShapeDtypeStruct((n_sample, LRU_WIDTH), F32),
            jax.ShapeDtypeStruct((n_sample, LRU_WIDTH), F32),
            jax.ShapeDtypeStruct((bd, WINDOW, KV_WIDTH), F32),
            jax.ShapeDtypeStruct((bd, WINDOW, KV_WIDTH), F32),
        ),
        compiler_params=pltpu.CompilerParams(dimension_semantics=("arbitrary",), vmem_limit_bytes=VMEM_LIMIT),
        name="sample_mixer",
    )(x_sample.reshape(n_sample, D_MODEL), mod_s, cos_s, sin_s, sinks_perm, h0_rows, cprev_rows, ck, cv, *wlist)
    tile_cnt = jnp.concatenate([cnt_p, cnt_s], axis=0)

    n_tiles = npt + nst
    n_assign = n_tok * TOP_K
    max_rows = n_assign + n_tiles * N_EXPERTS * (SUBLANES - 1) + N_EXPERTS * (MOE_TM - 1)
    n_blocks = -(-max_rows // MOE_TM)
    cnt = tile_cnt[:, 0, :N_EXPERTS].astype(jnp.int32)
    cnt8 = (cnt + SUBLANES - 1) // SUBLANES * SUBLANES
    counts = jnp.sum(cnt8, axis=0)
    pcounts = (counts + MOE_TM - 1) // MOE_TM * MOE_TM
    pend = jnp.cumsum(pcounts)
    pstart = pend - pcounts
    tile_off = pstart[None, :] + jnp.cumsum(cnt8, axis=0) - cnt8
    tot8 = jnp.sum(cnt8, axis=1)
    meta = jnp.concatenate([pstart + counts, pend, pend[-1:] // MOE_TM]).astype(jnp.int32)
    cnt8_flat = cnt8.reshape(-1)
    off_flat = tile_off.reshape(-1).astype(jnp.int32)

    xs = _dispatch(cnt8_flat, off_flat, tot8, meta, h2_p, h2_s, route_p, route_s, n_blocks * MOE_TM)

    big, small = MOE_CHUNK_CODES[0], MOE_CHUNK_CODES[1:]
    n_tm = pcounts // MOE_TM
    n_big = n_tm // big
    present = [(n_tm // b) % 2 for b in small]
    n_ch = n_big + sum(present)
    ch_end = jnp.cumsum(n_ch)
    ch_start = ch_end - n_ch
    n_chunks = n_blocks // big + len(small) * N_EXPERTS
    ci = jnp.arange(n_chunks, dtype=jnp.int32)
    owner = jnp.logical_and(ch_start[None, :] <= ci[:, None], ci[:, None] < ch_end[None, :])
    pick = lambda v: jnp.sum(jnp.where(owner, v[None, :], 0), axis=1)
    live = ci < ch_end[-1]
    local = ci - pick(ch_start)
    chunk_exp = jnp.where(live, pick(jnp.arange(N_EXPERTS, dtype=jnp.int32)), N_EXPERTS - 1).astype(jnp.int32)
    big_c = pick(n_big)
    code = jnp.where(local < big_c, big, 0)
    blocks_before = jnp.where(local < big_c, local * big, 0)
    order = big_c
    blocks = big_c * big
    for b, has in zip(small, present):
        has_c = pick(has)
        is_b = jnp.logical_and(has_c == 1, local == order)
        code = jnp.where(is_b, b, code)
        blocks_before = jnp.where(is_b, blocks, blocks_before)
        order = order + has_c
        blocks = blocks + has_c * b
    chunk_row = jnp.where(live, pick(pstart) + blocks_before * MOE_TM, 0).astype(jnp.int32)
    chunk_size = jnp.where(live, code, 0).astype(jnp.int32)
    has_rows = n_ch > 0
    e_ar = jnp.arange(N_EXPERTS, dtype=jnp.int32)
    w_slot = (jnp.cumsum(has_rows.astype(jnp.int32)) - 1) % 2
    later = jnp.logical_and(has_rows[None, :], e_ar[None, :] > e_ar[:, None])
    nxt = jnp.min(jnp.where(later, e_ar[None, :], N_EXPERTS), axis=1)
    nxt = jnp.where(nxt == N_EXPERTS, -1, nxt)
    weight_tbl = jnp.concatenate([jnp.logical_and(live, local == 0).astype(jnp.int32),
                                  pick(w_slot), jnp.where(live, pick(nxt), -1)]).astype(jnp.int32)
    ys = _moe(chunk_row, chunk_size, chunk_exp, weight_tbl, meta, xs, w1[l], b1[l][:, None, :], w2[l], b2[l][:, None, :])

    y_p, y_s = _combine(cnt8_flat, off_flat, tot8, ys, route_p, route_s, x1_p, x1_s, mod_p, g2_rows,
                        final_g[None, :], seq // SEQ_TILE)

    y_prompt = y_p.reshape(bp, seq, D_MODEL)
    y_sample = y_s.reshape(bd, tdec, D_MODEL)
    p_h = hlast_p[:, SUBLANES - 1, :][None]
    p_c = ulast_p[:, SUBLANES - (CONV_W - 1):, :][None]
    p_k = klast_p.reshape(1, bp, WINDOW, N_KV_HEADS, HEAD_DIM)
    p_v = vlast_p.reshape(1, bp, WINDOW, N_KV_HEADS, HEAD_DIM)
    s_h = hs_s.reshape(bd, tdec, LRU_WIDTH)[:, tdec - 1, :][None]
    s_c = u_s.reshape(bd, tdec, LRU_WIDTH)[:, tdec - (CONV_W - 1):, :][None]
    s_kk = s_k.reshape(1, bd, WINDOW, N_KV_HEADS, HEAD_DIM)
    s_vv = s_v.reshape(1, bd, WINDOW, N_KV_HEADS, HEAD_DIM)
    return (y_prompt, y_sample, p_h, p_c, p_k, p_v, s_h, s_c, s_kk, s_vv)
```

```python
import functools

import jax
import jax.numpy as jnp
from jax import lax
from jax.experimental import pallas as pl
from jax.experimental.pallas import tpu as pltpu

F32 = jnp.float32
BF16 = jnp.bfloat16

D_MODEL = 1024
LRU_WIDTH = 512
LRU_BLOCKS = 8
LRU_BLOCK_W = LRU_WIDTH // LRU_BLOCKS
CONV_W = 4
LRU_C = 8.0
HEAD_DIM = 64
N_HEADS = 8
N_KV_HEADS = 2
GROUP = N_HEADS // N_KV_HEADS
WINDOW = 128
ROPE_THETA = 10000.0
N_EXPERTS = 32
TOP_K = 4
D_FF = D_MODEL
SWIGLU_LIMIT = 7.0
SWIGLU_ALPHA = 1.702
NORM_EPS = 1e-5
PAST_LEN = 8192
Q_WIDTH = N_HEADS * HEAD_DIM
KV_WIDTH = N_KV_HEADS * HEAD_DIM
IN_WIDTH = 2 * LRU_WIDTH + Q_WIDTH + 2 * KV_WIDTH

LANES = 128
SUBLANES = 8
SEQ_TILE = 256
MIX_TILES = 2
SAMPLE_BT = 32
MOE_TM = 256
MOE_CHUNK_CODES = (4, 2, 1)
MOE_CH = MOE_CHUNK_CODES[0] * MOE_TM
NEG_BIG = -1e30
VMEM_LIMIT = 56 * 1024 * 1024


def _rms(x, g):
    return x * lax.rsqrt(jnp.mean(x * x, axis=-1, keepdims=True) + NORM_EPS) * g


def _rms_mod(x, gain, shift):
    return x * lax.rsqrt(jnp.mean(x * x, axis=-1, keepdims=True) + NORM_EPS) * gain + shift


def _group_scan(a, b):
    rows, width = a.shape
    groups = rows // SUBLANES
    a3 = a.reshape(groups, SUBLANES, width)
    b3 = b.reshape(groups, SUBLANES, width)
    t = lax.broadcasted_iota(jnp.int32, (1, SUBLANES, 1), 1)
    d = 1
    while d < SUBLANES:
        keep = t >= d
        a_s = jnp.where(keep, pltpu.roll(a3, d, axis=1), 1.0)
        b_s = jnp.where(keep, pltpu.roll(b3, d, axis=1), 0.0)
        b3 = a3 * b_s + b3
        a3 = a3 * a_s
        d *= 2
    return a3.reshape(rows, width), b3.reshape(rows, width)


def _chain_groups(a_grp, b_grp, h_in):
    rows = a_grp.shape[0]
    out = []
    carry = h_in
    for g in range(rows // SUBLANES):
        sl = slice(g * SUBLANES, (g + 1) * SUBLANES)
        hg = b_grp[sl] + a_grp[sl] * carry
        out.append(hg)
        carry = hg[SUBLANES - 1:SUBLANES]
    return jnp.concatenate(out, axis=0)


def _rope128(x, cos, sin_signed, first_half):
    sw = jnp.where(first_half, pltpu.roll(x, LANES - HEAD_DIM // 2, axis=1), pltpu.roll(x, HEAD_DIM // 2, axis=1))
    return x * cos + sw * sin_signed


def _softplus(x):
    return jnp.maximum(x, 0.0) + jnp.log1p(jnp.exp(-jnp.abs(x)))


def _lru_coeffs(uc, w, first_pos_mask):
    ub = uc.astype(BF16)
    half = LRU_WIDTH // 2
    ra = jnp.concatenate([jnp.dot(ub[:, :half], w['ga'][0], preferred_element_type=F32),
                          jnp.dot(ub[:, half:], w['ga'][1], preferred_element_type=F32)], axis=1)
    rx = jnp.concatenate([jnp.dot(ub[:, :half], w['gx'][0], preferred_element_type=F32),
                          jnp.dot(ub[:, half:], w['gx'][1], preferred_element_type=F32)], axis=1)
    r = jax.nn.sigmoid(ra + w['lru_ba'][...])
    i = jax.nn.sigmoid(rx + w['lru_bx'][...])
    log_a = -LRU_C * r * _softplus(-w['lam'][...])
    a = jnp.exp(log_a)
    om = 1.0 - a * a
    mult = jnp.where(om > 0.0, om * lax.rsqrt(om), 0.0)
    if first_pos_mask is not None:
        mult = jnp.where(first_pos_mask, 1.0, mult)
    return a, mult * i * uc


def _conv_taps(u, s1, s2, s3, w):
    cw = w['conv_w']
    return w['conv_b'][...] + s3 * cw[0:1, :] + s2 * cw[1:2, :] + s1 * cw[2:3, :] + u * cw[3:4, :]


def _in_proj(x, mod, w):
    sh1, sc1 = mod
    h = _rms_mod(x, w['norm1_g'][...] * (1.0 + sc1), sh1)
    return jnp.dot(h.astype(BF16), w['w_in'][...], preferred_element_type=F32) + w['b_in'][...]


def _post_mix(x, mix, mod, w):
    g1, sh2, sc2 = mod
    x1 = x + g1 * (jnp.dot(mix.astype(BF16), w['w_out'][...], preferred_element_type=F32) + w['b_out'][...])
    h2 = _rms_mod(x1, w['norm2_g'][...] * (1.0 + sc2), sh2)
    h2_hi = h2.astype(BF16)
    h2_lo = (h2 - h2_hi.astype(F32)).astype(BF16)
    logits = (jnp.dot(h2_hi, w['wr_hi'][...], preferred_element_type=F32)
              + jnp.dot(h2_lo, w['wr_hi'][...], preferred_element_type=F32)
              + jnp.dot(h2_hi, w['wr_lo'][...], preferred_element_type=F32)) + w['b_router'][...]
    return x1, h2, logits


ROUTE_E, ROUTE_G, ROUTE_R = 0, TOP_K, 2 * TOP_K


def _lane_roll1(v, shift):
    return pltpu.roll(jnp.broadcast_to(v, (SUBLANES, LANES)), shift, axis=1)[0:1]


def _route_tile(lg):
    rows = lg.shape[0]
    lane = lax.broadcasted_iota(jnp.int32, (1, LANES), 1)
    e_of = lane % N_EXPERTS
    grp = lane // N_EXPERTS
    e_id = e_of.astype(F32)
    onehot = jnp.zeros((rows, LANES), F32)
    vals, ids = [], []
    for k in range(TOP_K):
        m = jnp.max(lg, axis=1, keepdims=True)
        idx = jnp.min(jnp.where(lg == m, e_id, float(N_EXPERTS)), axis=1, keepdims=True)
        sel = e_id == idx
        lg = jnp.where(sel, -jnp.inf, lg)
        onehot = jnp.where(jnp.logical_and(sel, grp == k), 1.0, onehot)
        vals.append(m)
        ids.append(idx)
    ex = [jnp.exp(v - vals[0]) for v in vals]
    denom = ex[0] + ex[1] + ex[2] + ex[3]

    r_i = lax.broadcasted_iota(jnp.int32, (rows, rows), 0)
    c_i = lax.broadcasted_iota(jnp.int32, (rows, rows), 1)
    strict_lower = jnp.where(r_i > c_i, 1.0, 0.0).astype(BF16)
    prefix = jnp.dot(strict_lower, onehot.astype(BF16), preferred_element_type=F32)
    cnt = jnp.sum(onehot, axis=0, keepdims=True)
    base = jnp.zeros((1, LANES), F32)
    tot = cnt
    for s in range(1, TOP_K):
        rolled = _lane_roll1(cnt, s * N_EXPERTS)
        base = base + jnp.where(lane >= s * N_EXPERTS, rolled, 0.0)
        tot = tot + rolled
    pad_cnt = jnp.floor((tot + (SUBLANES - 1.0)) * (1.0 / SUBLANES)) * SUBLANES
    inc = pad_cnt
    d = 1
    while d < N_EXPERTS:
        inc = inc + jnp.where(e_of >= d, _lane_roll1(inc, d), 0.0)
        d *= 2
    strip_start = inc - pad_cnt
    ranked = onehot * (prefix + base + strip_start)

    route = jnp.zeros((rows, LANES), F32)
    for k in range(TOP_K):
        rank_k = jnp.sum(jnp.where(grp == k, ranked, 0.0), axis=1, keepdims=True)
        route = jnp.where(lane == ROUTE_E + k, ids[k], route)
        route = jnp.where(lane == ROUTE_G + k, ex[k] / denom, route)
        route = jnp.where(lane == ROUTE_R + k, rank_k, route)
    return route, tot


def _softmax_with_sink(s, sink_col):
    m = jnp.maximum(jnp.max(s, axis=-1, keepdims=True), sink_col)
    p = jnp.exp(s - m)
    denom = jnp.sum(p, axis=-1, keepdims=True) + jnp.exp(sink_col - m)
    return p, denom


WEIGHT_NAMES = ('norm1_g', 'w_in', 'b_in', 'conv_w', 'conv_b', 'ga', 'gx', 'lru_ba', 'lru_bx', 'lam',
                'w_out', 'b_out', 'norm2_g', 'wr_hi', 'wr_lo', 'b_router')


def _split_bf16(x):
    hi = x.astype(BF16)
    return hi, (x - hi.astype(F32)).astype(BF16)


def _ada_kernel(n_prompt_rows, c_ref, w_ref, b_ref, op_ref, os_ref):
    c = c_ref[...]
    s_hi, s_lo = _split_bf16(c * jax.nn.sigmoid(c))
    w_hi, w_lo = _split_bf16(w_ref[...])
    mod = (jnp.dot(s_hi, w_hi, preferred_element_type=F32) + jnp.dot(s_lo, w_hi, preferred_element_type=F32)
           + jnp.dot(s_hi, w_lo, preferred_element_type=F32)) + b_ref[...]
    op_ref[...] = mod[:n_prompt_rows]
    os_ref[...] = mod[n_prompt_rows:]


def _ada(c_all, n_prompt_rows, w_ada, b_ada):
    rows = c_all.shape[0]
    n_s = rows - n_prompt_rows
    assert n_prompt_rows % SUBLANES == 0
    return pl.pallas_call(
        functools.partial(_ada_kernel, n_prompt_rows),
        grid=(6,),
        in_specs=[pl.BlockSpec((rows, D_MODEL), lambda i: (0, 0)),
                  pl.BlockSpec((D_MODEL, D_MODEL), lambda i: (0, i)),
                  pl.BlockSpec((1, D_MODEL), lambda i: (0, i))],
        out_specs=(pl.BlockSpec((n_prompt_rows, D_MODEL), lambda i: (0, i)),
                   pl.BlockSpec((None, n_s, D_MODEL), lambda i: (i, 0, 0))),
        out_shape=(jax.ShapeDtypeStruct((n_prompt_rows, 6 * D_MODEL), F32),
                   jax.ShapeDtypeStruct((6, n_s, D_MODEL), F32)),
        compiler_params=pltpu.CompilerParams(dimension_semantics=("arbitrary",), vmem_limit_bytes=VMEM_LIMIT),
        name="ada",
    )(c_all, w_ada, b_ada)


def _prompt_body(seq_start, x_ref, mod_ref, cos_ref, sin_ref, sinks_ref, w, x1_ref, h2_ref, lg_ref,
                 hlast_ref, ulast_ref, klast_ref, vlast_ref, conv_c, h_c, k_c, v_c):
    ts = SEQ_TILE

    if seq_start is not None:
        @pl.when(seq_start)
        def _():
            conv_c[...] = jnp.zeros_like(conv_c)
            h_c[...] = jnp.zeros_like(h_c)
            k_c[...] = jnp.zeros_like(k_c)
            v_c[...] = jnp.zeros_like(v_c)

    x = x_ref[...]
    mod = mod_ref[...]
    proj = _in_proj(x, (mod[0:1], mod[1:2]), w)
    u = proj[:, :LRU_WIDTH]
    gate = proj[:, LRU_WIDTH:2 * LRU_WIDTH]
    o2 = 2 * LRU_WIDTH

    rowid = lax.broadcasted_iota(jnp.int32, (ts, 1), 0)
    u_ext = jnp.concatenate([conv_c[...], u], axis=0)
    s1, s2, s3 = (pltpu.roll(u_ext, d, axis=0)[SUBLANES:] for d in (1, 2, 3))
    uc = _conv_taps(u, s1, s2, s3, w)
    conv_c[...] = u[ts - SUBLANES:]
    ulast_ref[...] = u[ts - SUBLANES:]

    first_pos = None if seq_start is None else jnp.logical_and(rowid == 0, seq_start)
    a, bt = _lru_coeffs(uc, w, first_pos)
    hs = _chain_groups(*_group_scan(a, bt), h_c[0:1, :])
    h_tail = hs[ts - SUBLANES:]
    h_c[...] = jnp.broadcast_to(h_tail[SUBLANES - 1:SUBLANES, :], h_c.shape)
    hlast_ref[...] = h_tail
    lru_out = hs * jax.nn.gelu(gate)

    cos = cos_ref[...]
    sin = sin_ref[...]
    lane = lax.broadcasted_iota(jnp.int32, (1, LANES), 1)
    first_half = (lane % HEAD_DIM) < (HEAD_DIM // 2)
    qcols = [_rope128(proj[:, o2 + c * LANES:o2 + (c + 1) * LANES], cos, sin, first_half) * (HEAD_DIM ** -0.5)
             for c in range(4)]
    k = _rope128(proj[:, o2 + Q_WIDTH:o2 + Q_WIDTH + KV_WIDTH], cos, sin, first_half)
    v = proj[:, o2 + Q_WIDTH + KV_WIDTH:]
    k_ext = jnp.concatenate([k_c[...], k], axis=0).astype(BF16)
    v_ext = jnp.concatenate([v_c[...], v], axis=0).astype(BF16)
    k_c[...] = k[ts - WINDOW:]
    v_c[...] = v[ts - WINDOW:]
    klast_ref[...] = k[ts - WINDOW:]
    vlast_ref[...] = v[ts - WINDOW:]

    qi = lax.broadcasted_iota(jnp.int32, (WINDOW, 2 * WINDOW), 0)
    kj = lax.broadcasted_iota(jnp.int32, (WINDOW, 2 * WINDOW), 1)
    band = jnp.logical_and(kj > qi, kj <= qi + WINDOW)
    lane_lo = lane < HEAD_DIM
    grow = lax.broadcasted_iota(jnp.int32, (GROUP * WINDOW, 1), 0) // WINDOW
    attn_cols = [[] for _ in range(4)]
    for blk in range(ts // WINDOW):
        if seq_start is None or blk > 0:
            mask = band
        else:
            mask = jnp.logical_and(band, jnp.logical_or(kj >= WINDOW, jnp.logical_not(seq_start)))
        mask4 = jnp.concatenate([mask] * GROUP, axis=0)
        kb = k_ext[blk * WINDOW:(blk + 2) * WINDOW]
        vb = v_ext[blk * WINDOW:(blk + 2) * WINDOW]
        outs = []
        for kv in range(N_KV_HEADS):
            sel = lane_lo if kv == 0 else jnp.logical_not(lane_lo)
            qs = jnp.concatenate(
                [jnp.where(sel, qc[blk * WINDOW:(blk + 1) * WINDOW], 0.0) for qc in qcols], axis=0).astype(BF16)
            s = lax.dot_general(qs, kb, (((1,), (1,)), ((), ())), preferred_element_type=F32)
            s = jnp.where(mask4, s, NEG_BIG)
            sink_col = jnp.zeros((GROUP * WINDOW, 1), F32)
            for g in range(GROUP):
                sink_col = jnp.where(grow == g, sinks_ref[kv * GROUP + g], sink_col)
            p, denom = _softmax_with_sink(s, sink_col)
            outs.append(jnp.dot(p.astype(BF16), vb, preferred_element_type=F32) / denom)
        for c in range(4):
            attn_cols[c].append(jnp.where(lane_lo, outs[0][c * WINDOW:(c + 1) * WINDOW],
                                          outs[1][c * WINDOW:(c + 1) * WINDOW]))
    attn = jnp.concatenate([jnp.concatenate(cols, axis=0) for cols in attn_cols], axis=1)

    mix = jnp.concatenate([lru_out, attn], axis=1)
    x1, h2, logits = _post_mix(x, mix, (mod[2:3], mod[3:4], mod[4:5]), w)
    x1_ref[...] = x1
    h2_ref[...] = h2
    route, tot = _route_tile(logits)
    lg_ref[0][...] = route
    lg_ref[1][...] = jnp.broadcast_to(tot, lg_ref[1].shape)


def _expand_rows(m, t):
    b, wd = m.shape
    return jnp.broadcast_to(m[:, None, :], (b, t, wd)).reshape(b * t, wd)


def _sample_body(x_ref, mod_ref, cos_ref, sin_ref, sinks_ref, h0_ref, cprev_ref, ck_ref, cv_ref, w,
                 x1_ref, h2_ref, lg_ref, g2_ref, hs_ref, u_ref, ko_ref, vo_ref):
    bt_, t = SAMPLE_BT, SUBLANES
    rows = bt_ * t

    x = x_ref[...]
    mods = [_expand_rows(mod_ref[i], t) for i in range(6)]
    proj = _in_proj(x, (mods[0], mods[1]), w)
    u = proj[:, :LRU_WIDTH]
    gate = proj[:, LRU_WIDTH:2 * LRU_WIDTH]
    o2 = 2 * LRU_WIDTH
    u_ref[...] = u

    rowid = lax.broadcasted_iota(jnp.int32, (rows, 1), 0) % t
    cprev = cprev_ref[...]
    taps = []
    for d in (1, 2, 3):
        taps.append(jnp.where(rowid >= d, pltpu.roll(u, d, axis=0),
                              pltpu.roll(cprev, (d - (CONV_W - 1)) % rows, axis=0)))
    uc = _conv_taps(u, taps[0], taps[1], taps[2], w)

    a, bt = _lru_coeffs(uc, w, None)
    bt = bt + a * h0_ref[...]
    _, hs = _group_scan(a, bt)
    hs_ref[...] = hs
    lru_out = hs * jax.nn.gelu(gate)

    cos = cos_ref[...]
    sin = sin_ref[...]
    lane = lax.broadcasted_iota(jnp.int32, (1, LANES), 1)
    first_half = (lane % HEAD_DIM) < (HEAD_DIM // 2)
    qcols = [_rope128(proj[:, o2 + c * LANES:o2 + (c + 1) * LANES], cos, sin, first_half) * (HEAD_DIM ** -0.5)
             for c in range(4)]
    k = _rope128(proj[:, o2 + Q_WIDTH:o2 + Q_WIDTH + KV_WIDTH], cos, sin, first_half)
    v = proj[:, o2 + Q_WIDTH + KV_WIDTH:]
    k3 = k.reshape(bt_, t, KV_WIDTH)
    v3 = v.reshape(bt_, t, KV_WIDTH)
    ck = ck_ref[...]
    cv = cv_ref[...]
    ko_ref[:, :WINDOW - t, :] = ck[:, t:, :]
    ko_ref[:, WINDOW - t:, :] = k3
    vo_ref[:, :WINDOW - t, :] = cv[:, t:, :]
    vo_ref[:, WINDOW - t:, :] = v3

    ckb, cvb, k3b, v3b = ck.astype(BF16), cv.astype(BF16), k3.astype(BF16), v3.astype(BF16)
    lane_lo = lane < HEAD_DIM
    gq = GROUP * t
    tq = lax.broadcasted_iota(jnp.int32, (1, gq, 1), 1) % t
    mask_c = lax.broadcasted_iota(jnp.int32, (1, gq, WINDOW), 2) > tq
    mask_n = lax.broadcasted_iota(jnp.int32, (1, gq, t), 2) <= tq
    grow = lax.broadcasted_iota(jnp.int32, (1, gq, 1), 1) // t
    bdims = (((2,), (2,)), ((0,), (0,)))
    pdims = (((2,), (1,)), ((0,), (0,)))
    outs = []
    for kv in range(N_KV_HEADS):
        sel = lane_lo if kv == 0 else jnp.logical_not(lane_lo)
        q3 = jnp.concatenate([jnp.where(sel, qc, 0.0).reshape(bt_, t, LANES) for qc in qcols], axis=1).astype(BF16)
        sc = lax.dot_general(q3, ckb, bdims, preferred_element_type=F32)
        sn = lax.dot_general(q3, k3b, bdims, preferred_element_type=F32)
        sc = jnp.where(mask_c, sc, NEG_BIG)
        sn = jnp.where(mask_n, sn, NEG_BIG)
        sink_col = jnp.zeros((1, gq, 1), F32)
        for g in range(GROUP):
            sink_col = jnp.where(grow == g, sinks_ref[kv * GROUP + g], sink_col)
        m = jnp.maximum(jnp.maximum(jnp.max(sc, axis=-1, keepdims=True), jnp.max(sn, axis=-1, keepdims=True)),
                        sink_col)
        pc = jnp.exp(sc - m)
        pn = jnp.exp(sn - m)
        denom = jnp.sum(pc, axis=-1, keepdims=True) + jnp.sum(pn, axis=-1, keepdims=True) + jnp.exp(sink_col - m)
        o = (lax.dot_general(pc.astype(BF16), cvb, pdims, preferred_element_type=F32)
             + lax.dot_general(pn.astype(BF16), v3b, pdims, preferred_element_type=F32)) / denom
        outs.append(o)
    attn = jnp.concatenate(
        [jnp.where(lane_lo, outs[0][:, c * t:(c + 1) * t, :], outs[1][:, c * t:(c + 1) * t, :]).reshape(rows, LANES)
         for c in range(4)], axis=1)

    mix = jnp.concatenate([lru_out, attn], axis=1)
    x1, h2, logits = _post_mix(x, mix, (mods[2], mods[3], mods[4]), w)
    x1_ref[...] = x1
    h2_ref[...] = h2
    route, tot = _route_tile(logits)
    lg_ref[0][...] = route
    lg_ref[1][...] = jnp.broadcast_to(tot, lg_ref[1].shape)
    g2_ref[...] = mods[5]


def _prompt_kernel(steps_per_seq, x_ref, mod_ref, cos_ref, sin_ref, sinks_ref, *rest):
    nw = len(WEIGHT_NAMES)
    w = dict(zip(WEIGHT_NAMES, rest[:nw]))
    (x1_ref, h2_ref, route_ref, cnt_ref, hlast_ref, ulast_ref, klast_ref, vlast_ref,
     conv_c, h_c, k_c, v_c) = rest[nw:]
    seq_start = pl.program_id(0) % steps_per_seq == 0
    for sub in range(MIX_TILES):
        rows = pl.ds(sub * SEQ_TILE, SEQ_TILE)
        _prompt_body(seq_start if sub == 0 else None, x_ref.at[rows], mod_ref, cos_ref.at[rows], sin_ref.at[rows],
                     sinks_ref, w, x1_ref.at[rows], h2_ref.at[rows], (route_ref.at[rows], cnt_ref.at[sub]),
                     hlast_ref, ulast_ref, klast_ref, vlast_ref, conv_c, h_c, k_c, v_c)


def _sample_kernel(x_ref, mod_ref, cos_ref, sin_ref, sinks_ref, h0_ref, cprev_ref, ck_ref, cv_ref, *rest):
    nw = len(WEIGHT_NAMES)
    w = dict(zip(WEIGHT_NAMES, rest[:nw]))
    x1_ref, h2_ref, route_ref, cnt_ref, g2_ref, hs_ref, u_ref, ko_ref, vo_ref = rest[nw:]
    _sample_body(x_ref, mod_ref, cos_ref, sin_ref, sinks_ref, h0_ref, cprev_ref, ck_ref, cv_ref, w,
                 x1_ref, h2_ref, (route_ref, cnt_ref), g2_ref, hs_ref, u_ref, ko_ref, vo_ref)


PACK_W = D_MODEL // 2
U32 = jnp.int32


def _pack_bf16_pairs(x):
    return pltpu.pack_elementwise([x[:, :PACK_W], x[:, PACK_W:]], packed_dtype=BF16)


def _unpack_bf16_pairs(w):
    return tuple(pltpu.unpack_elementwise(w, index=k, packed_dtype=BF16, unpacked_dtype=F32).astype(BF16)
                 for k in range(2))


STRIP_SIZES = tuple(SUBLANES << b for b in range(6))
STRIP_LARGE = 64
SORT_ROWS = SEQ_TILE * TOP_K + N_EXPERTS * SUBLANES
TILE_WAIT_SIZES = tuple(SUBLANES << b for b in range(8))


def _for_strips(cnt_ref, off_ref, tile, buf_slot, hbm, sem, to_hbm, act):
    def e_body(e, local):
        n = cnt_ref[tile * N_EXPERTS + e]
        glob = off_ref[tile * N_EXPERTS + e]

        def pieces(sizes, done):
            for p in sizes:
                piece = n & p
                lo = pl.ds(pl.multiple_of(local + done, SUBLANES), p)
                gl = pl.ds(pl.multiple_of(glob + done, SUBLANES), p)

                @pl.when(piece != 0)
                def _():
                    if to_hbm:
                        act(pltpu.make_async_copy(buf_slot.at[lo], hbm.at[gl], sem))
                    else:
                        act(pltpu.make_async_copy(hbm.at[gl], buf_slot.at[lo], sem))
                done = done + piece

        large = tuple(p for p in reversed(STRIP_SIZES) if p >= STRIP_LARGE)
        small = tuple(p for p in reversed(STRIP_SIZES) if p < STRIP_LARGE)
        n_large = n & (-STRIP_LARGE)

        @pl.when(n_large != 0)
        def _():
            pieces(large, 0)
        pieces(small, n_large)
        return local + n
    lax.fori_loop(0, N_EXPERTS, e_body, 0)


def _wait_tile_rows(total, buf_slot, hbm, sem, to_hbm):
    for p in TILE_WAIT_SIZES:
        @pl.when((total & p) != 0)
        def _():
            if to_hbm:
                pltpu.make_async_copy(buf_slot.at[pl.ds(0, p)], hbm.at[pl.ds(0, p)], sem).wait()
            else:
                pltpu.make_async_copy(hbm.at[pl.ds(0, p)], buf_slot.at[pl.ds(0, p)], sem).wait()


def _dispatch_kernel(n_prompt_tiles, cnt_ref, off_ref, tot_ref, meta_ref, h2p_ref, h2s_ref, routep_ref, routes_ref,
                     xs_hbm, sbuf, zblk, sem, zsem):
    i = pl.program_id(0)
    nb = pl.num_programs(0)
    slot = i % 2
    n_blocks = xs_hbm.shape[0] // MOE_TM

    is_prompt = i < n_prompt_tiles
    h2 = jnp.where(is_prompt, h2p_ref[...], h2s_ref[...])
    route_t = jnp.where(is_prompt, routep_ref[...], routes_ref[...]).T
    r_pos = lax.broadcasted_iota(jnp.int32, (SORT_ROWS, SEQ_TILE), 0).astype(F32)
    perm = jnp.zeros((SORT_ROWS, SEQ_TILE), F32)
    for k in range(TOP_K):
        perm = perm + jnp.where(r_pos == route_t[ROUTE_R + k:ROUTE_R + k + 1, :], 1.0, 0.0)
    sbuf[slot] = _pack_bf16_pairs(jnp.dot(perm.astype(BF16), h2.astype(BF16), preferred_element_type=F32))

    _for_strips(cnt_ref, off_ref, i, sbuf.at[slot], xs_hbm, sem.at[slot], True, lambda cp: cp.start())

    @pl.when(i > 0)
    def _():
        _wait_tile_rows(tot_ref[jnp.maximum(i - 1, 0)], sbuf.at[1 - slot], xs_hbm, sem.at[1 - slot], True)

    @pl.when(i == nb - 1)
    def _():
        _wait_tile_rows(tot_ref[i], sbuf.at[slot], xs_hbm, sem.at[slot], True)
        zblk[...] = jnp.zeros_like(zblk)

        def for_region_tails(act):
            def e_body(e, carry):
                start = meta_ref[e]
                n = meta_ref[N_EXPERTS + e] - start
                done = 0
                for p in reversed([q for q in STRIP_SIZES if q < MOE_TM]):
                    piece = n & p
                    rows = pl.ds(pl.multiple_of(start + done, SUBLANES), p)

                    @pl.when(piece != 0)
                    def _():
                        act(pltpu.make_async_copy(zblk.at[pl.ds(0, p)], xs_hbm.at[rows], zsem.at[0]))
                    done = done + piece
                return carry
            lax.fori_loop(0, N_EXPERTS, e_body, 0)

        def for_tail_blocks(act):
            def b_body(j, carry):
                act(pltpu.make_async_copy(zblk, xs_hbm.at[pl.ds(pl.multiple_of(j * MOE_TM, MOE_TM), MOE_TM)],
                                          zsem.at[0]))
                return carry
            lax.fori_loop(meta_ref[2 * N_EXPERTS], n_blocks, b_body, 0)

        for_region_tails(lambda cp: cp.start())
        for_tail_blocks(lambda cp: cp.start())
        for_region_tails(lambda cp: cp.wait())
        for_tail_blocks(lambda cp: cp.wait())


def _dispatch(cnt8, tile_off, tot8, meta, h2_p, h2_s, route_p, route_s, n_rows):
    npt = h2_p.shape[0] // SEQ_TILE
    p_tile = lambda i, *_: (jnp.minimum(i, npt - 1), 0)
    s_tile = lambda i, *_: (jnp.maximum(i - npt, 0), 0)
    nt = tot8.shape[0]
    tt = SEQ_TILE
    grid_spec = pltpu.PrefetchScalarGridSpec(
        num_scalar_prefetch=4,
        grid=(nt,),
        in_specs=[pl.BlockSpec((tt, D_MODEL), p_tile), pl.BlockSpec((tt, D_MODEL), s_tile),
                  pl.BlockSpec((tt, LANES), p_tile), pl.BlockSpec((tt, LANES), s_tile)],
        out_specs=pl.BlockSpec(memory_space=pl.ANY),
        scratch_shapes=[pltpu.VMEM((2, SORT_ROWS, PACK_W), U32), pltpu.VMEM((MOE_TM, PACK_W), U32),
                        pltpu.SemaphoreType.DMA((2,)), pltpu.SemaphoreType.DMA((1,))],
    )
    return pl.pallas_call(
        functools.partial(_dispatch_kernel, npt),
        grid_spec=grid_spec,
        out_shape=jax.ShapeDtypeStruct((n_rows, PACK_W), U32),
        compiler_params=pltpu.CompilerParams(dimension_semantics=("arbitrary",), vmem_limit_bytes=VMEM_LIMIT),
        name="dispatch",
    )(cnt8, tile_off, tot8, meta, h2_p, h2_s, route_p, route_s)


def _expert_mlp(words, w1b, b1_ref, w2b, b2_ref, act_ref):
    xb = jnp.concatenate(_unpack_bf16_pairs(words), axis=1)
    q = D_FF // 4
    for c in range(4):
        zg = jnp.dot(xb, w1b[:, c * q:(c + 1) * q], preferred_element_type=F32) + b1_ref[:, c * q:(c + 1) * q]
        zl = (jnp.dot(xb, w1b[:, D_FF + c * q:D_FF + (c + 1) * q], preferred_element_type=F32)
              + b1_ref[:, D_FF + c * q:D_FF + (c + 1) * q])
        glu = jnp.minimum(zg, SWIGLU_LIMIT)
        lin = jnp.clip(zl, -SWIGLU_LIMIT, SWIGLU_LIMIT)
        act_ref[:, c * q:(c + 1) * q] = (glu * jax.nn.sigmoid(SWIGLU_ALPHA * glu) * (lin + 1.0)).astype(BF16)
    return _pack_bf16_pairs(jnp.dot(act_ref[...], w2b[...], preferred_element_type=F32) + b2_ref[...])


def _moe_kernel(row_ref, size_ref, exp_ref, wt_ref, meta_ref, xs_hbm, w1_hbm, b1_ref, w2_hbm, b2_ref, ys_hbm,
                xin, yout, w1f, w2f, w1b, w2b, act, isem, osem, wsem):
    tm = MOE_TM
    n_blocks = ys_hbm.shape[0] // tm
    n_tbl = row_ref.shape[0]
    n = meta_ref[2 * N_EXPERTS + 1]

    def for_chunk(j, fn):
        r = pl.multiple_of(row_ref[j], tm)
        for code in MOE_CHUNK_CODES:
            @pl.when(size_ref[j] == code)
            def _():
                fn(r, code * tm)

    def in_copy(r, rows, s):
        return pltpu.make_async_copy(xs_hbm.at[pl.ds(r, rows)], xin.at[s, pl.ds(0, rows)], isem.at[s])

    def out_copy(r, rows, s):
        return pltpu.make_async_copy(yout.at[s, pl.ds(0, rows)], ys_hbm.at[pl.ds(r, rows)], osem.at[s])

    def start_in(j, s):
        for_chunk(j, lambda r, rows: in_copy(r, rows, s).start())

    def wait_in(j, s):
        for_chunk(j, lambda r, rows: in_copy(r, rows, s).wait())

    def start_out(j, s):
        for_chunk(j, lambda r, rows: out_copy(r, rows, s).start())

    def wait_out(j, s):
        for_chunk(j, lambda r, rows: out_copy(r, rows, s).wait())

    def weight_copies(e, ws):
        return (pltpu.make_async_copy(w1_hbm.at[e], w1f.at[ws], wsem.at[ws]),
                pltpu.make_async_copy(w2_hbm.at[e], w2f.at[ws], wsem.at[ws]))

    start_in(0, 0)
    for cp in weight_copies(exp_ref[0], 0):
        cp.start()

    def trip(i, carry):
        slot = i % 2

        @pl.when(i + 1 < n)
        def _():
            start_in(i + 1, 1 - slot)

        @pl.when(i >= 2)
        def _():
            wait_out(i - 2, slot)

        @pl.when(wt_ref[i] == 1)
        def _():
            ws = wt_ref[n_tbl + i]
            nxt = wt_ref[2 * n_tbl + i]
            for cp in weight_copies(exp_ref[i], ws):
                cp.wait()

            @pl.when(nxt >= 0)
            def _():
                for cp in weight_copies(nxt, 1 - ws):
                    cp.start()

            chunk = 128
            def cast_body(c, carry2):
                k0 = pl.multiple_of(c * chunk, chunk)
                w1b[pl.ds(k0, chunk), :] = w1f[ws, pl.ds(k0, chunk), :].astype(BF16)
                w2b[pl.ds(k0, chunk), :] = w2f[ws, pl.ds(k0, chunk), :].astype(BF16)
                return carry2
            lax.fori_loop(0, D_MODEL // chunk, cast_body, 0)

        wait_in(i, slot)

        for code in MOE_CHUNK_CODES:
            rows = code * tm

            @pl.when(size_ref[i] == code)
            def _():
                yout[slot, 0:rows] = _expert_mlp(xin[slot, 0:rows], w1b, b1_ref.at[exp_ref[i]], w2b,
                                                 b2_ref.at[exp_ref[i]], act.at[pl.ds(0, rows)])

        start_out(i, slot)
        return carry
    lax.fori_loop(0, n, trip, 0)

    @pl.when(n >= 2)
    def _():
        wait_out(n - 2, n % 2)
    wait_out(n - 1, (n - 1) % 2)

    yout[1, 0:tm] = jnp.zeros((tm, PACK_W), U32)

    def zero_block(j):
        return pltpu.make_async_copy(yout.at[1, pl.ds(0, tm)], ys_hbm.at[pl.ds(pl.multiple_of(j * tm, tm), tm)],
                                     osem.at[1])

    def start_body(j, carry):
        zero_block(j).start()
        return carry

    def wait_body(j, carry):
        zero_block(j).wait()
        return carry
    lax.fori_loop(meta_ref[2 * N_EXPERTS], n_blocks, start_body, 0)
    lax.fori_loop(meta_ref[2 * N_EXPERTS], n_blocks, wait_body, 0)


def _moe(chunk_row, chunk_size, chunk_exp, weight_tbl, meta, xs, w1, b1, w2, b2):
    whole = lambda i, *_: (0, 0, 0)
    grid_spec = pltpu.PrefetchScalarGridSpec(
        num_scalar_prefetch=5,
        grid=(1,),
        in_specs=[
            pl.BlockSpec(memory_space=pl.ANY),
            pl.BlockSpec(memory_space=pl.ANY),
            pl.BlockSpec((N_EXPERTS, 1, 2 * D_FF), whole),
            pl.BlockSpec(memory_space=pl.ANY),
            pl.BlockSpec((N_EXPERTS, 1, D_MODEL), whole),
        ],
        out_specs=pl.BlockSpec(memory_space=pl.ANY),
        scratch_shapes=[pltpu.VMEM((2, MOE_CH, PACK_W), U32), pltpu.VMEM((2, MOE_CH, PACK_W), U32),
                        pltpu.VMEM((2, D_MODEL, 2 * D_FF), F32), pltpu.VMEM((2, D_FF, D_MODEL), F32),
                        pltpu.VMEM((D_MODEL, 2 * D_FF), BF16), pltpu.VMEM((D_FF, D_MODEL), BF16),
                        pltpu.VMEM((MOE_CH, D_FF), BF16),
                        pltpu.SemaphoreType.DMA((2,)), pltpu.SemaphoreType.DMA((2,)), pltpu.SemaphoreType.DMA((2,))],
    )
    return pl.pallas_call(
        _moe_kernel,
        grid_spec=grid_spec,
        out_shape=jax.ShapeDtypeStruct(xs.shape, U32),
        compiler_params=pltpu.CompilerParams(dimension_semantics=("arbitrary",), vmem_limit_bytes=VMEM_LIMIT),
        name="moe",
    )(chunk_row, chunk_size, chunk_exp, weight_tbl, meta, xs, w1, b1, w2, b2)


def _combine_kernel(n_prompt_tiles, cnt_ref, off_ref, tot_ref, ys_hbm, routep_ref, routes_ref, x1p_ref, x1s_ref,
                    modp_ref, g2s_ref, fg_ref, op_ref, os_ref, buf, sem):
    i = pl.program_id(0)
    nb = pl.num_programs(0)
    slot = i % 2

    def fetch(tile, s):
        _for_strips(cnt_ref, off_ref, tile, buf.at[s], ys_hbm, sem.at[s], False, lambda cp: cp.start())

    @pl.when(i == 0)
    def _():
        buf[...] = jnp.zeros_like(buf)
        fetch(0, 0)

    @pl.when(i + 1 < nb)
    def _():
        fetch(i + 1, 1 - slot)

    _wait_tile_rows(tot_ref[i], buf.at[slot], ys_hbm, sem.at[slot], False)

    is_prompt = i < n_prompt_tiles
    route = jnp.where(is_prompt, routep_ref[...], routes_ref[...])
    c_pos = lax.broadcasted_iota(jnp.int32, (SEQ_TILE, SORT_ROWS), 1).astype(F32)
    gmat = jnp.zeros((SEQ_TILE, SORT_ROWS), F32)
    for k in range(TOP_K):
        gmat = gmat + jnp.where(c_pos == route[:, ROUTE_R + k:ROUTE_R + k + 1],
                                route[:, ROUTE_G + k:ROUTE_G + k + 1], 0.0)
    g_bf = gmat.astype(BF16)
    ff = jnp.concatenate([jnp.dot(g_bf, yb, preferred_element_type=F32) for yb in _unpack_bf16_pairs(buf[slot])],
                         axis=1)
    g2 = jnp.where(i < n_prompt_tiles, modp_ref[5:6, :], g2s_ref[...])
    x = jnp.where(is_prompt, x1p_ref[...], x1s_ref[...]) + g2 * ff
    y = _rms(x, fg_ref[...])

    @pl.when(i < n_prompt_tiles)
    def _():
        op_ref[...] = y

    @pl.when(i >= n_prompt_tiles)
    def _():
        os_ref[...] = y


def _combine(cnt8, tile_off, tot8, ys, route_p, route_s, x1_p, x1_s, mod_p, g2_rows, final_g, tiles_per_batch):
    nt = tot8.shape[0]
    tt = SEQ_TILE
    npt = x1_p.shape[0] // tt
    p_tile = lambda i, *_: (jnp.minimum(i, npt - 1), 0)
    s_tile = lambda i, *_: (jnp.maximum(i - npt, 0), 0)
    grid_spec = pltpu.PrefetchScalarGridSpec(
        num_scalar_prefetch=3,
        grid=(nt,),
        in_specs=[
            pl.BlockSpec(memory_space=pl.ANY),
            pl.BlockSpec((tt, LANES), p_tile), pl.BlockSpec((tt, LANES), s_tile),
            pl.BlockSpec((tt, D_MODEL), p_tile), pl.BlockSpec((tt, D_MODEL), s_tile),
            pl.BlockSpec((None, 6, D_MODEL), lambda i, *_: (jnp.minimum(i, npt - 1) // tiles_per_batch, 0, 0)),
            pl.BlockSpec((tt, D_MODEL), s_tile),
            pl.BlockSpec((1, D_MODEL), lambda i, *_: (0, 0)),
        ],
        out_specs=(pl.BlockSpec((tt, D_MODEL), p_tile), pl.BlockSpec((tt, D_MODEL), s_tile)),
        scratch_shapes=[pltpu.VMEM((2, SORT_ROWS, PACK_W), U32), pltpu.SemaphoreType.DMA((2,))],
    )
    return pl.pallas_call(
        functools.partial(_combine_kernel, npt),
        grid_spec=grid_spec,
        out_shape=(jax.ShapeDtypeStruct((npt * tt, D_MODEL), F32),
                   jax.ShapeDtypeStruct(((nt - npt) * tt, D_MODEL), F32)),
        compiler_params=pltpu.CompilerParams(dimension_semantics=("arbitrary",), vmem_limit_bytes=VMEM_LIMIT),
        name="combine",
    )(cnt8, tile_off, tot8, ys, route_p, route_s, x1_p, x1_s, mod_p, g2_rows, final_g)


def _block_diag_halves(wg):
    per_half = LRU_BLOCKS // 2
    w4 = wg.reshape(2, per_half, LRU_BLOCK_W, LRU_BLOCK_W)
    on_diag = jnp.eye(per_half, dtype=bool)[None, :, None, :, None]
    dense = jnp.where(on_diag, w4[:, :, :, None, :], 0.0)
    return dense.reshape(2, per_half * LRU_BLOCK_W, per_half * LRU_BLOCK_W).astype(BF16)


def _rope_tables(pos):
    half = HEAD_DIM // 2
    inv = ROPE_THETA ** (-jnp.arange(half, dtype=F32) / half)
    ang = pos.astype(F32)[:, None] * inv[None, :]
    cos = jnp.cos(ang)
    sin = jnp.sin(ang)
    cos128 = jnp.concatenate([cos, cos, cos, cos], axis=1)
    sin128 = jnp.concatenate([-sin, sin, -sin, sin], axis=1)
    return cos128, sin128


def _resident_spec(arr):
    zeros = (0,) * arr.ndim
    return pl.BlockSpec(arr.shape, lambda i: zeros)


def kernel(x_prompt, x_sample, state_lru_h, state_conv, cache_win_k, cache_win_v, c_prompt, c_sample, w_ada, b_ada, norm1_g, w_in, b_in, conv_w, conv_b, lru_wa, lru_ba, lru_wx, lru_bx, lru_lambda, attn_sinks, w_out, b_out, norm2_g, w_router, b_router, w1, b1, w2, b2, final_g):
    bp, seq, _ = x_prompt.shape
    bd, tdec, _ = x_sample.shape
    assert tdec == SUBLANES and seq % (MIX_TILES * SEQ_TILE) == 0 and bd % SAMPLE_BT == 0
    assert SAMPLE_BT * tdec == SEQ_TILE
    n_prompt = bp * seq
    n_sample = bd * tdec
    n_tok = n_prompt + n_sample
    l = 0

    head_perm = [h for c in range(4) for h in (c, GROUP + c)]
    o2 = 2 * LRU_WIDTH

    def permute_heads(arr, axis, start):
        take = lambda a, b: lax.slice_in_dim(arr, a, b, axis=axis)
        heads = [take(start + h * HEAD_DIM, start + (h + 1) * HEAD_DIM) for h in head_perm]
        return jnp.concatenate([take(0, start)] + heads + [take(start + Q_WIDTH, arr.shape[axis])], axis=axis)

    sinks_perm = attn_sinks[l]

    wr = jnp.tile(w_router[l], (1, TOP_K))
    wr_hi = wr.astype(BF16)
    weights = dict(
        norm1_g=norm1_g[l][None, :], w_in=permute_heads(w_in[l], 1, o2).astype(BF16),
        b_in=permute_heads(b_in[l], 0, o2)[None, :],
        conv_w=conv_w[l], conv_b=conv_b[l][None, :],
        ga=_block_diag_halves(lru_wa[l]), gx=_block_diag_halves(lru_wx[l]),
        lru_ba=lru_ba[l][None, :], lru_bx=lru_bx[l][None, :], lam=lru_lambda[l][None, :],
        w_out=permute_heads(w_out[l], 0, LRU_WIDTH).astype(BF16), b_out=b_out[l][None, :], norm2_g=norm2_g[l][None, :],
        wr_hi=wr_hi, wr_lo=(wr - wr_hi.astype(F32)).astype(BF16),
        b_router=jnp.tile(b_router[l], TOP_K)[None, :],
    )
    wlist = [weights[n] for n in WEIGHT_NAMES]

    mod_p, mod_s = _ada(jnp.concatenate([c_prompt, c_sample], axis=0), bp, w_ada[l], b_ada[l][None, :])
    mod_p = mod_p.reshape(bp, 6, D_MODEL)

    cos_p, sin_p = _rope_tables(jnp.arange(seq, dtype=jnp.int32))
    cos_s, sin_s = _rope_tables(PAST_LEN + jnp.arange(tdec, dtype=jnp.int32))
    cos_s = jnp.tile(cos_s, (SAMPLE_BT, 1))
    sin_s = jnp.tile(sin_s, (SAMPLE_BT, 1))
    h0_rows = jnp.pad(state_lru_h[l][:, None, :], ((0, 0), (0, tdec - 1), (0, 0))).reshape(n_sample, LRU_WIDTH)
    cprev_rows = jnp.pad(state_conv[l], ((0, 0), (0, tdec - (CONV_W - 1)), (0, 0))).reshape(n_sample, LRU_WIDTH)
    ck = cache_win_k[l].reshape(bd, WINDOW, KV_WIDTH)
    cv = cache_win_v[l].reshape(bd, WINDOW, KV_WIDTH)
    npt = n_prompt // SEQ_TILE
    nst = n_sample // SEQ_TILE
    mix_rows = MIX_TILES * SEQ_TILE
    steps_per_seq = seq // mix_rows
    wspecs = [_resident_spec(a) for a in wlist]

    rows_p = lambda width: pl.BlockSpec((mix_rows, width), lambda i: (i, 0))
    tail_p = lambda rows, width: pl.BlockSpec((None, rows, width), lambda i: (i // steps_per_seq, 0, 0))
    (x1_p, h2_p, route_p, cnt_p, hlast_p, ulast_p, klast_p, vlast_p) = pl.pallas_call(
        functools.partial(_prompt_kernel, steps_per_seq),
        grid=(n_prompt // mix_rows,),
        in_specs=[rows_p(D_MODEL),
                  pl.BlockSpec((None, 6, D_MODEL), lambda i: (i // steps_per_seq, 0, 0)),
                  pl.BlockSpec((mix_rows, LANES), lambda i: (i % steps_per_seq, 0)),
                  pl.BlockSpec((mix_rows, LANES), lambda i: (i % steps_per_seq, 0)),
                  pl.BlockSpec(memory_space=pltpu.SMEM)] + wspecs,
        out_specs=(rows_p(D_MODEL), rows_p(D_MODEL), rows_p(LANES),
                   pl.BlockSpec((MIX_TILES, SUBLANES, LANES), lambda i: (i, 0, 0)),
                   tail_p(SUBLANES, LRU_WIDTH), tail_p(SUBLANES, LRU_WIDTH),
                   tail_p(WINDOW, KV_WIDTH), tail_p(WINDOW, KV_WIDTH)),
        out_shape=(
            jax.ShapeDtypeStruct((n_prompt, D_MODEL), F32),
            jax.ShapeDtypeStruct((n_prompt, D_MODEL), F32),
            jax.ShapeDtypeStruct((n_prompt, LANES), F32),
            jax.ShapeDtypeStruct((npt, SUBLANES, LANES), F32),
            jax.ShapeDtypeStruct((bp, SUBLANES, LRU_WIDTH), F32),
            jax.ShapeDtypeStruct((bp, SUBLANES, LRU_WIDTH), F32),
            jax.ShapeDtypeStruct((bp, WINDOW, KV_WIDTH), F32),
            jax.ShapeDtypeStruct((bp, WINDOW, KV_WIDTH), F32),
        ),
        scratch_shapes=[pltpu.VMEM((SUBLANES, LRU_WIDTH), F32), pltpu.VMEM((SUBLANES, LRU_WIDTH), F32),
                        pltpu.VMEM((WINDOW, KV_WIDTH), F32), pltpu.VMEM((WINDOW, KV_WIDTH), F32)],
        compiler_params=pltpu.CompilerParams(dimension_semantics=("arbitrary",), vmem_limit_bytes=VMEM_LIMIT),
        name="prompt_mixer",
    )(x_prompt.reshape(n_prompt, D_MODEL), mod_p, cos_p, sin_p, sinks_perm, *wlist)

    rows_s = lambda width: pl.BlockSpec((SEQ_TILE, width), lambda i: (i, 0))
    cache_spec = pl.BlockSpec((SAMPLE_BT, WINDOW, KV_WIDTH), lambda i: (i, 0, 0))
    (x1_s, h2_s, route_s, cnt_s, g2_rows, hs_s, u_s, s_k, s_v) = pl.pallas_call(
        _sample_kernel,
        grid=(nst,),
        in_specs=[rows_s(D_MODEL),
                  pl.BlockSpec((6, SAMPLE_BT, D_MODEL), lambda i: (0, i, 0)),
                  pl.BlockSpec((SEQ_TILE, LANES), lambda i: (0, 0)),
                  pl.BlockSpec((SEQ_TILE, LANES), lambda i: (0, 0)),
                  pl.BlockSpec(memory_space=pltpu.SMEM),
                  rows_s(LRU_WIDTH), rows_s(LRU_WIDTH), cache_spec, cache_spec] + wspecs,
        out_specs=(rows_s(D_MODEL), rows_s(D_MODEL), rows_s(LANES),
                   pl.BlockSpec((None, SUBLANES, LANES), lambda i: (i, 0, 0)),
                   rows_s(D_MODEL), rows_s(LRU_WIDTH), rows_s(LRU_WIDTH), cache_spec, cache_spec),
        out_shape=(
            jax.ShapeDtypeStruct((n_sample, D_MODEL), F32), jax.ShapeDtypeStruct((n_sample, D_MODEL), F32),
            jax.ShapeDtypeStruct((n_sample, LANES), F32), jax.ShapeDtypeStruct((nst, SUBLANES, LANES), F32),
            jax.ShapeDtypeStruct((n_sample, D_MODEL), F32),
            jax.ShapeDtypeStruct((n_sample, LRU_WIDTH), F32),
            jax.ShapeDtypeStruct((n_sample, LRU_WIDTH), F32),
            jax.ShapeDtypeStruct((bd, WINDOW, KV_WIDTH), F32),
            jax.ShapeDtypeStruct((bd, WINDOW, KV_WIDTH), F32),
        ),
        compiler_params=pltpu.CompilerParams(dimension_semantics=("arbitrary",), vmem_limit_bytes=VMEM_LIMIT),
        name="sample_mixer",
    )(x_sample.reshape(n_sample, D_MODEL), mod_s, cos_s, sin_s, sinks_perm, h0_rows, cprev_rows, ck, cv, *wlist)
    tile_cnt = jnp.concatenate([cnt_p, cnt_s], axis=0)

    n_tiles = npt + nst
    n_assign = n_tok * TOP_K
    max_rows = n_assign + n_tiles * N_EXPERTS * (SUBLANES - 1) + N_EXPERTS * (MOE_TM - 1)
    n_blocks = -(-max_rows // MOE_TM)
    cnt = tile_cnt[:, 0, :N_EXPERTS].astype(jnp.int32)
    cnt8 = (cnt + SUBLANES - 1) // SUBLANES * SUBLANES
    counts = jnp.sum(cnt8, axis=0)
    pcounts = (counts + MOE_TM - 1) // MOE_TM * MOE_TM
    pend = jnp.cumsum(pcounts)
    pstart = pend - pcounts
    tile_off = pstart[None, :] + jnp.cumsum(cnt8, axis=0) - cnt8
    tot8 = jnp.sum(cnt8, axis=1)
    meta = jnp.concatenate([pstart + counts, pend, pend[-1:] // MOE_TM]).astype(jnp.int32)
    cnt8_flat = cnt8.reshape(-1)
    off_flat = tile_off.reshape(-1).astype(jnp.int32)

    xs = _dispatch(cnt8_flat, off_flat, tot8, meta, h2_p, h2_s, route_p, route_s, n_blocks * MOE_TM)

    big, small = MOE_CHUNK_CODES[0], MOE_CHUNK_CODES[1:]
    n_tm = pcounts // MOE_TM
    n_big = n_tm // big
    present = [(n_tm // b) % 2 for b in small]
    n_ch = n_big + sum(present)
    ch_end = jnp.cumsum(n_ch)
    ch_start = ch_end - n_ch
    n_chunks = n_blocks // big + len(small) * N_EXPERTS
    ci = jnp.arange(n_chunks, dtype=jnp.int32)
    owner = jnp.logical_and(ch_start[None, :] <= ci[:, None], ci[:, None] < ch_end[None, :])
    pick = lambda v: jnp.sum(jnp.where(owner, v[None, :], 0), axis=1)
    live = ci < ch_end[-1]
    local = ci - pick(ch_start)
    chunk_exp = jnp.where(live, pick(jnp.arange(N_EXPERTS, dtype=jnp.int32)), N_EXPERTS - 1).astype(jnp.int32)
    big_c = pick(n_big)
    code = jnp.where(local < big_c, big, 0)
    blocks_before = jnp.where(local < big_c, local * big, 0)
    order = big_c
    blocks = big_c * big
    for b, has in zip(small, present):
        has_c = pick(has)
        is_b = jnp.logical_and(has_c == 1, local == order)
        code = jnp.where(is_b, b, code)
        blocks_before = jnp.where(is_b, blocks, blocks_before)
        order = order + has_c
        blocks = blocks + has_c * b
    chunk_row = jnp.where(live, pick(pstart) + blocks_before * MOE_TM, 0).astype(jnp.int32)
    chunk_size = jnp.where(live, code, 0).astype(jnp.int32)
    has_rows = n_ch > 0
    e_ar = jnp.arange(N_EXPERTS, dtype=jnp.int32)
    w_slot = (jnp.cumsum(has_rows.astype(jnp.int32)) - 1) % 2
    later = jnp.logical_and(has_rows[None, :], e_ar[None, :] > e_ar[:, None])
    nxt = jnp.min(jnp.where(later, e_ar[None, :], N_EXPERTS), axis=1)
    nxt = jnp.where(nxt == N_EXPERTS, -1, nxt)
    weight_tbl = jnp.concatenate([jnp.logical_and(live, local == 0).astype(jnp.int32),
                                  pick(w_slot), jnp.where(live, pick(nxt), -1)]).astype(jnp.int32)
    moe_meta = jnp.concatenate([meta, ch_end[-1:].astype(jnp.int32)])
    ys = _moe(chunk_row, chunk_size, chunk_exp, weight_tbl, moe_meta, xs, w1[l], b1[l][:, None, :], w2[l], b2[l][:, None, :])

    y_p, y_s = _combine(cnt8_flat, off_flat, tot8, ys, route_p, route_s, x1_p, x1_s, mod_p, g2_rows,
                        final_g[None, :], seq // SEQ_TILE)

    y_prompt = y_p.reshape(bp, seq, D_MODEL)
    y_sample = y_s.reshape(bd, tdec, D_MODEL)
    p_h = hlast_p[:, SUBLANES - 1, :][None]
    p_c = ulast_p[:, SUBLANES - (CONV_W - 1):, :][None]
    p_k = klast_p.reshape(1, bp, WINDOW, N_KV_HEADS, HEAD_DIM)
    p_v = vlast_p.reshape(1, bp, WINDOW, N_KV_HEADS, HEAD_DIM)
    s_h = hs_s.reshape(bd, tdec, LRU_WIDTH)[:, tdec - 1, :][None]
    s_c = u_s.reshape(bd, tdec, LRU_WIDTH)[:, tdec - (CONV_W - 1):, :][None]
    s_kk = s_k.reshape(1, bd, WINDOW, N_KV_HEADS, HEAD_DIM)
    s_vv = s_v.reshape(1, bd, WINDOW, N_KV_HEADS, HEAD_DIM)
    return (y_prompt, y_sample, p_h, p_c, p_k, p_v, s_h, s_c, s_kk, s_vv)
```

```python
import functools

import jax
import jax.numpy as jnp
from jax import lax
from jax.experimental import pallas as pl
from jax.experimental.pallas import tpu as pltpu

F32 = jnp.float32
BF16 = jnp.bfloat16

D_MODEL = 1024
LRU_WIDTH = 512
LRU_BLOCKS = 8
LRU_BLOCK_W = LRU_WIDTH // LRU_BLOCKS
CONV_W = 4
LRU_C = 8.0
HEAD_DIM = 64
N_HEADS = 8
N_KV_HEADS = 2
GROUP = N_HEADS // N_KV_HEADS
WINDOW = 128
ROPE_THETA = 10000.0
N_EXPERTS = 32
TOP_K = 4
D_FF = D_MODEL
SWIGLU_LIMIT = 7.0
SWIGLU_ALPHA = 1.702
NORM_EPS = 1e-5
PAST_LEN = 8192
Q_WIDTH = N_HEADS * HEAD_DIM
KV_WIDTH = N_KV_HEADS * HEAD_DIM
IN_WIDTH = 2 * LRU_WIDTH + Q_WIDTH + 2 * KV_WIDTH

LANES = 128
SUBLANES = 8
SEQ_TILE = 256
MIX_TILES = 2
SAMPLE_BT = 32
MOE_TM = 256
MOE_CHUNK_CODES = (4, 2, 1)
MOE_CH = MOE_CHUNK_CODES[0] * MOE_TM
NEG_BIG = -1e30
VMEM_LIMIT = 56 * 1024 * 1024


def _rms(x, g):
    return x * lax.rsqrt(jnp.mean(x * x, axis=-1, keepdims=True) + NORM_EPS) * g


def _rms_mod(x, gain, shift):
    return x * lax.rsqrt(jnp.mean(x * x, axis=-1, keepdims=True) + NORM_EPS) * gain + shift


def _group_scan(a, b):
    rows, width = a.shape
    groups = rows // SUBLANES
    a3 = a.reshape(groups, SUBLANES, width)
    b3 = b.reshape(groups, SUBLANES, width)
    t = lax.broadcasted_iota(jnp.int32, (1, SUBLANES, 1), 1)
    d = 1
    while d < SUBLANES:
        keep = t >= d
        a_s = jnp.where(keep, pltpu.roll(a3, d, axis=1), 1.0)
        b_s = jnp.where(keep, pltpu.roll(b3, d, axis=1), 0.0)
        b3 = a3 * b_s + b3
        a3 = a3 * a_s
        d *= 2
    return a3.reshape(rows, width), b3.reshape(rows, width)


def _chain_groups(a_grp, b_grp, h_in):
    rows = a_grp.shape[0]
    out = []
    carry = h_in
    for g in range(rows // SUBLANES):
        sl = slice(g * SUBLANES, (g + 1) * SUBLANES)
        hg = b_grp[sl] + a_grp[sl] * carry
        out.append(hg)
        carry = hg[SUBLANES - 1:SUBLANES]
    return jnp.concatenate(out, axis=0)


def _rope128(x, cos, sin_signed, first_half):
    sw = jnp.where(first_half, pltpu.roll(x, LANES - HEAD_DIM // 2, axis=1), pltpu.roll(x, HEAD_DIM // 2, axis=1))
    return x * cos + sw * sin_signed


def _softplus(x):
    return jnp.maximum(x, 0.0) + jnp.log1p(jnp.exp(-jnp.abs(x)))


def _lru_coeffs(uc, w, first_pos_mask):
    ub = uc.astype(BF16)
    half = LRU_WIDTH // 2
    ra = jnp.concatenate([jnp.dot(ub[:, :half], w['ga'][0], preferred_element_type=F32),
                          jnp.dot(ub[:, half:], w['ga'][1], preferred_element_type=F32)], axis=1)
    rx = jnp.concatenate([jnp.dot(ub[:, :half], w['gx'][0], preferred_element_type=F32),
                          jnp.dot(ub[:, half:], w['gx'][1], preferred_element_type=F32)], axis=1)
    r = jax.nn.sigmoid(ra + w['lru_ba'][...])
    i = jax.nn.sigmoid(rx + w['lru_bx'][...])
    log_a = -LRU_C * r * _softplus(-w['lam'][...])
    a = jnp.exp(log_a)
    om = 1.0 - a * a
    mult = jnp.where(om > 0.0, om * lax.rsqrt(om), 0.0)
    if first_pos_mask is not None:
        mult = jnp.where(first_pos_mask, 1.0, mult)
    return a, mult * i * uc


def _conv_taps(u, s1, s2, s3, w):
    cw = w['conv_w']
    return w['conv_b'][...] + s3 * cw[0:1, :] + s2 * cw[1:2, :] + s1 * cw[2:3, :] + u * cw[3:4, :]


def _in_proj(x, mod, w):
    sh1, sc1 = mod
    h = _rms_mod(x, w['norm1_g'][...] * (1.0 + sc1), sh1)
    return jnp.dot(h.astype(BF16), w['w_in'][...], preferred_element_type=F32) + w['b_in'][...]


def _post_mix(x, mix, mod, w):
    g1, sh2, sc2 = mod
    x1 = x + g1 * (jnp.dot(mix.astype(BF16), w['w_out'][...], preferred_element_type=F32) + w['b_out'][...])
    h2 = _rms_mod(x1, w['norm2_g'][...] * (1.0 + sc2), sh2)
    h2_hi = h2.astype(BF16)
    h2_lo = (h2 - h2_hi.astype(F32)).astype(BF16)
    logits = (jnp.dot(h2_hi, w['wr_hi'][...], preferred_element_type=F32)
              + jnp.dot(h2_lo, w['wr_hi'][...], preferred_element_type=F32)
              + jnp.dot(h2_hi, w['wr_lo'][...], preferred_element_type=F32)) + w['b_router'][...]
    return x1, h2, logits


ROUTE_E, ROUTE_G, ROUTE_R = 0, TOP_K, 2 * TOP_K


def _lane_roll1(v, shift):
    return pltpu.roll(jnp.broadcast_to(v, (SUBLANES, LANES)), shift, axis=1)[0:1]


def _route_tile(lg):
    rows = lg.shape[0]
    lane = lax.broadcasted_iota(jnp.int32, (1, LANES), 1)
    e_of = lane % N_EXPERTS
    grp = lane // N_EXPERTS
    e_id = e_of.astype(F32)
    onehot = jnp.zeros((rows, LANES), F32)
    vals, ids = [], []
    for k in range(TOP_K):
        m = jnp.max(lg, axis=1, keepdims=True)
        idx = jnp.min(jnp.where(lg == m, e_id, float(N_EXPERTS)), axis=1, keepdims=True)
        sel = e_id == idx
        lg = jnp.where(sel, -jnp.inf, lg)
        onehot = jnp.where(jnp.logical_and(sel, grp == k), 1.0, onehot)
        vals.append(m)
        ids.append(idx)
    ex = [jnp.exp(v - vals[0]) for v in vals]
    denom = ex[0] + ex[1] + ex[2] + ex[3]

    r_i = lax.broadcasted_iota(jnp.int32, (rows, rows), 0)
    c_i = lax.broadcasted_iota(jnp.int32, (rows, rows), 1)
    strict_lower = jnp.where(r_i > c_i, 1.0, 0.0).astype(BF16)
    prefix = jnp.dot(strict_lower, onehot.astype(BF16), preferred_element_type=F32)
    cnt = jnp.sum(onehot, axis=0, keepdims=True)
    base = jnp.zeros((1, LANES), F32)
    tot = cnt
    for s in range(1, TOP_K):
        rolled = _lane_roll1(cnt, s * N_EXPERTS)
        base = base + jnp.where(lane >= s * N_EXPERTS, rolled, 0.0)
        tot = tot + rolled
    pad_cnt = jnp.floor((tot + (SUBLANES - 1.0)) * (1.0 / SUBLANES)) * SUBLANES
    inc = pad_cnt
    d = 1
    while d < N_EXPERTS:
        inc = inc + jnp.where(e_of >= d, _lane_roll1(inc, d), 0.0)
        d *= 2
    strip_start = inc - pad_cnt
    ranked = onehot * (prefix + base + strip_start)

    route = jnp.zeros((rows, LANES), F32)
    for k in range(TOP_K):
        rank_k = jnp.sum(jnp.where(grp == k, ranked, 0.0), axis=1, keepdims=True)
        route = jnp.where(lane == ROUTE_E + k, ids[k], route)
        route = jnp.where(lane == ROUTE_G + k, ex[k] / denom, route)
        route = jnp.where(lane == ROUTE_R + k, rank_k, route)
    return route, tot


def _softmax_with_sink(s, sink_col):
    m = jnp.maximum(jnp.max(s, axis=-1, keepdims=True), sink_col)
    p = jnp.exp(s - m)
    denom = jnp.sum(p, axis=-1, keepdims=True) + jnp.exp(sink_col - m)
    return p, denom


WEIGHT_NAMES = ('norm1_g', 'w_in', 'b_in', 'conv_w', 'conv_b', 'ga', 'gx', 'lru_ba', 'lru_bx', 'lam',
                'w_out', 'b_out', 'norm2_g', 'wr_hi', 'wr_lo', 'b_router')


def _split_bf16(x):
    hi = x.astype(BF16)
    return hi, (x - hi.astype(F32)).astype(BF16)


def _ada_kernel(n_prompt_rows, c_ref, w_ref, b_ref, op_ref, os_ref):
    c = c_ref[...]
    s_hi, s_lo = _split_bf16(c * jax.nn.sigmoid(c))
    w_hi, w_lo = _split_bf16(w_ref[...])
    mod = (jnp.dot(s_hi, w_hi, preferred_element_type=F32) + jnp.dot(s_lo, w_hi, preferred_element_type=F32)
           + jnp.dot(s_hi, w_lo, preferred_element_type=F32)) + b_ref[...]
    op_ref[...] = mod[:n_prompt_rows]
    os_ref[...] = mod[n_prompt_rows:]


def _ada(c_all, n_prompt_rows, w_ada, b_ada):
    rows = c_all.shape[0]
    n_s = rows - n_prompt_rows
    assert n_prompt_rows % SUBLANES == 0
    return pl.pallas_call(
        functools.partial(_ada_kernel, n_prompt_rows),
        grid=(6,),
        in_specs=[pl.BlockSpec((rows, D_MODEL), lambda i: (0, 0)),
                  pl.BlockSpec((D_MODEL, D_MODEL), lambda i: (0, i)),
                  pl.BlockSpec((1, D_MODEL), lambda i: (0, i))],
        out_specs=(pl.BlockSpec((n_prompt_rows, D_MODEL), lambda i: (0, i)),
                   pl.BlockSpec((None, n_s, D_MODEL), lambda i: (i, 0, 0))),
        out_shape=(jax.ShapeDtypeStruct((n_prompt_rows, 6 * D_MODEL), F32),
                   jax.ShapeDtypeStruct((6, n_s, D_MODEL), F32)),
        compiler_params=pltpu.CompilerParams(dimension_semantics=("arbitrary",), vmem_limit_bytes=VMEM_LIMIT),
        name="ada",
    )(c_all, w_ada, b_ada)


def _prompt_body(seq_start, x_ref, mod_ref, cos_ref, sin_ref, sinks_ref, w, x1_ref, h2_ref, lg_ref,
                 hlast_ref, ulast_ref, klast_ref, vlast_ref, conv_c, h_c, k_c, v_c):
    ts = SEQ_TILE

    if seq_start is not None:
        @pl.when(seq_start)
        def _():
            conv_c[...] = jnp.zeros_like(conv_c)
            h_c[...] = jnp.zeros_like(h_c)
            k_c[...] = jnp.zeros_like(k_c)
            v_c[...] = jnp.zeros_like(v_c)

    x = x_ref[...]
    mod = mod_ref[...]
    proj = _in_proj(x, (mod[0:1], mod[1:2]), w)
    u = proj[:, :LRU_WIDTH]
    gate = proj[:, LRU_WIDTH:2 * LRU_WIDTH]
    o2 = 2 * LRU_WIDTH

    rowid = lax.broadcasted_iota(jnp.int32, (ts, 1), 0)
    u_ext = jnp.concatenate([conv_c[...], u], axis=0)
    s1, s2, s3 = (pltpu.roll(u_ext, d, axis=0)[SUBLANES:] for d in (1, 2, 3))
    uc = _conv_taps(u, s1, s2, s3, w)
    conv_c[...] = u[ts - SUBLANES:]
    ulast_ref[...] = u[ts - SUBLANES:]

    first_pos = None if seq_start is None else jnp.logical_and(rowid == 0, seq_start)
    a, bt = _lru_coeffs(uc, w, first_pos)
    hs = _chain_groups(*_group_scan(a, bt), h_c[0:1, :])
    h_tail = hs[ts - SUBLANES:]
    h_c[...] = jnp.broadcast_to(h_tail[SUBLANES - 1:SUBLANES, :], h_c.shape)
    hlast_ref[...] = h_tail
    lru_out = hs * jax.nn.gelu(gate)

    cos = cos_ref[...]
    sin = sin_ref[...]
    lane = lax.broadcasted_iota(jnp.int32, (1, LANES), 1)
    first_half = (lane % HEAD_DIM) < (HEAD_DIM // 2)
    qcols = [_rope128(proj[:, o2 + c * LANES:o2 + (c + 1) * LANES], cos, sin, first_half) * (HEAD_DIM ** -0.5)
             for c in range(4)]
    k = _rope128(proj[:, o2 + Q_WIDTH:o2 + Q_WIDTH + KV_WIDTH], cos, sin, first_half)
    v = proj[:, o2 + Q_WIDTH + KV_WIDTH:]
    k_ext = jnp.concatenate([k_c[...], k], axis=0).astype(BF16)
    v_ext = jnp.concatenate([v_c[...], v], axis=0).astype(BF16)
    k_c[...] = k[ts - WINDOW:]
    v_c[...] = v[ts - WINDOW:]
    klast_ref[...] = k[ts - WINDOW:]
    vlast_ref[...] = v[ts - WINDOW:]

    qi = lax.broadcasted_iota(jnp.int32, (WINDOW, 2 * WINDOW), 0)
    kj = lax.broadcasted_iota(jnp.int32, (WINDOW, 2 * WINDOW), 1)
    band = jnp.logical_and(kj > qi, kj <= qi + WINDOW)
    lane_lo = lane < HEAD_DIM
    grow = lax.broadcasted_iota(jnp.int32, (GROUP * WINDOW, 1), 0) // WINDOW
    attn_cols = [[] for _ in range(4)]
    for blk in range(ts // WINDOW):
        if seq_start is None or blk > 0:
            mask = band
        else:
            mask = jnp.logical_and(band, jnp.logical_or(kj >= WINDOW, jnp.logical_not(seq_start)))
        mask4 = jnp.concatenate([mask] * GROUP, axis=0)
        kb = k_ext[blk * WINDOW:(blk + 2) * WINDOW]
        vb = v_ext[blk * WINDOW:(blk + 2) * WINDOW]
        outs = []
        for kv in range(N_KV_HEADS):
            sel = lane_lo if kv == 0 else jnp.logical_not(lane_lo)
            qs = jnp.concatenate(
                [jnp.where(sel, qc[blk * WINDOW:(blk + 1) * WINDOW], 0.0) for qc in qcols], axis=0).astype(BF16)
            s = lax.dot_general(qs, kb, (((1,), (1,)), ((), ())), preferred_element_type=F32)
            s = jnp.where(mask4, s, NEG_BIG)
            sink_col = jnp.zeros((GROUP * WINDOW, 1), F32)
            for g in range(GROUP):
                sink_col = jnp.where(grow == g, sinks_ref[kv * GROUP + g], sink_col)
            p, denom = _softmax_with_sink(s, sink_col)
            outs.append(jnp.dot(p.astype(BF16), vb, preferred_element_type=F32) / denom)
        for c in range(4):
            attn_cols[c].append(jnp.where(lane_lo, outs[0][c * WINDOW:(c + 1) * WINDOW],
                                          outs[1][c * WINDOW:(c + 1) * WINDOW]))
    attn = jnp.concatenate([jnp.concatenate(cols, axis=0) for cols in attn_cols], axis=1)

    mix = jnp.concatenate([lru_out, attn], axis=1)
    x1, h2, logits = _post_mix(x, mix, (mod[2:3], mod[3:4], mod[4:5]), w)
    x1_ref[...] = x1
    h2_ref[...] = h2
    route, tot = _route_tile(logits)
    lg_ref[0][...] = route
    lg_ref[1][...] = jnp.broadcast_to(tot, lg_ref[1].shape)


def _expand_rows(m, t):
    b, wd = m.shape
    return jnp.broadcast_to(m[:, None, :], (b, t, wd)).reshape(b * t, wd)


def _sample_body(x_ref, mod_ref, cos_ref, sin_ref, sinks_ref, h0_ref, cprev_ref, ck_ref, cv_ref, w,
                 x1_ref, h2_ref, lg_ref, g2_ref, hs_ref, u_ref, ko_ref, vo_ref):
    bt_, t = SAMPLE_BT, SUBLANES
    rows = bt_ * t

    x = x_ref[...]
    mods = [_expand_rows(mod_ref[i], t) for i in range(6)]
    proj = _in_proj(x, (mods[0], mods[1]), w)
    u = proj[:, :LRU_WIDTH]
    gate = proj[:, LRU_WIDTH:2 * LRU_WIDTH]
    o2 = 2 * LRU_WIDTH
    u_ref[...] = u

    rowid = lax.broadcasted_iota(jnp.int32, (rows, 1), 0) % t
    cprev = cprev_ref[...]
    taps = []
    for d in (1, 2, 3):
        taps.append(jnp.where(rowid >= d, pltpu.roll(u, d, axis=0),
                              pltpu.roll(cprev, (d - (CONV_W - 1)) % rows, axis=0)))
    uc = _conv_taps(u, taps[0], taps[1], taps[2], w)

    a, bt = _lru_coeffs(uc, w, None)
    bt = bt + a * h0_ref[...]
    _, hs = _group_scan(a, bt)
    hs_ref[...] = hs
    lru_out = hs * jax.nn.gelu(gate)

    cos = cos_ref[...]
    sin = sin_ref[...]
    lane = lax.broadcasted_iota(jnp.int32, (1, LANES), 1)
    first_half = (lane % HEAD_DIM) < (HEAD_DIM // 2)
    qcols = [_rope128(proj[:, o2 + c * LANES:o2 + (c + 1) * LANES], cos, sin, first_half) * (HEAD_DIM ** -0.5)
             for c in range(4)]
    k = _rope128(proj[:, o2 + Q_WIDTH:o2 + Q_WIDTH + KV_WIDTH], cos, sin, first_half)
    v = proj[:, o2 + Q_WIDTH + KV_WIDTH:]
    k3 = k.reshape(bt_, t, KV_WIDTH)
    v3 = v.reshape(bt_, t, KV_WIDTH)
    ck = ck_ref[...]
    cv = cv_ref[...]
    ko_ref[:, :WINDOW - t, :] = ck[:, t:, :]
    ko_ref[:, WINDOW - t:, :] = k3
    vo_ref[:, :WINDOW - t, :] = cv[:, t:, :]
    vo_ref[:, WINDOW - t:, :] = v3

    ckb, cvb, k3b, v3b = ck.astype(BF16), cv.astype(BF16), k3.astype(BF16), v3.astype(BF16)
    lane_lo = lane < HEAD_DIM
    gq = GROUP * t
    tq = lax.broadcasted_iota(jnp.int32, (1, gq, 1), 1) % t
    mask_c = lax.broadcasted_iota(jnp.int32, (1, gq, WINDOW), 2) > tq
    mask_n = lax.broadcasted_iota(jnp.int32, (1, gq, t), 2) <= tq
    grow = lax.broadcasted_iota(jnp.int32, (1, gq, 1), 1) // t
    bdims = (((2,), (2,)), ((0,), (0,)))
    pdims = (((2,), (1,)), ((0,), (0,)))
    outs = []
    for kv in range(N_KV_HEADS):
        sel = lane_lo if kv == 0 else jnp.logical_not(lane_lo)
        q3 = jnp.concatenate([jnp.where(sel, qc, 0.0).reshape(bt_, t, LANES) for qc in qcols], axis=1).astype(BF16)
        sc = lax.dot_general(q3, ckb, bdims, preferred_element_type=F32)
        sn = lax.dot_general(q3, k3b, bdims, preferred_element_type=F32)
        sc = jnp.where(mask_c, sc, NEG_BIG)
        sn = jnp.where(mask_n, sn, NEG_BIG)
        sink_col = jnp.zeros((1, gq, 1), F32)
        for g in range(GROUP):
            sink_col = jnp.where(grow == g, sinks_ref[kv * GROUP + g], sink_col)
        m = jnp.maximum(jnp.maximum(jnp.max(sc, axis=-1, keepdims=True), jnp.max(sn, axis=-1, keepdims=True)),
                        sink_col)
        pc = jnp.exp(sc - m)
        pn = jnp.exp(sn - m)
        denom = jnp.sum(pc, axis=-1, keepdims=True) + jnp.sum(pn, axis=-1, keepdims=True) + jnp.exp(sink_col - m)
        o = (lax.dot_general(pc.astype(BF16), cvb, pdims, preferred_element_type=F32)
             + lax.dot_general(pn.astype(BF16), v3b, pdims, preferred_element_type=F32)) / denom
        outs.append(o)
    attn = jnp.concatenate(
        [jnp.where(lane_lo, outs[0][:, c * t:(c + 1) * t, :], outs[1][:, c * t:(c + 1) * t, :]).reshape(rows, LANES)
         for c in range(4)], axis=1)

    mix = jnp.concatenate([lru_out, attn], axis=1)
    x1, h2, logits = _post_mix(x, mix, (mods[2], mods[3], mods[4]), w)
    x1_ref[...] = x1
    h2_ref[...] = h2
    route, tot = _route_tile(logits)
    lg_ref[0][...] = route
    lg_ref[1][...] = jnp.broadcast_to(tot, lg_ref[1].shape)
    g2_ref[...] = mods[5]


def _prompt_kernel(steps_per_seq, x_ref, mod_ref, cos_ref, sin_ref, sinks_ref, *rest):
    nw = len(WEIGHT_NAMES)
    w = dict(zip(WEIGHT_NAMES, rest[:nw]))
    (x1_ref, h2_ref, route_ref, cnt_ref, hlast_ref, ulast_ref, klast_ref, vlast_ref,
     conv_c, h_c, k_c, v_c) = rest[nw:]
    seq_start = pl.program_id(0) % steps_per_seq == 0
    for sub in range(MIX_TILES):
        rows = pl.ds(sub * SEQ_TILE, SEQ_TILE)
        _prompt_body(seq_start if sub == 0 else None, x_ref.at[rows], mod_ref, cos_ref.at[rows], sin_ref.at[rows],
                     sinks_ref, w, x1_ref.at[rows], h2_ref.at[rows], (route_ref.at[rows], cnt_ref.at[sub]),
                     hlast_ref, ulast_ref, klast_ref, vlast_ref, conv_c, h_c, k_c, v_c)


def _sample_kernel(x_ref, mod_ref, cos_ref, sin_ref, sinks_ref, h0_ref, cprev_ref, ck_ref, cv_ref, *rest):
    nw = len(WEIGHT_NAMES)
    w = dict(zip(WEIGHT_NAMES, rest[:nw]))
    x1_ref, h2_ref, route_ref, cnt_ref, g2_ref, hs_ref, u_ref, ko_ref, vo_ref = rest[nw:]
    _sample_body(x_ref, mod_ref, cos_ref, sin_ref, sinks_ref, h0_ref, cprev_ref, ck_ref, cv_ref, w,
                 x1_ref, h2_ref, (route_ref, cnt_ref), g2_ref, hs_ref, u_ref, ko_ref, vo_ref)


PACK_W = D_MODEL // 2
PACKED = jnp.int32


def _pack_bf16_pairs(x):
    return pltpu.pack_elementwise([x[:, :PACK_W], x[:, PACK_W:]], packed_dtype=BF16)


def _unpack_bf16_pairs(w):
    return tuple(pltpu.unpack_elementwise(w, index=k, packed_dtype=BF16, unpacked_dtype=F32).astype(BF16)
                 for k in range(2))


STRIP_SIZES = tuple(SUBLANES << b for b in range(6))
STRIP_LARGE = 64
SORT_ROWS = SEQ_TILE * TOP_K + N_EXPERTS * SUBLANES
TILE_WAIT_SIZES = tuple(SUBLANES << b for b in range(8))


def _for_strips(cnt_ref, off_ref, tile, buf_slot, hbm, sem, to_hbm, act):
    def e_body(e, local):
        n = cnt_ref[tile * N_EXPERTS + e]
        glob = off_ref[tile * N_EXPERTS + e]

        def pieces(sizes, done):
            for p in sizes:
                piece = n & p
                lo = pl.ds(pl.multiple_of(local + done, SUBLANES), p)
                gl = pl.ds(pl.multiple_of(glob + done, SUBLANES), p)

                @pl.when(piece != 0)
                def _():
                    if to_hbm:
                        act(pltpu.make_async_copy(buf_slot.at[lo], hbm.at[gl], sem))
                    else:
                        act(pltpu.make_async_copy(hbm.at[gl], buf_slot.at[lo], sem))
                done = done + piece

        large = tuple(p for p in reversed(STRIP_SIZES) if p >= STRIP_LARGE)
        small = tuple(p for p in reversed(STRIP_SIZES) if p < STRIP_LARGE)
        n_large = n & (-STRIP_LARGE)

        @pl.when(n_large != 0)
        def _():
            pieces(large, 0)
        pieces(small, n_large)
        return local + n
    lax.fori_loop(0, N_EXPERTS, e_body, 0)


def _wait_tile_rows(total, buf_slot, hbm, sem, to_hbm):
    for p in TILE_WAIT_SIZES:
        @pl.when((total & p) != 0)
        def _():
            if to_hbm:
                pltpu.make_async_copy(buf_slot.at[pl.ds(0, p)], hbm.at[pl.ds(0, p)], sem).wait()
            else:
                pltpu.make_async_copy(hbm.at[pl.ds(0, p)], buf_slot.at[pl.ds(0, p)], sem).wait()


def _dispatch_kernel(n_prompt_tiles, cnt_ref, off_ref, tot_ref, meta_ref, h2p_ref, h2s_ref, routep_ref, routes_ref,
                     xs_hbm, sbuf, zblk, sem, zsem):
    i = pl.program_id(0)
    nb = pl.num_programs(0)
    slot = i % 2
    n_blocks = xs_hbm.shape[0] // MOE_TM

    is_prompt = i < n_prompt_tiles
    h2 = jnp.where(is_prompt, h2p_ref[...], h2s_ref[...])
    route_t = jnp.where(is_prompt, routep_ref[...], routes_ref[...]).T
    r_pos = lax.broadcasted_iota(jnp.int32, (SORT_ROWS, SEQ_TILE), 0).astype(F32)
    perm = jnp.zeros((SORT_ROWS, SEQ_TILE), F32)
    for k in range(TOP_K):
        perm = jnp.where(r_pos == route_t[ROUTE_R + k:ROUTE_R + k + 1, :], 1.0, perm)
    sbuf[slot] = _pack_bf16_pairs(jnp.dot(perm.astype(BF16), h2.astype(BF16), preferred_element_type=F32))

    _for_strips(cnt_ref, off_ref, i, sbuf.at[slot], xs_hbm, sem.at[slot], True, lambda cp: cp.start())

    @pl.when(i > 0)
    def _():
        _wait_tile_rows(tot_ref[jnp.maximum(i - 1, 0)], sbuf.at[1 - slot], xs_hbm, sem.at[1 - slot], True)

    @pl.when(i == nb - 1)
    def _():
        _wait_tile_rows(tot_ref[i], sbuf.at[slot], xs_hbm, sem.at[slot], True)
        zblk[...] = jnp.zeros_like(zblk)

        def for_region_tails(act):
            def e_body(e, carry):
                start = meta_ref[e]
                n = meta_ref[N_EXPERTS + e] - start
                done = 0
                for p in reversed([q for q in STRIP_SIZES if q < MOE_TM]):
                    piece = n & p
                    rows = pl.ds(pl.multiple_of(start + done, SUBLANES), p)

                    @pl.when(piece != 0)
                    def _():
                        act(pltpu.make_async_copy(zblk.at[pl.ds(0, p)], xs_hbm.at[rows], zsem.at[0]))
                    done = done + piece
                return carry
            lax.fori_loop(0, N_EXPERTS, e_body, 0)

        def for_tail_blocks(act):
            def b_body(j, carry):
                act(pltpu.make_async_copy(zblk, xs_hbm.at[pl.ds(pl.multiple_of(j * MOE_TM, MOE_TM), MOE_TM)],
                                          zsem.at[0]))
                return carry
            lax.fori_loop(meta_ref[2 * N_EXPERTS], n_blocks, b_body, 0)

        for_region_tails(lambda cp: cp.start())
        for_tail_blocks(lambda cp: cp.start())
        for_region_tails(lambda cp: cp.wait())
        for_tail_blocks(lambda cp: cp.wait())


def _dispatch(cnt8, tile_off, tot8, meta, h2_p, h2_s, route_p, route_s, n_rows):
    npt = h2_p.shape[0] // SEQ_TILE
    p_tile = lambda i, *_: (jnp.minimum(i, npt - 1), 0)
    s_tile = lambda i, *_: (jnp.maximum(i - npt, 0), 0)
    nt = tot8.shape[0]
    tt = SEQ_TILE
    grid_spec = pltpu.PrefetchScalarGridSpec(
        num_scalar_prefetch=4,
        grid=(nt,),
        in_specs=[pl.BlockSpec((tt, D_MODEL), p_tile), pl.BlockSpec((tt, D_MODEL), s_tile),
                  pl.BlockSpec((tt, LANES), p_tile), pl.BlockSpec((tt, LANES), s_tile)],
        out_specs=pl.BlockSpec(memory_space=pl.ANY),
        scratch_shapes=[pltpu.VMEM((2, SORT_ROWS, PACK_W), PACKED), pltpu.VMEM((MOE_TM, PACK_W), PACKED),
                        pltpu.SemaphoreType.DMA((2,)), pltpu.SemaphoreType.DMA((1,))],
    )
    return pl.pallas_call(
        functools.partial(_dispatch_kernel, npt),
        grid_spec=grid_spec,
        out_shape=jax.ShapeDtypeStruct((n_rows, PACK_W), PACKED),
        compiler_params=pltpu.CompilerParams(dimension_semantics=("arbitrary",), vmem_limit_bytes=VMEM_LIMIT),
        name="dispatch",
    )(cnt8, tile_off, tot8, meta, h2_p, h2_s, route_p, route_s)


def _expert_mlp(words, w1b, b1_ref, w2b, b2_ref, act_ref):
    xb = jnp.concatenate(_unpack_bf16_pairs(words), axis=1)
    q = D_FF // 4
    for c in range(4):
        zg = jnp.dot(xb, w1b[:, c * q:(c + 1) * q], preferred_element_type=F32) + b1_ref[:, c * q:(c + 1) * q]
        zl = (jnp.dot(xb, w1b[:, D_FF + c * q:D_FF + (c + 1) * q], preferred_element_type=F32)
              + b1_ref[:, D_FF + c * q:D_FF + (c + 1) * q])
        glu = jnp.minimum(zg, SWIGLU_LIMIT)
        lin = jnp.clip(zl, -SWIGLU_LIMIT, SWIGLU_LIMIT)
        act_ref[:, c * q:(c + 1) * q] = (glu * jax.nn.sigmoid(SWIGLU_ALPHA * glu) * (lin + 1.0)).astype(BF16)
    return _pack_bf16_pairs(jnp.dot(act_ref[...], w2b[...], preferred_element_type=F32) + b2_ref[...])


def _moe_kernel(row_ref, size_ref, exp_ref, wt_ref, meta_ref, xs_hbm, w1_hbm, b1_ref, w2_hbm, b2_ref, ys_hbm,
                xin, yout, w1f, w2f, w1b, w2b, act, isem, osem, wsem):
    tm = MOE_TM
    n_blocks = ys_hbm.shape[0] // tm
    n_tbl = row_ref.shape[0]
    n = meta_ref[2 * N_EXPERTS + 1]

    def for_chunk(j, fn):
        r = pl.multiple_of(row_ref[j], tm)
        for code in MOE_CHUNK_CODES:
            @pl.when(size_ref[j] == code)
            def _():
                fn(r, code * tm)

    def in_copy(r, rows, s):
        return pltpu.make_async_copy(xs_hbm.at[pl.ds(r, rows)], xin.at[s, pl.ds(0, rows)], isem.at[s])

    def out_copy(r, rows, s):
        return pltpu.make_async_copy(yout.at[s, pl.ds(0, rows)], ys_hbm.at[pl.ds(r, rows)], osem.at[s])

    def start_in(j, s):
        for_chunk(j, lambda r, rows: in_copy(r, rows, s).start())

    def wait_in(j, s):
        for_chunk(j, lambda r, rows: in_copy(r, rows, s).wait())

    def start_out(j, s):
        for_chunk(j, lambda r, rows: out_copy(r, rows, s).start())

    def wait_out(j, s):
        for_chunk(j, lambda r, rows: out_copy(r, rows, s).wait())

    def weight_copies(e, ws):
        return (pltpu.make_async_copy(w1_hbm.at[e], w1f.at[ws], wsem.at[ws]),
                pltpu.make_async_copy(w2_hbm.at[e], w2f.at[ws], wsem.at[ws]))

    start_in(0, 0)
    for cp in weight_copies(exp_ref[0], 0):
        cp.start()

    def trip(i, carry):
        slot = i % 2

        @pl.when(i + 1 < n)
        def _():
            start_in(i + 1, 1 - slot)

        @pl.when(i >= 2)
        def _():
            wait_out(i - 2, slot)

        @pl.when(wt_ref[i] == 1)
        def _():
            ws = wt_ref[n_tbl + i]
            nxt = wt_ref[2 * n_tbl + i]
            for cp in weight_copies(exp_ref[i], ws):
                cp.wait()

            @pl.when(nxt >= 0)
            def _():
                for cp in weight_copies(nxt, 1 - ws):
                    cp.start()

            chunk = 128
            def cast_body(c, carry2):
                k0 = pl.multiple_of(c * chunk, chunk)
                w1b[pl.ds(k0, chunk), :] = w1f[ws, pl.ds(k0, chunk), :].astype(BF16)
                w2b[pl.ds(k0, chunk), :] = w2f[ws, pl.ds(k0, chunk), :].astype(BF16)
                return carry2
            lax.fori_loop(0, D_MODEL // chunk, cast_body, 0)

        wait_in(i, slot)

        for code in MOE_CHUNK_CODES:
            rows = code * tm

            @pl.when(size_ref[i] == code)
            def _():
                yout[slot, 0:rows] = _expert_mlp(xin[slot, 0:rows], w1b, b1_ref.at[exp_ref[i]], w2b,
                                                 b2_ref.at[exp_ref[i]], act.at[pl.ds(0, rows)])

        start_out(i, slot)
        return carry
    lax.fori_loop(0, n, trip, 0)

    @pl.when(n >= 2)
    def _():
        wait_out(n - 2, n % 2)
    wait_out(n - 1, (n - 1) % 2)

    yout[1, 0:tm] = jnp.zeros((tm, PACK_W), PACKED)

    def zero_block(j):
        return pltpu.make_async_copy(yout.at[1, pl.ds(0, tm)], ys_hbm.at[pl.ds(pl.multiple_of(j * tm, tm), tm)],
                                     osem.at[1])

    def start_body(j, carry):
        zero_block(j).start()
        return carry

    def wait_body(j, carry):
        zero_block(j).wait()
        return carry
    lax.fori_loop(meta_ref[2 * N_EXPERTS], n_blocks, start_body, 0)
    lax.fori_loop(meta_ref[2 * N_EXPERTS], n_blocks, wait_body, 0)


def _moe(chunk_row, chunk_size, chunk_exp, weight_tbl, meta, xs, w1, b1, w2, b2):
    whole = lambda i, *_: (0, 0, 0)
    grid_spec = pltpu.PrefetchScalarGridSpec(
        num_scalar_prefetch=5,
        grid=(1,),
        in_specs=[
            pl.BlockSpec(memory_space=pl.ANY),
            pl.BlockSpec(memory_space=pl.ANY),
            pl.BlockSpec((N_EXPERTS, 1, 2 * D_FF), whole),
            pl.BlockSpec(memory_space=pl.ANY),
            pl.BlockSpec((N_EXPERTS, 1, D_MODEL), whole),
        ],
        out_specs=pl.BlockSpec(memory_space=pl.ANY),
        scratch_shapes=[pltpu.VMEM((2, MOE_CH, PACK_W), PACKED), pltpu.VMEM((2, MOE_CH, PACK_W), PACKED),
                        pltpu.VMEM((2, D_MODEL, 2 * D_FF), F32), pltpu.VMEM((2, D_FF, D_MODEL), F32),
                        pltpu.VMEM((D_MODEL, 2 * D_FF), BF16), pltpu.VMEM((D_FF, D_MODEL), BF16),
                        pltpu.VMEM((MOE_CH, D_FF), BF16),
                        pltpu.SemaphoreType.DMA((2,)), pltpu.SemaphoreType.DMA((2,)), pltpu.SemaphoreType.DMA((2,))],
    )
    return pl.pallas_call(
        _moe_kernel,
        grid_spec=grid_spec,
        out_shape=jax.ShapeDtypeStruct(xs.shape, PACKED),
        compiler_params=pltpu.CompilerParams(dimension_semantics=("arbitrary",), vmem_limit_bytes=VMEM_LIMIT),
        name="moe",
    )(chunk_row, chunk_size, chunk_exp, weight_tbl, meta, xs, w1, b1, w2, b2)


def _combine_kernel(n_prompt_tiles, cnt_ref, off_ref, tot_ref, ys_hbm, routep_ref, routes_ref, x1p_ref, x1s_ref,
                    modp_ref, g2s_ref, fg_ref, op_ref, os_ref, buf, sem):
    i = pl.program_id(0)
    nb = pl.num_programs(0)
    slot = i % 2

    def fetch(tile, s):
        _for_strips(cnt_ref, off_ref, tile, buf.at[s], ys_hbm, sem.at[s], False, lambda cp: cp.start())

    @pl.when(i == 0)
    def _():
        buf[...] = jnp.zeros_like(buf)
        fetch(0, 0)

    @pl.when(i + 1 < nb)
    def _():
        fetch(i + 1, 1 - slot)

    _wait_tile_rows(tot_ref[i], buf.at[slot], ys_hbm, sem.at[slot], False)

    is_prompt = i < n_prompt_tiles
    route = jnp.where(is_prompt, routep_ref[...], routes_ref[...])
    c_pos = lax.broadcasted_iota(jnp.int32, (SEQ_TILE, SORT_ROWS), 1).astype(F32)
    gmat = jnp.zeros((SEQ_TILE, SORT_ROWS), F32)
    for k in range(TOP_K):
        gmat = jnp.where(c_pos == route[:, ROUTE_R + k:ROUTE_R + k + 1], route[:, ROUTE_G + k:ROUTE_G + k + 1], gmat)
    g_bf = gmat.astype(BF16)
    ff = jnp.concatenate([jnp.dot(g_bf, yb, preferred_element_type=F32) for yb in _unpack_bf16_pairs(buf[slot])],
                         axis=1)
    g2 = jnp.where(i < n_prompt_tiles, modp_ref[5:6, :], g2s_ref[...])
    x = jnp.where(is_prompt, x1p_ref[...], x1s_ref[...]) + g2 * ff
    y = _rms(x, fg_ref[...])

    @pl.when(i < n_prompt_tiles)
    def _():
        op_ref[...] = y

    @pl.when(i >= n_prompt_tiles)
    def _():
        os_ref[...] = y


def _combine(cnt8, tile_off, tot8, ys, route_p, route_s, x1_p, x1_s, mod_p, g2_rows, final_g, tiles_per_batch):
    nt = tot8.shape[0]
    tt = SEQ_TILE
    npt = x1_p.shape[0] // tt
    p_tile = lambda i, *_: (jnp.minimum(i, npt - 1), 0)
    s_tile = lambda i, *_: (jnp.maximum(i - npt, 0), 0)
    grid_spec = pltpu.PrefetchScalarGridSpec(
        num_scalar_prefetch=3,
        grid=(nt,),
        in_specs=[
            pl.BlockSpec(memory_space=pl.ANY),
            pl.BlockSpec((tt, LANES), p_tile), pl.BlockSpec((tt, LANES), s_tile),
            pl.BlockSpec((tt, D_MODEL), p_tile), pl.BlockSpec((tt, D_MODEL), s_tile),
            pl.BlockSpec((None, 6, D_MODEL), lambda i, *_: (jnp.minimum(i, npt - 1) // tiles_per_batch, 0, 0)),
            pl.BlockSpec((tt, D_MODEL), s_tile),
            pl.BlockSpec((1, D_MODEL), lambda i, *_: (0, 0)),
        ],
        out_specs=(pl.BlockSpec((tt, D_MODEL), p_tile), pl.BlockSpec((tt, D_MODEL), s_tile)),
        scratch_shapes=[pltpu.VMEM((2, SORT_ROWS, PACK_W), PACKED), pltpu.SemaphoreType.DMA((2,))],
    )
    return pl.pallas_call(
        functools.partial(_combine_kernel, npt),
        grid_spec=grid_spec,
        out_shape=(jax.ShapeDtypeStruct((npt * tt, D_MODEL), F32),
                   jax.ShapeDtypeStruct(((nt - npt) * tt, D_MODEL), F32)),
        compiler_params=pltpu.CompilerParams(dimension_semantics=("arbitrary",), vmem_limit_bytes=VMEM_LIMIT),
        name="combine",
    )(cnt8, tile_off, tot8, ys, route_p, route_s, x1_p, x1_s, mod_p, g2_rows, final_g)


def _block_diag_halves(wg):
    per_half = LRU_BLOCKS // 2
    w4 = wg.reshape(2, per_half, LRU_BLOCK_W, LRU_BLOCK_W)
    on_diag = jnp.eye(per_half, dtype=bool)[None, :, None, :, None]
    dense = jnp.where(on_diag, w4[:, :, :, None, :], 0.0)
    return dense.reshape(2, per_half * LRU_BLOCK_W, per_half * LRU_BLOCK_W).astype(BF16)


def _rope_tables(pos):
    half = HEAD_DIM // 2
    inv = ROPE_THETA ** (-jnp.arange(half, dtype=F32) / half)
    ang = pos.astype(F32)[:, None] * inv[None, :]
    cos = jnp.cos(ang)
    sin = jnp.sin(ang)
    cos128 = jnp.concatenate([cos, cos, cos, cos], axis=1)
    sin128 = jnp.concatenate([-sin, sin, -sin, sin], axis=1)
    return cos128, sin128


def _resident_spec(arr):
    zeros = (0,) * arr.ndim
    return pl.BlockSpec(arr.shape, lambda i: zeros)


def kernel(x_prompt, x_sample, state_lru_h, state_conv, cache_win_k, cache_win_v, c_prompt, c_sample, w_ada, b_ada, norm1_g, w_in, b_in, conv_w, conv_b, lru_wa, lru_ba, lru_wx, lru_bx, lru_lambda, attn_sinks, w_out, b_out, norm2_g, w_router, b_router, w1, b1, w2, b2, final_g):
    bp, seq, _ = x_prompt.shape
    bd, tdec, _ = x_sample.shape
    assert tdec == SUBLANES and seq % (MIX_TILES * SEQ_TILE) == 0 and bd % SAMPLE_BT == 0
    assert SAMPLE_BT * tdec == SEQ_TILE
    n_prompt = bp * seq
    n_sample = bd * tdec
    n_tok = n_prompt + n_sample
    l = 0

    head_perm = [h for c in range(4) for h in (c, GROUP + c)]
    o2 = 2 * LRU_WIDTH

    def permute_heads(arr, axis, start):
        take = lambda a, b: lax.slice_in_dim(arr, a, b, axis=axis)
        heads = [take(start + h * HEAD_DIM, start + (h + 1) * HEAD_DIM) for h in head_perm]
        return jnp.concatenate([take(0, start)] + heads + [take(start + Q_WIDTH, arr.shape[axis])], axis=axis)

    sinks_perm = attn_sinks[l]

    wr = jnp.tile(w_router[l], (1, TOP_K))
    wr_hi = wr.astype(BF16)
    weights = dict(
        norm1_g=norm1_g[l][None, :], w_in=permute_heads(w_in[l], 1, o2).astype(BF16),
        b_in=permute_heads(b_in[l], 0, o2)[None, :],
        conv_w=conv_w[l], conv_b=conv_b[l][None, :],
        ga=_block_diag_halves(lru_wa[l]), gx=_block_diag_halves(lru_wx[l]),
        lru_ba=lru_ba[l][None, :], lru_bx=lru_bx[l][None, :], lam=lru_lambda[l][None, :],
        w_out=permute_heads(w_out[l], 0, LRU_WIDTH).astype(BF16), b_out=b_out[l][None, :], norm2_g=norm2_g[l][None, :],
        wr_hi=wr_hi, wr_lo=(wr - wr_hi.astype(F32)).astype(BF16),
        b_router=jnp.tile(b_router[l], TOP_K)[None, :],
    )
    wlist = [weights[n] for n in WEIGHT_NAMES]

    mod_p, mod_s = _ada(jnp.concatenate([c_prompt, c_sample], axis=0), bp, w_ada[l], b_ada[l][None, :])
    mod_p = mod_p.reshape(bp, 6, D_MODEL)

    cos_p, sin_p = _rope_tables(jnp.arange(seq, dtype=jnp.int32))
    cos_s, sin_s = _rope_tables(PAST_LEN + jnp.arange(tdec, dtype=jnp.int32))
    cos_s = jnp.tile(cos_s, (SAMPLE_BT, 1))
    sin_s = jnp.tile(sin_s, (SAMPLE_BT, 1))
    h0_rows = jnp.pad(state_lru_h[l][:, None, :], ((0, 0), (0, tdec - 1), (0, 0))).reshape(n_sample, LRU_WIDTH)
    cprev_rows = jnp.pad(state_conv[l], ((0, 0), (0, tdec - (CONV_W - 1)), (0, 0))).reshape(n_sample, LRU_WIDTH)
    ck = cache_win_k[l].reshape(bd, WINDOW, KV_WIDTH)
    cv = cache_win_v[l].reshape(bd, WINDOW, KV_WIDTH)
    npt = n_prompt // SEQ_TILE
    nst = n_sample // SEQ_TILE
    mix_rows = MIX_TILES * SEQ_TILE
    steps_per_seq = seq // mix_rows
    wspecs = [_resident_spec(a) for a in wlist]

    rows_p = lambda width: pl.BlockSpec((mix_rows, width), lambda i: (i, 0))
    tail_p = lambda rows, width: pl.BlockSpec((None, rows, width), lambda i: (i // steps_per_seq, 0, 0))
    (x1_p, h2_p, route_p, cnt_p, hlast_p, ulast_p, klast_p, vlast_p) = pl.pallas_call(
        functools.partial(_prompt_kernel, steps_per_seq),
        grid=(n_prompt // mix_rows,),
        in_specs=[rows_p(D_MODEL),
                  pl.BlockSpec((None, 6, D_MODEL), lambda i: (i // steps_per_seq, 0, 0)),
                  pl.BlockSpec((mix_rows, LANES), lambda i: (i % steps_per_seq, 0)),
                  pl.BlockSpec((mix_rows, LANES), lambda i: (i % steps_per_seq, 0)),
                  pl.BlockSpec(memory_space=pltpu.SMEM)] + wspecs,
        out_specs=(rows_p(D_MODEL), rows_p(D_MODEL), rows_p(LANES),
                   pl.BlockSpec((MIX_TILES, SUBLANES, LANES), lambda i: (i, 0, 0)),
                   tail_p(SUBLANES, LRU_WIDTH), tail_p(SUBLANES, LRU_WIDTH),
                   tail_p(WINDOW, KV_WIDTH), tail_p(WINDOW, KV_WIDTH)),
        out_shape=(
            jax.ShapeDtypeStruct((n_prompt, D_MODEL), F32),
            jax.ShapeDtypeStruct((n_prompt, D_MODEL), F32),
            jax.ShapeDtypeStruct((n_prompt, LANES), F32),
            jax.ShapeDtypeStruct((npt, SUBLANES, LANES), F32),
            jax.ShapeDtypeStruct((bp, SUBLANES, LRU_WIDTH), F32),
            jax.ShapeDtypeStruct((bp, SUBLANES, LRU_WIDTH), F32),
            jax.ShapeDtypeStruct((bp, WINDOW, KV_WIDTH), F32),
            jax.ShapeDtypeStruct((bp, WINDOW, KV_WIDTH), F32),
        ),
        scratch_shapes=[pltpu.VMEM((SUBLANES, LRU_WIDTH), F32), pltpu.VMEM((SUBLANES, LRU_WIDTH), F32),
                        pltpu.VMEM((WINDOW, KV_WIDTH), F32), pltpu.VMEM((WINDOW, KV_WIDTH), F32)],
        compiler_params=pltpu.CompilerParams(dimension_semantics=("arbitrary",), vmem_limit_bytes=VMEM_LIMIT),
        name="prompt_mixer",
    )(x_prompt.reshape(n_prompt, D_MODEL), mod_p, cos_p, sin_p, sinks_perm, *wlist)

    rows_s = lambda width: pl.BlockSpec((SEQ_TILE, width), lambda i: (i, 0))
    cache_spec = pl.BlockSpec((SAMPLE_BT, WINDOW, KV_WIDTH), lambda i: (i, 0, 0))
    (x1_s, h2_s, route_s, cnt_s, g2_rows, hs_s, u_s, s_k, s_v) = pl.pallas_call(
        _sample_kernel,
        grid=(nst,),
        in_specs=[rows_s(D_MODEL),
                  pl.BlockSpec((6, SAMPLE_BT, D_MODEL), lambda i: (0, i, 0)),
                  pl.BlockSpec((SEQ_TILE, LANES), lambda i: (0, 0)),
                  pl.BlockSpec((SEQ_TILE, LANES), lambda i: (0, 0)),
                  pl.BlockSpec(memory_space=pltpu.SMEM),
                  rows_s(LRU_WIDTH), rows_s(LRU_WIDTH), cache_spec, cache_spec] + wspecs,
        out_specs=(rows_s(D_MODEL), rows_s(D_MODEL), rows_s(LANES),
                   pl.BlockSpec((None, SUBLANES, LANES), lambda i: (i, 0, 0)),
                   rows_s(D_MODEL), rows_s(LRU_WIDTH), rows_s(LRU_WIDTH), cache_spec, cache_spec),
        out_shape=(
            jax.ShapeDtypeStruct((n_sample, D_MODEL), F32), jax.ShapeDtypeStruct((n_sample, D_MODEL), F32),
            jax.ShapeDtypeStruct((n_sample, LANES), F32), jax.ShapeDtypeStruct((nst, SUBLANES, LANES), F32),
            jax.ShapeDtypeStruct((n_sample, D_MODEL), F32),
            jax.ShapeDtypeStruct((n_sample, LRU_WIDTH), F32),
            jax.ShapeDtypeStruct((n_sample, LRU_WIDTH), F32),
            jax.ShapeDtypeStruct((bd, WINDOW, KV_WIDTH), F32),
            jax.ShapeDtypeStruct((bd, WINDOW, KV_WIDTH), F32),
        ),
        compiler_params=pltpu.CompilerParams(dimension_semantics=("arbitrary",), vmem_limit_bytes=VMEM_LIMIT),
        name="sample_mixer",
    )(x_sample.reshape(n_sample, D_MODEL), mod_s, cos_s, sin_s, sinks_perm, h0_rows, cprev_rows, ck, cv, *wlist)
    tile_cnt = jnp.concatenate([cnt_p, cnt_s], axis=0)

    n_tiles = npt + nst
    n_assign = n_tok * TOP_K
    max_rows = n_assign + n_tiles * N_EXPERTS * (SUBLANES - 1) + N_EXPERTS * (MOE_TM - 1)
    n_blocks = -(-max_rows // MOE_TM)
    cnt = tile_cnt[:, 0, :N_EXPERTS].astype(jnp.int32)
    cnt8 = (cnt + SUBLANES - 1) // SUBLANES * SUBLANES
    counts = jnp.sum(cnt8, axis=0)
    pcounts = (counts + MOE_TM - 1) // MOE_TM * MOE_TM
    pend = jnp.cumsum(pcounts)
    pstart = pend - pcounts
    tile_off = pstart[None, :] + jnp.cumsum(cnt8, axis=0) - cnt8
    tot8 = jnp.sum(cnt8, axis=1)
    meta = jnp.concatenate([pstart + counts, pend, pend[-1:] // MOE_TM]).astype(jnp.int32)
    cnt8_flat = cnt8.reshape(-1)
    off_flat = tile_off.reshape(-1).astype(jnp.int32)

    xs = _dispatch(cnt8_flat, off_flat, tot8, meta, h2_p, h2_s, route_p, route_s, n_blocks * MOE_TM)

    big, small = MOE_CHUNK_CODES[0], MOE_CHUNK_CODES[1:]
    n_tm = pcounts // MOE_TM
    n_big = n_tm // big
    present = [(n_tm // b) % 2 for b in small]
    n_ch = n_big + sum(present)
    ch_end = jnp.cumsum(n_ch)
    ch_start = ch_end - n_ch
    n_chunks = n_blocks // big + len(small) * N_EXPERTS
    ci = jnp.arange(n_chunks, dtype=jnp.int32)
    owner = jnp.logical_and(ch_start[None, :] <= ci[:, None], ci[:, None] < ch_end[None, :])
    pick = lambda v: jnp.sum(jnp.where(owner, v[None, :], 0), axis=1)
    live = ci < ch_end[-1]
    local = ci - pick(ch_start)
    chunk_exp = jnp.where(live, pick(jnp.arange(N_EXPERTS, dtype=jnp.int32)), N_EXPERTS - 1).astype(jnp.int32)
    big_c = pick(n_big)
    code = jnp.where(local < big_c, big, 0)
    blocks_before = jnp.where(local < big_c, local * big, 0)
    order = big_c
    blocks = big_c * big
    for b, has in zip(small, present):
        has_c = pick(has)
        is_b = jnp.logical_and(has_c == 1, local == order)
        code = jnp.where(is_b, b, code)
        blocks_before = jnp.where(is_b, blocks, blocks_before)
        order = order + has_c
        blocks = blocks + has_c * b
    chunk_row = jnp.where(live, pick(pstart) + blocks_before * MOE_TM, 0).astype(jnp.int32)
    chunk_size = jnp.where(live, code, 0).astype(jnp.int32)
    has_rows = n_ch > 0
    e_ar = jnp.arange(N_EXPERTS, dtype=jnp.int32)
    w_slot = (jnp.cumsum(has_rows.astype(jnp.int32)) - 1) % 2
    later = jnp.logical_and(has_rows[None, :], e_ar[None, :] > e_ar[:, None])
    nxt = jnp.min(jnp.where(later, e_ar[None, :], N_EXPERTS), axis=1)
    nxt = jnp.where(nxt == N_EXPERTS, -1, nxt)
    weight_tbl = jnp.concatenate([jnp.logical_and(live, local == 0).astype(jnp.int32),
                                  pick(w_slot), jnp.where(live, pick(nxt), -1)]).astype(jnp.int32)
    moe_meta = jnp.concatenate([meta, ch_end[-1:].astype(jnp.int32)])
    ys = _moe(chunk_row, chunk_size, chunk_exp, weight_tbl, moe_meta, xs, w1[l], b1[l][:, None, :], w2[l], b2[l][:, None, :])

    y_p, y_s = _combine(cnt8_flat, off_flat, tot8, ys, route_p, route_s, x1_p, x1_s, mod_p, g2_rows,
                        final_g[None, :], seq // SEQ_TILE)

    y_prompt = y_p.reshape(bp, seq, D_MODEL)
    y_sample = y_s.reshape(bd, tdec, D_MODEL)
    p_h = hlast_p[:, SUBLANES - 1, :][None]
    p_c = ulast_p[:, SUBLANES - (CONV_W - 1):, :][None]
    p_k = klast_p.reshape(1, bp, WINDOW, N_KV_HEADS, HEAD_DIM)
    p_v = vlast_p.reshape(1, bp, WINDOW, N_KV_HEADS, HEAD_DIM)
    s_h = hs_s.reshape(bd, tdec, LRU_WIDTH)[:, tdec - 1, :][None]
    s_c = u_s.reshape(bd, tdec, LRU_WIDTH)[:, tdec - (CONV_W - 1):, :][None]
    s_kk = s_k.reshape(1, bd, WINDOW, N_KV_HEADS, HEAD_DIM)
    s_vv = s_v.reshape(1, bd, WINDOW, N_KV_HEADS, HEAD_DIM)
    return (y_prompt, y_sample, p_h, p_c, p_k, p_v, s_h, s_c, s_kk, s_vv)
```

```python
import functools

import jax
import jax.numpy as jnp
from jax import lax
from jax.experimental import pallas as pl
from jax.experimental.pallas import tpu as pltpu

F32 = jnp.float32
BF16 = jnp.bfloat16

D_MODEL = 1024
LRU_WIDTH = 512
LRU_BLOCKS = 8
LRU_BLOCK_W = LRU_WIDTH // LRU_BLOCKS
CONV_W = 4
LRU_C = 8.0
HEAD_DIM = 64
N_HEADS = 8
N_KV_HEADS = 2
GROUP = N_HEADS // N_KV_HEADS
WINDOW = 128
ROPE_THETA = 10000.0
N_EXPERTS = 32
TOP_K = 4
D_FF = D_MODEL
SWIGLU_LIMIT = 7.0
SWIGLU_ALPHA = 1.702
NORM_EPS = 1e-5
PAST_LEN = 8192
Q_WIDTH = N_HEADS * HEAD_DIM
KV_WIDTH = N_KV_HEADS * HEAD_DIM
IN_WIDTH = 2 * LRU_WIDTH + Q_WIDTH + 2 * KV_WIDTH

LANES = 128
SUBLANES = 8
SEQ_TILE = 256
MIX_TILES = 2
SAMPLE_BT = 32
MOE_TM = 256
MOE_CHUNK_CODES = (4, 2, 1)
MOE_CH = MOE_CHUNK_CODES[0] * MOE_TM
NEG_BIG = -1e30
VMEM_LIMIT = 56 * 1024 * 1024


def _rms(x, g):
    return x * lax.rsqrt(jnp.mean(x * x, axis=-1, keepdims=True) + NORM_EPS) * g


def _rms_mod(x, gain, shift):
    return x * lax.rsqrt(jnp.mean(x * x, axis=-1, keepdims=True) + NORM_EPS) * gain + shift


def _group_scan(a, b):
    rows, width = a.shape
    groups = rows // SUBLANES
    a3 = a.reshape(groups, SUBLANES, width)
    b3 = b.reshape(groups, SUBLANES, width)
    t = lax.broadcasted_iota(jnp.int32, (1, SUBLANES, 1), 1)
    d = 1
    while d < SUBLANES:
        keep = t >= d
        a_s = jnp.where(keep, pltpu.roll(a3, d, axis=1), 1.0)
        b_s = jnp.where(keep, pltpu.roll(b3, d, axis=1), 0.0)
        b3 = a3 * b_s + b3
        a3 = a3 * a_s
        d *= 2
    return a3.reshape(rows, width), b3.reshape(rows, width)


def _chain_groups(a_grp, b_grp, h_in):
    rows = a_grp.shape[0]
    out = []
    carry = h_in
    for g in range(rows // SUBLANES):
        sl = slice(g * SUBLANES, (g + 1) * SUBLANES)
        hg = b_grp[sl] + a_grp[sl] * carry
        out.append(hg)
        carry = hg[SUBLANES - 1:SUBLANES]
    return jnp.concatenate(out, axis=0)


def _rope128(x, cos, sin_signed, first_half):
    sw = jnp.where(first_half, pltpu.roll(x, LANES - HEAD_DIM // 2, axis=1), pltpu.roll(x, HEAD_DIM // 2, axis=1))
    return x * cos + sw * sin_signed


def _softplus(x):
    return jnp.maximum(x, 0.0) + jnp.log1p(jnp.exp(-jnp.abs(x)))


def _lru_coeffs(uc, w, first_pos_mask):
    ub = uc.astype(BF16)
    half = LRU_WIDTH // 2
    ra = jnp.concatenate([jnp.dot(ub[:, :half], w['ga'][0], preferred_element_type=F32),
                          jnp.dot(ub[:, half:], w['ga'][1], preferred_element_type=F32)], axis=1)
    rx = jnp.concatenate([jnp.dot(ub[:, :half], w['gx'][0], preferred_element_type=F32),
                          jnp.dot(ub[:, half:], w['gx'][1], preferred_element_type=F32)], axis=1)
    r = jax.nn.sigmoid(ra + w['lru_ba'][...])
    i = jax.nn.sigmoid(rx + w['lru_bx'][...])
    log_a = -LRU_C * r * _softplus(-w['lam'][...])
    a = jnp.exp(log_a)
    om = 1.0 - a * a
    mult = jnp.where(om > 0.0, om * lax.rsqrt(om), 0.0)
    if first_pos_mask is not None:
        mult = jnp.where(first_pos_mask, 1.0, mult)
    return a, mult * i * uc


def _conv_taps(u, s1, s2, s3, w):
    cw = w['conv_w']
    return w['conv_b'][...] + s3 * cw[0:1, :] + s2 * cw[1:2, :] + s1 * cw[2:3, :] + u * cw[3:4, :]


def _in_proj(x, mod, w):
    sh1, sc1 = mod
    h = _rms_mod(x, w['norm1_g'][...] * (1.0 + sc1), sh1)
    return jnp.dot(h.astype(BF16), w['w_in'][...], preferred_element_type=F32) + w['b_in'][...]


def _post_mix(x, mix, mod, w):
    g1, sh2, sc2 = mod
    x1 = x + g1 * (jnp.dot(mix.astype(BF16), w['w_out'][...], preferred_element_type=F32) + w['b_out'][...])
    h2 = _rms_mod(x1, w['norm2_g'][...] * (1.0 + sc2), sh2)
    h2_hi = h2.astype(BF16)
    h2_lo = (h2 - h2_hi.astype(F32)).astype(BF16)
    logits = (jnp.dot(h2_hi, w['wr_hi'][...], preferred_element_type=F32)
              + jnp.dot(h2_lo, w['wr_hi'][...], preferred_element_type=F32)
              + jnp.dot(h2_hi, w['wr_lo'][...], preferred_element_type=F32)) + w['b_router'][...]
    return x1, h2_hi, logits


ROUTE_E, ROUTE_G, ROUTE_R = 0, TOP_K, 2 * TOP_K


def _lane_roll1(v, shift):
    return pltpu.roll(jnp.broadcast_to(v, (SUBLANES, LANES)), shift, axis=1)[0:1]


def _route_tile(lg):
    rows = lg.shape[0]
    lane = lax.broadcasted_iota(jnp.int32, (1, LANES), 1)
    e_of = lane % N_EXPERTS
    grp = lane // N_EXPERTS
    e_id = e_of.astype(F32)
    onehot = jnp.zeros((rows, LANES), F32)
    vals, ids = [], []
    for k in range(TOP_K):
        m = jnp.max(lg, axis=1, keepdims=True)
        idx = jnp.min(jnp.where(lg == m, e_id, float(N_EXPERTS)), axis=1, keepdims=True)
        sel = e_id == idx
        lg = jnp.where(sel, -jnp.inf, lg)
        onehot = jnp.where(jnp.logical_and(sel, grp == k), 1.0, onehot)
        vals.append(m)
        ids.append(idx)
    ex = [jnp.exp(v - vals[0]) for v in vals]
    denom = ex[0] + ex[1] + ex[2] + ex[3]

    r_i = lax.broadcasted_iota(jnp.int32, (rows, rows), 0)
    c_i = lax.broadcasted_iota(jnp.int32, (rows, rows), 1)
    strict_lower = jnp.where(r_i > c_i, 1.0, 0.0).astype(BF16)
    prefix = jnp.dot(strict_lower, onehot.astype(BF16), preferred_element_type=F32)
    cnt = jnp.sum(onehot, axis=0, keepdims=True)
    base = jnp.zeros((1, LANES), F32)
    tot = cnt
    for s in range(1, TOP_K):
        rolled = _lane_roll1(cnt, s * N_EXPERTS)
        base = base + jnp.where(lane >= s * N_EXPERTS, rolled, 0.0)
        tot = tot + rolled
    pad_cnt = jnp.floor((tot + (SUBLANES - 1.0)) * (1.0 / SUBLANES)) * SUBLANES
    inc = pad_cnt
    d = 1
    while d < N_EXPERTS:
        inc = inc + jnp.where(e_of >= d, _lane_roll1(inc, d), 0.0)
        d *= 2
    strip_start = inc - pad_cnt
    ranked = onehot * (prefix + base + strip_start)

    route = jnp.zeros((rows, LANES), F32)
    for k in range(TOP_K):
        rank_k = jnp.sum(jnp.where(grp == k, ranked, 0.0), axis=1, keepdims=True)
        route = jnp.where(lane == ROUTE_E + k, ids[k], route)
        route = jnp.where(lane == ROUTE_G + k, ex[k] / denom, route)
        route = jnp.where(lane == ROUTE_R + k, rank_k, route)
    return route, tot


def _softmax_with_sink(s, sink_col):
    m = jnp.maximum(jnp.max(s, axis=-1, keepdims=True), sink_col)
    p = jnp.exp(s - m)
    denom = jnp.sum(p, axis=-1, keepdims=True) + jnp.exp(sink_col - m)
    return p, denom


WEIGHT_NAMES = ('norm1_g', 'w_in', 'b_in', 'conv_w', 'conv_b', 'ga', 'gx', 'lru_ba', 'lru_bx', 'lam',
                'w_out', 'b_out', 'norm2_g', 'wr_hi', 'wr_lo', 'b_router')


def _split_bf16(x):
    hi = x.astype(BF16)
    return hi, (x - hi.astype(F32)).astype(BF16)


def _ada_kernel(n_prompt_rows, c_ref, w_ref, b_ref, op_ref, os_ref):
    c = c_ref[...]
    s_hi, s_lo = _split_bf16(c * jax.nn.sigmoid(c))
    w_hi, w_lo = _split_bf16(w_ref[...])
    mod = (jnp.dot(s_hi, w_hi, preferred_element_type=F32) + jnp.dot(s_lo, w_hi, preferred_element_type=F32)
           + jnp.dot(s_hi, w_lo, preferred_element_type=F32)) + b_ref[...]
    op_ref[...] = mod[:n_prompt_rows]
    os_ref[...] = mod[n_prompt_rows:]


def _ada(c_all, n_prompt_rows, w_ada, b_ada):
    rows = c_all.shape[0]
    n_s = rows - n_prompt_rows
    assert n_prompt_rows % SUBLANES == 0
    return pl.pallas_call(
        functools.partial(_ada_kernel, n_prompt_rows),
        grid=(6,),
        in_specs=[pl.BlockSpec((rows, D_MODEL), lambda i: (0, 0)),
                  pl.BlockSpec((D_MODEL, D_MODEL), lambda i: (0, i)),
                  pl.BlockSpec((1, D_MODEL), lambda i: (0, i))],
        out_specs=(pl.BlockSpec((n_prompt_rows, D_MODEL), lambda i: (0, i)),
                   pl.BlockSpec((None, n_s, D_MODEL), lambda i: (i, 0, 0))),
        out_shape=(jax.ShapeDtypeStruct((n_prompt_rows, 6 * D_MODEL), F32),
                   jax.ShapeDtypeStruct((6, n_s, D_MODEL), F32)),
        compiler_params=pltpu.CompilerParams(dimension_semantics=("arbitrary",), vmem_limit_bytes=VMEM_LIMIT),
        name="ada",
    )(c_all, w_ada, b_ada)


def _prompt_body(seq_start, x_ref, mod_ref, cos_ref, sin_ref, sinks_ref, w, x1_ref, h2_ref, lg_ref,
                 hlast_ref, ulast_ref, klast_ref, vlast_ref, conv_c, h_c, k_c, v_c):
    ts = SEQ_TILE

    if seq_start is not None:
        @pl.when(seq_start)
        def _():
            conv_c[...] = jnp.zeros_like(conv_c)
            h_c[...] = jnp.zeros_like(h_c)
            k_c[...] = jnp.zeros_like(k_c)
            v_c[...] = jnp.zeros_like(v_c)

    x = x_ref[...]
    mod = mod_ref[...]
    proj = _in_proj(x, (mod[0:1], mod[1:2]), w)
    u = proj[:, :LRU_WIDTH]
    gate = proj[:, LRU_WIDTH:2 * LRU_WIDTH]
    o2 = 2 * LRU_WIDTH

    rowid = lax.broadcasted_iota(jnp.int32, (ts, 1), 0)
    u_ext = jnp.concatenate([conv_c[...], u], axis=0)
    s1, s2, s3 = (pltpu.roll(u_ext, d, axis=0)[SUBLANES:] for d in (1, 2, 3))
    uc = _conv_taps(u, s1, s2, s3, w)
    conv_c[...] = u[ts - SUBLANES:]
    ulast_ref[...] = u[ts - SUBLANES:]

    first_pos = None if seq_start is None else jnp.logical_and(rowid == 0, seq_start)
    a, bt = _lru_coeffs(uc, w, first_pos)
    hs = _chain_groups(*_group_scan(a, bt), h_c[0:1, :])
    h_tail = hs[ts - SUBLANES:]
    h_c[...] = jnp.broadcast_to(h_tail[SUBLANES - 1:SUBLANES, :], h_c.shape)
    hlast_ref[...] = h_tail
    lru_out = hs * jax.nn.gelu(gate)

    cos = cos_ref[...]
    sin = sin_ref[...]
    lane = lax.broadcasted_iota(jnp.int32, (1, LANES), 1)
    first_half = (lane % HEAD_DIM) < (HEAD_DIM // 2)
    qcols = [_rope128(proj[:, o2 + c * LANES:o2 + (c + 1) * LANES], cos, sin, first_half) * (HEAD_DIM ** -0.5)
             for c in range(4)]
    k = _rope128(proj[:, o2 + Q_WIDTH:o2 + Q_WIDTH + KV_WIDTH], cos, sin, first_half)
    v = proj[:, o2 + Q_WIDTH + KV_WIDTH:]
    k_ext = jnp.concatenate([k_c[...], k], axis=0).astype(BF16)
    v_ext = jnp.concatenate([v_c[...], v], axis=0).astype(BF16)
    k_c[...] = k[ts - WINDOW:]
    v_c[...] = v[ts - WINDOW:]
    klast_ref[...] = k[ts - WINDOW:]
    vlast_ref[...] = v[ts - WINDOW:]

    qi = lax.broadcasted_iota(jnp.int32, (WINDOW, 2 * WINDOW), 0)
    kj = lax.broadcasted_iota(jnp.int32, (WINDOW, 2 * WINDOW), 1)
    band = jnp.logical_and(kj > qi, kj <= qi + WINDOW)
    lane_lo = lane < HEAD_DIM
    grow = lax.broadcasted_iota(jnp.int32, (GROUP * WINDOW, 1), 0) // WINDOW
    attn_cols = [[] for _ in range(4)]
    for blk in range(ts // WINDOW):
        if seq_start is None or blk > 0:
            mask = band
        else:
            mask = jnp.logical_and(band, jnp.logical_or(kj >= WINDOW, jnp.logical_not(seq_start)))
        mask4 = jnp.concatenate([mask] * GROUP, axis=0)
        kb = k_ext[blk * WINDOW:(blk + 2) * WINDOW]
        vb = v_ext[blk * WINDOW:(blk + 2) * WINDOW]
        outs = []
        for kv in range(N_KV_HEADS):
            sel = lane_lo if kv == 0 else jnp.logical_not(lane_lo)
            qs = jnp.concatenate(
                [jnp.where(sel, qc[blk * WINDOW:(blk + 1) * WINDOW], 0.0) for qc in qcols], axis=0).astype(BF16)
            s = lax.dot_general(qs, kb, (((1,), (1,)), ((), ())), preferred_element_type=F32)
            s = jnp.where(mask4, s, NEG_BIG)
            sink_col = jnp.zeros((GROUP * WINDOW, 1), F32)
            for g in range(GROUP):
                sink_col = jnp.where(grow == g, sinks_ref[kv * GROUP + g], sink_col)
            p, denom = _softmax_with_sink(s, sink_col)
            outs.append(jnp.dot(p.astype(BF16), vb, preferred_element_type=F32) / denom)
        for c in range(4):
            attn_cols[c].append(jnp.where(lane_lo, outs[0][c * WINDOW:(c + 1) * WINDOW],
                                          outs[1][c * WINDOW:(c + 1) * WINDOW]))
    attn = jnp.concatenate([jnp.concatenate(cols, axis=0) for cols in attn_cols], axis=1)

    mix = jnp.concatenate([lru_out, attn], axis=1)
    x1, h2, logits = _post_mix(x, mix, (mod[2:3], mod[3:4], mod[4:5]), w)
    x1_ref[...] = x1
    h2_ref[...] = h2
    route, tot = _route_tile(logits)
    lg_ref[0][...] = route
    lg_ref[1][...] = jnp.broadcast_to(tot, lg_ref[1].shape)


def _expand_rows(m, t):
    b, wd = m.shape
    return jnp.broadcast_to(m[:, None, :], (b, t, wd)).reshape(b * t, wd)


def _sample_body(x_ref, mod_ref, cos_ref, sin_ref, sinks_ref, h0_ref, cprev_ref, ck_ref, cv_ref, w,
                 x1_ref, h2_ref, lg_ref, g2_ref, hs_ref, u_ref, ko_ref, vo_ref):
    bt_, t = SAMPLE_BT, SUBLANES
    rows = bt_ * t

    x = x_ref[...]
    mods = [_expand_rows(mod_ref[i], t) for i in range(6)]
    proj = _in_proj(x, (mods[0], mods[1]), w)
    u = proj[:, :LRU_WIDTH]
    gate = proj[:, LRU_WIDTH:2 * LRU_WIDTH]
    o2 = 2 * LRU_WIDTH
    u_ref[...] = u

    rowid = lax.broadcasted_iota(jnp.int32, (rows, 1), 0) % t
    cprev = cprev_ref[...]
    taps = []
    for d in (1, 2, 3):
        taps.append(jnp.where(rowid >= d, pltpu.roll(u, d, axis=0),
                              pltpu.roll(cprev, (d - (CONV_W - 1)) % rows, axis=0)))
    uc = _conv_taps(u, taps[0], taps[1], taps[2], w)

    a, bt = _lru_coeffs(uc, w, None)
    bt = bt + a * h0_ref[...]
    _, hs = _group_scan(a, bt)
    hs_ref[...] = hs
    lru_out = hs * jax.nn.gelu(gate)

    cos = cos_ref[...]
    sin = sin_ref[...]
    lane = lax.broadcasted_iota(jnp.int32, (1, LANES), 1)
    first_half = (lane % HEAD_DIM) < (HEAD_DIM // 2)
    qcols = [_rope128(proj[:, o2 + c * LANES:o2 + (c + 1) * LANES], cos, sin, first_half) * (HEAD_DIM ** -0.5)
             for c in range(4)]
    k = _rope128(proj[:, o2 + Q_WIDTH:o2 + Q_WIDTH + KV_WIDTH], cos, sin, first_half)
    v = proj[:, o2 + Q_WIDTH + KV_WIDTH:]
    k3 = k.reshape(bt_, t, KV_WIDTH)
    v3 = v.reshape(bt_, t, KV_WIDTH)
    ck = ck_ref[...]
    cv = cv_ref[...]
    ko_ref[:, :WINDOW - t, :] = ck[:, t:, :]
    ko_ref[:, WINDOW - t:, :] = k3
    vo_ref[:, :WINDOW - t, :] = cv[:, t:, :]
    vo_ref[:, WINDOW - t:, :] = v3

    ckb, cvb, k3b, v3b = ck.astype(BF16), cv.astype(BF16), k3.astype(BF16), v3.astype(BF16)
    lane_lo = lane < HEAD_DIM
    gq = GROUP * t
    tq = lax.broadcasted_iota(jnp.int32, (1, gq, 1), 1) % t
    mask_c = lax.broadcasted_iota(jnp.int32, (1, gq, WINDOW), 2) > tq
    mask_n = lax.broadcasted_iota(jnp.int32, (1, gq, t), 2) <= tq
    grow = lax.broadcasted_iota(jnp.int32, (1, gq, 1), 1) // t
    bdims = (((2,), (2,)), ((0,), (0,)))
    pdims = (((2,), (1,)), ((0,), (0,)))
    outs = []
    for kv in range(N_KV_HEADS):
        sel = lane_lo if kv == 0 else jnp.logical_not(lane_lo)
        q3 = jnp.concatenate([jnp.where(sel, qc, 0.0).reshape(bt_, t, LANES) for qc in qcols], axis=1).astype(BF16)
        sc = lax.dot_general(q3, ckb, bdims, preferred_element_type=F32)
        sn = lax.dot_general(q3, k3b, bdims, preferred_element_type=F32)
        sc = jnp.where(mask_c, sc, NEG_BIG)
        sn = jnp.where(mask_n, sn, NEG_BIG)
        sink_col = jnp.zeros((1, gq, 1), F32)
        for g in range(GROUP):
            sink_col = jnp.where(grow == g, sinks_ref[kv * GROUP + g], sink_col)
        m = jnp.maximum(jnp.maximum(jnp.max(sc, axis=-1, keepdims=True), jnp.max(sn, axis=-1, keepdims=True)),
                        sink_col)
        pc = jnp.exp(sc - m)
        pn = jnp.exp(sn - m)
        denom = jnp.sum(pc, axis=-1, keepdims=True) + jnp.sum(pn, axis=-1, keepdims=True) + jnp.exp(sink_col - m)
        o = (lax.dot_general(pc.astype(BF16), cvb, pdims, preferred_element_type=F32)
             + lax.dot_general(pn.astype(BF16), v3b, pdims, preferred_element_type=F32)) / denom
        outs.append(o)
    attn = jnp.concatenate(
        [jnp.where(lane_lo, outs[0][:, c * t:(c + 1) * t, :], outs[1][:, c * t:(c + 1) * t, :]).reshape(rows, LANES)
         for c in range(4)], axis=1)

    mix = jnp.concatenate([lru_out, attn], axis=1)
    x1, h2, logits = _post_mix(x, mix, (mods[2], mods[3], mods[4]), w)
    x1_ref[...] = x1
    h2_ref[...] = h2
    route, tot = _route_tile(logits)
    lg_ref[0][...] = route
    lg_ref[1][...] = jnp.broadcast_to(tot, lg_ref[1].shape)
    g2_ref[...] = mods[5]


def _prompt_kernel(steps_per_seq, x_ref, mod_ref, cos_ref, sin_ref, sinks_ref, *rest):
    nw = len(WEIGHT_NAMES)
    w = dict(zip(WEIGHT_NAMES, rest[:nw]))
    (x1_ref, h2_ref, route_ref, cnt_ref, hlast_ref, ulast_ref, klast_ref, vlast_ref,
     conv_c, h_c, k_c, v_c) = rest[nw:]
    seq_start = pl.program_id(0) % steps_per_seq == 0
    for sub in range(MIX_TILES):
        rows = pl.ds(sub * SEQ_TILE, SEQ_TILE)
        _prompt_body(seq_start if sub == 0 else None, x_ref.at[rows], mod_ref, cos_ref.at[rows], sin_ref.at[rows],
                     sinks_ref, w, x1_ref.at[rows], h2_ref.at[rows], (route_ref.at[rows], cnt_ref.at[sub]),
                     hlast_ref, ulast_ref, klast_ref, vlast_ref, conv_c, h_c, k_c, v_c)


def _sample_kernel(x_ref, mod_ref, cos_ref, sin_ref, sinks_ref, h0_ref, cprev_ref, ck_ref, cv_ref, *rest):
    nw = len(WEIGHT_NAMES)
    w = dict(zip(WEIGHT_NAMES, rest[:nw]))
    x1_ref, h2_ref, route_ref, cnt_ref, g2_ref, hs_ref, u_ref, ko_ref, vo_ref = rest[nw:]
    _sample_body(x_ref, mod_ref, cos_ref, sin_ref, sinks_ref, h0_ref, cprev_ref, ck_ref, cv_ref, w,
                 x1_ref, h2_ref, (route_ref, cnt_ref), g2_ref, hs_ref, u_ref, ko_ref, vo_ref)


PACK_W = D_MODEL // 2
PACKED = jnp.int32


def _pack_bf16_pairs(x):
    return pltpu.pack_elementwise([x[:, :PACK_W], x[:, PACK_W:]], packed_dtype=BF16)


def _unpack_bf16_pairs(w):
    return tuple(pltpu.unpack_elementwise(w, index=k, packed_dtype=BF16, unpacked_dtype=F32).astype(BF16)
                 for k in range(2))


STRIP_SIZES = tuple(SUBLANES << b for b in range(6))
STRIP_LARGE = 64
SORT_ROWS = SEQ_TILE * TOP_K + N_EXPERTS * SUBLANES
TILE_WAIT_SIZES = tuple(SUBLANES << b for b in range(8))


def _for_strips(cnt_ref, off_ref, tile, buf_slot, hbm, sem, to_hbm, act):
    def e_body(e, local):
        n = cnt_ref[tile * N_EXPERTS + e]
        glob = off_ref[tile * N_EXPERTS + e]

        def pieces(sizes, done):
            for p in sizes:
                piece = n & p
                lo = pl.ds(pl.multiple_of(local + done, SUBLANES), p)
                gl = pl.ds(pl.multiple_of(glob + done, SUBLANES), p)

                @pl.when(piece != 0)
                def _():
                    if to_hbm:
                        act(pltpu.make_async_copy(buf_slot.at[lo], hbm.at[gl], sem))
                    else:
                        act(pltpu.make_async_copy(hbm.at[gl], buf_slot.at[lo], sem))
                done = done + piece

        large = tuple(p for p in reversed(STRIP_SIZES) if p >= STRIP_LARGE)
        small = tuple(p for p in reversed(STRIP_SIZES) if p < STRIP_LARGE)
        n_large = n & (-STRIP_LARGE)

        @pl.when(n_large != 0)
        def _():
            pieces(large, 0)
        pieces(small, n_large)
        return local + n
    lax.fori_loop(0, N_EXPERTS, e_body, 0)


def _wait_tile_rows(total, buf_slot, hbm, sem, to_hbm):
    for p in TILE_WAIT_SIZES:
        @pl.when((total & p) != 0)
        def _():
            if to_hbm:
                pltpu.make_async_copy(buf_slot.at[pl.ds(0, p)], hbm.at[pl.ds(0, p)], sem).wait()
            else:
                pltpu.make_async_copy(hbm.at[pl.ds(0, p)], buf_slot.at[pl.ds(0, p)], sem).wait()


def _dispatch_kernel(n_prompt_tiles, cnt_ref, off_ref, tot_ref, meta_ref, h2p_ref, h2s_ref, routep_ref, routes_ref,
                     xs_hbm, sbuf, zblk, sem, zsem):
    i = pl.program_id(0)
    nb = pl.num_programs(0)
    slot = i % 2
    n_blocks = xs_hbm.shape[0] // MOE_TM

    is_prompt = i < n_prompt_tiles
    h2 = jnp.where(is_prompt, h2p_ref[...], h2s_ref[...])
    route_t = jnp.where(is_prompt, routep_ref[...], routes_ref[...]).T
    r_pos = lax.broadcasted_iota(jnp.int32, (SORT_ROWS, SEQ_TILE), 0).astype(F32)
    perm = jnp.zeros((SORT_ROWS, SEQ_TILE), F32)
    for k in range(TOP_K):
        perm = jnp.where(r_pos == route_t[ROUTE_R + k:ROUTE_R + k + 1, :], 1.0, perm)
    sbuf[slot] = _pack_bf16_pairs(jnp.dot(perm.astype(BF16), h2, preferred_element_type=F32))

    _for_strips(cnt_ref, off_ref, i, sbuf.at[slot], xs_hbm, sem.at[slot], True, lambda cp: cp.start())

    @pl.when(i > 0)
    def _():
        _wait_tile_rows(tot_ref[jnp.maximum(i - 1, 0)], sbuf.at[1 - slot], xs_hbm, sem.at[1 - slot], True)

    @pl.when(i == nb - 1)
    def _():
        _wait_tile_rows(tot_ref[i], sbuf.at[slot], xs_hbm, sem.at[slot], True)
        zblk[...] = jnp.zeros_like(zblk)

        def for_region_tails(act):
            def e_body(e, carry):
                start = meta_ref[e]
                n = meta_ref[N_EXPERTS + e] - start
                done = 0
                for p in reversed([q for q in STRIP_SIZES if q < MOE_TM]):
                    piece = n & p
                    rows = pl.ds(pl.multiple_of(start + done, SUBLANES), p)

                    @pl.when(piece != 0)
                    def _():
                        act(pltpu.make_async_copy(zblk.at[pl.ds(0, p)], xs_hbm.at[rows], zsem.at[0]))
                    done = done + piece
                return carry
            lax.fori_loop(0, N_EXPERTS, e_body, 0)

        def for_tail_blocks(act):
            def b_body(j, carry):
                act(pltpu.make_async_copy(zblk, xs_hbm.at[pl.ds(pl.multiple_of(j * MOE_TM, MOE_TM), MOE_TM)],
                                          zsem.at[0]))
                return carry
            lax.fori_loop(meta_ref[2 * N_EXPERTS], n_blocks, b_body, 0)

        for_region_tails(lambda cp: cp.start())
        for_tail_blocks(lambda cp: cp.start())
        for_region_tails(lambda cp: cp.wait())
        for_tail_blocks(lambda cp: cp.wait())


def _dispatch(cnt8, tile_off, tot8, meta, h2_p, h2_s, route_p, route_s, n_rows):
    npt = h2_p.shape[0] // SEQ_TILE
    p_tile = lambda i, *_: (jnp.minimum(i, npt - 1), 0)
    s_tile = lambda i, *_: (jnp.maximum(i - npt, 0), 0)
    nt = tot8.shape[0]
    tt = SEQ_TILE
    grid_spec = pltpu.PrefetchScalarGridSpec(
        num_scalar_prefetch=4,
        grid=(nt,),
        in_specs=[pl.BlockSpec((tt, D_MODEL), p_tile), pl.BlockSpec((tt, D_MODEL), s_tile),
                  pl.BlockSpec((tt, LANES), p_tile), pl.BlockSpec((tt, LANES), s_tile)],
        out_specs=pl.BlockSpec(memory_space=pl.ANY),
        scratch_shapes=[pltpu.VMEM((2, SORT_ROWS, PACK_W), PACKED), pltpu.VMEM((MOE_TM, PACK_W), PACKED),
                        pltpu.SemaphoreType.DMA((2,)), pltpu.SemaphoreType.DMA((1,))],
    )
    return pl.pallas_call(
        functools.partial(_dispatch_kernel, npt),
        grid_spec=grid_spec,
        out_shape=jax.ShapeDtypeStruct((n_rows, PACK_W), PACKED),
        compiler_params=pltpu.CompilerParams(dimension_semantics=("arbitrary",), vmem_limit_bytes=VMEM_LIMIT),
        name="dispatch",
    )(cnt8, tile_off, tot8, meta, h2_p, h2_s, route_p, route_s)


def _expert_mlp(words, w1b, b1_ref, w2b, b2_ref, act_ref):
    xb = jnp.concatenate(_unpack_bf16_pairs(words), axis=1)
    q = D_FF // 4
    for c in range(4):
        zg = jnp.dot(xb, w1b[:, c * q:(c + 1) * q], preferred_element_type=F32) + b1_ref[:, c * q:(c + 1) * q]
        zl = (jnp.dot(xb, w1b[:, D_FF + c * q:D_FF + (c + 1) * q], preferred_element_type=F32)
              + b1_ref[:, D_FF + c * q:D_FF + (c + 1) * q])
        glu = jnp.minimum(zg, SWIGLU_LIMIT)
        lin = jnp.clip(zl, -SWIGLU_LIMIT, SWIGLU_LIMIT)
        act_ref[:, c * q:(c + 1) * q] = (glu * jax.nn.sigmoid(SWIGLU_ALPHA * glu) * (lin + 1.0)).astype(BF16)
    return _pack_bf16_pairs(jnp.dot(act_ref[...], w2b[...], preferred_element_type=F32) + b2_ref[...])


def _moe_kernel(row_ref, size_ref, exp_ref, wt_ref, meta_ref, xs_hbm, w1_hbm, b1_ref, w2_hbm, b2_ref, ys_hbm,
                xin, yout, w1f, w2f, w1b, w2b, act, isem, osem, wsem):
    tm = MOE_TM
    n_blocks = ys_hbm.shape[0] // tm
    n_tbl = row_ref.shape[0]
    n = meta_ref[2 * N_EXPERTS + 1]

    def for_chunk(j, fn):
        r = pl.multiple_of(row_ref[j], tm)
        for code in MOE_CHUNK_CODES:
            @pl.when(size_ref[j] == code)
            def _():
                fn(r, code * tm)

    def in_copy(r, rows, s):
        return pltpu.make_async_copy(xs_hbm.at[pl.ds(r, rows)], xin.at[s, pl.ds(0, rows)], isem.at[s])

    def out_copy(r, rows, s):
        return pltpu.make_async_copy(yout.at[s, pl.ds(0, rows)], ys_hbm.at[pl.ds(r, rows)], osem.at[s])

    def start_in(j, s):
        for_chunk(j, lambda r, rows: in_copy(r, rows, s).start())

    def wait_in(j, s):
        for_chunk(j, lambda r, rows: in_copy(r, rows, s).wait())

    def start_out(j, s):
        for_chunk(j, lambda r, rows: out_copy(r, rows, s).start())

    def wait_out(j, s):
        for_chunk(j, lambda r, rows: out_copy(r, rows, s).wait())

    def weight_copies(e, ws):
        return (pltpu.make_async_copy(w1_hbm.at[e], w1f.at[ws], wsem.at[ws]),
                pltpu.make_async_copy(w2_hbm.at[e], w2f.at[ws], wsem.at[ws]))

    start_in(0, 0)
    for cp in weight_copies(exp_ref[0], 0):
        cp.start()

    def trip(i, carry):
        slot = i % 2

        @pl.when(i + 1 < n)
        def _():
            start_in(i + 1, 1 - slot)

        @pl.when(i >= 2)
        def _():
            wait_out(i - 2, slot)

        @pl.when(wt_ref[i] == 1)
        def _():
            ws = wt_ref[n_tbl + i]
            nxt = wt_ref[2 * n_tbl + i]
            for cp in weight_copies(exp_ref[i], ws):
                cp.wait()

            @pl.when(nxt >= 0)
            def _():
                for cp in weight_copies(nxt, 1 - ws):
                    cp.start()

            chunk = 128
            def cast_body(c, carry2):
                k0 = pl.multiple_of(c * chunk, chunk)
                w1b[pl.ds(k0, chunk), :] = w1f[ws, pl.ds(k0, chunk), :].astype(BF16)
                w2b[pl.ds(k0, chunk), :] = w2f[ws, pl.ds(k0, chunk), :].astype(BF16)
                return carry2
            lax.fori_loop(0, D_MODEL // chunk, cast_body, 0)

        wait_in(i, slot)

        for code in MOE_CHUNK_CODES:
            rows = code * tm

            @pl.when(size_ref[i] == code)
            def _():
                yout[slot, 0:rows] = _expert_mlp(xin[slot, 0:rows], w1b, b1_ref.at[exp_ref[i]], w2b,
                                                 b2_ref.at[exp_ref[i]], act.at[pl.ds(0, rows)])

        start_out(i, slot)
        return carry
    lax.fori_loop(0, n, trip, 0)

    @pl.when(n >= 2)
    def _():
        wait_out(n - 2, n % 2)
    wait_out(n - 1, (n - 1) % 2)

    yout[1, 0:tm] = jnp.zeros((tm, PACK_W), PACKED)

    def zero_block(j):
        return pltpu.make_async_copy(yout.at[1, pl.ds(0, tm)], ys_hbm.at[pl.ds(pl.multiple_of(j * tm, tm), tm)],
                                     osem.at[1])

    def start_body(j, carry):
        zero_block(j).start()
        return carry

    def wait_body(j, carry):
        zero_block(j).wait()
        return carry
    lax.fori_loop(meta_ref[2 * N_EXPERTS], n_blocks, start_body, 0)
    lax.fori_loop(meta_ref[2 * N_EXPERTS], n_blocks, wait_body, 0)


def _moe(chunk_row, chunk_size, chunk_exp, weight_tbl, meta, xs, w1, b1, w2, b2):
    whole = lambda i, *_: (0, 0, 0)
    grid_spec = pltpu.PrefetchScalarGridSpec(
        num_scalar_prefetch=5,
        grid=(1,),
        in_specs=[
            pl.BlockSpec(memory_space=pl.ANY),
            pl.BlockSpec(memory_space=pl.ANY),
            pl.BlockSpec((N_EXPERTS, 1, 2 * D_FF), whole),
            pl.BlockSpec(memory_space=pl.ANY),
            pl.BlockSpec((N_EXPERTS, 1, D_MODEL), whole),
        ],
        out_specs=pl.BlockSpec(memory_space=pl.ANY),
        scratch_shapes=[pltpu.VMEM((2, MOE_CH, PACK_W), PACKED), pltpu.VMEM((2, MOE_CH, PACK_W), PACKED),
                        pltpu.VMEM((2, D_MODEL, 2 * D_FF), F32), pltpu.VMEM((2, D_FF, D_MODEL), F32),
                        pltpu.VMEM((D_MODEL, 2 * D_FF), BF16), pltpu.VMEM((D_FF, D_MODEL), BF16),
                        pltpu.VMEM((MOE_CH, D_FF), BF16),
                        pltpu.SemaphoreType.DMA((2,)), pltpu.SemaphoreType.DMA((2,)), pltpu.SemaphoreType.DMA((2,))],
    )
    return pl.pallas_call(
        _moe_kernel,
        grid_spec=grid_spec,
        out_shape=jax.ShapeDtypeStruct(xs.shape, PACKED),
        compiler_params=pltpu.CompilerParams(dimension_semantics=("arbitrary",), vmem_limit_bytes=VMEM_LIMIT),
        name="moe",
    )(chunk_row, chunk_size, chunk_exp, weight_tbl, meta, xs, w1, b1, w2, b2)


def _combine_kernel(n_prompt_tiles, cnt_ref, off_ref, tot_ref, ys_hbm, routep_ref, routes_ref, x1p_ref, x1s_ref,
                    modp_ref, g2s_ref, fg_ref, op_ref, os_ref, buf, sem):
    i = pl.program_id(0)
    nb = pl.num_programs(0)
    slot = i % 2

    def fetch(tile, s):
        _for_strips(cnt_ref, off_ref, tile, buf.at[s], ys_hbm, sem.at[s], False, lambda cp: cp.start())

    @pl.when(i == 0)
    def _():
        buf[...] = jnp.zeros_like(buf)
        fetch(0, 0)

    @pl.when(i + 1 < nb)
    def _():
        fetch(i + 1, 1 - slot)

    _wait_tile_rows(tot_ref[i], buf.at[slot], ys_hbm, sem.at[slot], False)

    is_prompt = i < n_prompt_tiles
    route = jnp.where(is_prompt, routep_ref[...], routes_ref[...])
    c_pos = lax.broadcasted_iota(jnp.int32, (SEQ_TILE, SORT_ROWS), 1).astype(F32)
    gmat = jnp.zeros((SEQ_TILE, SORT_ROWS), F32)
    for k in range(TOP_K):
        gmat = jnp.where(c_pos == route[:, ROUTE_R + k:ROUTE_R + k + 1], route[:, ROUTE_G + k:ROUTE_G + k + 1], gmat)
    g_bf = gmat.astype(BF16)
    ff = jnp.concatenate([jnp.dot(g_bf, yb, preferred_element_type=F32) for yb in _unpack_bf16_pairs(buf[slot])],
                         axis=1)
    g2 = jnp.where(i < n_prompt_tiles, modp_ref[5:6, :], g2s_ref[...])
    x = jnp.where(is_prompt, x1p_ref[...], x1s_ref[...]) + g2 * ff
    y = _rms(x, fg_ref[...])

    @pl.when(i < n_prompt_tiles)
    def _():
        op_ref[...] = y

    @pl.when(i >= n_prompt_tiles)
    def _():
        os_ref[...] = y


def _combine(cnt8, tile_off, tot8, ys, route_p, route_s, x1_p, x1_s, mod_p, g2_rows, final_g, tiles_per_batch):
    nt = tot8.shape[0]
    tt = SEQ_TILE
    npt = x1_p.shape[0] // tt
    p_tile = lambda i, *_: (jnp.minimum(i, npt - 1), 0)
    s_tile = lambda i, *_: (jnp.maximum(i - npt, 0), 0)
    grid_spec = pltpu.PrefetchScalarGridSpec(
        num_scalar_prefetch=3,
        grid=(nt,),
        in_specs=[
            pl.BlockSpec(memory_space=pl.ANY),
            pl.BlockSpec((tt, LANES), p_tile), pl.BlockSpec((tt, LANES), s_tile),
            pl.BlockSpec((tt, D_MODEL), p_tile), pl.BlockSpec((tt, D_MODEL), s_tile),
            pl.BlockSpec((None, 6, D_MODEL), lambda i, *_: (jnp.minimum(i, npt - 1) // tiles_per_batch, 0, 0)),
            pl.BlockSpec((tt, D_MODEL), s_tile),
            pl.BlockSpec((1, D_MODEL), lambda i, *_: (0, 0)),
        ],
        out_specs=(pl.BlockSpec((tt, D_MODEL), p_tile), pl.BlockSpec((tt, D_MODEL), s_tile)),
        scratch_shapes=[pltpu.VMEM((2, SORT_ROWS, PACK_W), PACKED), pltpu.SemaphoreType.DMA((2,))],
    )
    return pl.pallas_call(
        functools.partial(_combine_kernel, npt),
        grid_spec=grid_spec,
        out_shape=(jax.ShapeDtypeStruct((npt * tt, D_MODEL), F32),
                   jax.ShapeDtypeStruct(((nt - npt) * tt, D_MODEL), F32)),
        compiler_params=pltpu.CompilerParams(dimension_semantics=("arbitrary",), vmem_limit_bytes=VMEM_LIMIT),
        name="combine",
    )(cnt8, tile_off, tot8, ys, route_p, route_s, x1_p, x1_s, mod_p, g2_rows, final_g)


def _block_diag_halves(wg):
    per_half = LRU_BLOCKS // 2
    w4 = wg.reshape(2, per_half, LRU_BLOCK_W, LRU_BLOCK_W)
    on_diag = jnp.eye(per_half, dtype=bool)[None, :, None, :, None]
    dense = jnp.where(on_diag, w4[:, :, :, None, :], 0.0)
    return dense.reshape(2, per_half * LRU_BLOCK_W, per_half * LRU_BLOCK_W).astype(BF16)


def _rope_tables(pos):
    half = HEAD_DIM // 2
    inv = ROPE_THETA ** (-jnp.arange(half, dtype=F32) / half)
    ang = pos.astype(F32)[:, None] * inv[None, :]
    cos = jnp.cos(ang)
    sin = jnp.sin(ang)
    cos128 = jnp.concatenate([cos, cos, cos, cos], axis=1)
    sin128 = jnp.concatenate([-sin, sin, -sin, sin], axis=1)
    return cos128, sin128


def _resident_spec(arr):
    zeros = (0,) * arr.ndim
    return pl.BlockSpec(arr.shape, lambda i: zeros)


def kernel(x_prompt, x_sample, state_lru_h, state_conv, cache_win_k, cache_win_v, c_prompt, c_sample, w_ada, b_ada, norm1_g, w_in, b_in, conv_w, conv_b, lru_wa, lru_ba, lru_wx, lru_bx, lru_lambda, attn_sinks, w_out, b_out, norm2_g, w_router, b_router, w1, b1, w2, b2, final_g):
    bp, seq, _ = x_prompt.shape
    bd, tdec, _ = x_sample.shape
    assert tdec == SUBLANES and seq % (MIX_TILES * SEQ_TILE) == 0 and bd % SAMPLE_BT == 0
    assert SAMPLE_BT * tdec == SEQ_TILE
    n_prompt = bp * seq
    n_sample = bd * tdec
    n_tok = n_prompt + n_sample
    l = 0

    head_perm = [h for c in range(4) for h in (c, GROUP + c)]
    o2 = 2 * LRU_WIDTH

    def permute_heads(arr, axis, start):
        take = lambda a, b: lax.slice_in_dim(arr, a, b, axis=axis)
        heads = [take(start + h * HEAD_DIM, start + (h + 1) * HEAD_DIM) for h in head_perm]
        return jnp.concatenate([take(0, start)] + heads + [take(start + Q_WIDTH, arr.shape[axis])], axis=axis)

    sinks_perm = attn_sinks[l]

    wr = jnp.tile(w_router[l], (1, TOP_K))
    wr_hi = wr.astype(BF16)
    weights = dict(
        norm1_g=norm1_g[l][None, :], w_in=permute_heads(w_in[l], 1, o2).astype(BF16),
        b_in=permute_heads(b_in[l], 0, o2)[None, :],
        conv_w=conv_w[l], conv_b=conv_b[l][None, :],
        ga=_block_diag_halves(lru_wa[l]), gx=_block_diag_halves(lru_wx[l]),
        lru_ba=lru_ba[l][None, :], lru_bx=lru_bx[l][None, :], lam=lru_lambda[l][None, :],
        w_out=permute_heads(w_out[l], 0, LRU_WIDTH).astype(BF16), b_out=b_out[l][None, :], norm2_g=norm2_g[l][None, :],
        wr_hi=wr_hi, wr_lo=(wr - wr_hi.astype(F32)).astype(BF16),
        b_router=jnp.tile(b_router[l], TOP_K)[None, :],
    )
    wlist = [weights[n] for n in WEIGHT_NAMES]

    mod_p, mod_s = _ada(jnp.concatenate([c_prompt, c_sample], axis=0), bp, w_ada[l], b_ada[l][None, :])
    mod_p = mod_p.reshape(bp, 6, D_MODEL)

    cos_p, sin_p = _rope_tables(jnp.arange(seq, dtype=jnp.int32))
    cos_s, sin_s = _rope_tables(PAST_LEN + jnp.arange(tdec, dtype=jnp.int32))
    cos_s = jnp.tile(cos_s, (SAMPLE_BT, 1))
    sin_s = jnp.tile(sin_s, (SAMPLE_BT, 1))
    h0_rows = jnp.pad(state_lru_h[l][:, None, :], ((0, 0), (0, tdec - 1), (0, 0))).reshape(n_sample, LRU_WIDTH)
    cprev_rows = jnp.pad(state_conv[l], ((0, 0), (0, tdec - (CONV_W - 1)), (0, 0))).reshape(n_sample, LRU_WIDTH)
    ck = cache_win_k[l].reshape(bd, WINDOW, KV_WIDTH)
    cv = cache_win_v[l].reshape(bd, WINDOW, KV_WIDTH)
    npt = n_prompt // SEQ_TILE
    nst = n_sample // SEQ_TILE
    mix_rows = MIX_TILES * SEQ_TILE
    steps_per_seq = seq // mix_rows
    wspecs = [_resident_spec(a) for a in wlist]

    rows_p = lambda width: pl.BlockSpec((mix_rows, width), lambda i: (i, 0))
    tail_p = lambda rows, width: pl.BlockSpec((None, rows, width), lambda i: (i // steps_per_seq, 0, 0))
    (x1_p, h2_p, route_p, cnt_p, hlast_p, ulast_p, klast_p, vlast_p) = pl.pallas_call(
        functools.partial(_prompt_kernel, steps_per_seq),
        grid=(n_prompt // mix_rows,),
        in_specs=[rows_p(D_MODEL),
                  pl.BlockSpec((None, 6, D_MODEL), lambda i: (i // steps_per_seq, 0, 0)),
                  pl.BlockSpec((mix_rows, LANES), lambda i: (i % steps_per_seq, 0)),
                  pl.BlockSpec((mix_rows, LANES), lambda i: (i % steps_per_seq, 0)),
                  pl.BlockSpec(memory_space=pltpu.SMEM)] + wspecs,
        out_specs=(rows_p(D_MODEL), rows_p(D_MODEL), rows_p(LANES),
                   pl.BlockSpec((MIX_TILES, SUBLANES, LANES), lambda i: (i, 0, 0)),
                   tail_p(SUBLANES, LRU_WIDTH), tail_p(SUBLANES, LRU_WIDTH),
                   tail_p(WINDOW, KV_WIDTH), tail_p(WINDOW, KV_WIDTH)),
        out_shape=(
            jax.ShapeDtypeStruct((n_prompt, D_MODEL), F32),
            jax.ShapeDtypeStruct((n_prompt, D_MODEL), BF16),
            jax.ShapeDtypeStruct((n_prompt, LANES), F32),
            jax.ShapeDtypeStruct((npt, SUBLANES, LANES), F32),
            jax.ShapeDtypeStruct((bp, SUBLANES, LRU_WIDTH), F32),
            jax.ShapeDtypeStruct((bp, SUBLANES, LRU_WIDTH), F32),
            jax.ShapeDtypeStruct((bp, WINDOW, KV_WIDTH), F32),
            jax.ShapeDtypeStruct((bp, WINDOW, KV_WIDTH), F32),
        ),
        scratch_shapes=[pltpu.VMEM((SUBLANES, LRU_WIDTH), F32), pltpu.VMEM((SUBLANES, LRU_WIDTH), F32),
                        pltpu.VMEM((WINDOW, KV_WIDTH), F32), pltpu.VMEM((WINDOW, KV_WIDTH), F32)],
        compiler_params=pltpu.CompilerParams(dimension_semantics=("arbitrary",), vmem_limit_bytes=VMEM_LIMIT),
        name="prompt_mixer",
    )(x_prompt.reshape(n_prompt, D_MODEL), mod_p, cos_p, sin_p, sinks_perm, *wlist)

    rows_s = lambda width: pl.BlockSpec((SEQ_TILE, width), lambda i: (i, 0))
    cache_spec = pl.BlockSpec((SAMPLE_BT, WINDOW, KV_WIDTH), lambda i: (i, 0, 0))
    (x1_s, h2_s, route_s, cnt_s, g2_rows, hs_s, u_s, s_k, s_v) = pl.pallas_call(
        _sample_kernel,
        grid=(nst,),
        in_specs=[rows_s(D_MODEL),
                  pl.BlockSpec((6, SAMPLE_BT, D_MODEL), lambda i: (0, i, 0)),
                  pl.BlockSpec((SEQ_TILE, LANES), lambda i: (0, 0)),
                  pl.BlockSpec((SEQ_TILE, LANES), lambda i: (0, 0)),
                  pl.BlockSpec(memory_space=pltpu.SMEM),
                  rows_s(LRU_WIDTH), rows_s(LRU_WIDTH), cache_spec, cache_spec] + wspecs,
        out_specs=(rows_s(D_MODEL), rows_s(D_MODEL), rows_s(LANES),
                   pl.BlockSpec((None, SUBLANES, LANES), lambda i: (i, 0, 0)),
                   rows_s(D_MODEL), rows_s(LRU_WIDTH), rows_s(LRU_WIDTH), cache_spec, cache_spec),
        out_shape=(
            jax.ShapeDtypeStruct((n_sample, D_MODEL), F32), jax.ShapeDtypeStruct((n_sample, D_MODEL), BF16),
            jax.ShapeDtypeStruct((n_sample, LANES), F32), jax.ShapeDtypeStruct((nst, SUBLANES, LANES), F32),
            jax.ShapeDtypeStruct((n_sample, D_MODEL), F32),
            jax.ShapeDtypeStruct((n_sample, LRU_WIDTH), F32),
            jax.ShapeDtypeStruct((n_sample, LRU_WIDTH), F32),
            jax.ShapeDtypeStruct((bd, WINDOW, KV_WIDTH), F32),
            jax.ShapeDtypeStruct((bd, WINDOW, KV_WIDTH), F32),
        ),
        compiler_params=pltpu.CompilerParams(dimension_semantics=("arbitrary",), vmem_limit_bytes=VMEM_LIMIT),
        name="sample_mixer",
    )(x_sample.reshape(n_sample, D_MODEL), mod_s, cos_s, sin_s, sinks_perm, h0_rows, cprev_rows, ck, cv, *wlist)
    tile_cnt = jnp.concatenate([cnt_p, cnt_s], axis=0)

    n_tiles = npt + nst
    n_assign = n_tok * TOP_K
    max_rows = n_assign + n_tiles * N_EXPERTS * (SUBLANES - 1) + N_EXPERTS * (MOE_TM - 1)
    n_blocks = -(-max_rows // MOE_TM)
    cnt = tile_cnt[:, 0, :N_EXPERTS].astype(jnp.int32)
    cnt8 = (cnt + SUBLANES - 1) // SUBLANES * SUBLANES
    counts = jnp.sum(cnt8, axis=0)
    pcounts = (counts + MOE_TM - 1) // MOE_TM * MOE_TM
    pend = jnp.cumsum(pcounts)
    pstart = pend - pcounts
    tile_off = pstart[None, :] + jnp.cumsum(cnt8, axis=0) - cnt8
    tot8 = jnp.sum(cnt8, axis=1)
    meta = jnp.concatenate([pstart + counts, pend, pend[-1:] // MOE_TM]).astype(jnp.int32)
    cnt8_flat = cnt8.reshape(-1)
    off_flat = tile_off.reshape(-1).astype(jnp.int32)

    xs = _dispatch(cnt8_flat, off_flat, tot8, meta, h2_p, h2_s, route_p, route_s, n_blocks * MOE_TM)

    big, small = MOE_CHUNK_CODES[0], MOE_CHUNK_CODES[1:]
    n_tm = pcounts // MOE_TM
    n_big = n_tm // big
    present = [(n_tm // b) % 2 for b in small]
    n_ch = n_big + sum(present)
    ch_end = jnp.cumsum(n_ch)
    ch_start = ch_end - n_ch
    n_chunks = n_blocks // big + len(small) * N_EXPERTS
    ci = jnp.arange(n_chunks, dtype=jnp.int32)
    owner = jnp.logical_and(ch_start[None, :] <= ci[:, None], ci[:, None] < ch_end[None, :])
    pick = lambda v: jnp.sum(jnp.where(owner, v[None, :], 0), axis=1)
    live = ci < ch_end[-1]
    local = ci - pick(ch_start)
    chunk_exp = jnp.where(live, pick(jnp.arange(N_EXPERTS, dtype=jnp.int32)), N_EXPERTS - 1).astype(jnp.int32)
    big_c = pick(n_big)
    code = jnp.where(local < big_c, big, 0)
    blocks_before = jnp.where(local < big_c, local * big, 0)
    order = big_c
    blocks = big_c * big
    for b, has in zip(small, present):
        has_c = pick(has)
        is_b = jnp.logical_and(has_c == 1, local == order)
        code = jnp.where(is_b, b, code)
        blocks_before = jnp.where(is_b, blocks, blocks_before)
        order = order + has_c
        blocks = blocks + has_c * b
    chunk_row = jnp.where(live, pick(pstart) + blocks_before * MOE_TM, 0).astype(jnp.int32)
    chunk_size = jnp.where(live, code, 0).astype(jnp.int32)
    has_rows = n_ch > 0
    e_ar = jnp.arange(N_EXPERTS, dtype=jnp.int32)
    w_slot = (jnp.cumsum(has_rows.astype(jnp.int32)) - 1) % 2
    later = jnp.logical_and(has_rows[None, :], e_ar[None, :] > e_ar[:, None])
    nxt = jnp.min(jnp.where(later, e_ar[None, :], N_EXPERTS), axis=1)
    nxt = jnp.where(nxt == N_EXPERTS, -1, nxt)
    weight_tbl = jnp.concatenate([jnp.logical_and(live, local == 0).astype(jnp.int32),
                                  pick(w_slot), jnp.where(live, pick(nxt), -1)]).astype(jnp.int32)
    moe_meta = jnp.concatenate([meta, ch_end[-1:].astype(jnp.int32)])
    ys = _moe(chunk_row, chunk_size, chunk_exp, weight_tbl, moe_meta, xs, w1[l], b1[l][:, None, :], w2[l], b2[l][:, None, :])

    y_p, y_s = _combine(cnt8_flat, off_flat, tot8, ys, route_p, route_s, x1_p, x1_s, mod_p, g2_rows,
                        final_g[None, :], seq // SEQ_TILE)

    y_prompt = y_p.reshape(bp, seq, D_MODEL)
    y_sample = y_s.reshape(bd, tdec, D_MODEL)
    p_h = hlast_p[:, SUBLANES - 1, :][None]
    p_c = ulast_p[:, SUBLANES - (CONV_W - 1):, :][None]
    p_k = klast_p.reshape(1, bp, WINDOW, N_KV_HEADS, HEAD_DIM)
    p_v = vlast_p.reshape(1, bp, WINDOW, N_KV_HEADS, HEAD_DIM)
    s_h = hs_s.reshape(bd, tdec, LRU_WIDTH)[:, tdec - 1, :][None]
    s_c = u_s.reshape(bd, tdec, LRU_WIDTH)[:, tdec - (CONV_W - 1):, :][None]
    s_kk = s_k.reshape(1, bd, WINDOW, N_KV_HEADS, HEAD_DIM)
    s_vv = s_v.reshape(1, bd, WINDOW, N_KV_HEADS, HEAD_DIM)
    return (y_prompt, y_sample, p_h, p_c, p_k, p_v, s_h, s_c, s_kk, s_vv)
```

```python
import functools

import jax
import jax.numpy as jnp
from jax import lax
from jax.experimental import pallas as pl
from jax.experimental.pallas import tpu as pltpu

F32 = jnp.float32
BF16 = jnp.bfloat16

D_MODEL = 1024
LRU_WIDTH = 512
LRU_BLOCKS = 8
LRU_BLOCK_W = LRU_WIDTH // LRU_BLOCKS
CONV_W = 4
LRU_C = 8.0
HEAD_DIM = 64
N_HEADS = 8
N_KV_HEADS = 2
GROUP = N_HEADS // N_KV_HEADS
WINDOW = 128
ROPE_THETA = 10000.0
N_EXPERTS = 32
TOP_K = 4
D_FF = D_MODEL
SWIGLU_LIMIT = 7.0
SWIGLU_ALPHA = 1.702
NORM_EPS = 1e-5
PAST_LEN = 8192
Q_WIDTH = N_HEADS * HEAD_DIM
KV_WIDTH = N_KV_HEADS * HEAD_DIM
IN_WIDTH = 2 * LRU_WIDTH + Q_WIDTH + 2 * KV_WIDTH

LANES = 128
SUBLANES = 8
SEQ_TILE = 256
MIX_TILES = 2
SAMPLE_BT = 32
MOE_TM = 256
MOE_CHUNK_CODES = (4, 2, 1)
MOE_CH = MOE_CHUNK_CODES[0] * MOE_TM
NEG_BIG = -1e30
VMEM_LIMIT = 56 * 1024 * 1024


def _rms(x, g):
    return x * lax.rsqrt(jnp.mean(x * x, axis=-1, keepdims=True) + NORM_EPS) * g


def _rms_mod(x, gain, shift):
    return x * lax.rsqrt(jnp.mean(x * x, axis=-1, keepdims=True) + NORM_EPS) * gain + shift


def _group_scan(a, b):
    rows, width = a.shape
    groups = rows // SUBLANES
    a3 = a.reshape(groups, SUBLANES, width)
    b3 = b.reshape(groups, SUBLANES, width)
    t = lax.broadcasted_iota(jnp.int32, (1, SUBLANES, 1), 1)
    d = 1
    while d < SUBLANES:
        keep = t >= d
        a_s = jnp.where(keep, pltpu.roll(a3, d, axis=1), 1.0)
        b_s = jnp.where(keep, pltpu.roll(b3, d, axis=1), 0.0)
        b3 = a3 * b_s + b3
        a3 = a3 * a_s
        d *= 2
    return a3.reshape(rows, width), b3.reshape(rows, width)


def _chain_groups(a_grp, b_grp, h_in):
    rows = a_grp.shape[0]
    out = []
    carry = h_in
    for g in range(rows // SUBLANES):
        sl = slice(g * SUBLANES, (g + 1) * SUBLANES)
        hg = b_grp[sl] + a_grp[sl] * carry
        out.append(hg)
        carry = hg[SUBLANES - 1:SUBLANES]
    return jnp.concatenate(out, axis=0)


def _rope128(x, cos, sin_signed, first_half):
    sw = jnp.where(first_half, pltpu.roll(x, LANES - HEAD_DIM // 2, axis=1), pltpu.roll(x, HEAD_DIM // 2, axis=1))
    return x * cos + sw * sin_signed


def _softplus(x):
    return jnp.maximum(x, 0.0) + jnp.log1p(jnp.exp(-jnp.abs(x)))


def _lru_coeffs(uc, w, first_pos_mask):
    ub = uc.astype(BF16)
    half = LRU_WIDTH // 2
    ra = jnp.concatenate([jnp.dot(ub[:, :half], w['ga'][0], preferred_element_type=F32),
                          jnp.dot(ub[:, half:], w['ga'][1], preferred_element_type=F32)], axis=1)
    rx = jnp.concatenate([jnp.dot(ub[:, :half], w['gx'][0], preferred_element_type=F32),
                          jnp.dot(ub[:, half:], w['gx'][1], preferred_element_type=F32)], axis=1)
    r = jax.nn.sigmoid(ra + w['lru_ba'][...])
    i = jax.nn.sigmoid(rx + w['lru_bx'][...])
    log_a = -LRU_C * r * _softplus(-w['lam'][...])
    a = jnp.exp(log_a)
    om = 1.0 - a * a
    mult = jnp.where(om > 0.0, om * lax.rsqrt(om), 0.0)
    if first_pos_mask is not None:
        mult = jnp.where(first_pos_mask, 1.0, mult)
    return a, mult * i * uc


def _conv_taps(u, s1, s2, s3, w):
    cw = w['conv_w']
    return w['conv_b'][...] + s3 * cw[0:1, :] + s2 * cw[1:2, :] + s1 * cw[2:3, :] + u * cw[3:4, :]


def _in_proj(x, mod, w):
    sh1, sc1 = mod
    h = _rms_mod(x, w['norm1_g'][...] * (1.0 + sc1), sh1)
    return jnp.dot(h.astype(BF16), w['w_in'][...], preferred_element_type=F32) + w['b_in'][...]


def _post_mix(x, mix, mod, w):
    g1, sh2, sc2 = mod
    x1 = x + g1 * (jnp.dot(mix.astype(BF16), w['w_out'][...], preferred_element_type=F32) + w['b_out'][...])
    h2 = _rms_mod(x1, w['norm2_g'][...] * (1.0 + sc2), sh2)
    h2_hi = h2.astype(BF16)
    h2_lo = (h2 - h2_hi.astype(F32)).astype(BF16)
    logits = (jnp.dot(h2_hi, w['wr_hi'][...], preferred_element_type=F32)
              + jnp.dot(h2_lo, w['wr_hi'][...], preferred_element_type=F32)
              + jnp.dot(h2_hi, w['wr_lo'][...], preferred_element_type=F32)) + w['b_router'][...]
    return x1, h2_hi, logits


ROUTE_E, ROUTE_G, ROUTE_R = 0, TOP_K, 2 * TOP_K


def _lane_roll1(v, shift):
    return pltpu.roll(jnp.broadcast_to(v, (SUBLANES, LANES)), shift, axis=1)[0:1]


def _route_tile(lg):
    rows = lg.shape[0]
    lane = lax.broadcasted_iota(jnp.int32, (1, LANES), 1)
    e_of = lane % N_EXPERTS
    grp = lane // N_EXPERTS
    e_id = e_of.astype(F32)
    onehot = jnp.zeros((rows, LANES), F32)
    vals, ids = [], []
    for k in range(TOP_K):
        m = jnp.max(lg, axis=1, keepdims=True)
        idx = jnp.min(jnp.where(lg == m, e_id, float(N_EXPERTS)), axis=1, keepdims=True)
        sel = e_id == idx
        lg = jnp.where(sel, -jnp.inf, lg)
        onehot = jnp.where(jnp.logical_and(sel, grp == k), 1.0, onehot)
        vals.append(m)
        ids.append(idx)
    ex = [jnp.exp(v - vals[0]) for v in vals]
    denom = ex[0] + ex[1] + ex[2] + ex[3]

    r_i = lax.broadcasted_iota(jnp.int32, (rows, rows), 0)
    c_i = lax.broadcasted_iota(jnp.int32, (rows, rows), 1)
    strict_lower = jnp.where(r_i > c_i, 1.0, 0.0).astype(BF16)
    prefix = jnp.dot(strict_lower, onehot.astype(BF16), preferred_element_type=F32)
    cnt = jnp.sum(onehot, axis=0, keepdims=True)
    base = jnp.zeros((1, LANES), F32)
    tot = cnt
    for s in range(1, TOP_K):
        rolled = _lane_roll1(cnt, s * N_EXPERTS)
        base = base + jnp.where(lane >= s * N_EXPERTS, rolled, 0.0)
        tot = tot + rolled
    pad_cnt = jnp.floor((tot + (SUBLANES - 1.0)) * (1.0 / SUBLANES)) * SUBLANES
    inc = pad_cnt
    d = 1
    while d < N_EXPERTS:
        inc = inc + jnp.where(e_of >= d, _lane_roll1(inc, d), 0.0)
        d *= 2
    strip_start = inc - pad_cnt
    ranked = onehot * (prefix + base + strip_start)

    route = jnp.zeros((rows, LANES), F32)
    for k in range(TOP_K):
        rank_k = jnp.sum(jnp.where(grp == k, ranked, 0.0), axis=1, keepdims=True)
        route = jnp.where(lane == ROUTE_E + k, ids[k], route)
        route = jnp.where(lane == ROUTE_G + k, ex[k] / denom, route)
        route = jnp.where(lane == ROUTE_R + k, rank_k, route)
    return route, tot


def _softmax_with_sink(s, sink_col):
    m = jnp.maximum(jnp.max(s, axis=-1, keepdims=True), sink_col)
    p = jnp.exp(s - m)
    denom = jnp.sum(p, axis=-1, keepdims=True) + jnp.exp(sink_col - m)
    return p, denom


WEIGHT_NAMES = ('norm1_g', 'w_in', 'b_in', 'conv_w', 'conv_b', 'ga', 'gx', 'lru_ba', 'lru_bx', 'lam',
                'w_out', 'b_out', 'norm2_g', 'wr_hi', 'wr_lo', 'b_router')


def _split_bf16(x):
    hi = x.astype(BF16)
    return hi, (x - hi.astype(F32)).astype(BF16)


def _ada_kernel(n_prompt_rows, c_ref, w_ref, b_ref, op_ref, os_ref):
    c = c_ref[...]
    s_hi, s_lo = _split_bf16(c * jax.nn.sigmoid(c))
    w_hi, w_lo = _split_bf16(w_ref[...])
    mod = (jnp.dot(s_hi, w_hi, preferred_element_type=F32) + jnp.dot(s_lo, w_hi, preferred_element_type=F32)
           + jnp.dot(s_hi, w_lo, preferred_element_type=F32)) + b_ref[...]
    op_ref[...] = mod[:n_prompt_rows]
    os_ref[...] = mod[n_prompt_rows:]


def _ada(c_all, n_prompt_rows, w_ada, b_ada):
    rows = c_all.shape[0]
    n_s = rows - n_prompt_rows
    assert n_prompt_rows % SUBLANES == 0
    return pl.pallas_call(
        functools.partial(_ada_kernel, n_prompt_rows),
        grid=(6,),
        in_specs=[pl.BlockSpec((rows, D_MODEL), lambda i: (0, 0)),
                  pl.BlockSpec((D_MODEL, D_MODEL), lambda i: (0, i)),
                  pl.BlockSpec((1, D_MODEL), lambda i: (0, i))],
        out_specs=(pl.BlockSpec((n_prompt_rows, D_MODEL), lambda i: (0, i)),
                   pl.BlockSpec((None, n_s, D_MODEL), lambda i: (i, 0, 0))),
        out_shape=(jax.ShapeDtypeStruct((n_prompt_rows, 6 * D_MODEL), F32),
                   jax.ShapeDtypeStruct((6, n_s, D_MODEL), F32)),
        compiler_params=pltpu.CompilerParams(dimension_semantics=("arbitrary",), vmem_limit_bytes=VMEM_LIMIT),
        name="ada",
    )(c_all, w_ada, b_ada)


def _prompt_body(seq_start, x_ref, mod_ref, cos_ref, sin_ref, sinks_ref, w, x1_ref, h2_ref, lg_ref,
                 hlast_ref, ulast_ref, klast_ref, vlast_ref, conv_c, h_c, k_c, v_c):
    ts = SEQ_TILE

    if seq_start is not None:
        @pl.when(seq_start)
        def _():
            conv_c[...] = jnp.zeros_like(conv_c)
            h_c[...] = jnp.zeros_like(h_c)
            k_c[...] = jnp.zeros_like(k_c)
            v_c[...] = jnp.zeros_like(v_c)

    x = x_ref[...]
    mod = mod_ref[...]
    proj = _in_proj(x, (mod[0:1], mod[1:2]), w)
    u = proj[:, :LRU_WIDTH]
    gate = proj[:, LRU_WIDTH:2 * LRU_WIDTH]
    o2 = 2 * LRU_WIDTH

    rowid = lax.broadcasted_iota(jnp.int32, (ts, 1), 0)
    u_ext = jnp.concatenate([conv_c[...], u], axis=0)
    s1, s2, s3 = (pltpu.roll(u_ext, d, axis=0)[SUBLANES:] for d in (1, 2, 3))
    uc = _conv_taps(u, s1, s2, s3, w)
    conv_c[...] = u[ts - SUBLANES:]
    ulast_ref[...] = u[ts - SUBLANES:]

    first_pos = None if seq_start is None else jnp.logical_and(rowid == 0, seq_start)
    a, bt = _lru_coeffs(uc, w, first_pos)
    hs = _chain_groups(*_group_scan(a, bt), h_c[0:1, :])
    h_tail = hs[ts - SUBLANES:]
    h_c[...] = jnp.broadcast_to(h_tail[SUBLANES - 1:SUBLANES, :], h_c.shape)
    hlast_ref[...] = h_tail
    lru_out = hs * jax.nn.gelu(gate)

    cos = cos_ref[...]
    sin = sin_ref[...]
    lane = lax.broadcasted_iota(jnp.int32, (1, LANES), 1)
    first_half = (lane % HEAD_DIM) < (HEAD_DIM // 2)
    qcols = [_rope128(proj[:, o2 + c * LANES:o2 + (c + 1) * LANES], cos, sin, first_half) * (HEAD_DIM ** -0.5)
             for c in range(4)]
    k = _rope128(proj[:, o2 + Q_WIDTH:o2 + Q_WIDTH + KV_WIDTH], cos, sin, first_half)
    v = proj[:, o2 + Q_WIDTH + KV_WIDTH:]
    k_ext = jnp.concatenate([k_c[...], k], axis=0).astype(BF16)
    v_ext = jnp.concatenate([v_c[...], v], axis=0).astype(BF16)
    k_c[...] = k[ts - WINDOW:]
    v_c[...] = v[ts - WINDOW:]
    klast_ref[...] = k[ts - WINDOW:]
    vlast_ref[...] = v[ts - WINDOW:]

    qi = lax.broadcasted_iota(jnp.int32, (WINDOW, 2 * WINDOW), 0)
    kj = lax.broadcasted_iota(jnp.int32, (WINDOW, 2 * WINDOW), 1)
    band = jnp.logical_and(kj > qi, kj <= qi + WINDOW)
    lane_lo = lane < HEAD_DIM
    grow = lax.broadcasted_iota(jnp.int32, (GROUP * WINDOW, 1), 0) // WINDOW
    attn_cols = [[] for _ in range(4)]
    for blk in range(ts // WINDOW):
        if seq_start is None or blk > 0:
            mask = band
        else:
            mask = jnp.logical_and(band, jnp.logical_or(kj >= WINDOW, jnp.logical_not(seq_start)))
        mask4 = jnp.concatenate([mask] * GROUP, axis=0)
        kb = k_ext[blk * WINDOW:(blk + 2) * WINDOW]
        vb = v_ext[blk * WINDOW:(blk + 2) * WINDOW]
        outs = []
        for kv in range(N_KV_HEADS):
            sel = lane_lo if kv == 0 else jnp.logical_not(lane_lo)
            qs = jnp.concatenate(
                [jnp.where(sel, qc[blk * WINDOW:(blk + 1) * WINDOW], 0.0) for qc in qcols], axis=0).astype(BF16)
            s = lax.dot_general(qs, kb, (((1,), (1,)), ((), ())), preferred_element_type=F32)
            s = jnp.where(mask4, s, NEG_BIG)
            sink_col = jnp.zeros((GROUP * WINDOW, 1), F32)
            for g in range(GROUP):
                sink_col = jnp.where(grow == g, sinks_ref[kv * GROUP + g], sink_col)
            p, denom = _softmax_with_sink(s, sink_col)
            outs.append(jnp.dot(p.astype(BF16), vb, preferred_element_type=F32) / denom)
        for c in range(4):
            attn_cols[c].append(jnp.where(lane_lo, outs[0][c * WINDOW:(c + 1) * WINDOW],
                                          outs[1][c * WINDOW:(c + 1) * WINDOW]))
    attn = jnp.concatenate([jnp.concatenate(cols, axis=0) for cols in attn_cols], axis=1)

    mix = jnp.concatenate([lru_out, attn], axis=1)
    x1, h2, logits = _post_mix(x, mix, (mod[2:3], mod[3:4], mod[4:5]), w)
    x1_ref[...] = x1
    h2_ref[...] = h2
    route, tot = _route_tile(logits)
    lg_ref[0][...] = route
    lg_ref[1][...] = jnp.broadcast_to(tot, lg_ref[1].shape)


def _expand_rows(m, t):
    b, wd = m.shape
    return jnp.broadcast_to(m[:, None, :], (b, t, wd)).reshape(b * t, wd)


def _sample_body(x_ref, mod_ref, cos_ref, sin_ref, sinks_ref, h0_ref, cprev_ref, ck_ref, cv_ref, w,
                 x1_ref, h2_ref, lg_ref, g2_ref, hs_ref, u_ref, ko_ref, vo_ref):
    bt_, t = SAMPLE_BT, SUBLANES
    rows = bt_ * t

    x = x_ref[...]
    mods = [_expand_rows(mod_ref[i], t) for i in range(6)]
    proj = _in_proj(x, (mods[0], mods[1]), w)
    u = proj[:, :LRU_WIDTH]
    gate = proj[:, LRU_WIDTH:2 * LRU_WIDTH]
    o2 = 2 * LRU_WIDTH
    u_ref[...] = u

    rowid = lax.broadcasted_iota(jnp.int32, (rows, 1), 0) % t
    cprev = cprev_ref[...]
    taps = []
    for d in (1, 2, 3):
        taps.append(jnp.where(rowid >= d, pltpu.roll(u, d, axis=0),
                              pltpu.roll(cprev, (d - (CONV_W - 1)) % rows, axis=0)))
    uc = _conv_taps(u, taps[0], taps[1], taps[2], w)

    a, bt = _lru_coeffs(uc, w, None)
    bt = bt + a * h0_ref[...]
    _, hs = _group_scan(a, bt)
    hs_ref[...] = hs
    lru_out = hs * jax.nn.gelu(gate)

    cos = cos_ref[...]
    sin = sin_ref[...]
    lane = lax.broadcasted_iota(jnp.int32, (1, LANES), 1)
    first_half = (lane % HEAD_DIM) < (HEAD_DIM // 2)
    qcols = [_rope128(proj[:, o2 + c * LANES:o2 + (c + 1) * LANES], cos, sin, first_half) * (HEAD_DIM ** -0.5)
             for c in range(4)]
    k = _rope128(proj[:, o2 + Q_WIDTH:o2 + Q_WIDTH + KV_WIDTH], cos, sin, first_half)
    v = proj[:, o2 + Q_WIDTH + KV_WIDTH:]
    k3 = k.reshape(bt_, t, KV_WIDTH)
    v3 = v.reshape(bt_, t, KV_WIDTH)
    ck = ck_ref[...]
    cv = cv_ref[...]
    ko_ref[:, :WINDOW - t, :] = ck[:, t:, :]
    ko_ref[:, WINDOW - t:, :] = k3
    vo_ref[:, :WINDOW - t, :] = cv[:, t:, :]
    vo_ref[:, WINDOW - t:, :] = v3

    ckb, cvb, k3b, v3b = ck.astype(BF16), cv.astype(BF16), k3.astype(BF16), v3.astype(BF16)
    lane_lo = lane < HEAD_DIM
    gq = GROUP * t
    tq = lax.broadcasted_iota(jnp.int32, (1, gq, 1), 1) % t
    mask_c = lax.broadcasted_iota(jnp.int32, (1, gq, WINDOW), 2) > tq
    mask_n = lax.broadcasted_iota(jnp.int32, (1, gq, t), 2) <= tq
    grow = lax.broadcasted_iota(jnp.int32, (1, gq, 1), 1) // t
    bdims = (((2,), (2,)), ((0,), (0,)))
    pdims = (((2,), (1,)), ((0,), (0,)))
    outs = []
    for kv in range(N_KV_HEADS):
        sel = lane_lo if kv == 0 else jnp.logical_not(lane_lo)
        q3 = jnp.concatenate([jnp.where(sel, qc, 0.0).reshape(bt_, t, LANES) for qc in qcols], axis=1).astype(BF16)
        sc = lax.dot_general(q3, ckb, bdims, preferred_element_type=F32)
        sn = lax.dot_general(q3, k3b, bdims, preferred_element_type=F32)
        sc = jnp.where(mask_c, sc, NEG_BIG)
        sn = jnp.where(mask_n, sn, NEG_BIG)
        sink_col = jnp.zeros((1, gq, 1), F32)
        for g in range(GROUP):
            sink_col = jnp.where(grow == g, sinks_ref[kv * GROUP + g], sink_col)
        m = jnp.maximum(jnp.maximum(jnp.max(sc, axis=-1, keepdims=True), jnp.max(sn, axis=-1, keepdims=True)),
                        sink_col)
        pc = jnp.exp(sc - m)
        pn = jnp.exp(sn - m)
        denom = jnp.sum(pc, axis=-1, keepdims=True) + jnp.sum(pn, axis=-1, keepdims=True) + jnp.exp(sink_col - m)
        o = (lax.dot_general(pc.astype(BF16), cvb, pdims, preferred_element_type=F32)
             + lax.dot_general(pn.astype(BF16), v3b, pdims, preferred_element_type=F32)) / denom
        outs.append(o)
    attn = jnp.concatenate(
        [jnp.where(lane_lo, outs[0][:, c * t:(c + 1) * t, :], outs[1][:, c * t:(c + 1) * t, :]).reshape(rows, LANES)
         for c in range(4)], axis=1)

    mix = jnp.concatenate([lru_out, attn], axis=1)
    x1, h2, logits = _post_mix(x, mix, (mods[2], mods[3], mods[4]), w)
    x1_ref[...] = x1
    h2_ref[...] = h2
    route, tot = _route_tile(logits)
    lg_ref[0][...] = route
    lg_ref[1][...] = jnp.broadcast_to(tot, lg_ref[1].shape)
    g2_ref[...] = mods[5]


def _prompt_kernel(steps_per_seq, x_ref, mod_ref, cos_ref, sin_ref, sinks_ref, *rest):
    nw = len(WEIGHT_NAMES)
    w = dict(zip(WEIGHT_NAMES, rest[:nw]))
    (x1_ref, h2_ref, route_ref, cnt_ref, hlast_ref, ulast_ref, klast_ref, vlast_ref,
     conv_c, h_c, k_c, v_c) = rest[nw:]
    seq_start = pl.program_id(0) % steps_per_seq == 0
    for sub in range(MIX_TILES):
        rows = pl.ds(sub * SEQ_TILE, SEQ_TILE)
        _prompt_body(seq_start if sub == 0 else None, x_ref.at[rows], mod_ref, cos_ref.at[rows], sin_ref.at[rows],
                     sinks_ref, w, x1_ref.at[rows], h2_ref.at[rows], (route_ref.at[rows], cnt_ref.at[sub]),
                     hlast_ref, ulast_ref, klast_ref, vlast_ref, conv_c, h_c, k_c, v_c)


def _sample_kernel(x_ref, mod_ref, cos_ref, sin_ref, sinks_ref, h0_ref, cprev_ref, ck_ref, cv_ref, *rest):
    nw = len(WEIGHT_NAMES)
    w = dict(zip(WEIGHT_NAMES, rest[:nw]))
    x1_ref, h2_ref, route_ref, cnt_ref, g2_ref, hs_ref, u_ref, ko_ref, vo_ref = rest[nw:]
    _sample_body(x_ref, mod_ref, cos_ref, sin_ref, sinks_ref, h0_ref, cprev_ref, ck_ref, cv_ref, w,
                 x1_ref, h2_ref, (route_ref, cnt_ref), g2_ref, hs_ref, u_ref, ko_ref, vo_ref)


PACK_W = D_MODEL // 2
PACKED = jnp.int32


def _pack_bf16_pairs(x):
    return pltpu.pack_elementwise([x[:, :PACK_W], x[:, PACK_W:]], packed_dtype=BF16)


def _unpack_bf16_pairs(w):
    return tuple(pltpu.unpack_elementwise(w, index=k, packed_dtype=BF16, unpacked_dtype=F32).astype(BF16)
                 for k in range(2))


STRIP_SIZES = tuple(SUBLANES << b for b in range(6))
STRIP_LARGE = 64
SORT_ROWS = SEQ_TILE * TOP_K + N_EXPERTS * SUBLANES
TILE_WAIT_SIZES = tuple(SUBLANES << b for b in range(8))


def _for_strips(cnt_ref, off_ref, tile, buf_slot, hbm, sem, to_hbm, act):
    def e_body(e, local):
        n = cnt_ref[tile * N_EXPERTS + e]
        glob = off_ref[tile * N_EXPERTS + e]

        def pieces(sizes, done):
            for p in sizes:
                piece = n & p
                lo = pl.ds(pl.multiple_of(local + done, SUBLANES), p)
                gl = pl.ds(pl.multiple_of(glob + done, SUBLANES), p)

                @pl.when(piece != 0)
                def _():
                    if to_hbm:
                        act(pltpu.make_async_copy(buf_slot.at[lo], hbm.at[gl], sem))
                    else:
                        act(pltpu.make_async_copy(hbm.at[gl], buf_slot.at[lo], sem))
                done = done + piece

        large = tuple(p for p in reversed(STRIP_SIZES) if p >= STRIP_LARGE)
        small = tuple(p for p in reversed(STRIP_SIZES) if p < STRIP_LARGE)
        n_large = n & (-STRIP_LARGE)

        @pl.when(n_large != 0)
        def _():
            pieces(large, 0)
        pieces(small, n_large)
        return local + n
    lax.fori_loop(0, N_EXPERTS, e_body, 0)


def _wait_tile_rows(total, buf_slot, hbm, sem, to_hbm):
    for p in TILE_WAIT_SIZES:
        @pl.when((total & p) != 0)
        def _():
            if to_hbm:
                pltpu.make_async_copy(buf_slot.at[pl.ds(0, p)], hbm.at[pl.ds(0, p)], sem).wait()
            else:
                pltpu.make_async_copy(hbm.at[pl.ds(0, p)], buf_slot.at[pl.ds(0, p)], sem).wait()


def _dispatch_kernel(n_prompt_tiles, cnt_ref, off_ref, tot_ref, meta_ref, h2p_ref, h2s_ref, routep_ref, routes_ref,
                     xs_hbm, sbuf, zblk, sem, zsem):
    i = pl.program_id(0)
    nb = pl.num_programs(0)
    slot = i % 2
    n_blocks = xs_hbm.shape[0] // MOE_TM

    is_prompt = i < n_prompt_tiles
    h2 = jnp.where(is_prompt, h2p_ref[...], h2s_ref[...])
    route_t = jnp.where(is_prompt, routep_ref[...], routes_ref[...]).T
    r_pos = lax.broadcasted_iota(jnp.int16, (SORT_ROWS, SEQ_TILE), 0)
    perm = jnp.zeros((SORT_ROWS, SEQ_TILE), BF16)
    for k in range(TOP_K):
        pos_k = route_t[ROUTE_R + k:ROUTE_R + k + 1, :].astype(jnp.int32).astype(jnp.int16)
        perm = jnp.where(r_pos == pos_k, jnp.ones((), BF16), perm)
    sbuf[slot] = _pack_bf16_pairs(jnp.dot(perm, h2, preferred_element_type=F32))

    _for_strips(cnt_ref, off_ref, i, sbuf.at[slot], xs_hbm, sem.at[slot], True, lambda cp: cp.start())

    @pl.when(i > 0)
    def _():
        _wait_tile_rows(tot_ref[jnp.maximum(i - 1, 0)], sbuf.at[1 - slot], xs_hbm, sem.at[1 - slot], True)

    @pl.when(i == nb - 1)
    def _():
        _wait_tile_rows(tot_ref[i], sbuf.at[slot], xs_hbm, sem.at[slot], True)
        zblk[...] = jnp.zeros_like(zblk)

        def for_region_tails(act):
            def e_body(e, carry):
                start = meta_ref[e]
                n = meta_ref[N_EXPERTS + e] - start
                done = 0
                for p in reversed([q for q in STRIP_SIZES if q < MOE_TM]):
                    piece = n & p
                    rows = pl.ds(pl.multiple_of(start + done, SUBLANES), p)

                    @pl.when(piece != 0)
                    def _():
                        act(pltpu.make_async_copy(zblk.at[pl.ds(0, p)], xs_hbm.at[rows], zsem.at[0]))
                    done = done + piece
                return carry
            lax.fori_loop(0, N_EXPERTS, e_body, 0)

        def for_tail_blocks(act):
            def b_body(j, carry):
                act(pltpu.make_async_copy(zblk, xs_hbm.at[pl.ds(pl.multiple_of(j * MOE_TM, MOE_TM), MOE_TM)],
                                          zsem.at[0]))
                return carry
            lax.fori_loop(meta_ref[2 * N_EXPERTS], n_blocks, b_body, 0)

        for_region_tails(lambda cp: cp.start())
        for_tail_blocks(lambda cp: cp.start())
        for_region_tails(lambda cp: cp.wait())
        for_tail_blocks(lambda cp: cp.wait())


def _dispatch(cnt8, tile_off, tot8, meta, h2_p, h2_s, route_p, route_s, n_rows):
    npt = h2_p.shape[0] // SEQ_TILE
    p_tile = lambda i, *_: (jnp.minimum(i, npt - 1), 0)
    s_tile = lambda i, *_: (jnp.maximum(i - npt, 0), 0)
    nt = tot8.shape[0]
    tt = SEQ_TILE
    grid_spec = pltpu.PrefetchScalarGridSpec(
        num_scalar_prefetch=4,
        grid=(nt,),
        in_specs=[pl.BlockSpec((tt, D_MODEL), p_tile), pl.BlockSpec((tt, D_MODEL), s_tile),
                  pl.BlockSpec((tt, LANES), p_tile), pl.BlockSpec((tt, LANES), s_tile)],
        out_specs=pl.BlockSpec(memory_space=pl.ANY),
        scratch_shapes=[pltpu.VMEM((2, SORT_ROWS, PACK_W), PACKED), pltpu.VMEM((MOE_TM, PACK_W), PACKED),
                        pltpu.SemaphoreType.DMA((2,)), pltpu.SemaphoreType.DMA((1,))],
    )
    return pl.pallas_call(
        functools.partial(_dispatch_kernel, npt),
        grid_spec=grid_spec,
        out_shape=jax.ShapeDtypeStruct((n_rows, PACK_W), PACKED),
        compiler_params=pltpu.CompilerParams(dimension_semantics=("arbitrary",), vmem_limit_bytes=VMEM_LIMIT),
        name="dispatch",
    )(cnt8, tile_off, tot8, meta, h2_p, h2_s, route_p, route_s)


def _expert_mlp(words, w1b, b1_ref, w2b, b2_ref, act_ref):
    xb = jnp.concatenate(_unpack_bf16_pairs(words), axis=1)
    q = D_FF // 4
    for c in range(4):
        zg = jnp.dot(xb, w1b[:, c * q:(c + 1) * q], preferred_element_type=F32) + b1_ref[:, c * q:(c + 1) * q]
        zl = (jnp.dot(xb, w1b[:, D_FF + c * q:D_FF + (c + 1) * q], preferred_element_type=F32)
              + b1_ref[:, D_FF + c * q:D_FF + (c + 1) * q])
        glu = jnp.minimum(zg, SWIGLU_LIMIT)
        lin = jnp.clip(zl, -SWIGLU_LIMIT, SWIGLU_LIMIT)
        act_ref[:, c * q:(c + 1) * q] = (glu * jax.nn.sigmoid(SWIGLU_ALPHA * glu) * (lin + 1.0)).astype(BF16)
    return _pack_bf16_pairs(jnp.dot(act_ref[...], w2b[...], preferred_element_type=F32) + b2_ref[...])


def _moe_kernel(row_ref, size_ref, exp_ref, wt_ref, meta_ref, xs_hbm, w1_hbm, b1_ref, w2_hbm, b2_ref, ys_hbm,
                xin, yout, w1f, w2f, w1b, w2b, act, isem, osem, wsem):
    tm = MOE_TM
    n_blocks = ys_hbm.shape[0] // tm
    n_tbl = row_ref.shape[0]
    n = meta_ref[2 * N_EXPERTS + 1]

    def for_chunk(j, fn):
        r = pl.multiple_of(row_ref[j], tm)
        for code in MOE_CHUNK_CODES:
            @pl.when(size_ref[j] == code)
            def _():
                fn(r, code * tm)

    def in_copy(r, rows, s):
        return pltpu.make_async_copy(xs_hbm.at[pl.ds(r, rows)], xin.at[s, pl.ds(0, rows)], isem.at[s])

    def out_copy(r, rows, s):
        return pltpu.make_async_copy(yout.at[s, pl.ds(0, rows)], ys_hbm.at[pl.ds(r, rows)], osem.at[s])

    def start_in(j, s):
        for_chunk(j, lambda r, rows: in_copy(r, rows, s).start())

    def wait_in(j, s):
        for_chunk(j, lambda r, rows: in_copy(r, rows, s).wait())

    def start_out(j, s):
        for_chunk(j, lambda r, rows: out_copy(r, rows, s).start())

    def wait_out(j, s):
        for_chunk(j, lambda r, rows: out_copy(r, rows, s).wait())

    def weight_copies(e, ws):
        return (pltpu.make_async_copy(w1_hbm.at[e], w1f.at[ws], wsem.at[ws]),
                pltpu.make_async_copy(w2_hbm.at[e], w2f.at[ws], wsem.at[ws]))

    start_in(0, 0)
    for cp in weight_copies(exp_ref[0], 0):
        cp.start()

    def trip(i, carry):
        slot = i % 2

        @pl.when(i + 1 < n)
        def _():
            start_in(i + 1, 1 - slot)

        @pl.when(i >= 2)
        def _():
            wait_out(i - 2, slot)

        @pl.when(wt_ref[i] == 1)
        def _():
            ws = wt_ref[n_tbl + i]
            nxt = wt_ref[2 * n_tbl + i]
            for cp in weight_copies(exp_ref[i], ws):
                cp.wait()

            @pl.when(nxt >= 0)
            def _():
                for cp in weight_copies(nxt, 1 - ws):
                    cp.start()

            chunk = 128
            def cast_body(c, carry2):
                k0 = pl.multiple_of(c * chunk, chunk)
                w1b[pl.ds(k0, chunk), :] = w1f[ws, pl.ds(k0, chunk), :].astype(BF16)
                w2b[pl.ds(k0, chunk), :] = w2f[ws, pl.ds(k0, chunk), :].astype(BF16)
                return carry2
            lax.fori_loop(0, D_MODEL // chunk, cast_body, 0)

        wait_in(i, slot)

        for code in MOE_CHUNK_CODES:
            rows = code * tm

            @pl.when(size_ref[i] == code)
            def _():
                yout[slot, 0:rows] = _expert_mlp(xin[slot, 0:rows], w1b, b1_ref.at[exp_ref[i]], w2b,
                                                 b2_ref.at[exp_ref[i]], act.at[pl.ds(0, rows)])

        start_out(i, slot)
        return carry
    lax.fori_loop(0, n, trip, 0)

    @pl.when(n >= 2)
    def _():
        wait_out(n - 2, n % 2)
    wait_out(n - 1, (n - 1) % 2)

    yout[1, 0:tm] = jnp.zeros((tm, PACK_W), PACKED)

    def zero_block(j):
        return pltpu.make_async_copy(yout.at[1, pl.ds(0, tm)], ys_hbm.at[pl.ds(pl.multiple_of(j * tm, tm), tm)],
                                     osem.at[1])

    def start_body(j, carry):
        zero_block(j).start()
        return carry

    def wait_body(j, carry):
        zero_block(j).wait()
        return carry
    lax.fori_loop(meta_ref[2 * N_EXPERTS], n_blocks, start_body, 0)
    lax.fori_loop(meta_ref[2 * N_EXPERTS], n_blocks, wait_body, 0)


def _moe(chunk_row, chunk_size, chunk_exp, weight_tbl, meta, xs, w1, b1, w2, b2):
    whole = lambda i, *_: (0, 0, 0)
    grid_spec = pltpu.PrefetchScalarGridSpec(
        num_scalar_prefetch=5,
        grid=(1,),
        in_specs=[
            pl.BlockSpec(memory_space=pl.ANY),
            pl.BlockSpec(memory_space=pl.ANY),
            pl.BlockSpec((N_EXPERTS, 1, 2 * D_FF), whole),
            pl.BlockSpec(memory_space=pl.ANY),
            pl.BlockSpec((N_EXPERTS, 1, D_MODEL), whole),
        ],
        out_specs=pl.BlockSpec(memory_space=pl.ANY),
        scratch_shapes=[pltpu.VMEM((2, MOE_CH, PACK_W), PACKED), pltpu.VMEM((2, MOE_CH, PACK_W), PACKED),
                        pltpu.VMEM((2, D_MODEL, 2 * D_FF), F32), pltpu.VMEM((2, D_FF, D_MODEL), F32),
                        pltpu.VMEM((D_MODEL, 2 * D_FF), BF16), pltpu.VMEM((D_FF, D_MODEL), BF16),
                        pltpu.VMEM((MOE_CH, D_FF), BF16),
                        pltpu.SemaphoreType.DMA((2,)), pltpu.SemaphoreType.DMA((2,)), pltpu.SemaphoreType.DMA((2,))],
    )
    return pl.pallas_call(
        _moe_kernel,
        grid_spec=grid_spec,
        out_shape=jax.ShapeDtypeStruct(xs.shape, PACKED),
        compiler_params=pltpu.CompilerParams(dimension_semantics=("arbitrary",), vmem_limit_bytes=VMEM_LIMIT),
        name="moe",
    )(chunk_row, chunk_size, chunk_exp, weight_tbl, meta, xs, w1, b1, w2, b2)


def _combine_kernel(n_prompt_tiles, cnt_ref, off_ref, tot_ref, ys_hbm, routep_ref, routes_ref, x1p_ref, x1s_ref,
                    modp_ref, g2s_ref, fg_ref, op_ref, os_ref, buf, sem):
    i = pl.program_id(0)
    nb = pl.num_programs(0)
    slot = i % 2

    def fetch(tile, s):
        _for_strips(cnt_ref, off_ref, tile, buf.at[s], ys_hbm, sem.at[s], False, lambda cp: cp.start())

    @pl.when(i == 0)
    def _():
        buf[...] = jnp.zeros_like(buf)
        fetch(0, 0)

    @pl.when(i + 1 < nb)
    def _():
        fetch(i + 1, 1 - slot)

    _wait_tile_rows(tot_ref[i], buf.at[slot], ys_hbm, sem.at[slot], False)

    is_prompt = i < n_prompt_tiles
    route = jnp.where(is_prompt, routep_ref[...], routes_ref[...])
    c_pos = lax.broadcasted_iota(jnp.int16, (SEQ_TILE, SORT_ROWS), 1)
    g_bf = jnp.zeros((SEQ_TILE, SORT_ROWS), BF16)
    for k in range(TOP_K):
        pos_k = route[:, ROUTE_R + k:ROUTE_R + k + 1].astype(jnp.int32).astype(jnp.int16)
        g_bf = jnp.where(c_pos == pos_k, route[:, ROUTE_G + k:ROUTE_G + k + 1].astype(BF16), g_bf)
    ff = jnp.concatenate([jnp.dot(g_bf, yb, preferred_element_type=F32) for yb in _unpack_bf16_pairs(buf[slot])],
                         axis=1)
    g2 = jnp.where(i < n_prompt_tiles, modp_ref[5:6, :], g2s_ref[...])
    x = jnp.where(is_prompt, x1p_ref[...], x1s_ref[...]) + g2 * ff
    y = _rms(x, fg_ref[...])

    @pl.when(i < n_prompt_tiles)
    def _():
        op_ref[...] = y

    @pl.when(i >= n_prompt_tiles)
    def _():
        os_ref[...] = y


def _combine(cnt8, tile_off, tot8, ys, route_p, route_s, x1_p, x1_s, mod_p, g2_rows, final_g, tiles_per_batch):
    nt = tot8.shape[0]
    tt = SEQ_TILE
    npt = x1_p.shape[0] // tt
    p_tile = lambda i, *_: (jnp.minimum(i, npt - 1), 0)
    s_tile = lambda i, *_: (jnp.maximum(i - npt, 0), 0)
    grid_spec = pltpu.PrefetchScalarGridSpec(
        num_scalar_prefetch=3,
        grid=(nt,),
        in_specs=[
            pl.BlockSpec(memory_space=pl.ANY),
            pl.BlockSpec((tt, LANES), p_tile), pl.BlockSpec((tt, LANES), s_tile),
            pl.BlockSpec((tt, D_MODEL), p_tile), pl.BlockSpec((tt, D_MODEL), s_tile),
            pl.BlockSpec((None, 6, D_MODEL), lambda i, *_: (jnp.minimum(i, npt - 1) // tiles_per_batch, 0, 0)),
            pl.BlockSpec((tt, D_MODEL), s_tile),
            pl.BlockSpec((1, D_MODEL), lambda i, *_: (0, 0)),
        ],
        out_specs=(pl.BlockSpec((tt, D_MODEL), p_tile), pl.BlockSpec((tt, D_MODEL), s_tile)),
        scratch_shapes=[pltpu.VMEM((2, SORT_ROWS, PACK_W), PACKED), pltpu.SemaphoreType.DMA((2,))],
    )
    return pl.pallas_call(
        functools.partial(_combine_kernel, npt),
        grid_spec=grid_spec,
        out_shape=(jax.ShapeDtypeStruct((npt * tt, D_MODEL), F32),
                   jax.ShapeDtypeStruct(((nt - npt) * tt, D_MODEL), F32)),
        compiler_params=pltpu.CompilerParams(dimension_semantics=("arbitrary",), vmem_limit_bytes=VMEM_LIMIT),
        name="combine",
    )(cnt8, tile_off, tot8, ys, route_p, route_s, x1_p, x1_s, mod_p, g2_rows, final_g)


def _block_diag_halves(wg):
    per_half = LRU_BLOCKS // 2
    w4 = wg.reshape(2, per_half, LRU_BLOCK_W, LRU_BLOCK_W)
    on_diag = jnp.eye(per_half, dtype=bool)[None, :, None, :, None]
    dense = jnp.where(on_diag, w4[:, :, :, None, :], 0.0)
    return dense.reshape(2, per_half * LRU_BLOCK_W, per_half * LRU_BLOCK_W).astype(BF16)


def _rope_tables(pos):
    half = HEAD_DIM // 2
    inv = ROPE_THETA ** (-jnp.arange(half, dtype=F32) / half)
    ang = pos.astype(F32)[:, None] * inv[None, :]
    cos = jnp.cos(ang)
    sin = jnp.sin(ang)
    cos128 = jnp.concatenate([cos, cos, cos, cos], axis=1)
    sin128 = jnp.concatenate([-sin, sin, -sin, sin], axis=1)
    return cos128, sin128


def _resident_spec(arr):
    zeros = (0,) * arr.ndim
    return pl.BlockSpec(arr.shape, lambda i: zeros)


def kernel(x_prompt, x_sample, state_lru_h, state_conv, cache_win_k, cache_win_v, c_prompt, c_sample, w_ada, b_ada, norm1_g, w_in, b_in, conv_w, conv_b, lru_wa, lru_ba, lru_wx, lru_bx, lru_lambda, attn_sinks, w_out, b_out, norm2_g, w_router, b_router, w1, b1, w2, b2, final_g):
    bp, seq, _ = x_prompt.shape
    bd, tdec, _ = x_sample.shape
    assert tdec == SUBLANES and seq % (MIX_TILES * SEQ_TILE) == 0 and bd % SAMPLE_BT == 0
    assert SAMPLE_BT * tdec == SEQ_TILE
    n_prompt = bp * seq
    n_sample = bd * tdec
    n_tok = n_prompt + n_sample
    l = 0

    head_perm = [h for c in range(4) for h in (c, GROUP + c)]
    o2 = 2 * LRU_WIDTH

    def permute_heads(arr, axis, start):
        take = lambda a, b: lax.slice_in_dim(arr, a, b, axis=axis)
        heads = [take(start + h * HEAD_DIM, start + (h + 1) * HEAD_DIM) for h in head_perm]
        return jnp.concatenate([take(0, start)] + heads + [take(start + Q_WIDTH, arr.shape[axis])], axis=axis)

    sinks_perm = attn_sinks[l]

    wr = jnp.tile(w_router[l], (1, TOP_K))
    wr_hi = wr.astype(BF16)
    weights = dict(
        norm1_g=norm1_g[l][None, :], w_in=permute_heads(w_in[l], 1, o2).astype(BF16),
        b_in=permute_heads(b_in[l], 0, o2)[None, :],
        conv_w=conv_w[l], conv_b=conv_b[l][None, :],
        ga=_block_diag_halves(lru_wa[l]), gx=_block_diag_halves(lru_wx[l]),
        lru_ba=lru_ba[l][None, :], lru_bx=lru_bx[l][None, :], lam=lru_lambda[l][None, :],
        w_out=permute_heads(w_out[l], 0, LRU_WIDTH).astype(BF16), b_out=b_out[l][None, :], norm2_g=norm2_g[l][None, :],
        wr_hi=wr_hi, wr_lo=(wr - wr_hi.astype(F32)).astype(BF16),
        b_router=jnp.tile(b_router[l], TOP_K)[None, :],
    )
    wlist = [weights[n] for n in WEIGHT_NAMES]

    mod_p, mod_s = _ada(jnp.concatenate([c_prompt, c_sample], axis=0), bp, w_ada[l], b_ada[l][None, :])
    mod_p = mod_p.reshape(bp, 6, D_MODEL)

    cos_p, sin_p = _rope_tables(jnp.arange(seq, dtype=jnp.int32))
    cos_s, sin_s = _rope_tables(PAST_LEN + jnp.arange(tdec, dtype=jnp.int32))
    cos_s = jnp.tile(cos_s, (SAMPLE_BT, 1))
    sin_s = jnp.tile(sin_s, (SAMPLE_BT, 1))
    h0_rows = jnp.pad(state_lru_h[l][:, None, :], ((0, 0), (0, tdec - 1), (0, 0))).reshape(n_sample, LRU_WIDTH)
    cprev_rows = jnp.pad(state_conv[l], ((0, 0), (0, tdec - (CONV_W - 1)), (0, 0))).reshape(n_sample, LRU_WIDTH)
    ck = cache_win_k[l].reshape(bd, WINDOW, KV_WIDTH)
    cv = cache_win_v[l].reshape(bd, WINDOW, KV_WIDTH)
    npt = n_prompt // SEQ_TILE
    nst = n_sample // SEQ_TILE
    mix_rows = MIX_TILES * SEQ_TILE
    steps_per_seq = seq // mix_rows
    wspecs = [_resident_spec(a) for a in wlist]

    rows_p = lambda width: pl.BlockSpec((mix_rows, width), lambda i: (i, 0))
    tail_p = lambda rows, width: pl.BlockSpec((None, rows, width), lambda i: (i // steps_per_seq, 0, 0))
    (x1_p, h2_p, route_p, cnt_p, hlast_p, ulast_p, klast_p, vlast_p) = pl.pallas_call(
        functools.partial(_prompt_kernel, steps_per_seq),
        grid=(n_prompt // mix_rows,),
        in_specs=[rows_p(D_MODEL),
                  pl.BlockSpec((None, 6, D_MODEL), lambda i: (i // steps_per_seq, 0, 0)),
                  pl.BlockSpec((mix_rows, LANES), lambda i: (i % steps_per_seq, 0)),
                  pl.BlockSpec((mix_rows, LANES), lambda i: (i % steps_per_seq, 0)),
                  pl.BlockSpec(memory_space=pltpu.SMEM)] + wspecs,
        out_specs=(rows_p(D_MODEL), rows_p(D_MODEL), rows_p(LANES),
                   pl.BlockSpec((MIX_TILES, SUBLANES, LANES), lambda i: (i, 0, 0)),
                   tail_p(SUBLANES, LRU_WIDTH), tail_p(SUBLANES, LRU_WIDTH),
                   tail_p(WINDOW, KV_WIDTH), tail_p(WINDOW, KV_WIDTH)),
        out_shape=(
            jax.ShapeDtypeStruct((n_prompt, D_MODEL), F32),
            jax.ShapeDtypeStruct((n_prompt, D_MODEL), BF16),
            jax.ShapeDtypeStruct((n_prompt, LANES), F32),
            jax.ShapeDtypeStruct((npt, SUBLANES, LANES), F32),
            jax.ShapeDtypeStruct((bp, SUBLANES, LRU_WIDTH), F32),
            jax.ShapeDtypeStruct((bp, SUBLANES, LRU_WIDTH), F32),
            jax.ShapeDtypeStruct((bp, WINDOW, KV_WIDTH), F32),
            jax.ShapeDtypeStruct((bp, WINDOW, KV_WIDTH), F32),
        ),
        scratch_shapes=[pltpu.VMEM((SUBLANES, LRU_WIDTH), F32), pltpu.VMEM((SUBLANES, LRU_WIDTH), F32),
                        pltpu.VMEM((WINDOW, KV_WIDTH), F32), pltpu.VMEM((WINDOW, KV_WIDTH), F32)],
        compiler_params=pltpu.CompilerParams(dimension_semantics=("arbitrary",), vmem_limit_bytes=VMEM_LIMIT),
        name="prompt_mixer",
    )(x_prompt.reshape(n_prompt, D_MODEL), mod_p, cos_p, sin_p, sinks_perm, *wlist)

    rows_s = lambda width: pl.BlockSpec((SEQ_TILE, width), lambda i: (i, 0))
    cache_spec = pl.BlockSpec((SAMPLE_BT, WINDOW, KV_WIDTH), lambda i: (i, 0, 0))
    (x1_s, h2_s, route_s, cnt_s, g2_rows, hs_s, u_s, s_k, s_v) = pl.pallas_call(
        _sample_kernel,
        grid=(nst,),
        in_specs=[rows_s(D_MODEL),
                  pl.BlockSpec((6, SAMPLE_BT, D_MODEL), lambda i: (0, i, 0)),
                  pl.BlockSpec((SEQ_TILE, LANES), lambda i: (0, 0)),
                  pl.BlockSpec((SEQ_TILE, LANES), lambda i: (0, 0)),
                  pl.BlockSpec(memory_space=pltpu.SMEM),
                  rows_s(LRU_WIDTH), rows_s(LRU_WIDTH), cache_spec, cache_spec] + wspecs,
        out_specs=(rows_s(D_MODEL), rows_s(D_MODEL), rows_s(LANES),
                   pl.BlockSpec((None, SUBLANES, LANES), lambda i: (i, 0, 0)),
                   rows_s(D_MODEL), rows_s(LRU_WIDTH), rows_s(LRU_WIDTH), cache_spec, cache_spec),
        out_shape=(
            jax.ShapeDtypeStruct((n_sample, D_MODEL), F32), jax.ShapeDtypeStruct((n_sample, D_MODEL), BF16),
            jax.ShapeDtypeStruct((n_sample, LANES), F32), jax.ShapeDtypeStruct((nst, SUBLANES, LANES), F32),
            jax.ShapeDtypeStruct((n_sample, D_MODEL), F32),
            jax.ShapeDtypeStruct((n_sample, LRU_WIDTH), F32),
            jax.ShapeDtypeStruct((n_sample, LRU_WIDTH), F32),
            jax.ShapeDtypeStruct((bd, WINDOW, KV_WIDTH), F32),
            jax.ShapeDtypeStruct((bd, WINDOW, KV_WIDTH), F32),
        ),
        compiler_params=pltpu.CompilerParams(dimension_semantics=("arbitrary",), vmem_limit_bytes=VMEM_LIMIT),
        name="sample_mixer",
    )(x_sample.reshape(n_sample, D_MODEL), mod_s, cos_s, sin_s, sinks_perm, h0_rows, cprev_rows, ck, cv, *wlist)
    tile_cnt = jnp.concatenate([cnt_p, cnt_s], axis=0)

    n_tiles = npt + nst
    n_assign = n_tok * TOP_K
    max_rows = n_assign + n_tiles * N_EXPERTS * (SUBLANES - 1) + N_EXPERTS * (MOE_TM - 1)
    n_blocks = -(-max_rows // MOE_TM)
    cnt = tile_cnt[:, 0, :N_EXPERTS].astype(jnp.int32)
    cnt8 = (cnt + SUBLANES - 1) // SUBLANES * SUBLANES
    counts = jnp.sum(cnt8, axis=0)
    pcounts = (counts + MOE_TM - 1) // MOE_TM * MOE_TM
    pend = jnp.cumsum(pcounts)
    pstart = pend - pcounts
    tile_off = pstart[None, :] + jnp.cumsum(cnt8, axis=0) - cnt8
    tot8 = jnp.sum(cnt8, axis=1)
    meta = jnp.concatenate([pstart + counts, pend, pend[-1:] // MOE_TM]).astype(jnp.int32)
    cnt8_flat = cnt8.reshape(-1)
    off_flat = tile_off.reshape(-1).astype(jnp.int32)

    xs = _dispatch(cnt8_flat, off_flat, tot8, meta, h2_p, h2_s, route_p, route_s, n_blocks * MOE_TM)

    big, small = MOE_CHUNK_CODES[0], MOE_CHUNK_CODES[1:]
    n_tm = pcounts // MOE_TM
    n_big = n_tm // big
    present = [(n_tm // b) % 2 for b in small]
    n_ch = n_big + sum(present)
    ch_end = jnp.cumsum(n_ch)
    ch_start = ch_end - n_ch
    n_chunks = n_blocks // big + len(small) * N_EXPERTS
    ci = jnp.arange(n_chunks, dtype=jnp.int32)
    owner = jnp.logical_and(ch_start[None, :] <= ci[:, None], ci[:, None] < ch_end[None, :])
    pick = lambda v: jnp.sum(jnp.where(owner, v[None, :], 0), axis=1)
    live = ci < ch_end[-1]
    local = ci - pick(ch_start)
    chunk_exp = jnp.where(live, pick(jnp.arange(N_EXPERTS, dtype=jnp.int32)), N_EXPERTS - 1).astype(jnp.int32)
    big_c = pick(n_big)
    code = jnp.where(local < big_c, big, 0)
    blocks_before = jnp.where(local < big_c, local * big, 0)
    order = big_c
    blocks = big_c * big
    for b, has in zip(small, present):
        has_c = pick(has)
        is_b = jnp.logical_and(has_c == 1, local == order)
        code = jnp.where(is_b, b, code)
        blocks_before = jnp.where(is_b, blocks, blocks_before)
        order = order + has_c
        blocks = blocks + has_c * b
    chunk_row = jnp.where(live, pick(pstart) + blocks_before * MOE_TM, 0).astype(jnp.int32)
    chunk_size = jnp.where(live, code, 0).astype(jnp.int32)
    has_rows = n_ch > 0
    e_ar = jnp.arange(N_EXPERTS, dtype=jnp.int32)
    w_slot = (jnp.cumsum(has_rows.astype(jnp.int32)) - 1) % 2
    later = jnp.logical_and(has_rows[None, :], e_ar[None, :] > e_ar[:, None])
    nxt = jnp.min(jnp.where(later, e_ar[None, :], N_EXPERTS), axis=1)
    nxt = jnp.where(nxt == N_EXPERTS, -1, nxt)
    weight_tbl = jnp.concatenate([jnp.logical_and(live, local == 0).astype(jnp.int32),
                                  pick(w_slot), jnp.where(live, pick(nxt), -1)]).astype(jnp.int32)
    moe_meta = jnp.concatenate([meta, ch_end[-1:].astype(jnp.int32)])
    ys = _moe(chunk_row, chunk_size, chunk_exp, weight_tbl, moe_meta, xs, w1[l], b1[l][:, None, :], w2[l], b2[l][:, None, :])

    y_p, y_s = _combine(cnt8_flat, off_flat, tot8, ys, route_p, route_s, x1_p, x1_s, mod_p, g2_rows,
                        final_g[None, :], seq // SEQ_TILE)

    y_prompt = y_p.reshape(bp, seq, D_MODEL)
    y_sample = y_s.reshape(bd, tdec, D_MODEL)
    p_h = hlast_p[:, SUBLANES - 1, :][None]
    p_c = ulast_p[:, SUBLANES - (CONV_W - 1):, :][None]
    p_k = klast_p.reshape(1, bp, WINDOW, N_KV_HEADS, HEAD_DIM)
    p_v = vlast_p.reshape(1, bp, WINDOW, N_KV_HEADS, HEAD_DIM)
    s_h = hs_s.reshape(bd, tdec, LRU_WIDTH)[:, tdec - 1, :][None]
    s_c = u_s.reshape(bd, tdec, LRU_WIDTH)[:, tdec - (CONV_W - 1):, :][None]
    s_kk = s_k.reshape(1, bd, WINDOW, N_KV_HEADS, HEAD_DIM)
    s_vv = s_v.reshape(1, bd, WINDOW, N_KV_HEADS, HEAD_DIM)
    return (y_prompt, y_sample, p_h, p_c, p_k, p_v, s_h, s_c, s_kk, s_vv)
```

```python
import functools

import jax
import jax.numpy as jnp
from jax import lax
from jax.experimental import pallas as pl
from jax.experimental.pallas import tpu as pltpu

F32 = jnp.float32
BF16 = jnp.bfloat16

D_MODEL = 1024
LRU_WIDTH = 512
LRU_BLOCKS = 8
LRU_BLOCK_W = LRU_WIDTH // LRU_BLOCKS
CONV_W = 4
LRU_C = 8.0
HEAD_DIM = 64
N_HEADS = 8
N_KV_HEADS = 2
GROUP = N_HEADS // N_KV_HEADS
WINDOW = 128
ROPE_THETA = 10000.0
N_EXPERTS = 32
TOP_K = 4
D_FF = D_MODEL
SWIGLU_LIMIT = 7.0
SWIGLU_ALPHA = 1.702
NORM_EPS = 1e-5
PAST_LEN = 8192
Q_WIDTH = N_HEADS * HEAD_DIM
KV_WIDTH = N_KV_HEADS * HEAD_DIM
IN_WIDTH = 2 * LRU_WIDTH + Q_WIDTH + 2 * KV_WIDTH

LANES = 128
SUBLANES = 8
SEQ_TILE = 256
MIX_TILES = 2
SAMPLE_BT = 32
MOE_TM = 256
MOE_CHUNK_CODES = (4, 2, 1)
MOE_CH = MOE_CHUNK_CODES[0] * MOE_TM
NEG_BIG = -1e30
VMEM_LIMIT = 56 * 1024 * 1024


def _rms(x, g):
    return x * lax.rsqrt(jnp.mean(x * x, axis=-1, keepdims=True) + NORM_EPS) * g


def _rms_mod(x, gain, shift):
    return x * lax.rsqrt(jnp.mean(x * x, axis=-1, keepdims=True) + NORM_EPS) * gain + shift


def _group_scan(a, b):
    rows, width = a.shape
    groups = rows // SUBLANES
    a3 = a.reshape(groups, SUBLANES, width)
    b3 = b.reshape(groups, SUBLANES, width)
    t = lax.broadcasted_iota(jnp.int32, (1, SUBLANES, 1), 1)
    d = 1
    while d < SUBLANES:
        keep = t >= d
        a_s = jnp.where(keep, pltpu.roll(a3, d, axis=1), 1.0)
        b_s = jnp.where(keep, pltpu.roll(b3, d, axis=1), 0.0)
        b3 = a3 * b_s + b3
        a3 = a3 * a_s
        d *= 2
    return a3.reshape(rows, width), b3.reshape(rows, width)


def _chain_groups(a_grp, b_grp, h_in):
    rows = a_grp.shape[0]
    out = []
    carry = h_in
    for g in range(rows // SUBLANES):
        sl = slice(g * SUBLANES, (g + 1) * SUBLANES)
        hg = b_grp[sl] + a_grp[sl] * carry
        out.append(hg)
        carry = hg[SUBLANES - 1:SUBLANES]
    return jnp.concatenate(out, axis=0)


def _rope128(x, cos, sin_signed, first_half):
    sw = jnp.where(first_half, pltpu.roll(x, LANES - HEAD_DIM // 2, axis=1), pltpu.roll(x, HEAD_DIM // 2, axis=1))
    return x * cos + sw * sin_signed


def _softplus(x):
    return jnp.maximum(x, 0.0) + jnp.log1p(jnp.exp(-jnp.abs(x)))


def _lru_coeffs(uc, w, first_pos_mask):
    ub = uc.astype(BF16)
    half = LRU_WIDTH // 2
    ra = jnp.concatenate([jnp.dot(ub[:, :half], w['ga'][0], preferred_element_type=F32),
                          jnp.dot(ub[:, half:], w['ga'][1], preferred_element_type=F32)], axis=1)
    rx = jnp.concatenate([jnp.dot(ub[:, :half], w['gx'][0], preferred_element_type=F32),
                          jnp.dot(ub[:, half:], w['gx'][1], preferred_element_type=F32)], axis=1)
    r = jax.nn.sigmoid(ra + w['lru_ba'][...])
    i = jax.nn.sigmoid(rx + w['lru_bx'][...])
    log_a = -LRU_C * r * _softplus(-w['lam'][...])
    a = jnp.exp(log_a)
    om = 1.0 - a * a
    mult = jnp.where(om > 0.0, om * lax.rsqrt(om), 0.0)
    if first_pos_mask is not None:
        mult = jnp.where(first_pos_mask, 1.0, mult)
    return a, mult * i * uc


def _conv_taps(u, s1, s2, s3, w):
    cw = w['conv_w']
    return w['conv_b'][...] + s3 * cw[0:1, :] + s2 * cw[1:2, :] + s1 * cw[2:3, :] + u * cw[3:4, :]


def _in_proj(x, mod, w):
    sh1, sc1 = mod
    h = _rms_mod(x, w['norm1_g'][...] * (1.0 + sc1), sh1)
    return jnp.dot(h.astype(BF16), w['w_in'][...], preferred_element_type=F32) + w['b_in'][...]


def _post_mix(x, mix, mod, w):
    g1, sh2, sc2 = mod
    x1 = x + g1 * (jnp.dot(mix.astype(BF16), w['w_out'][...], preferred_element_type=F32) + w['b_out'][...])
    h2 = _rms_mod(x1, w['norm2_g'][...] * (1.0 + sc2), sh2)
    h2_hi = h2.astype(BF16)
    h2_lo = (h2 - h2_hi.astype(F32)).astype(BF16)
    logits = (jnp.dot(h2_hi, w['wr_hi'][...], preferred_element_type=F32)
              + jnp.dot(h2_lo, w['wr_hi'][...], preferred_element_type=F32)
              + jnp.dot(h2_hi, w['wr_lo'][...], preferred_element_type=F32)) + w['b_router'][...]
    return x1, h2_hi, logits


ROUTE_E, ROUTE_G, ROUTE_R = 0, TOP_K, 2 * TOP_K


def _lane_roll1(v, shift):
    return pltpu.roll(jnp.broadcast_to(v, (SUBLANES, LANES)), shift, axis=1)[0:1]


def _route_tile(lg):
    rows = lg.shape[0]
    lane = lax.broadcasted_iota(jnp.int32, (1, LANES), 1)
    e_of = lane % N_EXPERTS
    grp = lane // N_EXPERTS
    e_id = e_of.astype(F32)
    onehot = jnp.zeros((rows, LANES), F32)
    vals, ids = [], []
    for k in range(TOP_K):
        m = jnp.max(lg, axis=1, keepdims=True)
        idx = jnp.min(jnp.where(lg == m, e_id, float(N_EXPERTS)), axis=1, keepdims=True)
        sel = e_id == idx
        lg = jnp.where(sel, -jnp.inf, lg)
        onehot = jnp.where(jnp.logical_and(sel, grp == k), 1.0, onehot)
        vals.append(m)
        ids.append(idx)
    ex = [jnp.exp(v - vals[0]) for v in vals]
    denom = ex[0] + ex[1] + ex[2] + ex[3]

    r_i = lax.broadcasted_iota(jnp.int32, (rows, rows), 0)
    c_i = lax.broadcasted_iota(jnp.int32, (rows, rows), 1)
    strict_lower = jnp.where(r_i > c_i, 1.0, 0.0).astype(BF16)
    prefix = jnp.dot(strict_lower, onehot.astype(BF16), preferred_element_type=F32)
    cnt = jnp.sum(onehot, axis=0, keepdims=True)
    base = jnp.zeros((1, LANES), F32)
    tot = cnt
    for s in range(1, TOP_K):
        rolled = _lane_roll1(cnt, s * N_EXPERTS)
        base = base + jnp.where(lane >= s * N_EXPERTS, rolled, 0.0)
        tot = tot + rolled
    pad_cnt = jnp.floor((tot + (SUBLANES - 1.0)) * (1.0 / SUBLANES)) * SUBLANES
    inc = pad_cnt
    d = 1
    while d < N_EXPERTS:
        inc = inc + jnp.where(e_of >= d, _lane_roll1(inc, d), 0.0)
        d *= 2
    strip_start = inc - pad_cnt
    ranked = onehot * (prefix + base + strip_start)

    route = jnp.zeros((rows, LANES), F32)
    for k in range(TOP_K):
        rank_k = jnp.sum(jnp.where(grp == k, ranked, 0.0), axis=1, keepdims=True)
        route = jnp.where(lane == ROUTE_E + k, ids[k], route)
        route = jnp.where(lane == ROUTE_G + k, ex[k] / denom, route)
        route = jnp.where(lane == ROUTE_R + k, rank_k, route)
    return route, tot


def _softmax_with_sink(s, sink_col):
    m = jnp.maximum(jnp.max(s, axis=-1, keepdims=True), sink_col)
    p = jnp.exp(s - m)
    denom = jnp.sum(p, axis=-1, keepdims=True) + jnp.exp(sink_col - m)
    return p, denom


WEIGHT_NAMES = ('norm1_g', 'w_in', 'b_in', 'conv_w', 'conv_b', 'ga', 'gx', 'lru_ba', 'lru_bx', 'lam',
                'w_out', 'b_out', 'norm2_g', 'wr_hi', 'wr_lo', 'b_router')


def _split_bf16(x):
    hi = x.astype(BF16)
    return hi, (x - hi.astype(F32)).astype(BF16)


def _ada_kernel(n_prompt_rows, c_ref, w_ref, b_ref, op_ref, os_ref):
    c = c_ref[...]
    s_hi, s_lo = _split_bf16(c * jax.nn.sigmoid(c))
    w_hi, w_lo = _split_bf16(w_ref[...])
    mod = (jnp.dot(s_hi, w_hi, preferred_element_type=F32) + jnp.dot(s_lo, w_hi, preferred_element_type=F32)
           + jnp.dot(s_hi, w_lo, preferred_element_type=F32)) + b_ref[...]
    op_ref[...] = mod[:n_prompt_rows]
    os_ref[...] = mod[n_prompt_rows:]


def _ada(c_all, n_prompt_rows, w_ada, b_ada):
    rows = c_all.shape[0]
    n_s = rows - n_prompt_rows
    assert n_prompt_rows % SUBLANES == 0
    return pl.pallas_call(
        functools.partial(_ada_kernel, n_prompt_rows),
        grid=(6,),
        in_specs=[pl.BlockSpec((rows, D_MODEL), lambda i: (0, 0)),
                  pl.BlockSpec((D_MODEL, D_MODEL), lambda i: (0, i)),
                  pl.BlockSpec((1, D_MODEL), lambda i: (0, i))],
        out_specs=(pl.BlockSpec((n_prompt_rows, D_MODEL), lambda i: (0, i)),
                   pl.BlockSpec((None, n_s, D_MODEL), lambda i: (i, 0, 0))),
        out_shape=(jax.ShapeDtypeStruct((n_prompt_rows, 6 * D_MODEL), F32),
                   jax.ShapeDtypeStruct((6, n_s, D_MODEL), F32)),
        compiler_params=pltpu.CompilerParams(dimension_semantics=("arbitrary",), vmem_limit_bytes=VMEM_LIMIT),
        name="ada",
    )(c_all, w_ada, b_ada)


def _prompt_body(seq_start, x_ref, mod_ref, cos_ref, sin_ref, sinks_ref, w, x1_ref, h2_ref, lg_ref,
                 hlast_ref, ulast_ref, klast_ref, vlast_ref, conv_c, h_c, k_c, v_c):
    ts = SEQ_TILE

    if seq_start is not None:
        @pl.when(seq_start)
        def _():
            conv_c[...] = jnp.zeros_like(conv_c)
            h_c[...] = jnp.zeros_like(h_c)
            k_c[...] = jnp.zeros_like(k_c)
            v_c[...] = jnp.zeros_like(v_c)

    x = x_ref[...]
    mod = mod_ref[...]
    proj = _in_proj(x, (mod[0:1], mod[1:2]), w)
    u = proj[:, :LRU_WIDTH]
    gate = proj[:, LRU_WIDTH:2 * LRU_WIDTH]
    o2 = 2 * LRU_WIDTH

    rowid = lax.broadcasted_iota(jnp.int32, (ts, 1), 0)
    u_ext = jnp.concatenate([conv_c[...], u], axis=0)
    s1, s2, s3 = (pltpu.roll(u_ext, d, axis=0)[SUBLANES:] for d in (1, 2, 3))
    uc = _conv_taps(u, s1, s2, s3, w)
    conv_c[...] = u[ts - SUBLANES:]
    ulast_ref[...] = u[ts - SUBLANES:]

    first_pos = None if seq_start is None else jnp.logical_and(rowid == 0, seq_start)
    a, bt = _lru_coeffs(uc, w, first_pos)
    hs = _chain_groups(*_group_scan(a, bt), h_c[0:1, :])
    h_tail = hs[ts - SUBLANES:]
    h_c[...] = jnp.broadcast_to(h_tail[SUBLANES - 1:SUBLANES, :], h_c.shape)
    hlast_ref[...] = h_tail
    lru_out = hs * jax.nn.gelu(gate)

    cos = cos_ref[...]
    sin = sin_ref[...]
    lane = lax.broadcasted_iota(jnp.int32, (1, LANES), 1)
    first_half = (lane % HEAD_DIM) < (HEAD_DIM // 2)
    qcols = [_rope128(proj[:, o2 + c * LANES:o2 + (c + 1) * LANES], cos, sin, first_half) * (HEAD_DIM ** -0.5)
             for c in range(4)]
    k = _rope128(proj[:, o2 + Q_WIDTH:o2 + Q_WIDTH + KV_WIDTH], cos, sin, first_half)
    v = proj[:, o2 + Q_WIDTH + KV_WIDTH:]
    k_ext = jnp.concatenate([k_c[...], k], axis=0).astype(BF16)
    v_ext = jnp.concatenate([v_c[...], v], axis=0).astype(BF16)
    k_c[...] = k[ts - WINDOW:]
    v_c[...] = v[ts - WINDOW:]
    klast_ref[...] = k[ts - WINDOW:]
    vlast_ref[...] = v[ts - WINDOW:]

    grow = lax.broadcasted_iota(jnp.int32, (GROUP * WINDOW, 1), 0)
    from_prev = lax.broadcasted_iota(jnp.int32, (GROUP * WINDOW, WINDOW), 1) > grow % WINDOW
    grow = grow // WINDOW
    lane_lo = lane < HEAD_DIM
    attn_cols = [[] for _ in range(4)]
    for blk in range(ts // WINDOW):
        kb = k_ext[blk * WINDOW:(blk + 2) * WINDOW]
        v_prev = v_ext[blk * WINDOW:(blk + 1) * WINDOW]
        v_own = v_ext[(blk + 1) * WINDOW:(blk + 2) * WINDOW]
        outs = []
        for kv in range(N_KV_HEADS):
            sel = lane_lo if kv == 0 else jnp.logical_not(lane_lo)
            qs = jnp.concatenate(
                [jnp.where(sel, qc[blk * WINDOW:(blk + 1) * WINDOW], 0.0) for qc in qcols], axis=0).astype(BF16)
            s = lax.dot_general(qs, kb, (((1,), (1,)), ((), ())), preferred_element_type=F32)
            s_prev = s[:, :WINDOW]
            if seq_start is not None and blk == 0:
                s_prev = jnp.where(seq_start, NEG_BIG, s_prev)
            s = jnp.where(from_prev, s_prev, s[:, WINDOW:])
            sink_col = jnp.zeros((GROUP * WINDOW, 1), F32)
            for g in range(GROUP):
                sink_col = jnp.where(grow == g, sinks_ref[kv * GROUP + g], sink_col)
            p, denom = _softmax_with_sink(s, sink_col)
            pv = (jnp.dot(jnp.where(from_prev, p, 0.0).astype(BF16), v_prev, preferred_element_type=F32)
                  + jnp.dot(jnp.where(from_prev, 0.0, p).astype(BF16), v_own, preferred_element_type=F32))
            outs.append(pv / denom)
        for c in range(4):
            attn_cols[c].append(jnp.where(lane_lo, outs[0][c * WINDOW:(c + 1) * WINDOW],
                                          outs[1][c * WINDOW:(c + 1) * WINDOW]))
    attn = jnp.concatenate([jnp.concatenate(cols, axis=0) for cols in attn_cols], axis=1)

    mix = jnp.concatenate([lru_out, attn], axis=1)
    x1, h2, logits = _post_mix(x, mix, (mod[2:3], mod[3:4], mod[4:5]), w)
    x1_ref[...] = x1
    h2_ref[...] = h2
    route, tot = _route_tile(logits)
    lg_ref[0][...] = route
    lg_ref[1][...] = jnp.broadcast_to(tot, lg_ref[1].shape)


def _expand_rows(m, t):
    b, wd = m.shape
    return jnp.broadcast_to(m[:, None, :], (b, t, wd)).reshape(b * t, wd)


def _sample_body(x_ref, mod_ref, cos_ref, sin_ref, sinks_ref, h0_ref, cprev_ref, ck_ref, cv_ref, w,
                 x1_ref, h2_ref, lg_ref, g2_ref, hs_ref, u_ref, ko_ref, vo_ref):
    bt_, t = SAMPLE_BT, SUBLANES
    rows = bt_ * t

    x = x_ref[...]
    mods = [_expand_rows(mod_ref[i], t) for i in range(6)]
    proj = _in_proj(x, (mods[0], mods[1]), w)
    u = proj[:, :LRU_WIDTH]
    gate = proj[:, LRU_WIDTH:2 * LRU_WIDTH]
    o2 = 2 * LRU_WIDTH
    u_ref[...] = u

    rowid = lax.broadcasted_iota(jnp.int32, (rows, 1), 0) % t
    cprev = cprev_ref[...]
    taps = []
    for d in (1, 2, 3):
        taps.append(jnp.where(rowid >= d, pltpu.roll(u, d, axis=0),
                              pltpu.roll(cprev, (d - (CONV_W - 1)) % rows, axis=0)))
    uc = _conv_taps(u, taps[0], taps[1], taps[2], w)

    a, bt = _lru_coeffs(uc, w, None)
    bt = bt + a * h0_ref[...]
    _, hs = _group_scan(a, bt)
    hs_ref[...] = hs
    lru_out = hs * jax.nn.gelu(gate)

    cos = cos_ref[...]
    sin = sin_ref[...]
    lane = lax.broadcasted_iota(jnp.int32, (1, LANES), 1)
    first_half = (lane % HEAD_DIM) < (HEAD_DIM // 2)
    qcols = [_rope128(proj[:, o2 + c * LANES:o2 + (c + 1) * LANES], cos, sin, first_half) * (HEAD_DIM ** -0.5)
             for c in range(4)]
    k = _rope128(proj[:, o2 + Q_WIDTH:o2 + Q_WIDTH + KV_WIDTH], cos, sin, first_half)
    v = proj[:, o2 + Q_WIDTH + KV_WIDTH:]
    k3 = k.reshape(bt_, t, KV_WIDTH)
    v3 = v.reshape(bt_, t, KV_WIDTH)
    ck = ck_ref[...]
    cv = cv_ref[...]
    ko_ref[:, :WINDOW - t, :] = ck[:, t:, :]
    ko_ref[:, WINDOW - t:, :] = k3
    vo_ref[:, :WINDOW - t, :] = cv[:, t:, :]
    vo_ref[:, WINDOW - t:, :] = v3

    ckb, cvb, k3b, v3b = ck.astype(BF16), cv.astype(BF16), k3.astype(BF16), v3.astype(BF16)
    lane_lo = lane < HEAD_DIM
    gq = GROUP * t
    tq = lax.broadcasted_iota(jnp.int32, (1, gq, 1), 1) % t
    mask_c = lax.broadcasted_iota(jnp.int32, (1, gq, WINDOW), 2) > tq
    mask_n = lax.broadcasted_iota(jnp.int32, (1, gq, t), 2) <= tq
    grow = lax.broadcasted_iota(jnp.int32, (1, gq, 1), 1) // t
    bdims = (((2,), (2,)), ((0,), (0,)))
    pdims = (((2,), (1,)), ((0,), (0,)))
    outs = []
    for kv in range(N_KV_HEADS):
        sel = lane_lo if kv == 0 else jnp.logical_not(lane_lo)
        q3 = jnp.concatenate([jnp.where(sel, qc, 0.0).reshape(bt_, t, LANES) for qc in qcols], axis=1).astype(BF16)
        sc = lax.dot_general(q3, ckb, bdims, preferred_element_type=F32)
        sn = lax.dot_general(q3, k3b, bdims, preferred_element_type=F32)
        sc = jnp.where(mask_c, sc, NEG_BIG)
        sn = jnp.where(mask_n, sn, NEG_BIG)
        sink_col = jnp.zeros((1, gq, 1), F32)
        for g in range(GROUP):
            sink_col = jnp.where(grow == g, sinks_ref[kv * GROUP + g], sink_col)
        m = jnp.maximum(jnp.maximum(jnp.max(sc, axis=-1, keepdims=True), jnp.max(sn, axis=-1, keepdims=True)),
                        sink_col)
        pc = jnp.exp(sc - m)
        pn = jnp.exp(sn - m)
        denom = jnp.sum(pc, axis=-1, keepdims=True) + jnp.sum(pn, axis=-1, keepdims=True) + jnp.exp(sink_col - m)
        o = (lax.dot_general(pc.astype(BF16), cvb, pdims, preferred_element_type=F32)
             + lax.dot_general(pn.astype(BF16), v3b, pdims, preferred_element_type=F32)) / denom
        outs.append(o)
    attn = jnp.concatenate(
        [jnp.where(lane_lo, outs[0][:, c * t:(c + 1) * t, :], outs[1][:, c * t:(c + 1) * t, :]).reshape(rows, LANES)
         for c in range(4)], axis=1)

    mix = jnp.concatenate([lru_out, attn], axis=1)
    x1, h2, logits = _post_mix(x, mix, (mods[2], mods[3], mods[4]), w)
    x1_ref[...] = x1
    h2_ref[...] = h2
    route, tot = _route_tile(logits)
    lg_ref[0][...] = route
    lg_ref[1][...] = jnp.broadcast_to(tot, lg_ref[1].shape)
    g2_ref[...] = mods[5]


def _prompt_kernel(steps_per_seq, x_ref, mod_ref, cos_ref, sin_ref, sinks_ref, *rest):
    nw = len(WEIGHT_NAMES)
    w = dict(zip(WEIGHT_NAMES, rest[:nw]))
    (x1_ref, h2_ref, route_ref, cnt_ref, hlast_ref, ulast_ref, klast_ref, vlast_ref,
     conv_c, h_c, k_c, v_c) = rest[nw:]
    seq_start = pl.program_id(0) % steps_per_seq == 0
    for sub in range(MIX_TILES):
        rows = pl.ds(sub * SEQ_TILE, SEQ_TILE)
        _prompt_body(seq_start if sub == 0 else None, x_ref.at[rows], mod_ref, cos_ref.at[rows], sin_ref.at[rows],
                     sinks_ref, w, x1_ref.at[rows], h2_ref.at[rows], (route_ref.at[rows], cnt_ref.at[sub]),
                     hlast_ref, ulast_ref, klast_ref, vlast_ref, conv_c, h_c, k_c, v_c)


def _sample_kernel(x_ref, mod_ref, cos_ref, sin_ref, sinks_ref, h0_ref, cprev_ref, ck_ref, cv_ref, *rest):
    nw = len(WEIGHT_NAMES)
    w = dict(zip(WEIGHT_NAMES, rest[:nw]))
    x1_ref, h2_ref, route_ref, cnt_ref, g2_ref, hs_ref, u_ref, ko_ref, vo_ref = rest[nw:]
    _sample_body(x_ref, mod_ref, cos_ref, sin_ref, sinks_ref, h0_ref, cprev_ref, ck_ref, cv_ref, w,
                 x1_ref, h2_ref, (route_ref, cnt_ref), g2_ref, hs_ref, u_ref, ko_ref, vo_ref)


PACK_W = D_MODEL // 2
PACKED = jnp.int32


def _pack_bf16_pairs(x):
    return pltpu.pack_elementwise([x[:, :PACK_W], x[:, PACK_W:]], packed_dtype=BF16)


def _unpack_bf16_pairs(w):
    return tuple(pltpu.unpack_elementwise(w, index=k, packed_dtype=BF16, unpacked_dtype=F32).astype(BF16)
                 for k in range(2))


STRIP_SIZES = tuple(SUBLANES << b for b in range(6))
STRIP_LARGE = 64
SORT_ROWS = SEQ_TILE * TOP_K + N_EXPERTS * SUBLANES
TILE_WAIT_SIZES = tuple(SUBLANES << b for b in range(8))


def _for_strips(cnt_ref, off_ref, tile, buf_slot, hbm, sem, to_hbm, act):
    def e_body(e, local):
        n = cnt_ref[tile * N_EXPERTS + e]
        glob = off_ref[tile * N_EXPERTS + e]

        def pieces(sizes, done):
            for p in sizes:
                piece = n & p
                lo = pl.ds(pl.multiple_of(local + done, SUBLANES), p)
                gl = pl.ds(pl.multiple_of(glob + done, SUBLANES), p)

                @pl.when(piece != 0)
                def _():
                    if to_hbm:
                        act(pltpu.make_async_copy(buf_slot.at[lo], hbm.at[gl], sem))
                    else:
                        act(pltpu.make_async_copy(hbm.at[gl], buf_slot.at[lo], sem))
                done = done + piece

        large = tuple(p for p in reversed(STRIP_SIZES) if p >= STRIP_LARGE)
        small = tuple(p for p in reversed(STRIP_SIZES) if p < STRIP_LARGE)
        n_large = n & (-STRIP_LARGE)

        @pl.when(n_large != 0)
        def _():
            pieces(large, 0)
        pieces(small, n_large)
        return local + n
    lax.fori_loop(0, N_EXPERTS, e_body, 0)


def _wait_tile_rows(total, buf_slot, hbm, sem, to_hbm):
    for p in TILE_WAIT_SIZES:
        @pl.when((total & p) != 0)
        def _():
            if to_hbm:
                pltpu.make_async_copy(buf_slot.at[pl.ds(0, p)], hbm.at[pl.ds(0, p)], sem).wait()
            else:
                pltpu.make_async_copy(hbm.at[pl.ds(0, p)], buf_slot.at[pl.ds(0, p)], sem).wait()


def _dispatch_kernel(n_prompt_tiles, cnt_ref, off_ref, tot_ref, meta_ref, h2p_ref, h2s_ref, routep_ref, routes_ref,
                     xs_hbm, sbuf, zblk, sem, zsem):
    i = pl.program_id(0)
    nb = pl.num_programs(0)
    slot = i % 2
    n_blocks = xs_hbm.shape[0] // MOE_TM

    is_prompt = i < n_prompt_tiles
    h2 = jnp.where(is_prompt, h2p_ref[...], h2s_ref[...])
    route_t = jnp.where(is_prompt, routep_ref[...], routes_ref[...]).T
    r_pos = lax.broadcasted_iota(jnp.int16, (SORT_ROWS, SEQ_TILE), 0)
    perm = jnp.zeros((SORT_ROWS, SEQ_TILE), BF16)
    for k in range(TOP_K):
        pos_k = route_t[ROUTE_R + k:ROUTE_R + k + 1, :].astype(jnp.int32).astype(jnp.int16)
        perm = jnp.where(r_pos == pos_k, jnp.ones((), BF16), perm)
    sbuf[slot] = _pack_bf16_pairs(jnp.dot(perm, h2, preferred_element_type=F32))

    _for_strips(cnt_ref, off_ref, i, sbuf.at[slot], xs_hbm, sem.at[slot], True, lambda cp: cp.start())

    @pl.when(i > 0)
    def _():
        _wait_tile_rows(tot_ref[jnp.maximum(i - 1, 0)], sbuf.at[1 - slot], xs_hbm, sem.at[1 - slot], True)

    @pl.when(i == nb - 1)
    def _():
        _wait_tile_rows(tot_ref[i], sbuf.at[slot], xs_hbm, sem.at[slot], True)
        zblk[...] = jnp.zeros_like(zblk)

        def for_region_tails(act):
            def e_body(e, carry):
                start = meta_ref[e]
                n = meta_ref[N_EXPERTS + e] - start
                done = 0
                for p in reversed([q for q in STRIP_SIZES if q < MOE_TM]):
                    piece = n & p
                    rows = pl.ds(pl.multiple_of(start + done, SUBLANES), p)

                    @pl.when(piece != 0)
                    def _():
                        act(pltpu.make_async_copy(zblk.at[pl.ds(0, p)], xs_hbm.at[rows], zsem.at[0]))
                    done = done + piece
                return carry
            lax.fori_loop(0, N_EXPERTS, e_body, 0)

        def for_tail_blocks(act):
            def b_body(j, carry):
                act(pltpu.make_async_copy(zblk, xs_hbm.at[pl.ds(pl.multiple_of(j * MOE_TM, MOE_TM), MOE_TM)],
                                          zsem.at[0]))
                return carry
            lax.fori_loop(meta_ref[2 * N_EXPERTS], n_blocks, b_body, 0)

        for_region_tails(lambda cp: cp.start())
        for_tail_blocks(lambda cp: cp.start())
        for_region_tails(lambda cp: cp.wait())
        for_tail_blocks(lambda cp: cp.wait())


def _dispatch(cnt8, tile_off, tot8, meta, h2_p, h2_s, route_p, route_s, n_rows):
    npt = h2_p.shape[0] // SEQ_TILE
    p_tile = lambda i, *_: (jnp.minimum(i, npt - 1), 0)
    s_tile = lambda i, *_: (jnp.maximum(i - npt, 0), 0)
    nt = tot8.shape[0]
    tt = SEQ_TILE
    grid_spec = pltpu.PrefetchScalarGridSpec(
        num_scalar_prefetch=4,
        grid=(nt,),
        in_specs=[pl.BlockSpec((tt, D_MODEL), p_tile), pl.BlockSpec((tt, D_MODEL), s_tile),
                  pl.BlockSpec((tt, LANES), p_tile), pl.BlockSpec((tt, LANES), s_tile)],
        out_specs=pl.BlockSpec(memory_space=pl.ANY),
        scratch_shapes=[pltpu.VMEM((2, SORT_ROWS, PACK_W), PACKED), pltpu.VMEM((MOE_TM, PACK_W), PACKED),
                        pltpu.SemaphoreType.DMA((2,)), pltpu.SemaphoreType.DMA((1,))],
    )
    return pl.pallas_call(
        functools.partial(_dispatch_kernel, npt),
        grid_spec=grid_spec,
        out_shape=jax.ShapeDtypeStruct((n_rows, PACK_W), PACKED),
        compiler_params=pltpu.CompilerParams(dimension_semantics=("arbitrary",), vmem_limit_bytes=VMEM_LIMIT),
        name="dispatch",
    )(cnt8, tile_off, tot8, meta, h2_p, h2_s, route_p, route_s)


def _expert_mlp(words, w1b, b1_ref, w2b, b2_ref, act_ref):
    xb = jnp.concatenate(_unpack_bf16_pairs(words), axis=1)
    q = D_FF // 4
    for c in range(4):
        zg = jnp.dot(xb, w1b[:, c * q:(c + 1) * q], preferred_element_type=F32) + b1_ref[:, c * q:(c + 1) * q]
        zl = (jnp.dot(xb, w1b[:, D_FF + c * q:D_FF + (c + 1) * q], preferred_element_type=F32)
              + b1_ref[:, D_FF + c * q:D_FF + (c + 1) * q])
        glu = jnp.minimum(zg, SWIGLU_LIMIT)
        lin = jnp.clip(zl, -SWIGLU_LIMIT, SWIGLU_LIMIT)
        act_ref[:, c * q:(c + 1) * q] = (glu * jax.nn.sigmoid(SWIGLU_ALPHA * glu) * (lin + 1.0)).astype(BF16)
    return _pack_bf16_pairs(jnp.dot(act_ref[...], w2b[...], preferred_element_type=F32) + b2_ref[...])


def _moe_kernel(row_ref, size_ref, exp_ref, wt_ref, meta_ref, xs_hbm, w1_hbm, b1_ref, w2_hbm, b2_ref, ys_hbm,
                xin, yout, w1f, w2f, w1b, w2b, act, isem, osem, wsem):
    tm = MOE_TM
    n_blocks = ys_hbm.shape[0] // tm
    n_tbl = row_ref.shape[0]
    n = meta_ref[2 * N_EXPERTS + 1]

    def for_chunk(j, fn):
        r = pl.multiple_of(row_ref[j], tm)
        for code in MOE_CHUNK_CODES:
            @pl.when(size_ref[j] == code)
            def _():
                fn(r, code * tm)

    def in_copy(r, rows, s):
        return pltpu.make_async_copy(xs_hbm.at[pl.ds(r, rows)], xin.at[s, pl.ds(0, rows)], isem.at[s])

    def out_copy(r, rows, s):
        return pltpu.make_async_copy(yout.at[s, pl.ds(0, rows)], ys_hbm.at[pl.ds(r, rows)], osem.at[s])

    def start_in(j, s):
        for_chunk(j, lambda r, rows: in_copy(r, rows, s).start())

    def wait_in(j, s):
        for_chunk(j, lambda r, rows: in_copy(r, rows, s).wait())

    def start_out(j, s):
        for_chunk(j, lambda r, rows: out_copy(r, rows, s).start())

    def wait_out(j, s):
        for_chunk(j, lambda r, rows: out_copy(r, rows, s).wait())

    def weight_copies(e, ws):
        return (pltpu.make_async_copy(w1_hbm.at[e], w1f.at[ws], wsem.at[ws]),
                pltpu.make_async_copy(w2_hbm.at[e], w2f.at[ws], wsem.at[ws]))

    start_in(0, 0)
    for cp in weight_copies(exp_ref[0], 0):
        cp.start()

    def trip(i, carry):
        slot = i % 2

        @pl.when(i + 1 < n)
        def _():
            start_in(i + 1, 1 - slot)

        @pl.when(i >= 2)
        def _():
            wait_out(i - 2, slot)

        @pl.when(wt_ref[i] == 1)
        def _():
            ws = wt_ref[n_tbl + i]
            nxt = wt_ref[2 * n_tbl + i]
            for cp in weight_copies(exp_ref[i], ws):
                cp.wait()

            @pl.when(nxt >= 0)
            def _():
                for cp in weight_copies(nxt, 1 - ws):
                    cp.start()

            chunk = 128
            def cast_body(c, carry2):
                k0 = pl.multiple_of(c * chunk, chunk)
                w1b[pl.ds(k0, chunk), :] = w1f[ws, pl.ds(k0, chunk), :].astype(BF16)
                w2b[pl.ds(k0, chunk), :] = w2f[ws, pl.ds(k0, chunk), :].astype(BF16)
                return carry2
            lax.fori_loop(0, D_MODEL // chunk, cast_body, 0)

        wait_in(i, slot)

        for code in MOE_CHUNK_CODES:
            rows = code * tm

            @pl.when(size_ref[i] == code)
            def _():
                yout[slot, 0:rows] = _expert_mlp(xin[slot, 0:rows], w1b, b1_ref.at[exp_ref[i]], w2b,
                                                 b2_ref.at[exp_ref[i]], act.at[pl.ds(0, rows)])

        start_out(i, slot)
        return carry
    lax.fori_loop(0, n, trip, 0)

    @pl.when(n >= 2)
    def _():
        wait_out(n - 2, n % 2)
    wait_out(n - 1, (n - 1) % 2)

    yout[1, 0:tm] = jnp.zeros((tm, PACK_W), PACKED)

    def zero_block(j):
        return pltpu.make_async_copy(yout.at[1, pl.ds(0, tm)], ys_hbm.at[pl.ds(pl.multiple_of(j * tm, tm), tm)],
                                     osem.at[1])

    def start_body(j, carry):
        zero_block(j).start()
        return carry

    def wait_body(j, carry):
        zero_block(j).wait()
        return carry
    lax.fori_loop(meta_ref[2 * N_EXPERTS], n_blocks, start_body, 0)
    lax.fori_loop(meta_ref[2 * N_EXPERTS], n_blocks, wait_body, 0)


def _moe(chunk_row, chunk_size, chunk_exp, weight_tbl, meta, xs, w1, b1, w2, b2):
    whole = lambda i, *_: (0, 0, 0)
    grid_spec = pltpu.PrefetchScalarGridSpec(
        num_scalar_prefetch=5,
        grid=(1,),
        in_specs=[
            pl.BlockSpec(memory_space=pl.ANY),
            pl.BlockSpec(memory_space=pl.ANY),
            pl.BlockSpec((N_EXPERTS, 1, 2 * D_FF), whole),
            pl.BlockSpec(memory_space=pl.ANY),
            pl.BlockSpec((N_EXPERTS, 1, D_MODEL), whole),
        ],
        out_specs=pl.BlockSpec(memory_space=pl.ANY),
        scratch_shapes=[pltpu.VMEM((2, MOE_CH, PACK_W), PACKED), pltpu.VMEM((2, MOE_CH, PACK_W), PACKED),
                        pltpu.VMEM((2, D_MODEL, 2 * D_FF), F32), pltpu.VMEM((2, D_FF, D_MODEL), F32),
                        pltpu.VMEM((D_MODEL, 2 * D_FF), BF16), pltpu.VMEM((D_FF, D_MODEL), BF16),
                        pltpu.VMEM((MOE_CH, D_FF), BF16),
                        pltpu.SemaphoreType.DMA((2,)), pltpu.SemaphoreType.DMA((2,)), pltpu.SemaphoreType.DMA((2,))],
    )
    return pl.pallas_call(
        _moe_kernel,
        grid_spec=grid_spec,
        out_shape=jax.ShapeDtypeStruct(xs.shape, PACKED),
        compiler_params=pltpu.CompilerParams(dimension_semantics=("arbitrary",), vmem_limit_bytes=VMEM_LIMIT),
        name="moe",
    )(chunk_row, chunk_size, chunk_exp, weight_tbl, meta, xs, w1, b1, w2, b2)


def _combine_kernel(n_prompt_tiles, cnt_ref, off_ref, tot_ref, ys_hbm, routep_ref, routes_ref, x1p_ref, x1s_ref,
                    modp_ref, g2s_ref, fg_ref, op_ref, os_ref, buf, sem):
    i = pl.program_id(0)
    nb = pl.num_programs(0)
    slot = i % 2

    def fetch(tile, s):
        _for_strips(cnt_ref, off_ref, tile, buf.at[s], ys_hbm, sem.at[s], False, lambda cp: cp.start())

    @pl.when(i == 0)
    def _():
        buf[...] = jnp.zeros_like(buf)
        fetch(0, 0)

    @pl.when(i + 1 < nb)
    def _():
        fetch(i + 1, 1 - slot)

    _wait_tile_rows(tot_ref[i], buf.at[slot], ys_hbm, sem.at[slot], False)

    is_prompt = i < n_prompt_tiles
    route = jnp.where(is_prompt, routep_ref[...], routes_ref[...])
    c_pos = lax.broadcasted_iota(jnp.int16, (SEQ_TILE, SORT_ROWS), 1)
    g_bf = jnp.zeros((SEQ_TILE, SORT_ROWS), BF16)
    for k in range(TOP_K):
        pos_k = route[:, ROUTE_R + k:ROUTE_R + k + 1].astype(jnp.int32).astype(jnp.int16)
        g_bf = jnp.where(c_pos == pos_k, route[:, ROUTE_G + k:ROUTE_G + k + 1].astype(BF16), g_bf)
    ff = jnp.concatenate([jnp.dot(g_bf, yb, preferred_element_type=F32) for yb in _unpack_bf16_pairs(buf[slot])],
                         axis=1)
    g2 = jnp.where(i < n_prompt_tiles, modp_ref[5:6, :], g2s_ref[...])
    x = jnp.where(is_prompt, x1p_ref[...], x1s_ref[...]) + g2 * ff
    y = _rms(x, fg_ref[...])

    @pl.when(i < n_prompt_tiles)
    def _():
        op_ref[...] = y

    @pl.when(i >= n_prompt_tiles)
    def _():
        os_ref[...] = y


def _combine(cnt8, tile_off, tot8, ys, route_p, route_s, x1_p, x1_s, mod_p, g2_rows, final_g, tiles_per_batch):
    nt = tot8.shape[0]
    tt = SEQ_TILE
    npt = x1_p.shape[0] // tt
    p_tile = lambda i, *_: (jnp.minimum(i, npt - 1), 0)
    s_tile = lambda i, *_: (jnp.maximum(i - npt, 0), 0)
    grid_spec = pltpu.PrefetchScalarGridSpec(
        num_scalar_prefetch=3,
        grid=(nt,),
        in_specs=[
            pl.BlockSpec(memory_space=pl.ANY),
            pl.BlockSpec((tt, LANES), p_tile), pl.BlockSpec((tt, LANES), s_tile),
            pl.BlockSpec((tt, D_MODEL), p_tile), pl.BlockSpec((tt, D_MODEL), s_tile),
            pl.BlockSpec((None, 6, D_MODEL), lambda i, *_: (jnp.minimum(i, npt - 1) // tiles_per_batch, 0, 0)),
            pl.BlockSpec((tt, D_MODEL), s_tile),
            pl.BlockSpec((1, D_MODEL), lambda i, *_: (0, 0)),
        ],
        out_specs=(pl.BlockSpec((tt, D_MODEL), p_tile), pl.BlockSpec((tt, D_MODEL), s_tile)),
        scratch_shapes=[pltpu.VMEM((2, SORT_ROWS, PACK_W), PACKED), pltpu.SemaphoreType.DMA((2,))],
    )
    return pl.pallas_call(
        functools.partial(_combine_kernel, npt),
        grid_spec=grid_spec,
        out_shape=(jax.ShapeDtypeStruct((npt * tt, D_MODEL), F32),
                   jax.ShapeDtypeStruct(((nt - npt) * tt, D_MODEL), F32)),
        compiler_params=pltpu.CompilerParams(dimension_semantics=("arbitrary",), vmem_limit_bytes=VMEM_LIMIT),
        name="combine",
    )(cnt8, tile_off, tot8, ys, route_p, route_s, x1_p, x1_s, mod_p, g2_rows, final_g)


def _block_diag_halves(wg):
    per_half = LRU_BLOCKS // 2
    w4 = wg.reshape(2, per_half, LRU_BLOCK_W, LRU_BLOCK_W)
    on_diag = jnp.eye(per_half, dtype=bool)[None, :, None, :, None]
    dense = jnp.where(on_diag, w4[:, :, :, None, :], 0.0)
    return dense.reshape(2, per_half * LRU_BLOCK_W, per_half * LRU_BLOCK_W).astype(BF16)


def _rope_tables(pos):
    half = HEAD_DIM // 2
    inv = ROPE_THETA ** (-jnp.arange(half, dtype=F32) / half)
    ang = pos.astype(F32)[:, None] * inv[None, :]
    cos = jnp.cos(ang)
    sin = jnp.sin(ang)
    cos128 = jnp.concatenate([cos, cos, cos, cos], axis=1)
    sin128 = jnp.concatenate([-sin, sin, -sin, sin], axis=1)
    return cos128, sin128


def _resident_spec(arr):
    zeros = (0,) * arr.ndim
    return pl.BlockSpec(arr.shape, lambda i: zeros)


def kernel(x_prompt, x_sample, state_lru_h, state_conv, cache_win_k, cache_win_v, c_prompt, c_sample, w_ada, b_ada, norm1_g, w_in, b_in, conv_w, conv_b, lru_wa, lru_ba, lru_wx, lru_bx, lru_lambda, attn_sinks, w_out, b_out, norm2_g, w_router, b_router, w1, b1, w2, b2, final_g):
    bp, seq, _ = x_prompt.shape
    bd, tdec, _ = x_sample.shape
    assert tdec == SUBLANES and seq % (MIX_TILES * SEQ_TILE) == 0 and bd % SAMPLE_BT == 0
    assert SAMPLE_BT * tdec == SEQ_TILE
    n_prompt = bp * seq
    n_sample = bd * tdec
    n_tok = n_prompt + n_sample
    l = 0

    head_perm = [h for c in range(4) for h in (c, GROUP + c)]
    o2 = 2 * LRU_WIDTH

    def permute_heads(arr, axis, start):
        take = lambda a, b: lax.slice_in_dim(arr, a, b, axis=axis)
        heads = [take(start + h * HEAD_DIM, start + (h + 1) * HEAD_DIM) for h in head_perm]
        return jnp.concatenate([take(0, start)] + heads + [take(start + Q_WIDTH, arr.shape[axis])], axis=axis)

    sinks_perm = attn_sinks[l]

    wr = jnp.tile(w_router[l], (1, TOP_K))
    wr_hi = wr.astype(BF16)
    weights = dict(
        norm1_g=norm1_g[l][None, :], w_in=permute_heads(w_in[l], 1, o2).astype(BF16),
        b_in=permute_heads(b_in[l], 0, o2)[None, :],
        conv_w=conv_w[l], conv_b=conv_b[l][None, :],
        ga=_block_diag_halves(lru_wa[l]), gx=_block_diag_halves(lru_wx[l]),
        lru_ba=lru_ba[l][None, :], lru_bx=lru_bx[l][None, :], lam=lru_lambda[l][None, :],
        w_out=permute_heads(w_out[l], 0, LRU_WIDTH).astype(BF16), b_out=b_out[l][None, :], norm2_g=norm2_g[l][None, :],
        wr_hi=wr_hi, wr_lo=(wr - wr_hi.astype(F32)).astype(BF16),
        b_router=jnp.tile(b_router[l], TOP_K)[None, :],
    )
    wlist = [weights[n] for n in WEIGHT_NAMES]

    mod_p, mod_s = _ada(jnp.concatenate([c_prompt, c_sample], axis=0), bp, w_ada[l], b_ada[l][None, :])
    mod_p = mod_p.reshape(bp, 6, D_MODEL)

    cos_p, sin_p = _rope_tables(jnp.arange(seq, dtype=jnp.int32))
    cos_s, sin_s = _rope_tables(PAST_LEN + jnp.arange(tdec, dtype=jnp.int32))
    cos_s = jnp.tile(cos_s, (SAMPLE_BT, 1))
    sin_s = jnp.tile(sin_s, (SAMPLE_BT, 1))
    h0_rows = jnp.pad(state_lru_h[l][:, None, :], ((0, 0), (0, tdec - 1), (0, 0))).reshape(n_sample, LRU_WIDTH)
    cprev_rows = jnp.pad(state_conv[l], ((0, 0), (0, tdec - (CONV_W - 1)), (0, 0))).reshape(n_sample, LRU_WIDTH)
    ck = cache_win_k[l].reshape(bd, WINDOW, KV_WIDTH)
    cv = cache_win_v[l].reshape(bd, WINDOW, KV_WIDTH)
    npt = n_prompt // SEQ_TILE
    nst = n_sample // SEQ_TILE
    mix_rows = MIX_TILES * SEQ_TILE
    steps_per_seq = seq // mix_rows
    wspecs = [_resident_spec(a) for a in wlist]

    rows_p = lambda width: pl.BlockSpec((mix_rows, width), lambda i: (i, 0))
    tail_p = lambda rows, width: pl.BlockSpec((None, rows, width), lambda i: (i // steps_per_seq, 0, 0))
    (x1_p, h2_p, route_p, cnt_p, hlast_p, ulast_p, klast_p, vlast_p) = pl.pallas_call(
        functools.partial(_prompt_kernel, steps_per_seq),
        grid=(n_prompt // mix_rows,),
        in_specs=[rows_p(D_MODEL),
                  pl.BlockSpec((None, 6, D_MODEL), lambda i: (i // steps_per_seq, 0, 0)),
                  pl.BlockSpec((mix_rows, LANES), lambda i: (i % steps_per_seq, 0)),
                  pl.BlockSpec((mix_rows, LANES), lambda i: (i % steps_per_seq, 0)),
                  pl.BlockSpec(memory_space=pltpu.SMEM)] + wspecs,
        out_specs=(rows_p(D_MODEL), rows_p(D_MODEL), rows_p(LANES),
                   pl.BlockSpec((MIX_TILES, SUBLANES, LANES), lambda i: (i, 0, 0)),
                   tail_p(SUBLANES, LRU_WIDTH), tail_p(SUBLANES, LRU_WIDTH),
                   tail_p(WINDOW, KV_WIDTH), tail_p(WINDOW, KV_WIDTH)),
        out_shape=(
            jax.ShapeDtypeStruct((n_prompt, D_MODEL), F32),
            jax.ShapeDtypeStruct((n_prompt, D_MODEL), BF16),
            jax.ShapeDtypeStruct((n_prompt, LANES), F32),
            jax.ShapeDtypeStruct((npt, SUBLANES, LANES), F32),
            jax.ShapeDtypeStruct((bp, SUBLANES, LRU_WIDTH), F32),
            jax.ShapeDtypeStruct((bp, SUBLANES, LRU_WIDTH), F32),
            jax.ShapeDtypeStruct((bp, WINDOW, KV_WIDTH), F32),
            jax.ShapeDtypeStruct((bp, WINDOW, KV_WIDTH), F32),
        ),
        scratch_shapes=[pltpu.VMEM((SUBLANES, LRU_WIDTH), F32), pltpu.VMEM((SUBLANES, LRU_WIDTH), F32),
                        pltpu.VMEM((WINDOW, KV_WIDTH), F32), pltpu.VMEM((WINDOW, KV_WIDTH), F32)],
        compiler_params=pltpu.CompilerParams(dimension_semantics=("arbitrary",), vmem_limit_bytes=VMEM_LIMIT),
        name="prompt_mixer",
    )(x_prompt.reshape(n_prompt, D_MODEL), mod_p, cos_p, sin_p, sinks_perm, *wlist)

    rows_s = lambda width: pl.BlockSpec((SEQ_TILE, width), lambda i: (i, 0))
    cache_spec = pl.BlockSpec((SAMPLE_BT, WINDOW, KV_WIDTH), lambda i: (i, 0, 0))
    (x1_s, h2_s, route_s, cnt_s, g2_rows, hs_s, u_s, s_k, s_v) = pl.pallas_call(
        _sample_kernel,
        grid=(nst,),
        in_specs=[rows_s(D_MODEL),
                  pl.BlockSpec((6, SAMPLE_BT, D_MODEL), lambda i: (0, i, 0)),
                  pl.BlockSpec((SEQ_TILE, LANES), lambda i: (0, 0)),
                  pl.BlockSpec((SEQ_TILE, LANES), lambda i: (0, 0)),
                  pl.BlockSpec(memory_space=pltpu.SMEM),
                  rows_s(LRU_WIDTH), rows_s(LRU_WIDTH), cache_spec, cache_spec] + wspecs,
        out_specs=(rows_s(D_MODEL), rows_s(D_MODEL), rows_s(LANES),
                   pl.BlockSpec((None, SUBLANES, LANES), lambda i: (i, 0, 0)),
                   rows_s(D_MODEL), rows_s(LRU_WIDTH), rows_s(LRU_WIDTH), cache_spec, cache_spec),
        out_shape=(
            jax.ShapeDtypeStruct((n_sample, D_MODEL), F32), jax.ShapeDtypeStruct((n_sample, D_MODEL), BF16),
            jax.ShapeDtypeStruct((n_sample, LANES), F32), jax.ShapeDtypeStruct((nst, SUBLANES, LANES), F32),
            jax.ShapeDtypeStruct((n_sample, D_MODEL), F32),
            jax.ShapeDtypeStruct((n_sample, LRU_WIDTH), F32),
            jax.ShapeDtypeStruct((n_sample, LRU_WIDTH), F32),
            jax.ShapeDtypeStruct((bd, WINDOW, KV_WIDTH), F32),
            jax.ShapeDtypeStruct((bd, WINDOW, KV_WIDTH), F32),
        ),
        compiler_params=pltpu.CompilerParams(dimension_semantics=("arbitrary",), vmem_limit_bytes=VMEM_LIMIT),
        name="sample_mixer",
    )(x_sample.reshape(n_sample, D_MODEL), mod_s, cos_s, sin_s, sinks_perm, h0_rows, cprev_rows, ck, cv, *wlist)
    tile_cnt = jnp.concatenate([cnt_p, cnt_s], axis=0)

    n_tiles = npt + nst
    n_assign = n_tok * TOP_K
    max_rows = n_assign + n_tiles * N_EXPERTS * (SUBLANES - 1) + N_EXPERTS * (MOE_TM - 1)
    n_blocks = -(-max_rows // MOE_TM)
    cnt = tile_cnt[:, 0, :N_EXPERTS].astype(jnp.int32)
    cnt8 = (cnt + SUBLANES - 1) // SUBLANES * SUBLANES
    counts = jnp.sum(cnt8, axis=0)
    pcounts = (counts + MOE_TM - 1) // MOE_TM * MOE_TM
    pend = jnp.cumsum(pcounts)
    pstart = pend - pcounts
    tile_off = pstart[None, :] + jnp.cumsum(cnt8, axis=0) - cnt8
    tot8 = jnp.sum(cnt8, axis=1)
    meta = jnp.concatenate([pstart + counts, pend, pend[-1:] // MOE_TM]).astype(jnp.int32)
    cnt8_flat = cnt8.reshape(-1)
    off_flat = tile_off.reshape(-1).astype(jnp.int32)

    xs = _dispatch(cnt8_flat, off_flat, tot8, meta, h2_p, h2_s, route_p, route_s, n_blocks * MOE_TM)

    big, small = MOE_CHUNK_CODES[0], MOE_CHUNK_CODES[1:]
    n_tm = pcounts // MOE_TM
    n_big = n_tm // big
    present = [(n_tm // b) % 2 for b in small]
    n_ch = n_big + sum(present)
    ch_end = jnp.cumsum(n_ch)
    ch_start = ch_end - n_ch
    n_chunks = n_blocks // big + len(small) * N_EXPERTS
    ci = jnp.arange(n_chunks, dtype=jnp.int32)
    owner = jnp.logical_and(ch_start[None, :] <= ci[:, None], ci[:, None] < ch_end[None, :])
    pick = lambda v: jnp.sum(jnp.where(owner, v[None, :], 0), axis=1)
    live = ci < ch_end[-1]
    local = ci - pick(ch_start)
    chunk_exp = jnp.where(live, pick(jnp.arange(N_EXPERTS, dtype=jnp.int32)), N_EXPERTS - 1).astype(jnp.int32)
    big_c = pick(n_big)
    code = jnp.where(local < big_c, big, 0)
    blocks_before = jnp.where(local < big_c, local * big, 0)
    order = big_c
    blocks = big_c * big
    for b, has in zip(small, present):
        has_c = pick(has)
        is_b = jnp.logical_and(has_c == 1, local == order)
        code = jnp.where(is_b, b, code)
        blocks_before = jnp.where(is_b, blocks, blocks_before)
        order = order + has_c
        blocks = blocks + has_c * b
    chunk_row = jnp.where(live, pick(pstart) + blocks_before * MOE_TM, 0).astype(jnp.int32)
    chunk_size = jnp.where(live, code, 0).astype(jnp.int32)
    has_rows = n_ch > 0
    e_ar = jnp.arange(N_EXPERTS, dtype=jnp.int32)
    w_slot = (jnp.cumsum(has_rows.astype(jnp.int32)) - 1) % 2
    later = jnp.logical_and(has_rows[None, :], e_ar[None, :] > e_ar[:, None])
    nxt = jnp.min(jnp.where(later, e_ar[None, :], N_EXPERTS), axis=1)
    nxt = jnp.where(nxt == N_EXPERTS, -1, nxt)
    weight_tbl = jnp.concatenate([jnp.logical_and(live, local == 0).astype(jnp.int32),
                                  pick(w_slot), jnp.where(live, pick(nxt), -1)]).astype(jnp.int32)
    moe_meta = jnp.concatenate([meta, ch_end[-1:].astype(jnp.int32)])
    ys = _moe(chunk_row, chunk_size, chunk_exp, weight_tbl, moe_meta, xs, w1[l], b1[l][:, None, :], w2[l], b2[l][:, None, :])

    y_p, y_s = _combine(cnt8_flat, off_flat, tot8, ys, route_p, route_s, x1_p, x1_s, mod_p, g2_rows,
                        final_g[None, :], seq // SEQ_TILE)

    y_prompt = y_p.reshape(bp, seq, D_MODEL)
    y_sample = y_s.reshape(bd, tdec, D_MODEL)
    p_h = hlast_p[:, SUBLANES - 1, :][None]
    p_c = ulast_p[:, SUBLANES - (CONV_W - 1):, :][None]
    p_k = klast_p.reshape(1, bp, WINDOW, N_KV_HEADS, HEAD_DIM)
    p_v = vlast_p.reshape(1, bp, WINDOW, N_KV_HEADS, HEAD_DIM)
    s_h = hs_s.reshape(bd, tdec, LRU_WIDTH)[:, tdec - 1, :][None]
    s_c = u_s.reshape(bd, tdec, LRU_WIDTH)[:, tdec - (CONV_W - 1):, :][None]
    s_kk = s_k.reshape(1, bd, WINDOW, N_KV_HEADS, HEAD_DIM)
    s_vv = s_v.reshape(1, bd, WINDOW, N_KV_HEADS, HEAD_DIM)
    return (y_prompt, y_sample, p_h, p_c, p_k, p_v, s_h, s_c, s_kk, s_vv)
```

```python
import functools

import jax
import jax.numpy as jnp
from jax import lax
from jax.experimental import pallas as pl
from jax.experimental.pallas import tpu as pltpu

F32 = jnp.float32
BF16 = jnp.bfloat16

D_MODEL = 1024
LRU_WIDTH = 512
LRU_BLOCKS = 8
LRU_BLOCK_W = LRU_WIDTH // LRU_BLOCKS
CONV_W = 4
LRU_C = 8.0
HEAD_DIM = 64
N_HEADS = 8
N_KV_HEADS = 2
GROUP = N_HEADS // N_KV_HEADS
WINDOW = 128
ROPE_THETA = 10000.0
N_EXPERTS = 32
TOP_K = 4
D_FF = D_MODEL
SWIGLU_LIMIT = 7.0
SWIGLU_ALPHA = 1.702
NORM_EPS = 1e-5
PAST_LEN = 8192
Q_WIDTH = N_HEADS * HEAD_DIM
KV_WIDTH = N_KV_HEADS * HEAD_DIM
IN_WIDTH = 2 * LRU_WIDTH + Q_WIDTH + 2 * KV_WIDTH

LANES = 128
SUBLANES = 8
SEQ_TILE = 256
MIX_TILES = 2
SAMPLE_BT = 32
MOE_TM = 128
MOE_CHUNK_CODES = (8, 4, 2, 1)
MOE_CH = MOE_CHUNK_CODES[0] * MOE_TM
NEG_BIG = -1e30
VMEM_LIMIT = 56 * 1024 * 1024


def _rms(x, g):
    return x * lax.rsqrt(jnp.mean(x * x, axis=-1, keepdims=True) + NORM_EPS) * g


def _rms_mod(x, gain, shift):
    return x * lax.rsqrt(jnp.mean(x * x, axis=-1, keepdims=True) + NORM_EPS) * gain + shift


def _group_scan(a, b):
    rows, width = a.shape
    groups = rows // SUBLANES
    a3 = a.reshape(groups, SUBLANES, width)
    b3 = b.reshape(groups, SUBLANES, width)
    t = lax.broadcasted_iota(jnp.int32, (1, SUBLANES, 1), 1)
    d = 1
    while d < SUBLANES:
        keep = t >= d
        a_s = jnp.where(keep, pltpu.roll(a3, d, axis=1), 1.0)
        b_s = jnp.where(keep, pltpu.roll(b3, d, axis=1), 0.0)
        b3 = a3 * b_s + b3
        a3 = a3 * a_s
        d *= 2
    return a3.reshape(rows, width), b3.reshape(rows, width)


def _chain_groups(a_grp, b_grp, h_in):
    rows = a_grp.shape[0]
    out = []
    carry = h_in
    for g in range(rows // SUBLANES):
        sl = slice(g * SUBLANES, (g + 1) * SUBLANES)
        hg = b_grp[sl] + a_grp[sl] * carry
        out.append(hg)
        carry = hg[SUBLANES - 1:SUBLANES]
    return jnp.concatenate(out, axis=0)


def _rope128(x, cos, sin_signed, first_half):
    sw = jnp.where(first_half, pltpu.roll(x, LANES - HEAD_DIM // 2, axis=1), pltpu.roll(x, HEAD_DIM // 2, axis=1))
    return x * cos + sw * sin_signed


def _softplus(x):
    return jnp.maximum(x, 0.0) + jnp.log1p(jnp.exp(-jnp.abs(x)))


def _lru_coeffs(uc, w, first_pos_mask):
    ub = uc.astype(BF16)
    half = LRU_WIDTH // 2
    ra = jnp.concatenate([jnp.dot(ub[:, :half], w['ga'][0], preferred_element_type=F32),
                          jnp.dot(ub[:, half:], w['ga'][1], preferred_element_type=F32)], axis=1)
    rx = jnp.concatenate([jnp.dot(ub[:, :half], w['gx'][0], preferred_element_type=F32),
                          jnp.dot(ub[:, half:], w['gx'][1], preferred_element_type=F32)], axis=1)
    r = jax.nn.sigmoid(ra + w['lru_ba'][...])
    i = jax.nn.sigmoid(rx + w['lru_bx'][...])
    log_a = -LRU_C * r * _softplus(-w['lam'][...])
    a = jnp.exp(log_a)
    om = 1.0 - a * a
    mult = jnp.where(om > 0.0, om * lax.rsqrt(om), 0.0)
    if first_pos_mask is not None:
        mult = jnp.where(first_pos_mask, 1.0, mult)
    return a, mult * i * uc


def _conv_taps(u, s1, s2, s3, w):
    cw = w['conv_w']
    return w['conv_b'][...] + s3 * cw[0:1, :] + s2 * cw[1:2, :] + s1 * cw[2:3, :] + u * cw[3:4, :]


def _in_proj(x, mod, w):
    sh1, sc1 = mod
    h = _rms_mod(x, w['norm1_g'][...] * (1.0 + sc1), sh1)
    return jnp.dot(h.astype(BF16), w['w_in'][...], preferred_element_type=F32) + w['b_in'][...]


def _post_mix(x, mix, mod, w):
    g1, sh2, sc2 = mod
    x1 = x + g1 * (jnp.dot(mix.astype(BF16), w['w_out'][...], preferred_element_type=F32) + w['b_out'][...])
    h2 = _rms_mod(x1, w['norm2_g'][...] * (1.0 + sc2), sh2)
    h2_hi = h2.astype(BF16)
    h2_lo = (h2 - h2_hi.astype(F32)).astype(BF16)
    logits = (jnp.dot(h2_hi, w['wr_hi'][...], preferred_element_type=F32)
              + jnp.dot(h2_lo, w['wr_hi'][...], preferred_element_type=F32)
              + jnp.dot(h2_hi, w['wr_lo'][...], preferred_element_type=F32)) + w['b_router'][...]
    return x1, h2_hi, logits


ROUTE_E, ROUTE_G, ROUTE_R = 0, TOP_K, 2 * TOP_K


def _lane_roll1(v, shift):
    return pltpu.roll(jnp.broadcast_to(v, (SUBLANES, LANES)), shift, axis=1)[0:1]


def _route_tile(lg):
    rows = lg.shape[0]
    lane = lax.broadcasted_iota(jnp.int32, (1, LANES), 1)
    e_of = lane % N_EXPERTS
    grp = lane // N_EXPERTS
    e_id = e_of.astype(F32)
    onehot = jnp.zeros((rows, LANES), F32)
    vals, ids = [], []
    for k in range(TOP_K):
        m = jnp.max(lg, axis=1, keepdims=True)
        idx = jnp.min(jnp.where(lg == m, e_id, float(N_EXPERTS)), axis=1, keepdims=True)
        sel = e_id == idx
        lg = jnp.where(sel, -jnp.inf, lg)
        onehot = jnp.where(jnp.logical_and(sel, grp == k), 1.0, onehot)
        vals.append(m)
        ids.append(idx)
    ex = [jnp.exp(v - vals[0]) for v in vals]
    denom = ex[0] + ex[1] + ex[2] + ex[3]

    r_i = lax.broadcasted_iota(jnp.int32, (rows, rows), 0)
    c_i = lax.broadcasted_iota(jnp.int32, (rows, rows), 1)
    strict_lower = jnp.where(r_i > c_i, 1.0, 0.0).astype(BF16)
    prefix = jnp.dot(strict_lower, onehot.astype(BF16), preferred_element_type=F32)
    cnt = jnp.sum(onehot, axis=0, keepdims=True)
    base = jnp.zeros((1, LANES), F32)
    tot = cnt
    for s in range(1, TOP_K):
        rolled = _lane_roll1(cnt, s * N_EXPERTS)
        base = base + jnp.where(lane >= s * N_EXPERTS, rolled, 0.0)
        tot = tot + rolled
    pad_cnt = jnp.floor((tot + (SUBLANES - 1.0)) * (1.0 / SUBLANES)) * SUBLANES
    inc = pad_cnt
    d = 1
    while d < N_EXPERTS:
        inc = inc + jnp.where(e_of >= d, _lane_roll1(inc, d), 0.0)
        d *= 2
    strip_start = inc - pad_cnt
    ranked = onehot * (prefix + base + strip_start)

    route = jnp.zeros((rows, LANES), F32)
    for k in range(TOP_K):
        rank_k = jnp.sum(jnp.where(grp == k, ranked, 0.0), axis=1, keepdims=True)
        route = jnp.where(lane == ROUTE_E + k, ids[k], route)
        route = jnp.where(lane == ROUTE_G + k, ex[k] / denom, route)
        route = jnp.where(lane == ROUTE_R + k, rank_k, route)
    return route, tot


def _softmax_with_sink(s, sink_col):
    m = jnp.maximum(jnp.max(s, axis=-1, keepdims=True), sink_col)
    p = jnp.exp(s - m)
    denom = jnp.sum(p, axis=-1, keepdims=True) + jnp.exp(sink_col - m)
    return p, denom


WEIGHT_NAMES = ('norm1_g', 'w_in', 'b_in', 'conv_w', 'conv_b', 'ga', 'gx', 'lru_ba', 'lru_bx', 'lam',
                'w_out', 'b_out', 'norm2_g', 'wr_hi', 'wr_lo', 'b_router')


def _split_bf16(x):
    hi = x.astype(BF16)
    return hi, (x - hi.astype(F32)).astype(BF16)


def _ada_kernel(n_prompt_rows, c_ref, w_ref, b_ref, op_ref, os_ref):
    c = c_ref[...]
    s_hi, s_lo = _split_bf16(c * jax.nn.sigmoid(c))
    w_hi, w_lo = _split_bf16(w_ref[...])
    mod = (jnp.dot(s_hi, w_hi, preferred_element_type=F32) + jnp.dot(s_lo, w_hi, preferred_element_type=F32)
           + jnp.dot(s_hi, w_lo, preferred_element_type=F32)) + b_ref[...]
    op_ref[...] = mod[:n_prompt_rows]
    os_ref[...] = mod[n_prompt_rows:]


def _ada(c_all, n_prompt_rows, w_ada, b_ada):
    rows = c_all.shape[0]
    n_s = rows - n_prompt_rows
    assert n_prompt_rows % SUBLANES == 0
    return pl.pallas_call(
        functools.partial(_ada_kernel, n_prompt_rows),
        grid=(6,),
        in_specs=[pl.BlockSpec((rows, D_MODEL), lambda i: (0, 0)),
                  pl.BlockSpec((D_MODEL, D_MODEL), lambda i: (0, i)),
                  pl.BlockSpec((1, D_MODEL), lambda i: (0, i))],
        out_specs=(pl.BlockSpec((n_prompt_rows, D_MODEL), lambda i: (0, i)),
                   pl.BlockSpec((None, n_s, D_MODEL), lambda i: (i, 0, 0))),
        out_shape=(jax.ShapeDtypeStruct((n_prompt_rows, 6 * D_MODEL), F32),
                   jax.ShapeDtypeStruct((6, n_s, D_MODEL), F32)),
        compiler_params=pltpu.CompilerParams(dimension_semantics=("arbitrary",), vmem_limit_bytes=VMEM_LIMIT),
        name="ada",
    )(c_all, w_ada, b_ada)


def _prompt_body(seq_start, x_ref, mod_ref, cos_ref, sin_ref, sinks_ref, w, x1_ref, h2_ref, lg_ref,
                 hlast_ref, ulast_ref, klast_ref, vlast_ref, conv_c, h_c, k_c, v_c):
    ts = SEQ_TILE

    if seq_start is not None:
        @pl.when(seq_start)
        def _():
            conv_c[...] = jnp.zeros_like(conv_c)
            h_c[...] = jnp.zeros_like(h_c)
            k_c[...] = jnp.zeros_like(k_c)
            v_c[...] = jnp.zeros_like(v_c)

    x = x_ref[...]
    mod = mod_ref[...]
    proj = _in_proj(x, (mod[0:1], mod[1:2]), w)
    u = proj[:, :LRU_WIDTH]
    gate = proj[:, LRU_WIDTH:2 * LRU_WIDTH]
    o2 = 2 * LRU_WIDTH

    rowid = lax.broadcasted_iota(jnp.int32, (ts, 1), 0)
    u_ext = jnp.concatenate([conv_c[...], u], axis=0)
    s1, s2, s3 = (pltpu.roll(u_ext, d, axis=0)[SUBLANES:] for d in (1, 2, 3))
    uc = _conv_taps(u, s1, s2, s3, w)
    conv_c[...] = u[ts - SUBLANES:]
    ulast_ref[...] = u[ts - SUBLANES:]

    first_pos = None if seq_start is None else jnp.logical_and(rowid == 0, seq_start)
    a, bt = _lru_coeffs(uc, w, first_pos)
    hs = _chain_groups(*_group_scan(a, bt), h_c[0:1, :])
    h_tail = hs[ts - SUBLANES:]
    h_c[...] = jnp.broadcast_to(h_tail[SUBLANES - 1:SUBLANES, :], h_c.shape)
    hlast_ref[...] = h_tail
    lru_out = hs * jax.nn.gelu(gate)

    cos = cos_ref[...]
    sin = sin_ref[...]
    lane = lax.broadcasted_iota(jnp.int32, (1, LANES), 1)
    first_half = (lane % HEAD_DIM) < (HEAD_DIM // 2)
    qcols = [_rope128(proj[:, o2 + c * LANES:o2 + (c + 1) * LANES], cos, sin, first_half) * (HEAD_DIM ** -0.5)
             for c in range(4)]
    k = _rope128(proj[:, o2 + Q_WIDTH:o2 + Q_WIDTH + KV_WIDTH], cos, sin, first_half)
    v = proj[:, o2 + Q_WIDTH + KV_WIDTH:]
    k_ext = jnp.concatenate([k_c[...], k], axis=0).astype(BF16)
    v_ext = jnp.concatenate([v_c[...], v], axis=0).astype(BF16)
    k_c[...] = k[ts - WINDOW:]
    v_c[...] = v[ts - WINDOW:]
    klast_ref[...] = k[ts - WINDOW:]
    vlast_ref[...] = v[ts - WINDOW:]

    grow = lax.broadcasted_iota(jnp.int32, (GROUP * WINDOW, 1), 0)
    from_prev = lax.broadcasted_iota(jnp.int32, (GROUP * WINDOW, WINDOW), 1) > grow % WINDOW
    grow = grow // WINDOW
    lane_lo = lane < HEAD_DIM
    attn_cols = [[] for _ in range(4)]
    for blk in range(ts // WINDOW):
        kb = k_ext[blk * WINDOW:(blk + 2) * WINDOW]
        v_prev = v_ext[blk * WINDOW:(blk + 1) * WINDOW]
        v_own = v_ext[(blk + 1) * WINDOW:(blk + 2) * WINDOW]
        outs = []
        for kv in range(N_KV_HEADS):
            sel = lane_lo if kv == 0 else jnp.logical_not(lane_lo)
            qs = jnp.concatenate(
                [jnp.where(sel, qc[blk * WINDOW:(blk + 1) * WINDOW], 0.0) for qc in qcols], axis=0).astype(BF16)
            s = lax.dot_general(qs, kb, (((1,), (1,)), ((), ())), preferred_element_type=F32)
            s_prev = s[:, :WINDOW]
            if seq_start is not None and blk == 0:
                s_prev = jnp.where(seq_start, NEG_BIG, s_prev)
            s = jnp.where(from_prev, s_prev, s[:, WINDOW:])
            sink_col = jnp.zeros((GROUP * WINDOW, 1), F32)
            for g in range(GROUP):
                sink_col = jnp.where(grow == g, sinks_ref[kv * GROUP + g], sink_col)
            p, denom = _softmax_with_sink(s, sink_col)
            pv = (jnp.dot(jnp.where(from_prev, p, 0.0).astype(BF16), v_prev, preferred_element_type=F32)
                  + jnp.dot(jnp.where(from_prev, 0.0, p).astype(BF16), v_own, preferred_element_type=F32))
            outs.append(pv / denom)
        for c in range(4):
            attn_cols[c].append(jnp.where(lane_lo, outs[0][c * WINDOW:(c + 1) * WINDOW],
                                          outs[1][c * WINDOW:(c + 1) * WINDOW]))
    attn = jnp.concatenate([jnp.concatenate(cols, axis=0) for cols in attn_cols], axis=1)

    mix = jnp.concatenate([lru_out, attn], axis=1)
    x1, h2, logits = _post_mix(x, mix, (mod[2:3], mod[3:4], mod[4:5]), w)
    x1_ref[...] = x1
    h2_ref[...] = h2
    route, tot = _route_tile(logits)
    lg_ref[0][...] = route
    lg_ref[1][...] = jnp.broadcast_to(tot, lg_ref[1].shape)


def _expand_rows(m, t):
    b, wd = m.shape
    return jnp.broadcast_to(m[:, None, :], (b, t, wd)).reshape(b * t, wd)


def _sample_body(x_ref, mod_ref, cos_ref, sin_ref, sinks_ref, h0_ref, cprev_ref, ck_ref, cv_ref, w,
                 x1_ref, h2_ref, lg_ref, g2_ref, hs_ref, u_ref, ko_ref, vo_ref):
    bt_, t = SAMPLE_BT, SUBLANES
    rows = bt_ * t

    x = x_ref[...]
    mods = [_expand_rows(mod_ref[i], t) for i in range(6)]
    proj = _in_proj(x, (mods[0], mods[1]), w)
    u = proj[:, :LRU_WIDTH]
    gate = proj[:, LRU_WIDTH:2 * LRU_WIDTH]
    o2 = 2 * LRU_WIDTH
    u_ref[...] = u

    rowid = lax.broadcasted_iota(jnp.int32, (rows, 1), 0) % t
    cprev = cprev_ref[...]
    taps = []
    for d in (1, 2, 3):
        taps.append(jnp.where(rowid >= d, pltpu.roll(u, d, axis=0),
                              pltpu.roll(cprev, (d - (CONV_W - 1)) % rows, axis=0)))
    uc = _conv_taps(u, taps[0], taps[1], taps[2], w)

    a, bt = _lru_coeffs(uc, w, None)
    bt = bt + a * h0_ref[...]
    _, hs = _group_scan(a, bt)
    hs_ref[...] = hs
    lru_out = hs * jax.nn.gelu(gate)

    cos = cos_ref[...]
    sin = sin_ref[...]
    lane = lax.broadcasted_iota(jnp.int32, (1, LANES), 1)
    first_half = (lane % HEAD_DIM) < (HEAD_DIM // 2)
    qcols = [_rope128(proj[:, o2 + c * LANES:o2 + (c + 1) * LANES], cos, sin, first_half) * (HEAD_DIM ** -0.5)
             for c in range(4)]
    k = _rope128(proj[:, o2 + Q_WIDTH:o2 + Q_WIDTH + KV_WIDTH], cos, sin, first_half)
    v = proj[:, o2 + Q_WIDTH + KV_WIDTH:]
    k3 = k.reshape(bt_, t, KV_WIDTH)
    v3 = v.reshape(bt_, t, KV_WIDTH)
    ck = ck_ref[...]
    cv = cv_ref[...]
    ko_ref[:, :WINDOW - t, :] = ck[:, t:, :]
    ko_ref[:, WINDOW - t:, :] = k3
    vo_ref[:, :WINDOW - t, :] = cv[:, t:, :]
    vo_ref[:, WINDOW - t:, :] = v3

    ckb, cvb, k3b, v3b = ck.astype(BF16), cv.astype(BF16), k3.astype(BF16), v3.astype(BF16)
    lane_lo = lane < HEAD_DIM
    gq = GROUP * t
    tq = lax.broadcasted_iota(jnp.int32, (1, gq, 1), 1) % t
    mask_c = lax.broadcasted_iota(jnp.int32, (1, gq, WINDOW), 2) > tq
    mask_n = lax.broadcasted_iota(jnp.int32, (1, gq, t), 2) <= tq
    grow = lax.broadcasted_iota(jnp.int32, (1, gq, 1), 1) // t
    bdims = (((2,), (2,)), ((0,), (0,)))
    pdims = (((2,), (1,)), ((0,), (0,)))
    outs = []
    for kv in range(N_KV_HEADS):
        sel = lane_lo if kv == 0 else jnp.logical_not(lane_lo)
        q3 = jnp.concatenate([jnp.where(sel, qc, 0.0).reshape(bt_, t, LANES) for qc in qcols], axis=1).astype(BF16)
        sc = lax.dot_general(q3, ckb, bdims, preferred_element_type=F32)
        sn = lax.dot_general(q3, k3b, bdims, preferred_element_type=F32)
        sc = jnp.where(mask_c, sc, NEG_BIG)
        sn = jnp.where(mask_n, sn, NEG_BIG)
        sink_col = jnp.zeros((1, gq, 1), F32)
        for g in range(GROUP):
            sink_col = jnp.where(grow == g, sinks_ref[kv * GROUP + g], sink_col)
        m = jnp.maximum(jnp.maximum(jnp.max(sc, axis=-1, keepdims=True), jnp.max(sn, axis=-1, keepdims=True)),
                        sink_col)
        pc = jnp.exp(sc - m)
        pn = jnp.exp(sn - m)
        denom = jnp.sum(pc, axis=-1, keepdims=True) + jnp.sum(pn, axis=-1, keepdims=True) + jnp.exp(sink_col - m)
        o = (lax.dot_general(pc.astype(BF16), cvb, pdims, preferred_element_type=F32)
             + lax.dot_general(pn.astype(BF16), v3b, pdims, preferred_element_type=F32)) / denom
        outs.append(o)
    attn = jnp.concatenate(
        [jnp.where(lane_lo, outs[0][:, c * t:(c + 1) * t, :], outs[1][:, c * t:(c + 1) * t, :]).reshape(rows, LANES)
         for c in range(4)], axis=1)

    mix = jnp.concatenate([lru_out, attn], axis=1)
    x1, h2, logits = _post_mix(x, mix, (mods[2], mods[3], mods[4]), w)
    x1_ref[...] = x1
    h2_ref[...] = h2
    route, tot = _route_tile(logits)
    lg_ref[0][...] = route
    lg_ref[1][...] = jnp.broadcast_to(tot, lg_ref[1].shape)
    g2_ref[...] = mods[5]


def _prompt_kernel(steps_per_seq, x_ref, mod_ref, cos_ref, sin_ref, sinks_ref, *rest):
    nw = len(WEIGHT_NAMES)
    w = dict(zip(WEIGHT_NAMES, rest[:nw]))
    (x1_ref, h2_ref, route_ref, cnt_ref, hlast_ref, ulast_ref, klast_ref, vlast_ref,
     conv_c, h_c, k_c, v_c) = rest[nw:]
    seq_start = pl.program_id(0) % steps_per_seq == 0
    for sub in range(MIX_TILES):
        rows = pl.ds(sub * SEQ_TILE, SEQ_TILE)
        _prompt_body(seq_start if sub == 0 else None, x_ref.at[rows], mod_ref, cos_ref.at[rows], sin_ref.at[rows],
                     sinks_ref, w, x1_ref.at[rows], h2_ref.at[rows], (route_ref.at[rows], cnt_ref.at[sub]),
                     hlast_ref, ulast_ref, klast_ref, vlast_ref, conv_c, h_c, k_c, v_c)


def _sample_kernel(x_ref, mod_ref, cos_ref, sin_ref, sinks_ref, h0_ref, cprev_ref, ck_ref, cv_ref, *rest):
    nw = len(WEIGHT_NAMES)
    w = dict(zip(WEIGHT_NAMES, rest[:nw]))
    x1_ref, h2_ref, route_ref, cnt_ref, g2_ref, hs_ref, u_ref, ko_ref, vo_ref = rest[nw:]
    _sample_body(x_ref, mod_ref, cos_ref, sin_ref, sinks_ref, h0_ref, cprev_ref, ck_ref, cv_ref, w,
                 x1_ref, h2_ref, (route_ref, cnt_ref), g2_ref, hs_ref, u_ref, ko_ref, vo_ref)


PACK_W = D_MODEL // 2
PACKED = jnp.int32


def _pack_bf16_pairs(x):
    return pltpu.pack_elementwise([x[:, :PACK_W], x[:, PACK_W:]], packed_dtype=BF16)


def _unpack_bf16_pairs(w):
    return tuple(pltpu.unpack_elementwise(w, index=k, packed_dtype=BF16, unpacked_dtype=F32).astype(BF16)
                 for k in range(2))


STRIP_SIZES = tuple(SUBLANES << b for b in range(6))
STRIP_LARGE = 64
SORT_ROWS = SEQ_TILE * TOP_K + N_EXPERTS * SUBLANES
TILE_WAIT_SIZES = tuple(SUBLANES << b for b in range(8))


def _for_strips(cnt_ref, off_ref, tile, buf_slot, hbm, sem, to_hbm, act):
    def e_body(e, local):
        n = cnt_ref[tile * N_EXPERTS + e]
        glob = off_ref[tile * N_EXPERTS + e]

        def pieces(sizes, done):
            for p in sizes:
                piece = n & p
                lo = pl.ds(pl.multiple_of(local + done, SUBLANES), p)
                gl = pl.ds(pl.multiple_of(glob + done, SUBLANES), p)

                @pl.when(piece != 0)
                def _():
                    if to_hbm:
                        act(pltpu.make_async_copy(buf_slot.at[lo], hbm.at[gl], sem))
                    else:
                        act(pltpu.make_async_copy(hbm.at[gl], buf_slot.at[lo], sem))
                done = done + piece

        large = tuple(p for p in reversed(STRIP_SIZES) if p >= STRIP_LARGE)
        small = tuple(p for p in reversed(STRIP_SIZES) if p < STRIP_LARGE)
        n_large = n & (-STRIP_LARGE)

        @pl.when(n_large != 0)
        def _():
            pieces(large, 0)
        pieces(small, n_large)
        return local + n
    lax.fori_loop(0, N_EXPERTS, e_body, 0)


def _wait_tile_rows(total, buf_slot, hbm, sem, to_hbm):
    for p in TILE_WAIT_SIZES:
        @pl.when((total & p) != 0)
        def _():
            if to_hbm:
                pltpu.make_async_copy(buf_slot.at[pl.ds(0, p)], hbm.at[pl.ds(0, p)], sem).wait()
            else:
                pltpu.make_async_copy(hbm.at[pl.ds(0, p)], buf_slot.at[pl.ds(0, p)], sem).wait()


def _dispatch_kernel(n_prompt_tiles, cnt_ref, off_ref, tot_ref, meta_ref, h2p_ref, h2s_ref, routep_ref, routes_ref,
                     xs_hbm, sbuf, zblk, sem, zsem):
    i = pl.program_id(0)
    nb = pl.num_programs(0)
    slot = i % 2
    n_blocks = xs_hbm.shape[0] // MOE_TM

    is_prompt = i < n_prompt_tiles
    h2 = jnp.where(is_prompt, h2p_ref[...], h2s_ref[...])
    route_t = jnp.where(is_prompt, routep_ref[...], routes_ref[...]).T
    r_pos = lax.broadcasted_iota(jnp.int16, (SORT_ROWS, SEQ_TILE), 0)
    perm = jnp.zeros((SORT_ROWS, SEQ_TILE), BF16)
    for k in range(TOP_K):
        pos_k = route_t[ROUTE_R + k:ROUTE_R + k + 1, :].astype(jnp.int32).astype(jnp.int16)
        perm = jnp.where(r_pos == pos_k, jnp.ones((), BF16), perm)
    sbuf[slot] = _pack_bf16_pairs(jnp.dot(perm, h2, preferred_element_type=F32))

    _for_strips(cnt_ref, off_ref, i, sbuf.at[slot], xs_hbm, sem.at[slot], True, lambda cp: cp.start())

    @pl.when(i > 0)
    def _():
        _wait_tile_rows(tot_ref[jnp.maximum(i - 1, 0)], sbuf.at[1 - slot], xs_hbm, sem.at[1 - slot], True)

    @pl.when(i == nb - 1)
    def _():
        _wait_tile_rows(tot_ref[i], sbuf.at[slot], xs_hbm, sem.at[slot], True)
        zblk[...] = jnp.zeros_like(zblk)

        def for_region_tails(act):
            def e_body(e, carry):
                start = meta_ref[e]
                n = meta_ref[N_EXPERTS + e] - start
                done = 0
                for p in reversed([q for q in STRIP_SIZES if q < MOE_TM]):
                    piece = n & p
                    rows = pl.ds(pl.multiple_of(start + done, SUBLANES), p)

                    @pl.when(piece != 0)
                    def _():
                        act(pltpu.make_async_copy(zblk.at[pl.ds(0, p)], xs_hbm.at[rows], zsem.at[0]))
                    done = done + piece
                return carry
            lax.fori_loop(0, N_EXPERTS, e_body, 0)

        def for_tail_blocks(act):
            def b_body(j, carry):
                act(pltpu.make_async_copy(zblk, xs_hbm.at[pl.ds(pl.multiple_of(j * MOE_TM, MOE_TM), MOE_TM)],
                                          zsem.at[0]))
                return carry
            lax.fori_loop(meta_ref[2 * N_EXPERTS], n_blocks, b_body, 0)

        for_region_tails(lambda cp: cp.start())
        for_tail_blocks(lambda cp: cp.start())
        for_region_tails(lambda cp: cp.wait())
        for_tail_blocks(lambda cp: cp.wait())


def _dispatch(cnt8, tile_off, tot8, meta, h2_p, h2_s, route_p, route_s, n_rows):
    npt = h2_p.shape[0] // SEQ_TILE
    p_tile = lambda i, *_: (jnp.minimum(i, npt - 1), 0)
    s_tile = lambda i, *_: (jnp.maximum(i - npt, 0), 0)
    nt = tot8.shape[0]
    tt = SEQ_TILE
    grid_spec = pltpu.PrefetchScalarGridSpec(
        num_scalar_prefetch=4,
        grid=(nt,),
        in_specs=[pl.BlockSpec((tt, D_MODEL), p_tile), pl.BlockSpec((tt, D_MODEL), s_tile),
                  pl.BlockSpec((tt, LANES), p_tile), pl.BlockSpec((tt, LANES), s_tile)],
        out_specs=pl.BlockSpec(memory_space=pl.ANY),
        scratch_shapes=[pltpu.VMEM((2, SORT_ROWS, PACK_W), PACKED), pltpu.VMEM((MOE_TM, PACK_W), PACKED),
                        pltpu.SemaphoreType.DMA((2,)), pltpu.SemaphoreType.DMA((1,))],
    )
    return pl.pallas_call(
        functools.partial(_dispatch_kernel, npt),
        grid_spec=grid_spec,
        out_shape=jax.ShapeDtypeStruct((n_rows, PACK_W), PACKED),
        compiler_params=pltpu.CompilerParams(dimension_semantics=("arbitrary",), vmem_limit_bytes=VMEM_LIMIT),
        name="dispatch",
    )(cnt8, tile_off, tot8, meta, h2_p, h2_s, route_p, route_s)


def _expert_mlp(words, w1b, b1_ref, w2b, b2_ref, act_ref):
    xb = jnp.concatenate(_unpack_bf16_pairs(words), axis=1)
    q = D_FF // 4
    for c in range(4):
        zg = jnp.dot(xb, w1b[:, c * q:(c + 1) * q], preferred_element_type=F32) + b1_ref[:, c * q:(c + 1) * q]
        zl = (jnp.dot(xb, w1b[:, D_FF + c * q:D_FF + (c + 1) * q], preferred_element_type=F32)
              + b1_ref[:, D_FF + c * q:D_FF + (c + 1) * q])
        glu = jnp.minimum(zg, SWIGLU_LIMIT)
        lin = jnp.clip(zl, -SWIGLU_LIMIT, SWIGLU_LIMIT)
        act_ref[:, c * q:(c + 1) * q] = (glu * jax.nn.sigmoid(SWIGLU_ALPHA * glu) * (lin + 1.0)).astype(BF16)
    return _pack_bf16_pairs(jnp.dot(act_ref[...], w2b[...], preferred_element_type=F32) + b2_ref[...])


def _moe_kernel(row_ref, size_ref, exp_ref, wt_ref, meta_ref, xs_hbm, w1_hbm, b1_ref, w2_hbm, b2_ref, ys_hbm,
                xin, yout, w1f, w2f, w1b, w2b, act, isem, osem, wsem):
    tm = MOE_TM
    n_blocks = ys_hbm.shape[0] // tm
    n_tbl = row_ref.shape[0]
    n = meta_ref[2 * N_EXPERTS + 1]

    def for_chunk(j, fn):
        r = pl.multiple_of(row_ref[j], tm)
        for code in MOE_CHUNK_CODES:
            @pl.when(size_ref[j] == code)
            def _():
                fn(r, code * tm)

    def in_copy(r, rows, s):
        return pltpu.make_async_copy(xs_hbm.at[pl.ds(r, rows)], xin.at[s, pl.ds(0, rows)], isem.at[s])

    def out_copy(r, rows, s):
        return pltpu.make_async_copy(yout.at[s, pl.ds(0, rows)], ys_hbm.at[pl.ds(r, rows)], osem.at[s])

    def start_in(j, s):
        for_chunk(j, lambda r, rows: in_copy(r, rows, s).start())

    def wait_in(j, s):
        for_chunk(j, lambda r, rows: in_copy(r, rows, s).wait())

    def start_out(j, s):
        for_chunk(j, lambda r, rows: out_copy(r, rows, s).start())

    def wait_out(j, s):
        for_chunk(j, lambda r, rows: out_copy(r, rows, s).wait())

    def weight_copies(e, ws):
        return (pltpu.make_async_copy(w1_hbm.at[e], w1f.at[ws], wsem.at[ws]),
                pltpu.make_async_copy(w2_hbm.at[e], w2f.at[ws], wsem.at[ws]))

    start_in(0, 0)
    for cp in weight_copies(exp_ref[0], 0):
        cp.start()

    def trip(i, carry):
        slot = i % 2

        @pl.when(i + 1 < n)
        def _():
            start_in(i + 1, 1 - slot)

        @pl.when(i >= 2)
        def _():
            wait_out(i - 2, slot)

        @pl.when(wt_ref[i] == 1)
        def _():
            ws = wt_ref[n_tbl + i]
            nxt = wt_ref[2 * n_tbl + i]
            for cp in weight_copies(exp_ref[i], ws):
                cp.wait()

            @pl.when(nxt >= 0)
            def _():
                for cp in weight_copies(nxt, 1 - ws):
                    cp.start()

            chunk = 128
            def cast_body(c, carry2):
                k0 = pl.multiple_of(c * chunk, chunk)
                w1b[pl.ds(k0, chunk), :] = w1f[ws, pl.ds(k0, chunk), :].astype(BF16)
                w2b[pl.ds(k0, chunk), :] = w2f[ws, pl.ds(k0, chunk), :].astype(BF16)
                return carry2
            lax.fori_loop(0, D_MODEL // chunk, cast_body, 0)

        wait_in(i, slot)

        for code in MOE_CHUNK_CODES:
            rows = code * tm

            @pl.when(size_ref[i] == code)
            def _():
                yout[slot, 0:rows] = _expert_mlp(xin[slot, 0:rows], w1b, b1_ref.at[exp_ref[i]], w2b,
                                                 b2_ref.at[exp_ref[i]], act.at[pl.ds(0, rows)])

        start_out(i, slot)
        return carry
    lax.fori_loop(0, n, trip, 0)

    @pl.when(n >= 2)
    def _():
        wait_out(n - 2, n % 2)
    wait_out(n - 1, (n - 1) % 2)

    yout[1, 0:tm] = jnp.zeros((tm, PACK_W), PACKED)

    def zero_block(j):
        return pltpu.make_async_copy(yout.at[1, pl.ds(0, tm)], ys_hbm.at[pl.ds(pl.multiple_of(j * tm, tm), tm)],
                                     osem.at[1])

    def start_body(j, carry):
        zero_block(j).start()
        return carry

    def wait_body(j, carry):
        zero_block(j).wait()
        return carry
    lax.fori_loop(meta_ref[2 * N_EXPERTS], n_blocks, start_body, 0)
    lax.fori_loop(meta_ref[2 * N_EXPERTS], n_blocks, wait_body, 0)


def _moe(chunk_row, chunk_size, chunk_exp, weight_tbl, meta, xs, w1, b1, w2, b2):
    whole = lambda i, *_: (0, 0, 0)
    grid_spec = pltpu.PrefetchScalarGridSpec(
        num_scalar_prefetch=5,
        grid=(1,),
        in_specs=[
            pl.BlockSpec(memory_space=pl.ANY),
            pl.BlockSpec(memory_space=pl.ANY),
            pl.BlockSpec((N_EXPERTS, 1, 2 * D_FF), whole),
            pl.BlockSpec(memory_space=pl.ANY),
            pl.BlockSpec((N_EXPERTS, 1, D_MODEL), whole),
        ],
        out_specs=pl.BlockSpec(memory_space=pl.ANY),
        scratch_shapes=[pltpu.VMEM((2, MOE_CH, PACK_W), PACKED), pltpu.VMEM((2, MOE_CH, PACK_W), PACKED),
                        pltpu.VMEM((2, D_MODEL, 2 * D_FF), F32), pltpu.VMEM((2, D_FF, D_MODEL), F32),
                        pltpu.VMEM((D_MODEL, 2 * D_FF), BF16), pltpu.VMEM((D_FF, D_MODEL), BF16),
                        pltpu.VMEM((MOE_CH, D_FF), BF16),
                        pltpu.SemaphoreType.DMA((2,)), pltpu.SemaphoreType.DMA((2,)), pltpu.SemaphoreType.DMA((2,))],
    )
    return pl.pallas_call(
        _moe_kernel,
        grid_spec=grid_spec,
        out_shape=jax.ShapeDtypeStruct(xs.shape, PACKED),
        compiler_params=pltpu.CompilerParams(dimension_semantics=("arbitrary",), vmem_limit_bytes=VMEM_LIMIT),
        name="moe",
    )(chunk_row, chunk_size, chunk_exp, weight_tbl, meta, xs, w1, b1, w2, b2)


def _combine_kernel(n_prompt_tiles, cnt_ref, off_ref, tot_ref, ys_hbm, routep_ref, routes_ref, x1p_ref, x1s_ref,
                    modp_ref, g2s_ref, fg_ref, op_ref, os_ref, buf, sem):
    i = pl.program_id(0)
    nb = pl.num_programs(0)
    slot = i % 2

    def fetch(tile, s):
        _for_strips(cnt_ref, off_ref, tile, buf.at[s], ys_hbm, sem.at[s], False, lambda cp: cp.start())

    @pl.when(i == 0)
    def _():
        buf[...] = jnp.zeros_like(buf)
        fetch(0, 0)

    @pl.when(i + 1 < nb)
    def _():
        fetch(i + 1, 1 - slot)

    _wait_tile_rows(tot_ref[i], buf.at[slot], ys_hbm, sem.at[slot], False)

    is_prompt = i < n_prompt_tiles
    route = jnp.where(is_prompt, routep_ref[...], routes_ref[...])
    c_pos = lax.broadcasted_iota(jnp.int16, (SEQ_TILE, SORT_ROWS), 1)
    g_bf = jnp.zeros((SEQ_TILE, SORT_ROWS), BF16)
    for k in range(TOP_K):
        pos_k = route[:, ROUTE_R + k:ROUTE_R + k + 1].astype(jnp.int32).astype(jnp.int16)
        g_bf = jnp.where(c_pos == pos_k, route[:, ROUTE_G + k:ROUTE_G + k + 1].astype(BF16), g_bf)
    ff = jnp.concatenate([jnp.dot(g_bf, yb, preferred_element_type=F32) for yb in _unpack_bf16_pairs(buf[slot])],
                         axis=1)
    g2 = jnp.where(i < n_prompt_tiles, modp_ref[5:6, :], g2s_ref[...])
    x = jnp.where(is_prompt, x1p_ref[...], x1s_ref[...]) + g2 * ff
    y = _rms(x, fg_ref[...])

    @pl.when(i < n_prompt_tiles)
    def _():
        op_ref[...] = y

    @pl.when(i >= n_prompt_tiles)
    def _():
        os_ref[...] = y


def _combine(cnt8, tile_off, tot8, ys, route_p, route_s, x1_p, x1_s, mod_p, g2_rows, final_g, tiles_per_batch):
    nt = tot8.shape[0]
    tt = SEQ_TILE
    npt = x1_p.shape[0] // tt
    p_tile = lambda i, *_: (jnp.minimum(i, npt - 1), 0)
    s_tile = lambda i, *_: (jnp.maximum(i - npt, 0), 0)
    grid_spec = pltpu.PrefetchScalarGridSpec(
        num_scalar_prefetch=3,
        grid=(nt,),
        in_specs=[
            pl.BlockSpec(memory_space=pl.ANY),
            pl.BlockSpec((tt, LANES), p_tile), pl.BlockSpec((tt, LANES), s_tile),
            pl.BlockSpec((tt, D_MODEL), p_tile), pl.BlockSpec((tt, D_MODEL), s_tile),
            pl.BlockSpec((None, 6, D_MODEL), lambda i, *_: (jnp.minimum(i, npt - 1) // tiles_per_batch, 0, 0)),
            pl.BlockSpec((tt, D_MODEL), s_tile),
            pl.BlockSpec((1, D_MODEL), lambda i, *_: (0, 0)),
        ],
        out_specs=(pl.BlockSpec((tt, D_MODEL), p_tile), pl.BlockSpec((tt, D_MODEL), s_tile)),
        scratch_shapes=[pltpu.VMEM((2, SORT_ROWS, PACK_W), PACKED), pltpu.SemaphoreType.DMA((2,))],
    )
    return pl.pallas_call(
        functools.partial(_combine_kernel, npt),
        grid_spec=grid_spec,
        out_shape=(jax.ShapeDtypeStruct((npt * tt, D_MODEL), F32),
                   jax.ShapeDtypeStruct(((nt - npt) * tt, D_MODEL), F32)),
        compiler_params=pltpu.CompilerParams(dimension_semantics=("arbitrary",), vmem_limit_bytes=VMEM_LIMIT),
        name="combine",
    )(cnt8, tile_off, tot8, ys, route_p, route_s, x1_p, x1_s, mod_p, g2_rows, final_g)


def _block_diag_halves(wg):
    per_half = LRU_BLOCKS // 2
    w4 = wg.reshape(2, per_half, LRU_BLOCK_W, LRU_BLOCK_W)
    on_diag = jnp.eye(per_half, dtype=bool)[None, :, None, :, None]
    dense = jnp.where(on_diag, w4[:, :, :, None, :], 0.0)
    return dense.reshape(2, per_half * LRU_BLOCK_W, per_half * LRU_BLOCK_W).astype(BF16)


def _rope_tables(pos):
    half = HEAD_DIM // 2
    inv = ROPE_THETA ** (-jnp.arange(half, dtype=F32) / half)
    ang = pos.astype(F32)[:, None] * inv[None, :]
    cos = jnp.cos(ang)
    sin = jnp.sin(ang)
    cos128 = jnp.concatenate([cos, cos, cos, cos], axis=1)
    sin128 = jnp.concatenate([-sin, sin, -sin, sin], axis=1)
    return cos128, sin128


def _resident_spec(arr):
    zeros = (0,) * arr.ndim
    return pl.BlockSpec(arr.shape, lambda i: zeros)


def kernel(x_prompt, x_sample, state_lru_h, state_conv, cache_win_k, cache_win_v, c_prompt, c_sample, w_ada, b_ada, norm1_g, w_in, b_in, conv_w, conv_b, lru_wa, lru_ba, lru_wx, lru_bx, lru_lambda, attn_sinks, w_out, b_out, norm2_g, w_router, b_router, w1, b1, w2, b2, final_g):
    bp, seq, _ = x_prompt.shape
    bd, tdec, _ = x_sample.shape
    assert tdec == SUBLANES and seq % (MIX_TILES * SEQ_TILE) == 0 and bd % SAMPLE_BT == 0
    assert SAMPLE_BT * tdec == SEQ_TILE
    n_prompt = bp * seq
    n_sample = bd * tdec
    n_tok = n_prompt + n_sample
    l = 0

    head_perm = [h for c in range(4) for h in (c, GROUP + c)]
    o2 = 2 * LRU_WIDTH

    def permute_heads(arr, axis, start):
        take = lambda a, b: lax.slice_in_dim(arr, a, b, axis=axis)
        heads = [take(start + h * HEAD_DIM, start + (h + 1) * HEAD_DIM) for h in head_perm]
        return jnp.concatenate([take(0, start)] + heads + [take(start + Q_WIDTH, arr.shape[axis])], axis=axis)

    sinks_perm = attn_sinks[l]

    wr = jnp.tile(w_router[l], (1, TOP_K))
    wr_hi = wr.astype(BF16)
    weights = dict(
        norm1_g=norm1_g[l][None, :], w_in=permute_heads(w_in[l], 1, o2).astype(BF16),
        b_in=permute_heads(b_in[l], 0, o2)[None, :],
        conv_w=conv_w[l], conv_b=conv_b[l][None, :],
        ga=_block_diag_halves(lru_wa[l]), gx=_block_diag_halves(lru_wx[l]),
        lru_ba=lru_ba[l][None, :], lru_bx=lru_bx[l][None, :], lam=lru_lambda[l][None, :],
        w_out=permute_heads(w_out[l], 0, LRU_WIDTH).astype(BF16), b_out=b_out[l][None, :], norm2_g=norm2_g[l][None, :],
        wr_hi=wr_hi, wr_lo=(wr - wr_hi.astype(F32)).astype(BF16),
        b_router=jnp.tile(b_router[l], TOP_K)[None, :],
    )
    wlist = [weights[n] for n in WEIGHT_NAMES]

    mod_p, mod_s = _ada(jnp.concatenate([c_prompt, c_sample], axis=0), bp, w_ada[l], b_ada[l][None, :])
    mod_p = mod_p.reshape(bp, 6, D_MODEL)

    cos_p, sin_p = _rope_tables(jnp.arange(seq, dtype=jnp.int32))
    cos_s, sin_s = _rope_tables(PAST_LEN + jnp.arange(tdec, dtype=jnp.int32))
    cos_s = jnp.tile(cos_s, (SAMPLE_BT, 1))
    sin_s = jnp.tile(sin_s, (SAMPLE_BT, 1))
    h0_rows = jnp.pad(state_lru_h[l][:, None, :], ((0, 0), (0, tdec - 1), (0, 0))).reshape(n_sample, LRU_WIDTH)
    cprev_rows = jnp.pad(state_conv[l], ((0, 0), (0, tdec - (CONV_W - 1)), (0, 0))).reshape(n_sample, LRU_WIDTH)
    ck = cache_win_k[l].reshape(bd, WINDOW, KV_WIDTH)
    cv = cache_win_v[l].reshape(bd, WINDOW, KV_WIDTH)
    npt = n_prompt // SEQ_TILE
    nst = n_sample // SEQ_TILE
    mix_rows = MIX_TILES * SEQ_TILE
    steps_per_seq = seq // mix_rows
    wspecs = [_resident_spec(a) for a in wlist]

    rows_p = lambda width: pl.BlockSpec((mix_rows, width), lambda i: (i, 0))
    tail_p = lambda rows, width: pl.BlockSpec((None, rows, width), lambda i: (i // steps_per_seq, 0, 0))
    (x1_p, h2_p, route_p, cnt_p, hlast_p, ulast_p, klast_p, vlast_p) = pl.pallas_call(
        functools.partial(_prompt_kernel, steps_per_seq),
        grid=(n_prompt // mix_rows,),
        in_specs=[rows_p(D_MODEL),
                  pl.BlockSpec((None, 6, D_MODEL), lambda i: (i // steps_per_seq, 0, 0)),
                  pl.BlockSpec((mix_rows, LANES), lambda i: (i % steps_per_seq, 0)),
                  pl.BlockSpec((mix_rows, LANES), lambda i: (i % steps_per_seq, 0)),
                  pl.BlockSpec(memory_space=pltpu.SMEM)] + wspecs,
        out_specs=(rows_p(D_MODEL), rows_p(D_MODEL), rows_p(LANES),
                   pl.BlockSpec((MIX_TILES, SUBLANES, LANES), lambda i: (i, 0, 0)),
                   tail_p(SUBLANES, LRU_WIDTH), tail_p(SUBLANES, LRU_WIDTH),
                   tail_p(WINDOW, KV_WIDTH), tail_p(WINDOW, KV_WIDTH)),
        out_shape=(
            jax.ShapeDtypeStruct((n_prompt, D_MODEL), F32),
            jax.ShapeDtypeStruct((n_prompt, D_MODEL), BF16),
            jax.ShapeDtypeStruct((n_prompt, LANES), F32),
            jax.ShapeDtypeStruct((npt, SUBLANES, LANES), F32),
            jax.ShapeDtypeStruct((bp, SUBLANES, LRU_WIDTH), F32),
            jax.ShapeDtypeStruct((bp, SUBLANES, LRU_WIDTH), F32),
            jax.ShapeDtypeStruct((bp, WINDOW, KV_WIDTH), F32),
            jax.ShapeDtypeStruct((bp, WINDOW, KV_WIDTH), F32),
        ),
        scratch_shapes=[pltpu.VMEM((SUBLANES, LRU_WIDTH), F32), pltpu.VMEM((SUBLANES, LRU_WIDTH), F32),
                        pltpu.VMEM((WINDOW, KV_WIDTH), F32), pltpu.VMEM((WINDOW, KV_WIDTH), F32)],
        compiler_params=pltpu.CompilerParams(dimension_semantics=("arbitrary",), vmem_limit_bytes=VMEM_LIMIT),
        name="prompt_mixer",
    )(x_prompt.reshape(n_prompt, D_MODEL), mod_p, cos_p, sin_p, sinks_perm, *wlist)

    rows_s = lambda width: pl.BlockSpec((SEQ_TILE, width), lambda i: (i, 0))
    cache_spec = pl.BlockSpec((SAMPLE_BT, WINDOW, KV_WIDTH), lambda i: (i, 0, 0))
    (x1_s, h2_s, route_s, cnt_s, g2_rows, hs_s, u_s, s_k, s_v) = pl.pallas_call(
        _sample_kernel,
        grid=(nst,),
        in_specs=[rows_s(D_MODEL),
                  pl.BlockSpec((6, SAMPLE_BT, D_MODEL), lambda i: (0, i, 0)),
                  pl.BlockSpec((SEQ_TILE, LANES), lambda i: (0, 0)),
                  pl.BlockSpec((SEQ_TILE, LANES), lambda i: (0, 0)),
                  pl.BlockSpec(memory_space=pltpu.SMEM),
                  rows_s(LRU_WIDTH), rows_s(LRU_WIDTH), cache_spec, cache_spec] + wspecs,
        out_specs=(rows_s(D_MODEL), rows_s(D_MODEL), rows_s(LANES),
                   pl.BlockSpec((None, SUBLANES, LANES), lambda i: (i, 0, 0)),
                   rows_s(D_MODEL), rows_s(LRU_WIDTH), rows_s(LRU_WIDTH), cache_spec, cache_spec),
        out_shape=(
            jax.ShapeDtypeStruct((n_sample, D_MODEL), F32), jax.ShapeDtypeStruct((n_sample, D_MODEL), BF16),
            jax.ShapeDtypeStruct((n_sample, LANES), F32), jax.ShapeDtypeStruct((nst, SUBLANES, LANES), F32),
            jax.ShapeDtypeStruct((n_sample, D_MODEL), F32),
            jax.ShapeDtypeStruct((n_sample, LRU_WIDTH), F32),
            jax.ShapeDtypeStruct((n_sample, LRU_WIDTH), F32),
            jax.ShapeDtypeStruct((bd, WINDOW, KV_WIDTH), F32),
            jax.ShapeDtypeStruct((bd, WINDOW, KV_WIDTH), F32),
        ),
        compiler_params=pltpu.CompilerParams(dimension_semantics=("arbitrary",), vmem_limit_bytes=VMEM_LIMIT),
        name="sample_mixer",
    )(x_sample.reshape(n_sample, D_MODEL), mod_s, cos_s, sin_s, sinks_perm, h0_rows, cprev_rows, ck, cv, *wlist)
    tile_cnt = jnp.concatenate([cnt_p, cnt_s], axis=0)

    n_tiles = npt + nst
    n_assign = n_tok * TOP_K
    max_rows = n_assign + n_tiles * N_EXPERTS * (SUBLANES - 1) + N_EXPERTS * (MOE_TM - 1)
    n_blocks = -(-max_rows // MOE_TM)
    cnt = tile_cnt[:, 0, :N_EXPERTS].astype(jnp.int32)
    cnt8 = (cnt + SUBLANES - 1) // SUBLANES * SUBLANES
    counts = jnp.sum(cnt8, axis=0)
    pcounts = (counts + MOE_TM - 1) // MOE_TM * MOE_TM
    pend = jnp.cumsum(pcounts)
    pstart = pend - pcounts
    tile_off = pstart[None, :] + jnp.cumsum(cnt8, axis=0) - cnt8
    tot8 = jnp.sum(cnt8, axis=1)
    meta = jnp.concatenate([pstart + counts, pend, pend[-1:] // MOE_TM]).astype(jnp.int32)
    cnt8_flat = cnt8.reshape(-1)
    off_flat = tile_off.reshape(-1).astype(jnp.int32)

    xs = _dispatch(cnt8_flat, off_flat, tot8, meta, h2_p, h2_s, route_p, route_s, n_blocks * MOE_TM)

    big, small = MOE_CHUNK_CODES[0], MOE_CHUNK_CODES[1:]
    n_tm = pcounts // MOE_TM
    n_big = n_tm // big
    present = [(n_tm // b) % 2 for b in small]
    n_ch = n_big + sum(present)
    ch_end = jnp.cumsum(n_ch)
    ch_start = ch_end - n_ch
    n_chunks = n_blocks // big + len(small) * N_EXPERTS
    ci = jnp.arange(n_chunks, dtype=jnp.int32)
    owner = jnp.logical_and(ch_start[None, :] <= ci[:, None], ci[:, None] < ch_end[None, :])
    pick = lambda v: jnp.sum(jnp.where(owner, v[None, :], 0), axis=1)
    live = ci < ch_end[-1]
    local = ci - pick(ch_start)
    chunk_exp = jnp.where(live, pick(jnp.arange(N_EXPERTS, dtype=jnp.int32)), N_EXPERTS - 1).astype(jnp.int32)
    big_c = pick(n_big)
    code = jnp.where(local < big_c, big, 0)
    blocks_before = jnp.where(local < big_c, local * big, 0)
    order = big_c
    blocks = big_c * big
    for b, has in zip(small, present):
        has_c = pick(has)
        is_b = jnp.logical_and(has_c == 1, local == order)
        code = jnp.where(is_b, b, code)
        blocks_before = jnp.where(is_b, blocks, blocks_before)
        order = order + has_c
        blocks = blocks + has_c * b
    chunk_row = jnp.where(live, pick(pstart) + blocks_before * MOE_TM, 0).astype(jnp.int32)
    chunk_size = jnp.where(live, code, 0).astype(jnp.int32)
    has_rows = n_ch > 0
    e_ar = jnp.arange(N_EXPERTS, dtype=jnp.int32)
    w_slot = (jnp.cumsum(has_rows.astype(jnp.int32)) - 1) % 2
    later = jnp.logical_and(has_rows[None, :], e_ar[None, :] > e_ar[:, None])
    nxt = jnp.min(jnp.where(later, e_ar[None, :], N_EXPERTS), axis=1)
    nxt = jnp.where(nxt == N_EXPERTS, -1, nxt)
    weight_tbl = jnp.concatenate([jnp.logical_and(live, local == 0).astype(jnp.int32),
                                  pick(w_slot), jnp.where(live, pick(nxt), -1)]).astype(jnp.int32)
    moe_meta = jnp.concatenate([meta, ch_end[-1:].astype(jnp.int32)])
    ys = _moe(chunk_row, chunk_size, chunk_exp, weight_tbl, moe_meta, xs, w1[l], b1[l][:, None, :], w2[l], b2[l][:, None, :])

    y_p, y_s = _combine(cnt8_flat, off_flat, tot8, ys, route_p, route_s, x1_p, x1_s, mod_p, g2_rows,
                        final_g[None, :], seq // SEQ_TILE)

    y_prompt = y_p.reshape(bp, seq, D_MODEL)
    y_sample = y_s.reshape(bd, tdec, D_MODEL)
    p_h = hlast_p[:, SUBLANES - 1, :][None]
    p_c = ulast_p[:, SUBLANES - (CONV_W - 1):, :][None]
    p_k = klast_p.reshape(1, bp, WINDOW, N_KV_HEADS, HEAD_DIM)
    p_v = vlast_p.reshape(1, bp, WINDOW, N_KV_HEADS, HEAD_DIM)
    s_h = hs_s.reshape(bd, tdec, LRU_WIDTH)[:, tdec - 1, :][None]
    s_c = u_s.reshape(bd, tdec, LRU_WIDTH)[:, tdec - (CONV_W - 1):, :][None]
    s_kk = s_k.reshape(1, bd, WINDOW, N_KV_HEADS, HEAD_DIM)
    s_vv = s_v.reshape(1, bd, WINDOW, N_KV_HEADS, HEAD_DIM)
    return (y_prompt, y_sample, p_h, p_c, p_k, p_v, s_h, s_c, s_kk, s_vv)
```

```python
import functools

import jax
import jax.numpy as jnp
from jax import lax
from jax.experimental import pallas as pl
from jax.experimental.pallas import tpu as pltpu

F32 = jnp.float32
BF16 = jnp.bfloat16

D_MODEL = 1024
LRU_WIDTH = 512
LRU_BLOCKS = 8
LRU_BLOCK_W = LRU_WIDTH // LRU_BLOCKS
CONV_W = 4
LRU_C = 8.0
HEAD_DIM = 64
N_HEADS = 8
N_KV_HEADS = 2
GROUP = N_HEADS // N_KV_HEADS
WINDOW = 128
ROPE_THETA = 10000.0
N_EXPERTS = 32
TOP_K = 4
D_FF = D_MODEL
SWIGLU_LIMIT = 7.0
SWIGLU_ALPHA = 1.702
NORM_EPS = 1e-5
PAST_LEN = 8192
Q_WIDTH = N_HEADS * HEAD_DIM
KV_WIDTH = N_KV_HEADS * HEAD_DIM
IN_WIDTH = 2 * LRU_WIDTH + Q_WIDTH + 2 * KV_WIDTH

LANES = 128
SUBLANES = 8
SEQ_TILE = 256
MIX_TILES = 2
SAMPLE_BT = 32
MOE_TM = 256
MOE_CHUNK_CODES = (4, 2, 1)
MOE_CH = MOE_CHUNK_CODES[0] * MOE_TM
NEG_BIG = -1e30
VMEM_LIMIT = 56 * 1024 * 1024


def _rms(x, g):
    return x * lax.rsqrt(jnp.mean(x * x, axis=-1, keepdims=True) + NORM_EPS) * g


def _rms_mod(x, gain, shift):
    return x * lax.rsqrt(jnp.mean(x * x, axis=-1, keepdims=True) + NORM_EPS) * gain + shift


def _group_scan(a, b):
    rows, width = a.shape
    groups = rows // SUBLANES
    a3 = a.reshape(groups, SUBLANES, width)
    b3 = b.reshape(groups, SUBLANES, width)
    t = lax.broadcasted_iota(jnp.int32, (1, SUBLANES, 1), 1)
    d = 1
    while d < SUBLANES:
        keep = t >= d
        a_s = jnp.where(keep, pltpu.roll(a3, d, axis=1), 1.0)
        b_s = jnp.where(keep, pltpu.roll(b3, d, axis=1), 0.0)
        b3 = a3 * b_s + b3
        a3 = a3 * a_s
        d *= 2
    return a3.reshape(rows, width), b3.reshape(rows, width)


def _chain_groups(a_grp, b_grp, h_in):
    rows = a_grp.shape[0]
    out = []
    carry = h_in
    for g in range(rows // SUBLANES):
        sl = slice(g * SUBLANES, (g + 1) * SUBLANES)
        hg = b_grp[sl] + a_grp[sl] * carry
        out.append(hg)
        carry = hg[SUBLANES - 1:SUBLANES]
    return jnp.concatenate(out, axis=0)


def _rope128(x, cos, sin_signed, first_half):
    sw = jnp.where(first_half, pltpu.roll(x, LANES - HEAD_DIM // 2, axis=1), pltpu.roll(x, HEAD_DIM // 2, axis=1))
    return x * cos + sw * sin_signed


def _softplus(x):
    return jnp.maximum(x, 0.0) + jnp.log1p(jnp.exp(-jnp.abs(x)))


def _lru_coeffs(uc, w, first_pos_mask):
    ub = uc.astype(BF16)
    half = LRU_WIDTH // 2
    ra = jnp.concatenate([jnp.dot(ub[:, :half], w['ga'][0], preferred_element_type=F32),
                          jnp.dot(ub[:, half:], w['ga'][1], preferred_element_type=F32)], axis=1)
    rx = jnp.concatenate([jnp.dot(ub[:, :half], w['gx'][0], preferred_element_type=F32),
                          jnp.dot(ub[:, half:], w['gx'][1], preferred_element_type=F32)], axis=1)
    r = jax.nn.sigmoid(ra + w['lru_ba'][...])
    i = jax.nn.sigmoid(rx + w['lru_bx'][...])
    log_a = -LRU_C * r * _softplus(-w['lam'][...])
    a = jnp.exp(log_a)
    om = 1.0 - a * a
    mult = jnp.where(om > 0.0, om * lax.rsqrt(om), 0.0)
    if first_pos_mask is not None:
        mult = jnp.where(first_pos_mask, 1.0, mult)
    return a, mult * i * uc


def _conv_taps(u, s1, s2, s3, w):
    cw = w['conv_w']
    return w['conv_b'][...] + s3 * cw[0:1, :] + s2 * cw[1:2, :] + s1 * cw[2:3, :] + u * cw[3:4, :]


def _in_proj(x, mod, w):
    sh1, sc1 = mod
    h = _rms_mod(x, w['norm1_g'][...] * (1.0 + sc1), sh1)
    return jnp.dot(h.astype(BF16), w['w_in'][...], preferred_element_type=F32) + w['b_in'][...]


def _post_mix(x, mix, mod, w):
    g1, sh2, sc2 = mod
    x1 = x + g1 * (jnp.dot(mix.astype(BF16), w['w_out'][...], preferred_element_type=F32) + w['b_out'][...])
    h2 = _rms_mod(x1, w['norm2_g'][...] * (1.0 + sc2), sh2)
    h2_hi = h2.astype(BF16)
    h2_lo = (h2 - h2_hi.astype(F32)).astype(BF16)
    logits = (jnp.dot(h2_hi, w['wr_hi'][...], preferred_element_type=F32)
              + jnp.dot(h2_lo, w['wr_hi'][...], preferred_element_type=F32)
              + jnp.dot(h2_hi, w['wr_lo'][...], preferred_element_type=F32)) + w['b_router'][...]
    return x1, h2_hi, logits


ROUTE_E, ROUTE_G, ROUTE_R = 0, TOP_K, 2 * TOP_K


def _lane_roll1(v, shift):
    return pltpu.roll(jnp.broadcast_to(v, (SUBLANES, LANES)), shift, axis=1)[0:1]


def _route_tile(lg):
    rows = lg.shape[0]
    lane = lax.broadcasted_iota(jnp.int32, (1, LANES), 1)
    e_of = lane % N_EXPERTS
    grp = lane // N_EXPERTS
    e_id = e_of.astype(F32)
    onehot = jnp.zeros((rows, LANES), F32)
    vals, ids = [], []
    for k in range(TOP_K):
        m = jnp.max(lg, axis=1, keepdims=True)
        idx = jnp.min(jnp.where(lg == m, e_id, float(N_EXPERTS)), axis=1, keepdims=True)
        sel = e_id == idx
        lg = jnp.where(sel, -jnp.inf, lg)
        onehot = jnp.where(jnp.logical_and(sel, grp == k), 1.0, onehot)
        vals.append(m)
        ids.append(idx)
    ex = [jnp.exp(v - vals[0]) for v in vals]
    denom = ex[0] + ex[1] + ex[2] + ex[3]

    r_i = lax.broadcasted_iota(jnp.int32, (rows, rows), 0)
    c_i = lax.broadcasted_iota(jnp.int32, (rows, rows), 1)
    strict_lower = jnp.where(r_i > c_i, 1.0, 0.0).astype(BF16)
    prefix = jnp.dot(strict_lower, onehot.astype(BF16), preferred_element_type=F32)
    cnt = jnp.sum(onehot, axis=0, keepdims=True)
    base = jnp.zeros((1, LANES), F32)
    tot = cnt
    for s in range(1, TOP_K):
        rolled = _lane_roll1(cnt, s * N_EXPERTS)
        base = base + jnp.where(lane >= s * N_EXPERTS, rolled, 0.0)
        tot = tot + rolled
    pad_cnt = jnp.floor((tot + (SUBLANES - 1.0)) * (1.0 / SUBLANES)) * SUBLANES
    inc = pad_cnt
    d = 1
    while d < N_EXPERTS:
        inc = inc + jnp.where(e_of >= d, _lane_roll1(inc, d), 0.0)
        d *= 2
    strip_start = inc - pad_cnt
    ranked = onehot * (prefix + base + strip_start)

    route = jnp.zeros((rows, LANES), F32)
    for k in range(TOP_K):
        rank_k = jnp.sum(jnp.where(grp == k, ranked, 0.0), axis=1, keepdims=True)
        route = jnp.where(lane == ROUTE_E + k, ids[k], route)
        route = jnp.where(lane == ROUTE_G + k, ex[k] / denom, route)
        route = jnp.where(lane == ROUTE_R + k, rank_k, route)
    return route, tot


def _softmax_with_sink(s, sink_col):
    m = jnp.maximum(jnp.max(s, axis=-1, keepdims=True), sink_col)
    p = jnp.exp(s - m)
    denom = jnp.sum(p, axis=-1, keepdims=True) + jnp.exp(sink_col - m)
    return p, denom


WEIGHT_NAMES = ('norm1_g', 'w_in', 'b_in', 'conv_w', 'conv_b', 'ga', 'gx', 'lru_ba', 'lru_bx', 'lam',
                'w_out', 'b_out', 'norm2_g', 'wr_hi', 'wr_lo', 'b_router')


def _split_bf16(x):
    hi = x.astype(BF16)
    return hi, (x - hi.astype(F32)).astype(BF16)


def _ada_kernel(n_prompt_rows, c_ref, w_ref, b_ref, op_ref, os_ref):
    c = c_ref[...]
    s_hi, s_lo = _split_bf16(c * jax.nn.sigmoid(c))
    w_hi, w_lo = _split_bf16(w_ref[...])
    mod = (jnp.dot(s_hi, w_hi, preferred_element_type=F32) + jnp.dot(s_lo, w_hi, preferred_element_type=F32)
           + jnp.dot(s_hi, w_lo, preferred_element_type=F32)) + b_ref[...]
    op_ref[...] = mod[:n_prompt_rows]
    os_ref[...] = mod[n_prompt_rows:]


def _ada(c_all, n_prompt_rows, w_ada, b_ada):
    rows = c_all.shape[0]
    n_s = rows - n_prompt_rows
    assert n_prompt_rows % SUBLANES == 0
    return pl.pallas_call(
        functools.partial(_ada_kernel, n_prompt_rows),
        grid=(6,),
        in_specs=[pl.BlockSpec((rows, D_MODEL), lambda i: (0, 0)),
                  pl.BlockSpec((D_MODEL, D_MODEL), lambda i: (0, i)),
                  pl.BlockSpec((1, D_MODEL), lambda i: (0, i))],
        out_specs=(pl.BlockSpec((n_prompt_rows, D_MODEL), lambda i: (0, i)),
                   pl.BlockSpec((None, n_s, D_MODEL), lambda i: (i, 0, 0))),
        out_shape=(jax.ShapeDtypeStruct((n_prompt_rows, 6 * D_MODEL), F32),
                   jax.ShapeDtypeStruct((6, n_s, D_MODEL), F32)),
        compiler_params=pltpu.CompilerParams(dimension_semantics=("arbitrary",), vmem_limit_bytes=VMEM_LIMIT),
        name="ada",
    )(c_all, w_ada, b_ada)


def _prompt_body(seq_start, x_ref, mod_ref, cos_ref, sin_ref, sinks_ref, w, x1_ref, h2_ref, lg_ref,
                 hlast_ref, ulast_ref, klast_ref, vlast_ref, conv_c, h_c, k_c, v_c):
    ts = SEQ_TILE

    if seq_start is not None:
        @pl.when(seq_start)
        def _():
            conv_c[...] = jnp.zeros_like(conv_c)
            h_c[...] = jnp.zeros_like(h_c)
            k_c[...] = jnp.zeros_like(k_c)
            v_c[...] = jnp.zeros_like(v_c)

    x = x_ref[...]
    mod = mod_ref[...]
    proj = _in_proj(x, (mod[0:1], mod[1:2]), w)
    u = proj[:, :LRU_WIDTH]
    gate = proj[:, LRU_WIDTH:2 * LRU_WIDTH]
    o2 = 2 * LRU_WIDTH

    rowid = lax.broadcasted_iota(jnp.int32, (ts, 1), 0)
    u_ext = jnp.concatenate([conv_c[...], u], axis=0)
    s1, s2, s3 = (pltpu.roll(u_ext, d, axis=0)[SUBLANES:] for d in (1, 2, 3))
    uc = _conv_taps(u, s1, s2, s3, w)
    conv_c[...] = u[ts - SUBLANES:]
    ulast_ref[...] = u[ts - SUBLANES:]

    first_pos = None if seq_start is None else jnp.logical_and(rowid == 0, seq_start)
    a, bt = _lru_coeffs(uc, w, first_pos)
    hs = _chain_groups(*_group_scan(a, bt), h_c[0:1, :])
    h_tail = hs[ts - SUBLANES:]
    h_c[...] = jnp.broadcast_to(h_tail[SUBLANES - 1:SUBLANES, :], h_c.shape)
    hlast_ref[...] = h_tail
    lru_out = hs * jax.nn.gelu(gate)

    cos = cos_ref[...]
    sin = sin_ref[...]
    lane = lax.broadcasted_iota(jnp.int32, (1, LANES), 1)
    first_half = (lane % HEAD_DIM) < (HEAD_DIM // 2)
    qcols = [_rope128(proj[:, o2 + c * LANES:o2 + (c + 1) * LANES], cos, sin, first_half) * (HEAD_DIM ** -0.5)
             for c in range(4)]
    k = _rope128(proj[:, o2 + Q_WIDTH:o2 + Q_WIDTH + KV_WIDTH], cos, sin, first_half)
    v = proj[:, o2 + Q_WIDTH + KV_WIDTH:]
    k_ext = jnp.concatenate([k_c[...], k], axis=0).astype(BF16)
    v_ext = jnp.concatenate([v_c[...], v], axis=0).astype(BF16)
    k_c[...] = k[ts - WINDOW:]
    v_c[...] = v[ts - WINDOW:]
    klast_ref[...] = k[ts - WINDOW:]
    vlast_ref[...] = v[ts - WINDOW:]

    grow = lax.broadcasted_iota(jnp.int32, (GROUP * WINDOW, 1), 0)
    from_prev = lax.broadcasted_iota(jnp.int32, (GROUP * WINDOW, WINDOW), 1) > grow % WINDOW
    grow = grow // WINDOW
    lane_lo = lane < HEAD_DIM
    attn_cols = [[] for _ in range(4)]
    for blk in range(ts // WINDOW):
        kb = k_ext[blk * WINDOW:(blk + 2) * WINDOW]
        v_prev = v_ext[blk * WINDOW:(blk + 1) * WINDOW]
        v_own = v_ext[(blk + 1) * WINDOW:(blk + 2) * WINDOW]
        outs = []
        for kv in range(N_KV_HEADS):
            sel = lane_lo if kv == 0 else jnp.logical_not(lane_lo)
            qs = jnp.concatenate(
                [jnp.where(sel, qc[blk * WINDOW:(blk + 1) * WINDOW], 0.0) for qc in qcols], axis=0).astype(BF16)
            s = lax.dot_general(qs, kb, (((1,), (1,)), ((), ())), preferred_element_type=F32)
            s_prev = s[:, :WINDOW]
            if seq_start is not None and blk == 0:
                s_prev = jnp.where(seq_start, NEG_BIG, s_prev)
            s = jnp.where(from_prev, s_prev, s[:, WINDOW:])
            sink_col = jnp.zeros((GROUP * WINDOW, 1), F32)
            for g in range(GROUP):
                sink_col = jnp.where(grow == g, sinks_ref[kv * GROUP + g], sink_col)
            p, denom = _softmax_with_sink(s, sink_col)
            pv = (jnp.dot(jnp.where(from_prev, p, 0.0).astype(BF16), v_prev, preferred_element_type=F32)
                  + jnp.dot(jnp.where(from_prev, 0.0, p).astype(BF16), v_own, preferred_element_type=F32))
            outs.append(pv / denom)
        for c in range(4):
            attn_cols[c].append(jnp.where(lane_lo, outs[0][c * WINDOW:(c + 1) * WINDOW],
                                          outs[1][c * WINDOW:(c + 1) * WINDOW]))
    attn = jnp.concatenate([jnp.concatenate(cols, axis=0) for cols in attn_cols], axis=1)

    mix = jnp.concatenate([lru_out, attn], axis=1)
    x1, h2, logits = _post_mix(x, mix, (mod[2:3], mod[3:4], mod[4:5]), w)
    x1_ref[...] = x1
    h2_ref[...] = h2
    route, tot = _route_tile(logits)
    lg_ref[0][...] = route
    lg_ref[1][...] = jnp.broadcast_to(tot, lg_ref[1].shape)


def _expand_rows(m, t):
    b, wd = m.shape
    return jnp.broadcast_to(m[:, None, :], (b, t, wd)).reshape(b * t, wd)


def _sample_body(x_ref, mod_ref, cos_ref, sin_ref, sinks_ref, h0_ref, cprev_ref, ck_ref, cv_ref, w,
                 x1_ref, h2_ref, lg_ref, g2_ref, hs_ref, u_ref, ko_ref, vo_ref):
    bt_, t = SAMPLE_BT, SUBLANES
    rows = bt_ * t

    x = x_ref[...]
    mods = [_expand_rows(mod_ref[i], t) for i in range(6)]
    proj = _in_proj(x, (mods[0], mods[1]), w)
    u = proj[:, :LRU_WIDTH]
    gate = proj[:, LRU_WIDTH:2 * LRU_WIDTH]
    o2 = 2 * LRU_WIDTH
    u_ref[...] = u

    rowid = lax.broadcasted_iota(jnp.int32, (rows, 1), 0) % t
    cprev = cprev_ref[...]
    taps = []
    for d in (1, 2, 3):
        taps.append(jnp.where(rowid >= d, pltpu.roll(u, d, axis=0),
                              pltpu.roll(cprev, (d - (CONV_W - 1)) % rows, axis=0)))
    uc = _conv_taps(u, taps[0], taps[1], taps[2], w)

    a, bt = _lru_coeffs(uc, w, None)
    bt = bt + a * h0_ref[...]
    _, hs = _group_scan(a, bt)
    hs_ref[...] = hs
    lru_out = hs * jax.nn.gelu(gate)

    cos = cos_ref[...]
    sin = sin_ref[...]
    lane = lax.broadcasted_iota(jnp.int32, (1, LANES), 1)
    first_half = (lane % HEAD_DIM) < (HEAD_DIM // 2)
    qcols = [_rope128(proj[:, o2 + c * LANES:o2 + (c + 1) * LANES], cos, sin, first_half) * (HEAD_DIM ** -0.5)
             for c in range(4)]
    k = _rope128(proj[:, o2 + Q_WIDTH:o2 + Q_WIDTH + KV_WIDTH], cos, sin, first_half)
    v = proj[:, o2 + Q_WIDTH + KV_WIDTH:]
    k3 = k.reshape(bt_, t, KV_WIDTH)
    v3 = v.reshape(bt_, t, KV_WIDTH)
    ck = ck_ref[...]
    cv = cv_ref[...]
    ko_ref[:, :WINDOW - t, :] = ck[:, t:, :]
    ko_ref[:, WINDOW - t:, :] = k3
    vo_ref[:, :WINDOW - t, :] = cv[:, t:, :]
    vo_ref[:, WINDOW - t:, :] = v3

    ckb, cvb, k3b, v3b = ck.astype(BF16), cv.astype(BF16), k3.astype(BF16), v3.astype(BF16)
    lane_lo = lane < HEAD_DIM
    gq = GROUP * t
    tq = lax.broadcasted_iota(jnp.int32, (1, gq, 1), 1) % t
    mask_c = lax.broadcasted_iota(jnp.int32, (1, gq, WINDOW), 2) > tq
    mask_n = lax.broadcasted_iota(jnp.int32, (1, gq, t), 2) <= tq
    grow = lax.broadcasted_iota(jnp.int32, (1, gq, 1), 1) // t
    bdims = (((2,), (2,)), ((0,), (0,)))
    pdims = (((2,), (1,)), ((0,), (0,)))
    outs = []
    for kv in range(N_KV_HEADS):
        sel = lane_lo if kv == 0 else jnp.logical_not(lane_lo)
        q3 = jnp.concatenate([jnp.where(sel, qc, 0.0).reshape(bt_, t, LANES) for qc in qcols], axis=1).astype(BF16)
        sc = lax.dot_general(q3, ckb, bdims, preferred_element_type=F32)
        sn = lax.dot_general(q3, k3b, bdims, preferred_element_type=F32)
        sc = jnp.where(mask_c, sc, NEG_BIG)
        sn = jnp.where(mask_n, sn, NEG_BIG)
        sink_col = jnp.zeros((1, gq, 1), F32)
        for g in range(GROUP):
            sink_col = jnp.where(grow == g, sinks_ref[kv * GROUP + g], sink_col)
        m = jnp.maximum(jnp.maximum(jnp.max(sc, axis=-1, keepdims=True), jnp.max(sn, axis=-1, keepdims=True)),
                        sink_col)
        pc = jnp.exp(sc - m)
        pn = jnp.exp(sn - m)
        denom = jnp.sum(pc, axis=-1, keepdims=True) + jnp.sum(pn, axis=-1, keepdims=True) + jnp.exp(sink_col - m)
        o = (lax.dot_general(pc.astype(BF16), cvb, pdims, preferred_element_type=F32)
             + lax.dot_general(pn.astype(BF16), v3b, pdims, preferred_element_type=F32)) / denom
        outs.append(o)
    attn = jnp.concatenate(
        [jnp.where(lane_lo, outs[0][:, c * t:(c + 1) * t, :], outs[1][:, c * t:(c + 1) * t, :]).reshape(rows, LANES)
         for c in range(4)], axis=1)

    mix = jnp.concatenate([lru_out, attn], axis=1)
    x1, h2, logits = _post_mix(x, mix, (mods[2], mods[3], mods[4]), w)
    x1_ref[...] = x1
    h2_ref[...] = h2
    route, tot = _route_tile(logits)
    lg_ref[0][...] = route
    lg_ref[1][...] = jnp.broadcast_to(tot, lg_ref[1].shape)
    g2_ref[...] = mods[5]


def _prompt_kernel(steps_per_seq, x_ref, mod_ref, cos_ref, sin_ref, sinks_ref, *rest):
    nw = len(WEIGHT_NAMES)
    w = dict(zip(WEIGHT_NAMES, rest[:nw]))
    (x1_ref, h2_ref, route_ref, cnt_ref, hlast_ref, ulast_ref, klast_ref, vlast_ref,
     conv_c, h_c, k_c, v_c) = rest[nw:]
    seq_start = pl.program_id(0) % steps_per_seq == 0
    for sub in range(MIX_TILES):
        rows = pl.ds(sub * SEQ_TILE, SEQ_TILE)
        _prompt_body(seq_start if sub == 0 else None, x_ref.at[rows], mod_ref, cos_ref.at[rows], sin_ref.at[rows],
                     sinks_ref, w, x1_ref.at[rows], h2_ref.at[rows], (route_ref.at[rows], cnt_ref.at[sub]),
                     hlast_ref, ulast_ref, klast_ref, vlast_ref, conv_c, h_c, k_c, v_c)


def _sample_kernel(x_ref, mod_ref, cos_ref, sin_ref, sinks_ref, h0_ref, cprev_ref, ck_ref, cv_ref, *rest):
    nw = len(WEIGHT_NAMES)
    w = dict(zip(WEIGHT_NAMES, rest[:nw]))
    x1_ref, h2_ref, route_ref, cnt_ref, g2_ref, hs_ref, u_ref, ko_ref, vo_ref = rest[nw:]
    _sample_body(x_ref, mod_ref, cos_ref, sin_ref, sinks_ref, h0_ref, cprev_ref, ck_ref, cv_ref, w,
                 x1_ref, h2_ref, (route_ref, cnt_ref), g2_ref, hs_ref, u_ref, ko_ref, vo_ref)


PACK_W = D_MODEL // 2
PACKED = jnp.int32


def _pack_bf16_pairs(x):
    return pltpu.pack_elementwise([x[:, :PACK_W], x[:, PACK_W:]], packed_dtype=BF16)


def _unpack_bf16_pairs(w):
    return tuple(pltpu.unpack_elementwise(w, index=k, packed_dtype=BF16, unpacked_dtype=F32).astype(BF16)
                 for k in range(2))


STRIP_SIZES = tuple(SUBLANES << b for b in range(6))
STRIP_LARGE = 64
SORT_ROWS = SEQ_TILE * TOP_K + N_EXPERTS * SUBLANES
TILE_WAIT_SIZES = tuple(SUBLANES << b for b in range(8))


def _start_strip(copy, rows):
    copy.start(priority=STRIP_SIZES.index(rows) % 2)


def _for_strips(cnt_ref, off_ref, tile, buf_slot, hbm, sem, to_hbm, act):
    def e_body(e, local):
        n = cnt_ref[tile * N_EXPERTS + e]
        glob = off_ref[tile * N_EXPERTS + e]

        def pieces(sizes, done):
            for p in sizes:
                piece = n & p
                lo = pl.ds(pl.multiple_of(local + done, SUBLANES), p)
                gl = pl.ds(pl.multiple_of(glob + done, SUBLANES), p)

                @pl.when(piece != 0)
                def _():
                    if to_hbm:
                        act(pltpu.make_async_copy(buf_slot.at[lo], hbm.at[gl], sem), p)
                    else:
                        act(pltpu.make_async_copy(hbm.at[gl], buf_slot.at[lo], sem), p)
                done = done + piece

        large = tuple(p for p in reversed(STRIP_SIZES) if p >= STRIP_LARGE)
        small = tuple(p for p in reversed(STRIP_SIZES) if p < STRIP_LARGE)
        n_large = n & (-STRIP_LARGE)

        @pl.when(n_large != 0)
        def _():
            pieces(large, 0)
        pieces(small, n_large)
        return local + n
    lax.fori_loop(0, N_EXPERTS, e_body, 0)


def _wait_tile_rows(total, buf_slot, hbm, sem, to_hbm):
    for p in TILE_WAIT_SIZES:
        @pl.when((total & p) != 0)
        def _():
            if to_hbm:
                pltpu.make_async_copy(buf_slot.at[pl.ds(0, p)], hbm.at[pl.ds(0, p)], sem).wait()
            else:
                pltpu.make_async_copy(hbm.at[pl.ds(0, p)], buf_slot.at[pl.ds(0, p)], sem).wait()


def _dispatch_kernel(n_prompt_tiles, cnt_ref, off_ref, tot_ref, meta_ref, h2p_ref, h2s_ref, routep_ref, routes_ref,
                     xs_hbm, sbuf, zblk, sem, zsem):
    i = pl.program_id(0)
    nb = pl.num_programs(0)
    slot = i % 2
    n_blocks = xs_hbm.shape[0] // MOE_TM

    is_prompt = i < n_prompt_tiles
    h2 = jnp.where(is_prompt, h2p_ref[...], h2s_ref[...])
    route_t = jnp.where(is_prompt, routep_ref[...], routes_ref[...]).T
    r_pos = lax.broadcasted_iota(jnp.int16, (SORT_ROWS, SEQ_TILE), 0)
    perm = jnp.zeros((SORT_ROWS, SEQ_TILE), BF16)
    for k in range(TOP_K):
        pos_k = route_t[ROUTE_R + k:ROUTE_R + k + 1, :].astype(jnp.int32).astype(jnp.int16)
        perm = jnp.where(r_pos == pos_k, jnp.ones((), BF16), perm)
    sbuf[slot] = _pack_bf16_pairs(jnp.dot(perm, h2, preferred_element_type=F32))

    _for_strips(cnt_ref, off_ref, i, sbuf.at[slot], xs_hbm, sem.at[slot], True, _start_strip)

    @pl.when(i > 0)
    def _():
        _wait_tile_rows(tot_ref[jnp.maximum(i - 1, 0)], sbuf.at[1 - slot], xs_hbm, sem.at[1 - slot], True)

    @pl.when(i == nb - 1)
    def _():
        _wait_tile_rows(tot_ref[i], sbuf.at[slot], xs_hbm, sem.at[slot], True)
        zblk[...] = jnp.zeros_like(zblk)

        def for_region_tails(act):
            def e_body(e, carry):
                start = meta_ref[e]
                n = meta_ref[N_EXPERTS + e] - start
                done = 0
                for p in reversed([q for q in STRIP_SIZES if q < MOE_TM]):
                    piece = n & p
                    rows = pl.ds(pl.multiple_of(start + done, SUBLANES), p)

                    @pl.when(piece != 0)
                    def _():
                        act(pltpu.make_async_copy(zblk.at[pl.ds(0, p)], xs_hbm.at[rows], zsem.at[0]))
                    done = done + piece
                return carry
            lax.fori_loop(0, N_EXPERTS, e_body, 0)

        def for_tail_blocks(act):
            def b_body(j, carry):
                act(pltpu.make_async_copy(zblk, xs_hbm.at[pl.ds(pl.multiple_of(j * MOE_TM, MOE_TM), MOE_TM)],
                                          zsem.at[0]))
                return carry
            lax.fori_loop(meta_ref[2 * N_EXPERTS], n_blocks, b_body, 0)

        for_region_tails(lambda cp: cp.start())
        for_tail_blocks(lambda cp: cp.start())
        for_region_tails(lambda cp: cp.wait())
        for_tail_blocks(lambda cp: cp.wait())


def _dispatch(cnt8, tile_off, tot8, meta, h2_p, h2_s, route_p, route_s, n_rows):
    npt = h2_p.shape[0] // SEQ_TILE
    p_tile = lambda i, *_: (jnp.minimum(i, npt - 1), 0)
    s_tile = lambda i, *_: (jnp.maximum(i - npt, 0), 0)
    nt = tot8.shape[0]
    tt = SEQ_TILE
    grid_spec = pltpu.PrefetchScalarGridSpec(
        num_scalar_prefetch=4,
        grid=(nt,),
        in_specs=[pl.BlockSpec((tt, D_MODEL), p_tile), pl.BlockSpec((tt, D_MODEL), s_tile),
                  pl.BlockSpec((tt, LANES), p_tile), pl.BlockSpec((tt, LANES), s_tile)],
        out_specs=pl.BlockSpec(memory_space=pl.ANY),
        scratch_shapes=[pltpu.VMEM((2, SORT_ROWS, PACK_W), PACKED), pltpu.VMEM((MOE_TM, PACK_W), PACKED),
                        pltpu.SemaphoreType.DMA((2,)), pltpu.SemaphoreType.DMA((1,))],
    )
    return pl.pallas_call(
        functools.partial(_dispatch_kernel, npt),
        grid_spec=grid_spec,
        out_shape=jax.ShapeDtypeStruct((n_rows, PACK_W), PACKED),
        compiler_params=pltpu.CompilerParams(dimension_semantics=("arbitrary",), vmem_limit_bytes=VMEM_LIMIT),
        name="dispatch",
    )(cnt8, tile_off, tot8, meta, h2_p, h2_s, route_p, route_s)


def _expert_mlp(words, w1b, b1_ref, w2b, b2_ref, act_ref):
    xb = jnp.concatenate(_unpack_bf16_pairs(words), axis=1)
    q = D_FF // 4
    for c in range(4):
        zg = jnp.dot(xb, w1b[:, c * q:(c + 1) * q], preferred_element_type=F32) + b1_ref[:, c * q:(c + 1) * q]
        zl = (jnp.dot(xb, w1b[:, D_FF + c * q:D_FF + (c + 1) * q], preferred_element_type=F32)
              + b1_ref[:, D_FF + c * q:D_FF + (c + 1) * q])
        glu = jnp.minimum(zg, SWIGLU_LIMIT)
        lin = jnp.clip(zl, -SWIGLU_LIMIT, SWIGLU_LIMIT)
        act_ref[:, c * q:(c + 1) * q] = (glu * jax.nn.sigmoid(SWIGLU_ALPHA * glu) * (lin + 1.0)).astype(BF16)
    return _pack_bf16_pairs(jnp.dot(act_ref[...], w2b[...], preferred_element_type=F32) + b2_ref[...])


def _moe_kernel(row_ref, size_ref, exp_ref, wt_ref, meta_ref, xs_hbm, w1_hbm, b1_ref, w2_hbm, b2_ref, ys_hbm,
                xin, yout, w1f, w2f, w1b, w2b, act, isem, osem, wsem):
    tm = MOE_TM
    n_blocks = ys_hbm.shape[0] // tm
    n_tbl = row_ref.shape[0]
    n = meta_ref[2 * N_EXPERTS + 1]

    def for_chunk(j, fn):
        r = pl.multiple_of(row_ref[j], tm)
        for code in MOE_CHUNK_CODES:
            @pl.when(size_ref[j] == code)
            def _():
                fn(r, code * tm)

    def in_copy(r, rows, s):
        return pltpu.make_async_copy(xs_hbm.at[pl.ds(r, rows)], xin.at[s, pl.ds(0, rows)], isem.at[s])

    def out_copy(r, rows, s):
        return pltpu.make_async_copy(yout.at[s, pl.ds(0, rows)], ys_hbm.at[pl.ds(r, rows)], osem.at[s])

    def start_in(j, s):
        for_chunk(j, lambda r, rows: in_copy(r, rows, s).start())

    def wait_in(j, s):
        for_chunk(j, lambda r, rows: in_copy(r, rows, s).wait())

    def start_out(j, s):
        for_chunk(j, lambda r, rows: out_copy(r, rows, s).start())

    def wait_out(j, s):
        for_chunk(j, lambda r, rows: out_copy(r, rows, s).wait())

    def weight_copies(e, ws):
        return (pltpu.make_async_copy(w1_hbm.at[e], w1f.at[ws], wsem.at[ws]),
                pltpu.make_async_copy(w2_hbm.at[e], w2f.at[ws], wsem.at[ws]))

    start_in(0, 0)
    for cp in weight_copies(exp_ref[0], 0):
        cp.start()

    def trip(i, carry):
        slot = i % 2

        @pl.when(i + 1 < n)
        def _():
            start_in(i + 1, 1 - slot)

        @pl.when(i >= 2)
        def _():
            wait_out(i - 2, slot)

        @pl.when(wt_ref[i] == 1)
        def _():
            ws = wt_ref[n_tbl + i]
            nxt = wt_ref[2 * n_tbl + i]
            for cp in weight_copies(exp_ref[i], ws):
                cp.wait()

            @pl.when(nxt >= 0)
            def _():
                for cp in weight_copies(nxt, 1 - ws):
                    cp.start()

            chunk = 128
            def cast_body(c, carry2):
                k0 = pl.multiple_of(c * chunk, chunk)
                w1b[pl.ds(k0, chunk), :] = w1f[ws, pl.ds(k0, chunk), :].astype(BF16)
                w2b[pl.ds(k0, chunk), :] = w2f[ws, pl.ds(k0, chunk), :].astype(BF16)
                return carry2
            lax.fori_loop(0, D_MODEL // chunk, cast_body, 0)

        wait_in(i, slot)

        for code in MOE_CHUNK_CODES:
            rows = code * tm

            @pl.when(size_ref[i] == code)
            def _():
                yout[slot, 0:rows] = _expert_mlp(xin[slot, 0:rows], w1b, b1_ref.at[exp_ref[i]], w2b,
                                                 b2_ref.at[exp_ref[i]], act.at[pl.ds(0, rows)])

        start_out(i, slot)
        return carry
    lax.fori_loop(0, n, trip, 0)

    @pl.when(n >= 2)
    def _():
        wait_out(n - 2, n % 2)
    wait_out(n - 1, (n - 1) % 2)

    yout[1, 0:tm] = jnp.zeros((tm, PACK_W), PACKED)

    def zero_block(j):
        return pltpu.make_async_copy(yout.at[1, pl.ds(0, tm)], ys_hbm.at[pl.ds(pl.multiple_of(j * tm, tm), tm)],
                                     osem.at[1])

    def start_body(j, carry):
        zero_block(j).start()
        return carry

    def wait_body(j, carry):
        zero_block(j).wait()
        return carry
    lax.fori_loop(meta_ref[2 * N_EXPERTS], n_blocks, start_body, 0)
    lax.fori_loop(meta_ref[2 * N_EXPERTS], n_blocks, wait_body, 0)


def _moe(chunk_row, chunk_size, chunk_exp, weight_tbl, meta, xs, w1, b1, w2, b2):
    whole = lambda i, *_: (0, 0, 0)
    grid_spec = pltpu.PrefetchScalarGridSpec(
        num_scalar_prefetch=5,
        grid=(1,),
        in_specs=[
            pl.BlockSpec(memory_space=pl.ANY),
            pl.BlockSpec(memory_space=pl.ANY),
            pl.BlockSpec((N_EXPERTS, 1, 2 * D_FF), whole),
            pl.BlockSpec(memory_space=pl.ANY),
            pl.BlockSpec((N_EXPERTS, 1, D_MODEL), whole),
        ],
        out_specs=pl.BlockSpec(memory_space=pl.ANY),
        scratch_shapes=[pltpu.VMEM((2, MOE_CH, PACK_W), PACKED), pltpu.VMEM((2, MOE_CH, PACK_W), PACKED),
                        pltpu.VMEM((2, D_MODEL, 2 * D_FF), F32), pltpu.VMEM((2, D_FF, D_MODEL), F32),
                        pltpu.VMEM((D_MODEL, 2 * D_FF), BF16), pltpu.VMEM((D_FF, D_MODEL), BF16),
                        pltpu.VMEM((MOE_CH, D_FF), BF16),
                        pltpu.SemaphoreType.DMA((2,)), pltpu.SemaphoreType.DMA((2,)), pltpu.SemaphoreType.DMA((2,))],
    )
    return pl.pallas_call(
        _moe_kernel,
        grid_spec=grid_spec,
        out_shape=jax.ShapeDtypeStruct(xs.shape, PACKED),
        compiler_params=pltpu.CompilerParams(dimension_semantics=("arbitrary",), vmem_limit_bytes=VMEM_LIMIT),
        name="moe",
    )(chunk_row, chunk_size, chunk_exp, weight_tbl, meta, xs, w1, b1, w2, b2)


def _combine_kernel(n_prompt_tiles, cnt_ref, off_ref, tot_ref, ys_hbm, routep_ref, routes_ref, x1p_ref, x1s_ref,
                    modp_ref, g2s_ref, fg_ref, op_ref, os_ref, buf, sem):
    i = pl.program_id(0)
    nb = pl.num_programs(0)
    slot = i % 2

    def fetch(tile, s):
        _for_strips(cnt_ref, off_ref, tile, buf.at[s], ys_hbm, sem.at[s], False, _start_strip)

    @pl.when(i == 0)
    def _():
        buf[...] = jnp.zeros_like(buf)
        fetch(0, 0)

    @pl.when(i + 1 < nb)
    def _():
        fetch(i + 1, 1 - slot)

    _wait_tile_rows(tot_ref[i], buf.at[slot], ys_hbm, sem.at[slot], False)

    is_prompt = i < n_prompt_tiles
    route = jnp.where(is_prompt, routep_ref[...], routes_ref[...])
    c_pos = lax.broadcasted_iota(jnp.int16, (SEQ_TILE, SORT_ROWS), 1)
    g_bf = jnp.zeros((SEQ_TILE, SORT_ROWS), BF16)
    for k in range(TOP_K):
        pos_k = route[:, ROUTE_R + k:ROUTE_R + k + 1].astype(jnp.int32).astype(jnp.int16)
        g_bf = jnp.where(c_pos == pos_k, route[:, ROUTE_G + k:ROUTE_G + k + 1].astype(BF16), g_bf)
    ff = jnp.concatenate([jnp.dot(g_bf, yb, preferred_element_type=F32) for yb in _unpack_bf16_pairs(buf[slot])],
                         axis=1)
    g2 = jnp.where(i < n_prompt_tiles, modp_ref[5:6, :], g2s_ref[...])
    x = jnp.where(is_prompt, x1p_ref[...], x1s_ref[...]) + g2 * ff
    y = _rms(x, fg_ref[...])

    @pl.when(i < n_prompt_tiles)
    def _():
        op_ref[...] = y

    @pl.when(i >= n_prompt_tiles)
    def _():
        os_ref[...] = y


def _combine(cnt8, tile_off, tot8, ys, route_p, route_s, x1_p, x1_s, mod_p, g2_rows, final_g, tiles_per_batch):
    nt = tot8.shape[0]
    tt = SEQ_TILE
    npt = x1_p.shape[0] // tt
    p_tile = lambda i, *_: (jnp.minimum(i, npt - 1), 0)
    s_tile = lambda i, *_: (jnp.maximum(i - npt, 0), 0)
    grid_spec = pltpu.PrefetchScalarGridSpec(
        num_scalar_prefetch=3,
        grid=(nt,),
        in_specs=[
            pl.BlockSpec(memory_space=pl.ANY),
            pl.BlockSpec((tt, LANES), p_tile), pl.BlockSpec((tt, LANES), s_tile),
            pl.BlockSpec((tt, D_MODEL), p_tile), pl.BlockSpec((tt, D_MODEL), s_tile),
            pl.BlockSpec((None, 6, D_MODEL), lambda i, *_: (jnp.minimum(i, npt - 1) // tiles_per_batch, 0, 0)),
            pl.BlockSpec((tt, D_MODEL), s_tile),
            pl.BlockSpec((1, D_MODEL), lambda i, *_: (0, 0)),
        ],
        out_specs=(pl.BlockSpec((tt, D_MODEL), p_tile), pl.BlockSpec((tt, D_MODEL), s_tile)),
        scratch_shapes=[pltpu.VMEM((2, SORT_ROWS, PACK_W), PACKED), pltpu.SemaphoreType.DMA((2,))],
    )
    return pl.pallas_call(
        functools.partial(_combine_kernel, npt),
        grid_spec=grid_spec,
        out_shape=(jax.ShapeDtypeStruct((npt * tt, D_MODEL), F32),
                   jax.ShapeDtypeStruct(((nt - npt) * tt, D_MODEL), F32)),
        compiler_params=pltpu.CompilerParams(dimension_semantics=("arbitrary",), vmem_limit_bytes=VMEM_LIMIT),
        name="combine",
    )(cnt8, tile_off, tot8, ys, route_p, route_s, x1_p, x1_s, mod_p, g2_rows, final_g)


def _block_diag_halves(wg):
    per_half = LRU_BLOCKS // 2
    w4 = wg.reshape(2, per_half, LRU_BLOCK_W, LRU_BLOCK_W)
    on_diag = jnp.eye(per_half, dtype=bool)[None, :, None, :, None]
    dense = jnp.where(on_diag, w4[:, :, :, None, :], 0.0)
    return dense.reshape(2, per_half * LRU_BLOCK_W, per_half * LRU_BLOCK_W).astype(BF16)


def _rope_tables(pos):
    half = HEAD_DIM // 2
    inv = ROPE_THETA ** (-jnp.arange(half, dtype=F32) / half)
    ang = pos.astype(F32)[:, None] * inv[None, :]
    cos = jnp.cos(ang)
    sin = jnp.sin(ang)
    cos128 = jnp.concatenate([cos, cos, cos, cos], axis=1)
    sin128 = jnp.concatenate([-sin, sin, -sin, sin], axis=1)
    return cos128, sin128


def _resident_spec(arr):
    zeros = (0,) * arr.ndim
    return pl.BlockSpec(arr.shape, lambda i: zeros)


def kernel(x_prompt, x_sample, state_lru_h, state_conv, cache_win_k, cache_win_v, c_prompt, c_sample, w_ada, b_ada, norm1_g, w_in, b_in, conv_w, conv_b, lru_wa, lru_ba, lru_wx, lru_bx, lru_lambda, attn_sinks, w_out, b_out, norm2_g, w_router, b_router, w1, b1, w2, b2, final_g):
    bp, seq, _ = x_prompt.shape
    bd, tdec, _ = x_sample.shape
    assert tdec == SUBLANES and seq % (MIX_TILES * SEQ_TILE) == 0 and bd % SAMPLE_BT == 0
    assert SAMPLE_BT * tdec == SEQ_TILE
    n_prompt = bp * seq
    n_sample = bd * tdec
    n_tok = n_prompt + n_sample
    l = 0

    head_perm = [h for c in range(4) for h in (c, GROUP + c)]
    o2 = 2 * LRU_WIDTH

    def permute_heads(arr, axis, start):
        take = lambda a, b: lax.slice_in_dim(arr, a, b, axis=axis)
        heads = [take(start + h * HEAD_DIM, start + (h + 1) * HEAD_DIM) for h in head_perm]
        return jnp.concatenate([take(0, start)] + heads + [take(start + Q_WIDTH, arr.shape[axis])], axis=axis)

    sinks_perm = attn_sinks[l]

    wr = jnp.tile(w_router[l], (1, TOP_K))
    wr_hi = wr.astype(BF16)
    weights = dict(
        norm1_g=norm1_g[l][None, :], w_in=permute_heads(w_in[l], 1, o2).astype(BF16),
        b_in=permute_heads(b_in[l], 0, o2)[None, :],
        conv_w=conv_w[l], conv_b=conv_b[l][None, :],
        ga=_block_diag_halves(lru_wa[l]), gx=_block_diag_halves(lru_wx[l]),
        lru_ba=lru_ba[l][None, :], lru_bx=lru_bx[l][None, :], lam=lru_lambda[l][None, :],
        w_out=permute_heads(w_out[l], 0, LRU_WIDTH).astype(BF16), b_out=b_out[l][None, :], norm2_g=norm2_g[l][None, :],
        wr_hi=wr_hi, wr_lo=(wr - wr_hi.astype(F32)).astype(BF16),
        b_router=jnp.tile(b_router[l], TOP_K)[None, :],
    )
    wlist = [weights[n] for n in WEIGHT_NAMES]

    mod_p, mod_s = _ada(jnp.concatenate([c_prompt, c_sample], axis=0), bp, w_ada[l], b_ada[l][None, :])
    mod_p = mod_p.reshape(bp, 6, D_MODEL)

    cos_p, sin_p = _rope_tables(jnp.arange(seq, dtype=jnp.int32))
    cos_s, sin_s = _rope_tables(PAST_LEN + jnp.arange(tdec, dtype=jnp.int32))
    cos_s = jnp.tile(cos_s, (SAMPLE_BT, 1))
    sin_s = jnp.tile(sin_s, (SAMPLE_BT, 1))
    h0_rows = jnp.pad(state_lru_h[l][:, None, :], ((0, 0), (0, tdec - 1), (0, 0))).reshape(n_sample, LRU_WIDTH)
    cprev_rows = jnp.pad(state_conv[l], ((0, 0), (0, tdec - (CONV_W - 1)), (0, 0))).reshape(n_sample, LRU_WIDTH)
    ck = cache_win_k[l].reshape(bd, WINDOW, KV_WIDTH)
    cv = cache_win_v[l].reshape(bd, WINDOW, KV_WIDTH)
    npt = n_prompt // SEQ_TILE
    nst = n_sample // SEQ_TILE
    mix_rows = MIX_TILES * SEQ_TILE
    steps_per_seq = seq // mix_rows
    wspecs = [_resident_spec(a) for a in wlist]

    rows_p = lambda width: pl.BlockSpec((mix_rows, width), lambda i: (i, 0))
    tail_p = lambda rows, width: pl.BlockSpec((None, rows, width), lambda i: (i // steps_per_seq, 0, 0))
    (x1_p, h2_p, route_p, cnt_p, hlast_p, ulast_p, klast_p, vlast_p) = pl.pallas_call(
        functools.partial(_prompt_kernel, steps_per_seq),
        grid=(n_prompt // mix_rows,),
        in_specs=[rows_p(D_MODEL),
                  pl.BlockSpec((None, 6, D_MODEL), lambda i: (i // steps_per_seq, 0, 0)),
                  pl.BlockSpec((mix_rows, LANES), lambda i: (i % steps_per_seq, 0)),
                  pl.BlockSpec((mix_rows, LANES), lambda i: (i % steps_per_seq, 0)),
                  pl.BlockSpec(memory_space=pltpu.SMEM)] + wspecs,
        out_specs=(rows_p(D_MODEL), rows_p(D_MODEL), rows_p(LANES),
                   pl.BlockSpec((MIX_TILES, SUBLANES, LANES), lambda i: (i, 0, 0)),
                   tail_p(SUBLANES, LRU_WIDTH), tail_p(SUBLANES, LRU_WIDTH),
                   tail_p(WINDOW, KV_WIDTH), tail_p(WINDOW, KV_WIDTH)),
        out_shape=(
            jax.ShapeDtypeStruct((n_prompt, D_MODEL), F32),
            jax.ShapeDtypeStruct((n_prompt, D_MODEL), BF16),
            jax.ShapeDtypeStruct((n_prompt, LANES), F32),
            jax.ShapeDtypeStruct((npt, SUBLANES, LANES), F32),
            jax.ShapeDtypeStruct((bp, SUBLANES, LRU_WIDTH), F32),
            jax.ShapeDtypeStruct((bp, SUBLANES, LRU_WIDTH), F32),
            jax.ShapeDtypeStruct((bp, WINDOW, KV_WIDTH), F32),
            jax.ShapeDtypeStruct((bp, WINDOW, KV_WIDTH), F32),
        ),
        scratch_shapes=[pltpu.VMEM((SUBLANES, LRU_WIDTH), F32), pltpu.VMEM((SUBLANES, LRU_WIDTH), F32),
                        pltpu.VMEM((WINDOW, KV_WIDTH), F32), pltpu.VMEM((WINDOW, KV_WIDTH), F32)],
        compiler_params=pltpu.CompilerParams(dimension_semantics=("arbitrary",), vmem_limit_bytes=VMEM_LIMIT),
        name="prompt_mixer",
    )(x_prompt.reshape(n_prompt, D_MODEL), mod_p, cos_p, sin_p, sinks_perm, *wlist)

    rows_s = lambda width: pl.BlockSpec((SEQ_TILE, width), lambda i: (i, 0))
    cache_spec = pl.BlockSpec((SAMPLE_BT, WINDOW, KV_WIDTH), lambda i: (i, 0, 0))
    (x1_s, h2_s, route_s, cnt_s, g2_rows, hs_s, u_s, s_k, s_v) = pl.pallas_call(
        _sample_kernel,
        grid=(nst,),
        in_specs=[rows_s(D_MODEL),
                  pl.BlockSpec((6, SAMPLE_BT, D_MODEL), lambda i: (0, i, 0)),
                  pl.BlockSpec((SEQ_TILE, LANES), lambda i: (0, 0)),
                  pl.BlockSpec((SEQ_TILE, LANES), lambda i: (0, 0)),
                  pl.BlockSpec(memory_space=pltpu.SMEM),
                  rows_s(LRU_WIDTH), rows_s(LRU_WIDTH), cache_spec, cache_spec] + wspecs,
        out_specs=(rows_s(D_MODEL), rows_s(D_MODEL), rows_s(LANES),
                   pl.BlockSpec((None, SUBLANES, LANES), lambda i: (i, 0, 0)),
                   rows_s(D_MODEL), rows_s(LRU_WIDTH), rows_s(LRU_WIDTH), cache_spec, cache_spec),
        out_shape=(
            jax.ShapeDtypeStruct((n_sample, D_MODEL), F32), jax.ShapeDtypeStruct((n_sample, D_MODEL), BF16),
            jax.ShapeDtypeStruct((n_sample, LANES), F32), jax.ShapeDtypeStruct((nst, SUBLANES, LANES), F32),
            jax.ShapeDtypeStruct((n_sample, D_MODEL), F32),
            jax.ShapeDtypeStruct((n_sample, LRU_WIDTH), F32),
            jax.ShapeDtypeStruct((n_sample, LRU_WIDTH), F32),
            jax.ShapeDtypeStruct((bd, WINDOW, KV_WIDTH), F32),
            jax.ShapeDtypeStruct((bd, WINDOW, KV_WIDTH), F32),
        ),
        compiler_params=pltpu.CompilerParams(dimension_semantics=("arbitrary",), vmem_limit_bytes=VMEM_LIMIT),
        name="sample_mixer",
    )(x_sample.reshape(n_sample, D_MODEL), mod_s, cos_s, sin_s, sinks_perm, h0_rows, cprev_rows, ck, cv, *wlist)
    tile_cnt = jnp.concatenate([cnt_p, cnt_s], axis=0)

    n_tiles = npt + nst
    n_assign = n_tok * TOP_K
    max_rows = n_assign + n_tiles * N_EXPERTS * (SUBLANES - 1) + N_EXPERTS * (MOE_TM - 1)
    n_blocks = -(-max_rows // MOE_TM)
    cnt = tile_cnt[:, 0, :N_EXPERTS].astype(jnp.int32)
    cnt8 = (cnt + SUBLANES - 1) // SUBLANES * SUBLANES
    counts = jnp.sum(cnt8, axis=0)
    pcounts = (counts + MOE_TM - 1) // MOE_TM * MOE_TM
    pend = jnp.cumsum(pcounts)
    pstart = pend - pcounts
    tile_off = pstart[None, :] + jnp.cumsum(cnt8, axis=0) - cnt8
    tot8 = jnp.sum(cnt8, axis=1)
    meta = jnp.concatenate([pstart + counts, pend, pend[-1:] // MOE_TM]).astype(jnp.int32)
    cnt8_flat = cnt8.reshape(-1)
    off_flat = tile_off.reshape(-1).astype(jnp.int32)

    xs = _dispatch(cnt8_flat, off_flat, tot8, meta, h2_p, h2_s, route_p, route_s, n_blocks * MOE_TM)

    big, small = MOE_CHUNK_CODES[0], MOE_CHUNK_CODES[1:]
    n_tm = pcounts // MOE_TM
    n_big = n_tm // big
    present = [(n_tm // b) % 2 for b in small]
    n_ch = n_big + sum(present)
    ch_end = jnp.cumsum(n_ch)
    ch_start = ch_end - n_ch
    n_chunks = n_blocks // big + len(small) * N_EXPERTS
    ci = jnp.arange(n_chunks, dtype=jnp.int32)
    owner = jnp.logical_and(ch_start[None, :] <= ci[:, None], ci[:, None] < ch_end[None, :])
    pick = lambda v: jnp.sum(jnp.where(owner, v[None, :], 0), axis=1)
    live = ci < ch_end[-1]
    local = ci - pick(ch_start)
    chunk_exp = jnp.where(live, pick(jnp.arange(N_EXPERTS, dtype=jnp.int32)), N_EXPERTS - 1).astype(jnp.int32)
    big_c = pick(n_big)
    code = jnp.where(local < big_c, big, 0)
    blocks_before = jnp.where(local < big_c, local * big, 0)
    order = big_c
    blocks = big_c * big
    for b, has in zip(small, present):
        has_c = pick(has)
        is_b = jnp.logical_and(has_c == 1, local == order)
        code = jnp.where(is_b, b, code)
        blocks_before = jnp.where(is_b, blocks, blocks_before)
        order = order + has_c
        blocks = blocks + has_c * b
    chunk_row = jnp.where(live, pick(pstart) + blocks_before * MOE_TM, 0).astype(jnp.int32)
    chunk_size = jnp.where(live, code, 0).astype(jnp.int32)
    has_rows = n_ch > 0
    e_ar = jnp.arange(N_EXPERTS, dtype=jnp.int32)
    w_slot = (jnp.cumsum(has_rows.astype(jnp.int32)) - 1) % 2
    later = jnp.logical_and(has_rows[None, :], e_ar[None, :] > e_ar[:, None])
    nxt = jnp.min(jnp.where(later, e_ar[None, :], N_EXPERTS), axis=1)
    nxt = jnp.where(nxt == N_EXPERTS, -1, nxt)
    weight_tbl = jnp.concatenate([jnp.logical_and(live, local == 0).astype(jnp.int32),
                                  pick(w_slot), jnp.where(live, pick(nxt), -1)]).astype(jnp.int32)
    moe_meta = jnp.concatenate([meta, ch_end[-1:].astype(jnp.int32)])
    ys = _moe(chunk_row, chunk_size, chunk_exp, weight_tbl, moe_meta, xs, w1[l], b1[l][:, None, :], w2[l], b2[l][:, None, :])

    y_p, y_s = _combine(cnt8_flat, off_flat, tot8, ys, route_p, route_s, x1_p, x1_s, mod_p, g2_rows,
                        final_g[None, :], seq // SEQ_TILE)

    y_prompt = y_p.reshape(bp, seq, D_MODEL)
    y_sample = y_s.reshape(bd, tdec, D_MODEL)
    p_h = hlast_p[:, SUBLANES - 1, :][None]
    p_c = ulast_p[:, SUBLANES - (CONV_W - 1):, :][None]
    p_k = klast_p.reshape(1, bp, WINDOW, N_KV_HEADS, HEAD_DIM)
    p_v = vlast_p.reshape(1, bp, WINDOW, N_KV_HEADS, HEAD_DIM)
    s_h = hs_s.reshape(bd, tdec, LRU_WIDTH)[:, tdec - 1, :][None]
    s_c = u_s.reshape(bd, tdec, LRU_WIDTH)[:, tdec - (CONV_W - 1):, :][None]
    s_kk = s_k.reshape(1, bd, WINDOW, N_KV_HEADS, HEAD_DIM)
    s_vv = s_v.reshape(1, bd, WINDOW, N_KV_HEADS, HEAD_DIM)
    return (y_prompt, y_sample, p_h, p_c, p_k, p_v, s_h, s_c, s_kk, s_vv)
```

```python
import functools

import jax
import jax.numpy as jnp
from jax import lax
from jax.experimental import pallas as pl
from jax.experimental.pallas import tpu as pltpu

F32 = jnp.float32
BF16 = jnp.bfloat16

D_MODEL = 1024
LRU_WIDTH = 512
LRU_BLOCKS = 8
LRU_BLOCK_W = LRU_WIDTH // LRU_BLOCKS
CONV_W = 4
LRU_C = 8.0
HEAD_DIM = 64
N_HEADS = 8
N_KV_HEADS = 2
GROUP = N_HEADS // N_KV_HEADS
WINDOW = 128
ROPE_THETA = 10000.0
N_EXPERTS = 32
TOP_K = 4
D_FF = D_MODEL
SWIGLU_LIMIT = 7.0
SWIGLU_ALPHA = 1.702
NORM_EPS = 1e-5
PAST_LEN = 8192
Q_WIDTH = N_HEADS * HEAD_DIM
KV_WIDTH = N_KV_HEADS * HEAD_DIM
IN_WIDTH = 2 * LRU_WIDTH + Q_WIDTH + 2 * KV_WIDTH

LANES = 128
SUBLANES = 8
SEQ_TILE = 256
MIX_TILES = 2
SAMPLE_BT = 32
MOE_TM = 256
MOE_CHUNK_CODES = (4, 2, 1)
MOE_CH = MOE_CHUNK_CODES[0] * MOE_TM
NEG_BIG = -1e30
VMEM_LIMIT = 56 * 1024 * 1024


def _rms(x, g):
    return x * lax.rsqrt(jnp.mean(x * x, axis=-1, keepdims=True) + NORM_EPS) * g


def _rms_mod(x, gain, shift):
    return x * lax.rsqrt(jnp.mean(x * x, axis=-1, keepdims=True) + NORM_EPS) * gain + shift


def _group_scan(a, b):
    rows, width = a.shape
    groups = rows // SUBLANES
    a3 = a.reshape(groups, SUBLANES, width)
    b3 = b.reshape(groups, SUBLANES, width)
    t = lax.broadcasted_iota(jnp.int32, (1, SUBLANES, 1), 1)
    d = 1
    while d < SUBLANES:
        keep = t >= d
        a_s = jnp.where(keep, pltpu.roll(a3, d, axis=1), 1.0)
        b_s = jnp.where(keep, pltpu.roll(b3, d, axis=1), 0.0)
        b3 = a3 * b_s + b3
        a3 = a3 * a_s
        d *= 2
    return a3.reshape(rows, width), b3.reshape(rows, width)


def _chain_groups(a_grp, b_grp, h_in):
    rows = a_grp.shape[0]
    out = []
    carry = h_in
    for g in range(rows // SUBLANES):
        sl = slice(g * SUBLANES, (g + 1) * SUBLANES)
        hg = b_grp[sl] + a_grp[sl] * carry
        out.append(hg)
        carry = hg[SUBLANES - 1:SUBLANES]
    return jnp.concatenate(out, axis=0)


def _rope128(x, cos, sin_signed, first_half):
    sw = jnp.where(first_half, pltpu.roll(x, LANES - HEAD_DIM // 2, axis=1), pltpu.roll(x, HEAD_DIM // 2, axis=1))
    return x * cos + sw * sin_signed


def _softplus(x):
    return jnp.maximum(x, 0.0) + jnp.log1p(jnp.exp(-jnp.abs(x)))


def _lru_coeffs(uc, w, first_pos_mask):
    ub = uc.astype(BF16)
    half = LRU_WIDTH // 2
    ra = jnp.concatenate([jnp.dot(ub[:, :half], w['ga'][0], preferred_element_type=F32),
                          jnp.dot(ub[:, half:], w['ga'][1], preferred_element_type=F32)], axis=1)
    rx = jnp.concatenate([jnp.dot(ub[:, :half], w['gx'][0], preferred_element_type=F32),
                          jnp.dot(ub[:, half:], w['gx'][1], preferred_element_type=F32)], axis=1)
    r = jax.nn.sigmoid(ra + w['lru_ba'][...])
    i = jax.nn.sigmoid(rx + w['lru_bx'][...])
    log_a = -LRU_C * r * _softplus(-w['lam'][...])
    a = jnp.exp(log_a)
    om = 1.0 - a * a
    mult = jnp.where(om > 0.0, om * lax.rsqrt(om), 0.0)
    if first_pos_mask is not None:
        mult = jnp.where(first_pos_mask, 1.0, mult)
    return a, mult * i * uc


def _conv_taps(u, s1, s2, s3, w):
    cw = w['conv_w']
    return w['conv_b'][...] + s3 * cw[0:1, :] + s2 * cw[1:2, :] + s1 * cw[2:3, :] + u * cw[3:4, :]


def _in_proj(x, mod, w):
    sh1, sc1 = mod
    h = _rms_mod(x, w['norm1_g'][...] * (1.0 + sc1), sh1)
    return jnp.dot(h.astype(BF16), w['w_in'][...], preferred_element_type=F32) + w['b_in'][...]


def _post_mix(x, mix, mod, w):
    g1, sh2, sc2 = mod
    x1 = x + g1 * (jnp.dot(mix.astype(BF16), w['w_out'][...], preferred_element_type=F32) + w['b_out'][...])
    h2 = _rms_mod(x1, w['norm2_g'][...] * (1.0 + sc2), sh2)
    h2_hi = h2.astype(BF16)
    h2_lo = (h2 - h2_hi.astype(F32)).astype(BF16)
    logits = (jnp.dot(h2_hi, w['wr_hi'][...], preferred_element_type=F32)
              + jnp.dot(h2_lo, w['wr_hi'][...], preferred_element_type=F32)
              + jnp.dot(h2_hi, w['wr_lo'][...], preferred_element_type=F32)) + w['b_router'][...]
    return x1, h2_hi, logits


ROUTE_E, ROUTE_G, ROUTE_R = 0, TOP_K, 2 * TOP_K


def _lane_roll1(v, shift):
    return pltpu.roll(jnp.broadcast_to(v, (SUBLANES, LANES)), shift, axis=1)[0:1]


def _route_tile(lg):
    rows = lg.shape[0]
    lane = lax.broadcasted_iota(jnp.int32, (1, LANES), 1)
    e_of = lane % N_EXPERTS
    grp = lane // N_EXPERTS
    e_id = e_of.astype(F32)
    onehot = jnp.zeros((rows, LANES), F32)
    vals, ids = [], []
    for k in range(TOP_K):
        m = jnp.max(lg, axis=1, keepdims=True)
        idx = jnp.min(jnp.where(lg == m, e_id, float(N_EXPERTS)), axis=1, keepdims=True)
        sel = e_id == idx
        lg = jnp.where(sel, -jnp.inf, lg)
        onehot = jnp.where(jnp.logical_and(sel, grp == k), 1.0, onehot)
        vals.append(m)
        ids.append(idx)
    ex = [jnp.exp(v - vals[0]) for v in vals]
    denom = ex[0] + ex[1] + ex[2] + ex[3]

    r_i = lax.broadcasted_iota(jnp.int32, (rows, rows), 0)
    c_i = lax.broadcasted_iota(jnp.int32, (rows, rows), 1)
    strict_lower = jnp.where(r_i > c_i, 1.0, 0.0).astype(BF16)
    prefix = jnp.dot(strict_lower, onehot.astype(BF16), preferred_element_type=F32)
    cnt = jnp.sum(onehot, axis=0, keepdims=True)
    base = jnp.zeros((1, LANES), F32)
    tot = cnt
    for s in range(1, TOP_K):
        rolled = _lane_roll1(cnt, s * N_EXPERTS)
        base = base + jnp.where(lane >= s * N_EXPERTS, rolled, 0.0)
        tot = tot + rolled
    pad_cnt = jnp.floor((tot + (SUBLANES - 1.0)) * (1.0 / SUBLANES)) * SUBLANES
    inc = pad_cnt
    d = 1
    while d < N_EXPERTS:
        inc = inc + jnp.where(e_of >= d, _lane_roll1(inc, d), 0.0)
        d *= 2
    strip_start = inc - pad_cnt
    ranked = onehot * (prefix + base + strip_start)

    route = jnp.zeros((rows, LANES), F32)
    for k in range(TOP_K):
        rank_k = jnp.sum(jnp.where(grp == k, ranked, 0.0), axis=1, keepdims=True)
        route = jnp.where(lane == ROUTE_E + k, ids[k], route)
        route = jnp.where(lane == ROUTE_G + k, ex[k] / denom, route)
        route = jnp.where(lane == ROUTE_R + k, rank_k, route)
    return route, tot


def _softmax_with_sink(s, sink_col):
    m = jnp.maximum(jnp.max(s, axis=-1, keepdims=True), sink_col)
    p = jnp.exp(s - m)
    denom = jnp.sum(p, axis=-1, keepdims=True) + jnp.exp(sink_col - m)
    return p, denom


WEIGHT_NAMES = ('norm1_g', 'w_in', 'b_in', 'conv_w', 'conv_b', 'ga', 'gx', 'lru_ba', 'lru_bx', 'lam',
                'w_out', 'b_out', 'norm2_g', 'wr_hi', 'wr_lo', 'b_router')


def _split_bf16(x):
    hi = x.astype(BF16)
    return hi, (x - hi.astype(F32)).astype(BF16)


def _ada_kernel(n_prompt_rows, c_ref, w_ref, b_ref, op_ref, os_ref):
    c = c_ref[...]
    s_hi, s_lo = _split_bf16(c * jax.nn.sigmoid(c))
    w_hi, w_lo = _split_bf16(w_ref[...])
    mod = (jnp.dot(s_hi, w_hi, preferred_element_type=F32) + jnp.dot(s_lo, w_hi, preferred_element_type=F32)
           + jnp.dot(s_hi, w_lo, preferred_element_type=F32)) + b_ref[...]
    op_ref[...] = mod[:n_prompt_rows]
    os_ref[...] = mod[n_prompt_rows:]


def _ada(c_all, n_prompt_rows, w_ada, b_ada):
    rows = c_all.shape[0]
    n_s = rows - n_prompt_rows
    assert n_prompt_rows % SUBLANES == 0
    return pl.pallas_call(
        functools.partial(_ada_kernel, n_prompt_rows),
        grid=(6,),
        in_specs=[pl.BlockSpec((rows, D_MODEL), lambda i: (0, 0)),
                  pl.BlockSpec((D_MODEL, D_MODEL), lambda i: (0, i)),
                  pl.BlockSpec((1, D_MODEL), lambda i: (0, i))],
        out_specs=(pl.BlockSpec((n_prompt_rows, D_MODEL), lambda i: (0, i)),
                   pl.BlockSpec((None, n_s, D_MODEL), lambda i: (i, 0, 0))),
        out_shape=(jax.ShapeDtypeStruct((n_prompt_rows, 6 * D_MODEL), F32),
                   jax.ShapeDtypeStruct((6, n_s, D_MODEL), F32)),
        compiler_params=pltpu.CompilerParams(dimension_semantics=("arbitrary",), vmem_limit_bytes=VMEM_LIMIT),
        name="ada",
    )(c_all, w_ada, b_ada)


def _prompt_body(seq_start, x_ref, mod_ref, cos_ref, sin_ref, sinks_ref, w, x1_ref, h2_ref, lg_ref,
                 hlast_ref, ulast_ref, klast_ref, vlast_ref, conv_c, h_c, k_c, v_c):
    ts = SEQ_TILE

    if seq_start is not None:
        @pl.when(seq_start)
        def _():
            conv_c[...] = jnp.zeros_like(conv_c)
            h_c[...] = jnp.zeros_like(h_c)
            k_c[...] = jnp.zeros_like(k_c)
            v_c[...] = jnp.zeros_like(v_c)

    x = x_ref[...]
    mod = mod_ref[...]
    proj = _in_proj(x, (mod[0:1], mod[1:2]), w)
    u = proj[:, :LRU_WIDTH]
    gate = proj[:, LRU_WIDTH:2 * LRU_WIDTH]
    o2 = 2 * LRU_WIDTH

    rowid = lax.broadcasted_iota(jnp.int32, (ts, 1), 0)
    u_ext = jnp.concatenate([conv_c[...], u], axis=0)
    s1, s2, s3 = (pltpu.roll(u_ext, d, axis=0)[SUBLANES:] for d in (1, 2, 3))
    uc = _conv_taps(u, s1, s2, s3, w)
    conv_c[...] = u[ts - SUBLANES:]
    ulast_ref[...] = u[ts - SUBLANES:]

    first_pos = None if seq_start is None else jnp.logical_and(rowid == 0, seq_start)
    a, bt = _lru_coeffs(uc, w, first_pos)
    hs = _chain_groups(*_group_scan(a, bt), h_c[0:1, :])
    h_tail = hs[ts - SUBLANES:]
    h_c[...] = jnp.broadcast_to(h_tail[SUBLANES - 1:SUBLANES, :], h_c.shape)
    hlast_ref[...] = h_tail
    lru_out = hs * jax.nn.gelu(gate)

    cos = cos_ref[...]
    sin = sin_ref[...]
    lane = lax.broadcasted_iota(jnp.int32, (1, LANES), 1)
    first_half = (lane % HEAD_DIM) < (HEAD_DIM // 2)
    qcols = [_rope128(proj[:, o2 + c * LANES:o2 + (c + 1) * LANES], cos, sin, first_half) * (HEAD_DIM ** -0.5)
             for c in range(4)]
    k = _rope128(proj[:, o2 + Q_WIDTH:o2 + Q_WIDTH + KV_WIDTH], cos, sin, first_half)
    v = proj[:, o2 + Q_WIDTH + KV_WIDTH:]
    k_ext = jnp.concatenate([k_c[...], k], axis=0).astype(BF16)
    v_ext = jnp.concatenate([v_c[...], v], axis=0).astype(BF16)
    k_c[...] = k[ts - WINDOW:]
    v_c[...] = v[ts - WINDOW:]
    klast_ref[...] = k[ts - WINDOW:]
    vlast_ref[...] = v[ts - WINDOW:]

    grow = lax.broadcasted_iota(jnp.int32, (GROUP * WINDOW, 1), 0)
    from_prev = lax.broadcasted_iota(jnp.int32, (GROUP * WINDOW, WINDOW), 1) > grow % WINDOW
    grow = grow // WINDOW
    lane_lo = lane < HEAD_DIM
    attn_cols = [[] for _ in range(4)]
    for blk in range(ts // WINDOW):
        kb = k_ext[blk * WINDOW:(blk + 2) * WINDOW]
        v_prev = v_ext[blk * WINDOW:(blk + 1) * WINDOW]
        v_own = v_ext[(blk + 1) * WINDOW:(blk + 2) * WINDOW]
        outs = []
        for kv in range(N_KV_HEADS):
            sel = lane_lo if kv == 0 else jnp.logical_not(lane_lo)
            qs = jnp.concatenate(
                [jnp.where(sel, qc[blk * WINDOW:(blk + 1) * WINDOW], 0.0) for qc in qcols], axis=0).astype(BF16)
            s = lax.dot_general(qs, kb, (((1,), (1,)), ((), ())), preferred_element_type=F32)
            s_prev = s[:, :WINDOW]
            if seq_start is not None and blk == 0:
                s_prev = jnp.where(seq_start, NEG_BIG, s_prev)
            s = jnp.where(from_prev, s_prev, s[:, WINDOW:])
            sink_col = jnp.zeros((GROUP * WINDOW, 1), F32)
            for g in range(GROUP):
                sink_col = jnp.where(grow == g, sinks_ref[kv * GROUP + g], sink_col)
            p, denom = _softmax_with_sink(s, sink_col)
            pv = (jnp.dot(jnp.where(from_prev, p, 0.0).astype(BF16), v_prev, preferred_element_type=F32)
                  + jnp.dot(jnp.where(from_prev, 0.0, p).astype(BF16), v_own, preferred_element_type=F32))
            outs.append(pv / denom)
        for c in range(4):
            attn_cols[c].append(jnp.where(lane_lo, outs[0][c * WINDOW:(c + 1) * WINDOW],
                                          outs[1][c * WINDOW:(c + 1) * WINDOW]))
    attn = jnp.concatenate([jnp.concatenate(cols, axis=0) for cols in attn_cols], axis=1)

    mix = jnp.concatenate([lru_out, attn], axis=1)
    x1, h2, logits = _post_mix(x, mix, (mod[2:3], mod[3:4], mod[4:5]), w)
    x1_ref[...] = x1
    route, tot = _route_tile(logits)
    lg_ref[0][...] = route
    lg_ref[1][...] = jnp.broadcast_to(tot, lg_ref[1].shape)
    h2_ref[...] = _sorted_rows(route, h2)


def _expand_rows(m, t):
    b, wd = m.shape
    return jnp.broadcast_to(m[:, None, :], (b, t, wd)).reshape(b * t, wd)


def _sample_body(x_ref, mod_ref, cos_ref, sin_ref, sinks_ref, h0_ref, cprev_ref, ck_ref, cv_ref, w,
                 x1_ref, h2_ref, lg_ref, g2_ref, hs_ref, u_ref, ko_ref, vo_ref):
    bt_, t = SAMPLE_BT, SUBLANES
    rows = bt_ * t

    x = x_ref[...]
    mods = [_expand_rows(mod_ref[i], t) for i in range(6)]
    proj = _in_proj(x, (mods[0], mods[1]), w)
    u = proj[:, :LRU_WIDTH]
    gate = proj[:, LRU_WIDTH:2 * LRU_WIDTH]
    o2 = 2 * LRU_WIDTH
    u_ref[...] = u

    rowid = lax.broadcasted_iota(jnp.int32, (rows, 1), 0) % t
    cprev = cprev_ref[...]
    taps = []
    for d in (1, 2, 3):
        taps.append(jnp.where(rowid >= d, pltpu.roll(u, d, axis=0),
                              pltpu.roll(cprev, (d - (CONV_W - 1)) % rows, axis=0)))
    uc = _conv_taps(u, taps[0], taps[1], taps[2], w)

    a, bt = _lru_coeffs(uc, w, None)
    bt = bt + a * h0_ref[...]
    _, hs = _group_scan(a, bt)
    hs_ref[...] = hs
    lru_out = hs * jax.nn.gelu(gate)

    cos = cos_ref[...]
    sin = sin_ref[...]
    lane = lax.broadcasted_iota(jnp.int32, (1, LANES), 1)
    first_half = (lane % HEAD_DIM) < (HEAD_DIM // 2)
    qcols = [_rope128(proj[:, o2 + c * LANES:o2 + (c + 1) * LANES], cos, sin, first_half) * (HEAD_DIM ** -0.5)
             for c in range(4)]
    k = _rope128(proj[:, o2 + Q_WIDTH:o2 + Q_WIDTH + KV_WIDTH], cos, sin, first_half)
    v = proj[:, o2 + Q_WIDTH + KV_WIDTH:]
    k3 = k.reshape(bt_, t, KV_WIDTH)
    v3 = v.reshape(bt_, t, KV_WIDTH)
    ck = ck_ref[...]
    cv = cv_ref[...]
    ko_ref[:, :WINDOW - t, :] = ck[:, t:, :]
    ko_ref[:, WINDOW - t:, :] = k3
    vo_ref[:, :WINDOW - t, :] = cv[:, t:, :]
    vo_ref[:, WINDOW - t:, :] = v3

    ckb, cvb, k3b, v3b = ck.astype(BF16), cv.astype(BF16), k3.astype(BF16), v3.astype(BF16)
    lane_lo = lane < HEAD_DIM
    gq = GROUP * t
    tq = lax.broadcasted_iota(jnp.int32, (1, gq, 1), 1) % t
    mask_c = lax.broadcasted_iota(jnp.int32, (1, gq, WINDOW), 2) > tq
    mask_n = lax.broadcasted_iota(jnp.int32, (1, gq, t), 2) <= tq
    grow = lax.broadcasted_iota(jnp.int32, (1, gq, 1), 1) // t
    bdims = (((2,), (2,)), ((0,), (0,)))
    pdims = (((2,), (1,)), ((0,), (0,)))
    outs = []
    for kv in range(N_KV_HEADS):
        sel = lane_lo if kv == 0 else jnp.logical_not(lane_lo)
        q3 = jnp.concatenate([jnp.where(sel, qc, 0.0).reshape(bt_, t, LANES) for qc in qcols], axis=1).astype(BF16)
        sc = lax.dot_general(q3, ckb, bdims, preferred_element_type=F32)
        sn = lax.dot_general(q3, k3b, bdims, preferred_element_type=F32)
        sc = jnp.where(mask_c, sc, NEG_BIG)
        sn = jnp.where(mask_n, sn, NEG_BIG)
        sink_col = jnp.zeros((1, gq, 1), F32)
        for g in range(GROUP):
            sink_col = jnp.where(grow == g, sinks_ref[kv * GROUP + g], sink_col)
        m = jnp.maximum(jnp.maximum(jnp.max(sc, axis=-1, keepdims=True), jnp.max(sn, axis=-1, keepdims=True)),
                        sink_col)
        pc = jnp.exp(sc - m)
        pn = jnp.exp(sn - m)
        denom = jnp.sum(pc, axis=-1, keepdims=True) + jnp.sum(pn, axis=-1, keepdims=True) + jnp.exp(sink_col - m)
        o = (lax.dot_general(pc.astype(BF16), cvb, pdims, preferred_element_type=F32)
             + lax.dot_general(pn.astype(BF16), v3b, pdims, preferred_element_type=F32)) / denom
        outs.append(o)
    attn = jnp.concatenate(
        [jnp.where(lane_lo, outs[0][:, c * t:(c + 1) * t, :], outs[1][:, c * t:(c + 1) * t, :]).reshape(rows, LANES)
         for c in range(4)], axis=1)

    mix = jnp.concatenate([lru_out, attn], axis=1)
    x1, h2, logits = _post_mix(x, mix, (mods[2], mods[3], mods[4]), w)
    x1_ref[...] = x1
    route, tot = _route_tile(logits)
    lg_ref[0][...] = route
    lg_ref[1][...] = jnp.broadcast_to(tot, lg_ref[1].shape)
    h2_ref[...] = _sorted_rows(route, h2)
    g2_ref[...] = mods[5]


def _prompt_kernel(steps_per_seq, x_ref, mod_ref, cos_ref, sin_ref, sinks_ref, *rest):
    nw = len(WEIGHT_NAMES)
    w = dict(zip(WEIGHT_NAMES, rest[:nw]))
    (x1_ref, h2_ref, route_ref, cnt_ref, hlast_ref, ulast_ref, klast_ref, vlast_ref,
     conv_c, h_c, k_c, v_c) = rest[nw:]
    seq_start = pl.program_id(0) % steps_per_seq == 0
    for sub in range(MIX_TILES):
        rows = pl.ds(sub * SEQ_TILE, SEQ_TILE)
        _prompt_body(seq_start if sub == 0 else None, x_ref.at[rows], mod_ref, cos_ref.at[rows], sin_ref.at[rows],
                     sinks_ref, w, x1_ref.at[rows], h2_ref.at[pl.ds(sub * SORT_ROWS, SORT_ROWS)],
                     (route_ref.at[rows], cnt_ref.at[sub]),
                     hlast_ref, ulast_ref, klast_ref, vlast_ref, conv_c, h_c, k_c, v_c)


def _sample_kernel(x_ref, mod_ref, cos_ref, sin_ref, sinks_ref, h0_ref, cprev_ref, ck_ref, cv_ref, *rest):
    nw = len(WEIGHT_NAMES)
    w = dict(zip(WEIGHT_NAMES, rest[:nw]))
    x1_ref, h2_ref, route_ref, cnt_ref, g2_ref, hs_ref, u_ref, ko_ref, vo_ref = rest[nw:]
    _sample_body(x_ref, mod_ref, cos_ref, sin_ref, sinks_ref, h0_ref, cprev_ref, ck_ref, cv_ref, w,
                 x1_ref, h2_ref, (route_ref, cnt_ref), g2_ref, hs_ref, u_ref, ko_ref, vo_ref)


PACK_W = D_MODEL // 2
PACKED = jnp.int32


def _pack_bf16_pairs(x):
    return pltpu.pack_elementwise([x[:, :PACK_W], x[:, PACK_W:]], packed_dtype=BF16)


def _unpack_bf16_pairs(w):
    return tuple(pltpu.unpack_elementwise(w, index=k, packed_dtype=BF16, unpacked_dtype=F32).astype(BF16)
                 for k in range(2))


STRIP_SIZES = tuple(SUBLANES << b for b in range(6))
STRIP_LARGE = 64
SORT_ROWS = SEQ_TILE * TOP_K + N_EXPERTS * SUBLANES
TILE_WAIT_SIZES = tuple(SUBLANES << b for b in range(8))


def _for_strips(cnt_ref, off_ref, tile, buf_slot, hbm, sem, to_hbm, act):
    def e_body(e, local):
        n = cnt_ref[tile * N_EXPERTS + e]
        glob = off_ref[tile * N_EXPERTS + e]

        def pieces(sizes, done):
            for p in sizes:
                piece = n & p
                lo = pl.ds(pl.multiple_of(local + done, SUBLANES), p)
                gl = pl.ds(pl.multiple_of(glob + done, SUBLANES), p)

                @pl.when(piece != 0)
                def _():
                    if to_hbm:
                        act(pltpu.make_async_copy(buf_slot.at[lo], hbm.at[gl], sem))
                    else:
                        act(pltpu.make_async_copy(hbm.at[gl], buf_slot.at[lo], sem))
                done = done + piece

        large = tuple(p for p in reversed(STRIP_SIZES) if p >= STRIP_LARGE)
        small = tuple(p for p in reversed(STRIP_SIZES) if p < STRIP_LARGE)
        n_large = n & (-STRIP_LARGE)

        @pl.when(n_large != 0)
        def _():
            pieces(large, 0)
        pieces(small, n_large)
        return local + n
    lax.fori_loop(0, N_EXPERTS, e_body, 0)


def _wait_tile_rows(total, buf_slot, hbm, sem, to_hbm):
    for p in TILE_WAIT_SIZES:
        @pl.when((total & p) != 0)
        def _():
            if to_hbm:
                pltpu.make_async_copy(buf_slot.at[pl.ds(0, p)], hbm.at[pl.ds(0, p)], sem).wait()
            else:
                pltpu.make_async_copy(hbm.at[pl.ds(0, p)], buf_slot.at[pl.ds(0, p)], sem).wait()


def _sorted_rows(route, h2b):
    route_t = route.T
    r_pos = lax.broadcasted_iota(jnp.int16, (SORT_ROWS, SEQ_TILE), 0)
    perm = jnp.zeros((SORT_ROWS, SEQ_TILE), BF16)
    for k in range(TOP_K):
        pos_k = route_t[ROUTE_R + k:ROUTE_R + k + 1, :].astype(jnp.int32).astype(jnp.int16)
        perm = jnp.where(r_pos == pos_k, jnp.ones((), BF16), perm)
    return _pack_bf16_pairs(jnp.dot(perm, h2b, preferred_element_type=F32))


def _dispatch_kernel(n_prompt_tiles, cnt_ref, off_ref, tot_ref, meta_ref, locp_hbm, locs_hbm, xs_hbm, zblk, sem, zsem):
    i = pl.program_id(0)
    nb = pl.num_programs(0)
    slot = i % 2
    n_blocks = xs_hbm.shape[0] // MOE_TM

    def tile_rows(hbm, t):
        return hbm.at[pl.ds(pl.multiple_of(t * SORT_ROWS, SUBLANES), SORT_ROWS)]

    @pl.when(i < n_prompt_tiles)
    def _():
        _for_strips(cnt_ref, off_ref, i, tile_rows(locp_hbm, i), xs_hbm, sem.at[slot], True, lambda cp: cp.start())

    @pl.when(i >= n_prompt_tiles)
    def _():
        _for_strips(cnt_ref, off_ref, i, tile_rows(locs_hbm, i - n_prompt_tiles), xs_hbm, sem.at[slot], True,
                    lambda cp: cp.start())

    @pl.when(i > 0)
    def _():
        _wait_tile_rows(tot_ref[jnp.maximum(i - 1, 0)], locp_hbm, xs_hbm, sem.at[1 - slot], True)

    @pl.when(i == nb - 1)
    def _():
        _wait_tile_rows(tot_ref[i], locp_hbm, xs_hbm, sem.at[slot], True)
        zblk[...] = jnp.zeros_like(zblk)

        def for_region_tails(act):
            def e_body(e, carry):
                start = meta_ref[e]
                n = meta_ref[N_EXPERTS + e] - start
                done = 0
                for p in reversed([q for q in STRIP_SIZES if q < MOE_TM]):
                    piece = n & p
                    rows = pl.ds(pl.multiple_of(start + done, SUBLANES), p)

                    @pl.when(piece != 0)
                    def _():
                        act(pltpu.make_async_copy(zblk.at[pl.ds(0, p)], xs_hbm.at[rows], zsem.at[0]))
                    done = done + piece
                return carry
            lax.fori_loop(0, N_EXPERTS, e_body, 0)

        def for_tail_blocks(act):
            def b_body(j, carry):
                act(pltpu.make_async_copy(zblk, xs_hbm.at[pl.ds(pl.multiple_of(j * MOE_TM, MOE_TM), MOE_TM)],
                                          zsem.at[0]))
                return carry
            lax.fori_loop(meta_ref[2 * N_EXPERTS], n_blocks, b_body, 0)

        for_region_tails(lambda cp: cp.start())
        for_tail_blocks(lambda cp: cp.start())
        for_region_tails(lambda cp: cp.wait())
        for_tail_blocks(lambda cp: cp.wait())


def _dispatch(cnt8, tile_off, tot8, meta, sorted_p, sorted_s, n_rows):
    npt = sorted_p.shape[0] // SORT_ROWS
    nt = tot8.shape[0]
    grid_spec = pltpu.PrefetchScalarGridSpec(
        num_scalar_prefetch=4,
        grid=(nt,),
        in_specs=[pl.BlockSpec(memory_space=pl.ANY), pl.BlockSpec(memory_space=pl.ANY)],
        out_specs=pl.BlockSpec(memory_space=pl.ANY),
        scratch_shapes=[pltpu.VMEM((MOE_TM, PACK_W), PACKED),
                        pltpu.SemaphoreType.DMA((2,)), pltpu.SemaphoreType.DMA((1,))],
    )
    return pl.pallas_call(
        functools.partial(_dispatch_kernel, npt),
        grid_spec=grid_spec,
        out_shape=jax.ShapeDtypeStruct((n_rows, PACK_W), PACKED),
        compiler_params=pltpu.CompilerParams(dimension_semantics=("arbitrary",), vmem_limit_bytes=VMEM_LIMIT),
        name="dispatch",
    )(cnt8, tile_off, tot8, meta, sorted_p, sorted_s)


def _expert_mlp(words, w1b, b1_ref, w2b, b2_ref, act_ref):
    xb = jnp.concatenate(_unpack_bf16_pairs(words), axis=1)
    q = D_FF // 4
    for c in range(4):
        zg = jnp.dot(xb, w1b[:, c * q:(c + 1) * q], preferred_element_type=F32) + b1_ref[:, c * q:(c + 1) * q]
        zl = (jnp.dot(xb, w1b[:, D_FF + c * q:D_FF + (c + 1) * q], preferred_element_type=F32)
              + b1_ref[:, D_FF + c * q:D_FF + (c + 1) * q])
        glu = jnp.minimum(zg, SWIGLU_LIMIT)
        lin = jnp.clip(zl, -SWIGLU_LIMIT, SWIGLU_LIMIT)
        act_ref[:, c * q:(c + 1) * q] = (glu * jax.nn.sigmoid(SWIGLU_ALPHA * glu) * (lin + 1.0)).astype(BF16)
    return _pack_bf16_pairs(jnp.dot(act_ref[...], w2b[...], preferred_element_type=F32) + b2_ref[...])


def _moe_kernel(row_ref, size_ref, exp_ref, wt_ref, meta_ref, xs_hbm, w1_hbm, b1_ref, w2_hbm, b2_ref, ys_hbm,
                xin, yout, w1f, w2f, w1b, w2b, act, isem, osem, wsem):
    tm = MOE_TM
    n_blocks = ys_hbm.shape[0] // tm
    n_tbl = row_ref.shape[0]
    n = meta_ref[2 * N_EXPERTS + 1]

    def for_chunk(j, fn):
        r = pl.multiple_of(row_ref[j], tm)
        for code in MOE_CHUNK_CODES:
            @pl.when(size_ref[j] == code)
            def _():
                fn(r, code * tm)

    def in_copy(r, rows, s):
        return pltpu.make_async_copy(xs_hbm.at[pl.ds(r, rows)], xin.at[s, pl.ds(0, rows)], isem.at[s])

    def out_copy(r, rows, s):
        return pltpu.make_async_copy(yout.at[s, pl.ds(0, rows)], ys_hbm.at[pl.ds(r, rows)], osem.at[s])

    def start_in(j, s):
        for_chunk(j, lambda r, rows: in_copy(r, rows, s).start())

    def wait_in(j, s):
        for_chunk(j, lambda r, rows: in_copy(r, rows, s).wait())

    def start_out(j, s):
        for_chunk(j, lambda r, rows: out_copy(r, rows, s).start())

    def wait_out(j, s):
        for_chunk(j, lambda r, rows: out_copy(r, rows, s).wait())

    def weight_copies(e, ws):
        return (pltpu.make_async_copy(w1_hbm.at[e], w1f.at[ws], wsem.at[ws]),
                pltpu.make_async_copy(w2_hbm.at[e], w2f.at[ws], wsem.at[ws]))

    start_in(0, 0)
    for cp in weight_copies(exp_ref[0], 0):
        cp.start()

    def trip(i, carry):
        slot = i % 2

        @pl.when(i + 1 < n)
        def _():
            start_in(i + 1, 1 - slot)

        @pl.when(i >= 2)
        def _():
            wait_out(i - 2, slot)

        @pl.when(wt_ref[i] == 1)
        def _():
            ws = wt_ref[n_tbl + i]
            nxt = wt_ref[2 * n_tbl + i]
            for cp in weight_copies(exp_ref[i], ws):
                cp.wait()

            @pl.when(nxt >= 0)
            def _():
                for cp in weight_copies(nxt, 1 - ws):
                    cp.start()

            chunk = 128
            def cast_body(c, carry2):
                k0 = pl.multiple_of(c * chunk, chunk)
                w1b[pl.ds(k0, chunk), :] = w1f[ws, pl.ds(k0, chunk), :].astype(BF16)
                w2b[pl.ds(k0, chunk), :] = w2f[ws, pl.ds(k0, chunk), :].astype(BF16)
                return carry2
            lax.fori_loop(0, D_MODEL // chunk, cast_body, 0)

        wait_in(i, slot)

        for code in MOE_CHUNK_CODES:
            rows = code * tm

            @pl.when(size_ref[i] == code)
            def _():
                yout[slot, 0:rows] = _expert_mlp(xin[slot, 0:rows], w1b, b1_ref.at[exp_ref[i]], w2b,
                                                 b2_ref.at[exp_ref[i]], act.at[pl.ds(0, rows)])

        start_out(i, slot)
        return carry
    lax.fori_loop(0, n, trip, 0)

    @pl.when(n >= 2)
    def _():
        wait_out(n - 2, n % 2)
    wait_out(n - 1, (n - 1) % 2)

    yout[1, 0:tm] = jnp.zeros((tm, PACK_W), PACKED)

    def zero_block(j):
        return pltpu.make_async_copy(yout.at[1, pl.ds(0, tm)], ys_hbm.at[pl.ds(pl.multiple_of(j * tm, tm), tm)],
                                     osem.at[1])

    def start_body(j, carry):
        zero_block(j).start()
        return carry

    def wait_body(j, carry):
        zero_block(j).wait()
        return carry
    lax.fori_loop(meta_ref[2 * N_EXPERTS], n_blocks, start_body, 0)
    lax.fori_loop(meta_ref[2 * N_EXPERTS], n_blocks, wait_body, 0)


def _moe(chunk_row, chunk_size, chunk_exp, weight_tbl, meta, xs, w1, b1, w2, b2):
    whole = lambda i, *_: (0, 0, 0)
    grid_spec = pltpu.PrefetchScalarGridSpec(
        num_scalar_prefetch=5,
        grid=(1,),
        in_specs=[
            pl.BlockSpec(memory_space=pl.ANY),
            pl.BlockSpec(memory_space=pl.ANY),
            pl.BlockSpec((N_EXPERTS, 1, 2 * D_FF), whole),
            pl.BlockSpec(memory_space=pl.ANY),
            pl.BlockSpec((N_EXPERTS, 1, D_MODEL), whole),
        ],
        out_specs=pl.BlockSpec(memory_space=pl.ANY),
        scratch_shapes=[pltpu.VMEM((2, MOE_CH, PACK_W), PACKED), pltpu.VMEM((2, MOE_CH, PACK_W), PACKED),
                        pltpu.VMEM((2, D_MODEL, 2 * D_FF), F32), pltpu.VMEM((2, D_FF, D_MODEL), F32),
                        pltpu.VMEM((D_MODEL, 2 * D_FF), BF16), pltpu.VMEM((D_FF, D_MODEL), BF16),
                        pltpu.VMEM((MOE_CH, D_FF), BF16),
                        pltpu.SemaphoreType.DMA((2,)), pltpu.SemaphoreType.DMA((2,)), pltpu.SemaphoreType.DMA((2,))],
    )
    return pl.pallas_call(
        _moe_kernel,
        grid_spec=grid_spec,
        out_shape=jax.ShapeDtypeStruct(xs.shape, PACKED),
        compiler_params=pltpu.CompilerParams(dimension_semantics=("arbitrary",), vmem_limit_bytes=VMEM_LIMIT),
        name="moe",
    )(chunk_row, chunk_size, chunk_exp, weight_tbl, meta, xs, w1, b1, w2, b2)


def _combine_kernel(n_prompt_tiles, cnt_ref, off_ref, tot_ref, ys_hbm, routep_ref, routes_ref, x1p_ref, x1s_ref,
                    modp_ref, g2s_ref, fg_ref, op_ref, os_ref, buf, sem):
    i = pl.program_id(0)
    nb = pl.num_programs(0)
    slot = i % 2

    def fetch(tile, s):
        _for_strips(cnt_ref, off_ref, tile, buf.at[s], ys_hbm, sem.at[s], False, lambda cp: cp.start())

    @pl.when(i == 0)
    def _():
        buf[...] = jnp.zeros_like(buf)
        fetch(0, 0)

    @pl.when(i + 1 < nb)
    def _():
        fetch(i + 1, 1 - slot)

    _wait_tile_rows(tot_ref[i], buf.at[slot], ys_hbm, sem.at[slot], False)

    is_prompt = i < n_prompt_tiles
    route = jnp.where(is_prompt, routep_ref[...], routes_ref[...])
    c_pos = lax.broadcasted_iota(jnp.int16, (SEQ_TILE, SORT_ROWS), 1)
    g_bf = jnp.zeros((SEQ_TILE, SORT_ROWS), BF16)
    for k in range(TOP_K):
        pos_k = route[:, ROUTE_R + k:ROUTE_R + k + 1].astype(jnp.int32).astype(jnp.int16)
        g_bf = jnp.where(c_pos == pos_k, route[:, ROUTE_G + k:ROUTE_G + k + 1].astype(BF16), g_bf)
    ff = jnp.concatenate([jnp.dot(g_bf, yb, preferred_element_type=F32) for yb in _unpack_bf16_pairs(buf[slot])],
                         axis=1)
    g2 = jnp.where(i < n_prompt_tiles, modp_ref[5:6, :], g2s_ref[...])
    x = jnp.where(is_prompt, x1p_ref[...], x1s_ref[...]) + g2 * ff
    y = _rms(x, fg_ref[...])

    @pl.when(i < n_prompt_tiles)
    def _():
        op_ref[...] = y

    @pl.when(i >= n_prompt_tiles)
    def _():
        os_ref[...] = y


def _combine(cnt8, tile_off, tot8, ys, route_p, route_s, x1_p, x1_s, mod_p, g2_rows, final_g, tiles_per_batch):
    nt = tot8.shape[0]
    tt = SEQ_TILE
    npt = x1_p.shape[0] // tt
    p_tile = lambda i, *_: (jnp.minimum(i, npt - 1), 0)
    s_tile = lambda i, *_: (jnp.maximum(i - npt, 0), 0)
    grid_spec = pltpu.PrefetchScalarGridSpec(
        num_scalar_prefetch=3,
        grid=(nt,),
        in_specs=[
            pl.BlockSpec(memory_space=pl.ANY),
            pl.BlockSpec((tt, LANES), p_tile), pl.BlockSpec((tt, LANES), s_tile),
            pl.BlockSpec((tt, D_MODEL), p_tile), pl.BlockSpec((tt, D_MODEL), s_tile),
            pl.BlockSpec((None, 6, D_MODEL), lambda i, *_: (jnp.minimum(i, npt - 1) // tiles_per_batch, 0, 0)),
            pl.BlockSpec((tt, D_MODEL), s_tile),
            pl.BlockSpec((1, D_MODEL), lambda i, *_: (0, 0)),
        ],
        out_specs=(pl.BlockSpec((tt, D_MODEL), p_tile), pl.BlockSpec((tt, D_MODEL), s_tile)),
        scratch_shapes=[pltpu.VMEM((2, SORT_ROWS, PACK_W), PACKED), pltpu.SemaphoreType.DMA((2,))],
    )
    return pl.pallas_call(
        functools.partial(_combine_kernel, npt),
        grid_spec=grid_spec,
        out_shape=(jax.ShapeDtypeStruct((npt * tt, D_MODEL), F32),
                   jax.ShapeDtypeStruct(((nt - npt) * tt, D_MODEL), F32)),
        compiler_params=pltpu.CompilerParams(dimension_semantics=("arbitrary",), vmem_limit_bytes=VMEM_LIMIT),
        name="combine",
    )(cnt8, tile_off, tot8, ys, route_p, route_s, x1_p, x1_s, mod_p, g2_rows, final_g)


def _block_diag_halves(wg):
    per_half = LRU_BLOCKS // 2
    w4 = wg.reshape(2, per_half, LRU_BLOCK_W, LRU_BLOCK_W)
    on_diag = jnp.eye(per_half, dtype=bool)[None, :, None, :, None]
    dense = jnp.where(on_diag, w4[:, :, :, None, :], 0.0)
    return dense.reshape(2, per_half * LRU_BLOCK_W, per_half * LRU_BLOCK_W).astype(BF16)


def _rope_tables(pos):
    half = HEAD_DIM // 2
    inv = ROPE_THETA ** (-jnp.arange(half, dtype=F32) / half)
    ang = pos.astype(F32)[:, None] * inv[None, :]
    cos = jnp.cos(ang)
    sin = jnp.sin(ang)
    cos128 = jnp.concatenate([cos, cos, cos, cos], axis=1)
    sin128 = jnp.concatenate([-sin, sin, -sin, sin], axis=1)
    return cos128, sin128


def _resident_spec(arr):
    zeros = (0,) * arr.ndim
    return pl.BlockSpec(arr.shape, lambda i: zeros)


def kernel(x_prompt, x_sample, state_lru_h, state_conv, cache_win_k, cache_win_v, c_prompt, c_sample, w_ada, b_ada, norm1_g, w_in, b_in, conv_w, conv_b, lru_wa, lru_ba, lru_wx, lru_bx, lru_lambda, attn_sinks, w_out, b_out, norm2_g, w_router, b_router, w1, b1, w2, b2, final_g):
    bp, seq, _ = x_prompt.shape
    bd, tdec, _ = x_sample.shape
    assert tdec == SUBLANES and seq % (MIX_TILES * SEQ_TILE) == 0 and bd % SAMPLE_BT == 0
    assert SAMPLE_BT * tdec == SEQ_TILE
    n_prompt = bp * seq
    n_sample = bd * tdec
    n_tok = n_prompt + n_sample
    l = 0

    head_perm = [h for c in range(4) for h in (c, GROUP + c)]
    o2 = 2 * LRU_WIDTH

    def permute_heads(arr, axis, start):
        take = lambda a, b: lax.slice_in_dim(arr, a, b, axis=axis)
        heads = [take(start + h * HEAD_DIM, start + (h + 1) * HEAD_DIM) for h in head_perm]
        return jnp.concatenate([take(0, start)] + heads + [take(start + Q_WIDTH, arr.shape[axis])], axis=axis)

    sinks_perm = attn_sinks[l]

    wr = jnp.tile(w_router[l], (1, TOP_K))
    wr_hi = wr.astype(BF16)
    weights = dict(
        norm1_g=norm1_g[l][None, :], w_in=permute_heads(w_in[l], 1, o2).astype(BF16),
        b_in=permute_heads(b_in[l], 0, o2)[None, :],
        conv_w=conv_w[l], conv_b=conv_b[l][None, :],
        ga=_block_diag_halves(lru_wa[l]), gx=_block_diag_halves(lru_wx[l]),
        lru_ba=lru_ba[l][None, :], lru_bx=lru_bx[l][None, :], lam=lru_lambda[l][None, :],
        w_out=permute_heads(w_out[l], 0, LRU_WIDTH).astype(BF16), b_out=b_out[l][None, :], norm2_g=norm2_g[l][None, :],
        wr_hi=wr_hi, wr_lo=(wr - wr_hi.astype(F32)).astype(BF16),
        b_router=jnp.tile(b_router[l], TOP_K)[None, :],
    )
    wlist = [weights[n] for n in WEIGHT_NAMES]

    mod_p, mod_s = _ada(jnp.concatenate([c_prompt, c_sample], axis=0), bp, w_ada[l], b_ada[l][None, :])
    mod_p = mod_p.reshape(bp, 6, D_MODEL)

    cos_p, sin_p = _rope_tables(jnp.arange(seq, dtype=jnp.int32))
    cos_s, sin_s = _rope_tables(PAST_LEN + jnp.arange(tdec, dtype=jnp.int32))
    cos_s = jnp.tile(cos_s, (SAMPLE_BT, 1))
    sin_s = jnp.tile(sin_s, (SAMPLE_BT, 1))
    h0_rows = jnp.pad(state_lru_h[l][:, None, :], ((0, 0), (0, tdec - 1), (0, 0))).reshape(n_sample, LRU_WIDTH)
    cprev_rows = jnp.pad(state_conv[l], ((0, 0), (0, tdec - (CONV_W - 1)), (0, 0))).reshape(n_sample, LRU_WIDTH)
    ck = cache_win_k[l].reshape(bd, WINDOW, KV_WIDTH)
    cv = cache_win_v[l].reshape(bd, WINDOW, KV_WIDTH)
    npt = n_prompt // SEQ_TILE
    nst = n_sample // SEQ_TILE
    mix_rows = MIX_TILES * SEQ_TILE
    steps_per_seq = seq // mix_rows
    wspecs = [_resident_spec(a) for a in wlist]

    rows_p = lambda width: pl.BlockSpec((mix_rows, width), lambda i: (i, 0))
    tail_p = lambda rows, width: pl.BlockSpec((None, rows, width), lambda i: (i // steps_per_seq, 0, 0))
    (x1_p, h2_p, route_p, cnt_p, hlast_p, ulast_p, klast_p, vlast_p) = pl.pallas_call(
        functools.partial(_prompt_kernel, steps_per_seq),
        grid=(n_prompt // mix_rows,),
        in_specs=[rows_p(D_MODEL),
                  pl.BlockSpec((None, 6, D_MODEL), lambda i: (i // steps_per_seq, 0, 0)),
                  pl.BlockSpec((mix_rows, LANES), lambda i: (i % steps_per_seq, 0)),
                  pl.BlockSpec((mix_rows, LANES), lambda i: (i % steps_per_seq, 0)),
                  pl.BlockSpec(memory_space=pltpu.SMEM)] + wspecs,
        out_specs=(rows_p(D_MODEL), pl.BlockSpec((MIX_TILES * SORT_ROWS, PACK_W), lambda i: (i, 0)), rows_p(LANES),
                   pl.BlockSpec((MIX_TILES, SUBLANES, LANES), lambda i: (i, 0, 0)),
                   tail_p(SUBLANES, LRU_WIDTH), tail_p(SUBLANES, LRU_WIDTH),
                   tail_p(WINDOW, KV_WIDTH), tail_p(WINDOW, KV_WIDTH)),
        out_shape=(
            jax.ShapeDtypeStruct((n_prompt, D_MODEL), F32),
            jax.ShapeDtypeStruct((npt * SORT_ROWS, PACK_W), PACKED),
            jax.ShapeDtypeStruct((n_prompt, LANES), F32),
            jax.ShapeDtypeStruct((npt, SUBLANES, LANES), F32),
            jax.ShapeDtypeStruct((bp, SUBLANES, LRU_WIDTH), F32),
            jax.ShapeDtypeStruct((bp, SUBLANES, LRU_WIDTH), F32),
            jax.ShapeDtypeStruct((bp, WINDOW, KV_WIDTH), F32),
            jax.ShapeDtypeStruct((bp, WINDOW, KV_WIDTH), F32),
        ),
        scratch_shapes=[pltpu.VMEM((SUBLANES, LRU_WIDTH), F32), pltpu.VMEM((SUBLANES, LRU_WIDTH), F32),
                        pltpu.VMEM((WINDOW, KV_WIDTH), F32), pltpu.VMEM((WINDOW, KV_WIDTH), F32)],
        compiler_params=pltpu.CompilerParams(dimension_semantics=("arbitrary",), vmem_limit_bytes=VMEM_LIMIT),
        name="prompt_mixer",
    )(x_prompt.reshape(n_prompt, D_MODEL), mod_p, cos_p, sin_p, sinks_perm, *wlist)

    rows_s = lambda width: pl.BlockSpec((SEQ_TILE, width), lambda i: (i, 0))
    cache_spec = pl.BlockSpec((SAMPLE_BT, WINDOW, KV_WIDTH), lambda i: (i, 0, 0))
    (x1_s, h2_s, route_s, cnt_s, g2_rows, hs_s, u_s, s_k, s_v) = pl.pallas_call(
        _sample_kernel,
        grid=(nst,),
        in_specs=[rows_s(D_MODEL),
                  pl.BlockSpec((6, SAMPLE_BT, D_MODEL), lambda i: (0, i, 0)),
                  pl.BlockSpec((SEQ_TILE, LANES), lambda i: (0, 0)),
                  pl.BlockSpec((SEQ_TILE, LANES), lambda i: (0, 0)),
                  pl.BlockSpec(memory_space=pltpu.SMEM),
                  rows_s(LRU_WIDTH), rows_s(LRU_WIDTH), cache_spec, cache_spec] + wspecs,
        out_specs=(rows_s(D_MODEL), pl.BlockSpec((SORT_ROWS, PACK_W), lambda i: (i, 0)), rows_s(LANES),
                   pl.BlockSpec((None, SUBLANES, LANES), lambda i: (i, 0, 0)),
                   rows_s(D_MODEL), rows_s(LRU_WIDTH), rows_s(LRU_WIDTH), cache_spec, cache_spec),
        out_shape=(
            jax.ShapeDtypeStruct((n_sample, D_MODEL), F32), jax.ShapeDtypeStruct((nst * SORT_ROWS, PACK_W), PACKED),
            jax.ShapeDtypeStruct((n_sample, LANES), F32), jax.ShapeDtypeStruct((nst, SUBLANES, LANES), F32),
            jax.ShapeDtypeStruct((n_sample, D_MODEL), F32),
            jax.ShapeDtypeStruct((n_sample, LRU_WIDTH), F32),
            jax.ShapeDtypeStruct((n_sample, LRU_WIDTH), F32),
            jax.ShapeDtypeStruct((bd, WINDOW, KV_WIDTH), F32),
            jax.ShapeDtypeStruct((bd, WINDOW, KV_WIDTH), F32),
        ),
        compiler_params=pltpu.CompilerParams(dimension_semantics=("arbitrary",), vmem_limit_bytes=VMEM_LIMIT),
        name="sample_mixer",
    )(x_sample.reshape(n_sample, D_MODEL), mod_s, cos_s, sin_s, sinks_perm, h0_rows, cprev_rows, ck, cv, *wlist)
    tile_cnt = jnp.concatenate([cnt_p, cnt_s], axis=0)

    n_tiles = npt + nst
    n_assign = n_tok * TOP_K
    max_rows = n_assign + n_tiles * N_EXPERTS * (SUBLANES - 1) + N_EXPERTS * (MOE_TM - 1)
    n_blocks = -(-max_rows // MOE_TM)
    cnt = tile_cnt[:, 0, :N_EXPERTS].astype(jnp.int32)
    cnt8 = (cnt + SUBLANES - 1) // SUBLANES * SUBLANES
    counts = jnp.sum(cnt8, axis=0)
    pcounts = (counts + MOE_TM - 1) // MOE_TM * MOE_TM
    pend = jnp.cumsum(pcounts)
    pstart = pend - pcounts
    tile_off = pstart[None, :] + jnp.cumsum(cnt8, axis=0) - cnt8
    tot8 = jnp.sum(cnt8, axis=1)
    meta = jnp.concatenate([pstart + counts, pend, pend[-1:] // MOE_TM]).astype(jnp.int32)
    cnt8_flat = cnt8.reshape(-1)
    off_flat = tile_off.reshape(-1).astype(jnp.int32)

    xs = _dispatch(cnt8_flat, off_flat, tot8, meta, h2_p, h2_s, n_blocks * MOE_TM)

    big, small = MOE_CHUNK_CODES[0], MOE_CHUNK_CODES[1:]
    n_tm = pcounts // MOE_TM
    n_big = n_tm // big
    present = [(n_tm // b) % 2 for b in small]
    n_ch = n_big + sum(present)
    ch_end = jnp.cumsum(n_ch)
    ch_start = ch_end - n_ch
    n_chunks = n_blocks // big + len(small) * N_EXPERTS
    ci = jnp.arange(n_chunks, dtype=jnp.int32)
    owner = jnp.logical_and(ch_start[None, :] <= ci[:, None], ci[:, None] < ch_end[None, :])
    pick = lambda v: jnp.sum(jnp.where(owner, v[None, :], 0), axis=1)
    live = ci < ch_end[-1]
    local = ci - pick(ch_start)
    chunk_exp = jnp.where(live, pick(jnp.arange(N_EXPERTS, dtype=jnp.int32)), N_EXPERTS - 1).astype(jnp.int32)
    big_c = pick(n_big)
    code = jnp.where(local < big_c, big, 0)
    blocks_before = jnp.where(local < big_c, local * big, 0)
    order = big_c
    blocks = big_c * big
    for b, has in zip(small, present):
        has_c = pick(has)
        is_b = jnp.logical_and(has_c == 1, local == order)
        code = jnp.where(is_b, b, code)
        blocks_before = jnp.where(is_b, blocks, blocks_before)
        order = order + has_c
        blocks = blocks + has_c * b
    chunk_row = jnp.where(live, pick(pstart) + blocks_before * MOE_TM, 0).astype(jnp.int32)
    chunk_size = jnp.where(live, code, 0).astype(jnp.int32)
    has_rows = n_ch > 0
    e_ar = jnp.arange(N_EXPERTS, dtype=jnp.int32)
    w_slot = (jnp.cumsum(has_rows.astype(jnp.int32)) - 1) % 2
    later = jnp.logical_and(has_rows[None, :], e_ar[None, :] > e_ar[:, None])
    nxt = jnp.min(jnp.where(later, e_ar[None, :], N_EXPERTS), axis=1)
    nxt = jnp.where(nxt == N_EXPERTS, -1, nxt)
    weight_tbl = jnp.concatenate([jnp.logical_and(live, local == 0).astype(jnp.int32),
                                  pick(w_slot), jnp.where(live, pick(nxt), -1)]).astype(jnp.int32)
    moe_meta = jnp.concatenate([meta, ch_end[-1:].astype(jnp.int32)])
    ys = _moe(chunk_row, chunk_size, chunk_exp, weight_tbl, moe_meta, xs, w1[l], b1[l][:, None, :], w2[l], b2[l][:, None, :])

    y_p, y_s = _combine(cnt8_flat, off_flat, tot8, ys, route_p, route_s, x1_p, x1_s, mod_p, g2_rows,
                        final_g[None, :], seq // SEQ_TILE)

    y_prompt = y_p.reshape(bp, seq, D_MODEL)
    y_sample = y_s.reshape(bd, tdec, D_MODEL)
    p_h = hlast_p[:, SUBLANES - 1, :][None]
    p_c = ulast_p[:, SUBLANES - (CONV_W - 1):, :][None]
    p_k = klast_p.reshape(1, bp, WINDOW, N_KV_HEADS, HEAD_DIM)
    p_v = vlast_p.reshape(1, bp, WINDOW, N_KV_HEADS, HEAD_DIM)
    s_h = hs_s.reshape(bd, tdec, LRU_WIDTH)[:, tdec - 1, :][None]
    s_c = u_s.reshape(bd, tdec, LRU_WIDTH)[:, tdec - (CONV_W - 1):, :][None]
    s_kk = s_k.reshape(1, bd, WINDOW, N_KV_HEADS, HEAD_DIM)
    s_vv = s_v.reshape(1, bd, WINDOW, N_KV_HEADS, HEAD_DIM)
    return (y_prompt, y_sample, p_h, p_c, p_k, p_v, s_h, s_c, s_kk, s_vv)
```

```python
import functools

import jax
import jax.numpy as jnp
from jax import lax
from jax.experimental import pallas as pl
from jax.experimental.pallas import tpu as pltpu

F32 = jnp.float32
BF16 = jnp.bfloat16

D_MODEL = 1024
LRU_WIDTH = 512
LRU_BLOCKS = 8
LRU_BLOCK_W = LRU_WIDTH // LRU_BLOCKS
CONV_W = 4
LRU_C = 8.0
HEAD_DIM = 64
N_HEADS = 8
N_KV_HEADS = 2
GROUP = N_HEADS // N_KV_HEADS
WINDOW = 128
ROPE_THETA = 10000.0
N_EXPERTS = 32
TOP_K = 4
D_FF = D_MODEL
SWIGLU_LIMIT = 7.0
SWIGLU_ALPHA = 1.702
NORM_EPS = 1e-5
PAST_LEN = 8192
Q_WIDTH = N_HEADS * HEAD_DIM
KV_WIDTH = N_KV_HEADS * HEAD_DIM
IN_WIDTH = 2 * LRU_WIDTH + Q_WIDTH + 2 * KV_WIDTH

LANES = 128
SUBLANES = 8
SEQ_TILE = 256
MIX_TILES = 2
SAMPLE_BT = 32
MOE_TM = 256
MOE_CHUNK_CODES = (4, 2, 1)
MOE_CH = MOE_CHUNK_CODES[0] * MOE_TM
NEG_BIG = -1e30
VMEM_LIMIT = 56 * 1024 * 1024


def _rms(x, g):
    return x * lax.rsqrt(jnp.mean(x * x, axis=-1, keepdims=True) + NORM_EPS) * g


def _rms_mod(x, gain, shift):
    return x * lax.rsqrt(jnp.mean(x * x, axis=-1, keepdims=True) + NORM_EPS) * gain + shift


def _group_scan(a, b):
    rows, width = a.shape
    groups = rows // SUBLANES
    a3 = a.reshape(groups, SUBLANES, width)
    b3 = b.reshape(groups, SUBLANES, width)
    t = lax.broadcasted_iota(jnp.int32, (1, SUBLANES, 1), 1)
    d = 1
    while d < SUBLANES:
        keep = t >= d
        a_s = jnp.where(keep, pltpu.roll(a3, d, axis=1), 1.0)
        b_s = jnp.where(keep, pltpu.roll(b3, d, axis=1), 0.0)
        b3 = a3 * b_s + b3
        a3 = a3 * a_s
        d *= 2
    return a3.reshape(rows, width), b3.reshape(rows, width)


def _chain_groups(a_grp, b_grp, h_in):
    rows = a_grp.shape[0]
    out = []
    carry = h_in
    for g in range(rows // SUBLANES):
        sl = slice(g * SUBLANES, (g + 1) * SUBLANES)
        hg = b_grp[sl] + a_grp[sl] * carry
        out.append(hg)
        carry = hg[SUBLANES - 1:SUBLANES]
    return jnp.concatenate(out, axis=0)


def _rope128(x, cos, sin_signed, first_half):
    sw = jnp.where(first_half, pltpu.roll(x, LANES - HEAD_DIM // 2, axis=1), pltpu.roll(x, HEAD_DIM // 2, axis=1))
    return x * cos + sw * sin_signed


def _softplus(x):
    return jnp.maximum(x, 0.0) + jnp.log1p(jnp.exp(-jnp.abs(x)))


def _lru_coeffs(uc, w, first_pos_mask):
    ub = uc.astype(BF16)
    half = LRU_WIDTH // 2
    ra = jnp.concatenate([jnp.dot(ub[:, :half], w['ga'][0], preferred_element_type=F32),
                          jnp.dot(ub[:, half:], w['ga'][1], preferred_element_type=F32)], axis=1)
    rx = jnp.concatenate([jnp.dot(ub[:, :half], w['gx'][0], preferred_element_type=F32),
                          jnp.dot(ub[:, half:], w['gx'][1], preferred_element_type=F32)], axis=1)
    r = jax.nn.sigmoid(ra + w['lru_ba'][...])
    i = jax.nn.sigmoid(rx + w['lru_bx'][...])
    log_a = -LRU_C * r * _softplus(-w['lam'][...])
    a = jnp.exp(log_a)
    om = 1.0 - a * a
    mult = jnp.where(om > 0.0, om * lax.rsqrt(om), 0.0)
    if first_pos_mask is not None:
        mult = jnp.where(first_pos_mask, 1.0, mult)
    return a, mult * i * uc


def _conv_taps(u, s1, s2, s3, w):
    cw = w['conv_w']
    return w['conv_b'][...] + s3 * cw[0:1, :] + s2 * cw[1:2, :] + s1 * cw[2:3, :] + u * cw[3:4, :]


def _in_proj(x, mod, w):
    sh1, sc1 = mod
    h = _rms_mod(x, w['norm1_g'][...] * (1.0 + sc1), sh1)
    return jnp.dot(h.astype(BF16), w['w_in'][...], preferred_element_type=F32) + w['b_in'][...]


def _post_mix(x, mix, mod, w):
    g1, sh2, sc2 = mod
    x1 = x + g1 * (jnp.dot(mix.astype(BF16), w['w_out'][...], preferred_element_type=F32) + w['b_out'][...])
    h2 = _rms_mod(x1, w['norm2_g'][...] * (1.0 + sc2), sh2)
    h2_hi = h2.astype(BF16)
    h2_lo = (h2 - h2_hi.astype(F32)).astype(BF16)
    logits = (jnp.dot(h2_hi, w['wr_hi'][...], preferred_element_type=F32)
              + jnp.dot(h2_lo, w['wr_hi'][...], preferred_element_type=F32)
              + jnp.dot(h2_hi, w['wr_lo'][...], preferred_element_type=F32)) + w['b_router'][...]
    return x1, h2_hi, logits


ROUTE_E, ROUTE_G, ROUTE_R = 0, TOP_K, 2 * TOP_K


def _lane_roll1(v, shift):
    return pltpu.roll(jnp.broadcast_to(v, (SUBLANES, LANES)), shift, axis=1)[0:1]


def _route_tile(lg):
    rows = lg.shape[0]
    lane = lax.broadcasted_iota(jnp.int32, (1, LANES), 1)
    e_of = lane % N_EXPERTS
    grp = lane // N_EXPERTS
    e_id = e_of.astype(F32)
    onehot = jnp.zeros((rows, LANES), F32)
    vals, ids = [], []
    for k in range(TOP_K):
        m = jnp.max(lg, axis=1, keepdims=True)
        idx = jnp.min(jnp.where(lg == m, e_id, float(N_EXPERTS)), axis=1, keepdims=True)
        sel = e_id == idx
        lg = jnp.where(sel, -jnp.inf, lg)
        onehot = jnp.where(jnp.logical_and(sel, grp == k), 1.0, onehot)
        vals.append(m)
        ids.append(idx)
    ex = [jnp.exp(v - vals[0]) for v in vals]
    denom = ex[0] + ex[1] + ex[2] + ex[3]

    r_i = lax.broadcasted_iota(jnp.int32, (rows, rows), 0)
    c_i = lax.broadcasted_iota(jnp.int32, (rows, rows), 1)
    strict_lower = jnp.where(r_i > c_i, 1.0, 0.0).astype(BF16)
    prefix = jnp.dot(strict_lower, onehot.astype(BF16), preferred_element_type=F32)
    cnt = jnp.sum(onehot, axis=0, keepdims=True)
    base = jnp.zeros((1, LANES), F32)
    tot = cnt
    for s in range(1, TOP_K):
        rolled = _lane_roll1(cnt, s * N_EXPERTS)
        base = base + jnp.where(lane >= s * N_EXPERTS, rolled, 0.0)
        tot = tot + rolled
    pad_cnt = jnp.floor((tot + (SUBLANES - 1.0)) * (1.0 / SUBLANES)) * SUBLANES
    inc = pad_cnt
    d = 1
    while d < N_EXPERTS:
        inc = inc + jnp.where(e_of >= d, _lane_roll1(inc, d), 0.0)
        d *= 2
    strip_start = inc - pad_cnt
    ranked = onehot * (prefix + base + strip_start)

    route = jnp.zeros((rows, LANES), F32)
    for k in range(TOP_K):
        rank_k = jnp.sum(jnp.where(grp == k, ranked, 0.0), axis=1, keepdims=True)
        route = jnp.where(lane == ROUTE_E + k, ids[k], route)
        route = jnp.where(lane == ROUTE_G + k, ex[k] / denom, route)
        route = jnp.where(lane == ROUTE_R + k, rank_k, route)
    return route, tot


def _softmax_with_sink(s, sink_col):
    m = jnp.maximum(jnp.max(s, axis=-1, keepdims=True), sink_col)
    p = jnp.exp(s - m)
    denom = jnp.sum(p, axis=-1, keepdims=True) + jnp.exp(sink_col - m)
    return p, denom


WEIGHT_NAMES = ('norm1_g', 'w_in', 'b_in', 'conv_w', 'conv_b', 'ga', 'gx', 'lru_ba', 'lru_bx', 'lam',
                'w_out', 'b_out', 'norm2_g', 'wr_hi', 'wr_lo', 'b_router')


def _split_bf16(x):
    hi = x.astype(BF16)
    return hi, (x - hi.astype(F32)).astype(BF16)


def _ada_kernel(n_prompt_rows, c_ref, w_ref, b_ref, op_ref, os_ref):
    c = c_ref[...]
    s_hi, s_lo = _split_bf16(c * jax.nn.sigmoid(c))
    w_hi, w_lo = _split_bf16(w_ref[...])
    mod = (jnp.dot(s_hi, w_hi, preferred_element_type=F32) + jnp.dot(s_lo, w_hi, preferred_element_type=F32)
           + jnp.dot(s_hi, w_lo, preferred_element_type=F32)) + b_ref[...]
    op_ref[...] = mod[:n_prompt_rows]
    os_ref[...] = mod[n_prompt_rows:]


def _ada(c_all, n_prompt_rows, w_ada, b_ada):
    rows = c_all.shape[0]
    n_s = rows - n_prompt_rows
    assert n_prompt_rows % SUBLANES == 0
    return pl.pallas_call(
        functools.partial(_ada_kernel, n_prompt_rows),
        grid=(6,),
        in_specs=[pl.BlockSpec((rows, D_MODEL), lambda i: (0, 0)),
                  pl.BlockSpec((D_MODEL, D_MODEL), lambda i: (0, i)),
                  pl.BlockSpec((1, D_MODEL), lambda i: (0, i))],
        out_specs=(pl.BlockSpec((n_prompt_rows, D_MODEL), lambda i: (0, i)),
                   pl.BlockSpec((None, n_s, D_MODEL), lambda i: (i, 0, 0))),
        out_shape=(jax.ShapeDtypeStruct((n_prompt_rows, 6 * D_MODEL), F32),
                   jax.ShapeDtypeStruct((6, n_s, D_MODEL), F32)),
        compiler_params=pltpu.CompilerParams(dimension_semantics=("arbitrary",), vmem_limit_bytes=VMEM_LIMIT),
        name="ada",
    )(c_all, w_ada, b_ada)


def _prompt_body(seq_start, x_ref, mod_ref, cos_ref, sin_ref, sinks_ref, w, x1_ref, h2_ref, lg_ref,
                 hlast_ref, ulast_ref, klast_ref, vlast_ref, conv_c, h_c, k_c, v_c):
    ts = SEQ_TILE

    if seq_start is not None:
        @pl.when(seq_start)
        def _():
            conv_c[...] = jnp.zeros_like(conv_c)
            h_c[...] = jnp.zeros_like(h_c)
            k_c[...] = jnp.zeros_like(k_c)
            v_c[...] = jnp.zeros_like(v_c)

    x = x_ref[...]
    mod = mod_ref[...]
    proj = _in_proj(x, (mod[0:1], mod[1:2]), w)
    u = proj[:, :LRU_WIDTH]
    gate = proj[:, LRU_WIDTH:2 * LRU_WIDTH]
    o2 = 2 * LRU_WIDTH

    rowid = lax.broadcasted_iota(jnp.int32, (ts, 1), 0)
    u_ext = jnp.concatenate([conv_c[...], u], axis=0)
    s1, s2, s3 = (pltpu.roll(u_ext, d, axis=0)[SUBLANES:] for d in (1, 2, 3))
    uc = _conv_taps(u, s1, s2, s3, w)
    conv_c[...] = u[ts - SUBLANES:]
    ulast_ref[...] = u[ts - SUBLANES:]

    first_pos = None if seq_start is None else jnp.logical_and(rowid == 0, seq_start)
    a, bt = _lru_coeffs(uc, w, first_pos)
    hs = _chain_groups(*_group_scan(a, bt), h_c[0:1, :])
    h_tail = hs[ts - SUBLANES:]
    h_c[...] = jnp.broadcast_to(h_tail[SUBLANES - 1:SUBLANES, :], h_c.shape)
    hlast_ref[...] = h_tail
    lru_out = hs * jax.nn.gelu(gate)

    cos = cos_ref[...]
    sin = sin_ref[...]
    lane = lax.broadcasted_iota(jnp.int32, (1, LANES), 1)
    first_half = (lane % HEAD_DIM) < (HEAD_DIM // 2)
    qcols = [_rope128(proj[:, o2 + c * LANES:o2 + (c + 1) * LANES], cos, sin, first_half) * (HEAD_DIM ** -0.5)
             for c in range(4)]
    k = _rope128(proj[:, o2 + Q_WIDTH:o2 + Q_WIDTH + KV_WIDTH], cos, sin, first_half)
    v = proj[:, o2 + Q_WIDTH + KV_WIDTH:]
    k_ext = jnp.concatenate([k_c[...], k], axis=0).astype(BF16)
    v_ext = jnp.concatenate([v_c[...], v], axis=0).astype(BF16)
    k_c[...] = k[ts - WINDOW:]
    v_c[...] = v[ts - WINDOW:]
    klast_ref[...] = k[ts - WINDOW:]
    vlast_ref[...] = v[ts - WINDOW:]

    grow = lax.broadcasted_iota(jnp.int32, (GROUP * WINDOW, 1), 0)
    from_prev = lax.broadcasted_iota(jnp.int32, (GROUP * WINDOW, WINDOW), 1) > grow % WINDOW
    grow = grow // WINDOW
    lane_lo = lane < HEAD_DIM
    attn_cols = [[] for _ in range(4)]
    for blk in range(ts // WINDOW):
        kb = k_ext[blk * WINDOW:(blk + 2) * WINDOW]
        v_prev = v_ext[blk * WINDOW:(blk + 1) * WINDOW]
        v_own = v_ext[(blk + 1) * WINDOW:(blk + 2) * WINDOW]
        outs = []
        for kv in range(N_KV_HEADS):
            sel = lane_lo if kv == 0 else jnp.logical_not(lane_lo)
            qs = jnp.concatenate(
                [jnp.where(sel, qc[blk * WINDOW:(blk + 1) * WINDOW], 0.0) for qc in qcols], axis=0).astype(BF16)
            s = lax.dot_general(qs, kb, (((1,), (1,)), ((), ())), preferred_element_type=F32)
            s_prev = s[:, :WINDOW]
            if seq_start is not None and blk == 0:
                s_prev = jnp.where(seq_start, NEG_BIG, s_prev)
            s = jnp.where(from_prev, s_prev, s[:, WINDOW:])
            sink_col = jnp.zeros((GROUP * WINDOW, 1), F32)
            for g in range(GROUP):
                sink_col = jnp.where(grow == g, sinks_ref[kv * GROUP + g], sink_col)
            p, denom = _softmax_with_sink(s, sink_col)
            pv = (jnp.dot(jnp.where(from_prev, p, 0.0).astype(BF16), v_prev, preferred_element_type=F32)
                  + jnp.dot(jnp.where(from_prev, 0.0, p).astype(BF16), v_own, preferred_element_type=F32))
            outs.append(pv / denom)
        for c in range(4):
            attn_cols[c].append(jnp.where(lane_lo, outs[0][c * WINDOW:(c + 1) * WINDOW],
                                          outs[1][c * WINDOW:(c + 1) * WINDOW]))
    attn = jnp.concatenate([jnp.concatenate(cols, axis=0) for cols in attn_cols], axis=1)

    mix = jnp.concatenate([lru_out, attn], axis=1)
    x1, h2, logits = _post_mix(x, mix, (mod[2:3], mod[3:4], mod[4:5]), w)
    x1_ref[...] = x1
    h2_ref[...] = h2
    route, tot = _route_tile(logits)
    lg_ref[0][...] = route
    lg_ref[1][...] = jnp.broadcast_to(tot, lg_ref[1].shape)


def _expand_rows(m, t):
    b, wd = m.shape
    return jnp.broadcast_to(m[:, None, :], (b, t, wd)).reshape(b * t, wd)


def _sample_body(x_ref, mod_ref, cos_ref, sin_ref, sinks_ref, h0_ref, cprev_ref, ck_ref, cv_ref, w,
                 x1_ref, h2_ref, lg_ref, g2_ref, hs_ref, u_ref, ko_ref, vo_ref):
    bt_, t = SAMPLE_BT, SUBLANES
    rows = bt_ * t

    x = x_ref[...]
    mods = [_expand_rows(mod_ref[i], t) for i in range(6)]
    proj = _in_proj(x, (mods[0], mods[1]), w)
    u = proj[:, :LRU_WIDTH]
    gate = proj[:, LRU_WIDTH:2 * LRU_WIDTH]
    o2 = 2 * LRU_WIDTH
    u_ref[...] = u

    rowid = lax.broadcasted_iota(jnp.int32, (rows, 1), 0) % t
    cprev = cprev_ref[...]
    taps = []
    for d in (1, 2, 3):
        taps.append(jnp.where(rowid >= d, pltpu.roll(u, d, axis=0),
                              pltpu.roll(cprev, (d - (CONV_W - 1)) % rows, axis=0)))
    uc = _conv_taps(u, taps[0], taps[1], taps[2], w)

    a, bt = _lru_coeffs(uc, w, None)
    bt = bt + a * h0_ref[...]
    _, hs = _group_scan(a, bt)
    hs_ref[...] = hs
    lru_out = hs * jax.nn.gelu(gate)

    cos = cos_ref[...]
    sin = sin_ref[...]
    lane = lax.broadcasted_iota(jnp.int32, (1, LANES), 1)
    first_half = (lane % HEAD_DIM) < (HEAD_DIM // 2)
    qcols = [_rope128(proj[:, o2 + c * LANES:o2 + (c + 1) * LANES], cos, sin, first_half) * (HEAD_DIM ** -0.5)
             for c in range(4)]
    k = _rope128(proj[:, o2 + Q_WIDTH:o2 + Q_WIDTH + KV_WIDTH], cos, sin, first_half)
    v = proj[:, o2 + Q_WIDTH + KV_WIDTH:]
    k3 = k.reshape(bt_, t, KV_WIDTH)
    v3 = v.reshape(bt_, t, KV_WIDTH)
    ck = ck_ref[...]
    cv = cv_ref[...]
    ko_ref[:, :WINDOW - t, :] = ck[:, t:, :]
    ko_ref[:, WINDOW - t:, :] = k3
    vo_ref[:, :WINDOW - t, :] = cv[:, t:, :]
    vo_ref[:, WINDOW - t:, :] = v3

    ckb, cvb, k3b, v3b = ck.astype(BF16), cv.astype(BF16), k3.astype(BF16), v3.astype(BF16)
    lane_lo = lane < HEAD_DIM
    gq = GROUP * t
    tq = lax.broadcasted_iota(jnp.int32, (1, gq, 1), 1) % t
    mask_c = lax.broadcasted_iota(jnp.int32, (1, gq, WINDOW), 2) > tq
    mask_n = lax.broadcasted_iota(jnp.int32, (1, gq, t), 2) <= tq
    grow = lax.broadcasted_iota(jnp.int32, (1, gq, 1), 1) // t
    bdims = (((2,), (2,)), ((0,), (0,)))
    pdims = (((2,), (1,)), ((0,), (0,)))
    outs = []
    for kv in range(N_KV_HEADS):
        sel = lane_lo if kv == 0 else jnp.logical_not(lane_lo)
        q3 = jnp.concatenate([jnp.where(sel, qc, 0.0).reshape(bt_, t, LANES) for qc in qcols], axis=1).astype(BF16)
        sc = lax.dot_general(q3, ckb, bdims, preferred_element_type=F32)
        sn = lax.dot_general(q3, k3b, bdims, preferred_element_type=F32)
        sc = jnp.where(mask_c, sc, NEG_BIG)
        sn = jnp.where(mask_n, sn, NEG_BIG)
        sink_col = jnp.zeros((1, gq, 1), F32)
        for g in range(GROUP):
            sink_col = jnp.where(grow == g, sinks_ref[kv * GROUP + g], sink_col)
        m = jnp.maximum(jnp.maximum(jnp.max(sc, axis=-1, keepdims=True), jnp.max(sn, axis=-1, keepdims=True)),
                        sink_col)
        pc = jnp.exp(sc - m)
        pn = jnp.exp(sn - m)
        denom = jnp.sum(pc, axis=-1, keepdims=True) + jnp.sum(pn, axis=-1, keepdims=True) + jnp.exp(sink_col - m)
        o = (lax.dot_general(pc.astype(BF16), cvb, pdims, preferred_element_type=F32)
             + lax.dot_general(pn.astype(BF16), v3b, pdims, preferred_element_type=F32)) / denom
        outs.append(o)
    attn = jnp.concatenate(
        [jnp.where(lane_lo, outs[0][:, c * t:(c + 1) * t, :], outs[1][:, c * t:(c + 1) * t, :]).reshape(rows, LANES)
         for c in range(4)], axis=1)

    mix = jnp.concatenate([lru_out, attn], axis=1)
    x1, h2, logits = _post_mix(x, mix, (mods[2], mods[3], mods[4]), w)
    x1_ref[...] = x1
    h2_ref[...] = h2
    route, tot = _route_tile(logits)
    lg_ref[0][...] = route
    lg_ref[1][...] = jnp.broadcast_to(tot, lg_ref[1].shape)
    g2_ref[...] = mods[5]


def _prompt_kernel(steps_per_seq, x_ref, mod_ref, cos_ref, sin_ref, sinks_ref, *rest):
    nw = len(WEIGHT_NAMES)
    w = dict(zip(WEIGHT_NAMES, rest[:nw]))
    (x1_ref, h2_ref, route_ref, cnt_ref, hlast_ref, ulast_ref, klast_ref, vlast_ref,
     conv_c, h_c, k_c, v_c) = rest[nw:]
    seq_start = pl.program_id(0) % steps_per_seq == 0
    for sub in range(MIX_TILES):
        rows = pl.ds(sub * SEQ_TILE, SEQ_TILE)
        _prompt_body(seq_start if sub == 0 else None, x_ref.at[rows], mod_ref, cos_ref.at[rows], sin_ref.at[rows],
                     sinks_ref, w, x1_ref.at[rows], h2_ref.at[rows], (route_ref.at[rows], cnt_ref.at[sub]),
                     hlast_ref, ulast_ref, klast_ref, vlast_ref, conv_c, h_c, k_c, v_c)


def _sample_kernel(x_ref, mod_ref, cos_ref, sin_ref, sinks_ref, h0_ref, cprev_ref, ck_ref, cv_ref, *rest):
    nw = len(WEIGHT_NAMES)
    w = dict(zip(WEIGHT_NAMES, rest[:nw]))
    x1_ref, h2_ref, route_ref, cnt_ref, g2_ref, hs_ref, u_ref, ko_ref, vo_ref = rest[nw:]
    _sample_body(x_ref, mod_ref, cos_ref, sin_ref, sinks_ref, h0_ref, cprev_ref, ck_ref, cv_ref, w,
                 x1_ref, h2_ref, (route_ref, cnt_ref), g2_ref, hs_ref, u_ref, ko_ref, vo_ref)


PACK_W = D_MODEL // 2
PACKED = jnp.int32


def _pack_bf16_pairs(x):
    return pltpu.pack_elementwise([x[:, :PACK_W], x[:, PACK_W:]], packed_dtype=BF16)


def _unpack_bf16_pairs(w):
    return tuple(pltpu.unpack_elementwise(w, index=k, packed_dtype=BF16, unpacked_dtype=F32).astype(BF16)
                 for k in range(2))


STRIP_SIZES = tuple(SUBLANES << b for b in range(6))
DISPATCH_SLOTS = 3
STRIP_LARGE = 64
SORT_ROWS = SEQ_TILE * TOP_K + N_EXPERTS * SUBLANES
TILE_WAIT_SIZES = tuple(SUBLANES << b for b in range(8))


def _for_strips(cnt_ref, off_ref, tile, buf_slot, hbm, sem, to_hbm, act):
    def e_body(e, local):
        n = cnt_ref[tile * N_EXPERTS + e]
        glob = off_ref[tile * N_EXPERTS + e]

        def pieces(sizes, done):
            for p in sizes:
                piece = n & p
                lo = pl.ds(pl.multiple_of(local + done, SUBLANES), p)
                gl = pl.ds(pl.multiple_of(glob + done, SUBLANES), p)

                @pl.when(piece != 0)
                def _():
                    if to_hbm:
                        act(pltpu.make_async_copy(buf_slot.at[lo], hbm.at[gl], sem))
                    else:
                        act(pltpu.make_async_copy(hbm.at[gl], buf_slot.at[lo], sem))
                done = done + piece

        large = tuple(p for p in reversed(STRIP_SIZES) if p >= STRIP_LARGE)
        small = tuple(p for p in reversed(STRIP_SIZES) if p < STRIP_LARGE)
        n_large = n & (-STRIP_LARGE)

        @pl.when(n_large != 0)
        def _():
            pieces(large, 0)
        pieces(small, n_large)
        return local + n
    lax.fori_loop(0, N_EXPERTS, e_body, 0)


def _wait_tile_rows(total, buf_slot, hbm, sem, to_hbm):
    for p in TILE_WAIT_SIZES:
        @pl.when((total & p) != 0)
        def _():
            if to_hbm:
                pltpu.make_async_copy(buf_slot.at[pl.ds(0, p)], hbm.at[pl.ds(0, p)], sem).wait()
            else:
                pltpu.make_async_copy(hbm.at[pl.ds(0, p)], buf_slot.at[pl.ds(0, p)], sem).wait()


def _dispatch_kernel(n_prompt_tiles, cnt_ref, off_ref, tot_ref, meta_ref, h2p_ref, h2s_ref, routep_ref, routes_ref,
                     xs_hbm, sbuf, zblk, sem, zsem):
    i = pl.program_id(0)
    nb = pl.num_programs(0)
    slot = i % DISPATCH_SLOTS
    n_blocks = xs_hbm.shape[0] // MOE_TM

    is_prompt = i < n_prompt_tiles
    h2 = jnp.where(is_prompt, h2p_ref[...], h2s_ref[...])
    route_t = jnp.where(is_prompt, routep_ref[...], routes_ref[...]).T
    r_pos = lax.broadcasted_iota(jnp.int16, (SORT_ROWS, SEQ_TILE), 0)
    perm = jnp.zeros((SORT_ROWS, SEQ_TILE), BF16)
    for k in range(TOP_K):
        pos_k = route_t[ROUTE_R + k:ROUTE_R + k + 1, :].astype(jnp.int32).astype(jnp.int16)
        perm = jnp.where(r_pos == pos_k, jnp.ones((), BF16), perm)
    sbuf[slot] = _pack_bf16_pairs(jnp.dot(perm, h2, preferred_element_type=F32))

    _for_strips(cnt_ref, off_ref, i, sbuf.at[slot], xs_hbm, sem.at[slot], True, lambda cp: cp.start())

    def wait_tile(t):
        s = t % DISPATCH_SLOTS
        _wait_tile_rows(tot_ref[t], sbuf.at[s], xs_hbm, sem.at[s], True)

    @pl.when(i >= DISPATCH_SLOTS - 1)
    def _():
        wait_tile(jnp.maximum(i - (DISPATCH_SLOTS - 1), 0))

    @pl.when(i == nb - 1)
    def _():
        for back in range(DISPATCH_SLOTS - 2, -1, -1):
            @pl.when(i >= back)
            def _():
                wait_tile(jnp.maximum(i - back, 0))
        zblk[...] = jnp.zeros_like(zblk)

        def for_region_tails(act):
            def e_body(e, carry):
                start = meta_ref[e]
                n = meta_ref[N_EXPERTS + e] - start
                done = 0
                for p in reversed([q for q in STRIP_SIZES if q < MOE_TM]):
                    piece = n & p
                    rows = pl.ds(pl.multiple_of(start + done, SUBLANES), p)

                    @pl.when(piece != 0)
                    def _():
                        act(pltpu.make_async_copy(zblk.at[pl.ds(0, p)], xs_hbm.at[rows], zsem.at[0]))
                    done = done + piece
                return carry
            lax.fori_loop(0, N_EXPERTS, e_body, 0)

        def for_tail_blocks(act):
            def b_body(j, carry):
                act(pltpu.make_async_copy(zblk, xs_hbm.at[pl.ds(pl.multiple_of(j * MOE_TM, MOE_TM), MOE_TM)],
                                          zsem.at[0]))
                return carry
            lax.fori_loop(meta_ref[2 * N_EXPERTS], n_blocks, b_body, 0)

        for_region_tails(lambda cp: cp.start())
        for_tail_blocks(lambda cp: cp.start())
        for_region_tails(lambda cp: cp.wait())
        for_tail_blocks(lambda cp: cp.wait())


def _dispatch(cnt8, tile_off, tot8, meta, h2_p, h2_s, route_p, route_s, n_rows):
    npt = h2_p.shape[0] // SEQ_TILE
    p_tile = lambda i, *_: (jnp.minimum(i, npt - 1), 0)
    s_tile = lambda i, *_: (jnp.maximum(i - npt, 0), 0)
    nt = tot8.shape[0]
    tt = SEQ_TILE
    grid_spec = pltpu.PrefetchScalarGridSpec(
        num_scalar_prefetch=4,
        grid=(nt,),
        in_specs=[pl.BlockSpec((tt, D_MODEL), p_tile), pl.BlockSpec((tt, D_MODEL), s_tile),
                  pl.BlockSpec((tt, LANES), p_tile), pl.BlockSpec((tt, LANES), s_tile)],
        out_specs=pl.BlockSpec(memory_space=pl.ANY),
        scratch_shapes=[pltpu.VMEM((DISPATCH_SLOTS, SORT_ROWS, PACK_W), PACKED), pltpu.VMEM((MOE_TM, PACK_W), PACKED),
                        pltpu.SemaphoreType.DMA((DISPATCH_SLOTS,)), pltpu.SemaphoreType.DMA((1,))],
    )
    return pl.pallas_call(
        functools.partial(_dispatch_kernel, npt),
        grid_spec=grid_spec,
        out_shape=jax.ShapeDtypeStruct((n_rows, PACK_W), PACKED),
        compiler_params=pltpu.CompilerParams(dimension_semantics=("arbitrary",), vmem_limit_bytes=VMEM_LIMIT),
        name="dispatch",
    )(cnt8, tile_off, tot8, meta, h2_p, h2_s, route_p, route_s)


def _expert_mlp(words, w1b, b1_ref, w2b, b2_ref, act_ref):
    xb = jnp.concatenate(_unpack_bf16_pairs(words), axis=1)
    q = D_FF // 4
    for c in range(4):
        zg = jnp.dot(xb, w1b[:, c * q:(c + 1) * q], preferred_element_type=F32) + b1_ref[:, c * q:(c + 1) * q]
        zl = (jnp.dot(xb, w1b[:, D_FF + c * q:D_FF + (c + 1) * q], preferred_element_type=F32)
              + b1_ref[:, D_FF + c * q:D_FF + (c + 1) * q])
        glu = jnp.minimum(zg, SWIGLU_LIMIT)
        lin = jnp.clip(zl, -SWIGLU_LIMIT, SWIGLU_LIMIT)
        act_ref[:, c * q:(c + 1) * q] = (glu * jax.nn.sigmoid(SWIGLU_ALPHA * glu) * (lin + 1.0)).astype(BF16)
    return _pack_bf16_pairs(jnp.dot(act_ref[...], w2b[...], preferred_element_type=F32) + b2_ref[...])


def _moe_kernel(row_ref, size_ref, exp_ref, wt_ref, meta_ref, xs_hbm, w1_hbm, b1_ref, w2_hbm, b2_ref, ys_hbm,
                xin, yout, w1f, w2f, w1b, w2b, act, isem, osem, wsem):
    tm = MOE_TM
    n_blocks = ys_hbm.shape[0] // tm
    n_tbl = row_ref.shape[0]
    n = meta_ref[2 * N_EXPERTS + 1]

    def for_chunk(j, fn):
        r = pl.multiple_of(row_ref[j], tm)
        for code in MOE_CHUNK_CODES:
            @pl.when(size_ref[j] == code)
            def _():
                fn(r, code * tm)

    def in_copy(r, rows, s):
        return pltpu.make_async_copy(xs_hbm.at[pl.ds(r, rows)], xin.at[s, pl.ds(0, rows)], isem.at[s])

    def out_copy(r, rows, s):
        return pltpu.make_async_copy(yout.at[s, pl.ds(0, rows)], ys_hbm.at[pl.ds(r, rows)], osem.at[s])

    def start_in(j, s):
        for_chunk(j, lambda r, rows: in_copy(r, rows, s).start())

    def wait_in(j, s):
        for_chunk(j, lambda r, rows: in_copy(r, rows, s).wait())

    def start_out(j, s):
        for_chunk(j, lambda r, rows: out_copy(r, rows, s).start())

    def wait_out(j, s):
        for_chunk(j, lambda r, rows: out_copy(r, rows, s).wait())

    def weight_copies(e, ws):
        return (pltpu.make_async_copy(w1_hbm.at[e], w1f.at[ws], wsem.at[ws]),
                pltpu.make_async_copy(w2_hbm.at[e], w2f.at[ws], wsem.at[ws]))

    start_in(0, 0)
    for cp in weight_copies(exp_ref[0], 0):
        cp.start()

    def trip(i, carry):
        slot = i % 2

        @pl.when(i + 1 < n)
        def _():
            start_in(i + 1, 1 - slot)

        @pl.when(i >= 2)
        def _():
            wait_out(i - 2, slot)

        @pl.when(wt_ref[i] == 1)
        def _():
            ws = wt_ref[n_tbl + i]
            nxt = wt_ref[2 * n_tbl + i]
            for cp in weight_copies(exp_ref[i], ws):
                cp.wait()

            @pl.when(nxt >= 0)
            def _():
                for cp in weight_copies(nxt, 1 - ws):
                    cp.start()

            chunk = 128
            def cast_body(c, carry2):
                k0 = pl.multiple_of(c * chunk, chunk)
                w1b[pl.ds(k0, chunk), :] = w1f[ws, pl.ds(k0, chunk), :].astype(BF16)
                w2b[pl.ds(k0, chunk), :] = w2f[ws, pl.ds(k0, chunk), :].astype(BF16)
                return carry2
            lax.fori_loop(0, D_MODEL // chunk, cast_body, 0)

        wait_in(i, slot)

        for code in MOE_CHUNK_CODES:
            rows = code * tm

            @pl.when(size_ref[i] == code)
            def _():
                yout[slot, 0:rows] = _expert_mlp(xin[slot, 0:rows], w1b, b1_ref.at[exp_ref[i]], w2b,
                                                 b2_ref.at[exp_ref[i]], act.at[pl.ds(0, rows)])

        start_out(i, slot)
        return carry
    lax.fori_loop(0, n, trip, 0)

    @pl.when(n >= 2)
    def _():
        wait_out(n - 2, n % 2)
    wait_out(n - 1, (n - 1) % 2)

    yout[1, 0:tm] = jnp.zeros((tm, PACK_W), PACKED)

    def zero_block(j):
        return pltpu.make_async_copy(yout.at[1, pl.ds(0, tm)], ys_hbm.at[pl.ds(pl.multiple_of(j * tm, tm), tm)],
                                     osem.at[1])

    def start_body(j, carry):
        zero_block(j).start()
        return carry

    def wait_body(j, carry):
        zero_block(j).wait()
        return carry
    lax.fori_loop(meta_ref[2 * N_EXPERTS], n_blocks, start_body, 0)
    lax.fori_loop(meta_ref[2 * N_EXPERTS], n_blocks, wait_body, 0)


def _moe(chunk_row, chunk_size, chunk_exp, weight_tbl, meta, xs, w1, b1, w2, b2):
    whole = lambda i, *_: (0, 0, 0)
    grid_spec = pltpu.PrefetchScalarGridSpec(
        num_scalar_prefetch=5,
        grid=(1,),
        in_specs=[
            pl.BlockSpec(memory_space=pl.ANY),
            pl.BlockSpec(memory_space=pl.ANY),
            pl.BlockSpec((N_EXPERTS, 1, 2 * D_FF), whole),
            pl.BlockSpec(memory_space=pl.ANY),
            pl.BlockSpec((N_EXPERTS, 1, D_MODEL), whole),
        ],
        out_specs=pl.BlockSpec(memory_space=pl.ANY),
        scratch_shapes=[pltpu.VMEM((2, MOE_CH, PACK_W), PACKED), pltpu.VMEM((2, MOE_CH, PACK_W), PACKED),
                        pltpu.VMEM((2, D_MODEL, 2 * D_FF), F32), pltpu.VMEM((2, D_FF, D_MODEL), F32),
                        pltpu.VMEM((D_MODEL, 2 * D_FF), BF16), pltpu.VMEM((D_FF, D_MODEL), BF16),
                        pltpu.VMEM((MOE_CH, D_FF), BF16),
                        pltpu.SemaphoreType.DMA((2,)), pltpu.SemaphoreType.DMA((2,)), pltpu.SemaphoreType.DMA((2,))],
    )
    return pl.pallas_call(
        _moe_kernel,
        grid_spec=grid_spec,
        out_shape=jax.ShapeDtypeStruct(xs.shape, PACKED),
        compiler_params=pltpu.CompilerParams(dimension_semantics=("arbitrary",), vmem_limit_bytes=VMEM_LIMIT),
        name="moe",
    )(chunk_row, chunk_size, chunk_exp, weight_tbl, meta, xs, w1, b1, w2, b2)


def _combine_kernel(n_prompt_tiles, cnt_ref, off_ref, tot_ref, ys_hbm, routep_ref, routes_ref, x1p_ref, x1s_ref,
                    modp_ref, g2s_ref, fg_ref, op_ref, os_ref, buf, sem):
    i = pl.program_id(0)
    nb = pl.num_programs(0)
    slot = i % 2

    def fetch(tile, s):
        _for_strips(cnt_ref, off_ref, tile, buf.at[s], ys_hbm, sem.at[s], False, lambda cp: cp.start())

    @pl.when(i == 0)
    def _():
        buf[...] = jnp.zeros_like(buf)
        fetch(0, 0)

    @pl.when(i + 1 < nb)
    def _():
        fetch(i + 1, 1 - slot)

    _wait_tile_rows(tot_ref[i], buf.at[slot], ys_hbm, sem.at[slot], False)

    is_prompt = i < n_prompt_tiles
    route = jnp.where(is_prompt, routep_ref[...], routes_ref[...])
    c_pos = lax.broadcasted_iota(jnp.int16, (SEQ_TILE, SORT_ROWS), 1)
    g_bf = jnp.zeros((SEQ_TILE, SORT_ROWS), BF16)
    for k in range(TOP_K):
        pos_k = route[:, ROUTE_R + k:ROUTE_R + k + 1].astype(jnp.int32).astype(jnp.int16)
        g_bf = jnp.where(c_pos == pos_k, route[:, ROUTE_G + k:ROUTE_G + k + 1].astype(BF16), g_bf)
    ff = jnp.concatenate([jnp.dot(g_bf, yb, preferred_element_type=F32) for yb in _unpack_bf16_pairs(buf[slot])],
                         axis=1)
    g2 = jnp.where(i < n_prompt_tiles, modp_ref[5:6, :], g2s_ref[...])
    x = jnp.where(is_prompt, x1p_ref[...], x1s_ref[...]) + g2 * ff
    y = _rms(x, fg_ref[...])

    @pl.when(i < n_prompt_tiles)
    def _():
        op_ref[...] = y

    @pl.when(i >= n_prompt_tiles)
    def _():
        os_ref[...] = y


def _combine(cnt8, tile_off, tot8, ys, route_p, route_s, x1_p, x1_s, mod_p, g2_rows, final_g, tiles_per_batch):
    nt = tot8.shape[0]
    tt = SEQ_TILE
    npt = x1_p.shape[0] // tt
    p_tile = lambda i, *_: (jnp.minimum(i, npt - 1), 0)
    s_tile = lambda i, *_: (jnp.maximum(i - npt, 0), 0)
    grid_spec = pltpu.PrefetchScalarGridSpec(
        num_scalar_prefetch=3,
        grid=(nt,),
        in_specs=[
            pl.BlockSpec(memory_space=pl.ANY),
            pl.BlockSpec((tt, LANES), p_tile), pl.BlockSpec((tt, LANES), s_tile),
            pl.BlockSpec((tt, D_MODEL), p_tile), pl.BlockSpec((tt, D_MODEL), s_tile),
            pl.BlockSpec((None, 6, D_MODEL), lambda i, *_: (jnp.minimum(i, npt - 1) // tiles_per_batch, 0, 0)),
            pl.BlockSpec((tt, D_MODEL), s_tile),
            pl.BlockSpec((1, D_MODEL), lambda i, *_: (0, 0)),
        ],
        out_specs=(pl.BlockSpec((tt, D_MODEL), p_tile), pl.BlockSpec((tt, D_MODEL), s_tile)),
        scratch_shapes=[pltpu.VMEM((2, SORT_ROWS, PACK_W), PACKED), pltpu.SemaphoreType.DMA((2,))],
    )
    return pl.pallas_call(
        functools.partial(_combine_kernel, npt),
        grid_spec=grid_spec,
        out_shape=(jax.ShapeDtypeStruct((npt * tt, D_MODEL), F32),
                   jax.ShapeDtypeStruct(((nt - npt) * tt, D_MODEL), F32)),
        compiler_params=pltpu.CompilerParams(dimension_semantics=("arbitrary",), vmem_limit_bytes=VMEM_LIMIT),
        name="combine",
    )(cnt8, tile_off, tot8, ys, route_p, route_s, x1_p, x1_s, mod_p, g2_rows, final_g)


def _block_diag_halves(wg):
    per_half = LRU_BLOCKS // 2
    w4 = wg.reshape(2, per_half, LRU_BLOCK_W, LRU_BLOCK_W)
    on_diag = jnp.eye(per_half, dtype=bool)[None, :, None, :, None]
    dense = jnp.where(on_diag, w4[:, :, :, None, :], 0.0)
    return dense.reshape(2, per_half * LRU_BLOCK_W, per_half * LRU_BLOCK_W).astype(BF16)


def _rope_tables(pos):
    half = HEAD_DIM // 2
    inv = ROPE_THETA ** (-jnp.arange(half, dtype=F32) / half)
    ang = pos.astype(F32)[:, None] * inv[None, :]
    cos = jnp.cos(ang)
    sin = jnp.sin(ang)
    cos128 = jnp.concatenate([cos, cos, cos, cos], axis=1)
    sin128 = jnp.concatenate([-sin, sin, -sin, sin], axis=1)
    return cos128, sin128


def _resident_spec(arr):
    zeros = (0,) * arr.ndim
    return pl.BlockSpec(arr.shape, lambda i: zeros)


def kernel(x_prompt, x_sample, state_lru_h, state_conv, cache_win_k, cache_win_v, c_prompt, c_sample, w_ada, b_ada, norm1_g, w_in, b_in, conv_w, conv_b, lru_wa, lru_ba, lru_wx, lru_bx, lru_lambda, attn_sinks, w_out, b_out, norm2_g, w_router, b_router, w1, b1, w2, b2, final_g):
    bp, seq, _ = x_prompt.shape
    bd, tdec, _ = x_sample.shape
    assert tdec == SUBLANES and seq % (MIX_TILES * SEQ_TILE) == 0 and bd % SAMPLE_BT == 0
    assert SAMPLE_BT * tdec == SEQ_TILE
    n_prompt = bp * seq
    n_sample = bd * tdec
    n_tok = n_prompt + n_sample
    l = 0

    head_perm = [h for c in range(4) for h in (c, GROUP + c)]
    o2 = 2 * LRU_WIDTH

    def permute_heads(arr, axis, start):
        take = lambda a, b: lax.slice_in_dim(arr, a, b, axis=axis)
        heads = [take(start + h * HEAD_DIM, start + (h + 1) * HEAD_DIM) for h in head_perm]
        return jnp.concatenate([take(0, start)] + heads + [take(start + Q_WIDTH, arr.shape[axis])], axis=axis)

    sinks_perm = attn_sinks[l]

    wr = jnp.tile(w_router[l], (1, TOP_K))
    wr_hi = wr.astype(BF16)
    weights = dict(
        norm1_g=norm1_g[l][None, :], w_in=permute_heads(w_in[l], 1, o2).astype(BF16),
        b_in=permute_heads(b_in[l], 0, o2)[None, :],
        conv_w=conv_w[l], conv_b=conv_b[l][None, :],
        ga=_block_diag_halves(lru_wa[l]), gx=_block_diag_halves(lru_wx[l]),
        lru_ba=lru_ba[l][None, :], lru_bx=lru_bx[l][None, :], lam=lru_lambda[l][None, :],
        w_out=permute_heads(w_out[l], 0, LRU_WIDTH).astype(BF16), b_out=b_out[l][None, :], norm2_g=norm2_g[l][None, :],
        wr_hi=wr_hi, wr_lo=(wr - wr_hi.astype(F32)).astype(BF16),
        b_router=jnp.tile(b_router[l], TOP_K)[None, :],
    )
    wlist = [weights[n] for n in WEIGHT_NAMES]

    mod_p, mod_s = _ada(jnp.concatenate([c_prompt, c_sample], axis=0), bp, w_ada[l], b_ada[l][None, :])
    mod_p = mod_p.reshape(bp, 6, D_MODEL)

    cos_p, sin_p = _rope_tables(jnp.arange(seq, dtype=jnp.int32))
    cos_s, sin_s = _rope_tables(PAST_LEN + jnp.arange(tdec, dtype=jnp.int32))
    cos_s = jnp.tile(cos_s, (SAMPLE_BT, 1))
    sin_s = jnp.tile(sin_s, (SAMPLE_BT, 1))
    h0_rows = jnp.pad(state_lru_h[l][:, None, :], ((0, 0), (0, tdec - 1), (0, 0))).reshape(n_sample, LRU_WIDTH)
    cprev_rows = jnp.pad(state_conv[l], ((0, 0), (0, tdec - (CONV_W - 1)), (0, 0))).reshape(n_sample, LRU_WIDTH)
    ck = cache_win_k[l].reshape(bd, WINDOW, KV_WIDTH)
    cv = cache_win_v[l].reshape(bd, WINDOW, KV_WIDTH)
    npt = n_prompt // SEQ_TILE
    nst = n_sample // SEQ_TILE
    mix_rows = MIX_TILES * SEQ_TILE
    steps_per_seq = seq // mix_rows
    wspecs = [_resident_spec(a) for a in wlist]

    rows_p = lambda width: pl.BlockSpec((mix_rows, width), lambda i: (i, 0))
    tail_p = lambda rows, width: pl.BlockSpec((None, rows, width), lambda i: (i // steps_per_seq, 0, 0))
    (x1_p, h2_p, route_p, cnt_p, hlast_p, ulast_p, klast_p, vlast_p) = pl.pallas_call(
        functools.partial(_prompt_kernel, steps_per_seq),
        grid=(n_prompt // mix_rows,),
        in_specs=[rows_p(D_MODEL),
                  pl.BlockSpec((None, 6, D_MODEL), lambda i: (i // steps_per_seq, 0, 0)),
                  pl.BlockSpec((mix_rows, LANES), lambda i: (i % steps_per_seq, 0)),
                  pl.BlockSpec((mix_rows, LANES), lambda i: (i % steps_per_seq, 0)),
                  pl.BlockSpec(memory_space=pltpu.SMEM)] + wspecs,
        out_specs=(rows_p(D_MODEL), rows_p(D_MODEL), rows_p(LANES),
                   pl.BlockSpec((MIX_TILES, SUBLANES, LANES), lambda i: (i, 0, 0)),
                   tail_p(SUBLANES, LRU_WIDTH), tail_p(SUBLANES, LRU_WIDTH),
                   tail_p(WINDOW, KV_WIDTH), tail_p(WINDOW, KV_WIDTH)),
        out_shape=(
            jax.ShapeDtypeStruct((n_prompt, D_MODEL), F32),
            jax.ShapeDtypeStruct((n_prompt, D_MODEL), BF16),
            jax.ShapeDtypeStruct((n_prompt, LANES), F32),
            jax.ShapeDtypeStruct((npt, SUBLANES, LANES), F32),
            jax.ShapeDtypeStruct((bp, SUBLANES, LRU_WIDTH), F32),
            jax.ShapeDtypeStruct((bp, SUBLANES, LRU_WIDTH), F32),
            jax.ShapeDtypeStruct((bp, WINDOW, KV_WIDTH), F32),
            jax.ShapeDtypeStruct((bp, WINDOW, KV_WIDTH), F32),
        ),
        scratch_shapes=[pltpu.VMEM((SUBLANES, LRU_WIDTH), F32), pltpu.VMEM((SUBLANES, LRU_WIDTH), F32),
                        pltpu.VMEM((WINDOW, KV_WIDTH), F32), pltpu.VMEM((WINDOW, KV_WIDTH), F32)],
        compiler_params=pltpu.CompilerParams(dimension_semantics=("arbitrary",), vmem_limit_bytes=VMEM_LIMIT),
        name="prompt_mixer",
    )(x_prompt.reshape(n_prompt, D_MODEL), mod_p, cos_p, sin_p, sinks_perm, *wlist)

    rows_s = lambda width: pl.BlockSpec((SEQ_TILE, width), lambda i: (i, 0))
    cache_spec = pl.BlockSpec((SAMPLE_BT, WINDOW, KV_WIDTH), lambda i: (i, 0, 0))
    (x1_s, h2_s, route_s, cnt_s, g2_rows, hs_s, u_s, s_k, s_v) = pl.pallas_call(
        _sample_kernel,
        grid=(nst,),
        in_specs=[rows_s(D_MODEL),
                  pl.BlockSpec((6, SAMPLE_BT, D_MODEL), lambda i: (0, i, 0)),
                  pl.BlockSpec((SEQ_TILE, LANES), lambda i: (0, 0)),
                  pl.BlockSpec((SEQ_TILE, LANES), lambda i: (0, 0)),
                  pl.BlockSpec(memory_space=pltpu.SMEM),
                  rows_s(LRU_WIDTH), rows_s(LRU_WIDTH), cache_spec, cache_spec] + wspecs,
        out_specs=(rows_s(D_MODEL), rows_s(D_MODEL), rows_s(LANES),
                   pl.BlockSpec((None, SUBLANES, LANES), lambda i: (i, 0, 0)),
                   rows_s(D_MODEL), rows_s(LRU_WIDTH), rows_s(LRU_WIDTH), cache_spec, cache_spec),
        out_shape=(
            jax.ShapeDtypeStruct((n_sample, D_MODEL), F32), jax.ShapeDtypeStruct((n_sample, D_MODEL), BF16),
            jax.ShapeDtypeStruct((n_sample, LANES), F32), jax.ShapeDtypeStruct((nst, SUBLANES, LANES), F32),
            jax.ShapeDtypeStruct((n_sample, D_MODEL), F32),
            jax.ShapeDtypeStruct((n_sample, LRU_WIDTH), F32),
            jax.ShapeDtypeStruct((n_sample, LRU_WIDTH), F32),
            jax.ShapeDtypeStruct((bd, WINDOW, KV_WIDTH), F32),
            jax.ShapeDtypeStruct((bd, WINDOW, KV_WIDTH), F32),
        ),
        compiler_params=pltpu.CompilerParams(dimension_semantics=("arbitrary",), vmem_limit_bytes=VMEM_LIMIT),
        name="sample_mixer",
    )(x_sample.reshape(n_sample, D_MODEL), mod_s, cos_s, sin_s, sinks_perm, h0_rows, cprev_rows, ck, cv, *wlist)
    tile_cnt = jnp.concatenate([cnt_p, cnt_s], axis=0)

    n_tiles = npt + nst
    n_assign = n_tok * TOP_K
    max_rows = n_assign + n_tiles * N_EXPERTS * (SUBLANES - 1) + N_EXPERTS * (MOE_TM - 1)
    n_blocks = -(-max_rows // MOE_TM)
    cnt = tile_cnt[:, 0, :N_EXPERTS].astype(jnp.int32)
    cnt8 = (cnt + SUBLANES - 1) // SUBLANES * SUBLANES
    counts = jnp.sum(cnt8, axis=0)
    pcounts = (counts + MOE_TM - 1) // MOE_TM * MOE_TM
    pend = jnp.cumsum(pcounts)
    pstart = pend - pcounts
    tile_off = pstart[None, :] + jnp.cumsum(cnt8, axis=0) - cnt8
    tot8 = jnp.sum(cnt8, axis=1)
    meta = jnp.concatenate([pstart + counts, pend, pend[-1:] // MOE_TM]).astype(jnp.int32)
    cnt8_flat = cnt8.reshape(-1)
    off_flat = tile_off.reshape(-1).astype(jnp.int32)

    xs = _dispatch(cnt8_flat, off_flat, tot8, meta, h2_p, h2_s, route_p, route_s, n_blocks * MOE_TM)

    big, small = MOE_CHUNK_CODES[0], MOE_CHUNK_CODES[1:]
    n_tm = pcounts // MOE_TM
    n_big = n_tm // big
    present = [(n_tm // b) % 2 for b in small]
    n_ch = n_big + sum(present)
    ch_end = jnp.cumsum(n_ch)
    ch_start = ch_end - n_ch
    n_chunks = n_blocks // big + len(small) * N_EXPERTS
    ci = jnp.arange(n_chunks, dtype=jnp.int32)
    owner = jnp.logical_and(ch_start[None, :] <= ci[:, None], ci[:, None] < ch_end[None, :])
    pick = lambda v: jnp.sum(jnp.where(owner, v[None, :], 0), axis=1)
    live = ci < ch_end[-1]
    local = ci - pick(ch_start)
    chunk_exp = jnp.where(live, pick(jnp.arange(N_EXPERTS, dtype=jnp.int32)), N_EXPERTS - 1).astype(jnp.int32)
    big_c = pick(n_big)
    code = jnp.where(local < big_c, big, 0)
    blocks_before = jnp.where(local < big_c, local * big, 0)
    order = big_c
    blocks = big_c * big
    for b, has in zip(small, present):
        has_c = pick(has)
        is_b = jnp.logical_and(has_c == 1, local == order)
        code = jnp.where(is_b, b, code)
        blocks_before = jnp.where(is_b, blocks, blocks_before)
        order = order + has_c
        blocks = blocks + has_c * b
    chunk_row = jnp.where(live, pick(pstart) + blocks_before * MOE_TM, 0).astype(jnp.int32)
    chunk_size = jnp.where(live, code, 0).astype(jnp.int32)
    has_rows = n_ch > 0
    e_ar = jnp.arange(N_EXPERTS, dtype=jnp.int32)
    w_slot = (jnp.cumsum(has_rows.astype(jnp.int32)) - 1) % 2
    later = jnp.logical_and(has_rows[None, :], e_ar[None, :] > e_ar[:, None])
    nxt = jnp.min(jnp.where(later, e_ar[None, :], N_EXPERTS), axis=1)
    nxt = jnp.where(nxt == N_EXPERTS, -1, nxt)
    weight_tbl = jnp.concatenate([jnp.logical_and(live, local == 0).astype(jnp.int32),
                                  pick(w_slot), jnp.where(live, pick(nxt), -1)]).astype(jnp.int32)
    moe_meta = jnp.concatenate([meta, ch_end[-1:].astype(jnp.int32)])
    ys = _moe(chunk_row, chunk_size, chunk_exp, weight_tbl, moe_meta, xs, w1[l], b1[l][:, None, :], w2[l], b2[l][:, None, :])

    y_p, y_s = _combine(cnt8_flat, off_flat, tot8, ys, route_p, route_s, x1_p, x1_s, mod_p, g2_rows,
                        final_g[None, :], seq // SEQ_TILE)

    y_prompt = y_p.reshape(bp, seq, D_MODEL)
    y_sample = y_s.reshape(bd, tdec, D_MODEL)
    p_h = hlast_p[:, SUBLANES - 1, :][None]
    p_c = ulast_p[:, SUBLANES - (CONV_W - 1):, :][None]
    p_k = klast_p.reshape(1, bp, WINDOW, N_KV_HEADS, HEAD_DIM)
    p_v = vlast_p.reshape(1, bp, WINDOW, N_KV_HEADS, HEAD_DIM)
    s_h = hs_s.reshape(bd, tdec, LRU_WIDTH)[:, tdec - 1, :][None]
    s_c = u_s.reshape(bd, tdec, LRU_WIDTH)[:, tdec - (CONV_W - 1):, :][None]
    s_kk = s_k.reshape(1, bd, WINDOW, N_KV_HEADS, HEAD_DIM)
    s_vv = s_v.reshape(1, bd, WINDOW, N_KV_HEADS, HEAD_DIM)
    return (y_prompt, y_sample, p_h, p_c, p_k, p_v, s_h, s_c, s_kk, s_vv)
```
